```python
import jax, jax.numpy as jnp
from jax import lax
import numpy as np

D_MODEL = 1024
BATCH = 8
SEQ = 2048
DEPTH = 1
DEC_BATCH = 128
DEC_SEQ = 1
PAST_LEN = 16384
PAGE_SIZE = 128

N_META = 16
RET_HEADS = 4
RET_DK = 128
RET_DV = 256
RET_QK = RET_HEADS * RET_DK
RET_V = RET_HEADS * RET_DV
CHUNK = 128
LEAD_PAD = (-N_META) % CHUNK
ROPE_BASE = 10000.0
POOL_WINDOWS = (2, 4, 8, 16)
POOL_GROUPS = 4
POOL_GROUP_DIM = D_MODEL // 8
POOL_WIDTH = POOL_GROUPS * POOL_GROUP_DIM
POOL_BUF = max(POOL_WINDOWS) - 1
N_EXPERTS = 32
TOP_K = 4
D_FF = D_MODEL
SWIGLU_LIMIT = 7.0
SWIGLU_ALPHA = 1.702
MOE_BLOCK = 128
EPS = 1e-6
IN_WIDTHS = (RET_QK, RET_QK, RET_V, RET_V, POOL_WIDTH, D_MODEL, D_MODEL)
IN_TOTAL = sum(IN_WIDTHS)
IN_SPLITS = tuple(int(s) for s in np.cumsum(IN_WIDTHS)[:-1])

kernel_name = "gated_retention_pool_moe_step"


def rmsnorm(x, g):
    xf = x.astype(jnp.float32)
    y = xf * lax.rsqrt(jnp.mean(xf * xf, axis=-1, keepdims=True) + EPS)
    return (y * g.astype(jnp.float32)).astype(x.dtype)


def rotary(x, pos):
    d = x.shape[-1]
    inv = jnp.power(ROPE_BASE, -jnp.arange(0, d, 2, dtype=jnp.float32) / d)
    ang = pos[:, None] * inv[None, :]
    cos, sin = jnp.cos(ang), jnp.sin(ang)
    x1, x2 = x[..., : d // 2], x[..., d // 2:]
    return jnp.concatenate([x1 * cos - x2 * sin, x2 * cos + x1 * sin], axis=-1)


def retention(q, k, v, S0, chunk):
    B, H, T, dk = q.shape
    dv = v.shape[-1]
    nc = T // chunk
    log_g = jnp.log1p(-jnp.exp2(-5.0 - jnp.arange(H, dtype=jnp.float32)))
    i = jnp.arange(chunk, dtype=jnp.float32)
    diff = i[:, None] - i[None, :]
    mask = jnp.where(diff[None] >= 0, jnp.exp(jnp.maximum(diff, 0.0)[None] * log_g[:, None, None]), 0.0)
    q_dec = jnp.exp((i + 1.0)[None, :] * log_g[:, None])
    k_dec = jnp.exp((chunk - 1.0 - i)[None, :] * log_g[:, None])
    c_dec = jnp.exp(chunk * log_g)

    def to_chunks(a):
        return a.reshape(B, H, nc, chunk, a.shape[-1]).transpose(2, 0, 1, 3, 4)

    def step(S, xs):
        qc, kc, vc = xs
        scores = jnp.einsum('bhid,bhjd->bhij', qc, kc) * mask
        intra = jnp.einsum('bhij,bhjv->bhiv', scores, vc)
        cross = jnp.einsum('bhid,bhdv->bhiv', qc * q_dec[..., None], S)
        S_new = S * c_dec[:, None, None] + jnp.einsum('bhjd,bhjv->bhdv', kc * k_dec[..., None], vc)
        return S_new, intra + cross

    S_fin, o = lax.scan(step, S0, (to_chunks(q), to_chunks(k), to_chunks(v)))
    o = o.transpose(1, 2, 0, 3, 4).reshape(B, H, T, dv)
    return o, S_fin


def pool_mix(p, prev, pos0):
    B, T, W = p.shape
    ext = jnp.concatenate([prev.astype(p.dtype), p], axis=1)
    extf = ext.astype(jnp.float32)
    cs = jnp.concatenate([jnp.zeros((B, 1, W), jnp.float32), jnp.cumsum(extf, axis=1)], axis=1)
    pos = pos0 + jnp.arange(T, dtype=jnp.int32)
    outs = []
    for gi, w in enumerate(POOL_WINDOWS):
        sl = slice(gi * POOL_GROUP_DIM, (gi + 1) * POOL_GROUP_DIM)
        hi = cs[:, POOL_BUF + 1: POOL_BUF + 1 + T, sl]
        lo = cs[:, POOL_BUF + 1 - w: POOL_BUF + 1 - w + T, sl]
        cnt = jnp.minimum(pos + 1, w).astype(jnp.float32)
        outs.append((hi - lo) / cnt[None, :, None])
    pooled = jnp.concatenate(outs, axis=-1) - p.astype(jnp.float32)
    return pooled, ext[:, -POOL_BUF:]


def mixer(u, S0, pool_prev, pos0, lead_pad, chunk, w_in, ret_gn, pool_w, pool_scale,
          w_ret_branch, w_pool_branch, w_out):
    B, T, _ = u.shape
    proj = jnp.einsum('btd,de->bte', u, w_in)
    q, k, v, g, p, ga, gb = jnp.split(proj, IN_SPLITS, axis=-1)

    def heads(a, d):
        return a.reshape(B, T, RET_HEADS, d).transpose(0, 2, 1, 3).astype(jnp.float32)

    pos = (pos0 + jnp.arange(T, dtype=jnp.int32)).astype(jnp.float32)
    qh = rotary(heads(q, RET_DK), pos)
    kh = rotary(heads(k, RET_DK), pos) * (RET_DK ** -0.5)
    vh = heads(v, RET_DV)
    if lead_pad:
        padw = ((0, 0), (0, 0), (lead_pad, 0), (0, 0))
        qh, kh, vh = jnp.pad(qh, padw), jnp.pad(kh, padw), jnp.pad(vh, padw)
    o, S_new = retention(qh, kh, vh, S0.astype(jnp.float32), chunk)
    o = o[:, :, lead_pad:].transpose(0, 2, 1, 3)
    mu = jnp.mean(o, axis=-1, keepdims=True)
    var = jnp.mean(jnp.square(o - mu), axis=-1, keepdims=True)
    o = (o - mu) * lax.rsqrt(var + EPS) * ret_gn.astype(jnp.float32).reshape(RET_HEADS, RET_DV)
    o = o.reshape(B, T, RET_V).astype(u.dtype)
    ya = jnp.einsum('btc,cd->btd', jax.nn.silu(g) * o, w_ret_branch)

    pooled, pool_new = pool_mix(p, pool_prev, pos0)
    pooled = pooled.astype(u.dtype).reshape(B, T, POOL_GROUPS, POOL_GROUP_DIM)
    pooled = jnp.einsum('btgc,gce->btge', pooled, pool_w).reshape(B, T, POOL_WIDTH) * pool_scale
    yb = jnp.einsum('btc,cd->btd', pooled, w_pool_branch)

    merged = jax.nn.sigmoid(ga) * ya + jax.nn.sigmoid(gb) * yb
    return jnp.einsum('btd,de->bte', merged, w_out), S_new.astype(S0.dtype), pool_new


def moe(x, w_router, b_router, w_gate_up, b_gate_up, w_down, b_down):
    B, T, D = x.shape
    xt = x.reshape(B * T, D)
    NT = B * T
    NA = NT * TOP_K
    logits = jnp.einsum('td,de->te', xt, w_router).astype(jnp.float32) + b_router.astype(jnp.float32)
    top_val, top_idx = lax.top_k(logits, TOP_K)
    gates = jax.nn.softmax(top_val, axis=-1)
    flat_e = top_idx.reshape(-1)
    order = jnp.argsort(flat_e)
    sorted_e = flat_e[order]
    counts = jnp.bincount(flat_e, length=N_EXPERTS)
    padded = ((counts + MOE_BLOCK - 1) // MOE_BLOCK) * MOE_BLOCK
    pad_end = jnp.cumsum(padded)
    pad_start = pad_end - padded
    start = jnp.cumsum(counts) - counts
    dest = pad_start[sorted_e] + jnp.arange(NA, dtype=jnp.int32) - start[sorted_e]
    NB = (NA + MOE_BLOCK - 1) // MOE_BLOCK + N_EXPERTS
    row_tok = jnp.full((NB * MOE_BLOCK,), NT, jnp.int32).at[dest].set((order // TOP_K).astype(jnp.int32))
    block_e = jnp.minimum(jnp.searchsorted(pad_end, jnp.arange(NB, dtype=pad_end.dtype) * MOE_BLOCK, side='right'),
                          N_EXPERTS - 1)
    x_pad = jnp.concatenate([xt, jnp.zeros((1, D), xt.dtype)], axis=0)

    def expert_block(args):
        tok, e = args
        xb = x_pad[tok]
        h = jnp.einsum('rd,df->rf', xb, w_gate_up[e]) + b_gate_up[e]
        gate = jnp.minimum(h[:, :D_FF], SWIGLU_LIMIT)
        up = jnp.clip(h[:, D_FF:], -SWIGLU_LIMIT, SWIGLU_LIMIT)
        glu = gate * jax.nn.sigmoid(gate * SWIGLU_ALPHA)
        return jnp.einsum('rf,fd->rd', (up + 1.0) * glu, w_down[e]) + b_down[e]

    out_buf = lax.map(expert_block, (row_tok.reshape(NB, MOE_BLOCK), block_e)).reshape(NB * MOE_BLOCK, D)
    y_sorted = out_buf[dest]
    g_sorted = gates.reshape(-1)[order].astype(y_sorted.dtype)
    y = jnp.zeros((NT, D), y_sorted.dtype).at[order // TOP_K].add(y_sorted * g_sorted[:, None])
    return y.reshape(B, T, D).astype(x.dtype)


def setup_inputs(seed: int = 0) -> dict:
    key = jax.random.key(seed)
    ks = jax.random.split(key, 24)
    f32 = jnp.float32
    nrm = lambda k, shape, s: jax.random.normal(k, shape, f32) * s
    return {
        "x_prompt": nrm(ks[0], (BATCH, SEQ, D_MODEL), 1.0),
        "x_sample": nrm(ks[1], (DEC_BATCH, DEC_SEQ, D_MODEL), 1.0),
        "state_ret": nrm(ks[2], (DEPTH, DEC_BATCH, RET_HEADS, RET_DK, RET_DV), 0.5),
        "state_pool": nrm(ks[3], (DEPTH, DEC_BATCH, POOL_BUF, POOL_WIDTH), 1.0),
        "meta_tokens": nrm(ks[4], (N_META, D_MODEL), 1.0),
        "norm_mix": 1.0 + nrm(ks[5], (DEPTH, D_MODEL), 0.05),
        "w_in": nrm(ks[6], (DEPTH, D_MODEL, IN_TOTAL), D_MODEL ** -0.5),
        "ret_gn": 1.0 + nrm(ks[7], (DEPTH, RET_V), 0.05),
        "pool_w": nrm(ks[8], (DEPTH, POOL_GROUPS, POOL_GROUP_DIM, POOL_GROUP_DIM), POOL_GROUP_DIM ** -0.5),
        "pool_scale": 1.0 + nrm(ks[9], (DEPTH, POOL_WIDTH), 0.05),
        "w_ret_branch": nrm(ks[10], (DEPTH, RET_V, D_MODEL), RET_V ** -0.5),
        "w_pool_branch": nrm(ks[11], (DEPTH, POOL_WIDTH, D_MODEL), POOL_WIDTH ** -0.5),
        "w_out": nrm(ks[12], (DEPTH, D_MODEL, D_MODEL), D_MODEL ** -0.5),
        "norm_ffn": 1.0 + nrm(ks[13], (DEPTH, D_MODEL), 0.05),
        "w_router": nrm(ks[14], (DEPTH, D_MODEL, N_EXPERTS), D_MODEL ** -0.5),
        "b_router": nrm(ks[15], (DEPTH, N_EXPERTS), 0.01),
        "w_gate_up": nrm(ks[16], (DEPTH, N_EXPERTS, D_MODEL, 2 * D_FF), D_MODEL ** -0.5),
        "b_gate_up": nrm(ks[17], (DEPTH, N_EXPERTS, 2 * D_FF), 0.02),
        "w_down": nrm(ks[18], (DEPTH, N_EXPERTS, D_FF, D_MODEL), D_FF ** -0.5),
        "b_down": nrm(ks[19], (DEPTH, N_EXPERTS, D_MODEL), 0.02),
        "norm_final": 1.0 + nrm(ks[20], (D_MODEL,), 0.05),
    }


def reference(x_prompt, x_sample, state_ret, state_pool, meta_tokens, norm_mix, w_in, ret_gn, pool_w,
              pool_scale, w_ret_branch, w_pool_branch, w_out, norm_ffn, w_router, b_router, w_gate_up,
              b_gate_up, w_down, b_down, norm_final):
    B = x_prompt.shape[0]
    meta = jnp.broadcast_to(meta_tokens.astype(x_prompt.dtype)[None], (B, N_META, D_MODEL))
    hp = jnp.concatenate([meta, x_prompt], axis=1)
    hs = x_sample
    S0_p = jnp.zeros((B, RET_HEADS, RET_DK, RET_DV), state_ret.dtype)
    P0_p = jnp.zeros((B, POOL_BUF, POOL_WIDTH), x_prompt.dtype)
    rs_p, ps_p, rs_s, ps_s = [], [], [], []
    for l in range(DEPTH):
        mix_args = (w_in[l], ret_gn[l], pool_w[l], pool_scale[l], w_ret_branch[l], w_pool_branch[l], w_out[l])
        moe_args = (w_router[l], b_router[l], w_gate_up[l], b_gate_up[l], w_down[l], b_down[l])
        mp, Sp, Pp = mixer(rmsnorm(hp, norm_mix[l]), S0_p, P0_p, 0, LEAD_PAD, CHUNK, *mix_args)
        hp = hp + mp
        hp = hp + moe(rmsnorm(hp, norm_ffn[l]), *moe_args)
        ms, Ss, Ps = mixer(rmsnorm(hs, norm_mix[l]), state_ret[l], state_pool[l], PAST_LEN, 0, hs.shape[1],
                           *mix_args)
        hs = hs + ms
        hs = hs + moe(rmsnorm(hs, norm_ffn[l]), *moe_args)
        rs_p.append(Sp); ps_p.append(Pp); rs_s.append(Ss); ps_s.append(Ps)
    y_prompt = rmsnorm(hp[:, N_META:], norm_final)
    y_sample = rmsnorm(hs, norm_final)
    ret_state_prompt = jnp.stack(rs_p)
    pool_state_prompt = jnp.stack(ps_p)
    ret_state_sample = jnp.stack(rs_s)
    pool_state_sample = jnp.stack(ps_s)
    return (y_prompt, y_sample, ret_state_prompt, pool_state_prompt, ret_state_sample, pool_state_sample)
```

```python
import functools

import jax
import jax.numpy as jnp
import numpy as np
from jax import lax
from jax.experimental import pallas as pl
from jax.experimental.pallas import tpu as pltpu

F32 = jnp.float32
BF16 = jnp.bfloat16

D_MODEL = 1024
N_META = 16
RET_HEADS = 4
RET_DK = 128
RET_DV = 256
RET_QK = RET_HEADS * RET_DK
RET_V = RET_HEADS * RET_DV
CHUNK = 128
ROPE_BASE = 10000.0
POOL_WINDOWS = (2, 4, 8, 16)
POOL_GROUPS = 4
POOL_GROUP_DIM = 128
POOL_WIDTH = POOL_GROUPS * POOL_GROUP_DIM
POOL_BUF = max(POOL_WINDOWS) - 1
N_EXPERTS = 32
TOP_K = 4
D_FF = D_MODEL
SWIGLU_LIMIT = 7.0
SWIGLU_ALPHA = 1.702
EPS = 1e-6
IN_WIDTHS = (RET_QK, RET_QK, RET_V, RET_V, POOL_WIDTH, D_MODEL, D_MODEL)
IN_TOTAL = sum(IN_WIDTHS)
IN_OFFS = tuple(int(s) for s in np.cumsum((0,) + IN_WIDTHS))

LANES = 128
INPROJ_TILE = 512
MIXER_TILE = 256
MOE_BLOCK = 256
ROW_TILE = 128
SAMPLE_GROUP = 8
VMEM_LIMIT = 56 * 1024 * 1024

assert N_META + 1 >= max(POOL_WINDOWS)
assert POOL_WINDOWS == (2, 4, 8, 16)


def _dot(a, b):
    return jnp.dot(a, b, preferred_element_type=F32)


def _rmsnorm(x, w):
    return x * lax.rsqrt(jnp.mean(x * x, axis=-1, keepdims=True) + EPS) * w


def _inproj_kernel(x_ref, nw_ref, w_ref, cos_ref, sin_ref,
                   q_ref, k_ref, v_ref, g_ref, p_ref, ga_ref, gb_ref):
    xn = _rmsnorm(x_ref[...], nw_ref[...]).astype(BF16)
    cos = cos_ref[...]
    sin = sin_ref[...]

    def seg(i):
        return _dot(xn, w_ref[:, IN_OFFS[i]:IN_OFFS[i + 1]])

    def rot(a):
        return a * cos + pltpu.roll(a, RET_DK // 2, 1) * sin

    q = seg(0)
    k = seg(1)
    for h in range(RET_HEADS):
        sl = slice(h * RET_DK, (h + 1) * RET_DK)
        q_ref[:, sl] = rot(q[:, sl]).astype(q_ref.dtype)
        k_ref[:, sl] = (rot(k[:, sl]) * (RET_DK ** -0.5)).astype(k_ref.dtype)
    v_ref[...] = seg(2).astype(v_ref.dtype)
    g_ref[...] = seg(3).astype(g_ref.dtype)
    p_ref[...] = seg(4)
    ga_ref[...] = seg(5).astype(ga_ref.dtype)
    gb_ref[...] = seg(6).astype(gb_ref.dtype)


def _inproj(x2d, nw, w_in_bf, cosf, sinf, tile, n_outer, act_dtype):
    rows = x2d.shape[0]
    n_inner = rows // (tile * n_outer)
    row_map = lambda b, j: (b * n_inner + j, 0)
    tab_map = lambda b, j: (j, 0)
    const = lambda b, j: (0, 0)
    widths = IN_WIDTHS
    dts = (act_dtype, act_dtype, act_dtype, act_dtype, F32, act_dtype, act_dtype)
    return pl.pallas_call(
        _inproj_kernel,
        grid=(n_outer, n_inner),
        in_specs=[
            pl.BlockSpec((tile, D_MODEL), row_map),
            pl.BlockSpec((1, D_MODEL), const),
            pl.BlockSpec((D_MODEL, IN_TOTAL), const, pipeline_mode=pl.Buffered(1)),
            pl.BlockSpec((tile, RET_DK), tab_map),
            pl.BlockSpec((tile, RET_DK), tab_map),
        ],
        out_specs=[pl.BlockSpec((tile, w), row_map) for w in widths],
        out_shape=[jax.ShapeDtypeStruct((rows, w), dt) for w, dt in zip(widths, dts)],
        compiler_params=pltpu.CompilerParams(
            dimension_semantics=("arbitrary", "arbitrary"), vmem_limit_bytes=VMEM_LIMIT),
        name="inproj",
    )(x2d, nw, w_in_bf, cosf, sinf)


def _group_norm(o, gn_row):
    mu = jnp.mean(o, axis=-1, keepdims=True)
    var = jnp.mean(jnp.square(o - mu), axis=-1, keepdims=True)
    return (o - mu) * lax.rsqrt(var + EPS) * gn_row


def _pool_branch(groups, poolw_ref, pscale_ref, wpool_ref):
    pm = [_dot(g.astype(BF16), poolw_ref[i]) for i, g in enumerate(groups)]
    pm = jnp.concatenate(pm, axis=1) * pscale_ref[...]
    return _dot(pm.astype(BF16), wpool_ref[...])


def _merge_tail(o_norm, g, ga, gb, yb, x, wret_ref, wout_ref):
    gf = g.astype(F32)
    ya = _dot((gf * jax.nn.sigmoid(gf) * o_norm).astype(BF16), wret_ref[...])
    merged = jax.nn.sigmoid(ga.astype(F32)) * ya + jax.nn.sigmoid(gb.astype(F32)) * yb
    return x + _dot(merged.astype(BF16), wout_ref[...])


def _route(h1, nffn_ref, wrt_ref, br_ref, run_scr,
           xn2_ref, idx_ref, gate_ref, rank_ref, gatet_ref):
    tm = h1.shape[0]
    xn2 = _rmsnorm(h1, nffn_ref[...])
    xn2_ref[...] = xn2
    logits = lax.dot_general(wrt_ref[...], xn2.astype(BF16), (((1,), (1,)), ((), ())),
                             preferred_element_type=F32) + br_ref[...]
    e_iota = lax.broadcasted_iota(jnp.int32, (N_EXPERTS, tm), 0)
    work = logits
    vals, sels = [], []
    chosen = jnp.zeros((N_EXPERTS, tm), F32)
    for _ in range(TOP_K):
        m = jnp.max(work, axis=0, keepdims=True)
        sel = jnp.min(jnp.where(work == m, e_iota, N_EXPERTS), axis=0, keepdims=True)
        hit = e_iota == sel
        vals.append(m)
        sels.append(sel)
        chosen = jnp.where(hit, 1.0, chosen)
        work = jnp.where(hit, -jnp.inf, work)
    exps = [jnp.exp(v - vals[0]) for v in vals]
    denom = exps[0] + exps[1] + exps[2] + exps[3]
    gates = [e / denom for e in exps]
    r_i = lax.broadcasted_iota(jnp.int32, (tm, tm), 0)
    c_i = lax.broadcasted_iota(jnp.int32, (tm, tm), 1)
    before = jnp.where(r_i < c_i, 1.0, 0.0).astype(BF16)
    base = run_scr[...] + _dot(chosen.astype(BF16), before)
    for kk in range(TOP_K):
        rk = jnp.sum(jnp.where(e_iota == sels[kk], base, 0.0), axis=0, keepdims=True)
        rank_ref[kk:kk + 1, :] = rk.astype(jnp.int32)
        idx_ref[kk:kk + 1, :] = sels[kk]
        gate_ref[kk:kk + 1, :] = gates[kk]
    run_scr[...] = run_scr[...] + jnp.sum(chosen, axis=1, keepdims=True)
    row = lax.broadcasted_iota(jnp.int32, (LANES, tm), 0)
    gpad = jnp.zeros((LANES, tm), F32)
    for kk in range(TOP_K):
        gpad = jnp.where(row == kk, gates[kk], gpad)
    for c in range(tm // LANES):
        gatet_ref[c * LANES:(c + 1) * LANES, :] = gpad[:, c * LANES:(c + 1) * LANES].T


def _mixer_kernel(q_ref, k_ref, v_ref, g_ref, p_ref, ga_ref, gb_ref, x_ref,
                  kmeta_ref, vmeta_ref, pmeta_ref, mask_ref, qdec_ref, kdec_ref, cdec_ref, gn_ref,
                  poolw_ref, pscale_ref, wret_ref, wpool_ref, wout_ref, nffn_ref, wrt_ref, br_ref,
                  cnt0_ref,
                  h1_ref, xn2_ref, idx_ref, gate_ref, rank_ref, gatet_ref, cnt_ref, sfin_ref, pfin_ref,
                  s_scr, ext_scr, o_scr, run_scr):
    b = pl.program_id(0)
    j = pl.program_id(1)
    nj = pl.num_programs(1)
    tm = q_ref.shape[0]

    def state_update(s_old, kc, vc, h):
        kd = (kc.astype(F32) * kdec_ref[h]).astype(BF16)
        upd = lax.dot_general(kd, vc, (((0,), (0,)), ((), ())), preferred_element_type=F32)
        return s_old * cdec_ref[h] + upd

    @pl.when(jnp.logical_and(b == 0, j == 0))
    def _():
        run_scr[...] = cnt0_ref[:, 0:1]

    @pl.when(j == 0)
    def _():
        for h in range(RET_HEADS):
            kc = kmeta_ref[:, h * RET_DK:(h + 1) * RET_DK]
            vc = vmeta_ref[:, h * RET_DV:(h + 1) * RET_DV]
            s_scr[h] = state_update(jnp.zeros((RET_DK, RET_DV), F32), kc, vc, h)
        ext_scr[0:N_META, :] = pmeta_ref[...]

    for c in range(tm // CHUNK):
        rows = slice(c * CHUNK, (c + 1) * CHUNK)
        for h in range(RET_HEADS):
            qc = q_ref[rows, h * RET_DK:(h + 1) * RET_DK]
            kc = k_ref[rows, h * RET_DK:(h + 1) * RET_DK]
            vc = v_ref[rows, h * RET_DV:(h + 1) * RET_DV]
            s_old = s_scr[h]
            scores = lax.dot_general(qc, kc, (((1,), (1,)), ((), ())),
                                     preferred_element_type=F32) * mask_ref[h]
            qd = (qc.astype(F32) * qdec_ref[h]).astype(BF16)
            lhs = jnp.concatenate([scores.astype(BF16), qd], axis=1)
            rhs = jnp.concatenate([vc, s_old.astype(BF16)], axis=0)
            o = _dot(lhs, rhs)
            s_scr[h] = state_update(s_old, kc, vc, h)
            o_scr[rows, h * RET_DV:(h + 1) * RET_DV] = _group_norm(
                o, gn_ref[:, h * RET_DV:(h + 1) * RET_DV])

    p = p_ref[...]
    ext_scr[N_META:N_META + tm, :] = p
    a = ext_scr[...]
    g1 = POOL_GROUP_DIM
    s2 = a + pltpu.roll(a, 1, 0)
    s4 = s2[:, g1:] + pltpu.roll(s2[:, g1:], 2, 0)
    s8 = s4[:, g1:] + pltpu.roll(s4[:, g1:], 4, 0)
    s16 = s8[:, g1:] + pltpu.roll(s8[:, g1:], 8, 0)
    sums = (s2[N_META:, :g1], s4[N_META:, :g1], s8[N_META:, :g1], s16[N_META:, :])
    groups = [sums[i] * (1.0 / POOL_WINDOWS[i]) - p[:, i * g1:(i + 1) * g1] for i in range(POOL_GROUPS)]
    ext_scr[0:N_META, :] = ext_scr[tm:tm + N_META, :]

    yb = _pool_branch(groups, poolw_ref, pscale_ref, wpool_ref)
    h1 = _merge_tail(o_scr[...], g_ref[...], ga_ref[...], gb_ref[...], yb, x_ref[...], wret_ref, wout_ref)
    h1_ref[...] = h1
    _route(h1, nffn_ref, wrt_ref, br_ref, run_scr, xn2_ref, idx_ref, gate_ref, rank_ref, gatet_ref)
    cnt_ref[...] = jnp.broadcast_to(run_scr[...], cnt_ref.shape)

    @pl.when(j == nj - 1)
    def _():
        for h in range(RET_HEADS):
            sfin_ref[0, h] = s_scr[h]
        pfin_ref[0] = ext_scr[0:N_META, :]


def _mixer(proj, x2d, kmeta, vmeta, pmeta, dec, wts, cnt0, batch, seq):
    q, k, v, g, p, ga, gb = proj
    tm = MIXER_TILE
    nj = seq // tm
    rows = batch * seq
    row_map = lambda b, j: (b * nj + j, 0)
    lane_map = lambda b, j: (0, b * nj + j)
    c2 = lambda b, j: (0, 0)
    c3 = lambda b, j: (0, 0, 0)

    def whole(a):
        return pl.BlockSpec(a.shape, c2 if a.ndim == 2 else c3)

    mask, qdec, kdec, cdec = dec
    gn, poolw, pscale, wret, wpool, wout, nffn, wrt, br = wts
    in_arrays = [q, k, v, g, p, ga, gb, x2d, kmeta, vmeta, pmeta, mask, qdec, kdec, cdec, gn,
                 poolw, pscale, wret, wpool, wout, nffn, wrt, br, cnt0]
    in_specs = [pl.BlockSpec((tm, a.shape[1]), row_map) for a in in_arrays[:8]]
    in_specs += [whole(a) for a in in_arrays[8:]]
    out_shape = [
        jax.ShapeDtypeStruct((rows, D_MODEL), F32),
        jax.ShapeDtypeStruct((rows, D_MODEL), F32),
        jax.ShapeDtypeStruct((TOP_K, rows), jnp.int32),
        jax.ShapeDtypeStruct((TOP_K, rows), F32),
        jax.ShapeDtypeStruct((TOP_K, rows), jnp.int32),
        jax.ShapeDtypeStruct((rows, LANES), F32),
        jax.ShapeDtypeStruct((N_EXPERTS, LANES), F32),
        jax.ShapeDtypeStruct((batch, RET_HEADS, RET_DK, RET_DV), F32),
        jax.ShapeDtypeStruct((batch, N_META, POOL_WIDTH), F32),
    ]
    out_specs = [
        pl.BlockSpec((tm, D_MODEL), row_map),
        pl.BlockSpec((tm, D_MODEL), row_map),
        pl.BlockSpec((TOP_K, tm), lane_map),
        pl.BlockSpec((TOP_K, tm), lane_map),
        pl.BlockSpec((TOP_K, tm), lane_map),
        pl.BlockSpec((tm, LANES), row_map),
        pl.BlockSpec((N_EXPERTS, LANES), c2),
        pl.BlockSpec((1, RET_HEADS, RET_DK, RET_DV), lambda b, j: (b, 0, 0, 0)),
        pl.BlockSpec((1, N_META, POOL_WIDTH), lambda b, j: (b, 0, 0)),
    ]
    return pl.pallas_call(
        _mixer_kernel,
        grid=(batch, nj),
        in_specs=in_specs,
        out_specs=out_specs,
        out_shape=out_shape,
        scratch_shapes=[
            pltpu.VMEM((RET_HEADS, RET_DK, RET_DV), F32),
            pltpu.VMEM((N_META + tm, POOL_WIDTH), F32),
            pltpu.VMEM((tm, RET_V), F32),
            pltpu.VMEM((N_EXPERTS, 1), F32),
        ],
        compiler_params=pltpu.CompilerParams(
            dimension_semantics=("arbitrary", "arbitrary"), vmem_limit_bytes=VMEM_LIMIT),
        name="mixer",
    )(*in_arrays)


def _sample_kernel(sdec_ref, qt_ref, kt_ref, q_ref, k_ref, v_ref, g_ref, p_ref, ga_ref, gb_ref, x_ref,
                   st_ref, pool_ref, gn_ref,
                   poolw_ref, pscale_ref, wret_ref, wpool_ref, wout_ref, nffn_ref, wrt_ref, br_ref,
                   stout_ref, poolout_ref, h1_ref, xn2_ref, idx_ref, gate_ref, rank_ref, gatet_ref,
                   cnt_ref,
                   o_scr, run_scr):
    i = pl.program_id(0)
    n = pl.num_programs(0)
    grp = st_ref.shape[0]
    row0 = pl.multiple_of(i * grp, grp)

    @pl.when(i == 0)
    def _():
        run_scr[...] = jnp.zeros_like(run_scr)

    q8 = q_ref[pl.ds(row0, grp), :]
    k8 = k_ref[pl.ds(row0, grp), :]
    v8 = v_ref[pl.ds(row0, grp), :]
    for h in range(RET_HEADS):
        ksl = slice(h * RET_DK, (h + 1) * RET_DK)
        vsl = slice(h * RET_DV, (h + 1) * RET_DV)
        score = jnp.sum(q8[:, ksl] * k8[:, ksl], axis=1, keepdims=True) * sdec_ref[h, 0]
        intra = score * v8[:, vsl]
        for bb in range(grp):
            s_old = st_ref[bb, h]
            qcol = qt_ref[0, ksl, bb:bb + 1] * sdec_ref[h, 1]
            kcol = kt_ref[0, ksl, bb:bb + 1] * sdec_ref[h, 2]
            cross = jnp.sum(s_old * qcol, axis=0, keepdims=True)
            o_scr[pl.ds(row0 + bb, 1), vsl] = intra[bb:bb + 1, :] + cross
            stout_ref[bb, h] = s_old * sdec_ref[h, 3] + kcol * v8[bb:bb + 1, vsl]

    @pl.when(i == n - 1)
    def _():
        o = o_scr[...]
        o_norm = jnp.concatenate(
            [_group_norm(o[:, h * RET_DV:(h + 1) * RET_DV], gn_ref[:, h * RET_DV:(h + 1) * RET_DV])
             for h in range(RET_HEADS)], axis=1)
        p = p_ref[...]
        w = POOL_WIDTH
        g1 = POOL_GROUP_DIM

        def prev(r, lo):
            return pool_ref[:, r * w + lo:(r + 1) * w]

        s2 = p + prev(14, 0)
        s4 = s2[:, g1:] + prev(13, g1) + prev(12, g1)
        s8 = s4[:, g1:] + prev(11, 2 * g1) + prev(10, 2 * g1) + prev(9, 2 * g1) + prev(8, 2 * g1)
        s16 = s8[:, g1:]
        for r in range(7, -1, -1):
            s16 = s16 + prev(r, 3 * g1)
        sums = (s2[:, :g1], s4[:, :g1], s8[:, :g1], s16)
        groups = [sums[t] * (1.0 / POOL_WINDOWS[t]) - p[:, t * g1:(t + 1) * g1] for t in range(POOL_GROUPS)]
        poolout_ref[:, 0:(POOL_BUF - 1) * w] = pool_ref[:, w:POOL_BUF * w]
        poolout_ref[:, (POOL_BUF - 1) * w:] = p
        yb = _pool_branch(groups, poolw_ref, pscale_ref, wpool_ref)
        h1 = _merge_tail(o_norm, g_ref[...], ga_ref[...], gb_ref[...], yb, x_ref[...], wret_ref, wout_ref)
        h1_ref[...] = h1
        _route(h1, nffn_ref, wrt_ref, br_ref, run_scr, xn2_ref, idx_ref, gate_ref, rank_ref, gatet_ref)
        cnt_ref[...] = jnp.broadcast_to(run_scr[...], cnt_ref.shape)


def _sample_mixer(sdec, qt, kt, proj, x2d, state, pool2d, wts):
    q, k, v, g, p, ga, gb = proj
    nb = x2d.shape[0]
    grp = SAMPLE_GROUP
    c2 = lambda i: (0, 0)

    def whole(a):
        return pl.BlockSpec(a.shape, c2)

    gn, poolw, pscale, wret, wpool, wout, nffn, wrt, br = wts
    in_arrays = [sdec, qt, kt, q, k, v, g, p, ga, gb, x2d, state, pool2d, gn,
                 poolw, pscale, wret, wpool, wout, nffn, wrt, br]
    in_specs = [pl.BlockSpec(memory_space=pltpu.SMEM),
                pl.BlockSpec((1, RET_QK, grp), lambda i: (i, 0, 0)),
                pl.BlockSpec((1, RET_QK, grp), lambda i: (i, 0, 0))]
    in_specs += [whole(a) for a in (q, k, v, g, p, ga, gb, x2d)]
    in_specs += [pl.BlockSpec((grp, RET_HEADS, RET_DK, RET_DV), lambda i: (i, 0, 0, 0)), whole(pool2d), whole(gn),
                 pl.BlockSpec(poolw.shape, lambda i: (0, 0, 0))]
    in_specs += [whole(a) for a in (pscale, wret, wpool, wout, nffn, wrt, br)]
    out_shape = [
        jax.ShapeDtypeStruct(state.shape, F32),
        jax.ShapeDtypeStruct(pool2d.shape, F32),
        jax.ShapeDtypeStruct((nb, D_MODEL), F32),
        jax.ShapeDtypeStruct((nb, D_MODEL), F32),
        jax.ShapeDtypeStruct((TOP_K, nb), jnp.int32),
        jax.ShapeDtypeStruct((TOP_K, nb), F32),
        jax.ShapeDtypeStruct((TOP_K, nb), jnp.int32),
        jax.ShapeDtypeStruct((nb, LANES), F32),
        jax.ShapeDtypeStruct((N_EXPERTS, LANES), F32),
    ]
    out_specs = [pl.BlockSpec((grp, RET_HEADS, RET_DK, RET_DV), lambda i: (i, 0, 0, 0))]
    out_specs += [pl.BlockSpec(s.shape, c2) for s in out_shape[1:]]
    return pl.pallas_call(
        _sample_kernel,
        grid=(nb // grp,),
        in_specs=in_specs,
        out_specs=out_specs,
        out_shape=out_shape,
        scratch_shapes=[pltpu.VMEM((nb, RET_V), F32), pltpu.VMEM((N_EXPERTS, 1), F32)],
        compiler_params=pltpu.CompilerParams(
            dimension_semantics=("arbitrary",), vmem_limit_bytes=VMEM_LIMIT),
        name="sample_mixer",
    )(*in_arrays)


def _row_copy(src_ref, src_row, dst_ref, dst_row, sem):
    return pltpu.make_async_copy(src_ref.at[pl.ds(src_row, 1), :], dst_ref.at[pl.ds(dst_row, 1), :], sem)


def _dispatch_kernel(dest_ref, xp_ref, xs_ref, init_ref, out_ref, sem):
    del init_ref
    i = pl.program_id(0)
    n_prompt = pl.num_programs(0) - 1

    def scatter(src_ref):
        def issue(r, carry):
            for kk in range(TOP_K):
                _row_copy(src_ref, r, out_ref, dest_ref[kk, r], sem).start()
            return carry

        def drain(r, carry):
            for kk in range(TOP_K):
                _row_copy(src_ref, 0, out_ref, 0, sem).wait()
            return carry

        lax.fori_loop(0, ROW_TILE, issue, 0)
        lax.fori_loop(0, ROW_TILE, drain, 0)

    @pl.when(i < n_prompt)
    def _():
        scatter(xp_ref)

    @pl.when(i == n_prompt)
    def _():
        scatter(xs_ref)


def _dispatch(dest, xn2_p, xn2_s, n_sorted):
    n_prompt = xn2_p.shape[0] // ROW_TILE
    init = jnp.zeros((n_sorted, D_MODEL), F32)
    return pl.pallas_call(
        _dispatch_kernel,
        grid=(n_prompt + 1,),
        in_specs=[
            pl.BlockSpec((TOP_K, ROW_TILE), lambda i: (0, i), memory_space=pltpu.SMEM),
            pl.BlockSpec((ROW_TILE, D_MODEL), lambda i: (jnp.minimum(i, n_prompt - 1), 0)),
            pl.BlockSpec((ROW_TILE, D_MODEL), lambda i: (0, 0)),
            pl.BlockSpec(memory_space=pl.ANY),
        ],
        out_specs=pl.BlockSpec(memory_space=pl.ANY),
        out_shape=jax.ShapeDtypeStruct((n_sorted, D_MODEL), F32),
        scratch_shapes=[pltpu.SemaphoreType.DMA(())],
        input_output_aliases={3: 0},
        compiler_params=pltpu.CompilerParams(dimension_semantics=("arbitrary",)),
        name="moe_dispatch",
    )(dest, xn2_p, xn2_s, init)


def _expert_kernel(be_ref, nblk_ref, x_ref, wgu_ref, bgu_ref, wd_ref, bd_ref, y_ref):
    del be_ref
    live = pl.program_id(0) < nblk_ref[0]

    @pl.when(jnp.logical_not(live))
    def _():
        y_ref[...] = jnp.zeros_like(y_ref)

    @pl.when(live)
    def _():
        h = _dot(x_ref[...].astype(BF16), wgu_ref[0]) + bgu_ref[0]
        gate = jnp.minimum(h[:, :D_FF], SWIGLU_LIMIT)
        up = jnp.clip(h[:, D_FF:], -SWIGLU_LIMIT, SWIGLU_LIMIT)
        glu = gate * jax.nn.sigmoid(gate * SWIGLU_ALPHA)
        y_ref[...] = _dot(((up + 1.0) * glu).astype(BF16), wd_ref[0]) + bd_ref[0]


def _experts(block_e, nblk, x_sorted, wgu, bgu, wd, bd):
    n_blocks = x_sorted.shape[0] // MOE_BLOCK
    grid_spec = pltpu.PrefetchScalarGridSpec(
        num_scalar_prefetch=2,
        grid=(n_blocks,),
        in_specs=[
            pl.BlockSpec((MOE_BLOCK, D_MODEL), lambda i, be, nb: (i, 0)),
            pl.BlockSpec((1, D_MODEL, 2 * D_FF), lambda i, be, nb: (be[i], 0, 0)),
            pl.BlockSpec((1, 1, 2 * D_FF), lambda i, be, nb: (be[i], 0, 0)),
            pl.BlockSpec((1, D_FF, D_MODEL), lambda i, be, nb: (be[i], 0, 0)),
            pl.BlockSpec((1, 1, D_MODEL), lambda i, be, nb: (be[i], 0, 0)),
        ],
        out_specs=pl.BlockSpec((MOE_BLOCK, D_MODEL), lambda i, be, nb: (i, 0)),
    )
    return pl.pallas_call(
        _expert_kernel,
        grid_spec=grid_spec,
        out_shape=jax.ShapeDtypeStruct(x_sorted.shape, F32),
        compiler_params=pltpu.CompilerParams(
            dimension_semantics=("arbitrary",), vmem_limit_bytes=VMEM_LIMIT),
        name="moe_experts",
    )(block_e, nblk, x_sorted, wgu, bgu, wd, bd)


def _combine_kernel(dest_ref, ys_ref, hp_ref, hs_ref, gp_ref, gs_ref, nf_ref, yp_ref, ysmp_ref, rows_scr, sem):
    i = pl.program_id(0)
    n_prompt = pl.num_programs(0) - 1

    def issue(r, carry):
        for kk in range(TOP_K):
            _row_copy(ys_ref, dest_ref[kk, r], rows_scr.at[kk], r, sem).start()
        return carry

    def drain(r, carry):
        for kk in range(TOP_K):
            _row_copy(ys_ref, 0, rows_scr.at[kk], 0, sem).wait()
        return carry

    lax.fori_loop(0, ROW_TILE, issue, 0)
    lax.fori_loop(0, ROW_TILE, drain, 0)

    def finish(h_ref, gt_ref, out_ref):
        y = h_ref[...]
        gt = gt_ref[...]
        for kk in range(TOP_K):
            y = y + rows_scr[kk] * gt[:, kk:kk + 1]
        out_ref[...] = _rmsnorm(y, nf_ref[...])

    @pl.when(i < n_prompt)
    def _():
        finish(hp_ref, gp_ref, yp_ref)

    @pl.when(i == n_prompt)
    def _():
        finish(hs_ref, gs_ref, ysmp_ref)


def _combine(dest, y_sorted, h1_p, h1_s, gt_p, gt_s, nf):
    n_prompt = h1_p.shape[0] // ROW_TILE
    pmap = lambda i: (jnp.minimum(i, n_prompt - 1), 0)
    smap = lambda i: (0, 0)
    return pl.pallas_call(
        _combine_kernel,
        grid=(n_prompt + 1,),
        in_specs=[
            pl.BlockSpec((TOP_K, ROW_TILE), lambda i: (0, i), memory_space=pltpu.SMEM),
            pl.BlockSpec(memory_space=pl.ANY),
            pl.BlockSpec((ROW_TILE, D_MODEL), pmap),
            pl.BlockSpec((ROW_TILE, D_MODEL), smap),
            pl.BlockSpec((ROW_TILE, LANES), pmap),
            pl.BlockSpec((ROW_TILE, LANES), smap),
            pl.BlockSpec((1, D_MODEL), smap),
        ],
        out_specs=[pl.BlockSpec((ROW_TILE, D_MODEL), pmap), pl.BlockSpec((ROW_TILE, D_MODEL), smap)],
        out_shape=[jax.ShapeDtypeStruct(h1_p.shape, F32), jax.ShapeDtypeStruct(h1_s.shape, F32)],
        scratch_shapes=[pltpu.VMEM((TOP_K, ROW_TILE, D_MODEL), F32), pltpu.SemaphoreType.DMA(())],
        compiler_params=pltpu.CompilerParams(dimension_semantics=("arbitrary",)),
        name="moe_combine",
    )(dest, y_sorted, h1_p, h1_s, gt_p, gt_s, nf)


def _rotary_tables(pos):
    inv = jnp.power(ROPE_BASE, -jnp.arange(0, RET_DK, 2, dtype=F32) / RET_DK)
    ang = pos.astype(F32)[:, None] * inv[None, :]
    cos, sin = jnp.cos(ang), jnp.sin(ang)
    return jnp.concatenate([cos, cos], axis=1), jnp.concatenate([-sin, sin], axis=1)


def _decay_tables(chunk):
    log_g = jnp.log1p(-jnp.exp2(-5.0 - jnp.arange(RET_HEADS, dtype=F32)))
    i = jnp.arange(chunk, dtype=F32)
    diff = i[:, None] - i[None, :]
    mask = jnp.where(diff[None] >= 0, jnp.exp(jnp.maximum(diff, 0.0)[None] * log_g[:, None, None]), 0.0)
    q_dec = jnp.exp((i + 1.0)[None, :] * log_g[:, None])
    k_dec = jnp.exp((chunk - 1.0 - i)[None, :] * log_g[:, None])
    c_dec = jnp.exp(chunk * log_g)
    return mask, q_dec, k_dec, c_dec


def kernel(x_prompt, x_sample, state_ret, state_pool, meta_tokens, norm_mix, w_in, ret_gn, pool_w, pool_scale,
           w_ret_branch, w_pool_branch, w_out, norm_ffn, w_router, b_router, w_gate_up, b_gate_up, w_down, b_down,
           norm_final):
    batch, seq, _ = x_prompt.shape
    nb = x_sample.shape[0]
    past_len = 16384
    n_prompt_tok = batch * seq
    n_tok = n_prompt_tok + nb

    w_in_bf = w_in[0].astype(BF16)
    wts = (ret_gn[0][None, :], pool_w[0].astype(BF16), pool_scale[0][None, :],
           w_ret_branch[0].astype(BF16), w_pool_branch[0].astype(BF16), w_out[0].astype(BF16),
           norm_ffn[0][None, :], w_router[0].T.astype(BF16), b_router[0][:, None])
    wgu = w_gate_up[0].astype(BF16)
    wd = w_down[0].astype(BF16)
    bgu = b_gate_up[0][:, None, :]
    bd = b_down[0][:, None, :]
    nmix = norm_mix[0][None, :]

    cos_p, sin_p = _rotary_tables(N_META + jnp.arange(seq, dtype=jnp.int32))
    pos_small = jnp.concatenate([jnp.arange(N_META, dtype=jnp.int32), jnp.full((nb,), past_len, jnp.int32)])
    cos_s, sin_s = _rotary_tables(pos_small)
    mask, q_dec, k_dec, c_dec = _decay_tables(CHUNK)
    dec = (mask,
           jnp.broadcast_to(q_dec[:, :, None], (RET_HEADS, CHUNK, RET_DK)),
           jnp.broadcast_to(k_dec[:, :, None], (RET_HEADS, CHUNK, RET_DK)),
           jnp.broadcast_to(c_dec[:, None, None], (RET_HEADS, 1, RET_DV)))
    m1, q1, k1, c1 = _decay_tables(1)
    sdec = jnp.stack([m1[:, 0, 0], q1[:, 0], k1[:, 0], c1], axis=1)

    x2d = x_prompt.reshape(n_prompt_tok, D_MODEL)
    proj_p = _inproj(x2d, nmix, w_in_bf, cos_p, sin_p, INPROJ_TILE, batch, BF16)
    xs2d = x_sample.reshape(nb, D_MODEL)
    x_small = jnp.concatenate([meta_tokens, xs2d], axis=0)
    proj_small = _inproj(x_small, nmix, w_in_bf, cos_s, sin_s, N_META + nb, 1, F32)
    proj_s = tuple(a[N_META:] for a in proj_small)

    lead = CHUNK - N_META
    kmeta = jnp.pad(proj_small[1][:N_META], ((lead, 0), (0, 0))).astype(BF16)
    vmeta = jnp.pad(proj_small[2][:N_META], ((lead, 0), (0, 0))).astype(BF16)
    pmeta = proj_small[4][:N_META]

    grp = SAMPLE_GROUP

    def cols(a):
        return a.T.reshape(RET_QK, nb // grp, grp).transpose(1, 0, 2)

    pool2d = state_pool[0].reshape(nb, POOL_BUF * POOL_WIDTH)
    (st_s, pool_s, h1_s, xn2_s, idx_s, gate_s, rank_s, gt_s, cnt_s) = _sample_mixer(
        sdec, cols(proj_s[0]), cols(proj_s[1]), proj_s, xs2d, state_ret[0], pool2d, wts)

    (h1_p, xn2_p, idx_p, gate_p, rank_p, gt_p, cnt, s_fin, p_fin) = _mixer(
        proj_p, x2d, kmeta, vmeta, pmeta, dec, wts, cnt_s, batch, seq)

    counts = cnt[:, 0].astype(jnp.int32)
    padded = ((counts + MOE_BLOCK - 1) // MOE_BLOCK) * MOE_BLOCK
    pad_end = jnp.cumsum(padded)
    pad_start = pad_end - padded
    n_blocks = (n_tok * TOP_K) // MOE_BLOCK + N_EXPERTS
    block_e = jnp.minimum(
        jnp.searchsorted(pad_end, jnp.arange(n_blocks, dtype=jnp.int32) * MOE_BLOCK, side='right'),
        N_EXPERTS - 1).astype(jnp.int32)
    nblk = (pad_end[-1:] // MOE_BLOCK).astype(jnp.int32)
    idx = jnp.concatenate([idx_p, idx_s], axis=1)
    rank = jnp.concatenate([rank_p, rank_s], axis=1)
    dest = pad_start[idx] + rank

    x_sorted = _dispatch(dest, xn2_p, xn2_s, n_blocks * MOE_BLOCK)
    y_sorted = _experts(block_e, nblk, x_sorted, wgu, bgu, wd, bd)
    y_p, y_s = _combine(dest, y_sorted, h1_p, h1_s, gt_p, gt_s, norm_final[None, :])

    y_prompt = y_p.reshape(batch, seq, D_MODEL)
    y_sample = y_s.reshape(nb, 1, D_MODEL)
    ret_state_prompt = s_fin[None]
    pool_state_prompt = p_fin[:, 1:, :][None]
    ret_state_sample = st_s[None]
    pool_state_sample = pool_s.reshape(nb, POOL_BUF, POOL_WIDTH)[None]
    return (y_prompt, y_sample, ret_state_prompt, pool_state_prompt, ret_state_sample, pool_state_sample)
```

```python
import functools

import jax
import jax.numpy as jnp
import numpy as np
from jax import lax
from jax.experimental import pallas as pl
from jax.experimental.pallas import tpu as pltpu

F32 = jnp.float32
BF16 = jnp.bfloat16

D_MODEL = 1024
N_META = 16
RET_HEADS = 4
RET_DK = 128
RET_DV = 256
RET_QK = RET_HEADS * RET_DK
RET_V = RET_HEADS * RET_DV
CHUNK = 128
ROPE_BASE = 10000.0
POOL_WINDOWS = (2, 4, 8, 16)
POOL_GROUPS = 4
POOL_GROUP_DIM = 128
POOL_WIDTH = POOL_GROUPS * POOL_GROUP_DIM
POOL_BUF = max(POOL_WINDOWS) - 1
N_EXPERTS = 32
TOP_K = 4
D_FF = D_MODEL
SWIGLU_LIMIT = 7.0
SWIGLU_ALPHA = 1.702
EPS = 1e-6
IN_WIDTHS = (RET_QK, RET_QK, RET_V, RET_V, POOL_WIDTH, D_MODEL, D_MODEL)
IN_TOTAL = sum(IN_WIDTHS)
IN_OFFS = tuple(int(s) for s in np.cumsum((0,) + IN_WIDTHS))

LANES = 128
INPROJ_TILE = 512
MIXER_TILE = 256
MOE_BLOCK = 256
ROW_TILE = 128
SAMPLE_GROUP = 8
WEIGHT_CAST_ROWS = 128
VMEM_LIMIT = 56 * 1024 * 1024

assert N_META + 1 >= max(POOL_WINDOWS)
assert POOL_WINDOWS == (2, 4, 8, 16)


def _dot(a, b):
    return jnp.dot(a, b, preferred_element_type=F32)


def _rmsnorm(x, w):
    return x * lax.rsqrt(jnp.mean(x * x, axis=-1, keepdims=True) + EPS) * w


def _inproj_kernel(x_ref, nw_ref, w_ref, cos_ref, sin_ref,
                   q_ref, k_ref, v_ref, g_ref, p_ref, ga_ref, gb_ref):
    xn = _rmsnorm(x_ref[...], nw_ref[...]).astype(BF16)
    cos = cos_ref[...]
    sin = sin_ref[...]

    def seg(i):
        return _dot(xn, w_ref[:, IN_OFFS[i]:IN_OFFS[i + 1]])

    def rot(a):
        return a * cos + pltpu.roll(a, RET_DK // 2, 1) * sin

    q = seg(0)
    k = seg(1)
    for h in range(RET_HEADS):
        sl = slice(h * RET_DK, (h + 1) * RET_DK)
        q_ref[:, sl] = rot(q[:, sl]).astype(q_ref.dtype)
        k_ref[:, sl] = (rot(k[:, sl]) * (RET_DK ** -0.5)).astype(k_ref.dtype)
    v_ref[...] = seg(2).astype(v_ref.dtype)
    g_ref[...] = seg(3).astype(g_ref.dtype)
    p_ref[...] = seg(4)
    ga_ref[...] = seg(5).astype(ga_ref.dtype)
    gb_ref[...] = seg(6).astype(gb_ref.dtype)


def _inproj(x2d, nw, w_in_bf, cosf, sinf, tile, n_outer, act_dtype):
    rows = x2d.shape[0]
    n_inner = rows // (tile * n_outer)
    row_map = lambda b, j: (b * n_inner + j, 0)
    tab_map = lambda b, j: (j, 0)
    const = lambda b, j: (0, 0)
    widths = IN_WIDTHS
    dts = (act_dtype, act_dtype, act_dtype, act_dtype, F32, act_dtype, act_dtype)
    return pl.pallas_call(
        _inproj_kernel,
        grid=(n_outer, n_inner),
        in_specs=[
            pl.BlockSpec((tile, D_MODEL), row_map),
            pl.BlockSpec((1, D_MODEL), const),
            pl.BlockSpec((D_MODEL, IN_TOTAL), const, pipeline_mode=pl.Buffered(1)),
            pl.BlockSpec((tile, RET_DK), tab_map),
            pl.BlockSpec((tile, RET_DK), tab_map),
        ],
        out_specs=[pl.BlockSpec((tile, w), row_map) for w in widths],
        out_shape=[jax.ShapeDtypeStruct((rows, w), dt) for w, dt in zip(widths, dts)],
        compiler_params=pltpu.CompilerParams(
            dimension_semantics=("arbitrary", "arbitrary"), vmem_limit_bytes=VMEM_LIMIT),
        name="inproj",
    )(x2d, nw, w_in_bf, cosf, sinf)


def _group_norm(o, gn_row):
    mu = jnp.mean(o, axis=-1, keepdims=True)
    var = jnp.mean(jnp.square(o - mu), axis=-1, keepdims=True)
    return (o - mu) * lax.rsqrt(var + EPS) * gn_row


def _pool_branch(groups, poolw_ref, pscale_ref, wpool_ref):
    pm = [_dot(g.astype(BF16), poolw_ref[i]) for i, g in enumerate(groups)]
    pm = jnp.concatenate(pm, axis=1) * pscale_ref[...]
    return _dot(pm.astype(BF16), wpool_ref[...])


def _merge_tail(o_norm, g, ga, gb, yb, x, wret_ref, wout_ref):
    gf = g.astype(F32)
    ya = _dot((gf * jax.nn.sigmoid(gf) * o_norm).astype(BF16), wret_ref[...])
    merged = jax.nn.sigmoid(ga.astype(F32)) * ya + jax.nn.sigmoid(gb.astype(F32)) * yb
    return x + _dot(merged.astype(BF16), wout_ref[...])


def _route(h1, nffn_ref, wrt_ref, br_ref, run_scr,
           xn2_ref, idx_ref, gate_ref, rank_ref, gatet_ref):
    tm = h1.shape[0]
    xn2 = _rmsnorm(h1, nffn_ref[...])
    xn2_ref[...] = xn2
    logits = lax.dot_general(wrt_ref[...], xn2.astype(BF16), (((1,), (1,)), ((), ())),
                             preferred_element_type=F32) + br_ref[...]
    e_iota = lax.broadcasted_iota(jnp.int32, (N_EXPERTS, tm), 0)
    work = logits
    vals, sels = [], []
    chosen = jnp.zeros((N_EXPERTS, tm), F32)
    for _ in range(TOP_K):
        m = jnp.max(work, axis=0, keepdims=True)
        sel = jnp.min(jnp.where(work == m, e_iota, N_EXPERTS), axis=0, keepdims=True)
        hit = e_iota == sel
        vals.append(m)
        sels.append(sel)
        chosen = jnp.where(hit, 1.0, chosen)
        work = jnp.where(hit, -jnp.inf, work)
    exps = [jnp.exp(v - vals[0]) for v in vals]
    denom = exps[0] + exps[1] + exps[2] + exps[3]
    gates = [e / denom for e in exps]
    r_i = lax.broadcasted_iota(jnp.int32, (tm, tm), 0)
    c_i = lax.broadcasted_iota(jnp.int32, (tm, tm), 1)
    before = jnp.where(r_i < c_i, 1.0, 0.0).astype(BF16)
    base = run_scr[...] + _dot(chosen.astype(BF16), before)
    for kk in range(TOP_K):
        rk = jnp.sum(jnp.where(e_iota == sels[kk], base, 0.0), axis=0, keepdims=True)
        rank_ref[kk:kk + 1, :] = rk.astype(jnp.int32)
        idx_ref[kk:kk + 1, :] = sels[kk]
        gate_ref[kk:kk + 1, :] = gates[kk]
    run_scr[...] = run_scr[...] + jnp.sum(chosen, axis=1, keepdims=True)
    row = lax.broadcasted_iota(jnp.int32, (LANES, tm), 0)
    gpad = jnp.zeros((LANES, tm), F32)
    for kk in range(TOP_K):
        gpad = jnp.where(row == kk, gates[kk], gpad)
    for c in range(tm // LANES):
        gatet_ref[c * LANES:(c + 1) * LANES, :] = gpad[:, c * LANES:(c + 1) * LANES].T


def _mixer_kernel(q_ref, k_ref, v_ref, g_ref, p_ref, ga_ref, gb_ref, x_ref,
                  kmeta_ref, vmeta_ref, pmeta_ref, mask_ref, qdec_ref, kdec_ref, cdec_ref, gn_ref,
                  poolw_ref, pscale_ref, wret_ref, wpool_ref, wout_ref, nffn_ref, wrt_ref, br_ref,
                  cnt0_ref,
                  h1_ref, xn2_ref, idx_ref, gate_ref, rank_ref, gatet_ref, cnt_ref, sfin_ref, pfin_ref,
                  s_scr, ext_scr, o_scr, run_scr):
    b = pl.program_id(0)
    j = pl.program_id(1)
    nj = pl.num_programs(1)
    tm = q_ref.shape[0]

    def state_update(s_old, kc, vc, h):
        kd = (kc.astype(F32) * kdec_ref[h]).astype(BF16)
        upd = lax.dot_general(kd, vc, (((0,), (0,)), ((), ())), preferred_element_type=F32)
        return s_old * cdec_ref[h] + upd

    @pl.when(jnp.logical_and(b == 0, j == 0))
    def _():
        run_scr[...] = cnt0_ref[:, 0:1]

    @pl.when(j == 0)
    def _():
        for h in range(RET_HEADS):
            kc = kmeta_ref[:, h * RET_DK:(h + 1) * RET_DK]
            vc = vmeta_ref[:, h * RET_DV:(h + 1) * RET_DV]
            s_scr[h] = state_update(jnp.zeros((RET_DK, RET_DV), F32), kc, vc, h)
        ext_scr[0:N_META, :] = pmeta_ref[...]

    for c in range(tm // CHUNK):
        rows = slice(c * CHUNK, (c + 1) * CHUNK)
        for h in range(RET_HEADS):
            qc = q_ref[rows, h * RET_DK:(h + 1) * RET_DK]
            kc = k_ref[rows, h * RET_DK:(h + 1) * RET_DK]
            vc = v_ref[rows, h * RET_DV:(h + 1) * RET_DV]
            s_old = s_scr[h]
            scores = lax.dot_general(qc, kc, (((1,), (1,)), ((), ())),
                                     preferred_element_type=F32) * mask_ref[h]
            qd = (qc.astype(F32) * qdec_ref[h]).astype(BF16)
            lhs = jnp.concatenate([scores.astype(BF16), qd], axis=1)
            rhs = jnp.concatenate([vc, s_old.astype(BF16)], axis=0)
            o = _dot(lhs, rhs)
            s_scr[h] = state_update(s_old, kc, vc, h)
            o_scr[rows, h * RET_DV:(h + 1) * RET_DV] = _group_norm(
                o, gn_ref[:, h * RET_DV:(h + 1) * RET_DV])

    p = p_ref[...]
    ext_scr[N_META:N_META + tm, :] = p
    a = ext_scr[...]
    g1 = POOL_GROUP_DIM
    s2 = a + pltpu.roll(a, 1, 0)
    s4 = s2[:, g1:] + pltpu.roll(s2[:, g1:], 2, 0)
    s8 = s4[:, g1:] + pltpu.roll(s4[:, g1:], 4, 0)
    s16 = s8[:, g1:] + pltpu.roll(s8[:, g1:], 8, 0)
    sums = (s2[N_META:, :g1], s4[N_META:, :g1], s8[N_META:, :g1], s16[N_META:, :])
    groups = [sums[i] * (1.0 / POOL_WINDOWS[i]) - p[:, i * g1:(i + 1) * g1] for i in range(POOL_GROUPS)]
    ext_scr[0:N_META, :] = ext_scr[tm:tm + N_META, :]

    yb = _pool_branch(groups, poolw_ref, pscale_ref, wpool_ref)
    h1 = _merge_tail(o_scr[...], g_ref[...], ga_ref[...], gb_ref[...], yb, x_ref[...], wret_ref, wout_ref)
    h1_ref[...] = h1
    _route(h1, nffn_ref, wrt_ref, br_ref, run_scr, xn2_ref, idx_ref, gate_ref, rank_ref, gatet_ref)
    cnt_ref[...] = jnp.broadcast_to(run_scr[...], cnt_ref.shape)

    @pl.when(j == nj - 1)
    def _():
        for h in range(RET_HEADS):
            sfin_ref[0, h] = s_scr[h]
        pfin_ref[0] = ext_scr[0:N_META, :]


def _mixer(proj, x2d, kmeta, vmeta, pmeta, dec, wts, cnt0, batch, seq):
    q, k, v, g, p, ga, gb = proj
    tm = MIXER_TILE
    nj = seq // tm
    rows = batch * seq
    row_map = lambda b, j: (b * nj + j, 0)
    lane_map = lambda b, j: (0, b * nj + j)
    c2 = lambda b, j: (0, 0)
    c3 = lambda b, j: (0, 0, 0)

    def whole(a):
        return pl.BlockSpec(a.shape, c2 if a.ndim == 2 else c3)

    mask, qdec, kdec, cdec = dec
    gn, poolw, pscale, wret, wpool, wout, nffn, wrt, br = wts
    in_arrays = [q, k, v, g, p, ga, gb, x2d, kmeta, vmeta, pmeta, mask, qdec, kdec, cdec, gn,
                 poolw, pscale, wret, wpool, wout, nffn, wrt, br, cnt0]
    in_specs = [pl.BlockSpec((tm, a.shape[1]), row_map) for a in in_arrays[:8]]
    in_specs += [whole(a) for a in in_arrays[8:]]
    out_shape = [
        jax.ShapeDtypeStruct((rows, D_MODEL), F32),
        jax.ShapeDtypeStruct((rows, D_MODEL), F32),
        jax.ShapeDtypeStruct((TOP_K, rows), jnp.int32),
        jax.ShapeDtypeStruct((TOP_K, rows), F32),
        jax.ShapeDtypeStruct((TOP_K, rows), jnp.int32),
        jax.ShapeDtypeStruct((rows, LANES), F32),
        jax.ShapeDtypeStruct((N_EXPERTS, LANES), F32),
        jax.ShapeDtypeStruct((batch, RET_HEADS, RET_DK, RET_DV), F32),
        jax.ShapeDtypeStruct((batch, N_META, POOL_WIDTH), F32),
    ]
    out_specs = [
        pl.BlockSpec((tm, D_MODEL), row_map),
        pl.BlockSpec((tm, D_MODEL), row_map),
        pl.BlockSpec((TOP_K, tm), lane_map),
        pl.BlockSpec((TOP_K, tm), lane_map),
        pl.BlockSpec((TOP_K, tm), lane_map),
        pl.BlockSpec((tm, LANES), row_map),
        pl.BlockSpec((N_EXPERTS, LANES), c2),
        pl.BlockSpec((1, RET_HEADS, RET_DK, RET_DV), lambda b, j: (b, 0, 0, 0)),
        pl.BlockSpec((1, N_META, POOL_WIDTH), lambda b, j: (b, 0, 0)),
    ]
    return pl.pallas_call(
        _mixer_kernel,
        grid=(batch, nj),
        in_specs=in_specs,
        out_specs=out_specs,
        out_shape=out_shape,
        scratch_shapes=[
            pltpu.VMEM((RET_HEADS, RET_DK, RET_DV), F32),
            pltpu.VMEM((N_META + tm, POOL_WIDTH), F32),
            pltpu.VMEM((tm, RET_V), F32),
            pltpu.VMEM((N_EXPERTS, 1), F32),
        ],
        compiler_params=pltpu.CompilerParams(
            dimension_semantics=("arbitrary", "arbitrary"), vmem_limit_bytes=VMEM_LIMIT),
        name="mixer",
    )(*in_arrays)


def _sample_kernel(sdec_ref, qt_ref, kt_ref, q_ref, k_ref, v_ref, g_ref, p_ref, ga_ref, gb_ref, x_ref,
                   st_ref, pool_ref, gn_ref,
                   poolw_ref, pscale_ref, wret_ref, wpool_ref, wout_ref, nffn_ref, wrt_ref, br_ref,
                   stout_ref, poolout_ref, h1_ref, xn2_ref, idx_ref, gate_ref, rank_ref, gatet_ref,
                   cnt_ref,
                   o_scr, run_scr):
    i = pl.program_id(0)
    n = pl.num_programs(0)
    grp = st_ref.shape[0]
    row0 = pl.multiple_of(i * grp, grp)

    @pl.when(i == 0)
    def _():
        run_scr[...] = jnp.zeros_like(run_scr)

    q8 = q_ref[pl.ds(row0, grp), :]
    k8 = k_ref[pl.ds(row0, grp), :]
    v8 = v_ref[pl.ds(row0, grp), :]
    for h in range(RET_HEADS):
        ksl = slice(h * RET_DK, (h + 1) * RET_DK)
        vsl = slice(h * RET_DV, (h + 1) * RET_DV)
        score = jnp.sum(q8[:, ksl] * k8[:, ksl], axis=1, keepdims=True) * sdec_ref[h, 0]
        intra = score * v8[:, vsl]
        for bb in range(grp):
            s_old = st_ref[bb, h]
            qcol = qt_ref[0, ksl, bb:bb + 1] * sdec_ref[h, 1]
            kcol = kt_ref[0, ksl, bb:bb + 1] * sdec_ref[h, 2]
            cross = jnp.sum(s_old * qcol, axis=0, keepdims=True)
            o_scr[pl.ds(row0 + bb, 1), vsl] = intra[bb:bb + 1, :] + cross
            stout_ref[bb, h] = s_old * sdec_ref[h, 3] + kcol * v8[bb:bb + 1, vsl]

    @pl.when(i == n - 1)
    def _():
        o = o_scr[...]
        o_norm = jnp.concatenate(
            [_group_norm(o[:, h * RET_DV:(h + 1) * RET_DV], gn_ref[:, h * RET_DV:(h + 1) * RET_DV])
             for h in range(RET_HEADS)], axis=1)
        p = p_ref[...]
        w = POOL_WIDTH
        g1 = POOL_GROUP_DIM

        def prev(r, lo):
            return pool_ref[:, r * w + lo:(r + 1) * w]

        s2 = p + prev(14, 0)
        s4 = s2[:, g1:] + prev(13, g1) + prev(12, g1)
        s8 = s4[:, g1:] + prev(11, 2 * g1) + prev(10, 2 * g1) + prev(9, 2 * g1) + prev(8, 2 * g1)
        s16 = s8[:, g1:]
        for r in range(7, -1, -1):
            s16 = s16 + prev(r, 3 * g1)
        sums = (s2[:, :g1], s4[:, :g1], s8[:, :g1], s16)
        groups = [sums[t] * (1.0 / POOL_WINDOWS[t]) - p[:, t * g1:(t + 1) * g1] for t in range(POOL_GROUPS)]
        poolout_ref[:, 0:(POOL_BUF - 1) * w] = pool_ref[:, w:POOL_BUF * w]
        poolout_ref[:, (POOL_BUF - 1) * w:] = p
        yb = _pool_branch(groups, poolw_ref, pscale_ref, wpool_ref)
        h1 = _merge_tail(o_norm, g_ref[...], ga_ref[...], gb_ref[...], yb, x_ref[...], wret_ref, wout_ref)
        h1_ref[...] = h1
        _route(h1, nffn_ref, wrt_ref, br_ref, run_scr, xn2_ref, idx_ref, gate_ref, rank_ref, gatet_ref)
        cnt_ref[...] = jnp.broadcast_to(run_scr[...], cnt_ref.shape)


def _sample_mixer(sdec, qt, kt, proj, x2d, state, pool2d, wts):
    q, k, v, g, p, ga, gb = proj
    nb = x2d.shape[0]
    grp = SAMPLE_GROUP
    c2 = lambda i: (0, 0)

    def whole(a):
        return pl.BlockSpec(a.shape, c2)

    gn, poolw, pscale, wret, wpool, wout, nffn, wrt, br = wts
    in_arrays = [sdec, qt, kt, q, k, v, g, p, ga, gb, x2d, state, pool2d, gn,
                 poolw, pscale, wret, wpool, wout, nffn, wrt, br]
    in_specs = [pl.BlockSpec(memory_space=pltpu.SMEM),
                pl.BlockSpec((1, RET_QK, grp), lambda i: (i, 0, 0)),
                pl.BlockSpec((1, RET_QK, grp), lambda i: (i, 0, 0))]
    in_specs += [whole(a) for a in (q, k, v, g, p, ga, gb, x2d)]
    in_specs += [pl.BlockSpec((grp, RET_HEADS, RET_DK, RET_DV), lambda i: (i, 0, 0, 0)), whole(pool2d), whole(gn),
                 pl.BlockSpec(poolw.shape, lambda i: (0, 0, 0))]
    in_specs += [whole(a) for a in (pscale, wret, wpool, wout, nffn, wrt, br)]
    out_shape = [
        jax.ShapeDtypeStruct(state.shape, F32),
        jax.ShapeDtypeStruct(pool2d.shape, F32),
        jax.ShapeDtypeStruct((nb, D_MODEL), F32),
        jax.ShapeDtypeStruct((nb, D_MODEL), F32),
        jax.ShapeDtypeStruct((TOP_K, nb), jnp.int32),
        jax.ShapeDtypeStruct((TOP_K, nb), F32),
        jax.ShapeDtypeStruct((TOP_K, nb), jnp.int32),
        jax.ShapeDtypeStruct((nb, LANES), F32),
        jax.ShapeDtypeStruct((N_EXPERTS, LANES), F32),
    ]
    out_specs = [pl.BlockSpec((grp, RET_HEADS, RET_DK, RET_DV), lambda i: (i, 0, 0, 0))]
    out_specs += [pl.BlockSpec(s.shape, c2) for s in out_shape[1:]]
    return pl.pallas_call(
        _sample_kernel,
        grid=(nb // grp,),
        in_specs=in_specs,
        out_specs=out_specs,
        out_shape=out_shape,
        scratch_shapes=[pltpu.VMEM((nb, RET_V), F32), pltpu.VMEM((N_EXPERTS, 1), F32)],
        compiler_params=pltpu.CompilerParams(
            dimension_semantics=("arbitrary",), vmem_limit_bytes=VMEM_LIMIT),
        name="sample_mixer",
    )(*in_arrays)


def _row_copy(src_ref, src_row, dst_ref, dst_row, sem):
    return pltpu.make_async_copy(src_ref.at[pl.ds(src_row, 1), :], dst_ref.at[pl.ds(dst_row, 1), :], sem)


def _dispatch_kernel(dest_ref, xp_ref, xs_ref, init_ref, out_ref, sem):
    del init_ref
    i = pl.program_id(0)
    n_prompt = pl.num_programs(0) - 1

    def scatter(src_ref):
        def issue(r, carry):
            for kk in range(TOP_K):
                _row_copy(src_ref, r, out_ref, dest_ref[kk, r], sem).start()
            return carry

        def drain(r, carry):
            for kk in range(TOP_K):
                _row_copy(src_ref, 0, out_ref, 0, sem).wait()
            return carry

        lax.fori_loop(0, ROW_TILE, issue, 0)
        lax.fori_loop(0, ROW_TILE, drain, 0)

    @pl.when(i < n_prompt)
    def _():
        scatter(xp_ref)

    @pl.when(i == n_prompt)
    def _():
        scatter(xs_ref)


def _dispatch(dest, xn2_p, xn2_s, n_sorted):
    n_prompt = xn2_p.shape[0] // ROW_TILE
    init = jnp.zeros((n_sorted, D_MODEL), F32)
    return pl.pallas_call(
        _dispatch_kernel,
        grid=(n_prompt + 1,),
        in_specs=[
            pl.BlockSpec((TOP_K, ROW_TILE), lambda i: (0, i), memory_space=pltpu.SMEM),
            pl.BlockSpec((ROW_TILE, D_MODEL), lambda i: (jnp.minimum(i, n_prompt - 1), 0)),
            pl.BlockSpec((ROW_TILE, D_MODEL), lambda i: (0, 0)),
            pl.BlockSpec(memory_space=pl.ANY),
        ],
        out_specs=pl.BlockSpec(memory_space=pl.ANY),
        out_shape=jax.ShapeDtypeStruct((n_sorted, D_MODEL), F32),
        scratch_shapes=[pltpu.SemaphoreType.DMA(())],
        input_output_aliases={3: 0},
        compiler_params=pltpu.CompilerParams(dimension_semantics=("arbitrary",)),
        name="moe_dispatch",
    )(dest, xn2_p, xn2_s, init)


def _expert_kernel(be_ref, nblk_ref, x_ref, wgu_ref, bgu_ref, wd_ref, bd_ref, y_ref, wgu_bf, wd_bf):
    i = pl.program_id(0)
    live = i < nblk_ref[0]
    new_expert = jnp.logical_or(i == 0, be_ref[i] != be_ref[jnp.maximum(i - 1, 0)])

    @pl.when(jnp.logical_and(live, new_expert))
    def _():
        def cast(c, carry):
            rows = pl.ds(pl.multiple_of(c * WEIGHT_CAST_ROWS, WEIGHT_CAST_ROWS), WEIGHT_CAST_ROWS)
            wgu_bf[rows, :] = wgu_ref[0, rows, :].astype(BF16)
            wd_bf[rows, :] = wd_ref[0, rows, :].astype(BF16)
            return carry

        lax.fori_loop(0, D_MODEL // WEIGHT_CAST_ROWS, cast, 0)

    @pl.when(jnp.logical_not(live))
    def _():
        y_ref[...] = jnp.zeros_like(y_ref)

    @pl.when(live)
    def _():
        h = _dot(x_ref[...].astype(BF16), wgu_bf[...]) + bgu_ref[0]
        gate = jnp.minimum(h[:, :D_FF], SWIGLU_LIMIT)
        up = jnp.clip(h[:, D_FF:], -SWIGLU_LIMIT, SWIGLU_LIMIT)
        glu = gate * jax.nn.sigmoid(gate * SWIGLU_ALPHA)
        y_ref[...] = _dot(((up + 1.0) * glu).astype(BF16), wd_bf[...]) + bd_ref[0]


def _experts(block_e, nblk, x_sorted, wgu, bgu, wd, bd):
    n_blocks = x_sorted.shape[0] // MOE_BLOCK
    grid_spec = pltpu.PrefetchScalarGridSpec(
        num_scalar_prefetch=2,
        grid=(n_blocks,),
        in_specs=[
            pl.BlockSpec((MOE_BLOCK, D_MODEL), lambda i, be, nb: (i, 0)),
            pl.BlockSpec((1, D_MODEL, 2 * D_FF), lambda i, be, nb: (be[i], 0, 0)),
            pl.BlockSpec((1, 1, 2 * D_FF), lambda i, be, nb: (be[i], 0, 0)),
            pl.BlockSpec((1, D_FF, D_MODEL), lambda i, be, nb: (be[i], 0, 0)),
            pl.BlockSpec((1, 1, D_MODEL), lambda i, be, nb: (be[i], 0, 0)),
        ],
        out_specs=pl.BlockSpec((MOE_BLOCK, D_MODEL), lambda i, be, nb: (i, 0)),
        scratch_shapes=[pltpu.VMEM((D_MODEL, 2 * D_FF), BF16), pltpu.VMEM((D_FF, D_MODEL), BF16)],
    )
    return pl.pallas_call(
        _expert_kernel,
        grid_spec=grid_spec,
        out_shape=jax.ShapeDtypeStruct(x_sorted.shape, F32),
        compiler_params=pltpu.CompilerParams(
            dimension_semantics=("arbitrary",), vmem_limit_bytes=VMEM_LIMIT),
        name="moe_experts",
    )(block_e, nblk, x_sorted, wgu, bgu, wd, bd)


def _combine_kernel(dest_ref, ys_ref, hp_ref, hs_ref, gp_ref, gs_ref, nf_ref, yp_ref, ysmp_ref, rows_scr, sem):
    i = pl.program_id(0)
    n_prompt = pl.num_programs(0) - 1

    def issue(r, carry):
        for kk in range(TOP_K):
            _row_copy(ys_ref, dest_ref[kk, r], rows_scr.at[kk], r, sem).start()
        return carry

    def drain(r, carry):
        for kk in range(TOP_K):
            _row_copy(ys_ref, 0, rows_scr.at[kk], 0, sem).wait()
        return carry

    lax.fori_loop(0, ROW_TILE, issue, 0)
    lax.fori_loop(0, ROW_TILE, drain, 0)

    def finish(h_ref, gt_ref, out_ref):
        y = h_ref[...]
        gt = gt_ref[...]
        for kk in range(TOP_K):
            y = y + rows_scr[kk] * gt[:, kk:kk + 1]
        out_ref[...] = _rmsnorm(y, nf_ref[...])

    @pl.when(i < n_prompt)
    def _():
        finish(hp_ref, gp_ref, yp_ref)

    @pl.when(i == n_prompt)
    def _():
        finish(hs_ref, gs_ref, ysmp_ref)


def _combine(dest, y_sorted, h1_p, h1_s, gt_p, gt_s, nf):
    n_prompt = h1_p.shape[0] // ROW_TILE
    pmap = lambda i: (jnp.minimum(i, n_prompt - 1), 0)
    smap = lambda i: (0, 0)
    return pl.pallas_call(
        _combine_kernel,
        grid=(n_prompt + 1,),
        in_specs=[
            pl.BlockSpec((TOP_K, ROW_TILE), lambda i: (0, i), memory_space=pltpu.SMEM),
            pl.BlockSpec(memory_space=pl.ANY),
            pl.BlockSpec((ROW_TILE, D_MODEL), pmap),
            pl.BlockSpec((ROW_TILE, D_MODEL), smap),
            pl.BlockSpec((ROW_TILE, LANES), pmap),
            pl.BlockSpec((ROW_TILE, LANES), smap),
            pl.BlockSpec((1, D_MODEL), smap),
        ],
        out_specs=[pl.BlockSpec((ROW_TILE, D_MODEL), pmap), pl.BlockSpec((ROW_TILE, D_MODEL), smap)],
        out_shape=[jax.ShapeDtypeStruct(h1_p.shape, F32), jax.ShapeDtypeStruct(h1_s.shape, F32)],
        scratch_shapes=[pltpu.VMEM((TOP_K, ROW_TILE, D_MODEL), F32), pltpu.SemaphoreType.DMA(())],
        compiler_params=pltpu.CompilerParams(dimension_semantics=("arbitrary",)),
        name="moe_combine",
    )(dest, y_sorted, h1_p, h1_s, gt_p, gt_s, nf)


def _rotary_tables(pos):
    f = np.float32
    inv = np.power(f(ROPE_BASE), -np.arange(0, RET_DK, 2, dtype=f) / f(RET_DK)).astype(f)
    ang = (np.asarray(pos, f)[:, None] * inv[None, :]).astype(f)
    cos, sin = np.cos(ang).astype(f), np.sin(ang).astype(f)
    return np.concatenate([cos, cos], axis=1), np.concatenate([-sin, sin], axis=1)


def _decay_tables(chunk):
    f = np.float32
    log_g = np.log1p(-np.exp2(f(-5.0) - np.arange(RET_HEADS, dtype=f))).astype(f)
    i = np.arange(chunk, dtype=f)
    diff = i[:, None] - i[None, :]
    mask = np.where(diff[None] >= 0, np.exp(np.maximum(diff, f(0.0))[None] * log_g[:, None, None]), f(0.0)).astype(f)
    q_dec = np.exp((i + f(1.0))[None, :] * log_g[:, None]).astype(f)
    k_dec = np.exp((f(chunk) - f(1.0) - i)[None, :] * log_g[:, None]).astype(f)
    c_dec = np.exp(f(chunk) * log_g).astype(f)
    return mask, q_dec, k_dec, c_dec


def kernel(x_prompt, x_sample, state_ret, state_pool, meta_tokens, norm_mix, w_in, ret_gn, pool_w, pool_scale,
           w_ret_branch, w_pool_branch, w_out, norm_ffn, w_router, b_router, w_gate_up, b_gate_up, w_down, b_down,
           norm_final):
    batch, seq, _ = x_prompt.shape
    nb = x_sample.shape[0]
    past_len = 16384
    n_prompt_tok = batch * seq
    n_tok = n_prompt_tok + nb

    w_in_bf = w_in[0].astype(BF16)
    wts = (ret_gn[0][None, :], pool_w[0].astype(BF16), pool_scale[0][None, :],
           w_ret_branch[0].astype(BF16), w_pool_branch[0].astype(BF16), w_out[0].astype(BF16),
           norm_ffn[0][None, :], w_router[0].T.astype(BF16), b_router[0][:, None])
    wgu = w_gate_up[0]
    wd = w_down[0]
    bgu = b_gate_up[0][:, None, :]
    bd = b_down[0][:, None, :]
    nmix = norm_mix[0][None, :]

    cos_p, sin_p = _rotary_tables(N_META + np.arange(seq))
    cos_s, sin_s = _rotary_tables(np.concatenate([np.arange(N_META), np.full((nb,), past_len)]))
    mask, q_dec, k_dec, c_dec = _decay_tables(CHUNK)
    dec = (mask,
           np.ascontiguousarray(np.broadcast_to(q_dec[:, :, None], (RET_HEADS, CHUNK, RET_DK))),
           np.ascontiguousarray(np.broadcast_to(k_dec[:, :, None], (RET_HEADS, CHUNK, RET_DK))),
           np.ascontiguousarray(np.broadcast_to(c_dec[:, None, None], (RET_HEADS, 1, RET_DV))))
    m1, q1, k1, c1 = _decay_tables(1)
    sdec = np.stack([m1[:, 0, 0], q1[:, 0], k1[:, 0], c1], axis=1)

    x2d = x_prompt.reshape(n_prompt_tok, D_MODEL)
    proj_p = _inproj(x2d, nmix, w_in_bf, cos_p, sin_p, INPROJ_TILE, batch, BF16)
    xs2d = x_sample.reshape(nb, D_MODEL)
    x_small = jnp.concatenate([meta_tokens, xs2d], axis=0)
    proj_small = _inproj(x_small, nmix, w_in_bf, cos_s, sin_s, N_META + nb, 1, F32)
    proj_s = tuple(a[N_META:] for a in proj_small)

    lead = CHUNK - N_META
    kmeta = jnp.pad(proj_small[1][:N_META], ((lead, 0), (0, 0))).astype(BF16)
    vmeta = jnp.pad(proj_small[2][:N_META], ((lead, 0), (0, 0))).astype(BF16)
    pmeta = proj_small[4][:N_META]

    grp = SAMPLE_GROUP

    def cols(a):
        return a.T.reshape(RET_QK, nb // grp, grp).transpose(1, 0, 2)

    pool2d = state_pool[0].reshape(nb, POOL_BUF * POOL_WIDTH)
    (st_s, pool_s, h1_s, xn2_s, idx_s, gate_s, rank_s, gt_s, cnt_s) = _sample_mixer(
        sdec, cols(proj_s[0]), cols(proj_s[1]), proj_s, xs2d, state_ret[0], pool2d, wts)

    (h1_p, xn2_p, idx_p, gate_p, rank_p, gt_p, cnt, s_fin, p_fin) = _mixer(
        proj_p, x2d, kmeta, vmeta, pmeta, dec, wts, cnt_s, batch, seq)

    counts = cnt[:, 0].astype(jnp.int32)
    padded = ((counts + MOE_BLOCK - 1) // MOE_BLOCK) * MOE_BLOCK
    pad_end = jnp.cumsum(padded)
    pad_start = pad_end - padded
    n_blocks = (n_tok * TOP_K) // MOE_BLOCK + N_EXPERTS
    block_row = jnp.arange(n_blocks, dtype=jnp.int32) * MOE_BLOCK
    block_e = jnp.minimum(jnp.sum((pad_end[None, :] <= block_row[:, None]).astype(jnp.int32), axis=1),
                          N_EXPERTS - 1)
    nblk = (pad_end[-1:] // MOE_BLOCK).astype(jnp.int32)
    idx = jnp.concatenate([idx_p, idx_s], axis=1)
    rank = jnp.concatenate([rank_p, rank_s], axis=1)
    e_ids = jnp.arange(N_EXPERTS, dtype=jnp.int32)[:, None, None]
    dest = rank + jnp.sum(jnp.where(idx[None] == e_ids, pad_start[:, None, None], 0), axis=0)

    x_sorted = _dispatch(dest, xn2_p, xn2_s, n_blocks * MOE_BLOCK)
    y_sorted = _experts(block_e, nblk, x_sorted, wgu, bgu, wd, bd)
    y_p, y_s = _combine(dest, y_sorted, h1_p, h1_s, gt_p, gt_s, norm_final[None, :])

    y_prompt = y_p.reshape(batch, seq, D_MODEL)
    y_sample = y_s.reshape(nb, 1, D_MODEL)
    ret_state_prompt = s_fin[None]
    pool_state_prompt = p_fin[:, 1:, :][None]
    ret_state_sample = st_s[None]
    pool_state_sample = pool_s.reshape(nb, POOL_BUF, POOL_WIDTH)[None]
    return (y_prompt, y_sample, ret_state_prompt, pool_state_prompt, ret_state_sample, pool_state_sample)
```

```python
import functools

import jax
import jax.numpy as jnp
import numpy as np
from jax import lax
from jax.experimental import pallas as pl
from jax.experimental.pallas import tpu as pltpu

F32 = jnp.float32
BF16 = jnp.bfloat16

D_MODEL = 1024
N_META = 16
RET_HEADS = 4
RET_DK = 128
RET_DV = 256
RET_QK = RET_HEADS * RET_DK
RET_V = RET_HEADS * RET_DV
CHUNK = 128
ROPE_BASE = 10000.0
POOL_WINDOWS = (2, 4, 8, 16)
POOL_GROUPS = 4
POOL_GROUP_DIM = 128
POOL_WIDTH = POOL_GROUPS * POOL_GROUP_DIM
POOL_BUF = max(POOL_WINDOWS) - 1
N_EXPERTS = 32
TOP_K = 4
D_FF = D_MODEL
SWIGLU_LIMIT = 7.0
SWIGLU_ALPHA = 1.702
EPS = 1e-6
IN_WIDTHS = (RET_QK, RET_QK, RET_V, RET_V, POOL_WIDTH, D_MODEL, D_MODEL)
IN_TOTAL = sum(IN_WIDTHS)
IN_OFFS = tuple(int(s) for s in np.cumsum((0,) + IN_WIDTHS))

LANES = 128
INPROJ_TILE = 512
MIXER_TILE = 256
MOE_BLOCK = 256
ROW_TILE = 128
SAMPLE_GROUP = 8
WEIGHT_CAST_ROWS = 128
SCATTER_LEAD = 2 * MOE_BLOCK
SMEM_1D_TILE = 1024
INVERT_CHUNKS = 8
VMEM_LIMIT = 56 * 1024 * 1024

assert N_META + 1 >= max(POOL_WINDOWS)
assert POOL_WINDOWS == (2, 4, 8, 16)


def _dot(a, b):
    return jnp.dot(a, b, preferred_element_type=F32)


def _rmsnorm(x, w):
    return x * lax.rsqrt(jnp.mean(x * x, axis=-1, keepdims=True) + EPS) * w


def _inproj_kernel(x_ref, nw_ref, w_ref, cos_ref, sin_ref,
                   q_ref, k_ref, v_ref, g_ref, p_ref, ga_ref, gb_ref):
    xn = _rmsnorm(x_ref[...], nw_ref[...]).astype(BF16)
    cos = cos_ref[...]
    sin = sin_ref[...]

    def seg(i):
        return _dot(xn, w_ref[:, IN_OFFS[i]:IN_OFFS[i + 1]])

    def rot(a):
        return a * cos + pltpu.roll(a, RET_DK // 2, 1) * sin

    q = seg(0)
    k = seg(1)
    for h in range(RET_HEADS):
        sl = slice(h * RET_DK, (h + 1) * RET_DK)
        q_ref[:, sl] = rot(q[:, sl]).astype(q_ref.dtype)
        k_ref[:, sl] = (rot(k[:, sl]) * (RET_DK ** -0.5)).astype(k_ref.dtype)
    v_ref[...] = seg(2).astype(v_ref.dtype)
    g_ref[...] = seg(3).astype(g_ref.dtype)
    p_ref[...] = seg(4)
    ga_ref[...] = seg(5).astype(ga_ref.dtype)
    gb_ref[...] = seg(6).astype(gb_ref.dtype)


def _inproj(x2d, nw, w_in_bf, cosf, sinf, tile, n_outer, act_dtype):
    rows = x2d.shape[0]
    n_inner = rows // (tile * n_outer)
    row_map = lambda b, j: (b * n_inner + j, 0)
    tab_map = lambda b, j: (j, 0)
    const = lambda b, j: (0, 0)
    widths = IN_WIDTHS
    dts = (act_dtype, act_dtype, act_dtype, act_dtype, F32, act_dtype, act_dtype)
    return pl.pallas_call(
        _inproj_kernel,
        grid=(n_outer, n_inner),
        in_specs=[
            pl.BlockSpec((tile, D_MODEL), row_map),
            pl.BlockSpec((1, D_MODEL), const),
            pl.BlockSpec((D_MODEL, IN_TOTAL), const, pipeline_mode=pl.Buffered(1)),
            pl.BlockSpec((tile, RET_DK), tab_map),
            pl.BlockSpec((tile, RET_DK), tab_map),
        ],
        out_specs=[pl.BlockSpec((tile, w), row_map) for w in widths],
        out_shape=[jax.ShapeDtypeStruct((rows, w), dt) for w, dt in zip(widths, dts)],
        compiler_params=pltpu.CompilerParams(
            dimension_semantics=("arbitrary", "arbitrary"), vmem_limit_bytes=VMEM_LIMIT),
        name="inproj",
    )(x2d, nw, w_in_bf, cosf, sinf)


def _group_norm(o, gn_row):
    mu = jnp.mean(o, axis=-1, keepdims=True)
    var = jnp.mean(jnp.square(o - mu), axis=-1, keepdims=True)
    return (o - mu) * lax.rsqrt(var + EPS) * gn_row


def _pool_branch(groups, poolw_ref, pscale_ref, wpool_ref):
    pm = [_dot(g.astype(BF16), poolw_ref[i]) for i, g in enumerate(groups)]
    pm = jnp.concatenate(pm, axis=1) * pscale_ref[...]
    return _dot(pm.astype(BF16), wpool_ref[...])


def _merge_tail(o_norm, g, ga, gb, yb, x, wret_ref, wout_ref):
    gf = g.astype(F32)
    ya = _dot((gf * jax.nn.sigmoid(gf) * o_norm).astype(BF16), wret_ref[...])
    merged = jax.nn.sigmoid(ga.astype(F32)) * ya + jax.nn.sigmoid(gb.astype(F32)) * yb
    return x + _dot(merged.astype(BF16), wout_ref[...])


def _route(h1, nffn_ref, wrt_ref, br_ref, run_scr,
           xn2_ref, idx_ref, gate_ref, rank_ref, gatet_ref):
    tm = h1.shape[0]
    xn2 = _rmsnorm(h1, nffn_ref[...])
    xn2_ref[...] = xn2
    logits = lax.dot_general(wrt_ref[...], xn2.astype(BF16), (((1,), (1,)), ((), ())),
                             preferred_element_type=F32) + br_ref[...]
    e_iota = lax.broadcasted_iota(jnp.int32, (N_EXPERTS, tm), 0)
    work = logits
    vals, sels = [], []
    chosen = jnp.zeros((N_EXPERTS, tm), F32)
    for _ in range(TOP_K):
        m = jnp.max(work, axis=0, keepdims=True)
        sel = jnp.min(jnp.where(work == m, e_iota, N_EXPERTS), axis=0, keepdims=True)
        hit = e_iota == sel
        vals.append(m)
        sels.append(sel)
        chosen = jnp.where(hit, 1.0, chosen)
        work = jnp.where(hit, -jnp.inf, work)
    exps = [jnp.exp(v - vals[0]) for v in vals]
    denom = exps[0] + exps[1] + exps[2] + exps[3]
    gates = [e / denom for e in exps]
    r_i = lax.broadcasted_iota(jnp.int32, (tm, tm), 0)
    c_i = lax.broadcasted_iota(jnp.int32, (tm, tm), 1)
    before = jnp.where(r_i < c_i, 1.0, 0.0).astype(BF16)
    base = run_scr[...] + _dot(chosen.astype(BF16), before)
    for kk in range(TOP_K):
        rk = jnp.sum(jnp.where(e_iota == sels[kk], base, 0.0), axis=0, keepdims=True)
        rank_ref[kk:kk + 1, :] = rk.astype(jnp.int32)
        idx_ref[kk:kk + 1, :] = sels[kk]
        gate_ref[kk:kk + 1, :] = gates[kk]
    run_scr[...] = run_scr[...] + jnp.sum(chosen, axis=1, keepdims=True)
    row = lax.broadcasted_iota(jnp.int32, (LANES, tm), 0)
    gpad = jnp.zeros((LANES, tm), F32)
    for kk in range(TOP_K):
        gpad = jnp.where(row == kk, gates[kk], gpad)
    for c in range(tm // LANES):
        gatet_ref[c * LANES:(c + 1) * LANES, :] = gpad[:, c * LANES:(c + 1) * LANES].T


def _mixer_kernel(q_ref, k_ref, v_ref, g_ref, p_ref, ga_ref, gb_ref, x_ref,
                  kmeta_ref, vmeta_ref, pmeta_ref, mask_ref, qdec_ref, kdec_ref, cdec_ref, gn_ref,
                  poolw_ref, pscale_ref, wret_ref, wpool_ref, wout_ref, nffn_ref, wrt_ref, br_ref,
                  cnt0_ref, xn2_all_ref,
                  h1_ref, xn2_ref, idx_ref, gate_ref, rank_ref, gatet_ref, cnt_ref, sfin_ref, pfin_ref,
                  s_scr, ext_scr, o_scr, run_scr):
    del xn2_all_ref
    b = pl.program_id(0)
    j = pl.program_id(1)
    nj = pl.num_programs(1)
    tm = q_ref.shape[0]

    def state_update(s_old, kc, vc, h):
        kd = (kc.astype(F32) * kdec_ref[h]).astype(BF16)
        upd = lax.dot_general(kd, vc, (((0,), (0,)), ((), ())), preferred_element_type=F32)
        return s_old * cdec_ref[h] + upd

    @pl.when(jnp.logical_and(b == 0, j == 0))
    def _():
        run_scr[...] = cnt0_ref[:, 0:1]

    @pl.when(j == 0)
    def _():
        for h in range(RET_HEADS):
            kc = kmeta_ref[:, h * RET_DK:(h + 1) * RET_DK]
            vc = vmeta_ref[:, h * RET_DV:(h + 1) * RET_DV]
            s_scr[h] = state_update(jnp.zeros((RET_DK, RET_DV), F32), kc, vc, h)
        ext_scr[0:N_META, :] = pmeta_ref[...]

    for c in range(tm // CHUNK):
        rows = slice(c * CHUNK, (c + 1) * CHUNK)
        for h in range(RET_HEADS):
            qc = q_ref[rows, h * RET_DK:(h + 1) * RET_DK]
            kc = k_ref[rows, h * RET_DK:(h + 1) * RET_DK]
            vc = v_ref[rows, h * RET_DV:(h + 1) * RET_DV]
            s_old = s_scr[h]
            scores = lax.dot_general(qc, kc, (((1,), (1,)), ((), ())),
                                     preferred_element_type=F32) * mask_ref[h]
            qd = (qc.astype(F32) * qdec_ref[h]).astype(BF16)
            lhs = jnp.concatenate([scores.astype(BF16), qd], axis=1)
            rhs = jnp.concatenate([vc, s_old.astype(BF16)], axis=0)
            o = _dot(lhs, rhs)
            s_scr[h] = state_update(s_old, kc, vc, h)
            o_scr[rows, h * RET_DV:(h + 1) * RET_DV] = _group_norm(
                o, gn_ref[:, h * RET_DV:(h + 1) * RET_DV])

    p = p_ref[...]
    ext_scr[N_META:N_META + tm, :] = p
    a = ext_scr[...]
    g1 = POOL_GROUP_DIM
    s2 = a + pltpu.roll(a, 1, 0)
    s4 = s2[:, g1:] + pltpu.roll(s2[:, g1:], 2, 0)
    s8 = s4[:, g1:] + pltpu.roll(s4[:, g1:], 4, 0)
    s16 = s8[:, g1:] + pltpu.roll(s8[:, g1:], 8, 0)
    sums = (s2[N_META:, :g1], s4[N_META:, :g1], s8[N_META:, :g1], s16[N_META:, :])
    groups = [sums[i] * (1.0 / POOL_WINDOWS[i]) - p[:, i * g1:(i + 1) * g1] for i in range(POOL_GROUPS)]
    ext_scr[0:N_META, :] = ext_scr[tm:tm + N_META, :]

    yb = _pool_branch(groups, poolw_ref, pscale_ref, wpool_ref)
    h1 = _merge_tail(o_scr[...], g_ref[...], ga_ref[...], gb_ref[...], yb, x_ref[...], wret_ref, wout_ref)
    h1_ref[...] = h1
    _route(h1, nffn_ref, wrt_ref, br_ref, run_scr, xn2_ref, idx_ref, gate_ref, rank_ref, gatet_ref)
    cnt_ref[...] = jnp.broadcast_to(run_scr[...], cnt_ref.shape)

    @pl.when(j == nj - 1)
    def _():
        for h in range(RET_HEADS):
            sfin_ref[0, h] = s_scr[h]
        pfin_ref[0] = ext_scr[0:N_META, :]


def _mixer(proj, x2d, kmeta, vmeta, pmeta, dec, wts, cnt0, xn2_all, batch, seq):
    q, k, v, g, p, ga, gb = proj
    tm = MIXER_TILE
    nj = seq // tm
    rows = batch * seq
    row_map = lambda b, j: (b * nj + j, 0)
    lane_map = lambda b, j: (0, b * nj + j)
    c2 = lambda b, j: (0, 0)
    c3 = lambda b, j: (0, 0, 0)

    def whole(a):
        return pl.BlockSpec(a.shape, c2 if a.ndim == 2 else c3)

    mask, qdec, kdec, cdec = dec
    gn, poolw, pscale, wret, wpool, wout, nffn, wrt, br = wts
    in_arrays = [q, k, v, g, p, ga, gb, x2d, kmeta, vmeta, pmeta, mask, qdec, kdec, cdec, gn,
                 poolw, pscale, wret, wpool, wout, nffn, wrt, br, cnt0]
    in_specs = [pl.BlockSpec((tm, a.shape[1]), row_map) for a in in_arrays[:8]]
    in_specs += [whole(a) for a in in_arrays[8:]]
    xn2_alias_idx = len(in_arrays)
    in_arrays.append(xn2_all)
    in_specs.append(pl.BlockSpec(memory_space=pl.ANY))
    out_shape = [
        jax.ShapeDtypeStruct((rows, D_MODEL), F32),
        jax.ShapeDtypeStruct(xn2_all.shape, F32),
        jax.ShapeDtypeStruct((TOP_K, rows), jnp.int32),
        jax.ShapeDtypeStruct((TOP_K, rows), F32),
        jax.ShapeDtypeStruct((TOP_K, rows), jnp.int32),
        jax.ShapeDtypeStruct((rows, LANES), F32),
        jax.ShapeDtypeStruct((N_EXPERTS, LANES), F32),
        jax.ShapeDtypeStruct((batch, RET_HEADS, RET_DK, RET_DV), F32),
        jax.ShapeDtypeStruct((batch, N_META, POOL_WIDTH), F32),
    ]
    out_specs = [
        pl.BlockSpec((tm, D_MODEL), row_map),
        pl.BlockSpec((tm, D_MODEL), row_map),
        pl.BlockSpec((TOP_K, tm), lane_map),
        pl.BlockSpec((TOP_K, tm), lane_map),
        pl.BlockSpec((TOP_K, tm), lane_map),
        pl.BlockSpec((tm, LANES), row_map),
        pl.BlockSpec((N_EXPERTS, LANES), c2),
        pl.BlockSpec((1, RET_HEADS, RET_DK, RET_DV), lambda b, j: (b, 0, 0, 0)),
        pl.BlockSpec((1, N_META, POOL_WIDTH), lambda b, j: (b, 0, 0)),
    ]
    return pl.pallas_call(
        _mixer_kernel,
        grid=(batch, nj),
        in_specs=in_specs,
        out_specs=out_specs,
        out_shape=out_shape,
        scratch_shapes=[
            pltpu.VMEM((RET_HEADS, RET_DK, RET_DV), F32),
            pltpu.VMEM((N_META + tm, POOL_WIDTH), F32),
            pltpu.VMEM((tm, RET_V), F32),
            pltpu.VMEM((N_EXPERTS, 1), F32),
        ],
        input_output_aliases={xn2_alias_idx: 1},
        compiler_params=pltpu.CompilerParams(
            dimension_semantics=("arbitrary", "arbitrary"), vmem_limit_bytes=VMEM_LIMIT),
        name="mixer",
    )(*in_arrays)


def _sample_kernel(sdec_ref, qt_ref, kt_ref, q_ref, k_ref, v_ref, g_ref, p_ref, ga_ref, gb_ref, x_ref,
                   st_ref, pool_ref, gn_ref,
                   poolw_ref, pscale_ref, wret_ref, wpool_ref, wout_ref, nffn_ref, wrt_ref, br_ref, xn2_init_ref,
                   stout_ref, poolout_ref, h1_ref, xn2_ref, idx_ref, gate_ref, rank_ref, gatet_ref,
                   cnt_ref,
                   o_scr, run_scr):
    del xn2_init_ref
    i = pl.program_id(0)
    n = pl.num_programs(0)
    grp = st_ref.shape[0]
    row0 = pl.multiple_of(i * grp, grp)

    @pl.when(i == 0)
    def _():
        run_scr[...] = jnp.zeros_like(run_scr)

    q8 = q_ref[pl.ds(row0, grp), :]
    k8 = k_ref[pl.ds(row0, grp), :]
    v8 = v_ref[pl.ds(row0, grp), :]
    for h in range(RET_HEADS):
        ksl = slice(h * RET_DK, (h + 1) * RET_DK)
        vsl = slice(h * RET_DV, (h + 1) * RET_DV)
        score = jnp.sum(q8[:, ksl] * k8[:, ksl], axis=1, keepdims=True) * sdec_ref[h, 0]
        intra = score * v8[:, vsl]
        for bb in range(grp):
            s_old = st_ref[bb, h]
            qcol = qt_ref[0, ksl, bb:bb + 1] * sdec_ref[h, 1]
            kcol = kt_ref[0, ksl, bb:bb + 1] * sdec_ref[h, 2]
            cross = jnp.sum(s_old * qcol, axis=0, keepdims=True)
            o_scr[pl.ds(row0 + bb, 1), vsl] = intra[bb:bb + 1, :] + cross
            stout_ref[bb, h] = s_old * sdec_ref[h, 3] + kcol * v8[bb:bb + 1, vsl]

    @pl.when(i == n - 1)
    def _():
        o = o_scr[...]
        o_norm = jnp.concatenate(
            [_group_norm(o[:, h * RET_DV:(h + 1) * RET_DV], gn_ref[:, h * RET_DV:(h + 1) * RET_DV])
             for h in range(RET_HEADS)], axis=1)
        p = p_ref[...]
        w = POOL_WIDTH
        g1 = POOL_GROUP_DIM

        def prev(r, lo):
            return pool_ref[:, r * w + lo:(r + 1) * w]

        s2 = p + prev(14, 0)
        s4 = s2[:, g1:] + prev(13, g1) + prev(12, g1)
        s8 = s4[:, g1:] + prev(11, 2 * g1) + prev(10, 2 * g1) + prev(9, 2 * g1) + prev(8, 2 * g1)
        s16 = s8[:, g1:]
        for r in range(7, -1, -1):
            s16 = s16 + prev(r, 3 * g1)
        sums = (s2[:, :g1], s4[:, :g1], s8[:, :g1], s16)
        groups = [sums[t] * (1.0 / POOL_WINDOWS[t]) - p[:, t * g1:(t + 1) * g1] for t in range(POOL_GROUPS)]
        poolout_ref[:, 0:(POOL_BUF - 1) * w] = pool_ref[:, w:POOL_BUF * w]
        poolout_ref[:, (POOL_BUF - 1) * w:] = p
        yb = _pool_branch(groups, poolw_ref, pscale_ref, wpool_ref)
        h1 = _merge_tail(o_norm, g_ref[...], ga_ref[...], gb_ref[...], yb, x_ref[...], wret_ref, wout_ref)
        h1_ref[...] = h1
        _route(h1, nffn_ref, wrt_ref, br_ref, run_scr, xn2_ref, idx_ref, gate_ref, rank_ref, gatet_ref)
        cnt_ref[...] = jnp.broadcast_to(run_scr[...], cnt_ref.shape)


def _sample_mixer(sdec, qt, kt, proj, x2d, state, pool2d, wts, n_tok):
    q, k, v, g, p, ga, gb = proj
    nb = x2d.shape[0]
    assert (n_tok - nb) % nb == 0
    grp = SAMPLE_GROUP
    c2 = lambda i: (0, 0)

    def whole(a):
        return pl.BlockSpec(a.shape, c2)

    gn, poolw, pscale, wret, wpool, wout, nffn, wrt, br = wts
    in_arrays = [sdec, qt, kt, q, k, v, g, p, ga, gb, x2d, state, pool2d, gn,
                 poolw, pscale, wret, wpool, wout, nffn, wrt, br]
    in_specs = [pl.BlockSpec(memory_space=pltpu.SMEM),
                pl.BlockSpec((1, RET_QK, grp), lambda i: (i, 0, 0)),
                pl.BlockSpec((1, RET_QK, grp), lambda i: (i, 0, 0))]
    in_specs += [whole(a) for a in (q, k, v, g, p, ga, gb, x2d)]
    in_specs += [pl.BlockSpec((grp, RET_HEADS, RET_DK, RET_DV), lambda i: (i, 0, 0, 0)), whole(pool2d), whole(gn),
                 pl.BlockSpec(poolw.shape, lambda i: (0, 0, 0))]
    in_specs += [whole(a) for a in (pscale, wret, wpool, wout, nffn, wrt, br)]
    xn2_alias_idx = len(in_arrays)
    in_arrays.append(jnp.zeros((n_tok, D_MODEL), F32))
    in_specs.append(pl.BlockSpec(memory_space=pl.ANY))
    out_shape = [
        jax.ShapeDtypeStruct(state.shape, F32),
        jax.ShapeDtypeStruct(pool2d.shape, F32),
        jax.ShapeDtypeStruct((nb, D_MODEL), F32),
        jax.ShapeDtypeStruct((n_tok, D_MODEL), F32),
        jax.ShapeDtypeStruct((TOP_K, nb), jnp.int32),
        jax.ShapeDtypeStruct((TOP_K, nb), F32),
        jax.ShapeDtypeStruct((TOP_K, nb), jnp.int32),
        jax.ShapeDtypeStruct((nb, LANES), F32),
        jax.ShapeDtypeStruct((N_EXPERTS, LANES), F32),
    ]
    out_specs = [pl.BlockSpec((grp, RET_HEADS, RET_DK, RET_DV), lambda i: (i, 0, 0, 0))]
    out_specs += [pl.BlockSpec(s.shape, c2) for s in out_shape[1:]]
    out_specs[3] = pl.BlockSpec((nb, D_MODEL), lambda i: ((n_tok - nb) // nb, 0))
    return pl.pallas_call(
        _sample_kernel,
        grid=(nb // grp,),
        in_specs=in_specs,
        out_specs=out_specs,
        out_shape=out_shape,
        scratch_shapes=[pltpu.VMEM((nb, RET_V), F32), pltpu.VMEM((N_EXPERTS, 1), F32)],
        input_output_aliases={xn2_alias_idx: 3},
        compiler_params=pltpu.CompilerParams(
            dimension_semantics=("arbitrary",), vmem_limit_bytes=VMEM_LIMIT),
        name="sample_mixer",
    )(*in_arrays)


def _row_copy(src_ref, src_row, dst_ref, dst_row, sem):
    return pltpu.make_async_copy(src_ref.at[pl.ds(src_row, 1), :], dst_ref.at[pl.ds(dst_row, 1), :], sem)


def _invert_kernel(dest_ref, inv_ref, tok_ref, *, n_assign, n_tok):
    phase = pl.program_id(0)
    c = pl.program_id(1)
    n_chunks = pl.num_programs(1)

    @pl.when(phase == 0)
    def _():
        chunk = inv_ref.shape[0] // n_chunks

        def init(j, carry):
            r = c * chunk + j
            inv_ref[r] = n_assign + r
            tok_ref[r] = 0
            return carry

        lax.fori_loop(0, chunk, init, 0, unroll=8)

    @pl.when(phase > 0)
    def _():
        chunk = n_tok // n_chunks
        kk = phase - 1

        def fill(j, carry):
            t = c * chunk + j
            s = dest_ref[kk * n_tok + t] + SCATTER_LEAD
            inv_ref[s] = kk * n_tok + t
            tok_ref[s] = t
            return carry

        lax.fori_loop(0, chunk, fill, 0, unroll=8)


def _invert(dest_flat, n_inv, n_assign, n_tok):
    smem = pl.BlockSpec(memory_space=pltpu.SMEM)
    assert n_tok % INVERT_CHUNKS == 0 and n_inv % INVERT_CHUNKS == 0
    return pl.pallas_call(
        functools.partial(_invert_kernel, n_assign=n_assign, n_tok=n_tok),
        grid=(TOP_K + 1, INVERT_CHUNKS),
        in_specs=[smem],
        out_specs=[smem, smem],
        out_shape=[jax.ShapeDtypeStruct((n_inv,), jnp.int32)] * 2,
        compiler_params=pltpu.CompilerParams(dimension_semantics=("arbitrary", "arbitrary")),
        name="moe_invert",
    )(dest_flat)


def _expert_kernel(be_ref, nblk_ref, inv_ref, tok_ref, xall_ref, wgu_ref, bgu_ref, wd_ref, bd_ref, y4_ref,
                   xbuf0, xbuf1, ybuf0, ybuf1, wgu_bf, wd_bf, gsem, ssem, *, n_blocks):
    i = pl.program_id(0)
    nblk = nblk_ref[0]
    bm = MOE_BLOCK
    xbufs, ybufs = (xbuf0, xbuf1), (ybuf0, ybuf1)

    def start_gather(blk, xb, sem):
        base = (blk + 2) * bm
        for r in range(bm):
            _row_copy(xall_ref, tok_ref[base + r], xb, r, sem).start()

    def start_scatter(blk, yb, sem):
        base = (blk + 2) * bm
        for r in range(bm):
            _row_copy(yb, r, y4_ref, inv_ref[base + r], sem).start()

    def wait_block(buf, sem):
        pltpu.make_async_copy(xall_ref.at[pl.ds(0, bm), :], buf, sem).wait()

    @pl.when(i == 0)
    def _():
        ybuf0[...] = jnp.zeros_like(ybuf0)
        ybuf1[...] = jnp.zeros_like(ybuf1)
        start_gather(0, xbuf0, gsem.at[0])
        start_scatter(-2, ybuf0, ssem.at[0])

    new_expert = jnp.logical_or(i == 0, be_ref[i] != be_ref[jnp.maximum(i - 1, 0)])

    @pl.when(jnp.logical_and(i < nblk, new_expert))
    def _():
        def cast(c, carry):
            rows = pl.ds(pl.multiple_of(c * WEIGHT_CAST_ROWS, WEIGHT_CAST_ROWS), WEIGHT_CAST_ROWS)
            wgu_bf[rows, :] = wgu_ref[0, rows, :].astype(BF16)
            wd_bf[rows, :] = wd_ref[0, rows, :].astype(BF16)
            return carry

        lax.fori_loop(0, D_MODEL // WEIGHT_CAST_ROWS, cast, 0)

    for par in range(2):
        cur, oth = par, 1 - par

        @pl.when(jnp.logical_and(i < nblk, i % 2 == par))
        def _(cur=cur, oth=oth):
            wait_block(xbufs[cur], gsem.at[cur])
            wait_block(ybufs[cur], ssem.at[cur])
            start_gather(jnp.minimum(i + 1, n_blocks - 1), xbufs[oth], gsem.at[oth])
            start_scatter(i - 1, ybufs[oth], ssem.at[oth])
            h = _dot(xbufs[cur][...].astype(BF16), wgu_bf[...]) + bgu_ref[0]
            gate = jnp.minimum(h[:, :D_FF], SWIGLU_LIMIT)
            up = jnp.clip(h[:, D_FF:], -SWIGLU_LIMIT, SWIGLU_LIMIT)
            glu = gate * jax.nn.sigmoid(gate * SWIGLU_ALPHA)
            ybufs[cur][...] = _dot(((up + 1.0) * glu).astype(BF16), wd_bf[...]) + bd_ref[0]

        @pl.when(jnp.logical_and(i == nblk, i % 2 == par))
        def _(cur=cur, oth=oth):
            wait_block(xbufs[cur], gsem.at[cur])
            wait_block(ybufs[cur], ssem.at[cur])
            start_scatter(i - 1, ybufs[oth], ssem.at[oth])
            wait_block(ybufs[oth], ssem.at[oth])


def _experts(block_e, nblk, inv, tok, xall, wgu, bgu, wd, bd, n_blocks, n_y4_rows):
    wmap = lambda i, be, nb, iv, tk: (be[i], 0, 0)
    grid_spec = pltpu.PrefetchScalarGridSpec(
        num_scalar_prefetch=4,
        grid=(n_blocks + 1,),
        in_specs=[
            pl.BlockSpec(memory_space=pl.ANY),
            pl.BlockSpec((1, D_MODEL, 2 * D_FF), wmap),
            pl.BlockSpec((1, 1, 2 * D_FF), wmap),
            pl.BlockSpec((1, D_FF, D_MODEL), wmap),
            pl.BlockSpec((1, 1, D_MODEL), wmap),
        ],
        out_specs=pl.BlockSpec(memory_space=pl.ANY),
        scratch_shapes=[
            pltpu.VMEM((MOE_BLOCK, D_MODEL), F32), pltpu.VMEM((MOE_BLOCK, D_MODEL), F32),
            pltpu.VMEM((MOE_BLOCK, D_MODEL), F32), pltpu.VMEM((MOE_BLOCK, D_MODEL), F32),
            pltpu.VMEM((D_MODEL, 2 * D_FF), BF16), pltpu.VMEM((D_FF, D_MODEL), BF16),
            pltpu.SemaphoreType.DMA((2,)), pltpu.SemaphoreType.DMA((2,)),
        ],
    )
    return pl.pallas_call(
        functools.partial(_expert_kernel, n_blocks=n_blocks),
        grid_spec=grid_spec,
        out_shape=jax.ShapeDtypeStruct((n_y4_rows, D_MODEL), F32),
        compiler_params=pltpu.CompilerParams(
            dimension_semantics=("arbitrary",), vmem_limit_bytes=VMEM_LIMIT),
        name="moe_experts",
    )(block_e, nblk, inv, tok, xall, wgu, bgu, wd, bd)


def _combine_kernel(y0_ref, y1_ref, y2_ref, y3_ref, hp_ref, hs_ref, gp_ref, gs_ref, nf_ref, yp_ref, ysmp_ref):
    i = pl.program_id(0)
    n_prompt = pl.num_programs(0) - 1

    def finish(h_ref, gt_ref, out_ref):
        y = h_ref[...]
        gt = gt_ref[...]
        for kk, yk_ref in enumerate((y0_ref, y1_ref, y2_ref, y3_ref)):
            y = y + yk_ref[...] * gt[:, kk:kk + 1]
        out_ref[...] = _rmsnorm(y, nf_ref[...])

    @pl.when(i < n_prompt)
    def _():
        finish(hp_ref, gp_ref, yp_ref)

    @pl.when(i == n_prompt)
    def _():
        finish(hs_ref, gs_ref, ysmp_ref)


def _combine(y4, h1_p, h1_s, gt_p, gt_s, nf):
    n_prompt = h1_p.shape[0] // ROW_TILE
    pmap = lambda i: (jnp.minimum(i, n_prompt - 1), 0)
    smap = lambda i: (0, 0)
    return pl.pallas_call(
        _combine_kernel,
        grid=(n_prompt + 1,),
        in_specs=[pl.BlockSpec((ROW_TILE, D_MODEL), functools.partial(lambda i, kk: (kk * (n_prompt + 1) + i, 0), kk=kk))
                  for kk in range(TOP_K)] + [
            pl.BlockSpec((ROW_TILE, D_MODEL), pmap),
            pl.BlockSpec((ROW_TILE, D_MODEL), smap),
            pl.BlockSpec((ROW_TILE, LANES), pmap),
            pl.BlockSpec((ROW_TILE, LANES), smap),
            pl.BlockSpec((1, D_MODEL), smap),
        ],
        out_specs=[pl.BlockSpec((ROW_TILE, D_MODEL), pmap), pl.BlockSpec((ROW_TILE, D_MODEL), smap)],
        out_shape=[jax.ShapeDtypeStruct(h1_p.shape, F32), jax.ShapeDtypeStruct(h1_s.shape, F32)],
        compiler_params=pltpu.CompilerParams(dimension_semantics=("arbitrary",)),
        name="moe_combine",
    )(y4, y4, y4, y4, h1_p, h1_s, gt_p, gt_s, nf)


def _rotary_tables(pos):
    f = np.float32
    inv = np.power(f(ROPE_BASE), -np.arange(0, RET_DK, 2, dtype=f) / f(RET_DK)).astype(f)
    ang = (np.asarray(pos, f)[:, None] * inv[None, :]).astype(f)
    cos, sin = np.cos(ang).astype(f), np.sin(ang).astype(f)
    return np.concatenate([cos, cos], axis=1), np.concatenate([-sin, sin], axis=1)


def _decay_tables(chunk):
    f = np.float32
    log_g = np.log1p(-np.exp2(f(-5.0) - np.arange(RET_HEADS, dtype=f))).astype(f)
    i = np.arange(chunk, dtype=f)
    diff = i[:, None] - i[None, :]
    mask = np.where(diff[None] >= 0, np.exp(np.maximum(diff, f(0.0))[None] * log_g[:, None, None]), f(0.0)).astype(f)
    q_dec = np.exp((i + f(1.0))[None, :] * log_g[:, None]).astype(f)
    k_dec = np.exp((f(chunk) - f(1.0) - i)[None, :] * log_g[:, None]).astype(f)
    c_dec = np.exp(f(chunk) * log_g).astype(f)
    return mask, q_dec, k_dec, c_dec


def kernel(x_prompt, x_sample, state_ret, state_pool, meta_tokens, norm_mix, w_in, ret_gn, pool_w, pool_scale,
           w_ret_branch, w_pool_branch, w_out, norm_ffn, w_router, b_router, w_gate_up, b_gate_up, w_down, b_down,
           norm_final):
    batch, seq, _ = x_prompt.shape
    nb = x_sample.shape[0]
    past_len = 16384
    n_prompt_tok = batch * seq
    n_tok = n_prompt_tok + nb

    w_in_bf = w_in[0].astype(BF16)
    wts = (ret_gn[0][None, :], pool_w[0].astype(BF16), pool_scale[0][None, :],
           w_ret_branch[0].astype(BF16), w_pool_branch[0].astype(BF16), w_out[0].astype(BF16),
           norm_ffn[0][None, :], w_router[0].T.astype(BF16), b_router[0][:, None])
    wgu = w_gate_up[0]
    wd = w_down[0]
    bgu = b_gate_up[0][:, None, :]
    bd = b_down[0][:, None, :]
    nmix = norm_mix[0][None, :]

    cos_p, sin_p = _rotary_tables(N_META + np.arange(seq))
    cos_s, sin_s = _rotary_tables(np.concatenate([np.arange(N_META), np.full((nb,), past_len)]))
    mask, q_dec, k_dec, c_dec = _decay_tables(CHUNK)
    dec = (mask,
           np.ascontiguousarray(np.broadcast_to(q_dec[:, :, None], (RET_HEADS, CHUNK, RET_DK))),
           np.ascontiguousarray(np.broadcast_to(k_dec[:, :, None], (RET_HEADS, CHUNK, RET_DK))),
           np.ascontiguousarray(np.broadcast_to(c_dec[:, None, None], (RET_HEADS, 1, RET_DV))))
    m1, q1, k1, c1 = _decay_tables(1)
    sdec = np.stack([m1[:, 0, 0], q1[:, 0], k1[:, 0], c1], axis=1)

    x2d = x_prompt.reshape(n_prompt_tok, D_MODEL)
    proj_p = _inproj(x2d, nmix, w_in_bf, cos_p, sin_p, INPROJ_TILE, batch, BF16)
    xs2d = x_sample.reshape(nb, D_MODEL)
    x_small = jnp.concatenate([meta_tokens, xs2d], axis=0)
    proj_small = _inproj(x_small, nmix, w_in_bf, cos_s, sin_s, N_META + nb, 1, F32)
    proj_s = tuple(a[N_META:] for a in proj_small)

    lead = CHUNK - N_META
    kmeta = jnp.pad(proj_small[1][:N_META], ((lead, 0), (0, 0))).astype(BF16)
    vmeta = jnp.pad(proj_small[2][:N_META], ((lead, 0), (0, 0))).astype(BF16)
    pmeta = proj_small[4][:N_META]

    grp = SAMPLE_GROUP

    def cols(a):
        return a.T.reshape(RET_QK, nb // grp, grp).transpose(1, 0, 2)

    pool2d = state_pool[0].reshape(nb, POOL_BUF * POOL_WIDTH)
    (st_s, pool_s, h1_s, xn2_all, idx_s, gate_s, rank_s, gt_s, cnt_s) = _sample_mixer(
        sdec, cols(proj_s[0]), cols(proj_s[1]), proj_s, xs2d, state_ret[0], pool2d, wts, n_tok)

    (h1_p, xn2_all, idx_p, gate_p, rank_p, gt_p, cnt, s_fin, p_fin) = _mixer(
        proj_p, x2d, kmeta, vmeta, pmeta, dec, wts, cnt_s, xn2_all, batch, seq)

    counts = cnt[:, 0].astype(jnp.int32)
    padded = ((counts + MOE_BLOCK - 1) // MOE_BLOCK) * MOE_BLOCK
    pad_end = jnp.cumsum(padded)
    pad_start = pad_end - padded
    n_assign = n_tok * TOP_K
    n_blocks = n_assign // MOE_BLOCK + N_EXPERTS
    block_row = jnp.arange(n_blocks + 1, dtype=jnp.int32) * MOE_BLOCK
    block_e = jnp.minimum(jnp.sum((pad_end[None, :] <= block_row[:, None]).astype(jnp.int32), axis=1),
                          N_EXPERTS - 1)
    nblk = (pad_end[-1:] // MOE_BLOCK).astype(jnp.int32)
    idx = jnp.concatenate([idx_p, idx_s], axis=1)
    rank = jnp.concatenate([rank_p, rank_s], axis=1)
    e_ids = jnp.arange(N_EXPERTS, dtype=jnp.int32)[:, None, None]
    dest = rank + jnp.sum(jnp.where(idx[None] == e_ids, pad_start[:, None, None], 0), axis=0)

    n_inv = SCATTER_LEAD + n_blocks * MOE_BLOCK
    assert n_inv % SMEM_1D_TILE == 0 and n_tok % ROW_TILE == 0
    dest_flat = jnp.pad(dest.reshape(-1), (0, (-n_assign) % SMEM_1D_TILE))
    inv, tok = _invert(dest_flat, n_inv, n_assign, n_tok)
    y4 = _experts(block_e, nblk, inv, tok, xn2_all, wgu, bgu, wd, bd, n_blocks, n_assign + n_inv)
    y_p, y_s = _combine(y4, h1_p, h1_s, gt_p, gt_s, norm_final[None, :])

    y_prompt = y_p.reshape(batch, seq, D_MODEL)
    y_sample = y_s.reshape(nb, 1, D_MODEL)
    ret_state_prompt = s_fin[None]
    pool_state_prompt = p_fin[:, 1:, :][None]
    ret_state_sample = st_s[None]
    pool_state_sample = pool_s.reshape(nb, POOL_BUF, POOL_WIDTH)[None]
    return (y_prompt, y_sample, ret_state_prompt, pool_state_prompt, ret_state_sample, pool_state_sample)
```

```python
import functools

import jax
import jax.numpy as jnp
import numpy as np
from jax import lax
from jax.experimental import pallas as pl
from jax.experimental.pallas import tpu as pltpu

F32 = jnp.float32
BF16 = jnp.bfloat16

D_MODEL = 1024
N_META = 16
RET_HEADS = 4
RET_DK = 128
RET_DV = 256
RET_QK = RET_HEADS * RET_DK
RET_V = RET_HEADS * RET_DV
CHUNK = 128
ROPE_BASE = 10000.0
POOL_WINDOWS = (2, 4, 8, 16)
POOL_GROUPS = 4
POOL_GROUP_DIM = 128
POOL_WIDTH = POOL_GROUPS * POOL_GROUP_DIM
POOL_BUF = max(POOL_WINDOWS) - 1
N_EXPERTS = 32
TOP_K = 4
D_FF = D_MODEL
SWIGLU_LIMIT = 7.0
SWIGLU_ALPHA = 1.702
EPS = 1e-6
IN_WIDTHS = (RET_QK, RET_QK, RET_V, RET_V, POOL_WIDTH, D_MODEL, D_MODEL)
IN_TOTAL = sum(IN_WIDTHS)
IN_OFFS = tuple(int(s) for s in np.cumsum((0,) + IN_WIDTHS))

LANES = 128
ROW_CHUNKS = D_MODEL // LANES
INPROJ_TILE = 512
MIXER_TILE = 256
MOE_BLOCK = 256
ROW_TILE = 128
SAMPLE_GROUP = 8
WEIGHT_CAST_ROWS = 128
SCATTER_LEAD = 2 * MOE_BLOCK
SMEM_1D_TILE = 1024
INVERT_BLOCK = 5 * SMEM_1D_TILE
VMEM_LIMIT = 56 * 1024 * 1024

assert N_META + 1 >= max(POOL_WINDOWS)
assert POOL_WINDOWS == (2, 4, 8, 16)


def _dot(a, b):
    return jnp.dot(a, b, preferred_element_type=F32)


def _rmsnorm(x, w):
    return x * lax.rsqrt(jnp.mean(x * x, axis=-1, keepdims=True) + EPS) * w


def _store_rows_as_tiles(ref, x):
    rows = x.shape[0]
    for c in range(ROW_CHUNKS):
        ref[pl.ds(c, rows, stride=ROW_CHUNKS), :] = x[:, c * LANES:(c + 1) * LANES]


def _load_rows_from_tiles(ref, rows):
    return jnp.concatenate([ref[pl.ds(c, rows, stride=ROW_CHUNKS), :] for c in range(ROW_CHUNKS)], axis=1)


def _inproj_kernel(x_ref, nw_ref, w_ref, cos_ref, sin_ref,
                   q_ref, k_ref, v_ref, g_ref, p_ref, ga_ref, gb_ref):
    xn = _rmsnorm(x_ref[...], nw_ref[...]).astype(BF16)
    cos = cos_ref[...]
    sin = sin_ref[...]

    def seg(i):
        return _dot(xn, w_ref[:, IN_OFFS[i]:IN_OFFS[i + 1]])

    def rot(a):
        return a * cos + pltpu.roll(a, RET_DK // 2, 1) * sin

    q = seg(0)
    k = seg(1)
    for h in range(RET_HEADS):
        sl = slice(h * RET_DK, (h + 1) * RET_DK)
        q_ref[:, sl] = rot(q[:, sl]).astype(q_ref.dtype)
        k_ref[:, sl] = (rot(k[:, sl]) * (RET_DK ** -0.5)).astype(k_ref.dtype)
    v_ref[...] = seg(2).astype(v_ref.dtype)
    g_ref[...] = seg(3).astype(g_ref.dtype)
    p_ref[...] = seg(4)
    ga_ref[...] = seg(5).astype(ga_ref.dtype)
    gb_ref[...] = seg(6).astype(gb_ref.dtype)


def _inproj(x2d, nw, w_in_bf, cosf, sinf, tile, n_outer, act_dtype):
    rows = x2d.shape[0]
    n_inner = rows // (tile * n_outer)
    row_map = lambda b, j: (b * n_inner + j, 0)
    tab_map = lambda b, j: (j, 0)
    const = lambda b, j: (0, 0)
    widths = IN_WIDTHS
    dts = (act_dtype, act_dtype, act_dtype, act_dtype, F32, act_dtype, act_dtype)
    return pl.pallas_call(
        _inproj_kernel,
        grid=(n_outer, n_inner),
        in_specs=[
            pl.BlockSpec((tile, D_MODEL), row_map),
            pl.BlockSpec((1, D_MODEL), const),
            pl.BlockSpec((D_MODEL, IN_TOTAL), const, pipeline_mode=pl.Buffered(1)),
            pl.BlockSpec((tile, RET_DK), tab_map),
            pl.BlockSpec((tile, RET_DK), tab_map),
        ],
        out_specs=[pl.BlockSpec((tile, w), row_map) for w in widths],
        out_shape=[jax.ShapeDtypeStruct((rows, w), dt) for w, dt in zip(widths, dts)],
        compiler_params=pltpu.CompilerParams(
            dimension_semantics=("arbitrary", "arbitrary"), vmem_limit_bytes=VMEM_LIMIT),
        name="inproj",
    )(x2d, nw, w_in_bf, cosf, sinf)


def _group_norm(o, gn_row):
    mu = jnp.mean(o, axis=-1, keepdims=True)
    var = jnp.mean(jnp.square(o - mu), axis=-1, keepdims=True)
    return (o - mu) * lax.rsqrt(var + EPS) * gn_row


def _pool_branch(groups, poolw_ref, pscale_ref, wpool_ref):
    pm = [_dot(g.astype(BF16), poolw_ref[i]) for i, g in enumerate(groups)]
    pm = jnp.concatenate(pm, axis=1) * pscale_ref[...]
    return _dot(pm.astype(BF16), wpool_ref[...])


def _merge_tail(o_norm, g, ga, gb, yb, x, wret_ref, wout_ref):
    gf = g.astype(F32)
    ya = _dot((gf * jax.nn.sigmoid(gf) * o_norm).astype(BF16), wret_ref[...])
    merged = jax.nn.sigmoid(ga.astype(F32)) * ya + jax.nn.sigmoid(gb.astype(F32)) * yb
    return x + _dot(merged.astype(BF16), wout_ref[...])


def _route(h1, nffn_ref, wrt_ref, br_ref, run_scr,
           xn2_ref, idx_ref, gate_ref, rank_ref, gatet_ref):
    tm = h1.shape[0]
    xn2 = _rmsnorm(h1, nffn_ref[...])
    _store_rows_as_tiles(xn2_ref, xn2)
    logits = lax.dot_general(wrt_ref[...], xn2.astype(BF16), (((1,), (1,)), ((), ())),
                             preferred_element_type=F32) + br_ref[...]
    e_iota = lax.broadcasted_iota(jnp.int32, (N_EXPERTS, tm), 0)
    work = logits
    vals, sels = [], []
    chosen = jnp.zeros((N_EXPERTS, tm), F32)
    for _ in range(TOP_K):
        m = jnp.max(work, axis=0, keepdims=True)
        sel = jnp.min(jnp.where(work == m, e_iota, N_EXPERTS), axis=0, keepdims=True)
        hit = e_iota == sel
        vals.append(m)
        sels.append(sel)
        chosen = jnp.where(hit, 1.0, chosen)
        work = jnp.where(hit, -jnp.inf, work)
    exps = [jnp.exp(v - vals[0]) for v in vals]
    denom = exps[0] + exps[1] + exps[2] + exps[3]
    gates = [e / denom for e in exps]
    r_i = lax.broadcasted_iota(jnp.int32, (tm, tm), 0)
    c_i = lax.broadcasted_iota(jnp.int32, (tm, tm), 1)
    before = jnp.where(r_i < c_i, 1.0, 0.0).astype(BF16)
    base = run_scr[...] + _dot(chosen.astype(BF16), before)
    for kk in range(TOP_K):
        rk = jnp.sum(jnp.where(e_iota == sels[kk], base, 0.0), axis=0, keepdims=True)
        rank_ref[kk:kk + 1, :] = rk.astype(jnp.int32)
        idx_ref[kk:kk + 1, :] = sels[kk]
        gate_ref[kk:kk + 1, :] = gates[kk]
    run_scr[...] = run_scr[...] + jnp.sum(chosen, axis=1, keepdims=True)
    row = lax.broadcasted_iota(jnp.int32, (LANES, tm), 0)
    gpad = jnp.zeros((LANES, tm), F32)
    for kk in range(TOP_K):
        gpad = jnp.where(row == kk, gates[kk], gpad)
    for c in range(tm // LANES):
        gatet_ref[c * LANES:(c + 1) * LANES, :] = gpad[:, c * LANES:(c + 1) * LANES].T


def _mixer_kernel(q_ref, k_ref, v_ref, g_ref, p_ref, ga_ref, gb_ref, x_ref,
                  kmeta_ref, vmeta_ref, pmeta_ref, mask_ref, qdec_ref, kdec_ref, cdec_ref, gn_ref,
                  poolw_ref, pscale_ref, wret_ref, wpool_ref, wout_ref, nffn_ref, wrt_ref, br_ref,
                  cnt0_ref, xn2_all_ref,
                  h1_ref, xn2_ref, idx_ref, gate_ref, rank_ref, gatet_ref, cnt_ref, sfin_ref, pfin_ref,
                  s_scr, ext_scr, o_scr, run_scr):
    del xn2_all_ref
    b = pl.program_id(0)
    j = pl.program_id(1)
    nj = pl.num_programs(1)
    tm = q_ref.shape[0]

    def state_update(s_old, kc, vc, h):
        kd = (kc.astype(F32) * kdec_ref[h]).astype(BF16)
        upd = lax.dot_general(kd, vc, (((0,), (0,)), ((), ())), preferred_element_type=F32)
        return s_old * cdec_ref[h] + upd

    @pl.when(jnp.logical_and(b == 0, j == 0))
    def _():
        run_scr[...] = cnt0_ref[:, 0:1]

    @pl.when(j == 0)
    def _():
        for h in range(RET_HEADS):
            kc = kmeta_ref[:, h * RET_DK:(h + 1) * RET_DK]
            vc = vmeta_ref[:, h * RET_DV:(h + 1) * RET_DV]
            s_scr[h] = state_update(jnp.zeros((RET_DK, RET_DV), F32), kc, vc, h)
        ext_scr[0:N_META, :] = pmeta_ref[...]

    for c in range(tm // CHUNK):
        rows = slice(c * CHUNK, (c + 1) * CHUNK)
        for h in range(RET_HEADS):
            qc = q_ref[rows, h * RET_DK:(h + 1) * RET_DK]
            kc = k_ref[rows, h * RET_DK:(h + 1) * RET_DK]
            vc = v_ref[rows, h * RET_DV:(h + 1) * RET_DV]
            s_old = s_scr[h]
            scores = lax.dot_general(qc, kc, (((1,), (1,)), ((), ())),
                                     preferred_element_type=F32) * mask_ref[h]
            qd = (qc.astype(F32) * qdec_ref[h]).astype(BF16)
            lhs = jnp.concatenate([scores.astype(BF16), qd], axis=1)
            rhs = jnp.concatenate([vc, s_old.astype(BF16)], axis=0)
            o = _dot(lhs, rhs)
            s_scr[h] = state_update(s_old, kc, vc, h)
            o_scr[rows, h * RET_DV:(h + 1) * RET_DV] = _group_norm(
                o, gn_ref[:, h * RET_DV:(h + 1) * RET_DV])

    p = p_ref[...]
    ext_scr[N_META:N_META + tm, :] = p
    a = ext_scr[...]
    g1 = POOL_GROUP_DIM
    s2 = a + pltpu.roll(a, 1, 0)
    s4 = s2[:, g1:] + pltpu.roll(s2[:, g1:], 2, 0)
    s8 = s4[:, g1:] + pltpu.roll(s4[:, g1:], 4, 0)
    s16 = s8[:, g1:] + pltpu.roll(s8[:, g1:], 8, 0)
    sums = (s2[N_META:, :g1], s4[N_META:, :g1], s8[N_META:, :g1], s16[N_META:, :])
    groups = [sums[i] * (1.0 / POOL_WINDOWS[i]) - p[:, i * g1:(i + 1) * g1] for i in range(POOL_GROUPS)]
    ext_scr[0:N_META, :] = ext_scr[tm:tm + N_META, :]

    yb = _pool_branch(groups, poolw_ref, pscale_ref, wpool_ref)
    h1 = _merge_tail(o_scr[...], g_ref[...], ga_ref[...], gb_ref[...], yb, x_ref[...], wret_ref, wout_ref)
    h1_ref[...] = h1
    _route(h1, nffn_ref, wrt_ref, br_ref, run_scr, xn2_ref, idx_ref, gate_ref, rank_ref, gatet_ref)
    cnt_ref[...] = jnp.broadcast_to(run_scr[...], cnt_ref.shape)

    @pl.when(j == nj - 1)
    def _():
        for h in range(RET_HEADS):
            sfin_ref[0, h] = s_scr[h]
        pfin_ref[0] = ext_scr[0:N_META, :]


def _mixer(proj, x2d, kmeta, vmeta, pmeta, dec, wts, cnt0, xn2_all, batch, seq):
    q, k, v, g, p, ga, gb = proj
    tm = MIXER_TILE
    nj = seq // tm
    rows = batch * seq
    row_map = lambda b, j: (b * nj + j, 0)
    lane_map = lambda b, j: (0, b * nj + j)
    c2 = lambda b, j: (0, 0)
    c3 = lambda b, j: (0, 0, 0)

    def whole(a):
        return pl.BlockSpec(a.shape, c2 if a.ndim == 2 else c3)

    mask, qdec, kdec, cdec = dec
    gn, poolw, pscale, wret, wpool, wout, nffn, wrt, br = wts
    in_arrays = [q, k, v, g, p, ga, gb, x2d, kmeta, vmeta, pmeta, mask, qdec, kdec, cdec, gn,
                 poolw, pscale, wret, wpool, wout, nffn, wrt, br, cnt0]
    in_specs = [pl.BlockSpec((tm, a.shape[1]), row_map) for a in in_arrays[:8]]
    in_specs += [whole(a) for a in in_arrays[8:]]
    xn2_alias_idx = len(in_arrays)
    in_arrays.append(xn2_all)
    in_specs.append(pl.BlockSpec(memory_space=pl.ANY))
    out_shape = [
        jax.ShapeDtypeStruct((rows, D_MODEL), F32),
        jax.ShapeDtypeStruct(xn2_all.shape, F32),
        jax.ShapeDtypeStruct((TOP_K, rows), jnp.int32),
        jax.ShapeDtypeStruct((TOP_K, rows), F32),
        jax.ShapeDtypeStruct((TOP_K, rows), jnp.int32),
        jax.ShapeDtypeStruct((rows, LANES), F32),
        jax.ShapeDtypeStruct((N_EXPERTS, LANES), F32),
        jax.ShapeDtypeStruct((batch, RET_HEADS, RET_DK, RET_DV), F32),
        jax.ShapeDtypeStruct((batch, N_META, POOL_WIDTH), F32),
    ]
    out_specs = [
        pl.BlockSpec((tm, D_MODEL), row_map),
        pl.BlockSpec((tm * ROW_CHUNKS, LANES), row_map),
        pl.BlockSpec((TOP_K, tm), lane_map),
        pl.BlockSpec((TOP_K, tm), lane_map),
        pl.BlockSpec((TOP_K, tm), lane_map),
        pl.BlockSpec((tm, LANES), row_map),
        pl.BlockSpec((N_EXPERTS, LANES), c2),
        pl.BlockSpec((1, RET_HEADS, RET_DK, RET_DV), lambda b, j: (b, 0, 0, 0)),
        pl.BlockSpec((1, N_META, POOL_WIDTH), lambda b, j: (b, 0, 0)),
    ]
    return pl.pallas_call(
        _mixer_kernel,
        grid=(batch, nj),
        in_specs=in_specs,
        out_specs=out_specs,
        out_shape=out_shape,
        scratch_shapes=[
            pltpu.VMEM((RET_HEADS, RET_DK, RET_DV), F32),
            pltpu.VMEM((N_META + tm, POOL_WIDTH), F32),
            pltpu.VMEM((tm, RET_V), F32),
            pltpu.VMEM((N_EXPERTS, 1), F32),
        ],
        input_output_aliases={xn2_alias_idx: 1},
        compiler_params=pltpu.CompilerParams(
            dimension_semantics=("arbitrary", "arbitrary"), vmem_limit_bytes=VMEM_LIMIT),
        name="mixer",
    )(*in_arrays)


def _sample_kernel(sdec_ref, qt_ref, kt_ref, q_ref, k_ref, v_ref, g_ref, p_ref, ga_ref, gb_ref, x_ref,
                   st_ref, pool_ref, gn_ref,
                   poolw_ref, pscale_ref, wret_ref, wpool_ref, wout_ref, nffn_ref, wrt_ref, br_ref, xn2_init_ref,
                   stout_ref, poolout_ref, h1_ref, xn2_ref, idx_ref, gate_ref, rank_ref, gatet_ref,
                   cnt_ref,
                   o_scr, run_scr):
    del xn2_init_ref
    i = pl.program_id(0)
    n = pl.num_programs(0)
    grp = st_ref.shape[0]
    row0 = pl.multiple_of(i * grp, grp)

    @pl.when(i == 0)
    def _():
        run_scr[...] = jnp.zeros_like(run_scr)

    q8 = q_ref[pl.ds(row0, grp), :]
    k8 = k_ref[pl.ds(row0, grp), :]
    v8 = v_ref[pl.ds(row0, grp), :]
    for h in range(RET_HEADS):
        ksl = slice(h * RET_DK, (h + 1) * RET_DK)
        vsl = slice(h * RET_DV, (h + 1) * RET_DV)
        score = jnp.sum(q8[:, ksl] * k8[:, ksl], axis=1, keepdims=True) * sdec_ref[h, 0]
        intra = score * v8[:, vsl]
        for bb in range(grp):
            s_old = st_ref[bb, h]
            qcol = qt_ref[0, ksl, bb:bb + 1] * sdec_ref[h, 1]
            kcol = kt_ref[0, ksl, bb:bb + 1] * sdec_ref[h, 2]
            cross = jnp.sum(s_old * qcol, axis=0, keepdims=True)
            o_scr[pl.ds(row0 + bb, 1), vsl] = intra[bb:bb + 1, :] + cross
            stout_ref[bb, h] = s_old * sdec_ref[h, 3] + kcol * v8[bb:bb + 1, vsl]

    @pl.when(i == n - 1)
    def _():
        o = o_scr[...]
        o_norm = jnp.concatenate(
            [_group_norm(o[:, h * RET_DV:(h + 1) * RET_DV], gn_ref[:, h * RET_DV:(h + 1) * RET_DV])
             for h in range(RET_HEADS)], axis=1)
        p = p_ref[...]
        w = POOL_WIDTH
        g1 = POOL_GROUP_DIM

        def prev(r, lo):
            return pool_ref[:, r * w + lo:(r + 1) * w]

        s2 = p + prev(14, 0)
        s4 = s2[:, g1:] + prev(13, g1) + prev(12, g1)
        s8 = s4[:, g1:] + prev(11, 2 * g1) + prev(10, 2 * g1) + prev(9, 2 * g1) + prev(8, 2 * g1)
        s16 = s8[:, g1:]
        for r in range(7, -1, -1):
            s16 = s16 + prev(r, 3 * g1)
        sums = (s2[:, :g1], s4[:, :g1], s8[:, :g1], s16)
        groups = [sums[t] * (1.0 / POOL_WINDOWS[t]) - p[:, t * g1:(t + 1) * g1] for t in range(POOL_GROUPS)]
        poolout_ref[:, 0:(POOL_BUF - 1) * w] = pool_ref[:, w:POOL_BUF * w]
        poolout_ref[:, (POOL_BUF - 1) * w:] = p
        yb = _pool_branch(groups, poolw_ref, pscale_ref, wpool_ref)
        h1 = _merge_tail(o_norm, g_ref[...], ga_ref[...], gb_ref[...], yb, x_ref[...], wret_ref, wout_ref)
        h1_ref[...] = h1
        _route(h1, nffn_ref, wrt_ref, br_ref, run_scr, xn2_ref, idx_ref, gate_ref, rank_ref, gatet_ref)
        cnt_ref[...] = jnp.broadcast_to(run_scr[...], cnt_ref.shape)


def _sample_mixer(sdec, qt, kt, proj, x2d, state, pool2d, wts, n_tok):
    q, k, v, g, p, ga, gb = proj
    nb = x2d.shape[0]
    assert (n_tok - nb) % nb == 0
    grp = SAMPLE_GROUP
    c2 = lambda i: (0, 0)

    def whole(a):
        return pl.BlockSpec(a.shape, c2)

    gn, poolw, pscale, wret, wpool, wout, nffn, wrt, br = wts
    in_arrays = [sdec, qt, kt, q, k, v, g, p, ga, gb, x2d, state, pool2d, gn,
                 poolw, pscale, wret, wpool, wout, nffn, wrt, br]
    in_specs = [pl.BlockSpec(memory_space=pltpu.SMEM),
                pl.BlockSpec((1, RET_QK, grp), lambda i: (i, 0, 0)),
                pl.BlockSpec((1, RET_QK, grp), lambda i: (i, 0, 0))]
    in_specs += [whole(a) for a in (q, k, v, g, p, ga, gb, x2d)]
    in_specs += [pl.BlockSpec((grp, RET_HEADS, RET_DK, RET_DV), lambda i: (i, 0, 0, 0)), whole(pool2d), whole(gn),
                 pl.BlockSpec(poolw.shape, lambda i: (0, 0, 0))]
    in_specs += [whole(a) for a in (pscale, wret, wpool, wout, nffn, wrt, br)]
    xn2_alias_idx = len(in_arrays)
    in_arrays.append(jnp.zeros((n_tok * ROW_CHUNKS, LANES), F32))
    in_specs.append(pl.BlockSpec(memory_space=pl.ANY))
    out_shape = [
        jax.ShapeDtypeStruct(state.shape, F32),
        jax.ShapeDtypeStruct(pool2d.shape, F32),
        jax.ShapeDtypeStruct((nb, D_MODEL), F32),
        jax.ShapeDtypeStruct((n_tok * ROW_CHUNKS, LANES), F32),
        jax.ShapeDtypeStruct((TOP_K, nb), jnp.int32),
        jax.ShapeDtypeStruct((TOP_K, nb), F32),
        jax.ShapeDtypeStruct((TOP_K, nb), jnp.int32),
        jax.ShapeDtypeStruct((nb, LANES), F32),
        jax.ShapeDtypeStruct((N_EXPERTS, LANES), F32),
    ]
    out_specs = [pl.BlockSpec((grp, RET_HEADS, RET_DK, RET_DV), lambda i: (i, 0, 0, 0))]
    out_specs += [pl.BlockSpec(s.shape, c2) for s in out_shape[1:]]
    out_specs[3] = pl.BlockSpec((nb * ROW_CHUNKS, LANES), lambda i: ((n_tok - nb) // nb, 0))
    return pl.pallas_call(
        _sample_kernel,
        grid=(nb // grp,),
        in_specs=in_specs,
        out_specs=out_specs,
        out_shape=out_shape,
        scratch_shapes=[pltpu.VMEM((nb, RET_V), F32), pltpu.VMEM((N_EXPERTS, 1), F32)],
        input_output_aliases={xn2_alias_idx: 3},
        compiler_params=pltpu.CompilerParams(
            dimension_semantics=("arbitrary",), vmem_limit_bytes=VMEM_LIMIT),
        name="sample_mixer",
    )(*in_arrays)


def _tile_copy(src_ref, src_row, dst_ref, dst_row, sem):
    def tile(ref, row):
        start = row * ROW_CHUNKS
        if not isinstance(start, int):
            start = pl.multiple_of(start, ROW_CHUNKS)
        return ref.at[pl.ds(start, ROW_CHUNKS), :]

    return pltpu.make_async_copy(tile(src_ref, src_row), tile(dst_ref, dst_row), sem)


def _invert_kernel(dest_ref, inv0_ref, tok0_ref, inv_ref, tok_ref, *, n_tok):
    g = pl.program_id(0)

    @pl.when(g == 0)
    def _():
        pltpu.sync_copy(inv0_ref, inv_ref)
        pltpu.sync_copy(tok0_ref, tok_ref)

    per_step = dest_ref.shape[0] // TOP_K

    def fill(j, carry):
        t = g * per_step + j
        for kk in range(TOP_K):
            s = dest_ref[j * TOP_K + kk] + SCATTER_LEAD
            inv_ref[s] = kk * n_tok + t
            tok_ref[s] = t
        return carry

    lax.fori_loop(0, per_step, fill, 0, unroll=2)


def _invert(dest_tk, n_inv, n_assign, n_tok):
    smem = pl.BlockSpec(memory_space=pltpu.SMEM)
    hbm = pl.BlockSpec(memory_space=pl.ANY)
    assert dest_tk.shape[0] % INVERT_BLOCK == 0
    n_tab = n_inv + SMEM_1D_TILE
    inv0 = n_assign + jnp.arange(n_tab, dtype=jnp.int32)
    tok0 = jnp.zeros((n_tab,), jnp.int32)
    return pl.pallas_call(
        functools.partial(_invert_kernel, n_tok=n_tok),
        grid=(dest_tk.shape[0] // INVERT_BLOCK,),
        in_specs=[pl.BlockSpec((INVERT_BLOCK,), lambda g: (g,), memory_space=pltpu.SMEM), hbm, hbm],
        out_specs=[smem, smem],
        out_shape=[jax.ShapeDtypeStruct((n_tab,), jnp.int32)] * 2,
        compiler_params=pltpu.CompilerParams(dimension_semantics=("arbitrary",)),
        name="moe_invert",
    )(dest_tk, inv0, tok0)


def _expert_kernel(be_ref, nblk_ref, inv_ref, tok_ref, xall_ref, wgu_ref, bgu_ref, wd_ref, bd_ref, y4_ref,
                   xbuf0, xbuf1, ybuf0, ybuf1, wgu_bf, wd_bf, gsem, ssem, *, n_blocks):
    i = pl.program_id(0)
    nblk = nblk_ref[0]
    bm = MOE_BLOCK
    xbufs, ybufs = (xbuf0, xbuf1), (ybuf0, ybuf1)

    def start_gather(blk, xb, sem):
        base = (blk + 2) * bm
        for r in range(bm):
            _tile_copy(xall_ref, tok_ref[base + r], xb, r, sem).start(priority=r % 2)

    def start_scatter(blk, yb, sem):
        base = (blk + 2) * bm
        for r in range(bm):
            _tile_copy(yb, r, y4_ref, inv_ref[base + r], sem).start(priority=r % 2)

    def wait_block(buf, sem):
        pltpu.make_async_copy(xall_ref.at[pl.ds(0, bm * ROW_CHUNKS), :], buf, sem).wait()

    @pl.when(i == 0)
    def _():
        ybuf0[...] = jnp.zeros_like(ybuf0)
        ybuf1[...] = jnp.zeros_like(ybuf1)
        start_gather(0, xbuf0, gsem.at[0])
        start_scatter(-2, ybuf0, ssem.at[0])

    new_expert = jnp.logical_or(i == 0, be_ref[i] != be_ref[jnp.maximum(i - 1, 0)])

    @pl.when(jnp.logical_and(i < nblk, new_expert))
    def _():
        def cast(c, carry):
            rows = pl.ds(pl.multiple_of(c * WEIGHT_CAST_ROWS, WEIGHT_CAST_ROWS), WEIGHT_CAST_ROWS)
            wgu_bf[rows, :] = wgu_ref[0, rows, :].astype(BF16)
            wd_bf[rows, :] = wd_ref[0, rows, :].astype(BF16)
            return carry

        lax.fori_loop(0, D_MODEL // WEIGHT_CAST_ROWS, cast, 0)

    for par in range(2):
        cur, oth = par, 1 - par

        @pl.when(jnp.logical_and(i < nblk, i % 2 == par))
        def _(cur=cur, oth=oth):
            wait_block(xbufs[cur], gsem.at[cur])
            wait_block(ybufs[cur], ssem.at[cur])
            start_gather(jnp.minimum(i + 1, n_blocks - 1), xbufs[oth], gsem.at[oth])
            start_scatter(i - 1, ybufs[oth], ssem.at[oth])
            x = _load_rows_from_tiles(xbufs[cur], bm)
            h = _dot(x.astype(BF16), wgu_bf[...]) + bgu_ref[0]
            gate = jnp.minimum(h[:, :D_FF], SWIGLU_LIMIT)
            up = jnp.clip(h[:, D_FF:], -SWIGLU_LIMIT, SWIGLU_LIMIT)
            glu = gate * jax.nn.sigmoid(gate * SWIGLU_ALPHA)
            y = _dot(((up + 1.0) * glu).astype(BF16), wd_bf[...]) + bd_ref[0]
            _store_rows_as_tiles(ybufs[cur], y)

        @pl.when(jnp.logical_and(i == nblk, i % 2 == par))
        def _(cur=cur, oth=oth):
            wait_block(xbufs[cur], gsem.at[cur])
            wait_block(ybufs[cur], ssem.at[cur])
            start_scatter(i - 1, ybufs[oth], ssem.at[oth])
            wait_block(ybufs[oth], ssem.at[oth])


def _experts(block_e, nblk, inv, tok, xall, wgu, bgu, wd, bd, n_blocks, n_y4_rows):
    wmap = lambda i, be, nb, iv, tk: (be[i], 0, 0)
    grid_spec = pltpu.PrefetchScalarGridSpec(
        num_scalar_prefetch=4,
        grid=(n_blocks + 1,),
        in_specs=[
            pl.BlockSpec(memory_space=pl.ANY),
            pl.BlockSpec((1, D_MODEL, 2 * D_FF), wmap),
            pl.BlockSpec((1, 1, 2 * D_FF), wmap),
            pl.BlockSpec((1, D_FF, D_MODEL), wmap),
            pl.BlockSpec((1, 1, D_MODEL), wmap),
        ],
        out_specs=pl.BlockSpec(memory_space=pl.ANY),
        scratch_shapes=[pltpu.VMEM((MOE_BLOCK * ROW_CHUNKS, LANES), F32)] * 4 + [
            pltpu.VMEM((D_MODEL, 2 * D_FF), BF16), pltpu.VMEM((D_FF, D_MODEL), BF16),
            pltpu.SemaphoreType.DMA((2,)), pltpu.SemaphoreType.DMA((2,)),
        ],
    )
    return pl.pallas_call(
        functools.partial(_expert_kernel, n_blocks=n_blocks),
        grid_spec=grid_spec,
        out_shape=jax.ShapeDtypeStruct((n_y4_rows * ROW_CHUNKS, LANES), F32),
        compiler_params=pltpu.CompilerParams(
            dimension_semantics=("arbitrary",), vmem_limit_bytes=VMEM_LIMIT),
        name="moe_experts",
    )(block_e, nblk, inv, tok, xall, wgu, bgu, wd, bd)


def _combine_kernel(y0_ref, y1_ref, y2_ref, y3_ref, hp_ref, hs_ref, gp_ref, gs_ref, nf_ref, yp_ref, ysmp_ref):
    i = pl.program_id(0)
    n_prompt = pl.num_programs(0) - 1

    def finish(h_ref, gt_ref, out_ref):
        y = h_ref[...]
        gt = gt_ref[...]
        for kk, yk_ref in enumerate((y0_ref, y1_ref, y2_ref, y3_ref)):
            y = y + _load_rows_from_tiles(yk_ref, ROW_TILE) * gt[:, kk:kk + 1]
        out_ref[...] = _rmsnorm(y, nf_ref[...])

    @pl.when(i < n_prompt)
    def _():
        finish(hp_ref, gp_ref, yp_ref)

    @pl.when(i == n_prompt)
    def _():
        finish(hs_ref, gs_ref, ysmp_ref)


def _combine(y4, h1_p, h1_s, gt_p, gt_s, nf):
    n_prompt = h1_p.shape[0] // ROW_TILE
    pmap = lambda i: (jnp.minimum(i, n_prompt - 1), 0)
    smap = lambda i: (0, 0)
    return pl.pallas_call(
        _combine_kernel,
        grid=(n_prompt + 1,),
        in_specs=[pl.BlockSpec((ROW_TILE * ROW_CHUNKS, LANES),
                               functools.partial(lambda i, kk: (kk * (n_prompt + 1) + i, 0), kk=kk))
                  for kk in range(TOP_K)] + [
            pl.BlockSpec((ROW_TILE, D_MODEL), pmap),
            pl.BlockSpec((ROW_TILE, D_MODEL), smap),
            pl.BlockSpec((ROW_TILE, LANES), pmap),
            pl.BlockSpec((ROW_TILE, LANES), smap),
            pl.BlockSpec((1, D_MODEL), smap),
        ],
        out_specs=[pl.BlockSpec((ROW_TILE, D_MODEL), pmap), pl.BlockSpec((ROW_TILE, D_MODEL), smap)],
        out_shape=[jax.ShapeDtypeStruct(h1_p.shape, F32), jax.ShapeDtypeStruct(h1_s.shape, F32)],
        compiler_params=pltpu.CompilerParams(dimension_semantics=("arbitrary",)),
        name="moe_combine",
    )(y4, y4, y4, y4, h1_p, h1_s, gt_p, gt_s, nf)


def _rotary_tables(pos):
    f = np.float32
    inv = np.power(f(ROPE_BASE), -np.arange(0, RET_DK, 2, dtype=f) / f(RET_DK)).astype(f)
    ang = (np.asarray(pos, f)[:, None] * inv[None, :]).astype(f)
    cos, sin = np.cos(ang).astype(f), np.sin(ang).astype(f)
    return np.concatenate([cos, cos], axis=1), np.concatenate([-sin, sin], axis=1)


def _decay_tables(chunk):
    f = np.float32
    log_g = np.log1p(-np.exp2(f(-5.0) - np.arange(RET_HEADS, dtype=f))).astype(f)
    i = np.arange(chunk, dtype=f)
    diff = i[:, None] - i[None, :]
    mask = np.where(diff[None] >= 0, np.exp(np.maximum(diff, f(0.0))[None] * log_g[:, None, None]), f(0.0)).astype(f)
    q_dec = np.exp((i + f(1.0))[None, :] * log_g[:, None]).astype(f)
    k_dec = np.exp((f(chunk) - f(1.0) - i)[None, :] * log_g[:, None]).astype(f)
    c_dec = np.exp(f(chunk) * log_g).astype(f)
    return mask, q_dec, k_dec, c_dec


def kernel(x_prompt, x_sample, state_ret, state_pool, meta_tokens, norm_mix, w_in, ret_gn, pool_w, pool_scale,
           w_ret_branch, w_pool_branch, w_out, norm_ffn, w_router, b_router, w_gate_up, b_gate_up, w_down, b_down,
           norm_final):
    batch, seq, _ = x_prompt.shape
    nb = x_sample.shape[0]
    past_len = 16384
    n_prompt_tok = batch * seq
    n_tok = n_prompt_tok + nb

    w_in_bf = w_in[0].astype(BF16)
    wts = (ret_gn[0][None, :], pool_w[0].astype(BF16), pool_scale[0][None, :],
           w_ret_branch[0].astype(BF16), w_pool_branch[0].astype(BF16), w_out[0].astype(BF16),
           norm_ffn[0][None, :], w_router[0].T.astype(BF16), b_router[0][:, None])
    wgu = w_gate_up[0]
    wd = w_down[0]
    bgu = b_gate_up[0][:, None, :]
    bd = b_down[0][:, None, :]
    nmix = norm_mix[0][None, :]

    cos_p, sin_p = _rotary_tables(N_META + np.arange(seq))
    cos_s, sin_s = _rotary_tables(np.concatenate([np.arange(N_META), np.full((nb,), past_len)]))
    mask, q_dec, k_dec, c_dec = _decay_tables(CHUNK)
    dec = (mask,
           np.ascontiguousarray(np.broadcast_to(q_dec[:, :, None], (RET_HEADS, CHUNK, RET_DK))),
           np.ascontiguousarray(np.broadcast_to(k_dec[:, :, None], (RET_HEADS, CHUNK, RET_DK))),
           np.ascontiguousarray(np.broadcast_to(c_dec[:, None, None], (RET_HEADS, 1, RET_DV))))
    m1, q1, k1, c1 = _decay_tables(1)
    sdec = np.stack([m1[:, 0, 0], q1[:, 0], k1[:, 0], c1], axis=1)

    x2d = x_prompt.reshape(n_prompt_tok, D_MODEL)
    proj_p = _inproj(x2d, nmix, w_in_bf, cos_p, sin_p, INPROJ_TILE, batch, BF16)
    xs2d = x_sample.reshape(nb, D_MODEL)
    x_small = jnp.concatenate([meta_tokens, xs2d], axis=0)
    proj_small = _inproj(x_small, nmix, w_in_bf, cos_s, sin_s, N_META + nb, 1, F32)
    proj_s = tuple(a[N_META:] for a in proj_small)

    lead = CHUNK - N_META
    kmeta = jnp.pad(proj_small[1][:N_META], ((lead, 0), (0, 0))).astype(BF16)
    vmeta = jnp.pad(proj_small[2][:N_META], ((lead, 0), (0, 0))).astype(BF16)
    pmeta = proj_small[4][:N_META]

    grp = SAMPLE_GROUP

    def cols(a):
        return a.T.reshape(RET_QK, nb // grp, grp).transpose(1, 0, 2)

    pool2d = state_pool[0].reshape(nb, POOL_BUF * POOL_WIDTH)
    (st_s, pool_s, h1_s, xn2_all, idx_s, gate_s, rank_s, gt_s, cnt_s) = _sample_mixer(
        sdec, cols(proj_s[0]), cols(proj_s[1]), proj_s, xs2d, state_ret[0], pool2d, wts, n_tok)

    (h1_p, xn2_all, idx_p, gate_p, rank_p, gt_p, cnt, s_fin, p_fin) = _mixer(
        proj_p, x2d, kmeta, vmeta, pmeta, dec, wts, cnt_s, xn2_all, batch, seq)

    counts = cnt[:, 0].astype(jnp.int32)
    padded = ((counts + MOE_BLOCK - 1) // MOE_BLOCK) * MOE_BLOCK
    pad_end = jnp.cumsum(padded)
    pad_start = pad_end - padded
    n_assign = n_tok * TOP_K
    n_blocks = n_assign // MOE_BLOCK + N_EXPERTS
    block_row = jnp.arange(n_blocks + 1, dtype=jnp.int32) * MOE_BLOCK
    block_e = jnp.minimum(jnp.sum((pad_end[None, :] <= block_row[:, None]).astype(jnp.int32), axis=1),
                          N_EXPERTS - 1)
    nblk = (pad_end[-1:] // MOE_BLOCK).astype(jnp.int32)
    idx = jnp.concatenate([idx_p, idx_s], axis=1)
    rank = jnp.concatenate([rank_p, rank_s], axis=1)
    e_ids = jnp.arange(N_EXPERTS, dtype=jnp.int32)[:, None, None]
    dest = rank + jnp.sum(jnp.where(idx[None] == e_ids, pad_start[:, None, None], 0), axis=0)

    n_inv = SCATTER_LEAD + n_blocks * MOE_BLOCK
    assert n_inv % SMEM_1D_TILE == 0 and n_tok % ROW_TILE == 0
    dest_tk = jnp.pad(dest.T.reshape(-1), (0, (-n_assign) % INVERT_BLOCK), constant_values=n_inv - SCATTER_LEAD)
    inv, tok = _invert(dest_tk, n_inv, n_assign, n_tok)
    y4 = _experts(block_e, nblk, inv, tok, xn2_all, wgu, bgu, wd, bd, n_blocks, n_assign + n_inv)
    y_p, y_s = _combine(y4, h1_p, h1_s, gt_p, gt_s, norm_final[None, :])

    y_prompt = y_p.reshape(batch, seq, D_MODEL)
    y_sample = y_s.reshape(nb, 1, D_MODEL)
    ret_state_prompt = s_fin[None]
    pool_state_prompt = p_fin[:, 1:, :][None]
    ret_state_sample = st_s[None]
    pool_state_sample = pool_s.reshape(nb, POOL_BUF, POOL_WIDTH)[None]
    return (y_prompt, y_sample, ret_state_prompt, pool_state_prompt, ret_state_sample, pool_state_sample)
```

```python
import functools

import jax
import jax.numpy as jnp
import numpy as np
from jax import lax
from jax.experimental import pallas as pl
from jax.experimental.pallas import tpu as pltpu

F32 = jnp.float32
BF16 = jnp.bfloat16

D_MODEL = 1024
N_META = 16
RET_HEADS = 4
RET_DK = 128
RET_DV = 256
RET_QK = RET_HEADS * RET_DK
RET_V = RET_HEADS * RET_DV
CHUNK = 128
ROPE_BASE = 10000.0
POOL_WINDOWS = (2, 4, 8, 16)
POOL_GROUPS = 4
POOL_GROUP_DIM = 128
POOL_WIDTH = POOL_GROUPS * POOL_GROUP_DIM
POOL_BUF = max(POOL_WINDOWS) - 1
N_EXPERTS = 32
TOP_K = 4
D_FF = D_MODEL
SWIGLU_LIMIT = 7.0
SWIGLU_ALPHA = 1.702
EPS = 1e-6
IN_WIDTHS = (RET_QK, RET_QK, RET_V, RET_V, POOL_WIDTH, D_MODEL, D_MODEL)
IN_TOTAL = sum(IN_WIDTHS)
IN_OFFS = tuple(int(s) for s in np.cumsum((0,) + IN_WIDTHS))

LANES = 128
ROW_CHUNKS = D_MODEL // LANES
INPROJ_TILE = 512
MIXER_TILE = 256
MOE_BLOCK = 256
ROW_TILE = 128
SAMPLE_GROUP = 8
WEIGHT_CAST_ROWS = 128
VMEM_LIMIT = 56 * 1024 * 1024

assert N_META + 1 >= max(POOL_WINDOWS)
assert POOL_WINDOWS == (2, 4, 8, 16)


def _dot(a, b):
    return jnp.dot(a, b, preferred_element_type=F32)


def _rmsnorm(x, w):
    return x * lax.rsqrt(jnp.mean(x * x, axis=-1, keepdims=True) + EPS) * w


def _store_rows_as_tiles(ref, x):
    rows = x.shape[0]
    for c in range(ROW_CHUNKS):
        ref[pl.ds(c, rows, stride=ROW_CHUNKS), :] = x[:, c * LANES:(c + 1) * LANES]


def _load_rows_from_tiles(ref, rows):
    return jnp.concatenate([ref[pl.ds(c, rows, stride=ROW_CHUNKS), :] for c in range(ROW_CHUNKS)], axis=1)


def _inproj_kernel(x_ref, nw_ref, w_ref, cos_ref, sin_ref,
                   q_ref, k_ref, v_ref, g_ref, p_ref, ga_ref, gb_ref):
    xn = _rmsnorm(x_ref[...], nw_ref[...]).astype(BF16)
    cos = cos_ref[...]
    sin = sin_ref[...]

    def seg(i):
        return _dot(xn, w_ref[:, IN_OFFS[i]:IN_OFFS[i + 1]])

    def rot(a):
        return a * cos + pltpu.roll(a, RET_DK // 2, 1) * sin

    q = seg(0)
    k = seg(1)
    for h in range(RET_HEADS):
        sl = slice(h * RET_DK, (h + 1) * RET_DK)
        q_ref[:, sl] = rot(q[:, sl]).astype(q_ref.dtype)
        k_ref[:, sl] = (rot(k[:, sl]) * (RET_DK ** -0.5)).astype(k_ref.dtype)
    v_ref[...] = seg(2).astype(v_ref.dtype)
    g_ref[...] = seg(3).astype(g_ref.dtype)
    p_ref[...] = seg(4)
    ga_ref[...] = seg(5).astype(ga_ref.dtype)
    gb_ref[...] = seg(6).astype(gb_ref.dtype)


def _inproj(x2d, nw, w_in_bf, cosf, sinf, tile, n_outer, act_dtype):
    rows = x2d.shape[0]
    n_inner = rows // (tile * n_outer)
    row_map = lambda b, j: (b * n_inner + j, 0)
    tab_map = lambda b, j: (j, 0)
    const = lambda b, j: (0, 0)
    widths = IN_WIDTHS
    dts = (act_dtype, act_dtype, act_dtype, act_dtype, F32, act_dtype, act_dtype)
    return pl.pallas_call(
        _inproj_kernel,
        grid=(n_outer, n_inner),
        in_specs=[
            pl.BlockSpec((tile, D_MODEL), row_map),
            pl.BlockSpec((1, D_MODEL), const),
            pl.BlockSpec((D_MODEL, IN_TOTAL), const, pipeline_mode=pl.Buffered(1)),
            pl.BlockSpec((tile, RET_DK), tab_map),
            pl.BlockSpec((tile, RET_DK), tab_map),
        ],
        out_specs=[pl.BlockSpec((tile, w), row_map) for w in widths],
        out_shape=[jax.ShapeDtypeStruct((rows, w), dt) for w, dt in zip(widths, dts)],
        compiler_params=pltpu.CompilerParams(
            dimension_semantics=("arbitrary", "arbitrary"), vmem_limit_bytes=VMEM_LIMIT),
        name="inproj",
    )(x2d, nw, w_in_bf, cosf, sinf)


def _group_norm(o, gn_row):
    mu = jnp.mean(o, axis=-1, keepdims=True)
    var = jnp.mean(jnp.square(o - mu), axis=-1, keepdims=True)
    return (o - mu) * lax.rsqrt(var + EPS) * gn_row


def _pool_branch(groups, poolw_ref, pscale_ref, wpool_ref):
    pm = [_dot(g.astype(BF16), poolw_ref[i]) for i, g in enumerate(groups)]
    pm = jnp.concatenate(pm, axis=1) * pscale_ref[...]
    return _dot(pm.astype(BF16), wpool_ref[...])


def _merge_tail(o_norm, g, ga, gb, yb, x, wret_ref, wout_ref):
    gf = g.astype(F32)
    ya = _dot((gf * jax.nn.sigmoid(gf) * o_norm).astype(BF16), wret_ref[...])
    merged = jax.nn.sigmoid(ga.astype(F32)) * ya + jax.nn.sigmoid(gb.astype(F32)) * yb
    return x + _dot(merged.astype(BF16), wout_ref[...])


def _route(h1, nffn_ref, wrt_ref, br_ref, run_scr,
           xn2_ref, idx_ref, gate_ref, rank_ref):
    tm = h1.shape[0]
    xn2 = _rmsnorm(h1, nffn_ref[...])
    xn2_ref[...] = xn2
    logits = lax.dot_general(wrt_ref[...], xn2.astype(BF16), (((1,), (1,)), ((), ())),
                             preferred_element_type=F32) + br_ref[...]
    e_iota = lax.broadcasted_iota(jnp.int32, (N_EXPERTS, tm), 0)
    work = logits
    vals, sels = [], []
    chosen = jnp.zeros((N_EXPERTS, tm), F32)
    for _ in range(TOP_K):
        m = jnp.max(work, axis=0, keepdims=True)
        sel = jnp.min(jnp.where(work == m, e_iota, N_EXPERTS), axis=0, keepdims=True)
        hit = e_iota == sel
        vals.append(m)
        sels.append(sel)
        chosen = jnp.where(hit, 1.0, chosen)
        work = jnp.where(hit, -jnp.inf, work)
    exps = [jnp.exp(v - vals[0]) for v in vals]
    denom = exps[0] + exps[1] + exps[2] + exps[3]
    gates = [e / denom for e in exps]
    r_i = lax.broadcasted_iota(jnp.int32, (tm, tm), 0)
    c_i = lax.broadcasted_iota(jnp.int32, (tm, tm), 1)
    before = jnp.where(r_i < c_i, 1.0, 0.0).astype(BF16)
    base = run_scr[...] + _dot(chosen.astype(BF16), before)
    for kk in range(TOP_K):
        rk = jnp.sum(jnp.where(e_iota == sels[kk], base, 0.0), axis=0, keepdims=True)
        rank_ref[kk:kk + 1, :] = rk.astype(jnp.int32)
        idx_ref[kk:kk + 1, :] = sels[kk]
        gate_ref[kk:kk + 1, :] = gates[kk]
    run_scr[...] = run_scr[...] + jnp.sum(chosen, axis=1, keepdims=True)


def _mixer_kernel(q_ref, k_ref, v_ref, g_ref, p_ref, ga_ref, gb_ref, x_ref,
                  kmeta_ref, vmeta_ref, pmeta_ref, mask_ref, qdec_ref, kdec_ref, cdec_ref, gn_ref,
                  poolw_ref, pscale_ref, wret_ref, wpool_ref, wout_ref, nffn_ref, wrt_ref, br_ref,
                  cnt0_ref,
                  h1_ref, xn2_ref, idx_ref, gate_ref, rank_ref, cnt_ref, sfin_ref, pfin_ref,
                  s_scr, ext_scr, o_scr, run_scr):
    b = pl.program_id(0)
    j = pl.program_id(1)
    nj = pl.num_programs(1)
    tm = q_ref.shape[0]

    def state_update(s_old, kc, vc, h):
        kd = (kc.astype(F32) * kdec_ref[h]).astype(BF16)
        upd = lax.dot_general(kd, vc, (((0,), (0,)), ((), ())), preferred_element_type=F32)
        return s_old * cdec_ref[h] + upd

    @pl.when(jnp.logical_and(b == 0, j == 0))
    def _():
        run_scr[...] = cnt0_ref[:, 0:1]

    @pl.when(j == 0)
    def _():
        for h in range(RET_HEADS):
            kc = kmeta_ref[:, h * RET_DK:(h + 1) * RET_DK]
            vc = vmeta_ref[:, h * RET_DV:(h + 1) * RET_DV]
            s_scr[h] = state_update(jnp.zeros((RET_DK, RET_DV), F32), kc, vc, h)
        ext_scr[0:N_META, :] = pmeta_ref[...]

    for c in range(tm // CHUNK):
        rows = slice(c * CHUNK, (c + 1) * CHUNK)
        for h in range(RET_HEADS):
            qc = q_ref[rows, h * RET_DK:(h + 1) * RET_DK]
            kc = k_ref[rows, h * RET_DK:(h + 1) * RET_DK]
            vc = v_ref[rows, h * RET_DV:(h + 1) * RET_DV]
            s_old = s_scr[h]
            scores = lax.dot_general(qc, kc, (((1,), (1,)), ((), ())),
                                     preferred_element_type=F32) * mask_ref[h]
            qd = (qc.astype(F32) * qdec_ref[h]).astype(BF16)
            lhs = jnp.concatenate([scores.astype(BF16), qd], axis=1)
            rhs = jnp.concatenate([vc, s_old.astype(BF16)], axis=0)
            o = _dot(lhs, rhs)
            s_scr[h] = state_update(s_old, kc, vc, h)
            o_scr[rows, h * RET_DV:(h + 1) * RET_DV] = _group_norm(
                o, gn_ref[:, h * RET_DV:(h + 1) * RET_DV])

    p = p_ref[...]
    ext_scr[N_META:N_META + tm, :] = p
    a = ext_scr[...]
    g1 = POOL_GROUP_DIM
    s2 = a + pltpu.roll(a, 1, 0)
    s4 = s2[:, g1:] + pltpu.roll(s2[:, g1:], 2, 0)
    s8 = s4[:, g1:] + pltpu.roll(s4[:, g1:], 4, 0)
    s16 = s8[:, g1:] + pltpu.roll(s8[:, g1:], 8, 0)
    sums = (s2[N_META:, :g1], s4[N_META:, :g1], s8[N_META:, :g1], s16[N_META:, :])
    groups = [sums[i] * (1.0 / POOL_WINDOWS[i]) - p[:, i * g1:(i + 1) * g1] for i in range(POOL_GROUPS)]
    ext_scr[0:N_META, :] = ext_scr[tm:tm + N_META, :]

    yb = _pool_branch(groups, poolw_ref, pscale_ref, wpool_ref)
    h1 = _merge_tail(o_scr[...], g_ref[...], ga_ref[...], gb_ref[...], yb, x_ref[...], wret_ref, wout_ref)
    h1_ref[...] = h1
    _route(h1, nffn_ref, wrt_ref, br_ref, run_scr, xn2_ref, idx_ref, gate_ref, rank_ref)
    cnt_ref[...] = jnp.broadcast_to(run_scr[...], cnt_ref.shape)

    @pl.when(j == nj - 1)
    def _():
        for h in range(RET_HEADS):
            sfin_ref[0, h] = s_scr[h]
        pfin_ref[0] = ext_scr[0:N_META, :]


def _mixer(proj, x2d, kmeta, vmeta, pmeta, dec, wts, cnt0, batch, seq):
    q, k, v, g, p, ga, gb = proj
    tm = MIXER_TILE
    nj = seq // tm
    rows = batch * seq
    row_map = lambda b, j: (b * nj + j, 0)
    lane_map = lambda b, j: (0, b * nj + j)
    c2 = lambda b, j: (0, 0)
    c3 = lambda b, j: (0, 0, 0)

    def whole(a):
        return pl.BlockSpec(a.shape, c2 if a.ndim == 2 else c3)

    mask, qdec, kdec, cdec = dec
    gn, poolw, pscale, wret, wpool, wout, nffn, wrt, br = wts
    in_arrays = [q, k, v, g, p, ga, gb, x2d, kmeta, vmeta, pmeta, mask, qdec, kdec, cdec, gn,
                 poolw, pscale, wret, wpool, wout, nffn, wrt, br, cnt0]
    in_specs = [pl.BlockSpec((tm, a.shape[1]), row_map) for a in in_arrays[:8]]
    in_specs += [whole(a) for a in in_arrays[8:]]
    out_shape = [
        jax.ShapeDtypeStruct((rows, D_MODEL), F32),
        jax.ShapeDtypeStruct((rows, D_MODEL), F32),
        jax.ShapeDtypeStruct((TOP_K, rows), jnp.int32),
        jax.ShapeDtypeStruct((TOP_K, rows), F32),
        jax.ShapeDtypeStruct((TOP_K, rows), jnp.int32),
        jax.ShapeDtypeStruct((N_EXPERTS, LANES), F32),
        jax.ShapeDtypeStruct((batch, RET_HEADS, RET_DK, RET_DV), F32),
        jax.ShapeDtypeStruct((batch, N_META, POOL_WIDTH), F32),
    ]
    out_specs = [
        pl.BlockSpec((tm, D_MODEL), row_map),
        pl.BlockSpec((tm, D_MODEL), row_map),
        pl.BlockSpec((TOP_K, tm), lane_map),
        pl.BlockSpec((TOP_K, tm), lane_map),
        pl.BlockSpec((TOP_K, tm), lane_map),
        pl.BlockSpec((N_EXPERTS, LANES), c2),
        pl.BlockSpec((1, RET_HEADS, RET_DK, RET_DV), lambda b, j: (b, 0, 0, 0)),
        pl.BlockSpec((1, N_META, POOL_WIDTH), lambda b, j: (b, 0, 0)),
    ]
    return pl.pallas_call(
        _mixer_kernel,
        grid=(batch, nj),
        in_specs=in_specs,
        out_specs=out_specs,
        out_shape=out_shape,
        scratch_shapes=[
            pltpu.VMEM((RET_HEADS, RET_DK, RET_DV), F32),
            pltpu.VMEM((N_META + tm, POOL_WIDTH), F32),
            pltpu.VMEM((tm, RET_V), F32),
            pltpu.VMEM((N_EXPERTS, 1), F32),
        ],
        compiler_params=pltpu.CompilerParams(
            dimension_semantics=("arbitrary", "arbitrary"), vmem_limit_bytes=VMEM_LIMIT),
        name="mixer",
    )(*in_arrays)


def _sample_kernel(sdec_ref, qt_ref, kt_ref, q_ref, k_ref, v_ref, g_ref, p_ref, ga_ref, gb_ref, x_ref,
                   st_ref, pool_ref, gn_ref,
                   poolw_ref, pscale_ref, wret_ref, wpool_ref, wout_ref, nffn_ref, wrt_ref, br_ref,
                   stout_ref, poolout_ref, h1_ref, xn2_ref, idx_ref, gate_ref, rank_ref,
                   cnt_ref,
                   o_scr, run_scr):
    i = pl.program_id(0)
    n = pl.num_programs(0)
    grp = st_ref.shape[0]
    row0 = pl.multiple_of(i * grp, grp)

    @pl.when(i == 0)
    def _():
        run_scr[...] = jnp.zeros_like(run_scr)

    q8 = q_ref[pl.ds(row0, grp), :]
    k8 = k_ref[pl.ds(row0, grp), :]
    v8 = v_ref[pl.ds(row0, grp), :]
    for h in range(RET_HEADS):
        ksl = slice(h * RET_DK, (h + 1) * RET_DK)
        vsl = slice(h * RET_DV, (h + 1) * RET_DV)
        score = jnp.sum(q8[:, ksl] * k8[:, ksl], axis=1, keepdims=True) * sdec_ref[h, 0]
        intra = score * v8[:, vsl]
        for bb in range(grp):
            s_old = st_ref[bb, h]
            qcol = qt_ref[0, ksl, bb:bb + 1] * sdec_ref[h, 1]
            kcol = kt_ref[0, ksl, bb:bb + 1] * sdec_ref[h, 2]
            cross = jnp.sum(s_old * qcol, axis=0, keepdims=True)
            o_scr[pl.ds(row0 + bb, 1), vsl] = intra[bb:bb + 1, :] + cross
            stout_ref[bb, h] = s_old * sdec_ref[h, 3] + kcol * v8[bb:bb + 1, vsl]

    @pl.when(i == n - 1)
    def _():
        o = o_scr[...]
        o_norm = jnp.concatenate(
            [_group_norm(o[:, h * RET_DV:(h + 1) * RET_DV], gn_ref[:, h * RET_DV:(h + 1) * RET_DV])
             for h in range(RET_HEADS)], axis=1)
        p = p_ref[...]
        w = POOL_WIDTH
        g1 = POOL_GROUP_DIM

        def prev(r, lo):
            return pool_ref[:, r * w + lo:(r + 1) * w]

        s2 = p + prev(14, 0)
        s4 = s2[:, g1:] + prev(13, g1) + prev(12, g1)
        s8 = s4[:, g1:] + prev(11, 2 * g1) + prev(10, 2 * g1) + prev(9, 2 * g1) + prev(8, 2 * g1)
        s16 = s8[:, g1:]
        for r in range(7, -1, -1):
            s16 = s16 + prev(r, 3 * g1)
        sums = (s2[:, :g1], s4[:, :g1], s8[:, :g1], s16)
        groups = [sums[t] * (1.0 / POOL_WINDOWS[t]) - p[:, t * g1:(t + 1) * g1] for t in range(POOL_GROUPS)]
        poolout_ref[:, 0:(POOL_BUF - 1) * w] = pool_ref[:, w:POOL_BUF * w]
        poolout_ref[:, (POOL_BUF - 1) * w:] = p
        yb = _pool_branch(groups, poolw_ref, pscale_ref, wpool_ref)
        h1 = _merge_tail(o_norm, g_ref[...], ga_ref[...], gb_ref[...], yb, x_ref[...], wret_ref, wout_ref)
        h1_ref[...] = h1
        _route(h1, nffn_ref, wrt_ref, br_ref, run_scr, xn2_ref, idx_ref, gate_ref, rank_ref)
        cnt_ref[...] = jnp.broadcast_to(run_scr[...], cnt_ref.shape)


def _sample_mixer(sdec, qt, kt, proj, x2d, state, pool2d, wts):
    q, k, v, g, p, ga, gb = proj
    nb = x2d.shape[0]
    grp = SAMPLE_GROUP
    c2 = lambda i: (0, 0)

    def whole(a):
        return pl.BlockSpec(a.shape, c2)

    gn, poolw, pscale, wret, wpool, wout, nffn, wrt, br = wts
    in_arrays = [sdec, qt, kt, q, k, v, g, p, ga, gb, x2d, state, pool2d, gn,
                 poolw, pscale, wret, wpool, wout, nffn, wrt, br]
    in_specs = [pl.BlockSpec(memory_space=pltpu.SMEM),
                pl.BlockSpec((1, RET_QK, grp), lambda i: (i, 0, 0)),
                pl.BlockSpec((1, RET_QK, grp), lambda i: (i, 0, 0))]
    in_specs += [whole(a) for a in (q, k, v, g, p, ga, gb, x2d)]
    in_specs += [pl.BlockSpec((grp, RET_HEADS, RET_DK, RET_DV), lambda i: (i, 0, 0, 0)), whole(pool2d), whole(gn),
                 pl.BlockSpec(poolw.shape, lambda i: (0, 0, 0))]
    in_specs += [whole(a) for a in (pscale, wret, wpool, wout, nffn, wrt, br)]
    out_shape = [
        jax.ShapeDtypeStruct(state.shape, F32),
        jax.ShapeDtypeStruct(pool2d.shape, F32),
        jax.ShapeDtypeStruct((nb, D_MODEL), F32),
        jax.ShapeDtypeStruct((nb, D_MODEL), F32),
        jax.ShapeDtypeStruct((TOP_K, nb), jnp.int32),
        jax.ShapeDtypeStruct((TOP_K, nb), F32),
        jax.ShapeDtypeStruct((TOP_K, nb), jnp.int32),
        jax.ShapeDtypeStruct((N_EXPERTS, LANES), F32),
    ]
    out_specs = [pl.BlockSpec((grp, RET_HEADS, RET_DK, RET_DV), lambda i: (i, 0, 0, 0))]
    out_specs += [pl.BlockSpec(s.shape, c2) for s in out_shape[1:]]
    return pl.pallas_call(
        _sample_kernel,
        grid=(nb // grp,),
        in_specs=in_specs,
        out_specs=out_specs,
        out_shape=out_shape,
        scratch_shapes=[pltpu.VMEM((nb, RET_V), F32), pltpu.VMEM((N_EXPERTS, 1), F32)],
        compiler_params=pltpu.CompilerParams(
            dimension_semantics=("arbitrary",), vmem_limit_bytes=VMEM_LIMIT),
        name="sample_mixer",
    )(*in_arrays)


TILE_ROWS = ROW_TILE * TOP_K


def _run_copy(src_ref, src_row, dst_ref, dst_row, n_rows, sem):
    def rows(ref, row):
        return ref.at[pl.ds(pl.multiple_of(row * ROW_CHUNKS, ROW_CHUNKS), n_rows * ROW_CHUNKS), :]

    return pltpu.make_async_copy(rows(src_ref, src_row), rows(dst_ref, dst_row), sem)


def _for_each_run(tcnt_ref, tile, fn):
    def body(e, off):
        n = tcnt_ref[tile * N_EXPERTS + e]

        @pl.when(n > 0)
        def _():
            fn(e, off, n)

        return off + n

    lax.fori_loop(0, N_EXPERTS, body, 0)


def _dispatch_kernel(tcnt_ref, tdst_ref, zrow_ref, zcnt_ref, nblk_ref, pos_ref, xp_ref, xs_ref, out_ref,
                     sorted_scr, zero_scr, sem, zsem):
    i = pl.program_id(0)
    n_prompt = pl.num_programs(0) - 1
    n_blocks = out_ref.shape[0] // (MOE_BLOCK * ROW_CHUNKS)

    def wait_tile():
        pltpu.make_async_copy(out_ref.at[pl.ds(0, TILE_ROWS * ROW_CHUNKS), :], sorted_scr, sem).wait()

    def for_each_pad(fn):
        def body(e, carry):
            n = zcnt_ref[e]

            @pl.when(n > 0)
            def _():
                fn(_run_copy(zero_scr, 0, out_ref, zrow_ref[e], n, zsem))

            dead = nblk_ref[0] + e

            @pl.when(dead < n_blocks)
            def _():
                fn(_run_copy(zero_scr, 0, out_ref, dead * MOE_BLOCK, MOE_BLOCK, zsem))

            return carry

        lax.fori_loop(0, N_EXPERTS, body, 0)

    @pl.when(i == 0)
    def _():
        zero_scr[...] = jnp.zeros_like(zero_scr)
        for_each_pad(lambda cp: cp.start())

    def sort_tile(x_ref):
        r_iota = lax.broadcasted_iota(jnp.int32, (TILE_ROWS, ROW_TILE), 0)
        hit = r_iota == pos_ref[0:1, :]
        for kk in range(1, TOP_K):
            hit = jnp.logical_or(hit, r_iota == pos_ref[kk:kk + 1, :])
        perm = jnp.where(hit, 1.0, 0.0).astype(BF16)
        xs = _dot(perm, x_ref[...].astype(BF16))

        @pl.when(i > 0)
        def _():
            wait_tile()

        _store_rows_as_tiles(sorted_scr, xs)

    @pl.when(i < n_prompt)
    def _():
        sort_tile(xp_ref)

    @pl.when(i == n_prompt)
    def _():
        sort_tile(xs_ref)

    _for_each_run(tcnt_ref, i, lambda e, off, n: _run_copy(
        sorted_scr, off, out_ref, tdst_ref[i * N_EXPERTS + e], n, sem).start())

    @pl.when(i == n_prompt)
    def _():
        wait_tile()
        for_each_pad(lambda cp: cp.wait())


def _dispatch(tcnt, tdst, zrow, zcnt, nblk, pos, xn2_p, xn2_s, n_sorted):
    n_prompt = xn2_p.shape[0] // ROW_TILE
    grid_spec = pltpu.PrefetchScalarGridSpec(
        num_scalar_prefetch=5,
        grid=(n_prompt + 1,),
        in_specs=[
            pl.BlockSpec((TOP_K, ROW_TILE), lambda i, *_: (0, i)),
            pl.BlockSpec((ROW_TILE, D_MODEL), lambda i, *_: (jnp.minimum(i, n_prompt - 1), 0)),
            pl.BlockSpec((ROW_TILE, D_MODEL), lambda i, *_: (0, 0)),
        ],
        out_specs=pl.BlockSpec(memory_space=pl.ANY),
        scratch_shapes=[
            pltpu.VMEM((TILE_ROWS * ROW_CHUNKS, LANES), F32),
            pltpu.VMEM((MOE_BLOCK * ROW_CHUNKS, LANES), F32),
            pltpu.SemaphoreType.DMA(()), pltpu.SemaphoreType.DMA(()),
        ],
    )
    return pl.pallas_call(
        _dispatch_kernel,
        grid_spec=grid_spec,
        out_shape=jax.ShapeDtypeStruct((n_sorted * ROW_CHUNKS, LANES), F32),
        compiler_params=pltpu.CompilerParams(dimension_semantics=("arbitrary",)),
        name="moe_dispatch",
    )(tcnt, tdst, zrow, zcnt, nblk, pos, xn2_p, xn2_s)


def _expert_kernel(be_ref, nblk_ref, x_ref, wgu_ref, bgu_ref, wd_ref, bd_ref, y_ref, wgu_bf, wd_bf):
    i = pl.program_id(0)
    live = i < nblk_ref[0]
    new_expert = jnp.logical_or(i == 0, be_ref[i] != be_ref[jnp.maximum(i - 1, 0)])

    @pl.when(jnp.logical_and(live, new_expert))
    def _():
        def cast(c, carry):
            rows = pl.ds(pl.multiple_of(c * WEIGHT_CAST_ROWS, WEIGHT_CAST_ROWS), WEIGHT_CAST_ROWS)
            wgu_bf[rows, :] = wgu_ref[0, rows, :].astype(BF16)
            wd_bf[rows, :] = wd_ref[0, rows, :].astype(BF16)
            return carry

        lax.fori_loop(0, D_MODEL // WEIGHT_CAST_ROWS, cast, 0)

    @pl.when(jnp.logical_not(live))
    def _():
        y_ref[...] = jnp.zeros_like(y_ref)

    @pl.when(live)
    def _():
        x = _load_rows_from_tiles(x_ref, MOE_BLOCK)
        h = _dot(x.astype(BF16), wgu_bf[...]) + bgu_ref[0]
        gate = jnp.minimum(h[:, :D_FF], SWIGLU_LIMIT)
        up = jnp.clip(h[:, D_FF:], -SWIGLU_LIMIT, SWIGLU_LIMIT)
        glu = gate * jax.nn.sigmoid(gate * SWIGLU_ALPHA)
        y = _dot(((up + 1.0) * glu).astype(BF16), wd_bf[...]) + bd_ref[0]
        _store_rows_as_tiles(y_ref, y)


def _experts(block_e, nblk, x_sorted, wgu, bgu, wd, bd):
    n_blocks = x_sorted.shape[0] // (MOE_BLOCK * ROW_CHUNKS)
    wmap = lambda i, be, nb: (be[i], 0, 0)
    rmap = lambda i, be, nb: (i, 0)
    grid_spec = pltpu.PrefetchScalarGridSpec(
        num_scalar_prefetch=2,
        grid=(n_blocks,),
        in_specs=[
            pl.BlockSpec((MOE_BLOCK * ROW_CHUNKS, LANES), rmap),
            pl.BlockSpec((1, D_MODEL, 2 * D_FF), wmap),
            pl.BlockSpec((1, 1, 2 * D_FF), wmap),
            pl.BlockSpec((1, D_FF, D_MODEL), wmap),
            pl.BlockSpec((1, 1, D_MODEL), wmap),
        ],
        out_specs=pl.BlockSpec((MOE_BLOCK * ROW_CHUNKS, LANES), rmap),
        scratch_shapes=[pltpu.VMEM((D_MODEL, 2 * D_FF), BF16), pltpu.VMEM((D_FF, D_MODEL), BF16)],
    )
    return pl.pallas_call(
        _expert_kernel,
        grid_spec=grid_spec,
        out_shape=jax.ShapeDtypeStruct(x_sorted.shape, F32),
        compiler_params=pltpu.CompilerParams(
            dimension_semantics=("arbitrary",), vmem_limit_bytes=VMEM_LIMIT),
        name="moe_experts",
    )(block_e, nblk, x_sorted, wgu, bgu, wd, bd)


def _combine_kernel(tcnt_ref, tdst_ref, pos_ref, gate_ref, ys_ref, hp_ref, hs_ref, nf_ref, yp_ref, ysmp_ref,
                    runs_scr, sem):
    i = pl.program_id(0)
    n_tiles = pl.num_programs(0)
    n_prompt = n_tiles - 1
    slot = i % 2

    def start_runs(tile, s):
        _for_each_run(tcnt_ref, tile, lambda e, off, n: _run_copy(
            ys_ref, tdst_ref[tile * N_EXPERTS + e], runs_scr.at[s], off, n, sem.at[s]).start())

    @pl.when(i == 0)
    def _():
        start_runs(0, 0)

    @pl.when(i + 1 < n_tiles)
    def _():
        start_runs(i + 1, 1 - slot)

    pltpu.make_async_copy(ys_ref.at[pl.ds(0, TILE_ROWS * ROW_CHUNKS), :], runs_scr.at[slot], sem.at[slot]).wait()
    ys = _load_rows_from_tiles(runs_scr.at[slot], TILE_ROWS).astype(BF16)
    r_iota = lax.broadcasted_iota(jnp.int32, (TILE_ROWS, ROW_TILE), 0)
    gmat = jnp.zeros((TILE_ROWS, ROW_TILE), F32)
    for kk in range(TOP_K):
        gmat = jnp.where(r_iota == pos_ref[kk:kk + 1, :], gate_ref[kk:kk + 1, :], gmat)
    g_hi = gmat.astype(BF16)
    g_lo = (gmat - g_hi.astype(F32)).astype(BF16)
    contract0 = (((0,), (0,)), ((), ()))
    moe = (lax.dot_general(g_hi, ys, contract0, preferred_element_type=F32)
           + lax.dot_general(g_lo, ys, contract0, preferred_element_type=F32))

    @pl.when(i < n_prompt)
    def _():
        yp_ref[...] = _rmsnorm(hp_ref[...] + moe, nf_ref[...])

    @pl.when(i == n_prompt)
    def _():
        ysmp_ref[...] = _rmsnorm(hs_ref[...] + moe, nf_ref[...])


def _combine(tcnt, tdst, pos, gates, y_sorted, h1_p, h1_s, nf):
    n_prompt = h1_p.shape[0] // ROW_TILE
    pmap = lambda i, *_: (jnp.minimum(i, n_prompt - 1), 0)
    smap = lambda i, *_: (0, 0)
    lmap = lambda i, *_: (0, i)
    grid_spec = pltpu.PrefetchScalarGridSpec(
        num_scalar_prefetch=2,
        grid=(n_prompt + 1,),
        in_specs=[
            pl.BlockSpec((TOP_K, ROW_TILE), lmap),
            pl.BlockSpec((TOP_K, ROW_TILE), lmap),
            pl.BlockSpec(memory_space=pl.ANY),
            pl.BlockSpec((ROW_TILE, D_MODEL), pmap),
            pl.BlockSpec((ROW_TILE, D_MODEL), smap),
            pl.BlockSpec((1, D_MODEL), smap),
        ],
        out_specs=[pl.BlockSpec((ROW_TILE, D_MODEL), pmap), pl.BlockSpec((ROW_TILE, D_MODEL), smap)],
        scratch_shapes=[pltpu.VMEM((2, TILE_ROWS * ROW_CHUNKS, LANES), F32), pltpu.SemaphoreType.DMA((2,))],
    )
    return pl.pallas_call(
        _combine_kernel,
        grid_spec=grid_spec,
        out_shape=[jax.ShapeDtypeStruct(h1_p.shape, F32), jax.ShapeDtypeStruct(h1_s.shape, F32)],
        compiler_params=pltpu.CompilerParams(dimension_semantics=("arbitrary",)),
        name="moe_combine",
    )(tcnt, tdst, pos, gates, y_sorted, h1_p, h1_s, nf)


def _rotary_tables(pos):
    f = np.float32
    inv = np.power(f(ROPE_BASE), -np.arange(0, RET_DK, 2, dtype=f) / f(RET_DK)).astype(f)
    ang = (np.asarray(pos, f)[:, None] * inv[None, :]).astype(f)
    cos, sin = np.cos(ang).astype(f), np.sin(ang).astype(f)
    return np.concatenate([cos, cos], axis=1), np.concatenate([-sin, sin], axis=1)


def _decay_tables(chunk):
    f = np.float32
    log_g = np.log1p(-np.exp2(f(-5.0) - np.arange(RET_HEADS, dtype=f))).astype(f)
    i = np.arange(chunk, dtype=f)
    diff = i[:, None] - i[None, :]
    mask = np.where(diff[None] >= 0, np.exp(np.maximum(diff, f(0.0))[None] * log_g[:, None, None]), f(0.0)).astype(f)
    q_dec = np.exp((i + f(1.0))[None, :] * log_g[:, None]).astype(f)
    k_dec = np.exp((f(chunk) - f(1.0) - i)[None, :] * log_g[:, None]).astype(f)
    c_dec = np.exp(f(chunk) * log_g).astype(f)
    return mask, q_dec, k_dec, c_dec


def kernel(x_prompt, x_sample, state_ret, state_pool, meta_tokens, norm_mix, w_in, ret_gn, pool_w, pool_scale,
           w_ret_branch, w_pool_branch, w_out, norm_ffn, w_router, b_router, w_gate_up, b_gate_up, w_down, b_down,
           norm_final):
    batch, seq, _ = x_prompt.shape
    nb = x_sample.shape[0]
    past_len = 16384
    n_prompt_tok = batch * seq
    n_tok = n_prompt_tok + nb

    w_in_bf = w_in[0].astype(BF16)
    wts = (ret_gn[0][None, :], pool_w[0].astype(BF16), pool_scale[0][None, :],
           w_ret_branch[0].astype(BF16), w_pool_branch[0].astype(BF16), w_out[0].astype(BF16),
           norm_ffn[0][None, :], w_router[0].T.astype(BF16), b_router[0][:, None])
    wgu = w_gate_up[0]
    wd = w_down[0]
    bgu = b_gate_up[0][:, None, :]
    bd = b_down[0][:, None, :]
    nmix = norm_mix[0][None, :]

    cos_p, sin_p = _rotary_tables(N_META + np.arange(seq))
    cos_s, sin_s = _rotary_tables(np.concatenate([np.arange(N_META), np.full((nb,), past_len)]))
    mask, q_dec, k_dec, c_dec = _decay_tables(CHUNK)
    dec = (mask,
           np.ascontiguousarray(np.broadcast_to(q_dec[:, :, None], (RET_HEADS, CHUNK, RET_DK))),
           np.ascontiguousarray(np.broadcast_to(k_dec[:, :, None], (RET_HEADS, CHUNK, RET_DK))),
           np.ascontiguousarray(np.broadcast_to(c_dec[:, None, None], (RET_HEADS, 1, RET_DV))))
    m1, q1, k1, c1 = _decay_tables(1)
    sdec = np.stack([m1[:, 0, 0], q1[:, 0], k1[:, 0], c1], axis=1)

    x2d = x_prompt.reshape(n_prompt_tok, D_MODEL)
    proj_p = _inproj(x2d, nmix, w_in_bf, cos_p, sin_p, INPROJ_TILE, batch, BF16)
    xs2d = x_sample.reshape(nb, D_MODEL)
    x_small = jnp.concatenate([meta_tokens, xs2d], axis=0)
    proj_small = _inproj(x_small, nmix, w_in_bf, cos_s, sin_s, N_META + nb, 1, F32)
    proj_s = tuple(a[N_META:] for a in proj_small)

    lead = CHUNK - N_META
    kmeta = jnp.pad(proj_small[1][:N_META], ((lead, 0), (0, 0))).astype(BF16)
    vmeta = jnp.pad(proj_small[2][:N_META], ((lead, 0), (0, 0))).astype(BF16)
    pmeta = proj_small[4][:N_META]

    grp = SAMPLE_GROUP

    def cols(a):
        return a.T.reshape(RET_QK, nb // grp, grp).transpose(1, 0, 2)

    pool2d = state_pool[0].reshape(nb, POOL_BUF * POOL_WIDTH)
    (st_s, pool_s, h1_s, xn2_s, idx_s, gate_s, rank_s, cnt_s) = _sample_mixer(
        sdec, cols(proj_s[0]), cols(proj_s[1]), proj_s, xs2d, state_ret[0], pool2d, wts)

    (h1_p, xn2_p, idx_p, gate_p, rank_p, cnt, s_fin, p_fin) = _mixer(
        proj_p, x2d, kmeta, vmeta, pmeta, dec, wts, cnt_s, batch, seq)

    assert n_tok % ROW_TILE == 0 and nb == ROW_TILE
    n_tiles = n_tok // ROW_TILE
    i32 = jnp.int32
    counts = cnt[:, 0].astype(i32)
    padded = ((counts + MOE_BLOCK - 1) // MOE_BLOCK) * MOE_BLOCK
    pad_end = jnp.cumsum(padded)
    pad_start = pad_end - padded
    n_blocks = (n_tok * TOP_K) // MOE_BLOCK + N_EXPERTS
    block_row = jnp.arange(n_blocks, dtype=i32) * MOE_BLOCK
    block_e = jnp.minimum(jnp.sum((pad_end[None, :] <= block_row[:, None]).astype(i32), axis=1), N_EXPERTS - 1)
    nblk = (pad_end[-1:] // MOE_BLOCK).astype(i32)
    idx = jnp.concatenate([idx_p, idx_s], axis=1)
    rank = jnp.concatenate([rank_p, rank_s], axis=1)
    gates = jnp.concatenate([gate_p, gate_s], axis=1)
    onehot = idx[None] == jnp.arange(N_EXPERTS, dtype=i32)[:, None, None]
    tile_cnt = jnp.sum(onehot.reshape(N_EXPERTS, TOP_K, n_tiles, ROW_TILE).astype(i32), axis=(1, 3)).T
    by_time = jnp.concatenate([tile_cnt[-1:], tile_cnt[:-1]], axis=0)
    before_time = jnp.cumsum(by_time, axis=0) - by_time
    run_before = jnp.concatenate([before_time[1:], before_time[:1]], axis=0)
    tile_off = jnp.cumsum(tile_cnt, axis=1) - tile_cnt
    tile_dst = pad_start[None, :] + run_before
    delta = jnp.repeat((tile_off - run_before).T, ROW_TILE, axis=1)
    pos = rank + jnp.sum(jnp.where(onehot, delta[:, None, :], 0), axis=0)
    tcnt, tdst = tile_cnt.reshape(-1), tile_dst.reshape(-1)

    x_sorted = _dispatch(tcnt, tdst, pad_start + counts, padded - counts, nblk, pos, xn2_p, xn2_s,
                         n_blocks * MOE_BLOCK)
    y_sorted = _experts(block_e, nblk, x_sorted, wgu, bgu, wd, bd)
    y_p, y_s = _combine(tcnt, tdst, pos, gates, y_sorted, h1_p, h1_s, norm_final[None, :])

    y_prompt = y_p.reshape(batch, seq, D_MODEL)
    y_sample = y_s.reshape(nb, 1, D_MODEL)
    ret_state_prompt = s_fin[None]
    pool_state_prompt = p_fin[:, 1:, :][None]
    ret_state_sample = st_s[None]
    pool_state_sample = pool_s.reshape(nb, POOL_BUF, POOL_WIDTH)[None]
    return (y_prompt, y_sample, ret_state_prompt, pool_state_prompt, ret_state_sample, pool_state_sample)
```

```python
import functools

import jax
import jax.numpy as jnp
import numpy as np
from jax import lax
from jax.experimental import pallas as pl
from jax.experimental.pallas import tpu as pltpu

F32 = jnp.float32
BF16 = jnp.bfloat16

D_MODEL = 1024
N_META = 16
RET_HEADS = 4
RET_DK = 128
RET_DV = 256
RET_QK = RET_HEADS * RET_DK
RET_V = RET_HEADS * RET_DV
CHUNK = 128
ROPE_BASE = 10000.0
POOL_WINDOWS = (2, 4, 8, 16)
POOL_GROUPS = 4
POOL_GROUP_DIM = 128
POOL_WIDTH = POOL_GROUPS * POOL_GROUP_DIM
POOL_BUF = max(POOL_WINDOWS) - 1
N_EXPERTS = 32
TOP_K = 4
D_FF = D_MODEL
SWIGLU_LIMIT = 7.0
SWIGLU_ALPHA = 1.702
EPS = 1e-6
IN_WIDTHS = (RET_QK, RET_QK, RET_V, RET_V, POOL_WIDTH, D_MODEL, D_MODEL)
IN_TOTAL = sum(IN_WIDTHS)
IN_OFFS = tuple(int(s) for s in np.cumsum((0,) + IN_WIDTHS))

LANES = 128
ROW_CHUNKS = D_MODEL // LANES
INPROJ_TILE = 512
MIXER_TILE = 256
MOE_BLOCK = 256
ROW_TILE = 128
SAMPLE_GROUP = 8
WEIGHT_CAST_ROWS = 128
VMEM_LIMIT = 56 * 1024 * 1024

assert N_META + 1 >= max(POOL_WINDOWS)
assert POOL_WINDOWS == (2, 4, 8, 16)


def _dot(a, b):
    return jnp.dot(a, b, preferred_element_type=F32)


def _rmsnorm(x, w):
    return x * lax.rsqrt(jnp.mean(x * x, axis=-1, keepdims=True) + EPS) * w


def _store_rows_as_tiles(ref, x):
    rows = x.shape[0]
    for c in range(ROW_CHUNKS):
        ref[pl.ds(c, rows, stride=ROW_CHUNKS), :] = x[:, c * LANES:(c + 1) * LANES]


def _load_rows_from_tiles(ref, rows):
    return jnp.concatenate([ref[pl.ds(c, rows, stride=ROW_CHUNKS), :] for c in range(ROW_CHUNKS)], axis=1)


def _inproj_kernel(x_ref, nw_ref, w_ref, cos_ref, sin_ref,
                   q_ref, k_ref, v_ref, g_ref, p_ref, ga_ref, gb_ref):
    xn = _rmsnorm(x_ref[...], nw_ref[...]).astype(BF16)
    cos = cos_ref[...]
    sin = sin_ref[...]

    def seg(i):
        return _dot(xn, w_ref[:, IN_OFFS[i]:IN_OFFS[i + 1]])

    def rot(a):
        return a * cos + pltpu.roll(a, RET_DK // 2, 1) * sin

    q = seg(0)
    k = seg(1)
    for h in range(RET_HEADS):
        sl = slice(h * RET_DK, (h + 1) * RET_DK)
        q_ref[:, sl] = rot(q[:, sl]).astype(q_ref.dtype)
        k_ref[:, sl] = (rot(k[:, sl]) * (RET_DK ** -0.5)).astype(k_ref.dtype)
    v_ref[...] = seg(2).astype(v_ref.dtype)
    g_ref[...] = seg(3).astype(g_ref.dtype)
    p_ref[...] = seg(4)
    ga_ref[...] = seg(5).astype(ga_ref.dtype)
    gb_ref[...] = seg(6).astype(gb_ref.dtype)


def _inproj(x2d, nw, w_in_bf, cosf, sinf, tile, n_outer, act_dtype):
    rows = x2d.shape[0]
    n_inner = rows // (tile * n_outer)
    row_map = lambda b, j: (b * n_inner + j, 0)
    tab_map = lambda b, j: (j, 0)
    const = lambda b, j: (0, 0)
    widths = IN_WIDTHS
    dts = (act_dtype, act_dtype, act_dtype, act_dtype, F32, act_dtype, act_dtype)
    return pl.pallas_call(
        _inproj_kernel,
        grid=(n_outer, n_inner),
        in_specs=[
            pl.BlockSpec((tile, D_MODEL), row_map),
            pl.BlockSpec((1, D_MODEL), const),
            pl.BlockSpec((D_MODEL, IN_TOTAL), const, pipeline_mode=pl.Buffered(1)),
            pl.BlockSpec((tile, RET_DK), tab_map),
            pl.BlockSpec((tile, RET_DK), tab_map),
        ],
        out_specs=[pl.BlockSpec((tile, w), row_map) for w in widths],
        out_shape=[jax.ShapeDtypeStruct((rows, w), dt) for w, dt in zip(widths, dts)],
        compiler_params=pltpu.CompilerParams(
            dimension_semantics=("arbitrary", "arbitrary"), vmem_limit_bytes=VMEM_LIMIT),
        name="inproj",
    )(x2d, nw, w_in_bf, cosf, sinf)


def _group_norm(o, gn_row):
    mu = jnp.mean(o, axis=-1, keepdims=True)
    var = jnp.mean(jnp.square(o - mu), axis=-1, keepdims=True)
    return (o - mu) * lax.rsqrt(var + EPS) * gn_row


def _pool_branch(groups, poolw_ref, pscale_ref, wpool_ref):
    pm = [_dot(g.astype(BF16), poolw_ref[i]) for i, g in enumerate(groups)]
    pm = jnp.concatenate(pm, axis=1) * pscale_ref[...]
    return _dot(pm.astype(BF16), wpool_ref[...])


def _merge_tail(o_norm, g, ga, gb, yb, x, wret_ref, wout_ref):
    gf = g.astype(F32)
    ya = _dot((gf * jax.nn.sigmoid(gf) * o_norm).astype(BF16), wret_ref[...])
    merged = jax.nn.sigmoid(ga.astype(F32)) * ya + jax.nn.sigmoid(gb.astype(F32)) * yb
    return x + _dot(merged.astype(BF16), wout_ref[...])


def _route(h1, nffn_ref, wrt_ref, br_ref, run_scr,
           xn2_ref, idx_ref, gate_ref, rank_ref):
    tm = h1.shape[0]
    xn2 = _rmsnorm(h1, nffn_ref[...])
    xn2_ref[...] = xn2
    logits = lax.dot_general(wrt_ref[...], xn2.astype(BF16), (((1,), (1,)), ((), ())),
                             preferred_element_type=F32) + br_ref[...]
    e_iota = lax.broadcasted_iota(jnp.int32, (N_EXPERTS, tm), 0)
    work = logits
    vals, sels = [], []
    chosen = jnp.zeros((N_EXPERTS, tm), F32)
    for _ in range(TOP_K):
        m = jnp.max(work, axis=0, keepdims=True)
        sel = jnp.min(jnp.where(work == m, e_iota, N_EXPERTS), axis=0, keepdims=True)
        hit = e_iota == sel
        vals.append(m)
        sels.append(sel)
        chosen = jnp.where(hit, 1.0, chosen)
        work = jnp.where(hit, -jnp.inf, work)
    exps = [jnp.exp(v - vals[0]) for v in vals]
    denom = exps[0] + exps[1] + exps[2] + exps[3]
    gates = [e / denom for e in exps]
    r_i = lax.broadcasted_iota(jnp.int32, (tm, tm), 0)
    c_i = lax.broadcasted_iota(jnp.int32, (tm, tm), 1)
    before = jnp.where(r_i < c_i, 1.0, 0.0).astype(BF16)
    base = run_scr[...] + _dot(chosen.astype(BF16), before)
    for kk in range(TOP_K):
        rk = jnp.sum(jnp.where(e_iota == sels[kk], base, 0.0), axis=0, keepdims=True)
        rank_ref[kk:kk + 1, :] = rk.astype(jnp.int32)
        idx_ref[kk:kk + 1, :] = sels[kk]
        gate_ref[kk:kk + 1, :] = gates[kk]
    run_scr[...] = run_scr[...] + jnp.sum(chosen, axis=1, keepdims=True)


def _mixer_kernel(q_ref, k_ref, v_ref, g_ref, p_ref, ga_ref, gb_ref, x_ref,
                  kmeta_ref, vmeta_ref, pmeta_ref, mask_ref, qdec_ref, kdec_ref, cdec_ref, gn_ref,
                  poolw_ref, pscale_ref, wret_ref, wpool_ref, wout_ref, nffn_ref, wrt_ref, br_ref,
                  cnt0_ref,
                  h1_ref, xn2_ref, idx_ref, gate_ref, rank_ref, cnt_ref, sfin_ref, pfin_ref,
                  s_scr, ext_scr, o_scr, run_scr):
    b = pl.program_id(0)
    j = pl.program_id(1)
    nj = pl.num_programs(1)
    tm = q_ref.shape[0]

    def state_update(s_old, kc, vc, h):
        kd = (kc.astype(F32) * kdec_ref[h]).astype(BF16)
        upd = lax.dot_general(kd, vc, (((0,), (0,)), ((), ())), preferred_element_type=F32)
        return s_old * cdec_ref[h] + upd

    @pl.when(jnp.logical_and(b == 0, j == 0))
    def _():
        run_scr[...] = cnt0_ref[:, 0:1]

    @pl.when(j == 0)
    def _():
        for h in range(RET_HEADS):
            kc = kmeta_ref[:, h * RET_DK:(h + 1) * RET_DK]
            vc = vmeta_ref[:, h * RET_DV:(h + 1) * RET_DV]
            s_scr[h] = state_update(jnp.zeros((RET_DK, RET_DV), F32), kc, vc, h)
        ext_scr[0:N_META, :] = pmeta_ref[...]

    for c in range(tm // CHUNK):
        rows = slice(c * CHUNK, (c + 1) * CHUNK)
        for h in range(RET_HEADS):
            qc = q_ref[rows, h * RET_DK:(h + 1) * RET_DK]
            kc = k_ref[rows, h * RET_DK:(h + 1) * RET_DK]
            vc = v_ref[rows, h * RET_DV:(h + 1) * RET_DV]
            s_old = s_scr[h]
            scores = lax.dot_general(qc, kc, (((1,), (1,)), ((), ())),
                                     preferred_element_type=F32) * mask_ref[h]
            qd = (qc.astype(F32) * qdec_ref[h]).astype(BF16)
            lhs = jnp.concatenate([scores.astype(BF16), qd], axis=1)
            rhs = jnp.concatenate([vc, s_old.astype(BF16)], axis=0)
            o = _dot(lhs, rhs)
            s_scr[h] = state_update(s_old, kc, vc, h)
            o_scr[rows, h * RET_DV:(h + 1) * RET_DV] = _group_norm(
                o, gn_ref[:, h * RET_DV:(h + 1) * RET_DV])

    p = p_ref[...]
    ext_scr[N_META:N_META + tm, :] = p
    a = ext_scr[...]
    g1 = POOL_GROUP_DIM
    s2 = a + pltpu.roll(a, 1, 0)
    s4 = s2[:, g1:] + pltpu.roll(s2[:, g1:], 2, 0)
    s8 = s4[:, g1:] + pltpu.roll(s4[:, g1:], 4, 0)
    s16 = s8[:, g1:] + pltpu.roll(s8[:, g1:], 8, 0)
    sums = (s2[N_META:, :g1], s4[N_META:, :g1], s8[N_META:, :g1], s16[N_META:, :])
    groups = [sums[i] * (1.0 / POOL_WINDOWS[i]) - p[:, i * g1:(i + 1) * g1] for i in range(POOL_GROUPS)]
    ext_scr[0:N_META, :] = ext_scr[tm:tm + N_META, :]

    yb = _pool_branch(groups, poolw_ref, pscale_ref, wpool_ref)
    h1 = _merge_tail(o_scr[...], g_ref[...], ga_ref[...], gb_ref[...], yb, x_ref[...], wret_ref, wout_ref)
    h1_ref[...] = h1
    _route(h1, nffn_ref, wrt_ref, br_ref, run_scr, xn2_ref, idx_ref, gate_ref, rank_ref)
    cnt_ref[...] = jnp.broadcast_to(run_scr[...], cnt_ref.shape)

    @pl.when(j == nj - 1)
    def _():
        for h in range(RET_HEADS):
            sfin_ref[0, h] = s_scr[h]
        pfin_ref[0] = ext_scr[0:N_META, :]


def _mixer(proj, x2d, kmeta, vmeta, pmeta, dec, wts, cnt0, batch, seq):
    q, k, v, g, p, ga, gb = proj
    tm = MIXER_TILE
    nj = seq // tm
    rows = batch * seq
    row_map = lambda b, j: (b * nj + j, 0)
    lane_map = lambda b, j: (0, b * nj + j)
    c2 = lambda b, j: (0, 0)
    c3 = lambda b, j: (0, 0, 0)

    def whole(a):
        return pl.BlockSpec(a.shape, c2 if a.ndim == 2 else c3)

    mask, qdec, kdec, cdec = dec
    gn, poolw, pscale, wret, wpool, wout, nffn, wrt, br = wts
    in_arrays = [q, k, v, g, p, ga, gb, x2d, kmeta, vmeta, pmeta, mask, qdec, kdec, cdec, gn,
                 poolw, pscale, wret, wpool, wout, nffn, wrt, br, cnt0]
    in_specs = [pl.BlockSpec((tm, a.shape[1]), row_map) for a in in_arrays[:8]]
    in_specs += [whole(a) for a in in_arrays[8:]]
    out_shape = [
        jax.ShapeDtypeStruct((rows, D_MODEL), F32),
        jax.ShapeDtypeStruct((rows, D_MODEL), F32),
        jax.ShapeDtypeStruct((TOP_K, rows), jnp.int32),
        jax.ShapeDtypeStruct((TOP_K, rows), F32),
        jax.ShapeDtypeStruct((TOP_K, rows), jnp.int32),
        jax.ShapeDtypeStruct((N_EXPERTS, LANES), F32),
        jax.ShapeDtypeStruct((batch, RET_HEADS, RET_DK, RET_DV), F32),
        jax.ShapeDtypeStruct((batch, N_META, POOL_WIDTH), F32),
    ]
    out_specs = [
        pl.BlockSpec((tm, D_MODEL), row_map),
        pl.BlockSpec((tm, D_MODEL), row_map),
        pl.BlockSpec((TOP_K, tm), lane_map),
        pl.BlockSpec((TOP_K, tm), lane_map),
        pl.BlockSpec((TOP_K, tm), lane_map),
        pl.BlockSpec((N_EXPERTS, LANES), c2),
        pl.BlockSpec((1, RET_HEADS, RET_DK, RET_DV), lambda b, j: (b, 0, 0, 0)),
        pl.BlockSpec((1, N_META, POOL_WIDTH), lambda b, j: (b, 0, 0)),
    ]
    return pl.pallas_call(
        _mixer_kernel,
        grid=(batch, nj),
        in_specs=in_specs,
        out_specs=out_specs,
        out_shape=out_shape,
        scratch_shapes=[
            pltpu.VMEM((RET_HEADS, RET_DK, RET_DV), F32),
            pltpu.VMEM((N_META + tm, POOL_WIDTH), F32),
            pltpu.VMEM((tm, RET_V), F32),
            pltpu.VMEM((N_EXPERTS, 1), F32),
        ],
        compiler_params=pltpu.CompilerParams(
            dimension_semantics=("arbitrary", "arbitrary"), vmem_limit_bytes=VMEM_LIMIT),
        name="mixer",
    )(*in_arrays)


def _sample_kernel(sdec_ref, qt_ref, kt_ref, q_ref, k_ref, v_ref, g_ref, p_ref, ga_ref, gb_ref, x_ref,
                   st_ref, pool_ref, gn_ref,
                   poolw_ref, pscale_ref, wret_ref, wpool_ref, wout_ref, nffn_ref, wrt_ref, br_ref,
                   stout_ref, poolout_ref, h1_ref, xn2_ref, idx_ref, gate_ref, rank_ref,
                   cnt_ref,
                   o_scr, run_scr):
    i = pl.program_id(0)
    n = pl.num_programs(0)
    grp = st_ref.shape[0]
    row0 = pl.multiple_of(i * grp, grp)

    @pl.when(i == 0)
    def _():
        run_scr[...] = jnp.zeros_like(run_scr)

    q8 = q_ref[pl.ds(row0, grp), :]
    k8 = k_ref[pl.ds(row0, grp), :]
    v8 = v_ref[pl.ds(row0, grp), :]
    for h in range(RET_HEADS):
        ksl = slice(h * RET_DK, (h + 1) * RET_DK)
        vsl = slice(h * RET_DV, (h + 1) * RET_DV)
        score = jnp.sum(q8[:, ksl] * k8[:, ksl], axis=1, keepdims=True) * sdec_ref[h, 0]
        intra = score * v8[:, vsl]
        for bb in range(grp):
            s_old = st_ref[bb, h]
            qcol = qt_ref[0, ksl, bb:bb + 1] * sdec_ref[h, 1]
            kcol = kt_ref[0, ksl, bb:bb + 1] * sdec_ref[h, 2]
            cross = jnp.sum(s_old * qcol, axis=0, keepdims=True)
            o_scr[pl.ds(row0 + bb, 1), vsl] = intra[bb:bb + 1, :] + cross
            stout_ref[bb, h] = s_old * sdec_ref[h, 3] + kcol * v8[bb:bb + 1, vsl]

    @pl.when(i == n - 1)
    def _():
        o = o_scr[...]
        o_norm = jnp.concatenate(
            [_group_norm(o[:, h * RET_DV:(h + 1) * RET_DV], gn_ref[:, h * RET_DV:(h + 1) * RET_DV])
             for h in range(RET_HEADS)], axis=1)
        p = p_ref[...]
        w = POOL_WIDTH
        g1 = POOL_GROUP_DIM

        def prev(r, lo):
            return pool_ref[:, r * w + lo:(r + 1) * w]

        s2 = p + prev(14, 0)
        s4 = s2[:, g1:] + prev(13, g1) + prev(12, g1)
        s8 = s4[:, g1:] + prev(11, 2 * g1) + prev(10, 2 * g1) + prev(9, 2 * g1) + prev(8, 2 * g1)
        s16 = s8[:, g1:]
        for r in range(7, -1, -1):
            s16 = s16 + prev(r, 3 * g1)
        sums = (s2[:, :g1], s4[:, :g1], s8[:, :g1], s16)
        groups = [sums[t] * (1.0 / POOL_WINDOWS[t]) - p[:, t * g1:(t + 1) * g1] for t in range(POOL_GROUPS)]
        poolout_ref[:, 0:(POOL_BUF - 1) * w] = pool_ref[:, w:POOL_BUF * w]
        poolout_ref[:, (POOL_BUF - 1) * w:] = p
        yb = _pool_branch(groups, poolw_ref, pscale_ref, wpool_ref)
        h1 = _merge_tail(o_norm, g_ref[...], ga_ref[...], gb_ref[...], yb, x_ref[...], wret_ref, wout_ref)
        h1_ref[...] = h1
        _route(h1, nffn_ref, wrt_ref, br_ref, run_scr, xn2_ref, idx_ref, gate_ref, rank_ref)
        cnt_ref[...] = jnp.broadcast_to(run_scr[...], cnt_ref.shape)


def _sample_mixer(sdec, qt, kt, proj, x2d, state, pool2d, wts):
    q, k, v, g, p, ga, gb = proj
    nb = x2d.shape[0]
    grp = SAMPLE_GROUP
    c2 = lambda i: (0, 0)

    def whole(a):
        return pl.BlockSpec(a.shape, c2)

    gn, poolw, pscale, wret, wpool, wout, nffn, wrt, br = wts
    in_arrays = [sdec, qt, kt, q, k, v, g, p, ga, gb, x2d, state, pool2d, gn,
                 poolw, pscale, wret, wpool, wout, nffn, wrt, br]
    in_specs = [pl.BlockSpec(memory_space=pltpu.SMEM),
                pl.BlockSpec((1, RET_QK, grp), lambda i: (i, 0, 0)),
                pl.BlockSpec((1, RET_QK, grp), lambda i: (i, 0, 0))]
    in_specs += [whole(a) for a in (q, k, v, g, p, ga, gb, x2d)]
    in_specs += [pl.BlockSpec((grp, RET_HEADS, RET_DK, RET_DV), lambda i: (i, 0, 0, 0)), whole(pool2d), whole(gn),
                 pl.BlockSpec(poolw.shape, lambda i: (0, 0, 0))]
    in_specs += [whole(a) for a in (pscale, wret, wpool, wout, nffn, wrt, br)]
    out_shape = [
        jax.ShapeDtypeStruct(state.shape, F32),
        jax.ShapeDtypeStruct(pool2d.shape, F32),
        jax.ShapeDtypeStruct((nb, D_MODEL), F32),
        jax.ShapeDtypeStruct((nb, D_MODEL), F32),
        jax.ShapeDtypeStruct((TOP_K, nb), jnp.int32),
        jax.ShapeDtypeStruct((TOP_K, nb), F32),
        jax.ShapeDtypeStruct((TOP_K, nb), jnp.int32),
        jax.ShapeDtypeStruct((N_EXPERTS, LANES), F32),
    ]
    out_specs = [pl.BlockSpec((grp, RET_HEADS, RET_DK, RET_DV), lambda i: (i, 0, 0, 0))]
    out_specs += [pl.BlockSpec(s.shape, c2) for s in out_shape[1:]]
    return pl.pallas_call(
        _sample_kernel,
        grid=(nb // grp,),
        in_specs=in_specs,
        out_specs=out_specs,
        out_shape=out_shape,
        scratch_shapes=[pltpu.VMEM((nb, RET_V), F32), pltpu.VMEM((N_EXPERTS, 1), F32)],
        compiler_params=pltpu.CompilerParams(
            dimension_semantics=("arbitrary",), vmem_limit_bytes=VMEM_LIMIT),
        name="sample_mixer",
    )(*in_arrays)


TILE_ROWS = ROW_TILE * TOP_K


def _run_copy(src_ref, src_row, dst_ref, dst_row, n_rows, sem):
    def rows(ref, row):
        return ref.at[pl.ds(pl.multiple_of(row * ROW_CHUNKS, ROW_CHUNKS), n_rows * ROW_CHUNKS), :]

    return pltpu.make_async_copy(rows(src_ref, src_row), rows(dst_ref, dst_row), sem)


def _for_each_run(tcnt_ref, tile, fn):
    def body(e, off):
        n = tcnt_ref[tile * N_EXPERTS + e]

        @pl.when(n > 0)
        def _():
            fn(e, off, n)

        return off + n

    lax.fori_loop(0, N_EXPERTS, body, 0)


def _dispatch_kernel(tcnt_ref, tdst_ref, zrow_ref, zcnt_ref, nblk_ref, pos_ref, xp_ref, xs_ref, out_ref,
                     sorted_scr, zero_scr, sem, zsem):
    i = pl.program_id(0)
    n_prompt = pl.num_programs(0) - 1
    n_blocks = out_ref.shape[0] // (MOE_BLOCK * ROW_CHUNKS)
    slot = i % 2

    def wait_tile(s):
        pltpu.make_async_copy(out_ref.at[pl.ds(0, TILE_ROWS * ROW_CHUNKS), :], sorted_scr.at[s], sem.at[s]).wait()

    def for_each_pad(fn):
        def body(e, carry):
            n = zcnt_ref[e]

            @pl.when(n > 0)
            def _():
                fn(_run_copy(zero_scr, 0, out_ref, zrow_ref[e], n, zsem))

            dead = nblk_ref[0] + e

            @pl.when(dead < n_blocks)
            def _():
                fn(_run_copy(zero_scr, 0, out_ref, dead * MOE_BLOCK, MOE_BLOCK, zsem))

            return carry

        lax.fori_loop(0, N_EXPERTS, body, 0)

    @pl.when(i == 0)
    def _():
        zero_scr[...] = jnp.zeros_like(zero_scr)
        for_each_pad(lambda cp: cp.start())

    def sort_tile(x_ref):
        r_iota = lax.broadcasted_iota(jnp.int32, (TILE_ROWS, ROW_TILE), 0)
        hit = r_iota == pos_ref[0:1, :]
        for kk in range(1, TOP_K):
            hit = jnp.logical_or(hit, r_iota == pos_ref[kk:kk + 1, :])
        perm = jnp.where(hit, 1.0, 0.0).astype(BF16)
        xs = _dot(perm, x_ref[...].astype(BF16))

        @pl.when(i >= 2)
        def _():
            wait_tile(slot)

        _store_rows_as_tiles(sorted_scr.at[slot], xs)

    @pl.when(i < n_prompt)
    def _():
        sort_tile(xp_ref)

    @pl.when(i == n_prompt)
    def _():
        sort_tile(xs_ref)

    _for_each_run(tcnt_ref, i, lambda e, off, n: _run_copy(
        sorted_scr.at[slot], off, out_ref, tdst_ref[i * N_EXPERTS + e], n, sem.at[slot]).start())

    @pl.when(i == n_prompt)
    def _():
        wait_tile(1 - slot)
        wait_tile(slot)
        for_each_pad(lambda cp: cp.wait())


def _dispatch(tcnt, tdst, zrow, zcnt, nblk, pos, xn2_p, xn2_s, n_sorted):
    n_prompt = xn2_p.shape[0] // ROW_TILE
    grid_spec = pltpu.PrefetchScalarGridSpec(
        num_scalar_prefetch=5,
        grid=(n_prompt + 1,),
        in_specs=[
            pl.BlockSpec((TOP_K, ROW_TILE), lambda i, *_: (0, i)),
            pl.BlockSpec((ROW_TILE, D_MODEL), lambda i, *_: (jnp.minimum(i, n_prompt - 1), 0)),
            pl.BlockSpec((ROW_TILE, D_MODEL), lambda i, *_: (0, 0)),
        ],
        out_specs=pl.BlockSpec(memory_space=pl.ANY),
        scratch_shapes=[
            pltpu.VMEM((2, TILE_ROWS * ROW_CHUNKS, LANES), F32),
            pltpu.VMEM((MOE_BLOCK * ROW_CHUNKS, LANES), F32),
            pltpu.SemaphoreType.DMA((2,)), pltpu.SemaphoreType.DMA(()),
        ],
    )
    return pl.pallas_call(
        _dispatch_kernel,
        grid_spec=grid_spec,
        out_shape=jax.ShapeDtypeStruct((n_sorted * ROW_CHUNKS, LANES), F32),
        compiler_params=pltpu.CompilerParams(dimension_semantics=("arbitrary",)),
        name="moe_dispatch",
    )(tcnt, tdst, zrow, zcnt, nblk, pos, xn2_p, xn2_s)


def _expert_kernel(be_ref, nblk_ref, eord_ref, enext_ref, x_ref, wgu_ref, bgu_ref, wd_ref, bd_ref, y_ref,
                   wgu_f32, wd_f32, wgu_bf, wd_bf, wsem):
    i = pl.program_id(0)
    live = i < nblk_ref[0]
    new_expert = jnp.logical_or(i == 0, be_ref[i] != be_ref[jnp.maximum(i - 1, 0)])

    def weight_copies(e, s):
        return (pltpu.make_async_copy(wgu_ref.at[e], wgu_f32.at[s], wsem.at[0, s]),
                pltpu.make_async_copy(wd_ref.at[e], wd_f32.at[s], wsem.at[1, s]))

    @pl.when(jnp.logical_and(live, new_expert))
    def _():
        slot = eord_ref[i] % 2

        @pl.when(i == 0)
        def _():
            for cp in weight_copies(be_ref[0], 0):
                cp.start()

        for cp in weight_copies(be_ref[i], slot):
            cp.wait()

        @pl.when(enext_ref[i] >= 0)
        def _():
            for cp in weight_copies(enext_ref[i], 1 - slot):
                cp.start()

        def cast(c, carry):
            rows = pl.ds(pl.multiple_of(c * WEIGHT_CAST_ROWS, WEIGHT_CAST_ROWS), WEIGHT_CAST_ROWS)
            wgu_bf[rows, :] = wgu_f32[slot, rows, :].astype(BF16)
            wd_bf[rows, :] = wd_f32[slot, rows, :].astype(BF16)
            return carry

        lax.fori_loop(0, D_MODEL // WEIGHT_CAST_ROWS, cast, 0)

    @pl.when(jnp.logical_not(live))
    def _():
        y_ref[...] = jnp.zeros_like(y_ref)

    @pl.when(live)
    def _():
        x = _load_rows_from_tiles(x_ref, MOE_BLOCK)
        h = _dot(x.astype(BF16), wgu_bf[...]) + bgu_ref[0]
        gate = jnp.minimum(h[:, :D_FF], SWIGLU_LIMIT)
        up = jnp.clip(h[:, D_FF:], -SWIGLU_LIMIT, SWIGLU_LIMIT)
        glu = gate * jax.nn.sigmoid(gate * SWIGLU_ALPHA)
        y = _dot(((up + 1.0) * glu).astype(BF16), wd_bf[...]) + bd_ref[0]
        _store_rows_as_tiles(y_ref, y)


def _experts(block_e, nblk, block_eord, block_enext, x_sorted, wgu, bgu, wd, bd):
    n_blocks = x_sorted.shape[0] // (MOE_BLOCK * ROW_CHUNKS)
    wmap = lambda i, be, *_: (be[i], 0, 0)
    rmap = lambda i, *_: (i, 0)
    hbm = pl.BlockSpec(memory_space=pl.ANY)
    grid_spec = pltpu.PrefetchScalarGridSpec(
        num_scalar_prefetch=4,
        grid=(n_blocks,),
        in_specs=[
            pl.BlockSpec((MOE_BLOCK * ROW_CHUNKS, LANES), rmap),
            hbm,
            pl.BlockSpec((1, 1, 2 * D_FF), wmap),
            hbm,
            pl.BlockSpec((1, 1, D_MODEL), wmap),
        ],
        out_specs=pl.BlockSpec((MOE_BLOCK * ROW_CHUNKS, LANES), rmap),
        scratch_shapes=[
            pltpu.VMEM((2, D_MODEL, 2 * D_FF), F32), pltpu.VMEM((2, D_FF, D_MODEL), F32),
            pltpu.VMEM((D_MODEL, 2 * D_FF), BF16), pltpu.VMEM((D_FF, D_MODEL), BF16),
            pltpu.SemaphoreType.DMA((2, 2)),
        ],
    )
    return pl.pallas_call(
        _expert_kernel,
        grid_spec=grid_spec,
        out_shape=jax.ShapeDtypeStruct(x_sorted.shape, F32),
        compiler_params=pltpu.CompilerParams(
            dimension_semantics=("arbitrary",), vmem_limit_bytes=VMEM_LIMIT),
        name="moe_experts",
    )(block_e, nblk, block_eord, block_enext, x_sorted, wgu, bgu, wd, bd)


def _combine_kernel(tcnt_ref, tdst_ref, pos_ref, gate_ref, ys_ref, hp_ref, hs_ref, nf_ref, yp_ref, ysmp_ref,
                    runs_scr, sem):
    i = pl.program_id(0)
    n_tiles = pl.num_programs(0)
    n_prompt = n_tiles - 1
    slot = i % 2

    def start_runs(tile, s):
        _for_each_run(tcnt_ref, tile, lambda e, off, n: _run_copy(
            ys_ref, tdst_ref[tile * N_EXPERTS + e], runs_scr.at[s], off, n, sem.at[s]).start())

    @pl.when(i == 0)
    def _():
        start_runs(0, 0)

    @pl.when(i + 1 < n_tiles)
    def _():
        start_runs(i + 1, 1 - slot)

    pltpu.make_async_copy(ys_ref.at[pl.ds(0, TILE_ROWS * ROW_CHUNKS), :], runs_scr.at[slot], sem.at[slot]).wait()
    ys = _load_rows_from_tiles(runs_scr.at[slot], TILE_ROWS).astype(BF16)
    r_iota = lax.broadcasted_iota(jnp.int32, (TILE_ROWS, ROW_TILE), 0)
    gmat = jnp.zeros((TILE_ROWS, ROW_TILE), F32)
    for kk in range(TOP_K):
        gmat = jnp.where(r_iota == pos_ref[kk:kk + 1, :], gate_ref[kk:kk + 1, :], gmat)
    g_hi = gmat.astype(BF16)
    g_lo = (gmat - g_hi.astype(F32)).astype(BF16)
    contract0 = (((0,), (0,)), ((), ()))
    moe = (lax.dot_general(g_hi, ys, contract0, preferred_element_type=F32)
           + lax.dot_general(g_lo, ys, contract0, preferred_element_type=F32))

    @pl.when(i < n_prompt)
    def _():
        yp_ref[...] = _rmsnorm(hp_ref[...] + moe, nf_ref[...])

    @pl.when(i == n_prompt)
    def _():
        ysmp_ref[...] = _rmsnorm(hs_ref[...] + moe, nf_ref[...])


def _combine(tcnt, tdst, pos, gates, y_sorted, h1_p, h1_s, nf):
    n_prompt = h1_p.shape[0] // ROW_TILE
    pmap = lambda i, *_: (jnp.minimum(i, n_prompt - 1), 0)
    smap = lambda i, *_: (0, 0)
    lmap = lambda i, *_: (0, i)
    grid_spec = pltpu.PrefetchScalarGridSpec(
        num_scalar_prefetch=2,
        grid=(n_prompt + 1,),
        in_specs=[
            pl.BlockSpec((TOP_K, ROW_TILE), lmap),
            pl.BlockSpec((TOP_K, ROW_TILE), lmap),
            pl.BlockSpec(memory_space=pl.ANY),
            pl.BlockSpec((ROW_TILE, D_MODEL), pmap),
            pl.BlockSpec((ROW_TILE, D_MODEL), smap),
            pl.BlockSpec((1, D_MODEL), smap),
        ],
        out_specs=[pl.BlockSpec((ROW_TILE, D_MODEL), pmap), pl.BlockSpec((ROW_TILE, D_MODEL), smap)],
        scratch_shapes=[pltpu.VMEM((2, TILE_ROWS * ROW_CHUNKS, LANES), F32), pltpu.SemaphoreType.DMA((2,))],
    )
    return pl.pallas_call(
        _combine_kernel,
        grid_spec=grid_spec,
        out_shape=[jax.ShapeDtypeStruct(h1_p.shape, F32), jax.ShapeDtypeStruct(h1_s.shape, F32)],
        compiler_params=pltpu.CompilerParams(dimension_semantics=("arbitrary",)),
        name="moe_combine",
    )(tcnt, tdst, pos, gates, y_sorted, h1_p, h1_s, nf)


def _rotary_tables(pos):
    f = np.float32
    inv = np.power(f(ROPE_BASE), -np.arange(0, RET_DK, 2, dtype=f) / f(RET_DK)).astype(f)
    ang = (np.asarray(pos, f)[:, None] * inv[None, :]).astype(f)
    cos, sin = np.cos(ang).astype(f), np.sin(ang).astype(f)
    return np.concatenate([cos, cos], axis=1), np.concatenate([-sin, sin], axis=1)


def _decay_tables(chunk):
    f = np.float32
    log_g = np.log1p(-np.exp2(f(-5.0) - np.arange(RET_HEADS, dtype=f))).astype(f)
    i = np.arange(chunk, dtype=f)
    diff = i[:, None] - i[None, :]
    mask = np.where(diff[None] >= 0, np.exp(np.maximum(diff, f(0.0))[None] * log_g[:, None, None]), f(0.0)).astype(f)
    q_dec = np.exp((i + f(1.0))[None, :] * log_g[:, None]).astype(f)
    k_dec = np.exp((f(chunk) - f(1.0) - i)[None, :] * log_g[:, None]).astype(f)
    c_dec = np.exp(f(chunk) * log_g).astype(f)
    return mask, q_dec, k_dec, c_dec


def kernel(x_prompt, x_sample, state_ret, state_pool, meta_tokens, norm_mix, w_in, ret_gn, pool_w, pool_scale,
           w_ret_branch, w_pool_branch, w_out, norm_ffn, w_router, b_router, w_gate_up, b_gate_up, w_down, b_down,
           norm_final):
    batch, seq, _ = x_prompt.shape
    nb = x_sample.shape[0]
    past_len = 16384
    n_prompt_tok = batch * seq
    n_tok = n_prompt_tok + nb

    w_in_bf = w_in[0].astype(BF16)
    wts = (ret_gn[0][None, :], pool_w[0].astype(BF16), pool_scale[0][None, :],
           w_ret_branch[0].astype(BF16), w_pool_branch[0].astype(BF16), w_out[0].astype(BF16),
           norm_ffn[0][None, :], w_router[0].T.astype(BF16), b_router[0][:, None])
    wgu = w_gate_up[0]
    wd = w_down[0]
    bgu = b_gate_up[0][:, None, :]
    bd = b_down[0][:, None, :]
    nmix = norm_mix[0][None, :]

    cos_p, sin_p = _rotary_tables(N_META + np.arange(seq))
    cos_s, sin_s = _rotary_tables(np.concatenate([np.arange(N_META), np.full((nb,), past_len)]))
    mask, q_dec, k_dec, c_dec = _decay_tables(CHUNK)
    dec = (mask,
           np.ascontiguousarray(np.broadcast_to(q_dec[:, :, None], (RET_HEADS, CHUNK, RET_DK))),
           np.ascontiguousarray(np.broadcast_to(k_dec[:, :, None], (RET_HEADS, CHUNK, RET_DK))),
           np.ascontiguousarray(np.broadcast_to(c_dec[:, None, None], (RET_HEADS, 1, RET_DV))))
    m1, q1, k1, c1 = _decay_tables(1)
    sdec = np.stack([m1[:, 0, 0], q1[:, 0], k1[:, 0], c1], axis=1)

    x2d = x_prompt.reshape(n_prompt_tok, D_MODEL)
    proj_p = _inproj(x2d, nmix, w_in_bf, cos_p, sin_p, INPROJ_TILE, batch, BF16)
    xs2d = x_sample.reshape(nb, D_MODEL)
    x_small = jnp.concatenate([meta_tokens, xs2d], axis=0)
    proj_small = _inproj(x_small, nmix, w_in_bf, cos_s, sin_s, N_META + nb, 1, F32)
    proj_s = tuple(a[N_META:] for a in proj_small)

    lead = CHUNK - N_META
    kmeta = jnp.pad(proj_small[1][:N_META], ((lead, 0), (0, 0))).astype(BF16)
    vmeta = jnp.pad(proj_small[2][:N_META], ((lead, 0), (0, 0))).astype(BF16)
    pmeta = proj_small[4][:N_META]

    grp = SAMPLE_GROUP

    def cols(a):
        return a.T.reshape(RET_QK, nb // grp, grp).transpose(1, 0, 2)

    pool2d = state_pool[0].reshape(nb, POOL_BUF * POOL_WIDTH)
    (st_s, pool_s, h1_s, xn2_s, idx_s, gate_s, rank_s, cnt_s) = _sample_mixer(
        sdec, cols(proj_s[0]), cols(proj_s[1]), proj_s, xs2d, state_ret[0], pool2d, wts)

    (h1_p, xn2_p, idx_p, gate_p, rank_p, cnt, s_fin, p_fin) = _mixer(
        proj_p, x2d, kmeta, vmeta, pmeta, dec, wts, cnt_s, batch, seq)

    assert n_tok % ROW_TILE == 0 and nb == ROW_TILE
    n_tiles = n_tok // ROW_TILE
    i32 = jnp.int32
    counts = cnt[:, 0].astype(i32)
    padded = ((counts + MOE_BLOCK - 1) // MOE_BLOCK) * MOE_BLOCK
    pad_end = jnp.cumsum(padded)
    pad_start = pad_end - padded
    n_blocks = (n_tok * TOP_K) // MOE_BLOCK + N_EXPERTS
    block_row = jnp.arange(n_blocks, dtype=i32) * MOE_BLOCK
    block_e = jnp.minimum(jnp.sum((pad_end[None, :] <= block_row[:, None]).astype(i32), axis=1), N_EXPERTS - 1)
    nblk = (pad_end[-1:] // MOE_BLOCK).astype(i32)
    e_row = jnp.arange(N_EXPERTS, dtype=i32)
    used = padded > 0
    e_ord = jnp.cumsum(used.astype(i32)) - 1
    later_used = jnp.logical_and(e_row[None, :] > e_row[:, None], used[None, :])
    e_next = jnp.min(jnp.where(later_used, e_row[None, :], N_EXPERTS), axis=1)
    e_next = jnp.where(e_next == N_EXPERTS, -1, e_next)
    of_block = block_e[:, None] == e_row[None, :]
    block_eord = jnp.sum(jnp.where(of_block, e_ord[None, :], 0), axis=1)
    block_enext = jnp.sum(jnp.where(of_block, e_next[None, :], 0), axis=1)
    idx = jnp.concatenate([idx_p, idx_s], axis=1)
    rank = jnp.concatenate([rank_p, rank_s], axis=1)
    gates = jnp.concatenate([gate_p, gate_s], axis=1)
    onehot = idx[None] == jnp.arange(N_EXPERTS, dtype=i32)[:, None, None]
    tile_cnt = jnp.sum(onehot.reshape(N_EXPERTS, TOP_K, n_tiles, ROW_TILE).astype(i32), axis=(1, 3)).T
    by_time = jnp.concatenate([tile_cnt[-1:], tile_cnt[:-1]], axis=0)
    before_time = jnp.cumsum(by_time, axis=0) - by_time
    run_before = jnp.concatenate([before_time[1:], before_time[:1]], axis=0)
    tile_off = jnp.cumsum(tile_cnt, axis=1) - tile_cnt
    tile_dst = pad_start[None, :] + run_before
    delta = jnp.repeat((tile_off - run_before).T, ROW_TILE, axis=1)
    pos = rank + jnp.sum(jnp.where(onehot, delta[:, None, :], 0), axis=0)
    tcnt, tdst = tile_cnt.reshape(-1), tile_dst.reshape(-1)

    x_sorted = _dispatch(tcnt, tdst, pad_start + counts, padded - counts, nblk, pos, xn2_p, xn2_s,
                         n_blocks * MOE_BLOCK)
    y_sorted = _experts(block_e, nblk, block_eord, block_enext, x_sorted, wgu, bgu, wd, bd)
    y_p, y_s = _combine(tcnt, tdst, pos, gates, y_sorted, h1_p, h1_s, norm_final[None, :])

    y_prompt = y_p.reshape(batch, seq, D_MODEL)
    y_sample = y_s.reshape(nb, 1, D_MODEL)
    ret_state_prompt = s_fin[None]
    pool_state_prompt = p_fin[:, 1:, :][None]
    ret_state_sample = st_s[None]
    pool_state_sample = pool_s.reshape(nb, POOL_BUF, POOL_WIDTH)[None]
    return (y_prompt, y_sample, ret_state_prompt, pool_state_prompt, ret_state_sample, pool_state_sample)
```

```python
import functools

import jax
import jax.numpy as jnp
import numpy as np
from jax import lax
from jax.experimental import pallas as pl
from jax.experimental.pallas import tpu as pltpu

F32 = jnp.float32
BF16 = jnp.bfloat16

D_MODEL = 1024
N_META = 16
RET_HEADS = 4
RET_DK = 128
RET_DV = 256
RET_QK = RET_HEADS * RET_DK
RET_V = RET_HEADS * RET_DV
CHUNK = 128
ROPE_BASE = 10000.0
POOL_WINDOWS = (2, 4, 8, 16)
POOL_GROUPS = 4
POOL_GROUP_DIM = 128
POOL_WIDTH = POOL_GROUPS * POOL_GROUP_DIM
POOL_BUF = max(POOL_WINDOWS) - 1
N_EXPERTS = 32
TOP_K = 4
D_FF = D_MODEL
SWIGLU_LIMIT = 7.0
SWIGLU_ALPHA = 1.702
EPS = 1e-6
IN_WIDTHS = (RET_QK, RET_QK, RET_V, RET_V, POOL_WIDTH, D_MODEL, D_MODEL)
IN_TOTAL = sum(IN_WIDTHS)
IN_OFFS = tuple(int(s) for s in np.cumsum((0,) + IN_WIDTHS))

LANES = 128
ROW_CHUNKS = D_MODEL // (2 * LANES)
INPROJ_TILE = 512
MIXER_TILE = 256
MOE_BLOCK = 256
ROW_TILE = 128
SAMPLE_GROUP = 8
WEIGHT_CAST_ROWS = 128
VMEM_LIMIT = 56 * 1024 * 1024

assert N_META + 1 >= max(POOL_WINDOWS)
assert POOL_WINDOWS == (2, 4, 8, 16)


def _dot(a, b):
    return jnp.dot(a, b, preferred_element_type=F32)


def _rmsnorm(x, w):
    return x * lax.rsqrt(jnp.mean(x * x, axis=-1, keepdims=True) + EPS) * w


def _sigmoid(x):
    return 0.5 * jnp.tanh(0.5 * x) + 0.5


def _store_rows_as_tiles(ref, x):
    half = D_MODEL // 2
    hi = pltpu.bitcast(x[:, :half], jnp.uint32) & jnp.uint32(0xFFFF0000)
    lo = pltpu.bitcast(x[:, half:], jnp.uint32) >> 16
    ref[...] = (hi | lo).reshape(x.shape[0], ROW_CHUNKS, LANES)


def _load_rows_from_tiles(ref, rows):
    w = ref[...].reshape(rows, D_MODEL // 2)
    hi = pltpu.bitcast(w & jnp.uint32(0xFFFF0000), F32)
    lo = pltpu.bitcast(w << 16, F32)
    return jnp.concatenate([hi, lo], axis=1)


def _inproj_kernel(x_ref, nw_ref, w_ref, cos_ref, sin_ref,
                   q_ref, k_ref, v_ref, g_ref, p_ref, ga_ref, gb_ref):
    xn = _rmsnorm(x_ref[...], nw_ref[...]).astype(BF16)
    cos = cos_ref[...]
    sin = sin_ref[...]

    def seg(i):
        return _dot(xn, w_ref[:, IN_OFFS[i]:IN_OFFS[i + 1]])

    def rot(a):
        return a * cos + pltpu.roll(a, RET_DK // 2, 1) * sin

    q = seg(0)
    k = seg(1)
    for h in range(RET_HEADS):
        sl = slice(h * RET_DK, (h + 1) * RET_DK)
        q_ref[:, sl] = rot(q[:, sl]).astype(q_ref.dtype)
        k_ref[:, sl] = (rot(k[:, sl]) * (RET_DK ** -0.5)).astype(k_ref.dtype)
    v_ref[...] = seg(2).astype(v_ref.dtype)
    g_ref[...] = seg(3).astype(g_ref.dtype)
    p_ref[...] = seg(4)
    ga_ref[...] = seg(5).astype(ga_ref.dtype)
    gb_ref[...] = seg(6).astype(gb_ref.dtype)


def _inproj(x2d, nw, w_in_bf, cosf, sinf, tile, n_outer, act_dtype):
    rows = x2d.shape[0]
    n_inner = rows // (tile * n_outer)
    row_map = lambda b, j: (b * n_inner + j, 0)
    tab_map = lambda b, j: (j, 0)
    const = lambda b, j: (0, 0)
    widths = IN_WIDTHS
    dts = (act_dtype, act_dtype, act_dtype, act_dtype, F32, act_dtype, act_dtype)
    return pl.pallas_call(
        _inproj_kernel,
        grid=(n_outer, n_inner),
        in_specs=[
            pl.BlockSpec((tile, D_MODEL), row_map),
            pl.BlockSpec((1, D_MODEL), const),
            pl.BlockSpec((D_MODEL, IN_TOTAL), const, pipeline_mode=pl.Buffered(1)),
            pl.BlockSpec((tile, RET_DK), tab_map),
            pl.BlockSpec((tile, RET_DK), tab_map),
        ],
        out_specs=[pl.BlockSpec((tile, w), row_map) for w in widths],
        out_shape=[jax.ShapeDtypeStruct((rows, w), dt) for w, dt in zip(widths, dts)],
        compiler_params=pltpu.CompilerParams(
            dimension_semantics=("arbitrary", "arbitrary"), vmem_limit_bytes=VMEM_LIMIT),
        name="inproj",
    )(x2d, nw, w_in_bf, cosf, sinf)


def _group_norm(o, gn_row):
    mu = jnp.mean(o, axis=-1, keepdims=True)
    var = jnp.mean(jnp.square(o - mu), axis=-1, keepdims=True)
    return (o - mu) * lax.rsqrt(var + EPS) * gn_row


def _pool_branch(groups, poolw_ref, pscale_ref, wpool_ref):
    pm = [_dot(g.astype(BF16), poolw_ref[i]) for i, g in enumerate(groups)]
    pm = jnp.concatenate(pm, axis=1) * pscale_ref[...]
    return _dot(pm.astype(BF16), wpool_ref[...])


def _merge_tail(o_norm, g, ga, gb, yb, x, wret_ref, wout_ref):
    gf = g.astype(F32)
    ya = _dot((gf * _sigmoid(gf) * o_norm).astype(BF16), wret_ref[...])
    merged = _sigmoid(ga.astype(F32)) * ya + _sigmoid(gb.astype(F32)) * yb
    return x + _dot(merged.astype(BF16), wout_ref[...])


def _route(h1, nffn_ref, wrt_ref, br_ref, run_scr,
           xn2_ref, idx_ref, gate_ref, rank_ref):
    tm = h1.shape[0]
    xn2 = _rmsnorm(h1, nffn_ref[...])
    xn2_ref[...] = xn2
    logits = lax.dot_general(wrt_ref[...], xn2.astype(BF16), (((1,), (1,)), ((), ())),
                             preferred_element_type=F32) + br_ref[...]
    e_iota = lax.broadcasted_iota(jnp.int32, (N_EXPERTS, tm), 0)
    work = logits
    vals, sels = [], []
    chosen = jnp.zeros((N_EXPERTS, tm), F32)
    for _ in range(TOP_K):
        m = jnp.max(work, axis=0, keepdims=True)
        sel = jnp.min(jnp.where(work == m, e_iota, N_EXPERTS), axis=0, keepdims=True)
        hit = e_iota == sel
        vals.append(m)
        sels.append(sel)
        chosen = jnp.where(hit, 1.0, chosen)
        work = jnp.where(hit, -jnp.inf, work)
    exps = [jnp.exp(v - vals[0]) for v in vals]
    denom = exps[0] + exps[1] + exps[2] + exps[3]
    gates = [e / denom for e in exps]
    r_i = lax.broadcasted_iota(jnp.int32, (tm, tm), 0)
    c_i = lax.broadcasted_iota(jnp.int32, (tm, tm), 1)
    before = jnp.where(r_i < c_i, 1.0, 0.0).astype(BF16)
    base = run_scr[...] + _dot(chosen.astype(BF16), before)
    for kk in range(TOP_K):
        rk = jnp.sum(jnp.where(e_iota == sels[kk], base, 0.0), axis=0, keepdims=True)
        rank_ref[kk:kk + 1, :] = rk.astype(jnp.int32)
        idx_ref[kk:kk + 1, :] = sels[kk]
        gate_ref[kk:kk + 1, :] = gates[kk]
    run_scr[...] = run_scr[...] + jnp.sum(chosen, axis=1, keepdims=True)


def _mixer_kernel(q_ref, k_ref, v_ref, g_ref, p_ref, ga_ref, gb_ref, x_ref,
                  kmeta_ref, vmeta_ref, pmeta_ref, mask_ref, qdec_ref, kdec_ref, cdec_ref, gn_ref,
                  poolw_ref, pscale_ref, wret_ref, wpool_ref, wout_ref, nffn_ref, wrt_ref, br_ref,
                  cnt0_ref,
                  h1_ref, xn2_ref, idx_ref, gate_ref, rank_ref, cnt_ref, sfin_ref, pfin_ref,
                  s_scr, ext_scr, o_scr, run_scr):
    b = pl.program_id(0)
    j = pl.program_id(1)
    nj = pl.num_programs(1)
    tm = q_ref.shape[0]

    def state_update(s_old, kc, vc, h):
        kd = (kc.astype(F32) * kdec_ref[h]).astype(BF16)
        upd = lax.dot_general(kd, vc, (((0,), (0,)), ((), ())), preferred_element_type=F32)
        return s_old * cdec_ref[h] + upd

    @pl.when(jnp.logical_and(b == 0, j == 0))
    def _():
        run_scr[...] = cnt0_ref[:, 0:1]

    @pl.when(j == 0)
    def _():
        for h in range(RET_HEADS):
            kc = kmeta_ref[:, h * RET_DK:(h + 1) * RET_DK]
            vc = vmeta_ref[:, h * RET_DV:(h + 1) * RET_DV]
            s_scr[h] = state_update(jnp.zeros((RET_DK, RET_DV), F32), kc, vc, h)
        ext_scr[0:N_META, :] = pmeta_ref[...]

    for c in range(tm // CHUNK):
        rows = slice(c * CHUNK, (c + 1) * CHUNK)
        for h in range(RET_HEADS):
            qc = q_ref[rows, h * RET_DK:(h + 1) * RET_DK]
            kc = k_ref[rows, h * RET_DK:(h + 1) * RET_DK]
            vc = v_ref[rows, h * RET_DV:(h + 1) * RET_DV]
            s_old = s_scr[h]
            scores = lax.dot_general(qc, kc, (((1,), (1,)), ((), ())),
                                     preferred_element_type=F32) * mask_ref[h]
            qd = (qc.astype(F32) * qdec_ref[h]).astype(BF16)
            lhs = jnp.concatenate([scores.astype(BF16), qd], axis=1)
            rhs = jnp.concatenate([vc, s_old.astype(BF16)], axis=0)
            o = _dot(lhs, rhs)
            s_scr[h] = state_update(s_old, kc, vc, h)
            o_scr[rows, h * RET_DV:(h + 1) * RET_DV] = _group_norm(
                o, gn_ref[:, h * RET_DV:(h + 1) * RET_DV])

    p = p_ref[...]
    ext_scr[N_META:N_META + tm, :] = p
    a = ext_scr[...]
    g1 = POOL_GROUP_DIM
    s2 = a + pltpu.roll(a, 1, 0)
    s4 = s2[:, g1:] + pltpu.roll(s2[:, g1:], 2, 0)
    s8 = s4[:, g1:] + pltpu.roll(s4[:, g1:], 4, 0)
    s16 = s8[:, g1:] + pltpu.roll(s8[:, g1:], 8, 0)
    sums = (s2[N_META:, :g1], s4[N_META:, :g1], s8[N_META:, :g1], s16[N_META:, :])
    groups = [sums[i] * (1.0 / POOL_WINDOWS[i]) - p[:, i * g1:(i + 1) * g1] for i in range(POOL_GROUPS)]
    ext_scr[0:N_META, :] = ext_scr[tm:tm + N_META, :]

    yb = _pool_branch(groups, poolw_ref, pscale_ref, wpool_ref)
    h1 = _merge_tail(o_scr[...], g_ref[...], ga_ref[...], gb_ref[...], yb, x_ref[...], wret_ref, wout_ref)
    h1_ref[...] = h1
    _route(h1, nffn_ref, wrt_ref, br_ref, run_scr, xn2_ref, idx_ref, gate_ref, rank_ref)
    cnt_ref[...] = jnp.broadcast_to(run_scr[...], cnt_ref.shape)

    @pl.when(j == nj - 1)
    def _():
        for h in range(RET_HEADS):
            sfin_ref[0, h] = s_scr[h]
        pfin_ref[0] = ext_scr[0:N_META, :]


def _mixer(proj, x2d, kmeta, vmeta, pmeta, dec, wts, cnt0, batch, seq):
    q, k, v, g, p, ga, gb = proj
    tm = MIXER_TILE
    nj = seq // tm
    rows = batch * seq
    row_map = lambda b, j: (b * nj + j, 0)
    lane_map = lambda b, j: (0, b * nj + j)
    c2 = lambda b, j: (0, 0)
    c3 = lambda b, j: (0, 0, 0)

    def whole(a):
        return pl.BlockSpec(a.shape, c2 if a.ndim == 2 else c3)

    mask, qdec, kdec, cdec = dec
    gn, poolw, pscale, wret, wpool, wout, nffn, wrt, br = wts
    in_arrays = [q, k, v, g, p, ga, gb, x2d, kmeta, vmeta, pmeta, mask, qdec, kdec, cdec, gn,
                 poolw, pscale, wret, wpool, wout, nffn, wrt, br, cnt0]
    in_specs = [pl.BlockSpec((tm, a.shape[1]), row_map) for a in in_arrays[:8]]
    in_specs += [whole(a) for a in in_arrays[8:]]
    out_shape = [
        jax.ShapeDtypeStruct((rows, D_MODEL), F32),
        jax.ShapeDtypeStruct((rows, D_MODEL), F32),
        jax.ShapeDtypeStruct((TOP_K, rows), jnp.int32),
        jax.ShapeDtypeStruct((TOP_K, rows), F32),
        jax.ShapeDtypeStruct((TOP_K, rows), jnp.int32),
        jax.ShapeDtypeStruct((N_EXPERTS, LANES), F32),
        jax.ShapeDtypeStruct((batch, RET_HEADS, RET_DK, RET_DV), F32),
        jax.ShapeDtypeStruct((batch, N_META, POOL_WIDTH), F32),
    ]
    out_specs = [
        pl.BlockSpec((tm, D_MODEL), row_map),
        pl.BlockSpec((tm, D_MODEL), row_map),
        pl.BlockSpec((TOP_K, tm), lane_map),
        pl.BlockSpec((TOP_K, tm), lane_map),
        pl.BlockSpec((TOP_K, tm), lane_map),
        pl.BlockSpec((N_EXPERTS, LANES), c2),
        pl.BlockSpec((1, RET_HEADS, RET_DK, RET_DV), lambda b, j: (b, 0, 0, 0)),
        pl.BlockSpec((1, N_META, POOL_WIDTH), lambda b, j: (b, 0, 0)),
    ]
    return pl.pallas_call(
        _mixer_kernel,
        grid=(batch, nj),
        in_specs=in_specs,
        out_specs=out_specs,
        out_shape=out_shape,
        scratch_shapes=[
            pltpu.VMEM((RET_HEADS, RET_DK, RET_DV), F32),
            pltpu.VMEM((N_META + tm, POOL_WIDTH), F32),
            pltpu.VMEM((tm, RET_V), F32),
            pltpu.VMEM((N_EXPERTS, 1), F32),
        ],
        compiler_params=pltpu.CompilerParams(
            dimension_semantics=("arbitrary", "arbitrary"), vmem_limit_bytes=VMEM_LIMIT),
        name="mixer",
    )(*in_arrays)


def _sample_kernel(sdec_ref, qt_ref, kt_ref, q_ref, k_ref, v_ref, g_ref, p_ref, ga_ref, gb_ref, x_ref,
                   st_ref, pool_ref, gn_ref,
                   poolw_ref, pscale_ref, wret_ref, wpool_ref, wout_ref, nffn_ref, wrt_ref, br_ref,
                   stout_ref, poolout_ref, h1_ref, xn2_ref, idx_ref, gate_ref, rank_ref,
                   cnt_ref,
                   o_scr, run_scr):
    i = pl.program_id(0)
    n = pl.num_programs(0)
    grp = st_ref.shape[0]
    row0 = pl.multiple_of(i * grp, grp)

    @pl.when(i == 0)
    def _():
        run_scr[...] = jnp.zeros_like(run_scr)

    q8 = q_ref[pl.ds(row0, grp), :]
    k8 = k_ref[pl.ds(row0, grp), :]
    v8 = v_ref[pl.ds(row0, grp), :]
    for h in range(RET_HEADS):
        ksl = slice(h * RET_DK, (h + 1) * RET_DK)
        vsl = slice(h * RET_DV, (h + 1) * RET_DV)
        score = jnp.sum(q8[:, ksl] * k8[:, ksl], axis=1, keepdims=True) * sdec_ref[h, 0]
        intra = score * v8[:, vsl]
        for bb in range(grp):
            s_old = st_ref[bb, h]
            qcol = qt_ref[0, ksl, bb:bb + 1] * sdec_ref[h, 1]
            kcol = kt_ref[0, ksl, bb:bb + 1] * sdec_ref[h, 2]
            cross = jnp.sum(s_old * qcol, axis=0, keepdims=True)
            o_scr[pl.ds(row0 + bb, 1), vsl] = intra[bb:bb + 1, :] + cross
            stout_ref[bb, h] = s_old * sdec_ref[h, 3] + kcol * v8[bb:bb + 1, vsl]

    @pl.when(i == n - 1)
    def _():
        o = o_scr[...]
        o_norm = jnp.concatenate(
            [_group_norm(o[:, h * RET_DV:(h + 1) * RET_DV], gn_ref[:, h * RET_DV:(h + 1) * RET_DV])
             for h in range(RET_HEADS)], axis=1)
        p = p_ref[...]
        w = POOL_WIDTH
        g1 = POOL_GROUP_DIM

        def prev(r, lo):
            return pool_ref[:, r * w + lo:(r + 1) * w]

        s2 = p + prev(14, 0)
        s4 = s2[:, g1:] + prev(13, g1) + prev(12, g1)
        s8 = s4[:, g1:] + prev(11, 2 * g1) + prev(10, 2 * g1) + prev(9, 2 * g1) + prev(8, 2 * g1)
        s16 = s8[:, g1:]
        for r in range(7, -1, -1):
            s16 = s16 + prev(r, 3 * g1)
        sums = (s2[:, :g1], s4[:, :g1], s8[:, :g1], s16)
        groups = [sums[t] * (1.0 / POOL_WINDOWS[t]) - p[:, t * g1:(t + 1) * g1] for t in range(POOL_GROUPS)]
        poolout_ref[:, 0:(POOL_BUF - 1) * w] = pool_ref[:, w:POOL_BUF * w]
        poolout_ref[:, (POOL_BUF - 1) * w:] = p
        yb = _pool_branch(groups, poolw_ref, pscale_ref, wpool_ref)
        h1 = _merge_tail(o_norm, g_ref[...], ga_ref[...], gb_ref[...], yb, x_ref[...], wret_ref, wout_ref)
        h1_ref[...] = h1
        _route(h1, nffn_ref, wrt_ref, br_ref, run_scr, xn2_ref, idx_ref, gate_ref, rank_ref)
        cnt_ref[...] = jnp.broadcast_to(run_scr[...], cnt_ref.shape)


def _sample_mixer(sdec, qt, kt, proj, x2d, state, pool2d, wts):
    q, k, v, g, p, ga, gb = proj
    nb = x2d.shape[0]
    grp = SAMPLE_GROUP
    c2 = lambda i: (0, 0)

    def whole(a):
        return pl.BlockSpec(a.shape, c2)

    gn, poolw, pscale, wret, wpool, wout, nffn, wrt, br = wts
    in_arrays = [sdec, qt, kt, q, k, v, g, p, ga, gb, x2d, state, pool2d, gn,
                 poolw, pscale, wret, wpool, wout, nffn, wrt, br]
    in_specs = [pl.BlockSpec(memory_space=pltpu.SMEM),
                pl.BlockSpec((1, RET_QK, grp), lambda i: (i, 0, 0)),
                pl.BlockSpec((1, RET_QK, grp), lambda i: (i, 0, 0))]
    in_specs += [whole(a) for a in (q, k, v, g, p, ga, gb, x2d)]
    in_specs += [pl.BlockSpec((grp, RET_HEADS, RET_DK, RET_DV), lambda i: (i, 0, 0, 0)), whole(pool2d), whole(gn),
                 pl.BlockSpec(poolw.shape, lambda i: (0, 0, 0))]
    in_specs += [whole(a) for a in (pscale, wret, wpool, wout, nffn, wrt, br)]
    out_shape = [
        jax.ShapeDtypeStruct(state.shape, F32),
        jax.ShapeDtypeStruct(pool2d.shape, F32),
        jax.ShapeDtypeStruct((nb, D_MODEL), F32),
        jax.ShapeDtypeStruct((nb, D_MODEL), F32),
        jax.ShapeDtypeStruct((TOP_K, nb), jnp.int32),
        jax.ShapeDtypeStruct((TOP_K, nb), F32),
        jax.ShapeDtypeStruct((TOP_K, nb), jnp.int32),
        jax.ShapeDtypeStruct((N_EXPERTS, LANES), F32),
    ]
    out_specs = [pl.BlockSpec((grp, RET_HEADS, RET_DK, RET_DV), lambda i: (i, 0, 0, 0))]
    out_specs += [pl.BlockSpec(s.shape, c2) for s in out_shape[1:]]
    return pl.pallas_call(
        _sample_kernel,
        grid=(nb // grp,),
        in_specs=in_specs,
        out_specs=out_specs,
        out_shape=out_shape,
        scratch_shapes=[pltpu.VMEM((nb, RET_V), F32), pltpu.VMEM((N_EXPERTS, 1), F32)],
        compiler_params=pltpu.CompilerParams(
            dimension_semantics=("arbitrary",), vmem_limit_bytes=VMEM_LIMIT),
        name="sample_mixer",
    )(*in_arrays)


TILE_ROWS = ROW_TILE * TOP_K


def _run_copy(src_ref, src_row, dst_ref, dst_row, n_rows, sem):
    return pltpu.make_async_copy(src_ref.at[pl.ds(src_row, n_rows)], dst_ref.at[pl.ds(dst_row, n_rows)], sem)


def _for_each_run(tcnt_ref, tile, fn):
    def body(e, off):
        n = tcnt_ref[tile * N_EXPERTS + e]

        @pl.when(n > 0)
        def _():
            fn(e, off, n)

        return off + n

    lax.fori_loop(0, N_EXPERTS, body, 0)


def _dispatch_kernel(tcnt_ref, tdst_ref, zrow_ref, zcnt_ref, nblk_ref, pos_ref, xp_ref, xs_ref, out_ref,
                     sorted_scr, zero_scr, sem, zsem):
    i = pl.program_id(0)
    n_prompt = pl.num_programs(0) - 1
    n_blocks = out_ref.shape[0] // MOE_BLOCK
    slot = i % 2

    def wait_tile(s):
        pltpu.make_async_copy(out_ref.at[pl.ds(0, TILE_ROWS)], sorted_scr.at[s], sem.at[s]).wait()

    def for_each_pad(fn):
        def body(e, carry):
            n = zcnt_ref[e]

            @pl.when(n > 0)
            def _():
                fn(_run_copy(zero_scr, 0, out_ref, zrow_ref[e], n, zsem))

            dead = nblk_ref[0] + e

            @pl.when(dead < n_blocks)
            def _():
                fn(_run_copy(zero_scr, 0, out_ref, dead * MOE_BLOCK, MOE_BLOCK, zsem))

            return carry

        lax.fori_loop(0, N_EXPERTS, body, 0)

    @pl.when(i == 0)
    def _():
        zero_scr[...] = jnp.zeros_like(zero_scr)
        for_each_pad(lambda cp: cp.start())

    def sort_tile(x_ref):
        r_iota = lax.broadcasted_iota(jnp.int32, (TILE_ROWS, ROW_TILE), 0)
        hit = r_iota == pos_ref[0:1, :]
        for kk in range(1, TOP_K):
            hit = jnp.logical_or(hit, r_iota == pos_ref[kk:kk + 1, :])
        perm = jnp.where(hit, 1.0, 0.0).astype(BF16)
        xs = _dot(perm, x_ref[...].astype(BF16))

        @pl.when(i >= 2)
        def _():
            wait_tile(slot)

        _store_rows_as_tiles(sorted_scr.at[slot], xs)

    @pl.when(i < n_prompt)
    def _():
        sort_tile(xp_ref)

    @pl.when(i == n_prompt)
    def _():
        sort_tile(xs_ref)

    _for_each_run(tcnt_ref, i, lambda e, off, n: _run_copy(
        sorted_scr.at[slot], off, out_ref, tdst_ref[i * N_EXPERTS + e], n, sem.at[slot]).start())

    @pl.when(i == n_prompt)
    def _():
        wait_tile(1 - slot)
        wait_tile(slot)
        for_each_pad(lambda cp: cp.wait())


def _dispatch(tcnt, tdst, zrow, zcnt, nblk, pos, xn2_p, xn2_s, n_sorted):
    n_prompt = xn2_p.shape[0] // ROW_TILE
    grid_spec = pltpu.PrefetchScalarGridSpec(
        num_scalar_prefetch=5,
        grid=(n_prompt + 1,),
        in_specs=[
            pl.BlockSpec((TOP_K, ROW_TILE), lambda i, *_: (0, i)),
            pl.BlockSpec((ROW_TILE, D_MODEL), lambda i, *_: (jnp.minimum(i, n_prompt - 1), 0)),
            pl.BlockSpec((ROW_TILE, D_MODEL), lambda i, *_: (0, 0)),
        ],
        out_specs=pl.BlockSpec(memory_space=pl.ANY),
        scratch_shapes=[
            pltpu.VMEM((2, TILE_ROWS, ROW_CHUNKS, LANES), jnp.uint32),
            pltpu.VMEM((MOE_BLOCK, ROW_CHUNKS, LANES), jnp.uint32),
            pltpu.SemaphoreType.DMA((2,)), pltpu.SemaphoreType.DMA(()),
        ],
    )
    return pl.pallas_call(
        _dispatch_kernel,
        grid_spec=grid_spec,
        out_shape=jax.ShapeDtypeStruct((n_sorted, ROW_CHUNKS, LANES), jnp.uint32),
        compiler_params=pltpu.CompilerParams(dimension_semantics=("arbitrary",)),
        name="moe_dispatch",
    )(tcnt, tdst, zrow, zcnt, nblk, pos, xn2_p, xn2_s)


def _expert_kernel(be_ref, nblk_ref, eord_ref, enext_ref, x_ref, wgu_ref, bgu_ref, wd_ref, bd_ref, y_ref,
                   wgu_f32, wd_f32, wgu_bf, wd_bf, wsem):
    i = pl.program_id(0)
    live = i < nblk_ref[0]
    new_expert = jnp.logical_or(i == 0, be_ref[i] != be_ref[jnp.maximum(i - 1, 0)])

    def weight_copies(e, s):
        return (pltpu.make_async_copy(wgu_ref.at[e], wgu_f32.at[s], wsem.at[0, s]),
                pltpu.make_async_copy(wd_ref.at[e], wd_f32.at[s], wsem.at[1, s]))

    @pl.when(jnp.logical_and(live, new_expert))
    def _():
        slot = eord_ref[i] % 2

        @pl.when(i == 0)
        def _():
            for cp in weight_copies(be_ref[0], 0):
                cp.start()

        for cp in weight_copies(be_ref[i], slot):
            cp.wait()

        @pl.when(enext_ref[i] >= 0)
        def _():
            for cp in weight_copies(enext_ref[i], 1 - slot):
                cp.start()

        def cast(c, carry):
            rows = pl.ds(pl.multiple_of(c * WEIGHT_CAST_ROWS, WEIGHT_CAST_ROWS), WEIGHT_CAST_ROWS)
            wgu_bf[rows, :] = wgu_f32[slot, rows, :].astype(BF16)
            wd_bf[rows, :] = wd_f32[slot, rows, :].astype(BF16)
            return carry

        lax.fori_loop(0, D_MODEL // WEIGHT_CAST_ROWS, cast, 0)

    @pl.when(jnp.logical_not(live))
    def _():
        y_ref[...] = jnp.zeros_like(y_ref)

    @pl.when(live)
    def _():
        x = _load_rows_from_tiles(x_ref, MOE_BLOCK)
        h = _dot(x.astype(BF16), wgu_bf[...]) + bgu_ref[0]
        gate = jnp.minimum(h[:, :D_FF], SWIGLU_LIMIT)
        up = jnp.clip(h[:, D_FF:], -SWIGLU_LIMIT, SWIGLU_LIMIT)
        glu = gate * _sigmoid(gate * SWIGLU_ALPHA)
        y = _dot(((up + 1.0) * glu).astype(BF16), wd_bf[...]) + bd_ref[0]
        _store_rows_as_tiles(y_ref, y.astype(BF16).astype(F32))


def _experts(block_e, nblk, block_eord, block_enext, x_sorted, wgu, bgu, wd, bd):
    n_blocks = x_sorted.shape[0] // MOE_BLOCK
    wmap = lambda i, be, *_: (be[i], 0, 0)
    rmap = lambda i, *_: (i, 0, 0)
    hbm = pl.BlockSpec(memory_space=pl.ANY)
    grid_spec = pltpu.PrefetchScalarGridSpec(
        num_scalar_prefetch=4,
        grid=(n_blocks,),
        in_specs=[
            pl.BlockSpec((MOE_BLOCK, ROW_CHUNKS, LANES), rmap),
            hbm,
            pl.BlockSpec((1, 1, 2 * D_FF), wmap),
            hbm,
            pl.BlockSpec((1, 1, D_MODEL), wmap),
        ],
        out_specs=pl.BlockSpec((MOE_BLOCK, ROW_CHUNKS, LANES), rmap),
        scratch_shapes=[
            pltpu.VMEM((2, D_MODEL, 2 * D_FF), F32), pltpu.VMEM((2, D_FF, D_MODEL), F32),
            pltpu.VMEM((D_MODEL, 2 * D_FF), BF16), pltpu.VMEM((D_FF, D_MODEL), BF16),
            pltpu.SemaphoreType.DMA((2, 2)),
        ],
    )
    return pl.pallas_call(
        _expert_kernel,
        grid_spec=grid_spec,
        out_shape=jax.ShapeDtypeStruct(x_sorted.shape, jnp.uint32),
        compiler_params=pltpu.CompilerParams(
            dimension_semantics=("arbitrary",), vmem_limit_bytes=VMEM_LIMIT),
        name="moe_experts",
    )(block_e, nblk, block_eord, block_enext, x_sorted, wgu, bgu, wd, bd)


def _combine_kernel(tcnt_ref, tdst_ref, pos_ref, gate_ref, ys_ref, hp_ref, hs_ref, nf_ref, yp_ref, ysmp_ref,
                    runs_scr, sem):
    i = pl.program_id(0)
    n_tiles = pl.num_programs(0)
    n_prompt = n_tiles - 1
    slot = i % 2

    def start_runs(tile, s):
        _for_each_run(tcnt_ref, tile, lambda e, off, n: _run_copy(
            ys_ref, tdst_ref[tile * N_EXPERTS + e], runs_scr.at[s], off, n, sem.at[s]).start())

    @pl.when(i == 0)
    def _():
        start_runs(0, 0)

    @pl.when(i + 1 < n_tiles)
    def _():
        start_runs(i + 1, 1 - slot)

    pltpu.make_async_copy(ys_ref.at[pl.ds(0, TILE_ROWS)], runs_scr.at[slot], sem.at[slot]).wait()
    ys = _load_rows_from_tiles(runs_scr.at[slot], TILE_ROWS).astype(BF16)
    r_iota = lax.broadcasted_iota(jnp.int32, (TILE_ROWS, ROW_TILE), 0)
    gmat = jnp.zeros((TILE_ROWS, ROW_TILE), F32)
    for kk in range(TOP_K):
        gmat = jnp.where(r_iota == pos_ref[kk:kk + 1, :], gate_ref[kk:kk + 1, :], gmat)
    g_hi = gmat.astype(BF16)
    g_lo = (gmat - g_hi.astype(F32)).astype(BF16)
    contract0 = (((0,), (0,)), ((), ()))
    moe = (lax.dot_general(g_hi, ys, contract0, preferred_element_type=F32)
           + lax.dot_general(g_lo, ys, contract0, preferred_element_type=F32))

    @pl.when(i < n_prompt)
    def _():
        yp_ref[...] = _rmsnorm(hp_ref[...] + moe, nf_ref[...])

    @pl.when(i == n_prompt)
    def _():
        ysmp_ref[...] = _rmsnorm(hs_ref[...] + moe, nf_ref[...])


def _combine(tcnt, tdst, pos, gates, y_sorted, h1_p, h1_s, nf):
    n_prompt = h1_p.shape[0] // ROW_TILE
    pmap = lambda i, *_: (jnp.minimum(i, n_prompt - 1), 0)
    smap = lambda i, *_: (0, 0)
    lmap = lambda i, *_: (0, i)
    grid_spec = pltpu.PrefetchScalarGridSpec(
        num_scalar_prefetch=2,
        grid=(n_prompt + 1,),
        in_specs=[
            pl.BlockSpec((TOP_K, ROW_TILE), lmap),
            pl.BlockSpec((TOP_K, ROW_TILE), lmap),
            pl.BlockSpec(memory_space=pl.ANY),
            pl.BlockSpec((ROW_TILE, D_MODEL), pmap),
            pl.BlockSpec((ROW_TILE, D_MODEL), smap),
            pl.BlockSpec((1, D_MODEL), smap),
        ],
        out_specs=[pl.BlockSpec((ROW_TILE, D_MODEL), pmap), pl.BlockSpec((ROW_TILE, D_MODEL), smap)],
        scratch_shapes=[pltpu.VMEM((2, TILE_ROWS, ROW_CHUNKS, LANES), jnp.uint32), pltpu.SemaphoreType.DMA((2,))],
    )
    return pl.pallas_call(
        _combine_kernel,
        grid_spec=grid_spec,
        out_shape=[jax.ShapeDtypeStruct(h1_p.shape, F32), jax.ShapeDtypeStruct(h1_s.shape, F32)],
        compiler_params=pltpu.CompilerParams(dimension_semantics=("arbitrary",)),
        name="moe_combine",
    )(tcnt, tdst, pos, gates, y_sorted, h1_p, h1_s, nf)


def _rotary_tables(pos):
    f = np.float32
    inv = np.power(f(ROPE_BASE), -np.arange(0, RET_DK, 2, dtype=f) / f(RET_DK)).astype(f)
    ang = (np.asarray(pos, f)[:, None] * inv[None, :]).astype(f)
    cos, sin = np.cos(ang).astype(f), np.sin(ang).astype(f)
    return np.concatenate([cos, cos], axis=1), np.concatenate([-sin, sin], axis=1)


def _decay_tables(chunk):
    f = np.float32
    log_g = np.log1p(-np.exp2(f(-5.0) - np.arange(RET_HEADS, dtype=f))).astype(f)
    i = np.arange(chunk, dtype=f)
    diff = i[:, None] - i[None, :]
    mask = np.where(diff[None] >= 0, np.exp(np.maximum(diff, f(0.0))[None] * log_g[:, None, None]), f(0.0)).astype(f)
    q_dec = np.exp((i + f(1.0))[None, :] * log_g[:, None]).astype(f)
    k_dec = np.exp((f(chunk) - f(1.0) - i)[None, :] * log_g[:, None]).astype(f)
    c_dec = np.exp(f(chunk) * log_g).astype(f)
    return mask, q_dec, k_dec, c_dec


def kernel(x_prompt, x_sample, state_ret, state_pool, meta_tokens, norm_mix, w_in, ret_gn, pool_w, pool_scale,
           w_ret_branch, w_pool_branch, w_out, norm_ffn, w_router, b_router, w_gate_up, b_gate_up, w_down, b_down,
           norm_final):
    batch, seq, _ = x_prompt.shape
    nb = x_sample.shape[0]
    past_len = 16384
    n_prompt_tok = batch * seq
    n_tok = n_prompt_tok + nb

    w_in_bf = w_in[0].astype(BF16)
    wts = (ret_gn[0][None, :], pool_w[0].astype(BF16), pool_scale[0][None, :],
           w_ret_branch[0].astype(BF16), w_pool_branch[0].astype(BF16), w_out[0].astype(BF16),
           norm_ffn[0][None, :], w_router[0].T.astype(BF16), b_router[0][:, None])
    wgu = w_gate_up[0]
    wd = w_down[0]
    bgu = b_gate_up[0][:, None, :]
    bd = b_down[0][:, None, :]
    nmix = norm_mix[0][None, :]

    cos_p, sin_p = _rotary_tables(N_META + np.arange(seq))
    cos_s, sin_s = _rotary_tables(np.concatenate([np.arange(N_META), np.full((nb,), past_len)]))
    mask, q_dec, k_dec, c_dec = _decay_tables(CHUNK)
    dec = (mask,
           np.ascontiguousarray(np.broadcast_to(q_dec[:, :, None], (RET_HEADS, CHUNK, RET_DK))),
           np.ascontiguousarray(np.broadcast_to(k_dec[:, :, None], (RET_HEADS, CHUNK, RET_DK))),
           np.ascontiguousarray(np.broadcast_to(c_dec[:, None, None], (RET_HEADS, 1, RET_DV))))
    m1, q1, k1, c1 = _decay_tables(1)
    sdec = np.stack([m1[:, 0, 0], q1[:, 0], k1[:, 0], c1], axis=1)

    x2d = x_prompt.reshape(n_prompt_tok, D_MODEL)
    proj_p = _inproj(x2d, nmix, w_in_bf, cos_p, sin_p, INPROJ_TILE, batch, BF16)
    xs2d = x_sample.reshape(nb, D_MODEL)
    x_small = jnp.concatenate([meta_tokens, xs2d], axis=0)
    proj_small = _inproj(x_small, nmix, w_in_bf, cos_s, sin_s, N_META + nb, 1, F32)
    proj_s = tuple(a[N_META:] for a in proj_small)

    lead = CHUNK - N_META
    kmeta = jnp.pad(proj_small[1][:N_META], ((lead, 0), (0, 0))).astype(BF16)
    vmeta = jnp.pad(proj_small[2][:N_META], ((lead, 0), (0, 0))).astype(BF16)
    pmeta = proj_small[4][:N_META]

    grp = SAMPLE_GROUP

    def cols(a):
        return a.T.reshape(RET_QK, nb // grp, grp).transpose(1, 0, 2)

    pool2d = state_pool[0].reshape(nb, POOL_BUF * POOL_WIDTH)
    (st_s, pool_s, h1_s, xn2_s, idx_s, gate_s, rank_s, cnt_s) = _sample_mixer(
        sdec, cols(proj_s[0]), cols(proj_s[1]), proj_s, xs2d, state_ret[0], pool2d, wts)

    (h1_p, xn2_p, idx_p, gate_p, rank_p, cnt, s_fin, p_fin) = _mixer(
        proj_p, x2d, kmeta, vmeta, pmeta, dec, wts, cnt_s, batch, seq)

    assert n_tok % ROW_TILE == 0 and nb == ROW_TILE
    n_tiles = n_tok // ROW_TILE
    i32 = jnp.int32
    counts = cnt[:, 0].astype(i32)
    padded = ((counts + MOE_BLOCK - 1) // MOE_BLOCK) * MOE_BLOCK
    pad_end = jnp.cumsum(padded)
    pad_start = pad_end - padded
    n_blocks = (n_tok * TOP_K) // MOE_BLOCK + N_EXPERTS
    block_row = jnp.arange(n_blocks, dtype=i32) * MOE_BLOCK
    block_e = jnp.minimum(jnp.sum((pad_end[None, :] <= block_row[:, None]).astype(i32), axis=1), N_EXPERTS - 1)
    nblk = (pad_end[-1:] // MOE_BLOCK).astype(i32)
    e_row = jnp.arange(N_EXPERTS, dtype=i32)
    used = padded > 0
    e_ord = jnp.cumsum(used.astype(i32)) - 1
    later_used = jnp.logical_and(e_row[None, :] > e_row[:, None], used[None, :])
    e_next = jnp.min(jnp.where(later_used, e_row[None, :], N_EXPERTS), axis=1)
    e_next = jnp.where(e_next == N_EXPERTS, -1, e_next)
    of_block = block_e[:, None] == e_row[None, :]
    block_eord = jnp.sum(jnp.where(of_block, e_ord[None, :], 0), axis=1)
    block_enext = jnp.sum(jnp.where(of_block, e_next[None, :], 0), axis=1)
    idx = jnp.concatenate([idx_p, idx_s], axis=1)
    rank = jnp.concatenate([rank_p, rank_s], axis=1)
    gates = jnp.concatenate([gate_p, gate_s], axis=1)
    onehot = idx[None] == jnp.arange(N_EXPERTS, dtype=i32)[:, None, None]
    tile_cnt = jnp.sum(onehot.reshape(N_EXPERTS, TOP_K, n_tiles, ROW_TILE).astype(i32), axis=(1, 3)).T
    by_time = jnp.concatenate([tile_cnt[-1:], tile_cnt[:-1]], axis=0)
    before_time = jnp.cumsum(by_time, axis=0) - by_time
    run_before = jnp.concatenate([before_time[1:], before_time[:1]], axis=0)
    tile_off = jnp.cumsum(tile_cnt, axis=1) - tile_cnt
    tile_dst = pad_start[None, :] + run_before
    delta = jnp.repeat((tile_off - run_before).T, ROW_TILE, axis=1)
    pos = rank + jnp.sum(jnp.where(onehot, delta[:, None, :], 0), axis=0)
    tcnt, tdst = tile_cnt.reshape(-1), tile_dst.reshape(-1)

    x_sorted = _dispatch(tcnt, tdst, pad_start + counts, padded - counts, nblk, pos, xn2_p, xn2_s,
                         n_blocks * MOE_BLOCK)
    y_sorted = _experts(block_e, nblk, block_eord, block_enext, x_sorted, wgu, bgu, wd, bd)
    y_p, y_s = _combine(tcnt, tdst, pos, gates, y_sorted, h1_p, h1_s, norm_final[None, :])

    y_prompt = y_p.reshape(batch, seq, D_MODEL)
    y_sample = y_s.reshape(nb, 1, D_MODEL)
    ret_state_prompt = s_fin[None]
    pool_state_prompt = p_fin[:, 1:, :][None]
    ret_state_sample = st_s[None]
    pool_state_sample = pool_s.reshape(nb, POOL_BUF, POOL_WIDTH)[None]
    return (y_prompt, y_sample, ret_state_prompt, pool_state_prompt, ret_state_sample, pool_state_sample)
```

```python
import functools

import jax
import jax.numpy as jnp
import numpy as np
from jax import lax
from jax.experimental import pallas as pl
from jax.experimental.pallas import tpu as pltpu

F32 = jnp.float32
BF16 = jnp.bfloat16

D_MODEL = 1024
N_META = 16
RET_HEADS = 4
RET_DK = 128
RET_DV = 256
RET_QK = RET_HEADS * RET_DK
RET_V = RET_HEADS * RET_DV
CHUNK = 128
ROPE_BASE = 10000.0
POOL_WINDOWS = (2, 4, 8, 16)
POOL_GROUPS = 4
POOL_GROUP_DIM = 128
POOL_WIDTH = POOL_GROUPS * POOL_GROUP_DIM
POOL_BUF = max(POOL_WINDOWS) - 1
N_EXPERTS = 32
TOP_K = 4
D_FF = D_MODEL
SWIGLU_LIMIT = 7.0
SWIGLU_ALPHA = 1.702
EPS = 1e-6
IN_WIDTHS = (RET_QK, RET_QK, RET_V, RET_V, POOL_WIDTH, D_MODEL, D_MODEL)
IN_TOTAL = sum(IN_WIDTHS)
IN_OFFS = tuple(int(s) for s in np.cumsum((0,) + IN_WIDTHS))

LANES = 128
ROW_CHUNKS = D_MODEL // (2 * LANES)
INPROJ_TILE = 512
MIXER_TILE = 512
MOE_BLOCK = 512
ROW_TILE = 128
SAMPLE_GROUP = 8
WEIGHT_CAST_ROWS = 128
VMEM_LIMIT = 56 * 1024 * 1024

assert N_META + 1 >= max(POOL_WINDOWS)
assert POOL_WINDOWS == (2, 4, 8, 16)


def _dot(a, b):
    return jnp.dot(a, b, preferred_element_type=F32)


def _rmsnorm(x, w):
    return x * lax.rsqrt(jnp.mean(x * x, axis=-1, keepdims=True) + EPS) * w


def _sigmoid(x):
    return 0.5 * jnp.tanh(0.5 * x) + 0.5


def _store_rows_as_tiles(ref, x):
    half = D_MODEL // 2
    hi = pltpu.bitcast(x[:, :half], jnp.uint32) & jnp.uint32(0xFFFF0000)
    lo = pltpu.bitcast(x[:, half:], jnp.uint32) >> 16
    ref[...] = (hi | lo).reshape(x.shape[0], ROW_CHUNKS, LANES)


def _load_rows_from_tiles(ref, rows):
    w = ref[...].reshape(rows, D_MODEL // 2)
    hi = pltpu.bitcast(w & jnp.uint32(0xFFFF0000), F32)
    lo = pltpu.bitcast(w << 16, F32)
    return jnp.concatenate([hi, lo], axis=1)


def _inproj_kernel(x_ref, nw_ref, w_ref, cos_ref, sin_ref,
                   q_ref, k_ref, v_ref, g_ref, p_ref, ga_ref, gb_ref):
    xn = _rmsnorm(x_ref[...], nw_ref[...]).astype(BF16)
    cos = cos_ref[...]
    sin = sin_ref[...]

    def seg(i):
        return _dot(xn, w_ref[:, IN_OFFS[i]:IN_OFFS[i + 1]])

    def rot(a):
        return a * cos + pltpu.roll(a, RET_DK // 2, 1) * sin

    q = seg(0)
    k = seg(1)
    for h in range(RET_HEADS):
        sl = slice(h * RET_DK, (h + 1) * RET_DK)
        q_ref[:, sl] = rot(q[:, sl]).astype(q_ref.dtype)
        k_ref[:, sl] = (rot(k[:, sl]) * (RET_DK ** -0.5)).astype(k_ref.dtype)
    v_ref[...] = seg(2).astype(v_ref.dtype)
    g_ref[...] = seg(3).astype(g_ref.dtype)
    p_ref[...] = seg(4)
    ga_ref[...] = seg(5).astype(ga_ref.dtype)
    gb_ref[...] = seg(6).astype(gb_ref.dtype)


def _inproj(x2d, nw, w_in_bf, cosf, sinf, tile, n_outer, act_dtype):
    rows = x2d.shape[0]
    n_inner = rows // (tile * n_outer)
    row_map = lambda b, j: (b * n_inner + j, 0)
    tab_map = lambda b, j: (j, 0)
    const = lambda b, j: (0, 0)
    widths = IN_WIDTHS
    dts = (act_dtype, act_dtype, act_dtype, act_dtype, F32, act_dtype, act_dtype)
    return pl.pallas_call(
        _inproj_kernel,
        grid=(n_outer, n_inner),
        in_specs=[
            pl.BlockSpec((tile, D_MODEL), row_map),
            pl.BlockSpec((1, D_MODEL), const),
            pl.BlockSpec((D_MODEL, IN_TOTAL), const, pipeline_mode=pl.Buffered(1)),
            pl.BlockSpec((tile, RET_DK), tab_map),
            pl.BlockSpec((tile, RET_DK), tab_map),
        ],
        out_specs=[pl.BlockSpec((tile, w), row_map) for w in widths],
        out_shape=[jax.ShapeDtypeStruct((rows, w), dt) for w, dt in zip(widths, dts)],
        compiler_params=pltpu.CompilerParams(
            dimension_semantics=("arbitrary", "arbitrary"), vmem_limit_bytes=VMEM_LIMIT),
        name="inproj",
    )(x2d, nw, w_in_bf, cosf, sinf)


def _group_norm(o, gn_row):
    mu = jnp.mean(o, axis=-1, keepdims=True)
    var = jnp.mean(jnp.square(o - mu), axis=-1, keepdims=True)
    return (o - mu) * lax.rsqrt(var + EPS) * gn_row


def _pool_branch(groups, poolw_ref, pscale_ref, wpool_ref):
    pm = [_dot(g.astype(BF16), poolw_ref[i]) for i, g in enumerate(groups)]
    pm = jnp.concatenate(pm, axis=1) * pscale_ref[...]
    return _dot(pm.astype(BF16), wpool_ref[...])


def _merge_tail(o_norm, g, ga, gb, yb, x, wret_ref, wout_ref):
    gf = g.astype(F32)
    ya = _dot((gf * _sigmoid(gf) * o_norm).astype(BF16), wret_ref[...])
    merged = _sigmoid(ga.astype(F32)) * ya + _sigmoid(gb.astype(F32)) * yb
    return x + _dot(merged.astype(BF16), wout_ref[...])


def _route(h1, nffn_ref, wrt_ref, br_ref, run_scr,
           xn2_ref, idx_ref, gate_ref, rank_ref):
    tm = h1.shape[0]
    xn2 = _rmsnorm(h1, nffn_ref[...])
    xn2_ref[...] = xn2
    logits = lax.dot_general(wrt_ref[...], xn2.astype(BF16), (((1,), (1,)), ((), ())),
                             preferred_element_type=F32) + br_ref[...]
    e_iota = lax.broadcasted_iota(jnp.int32, (N_EXPERTS, tm), 0)
    work = logits
    vals, sels = [], []
    chosen = jnp.zeros((N_EXPERTS, tm), F32)
    for _ in range(TOP_K):
        m = jnp.max(work, axis=0, keepdims=True)
        sel = jnp.min(jnp.where(work == m, e_iota, N_EXPERTS), axis=0, keepdims=True)
        hit = e_iota == sel
        vals.append(m)
        sels.append(sel)
        chosen = jnp.where(hit, 1.0, chosen)
        work = jnp.where(hit, -jnp.inf, work)
    exps = [jnp.exp(v - vals[0]) for v in vals]
    denom = exps[0] + exps[1] + exps[2] + exps[3]
    gates = [e / denom for e in exps]
    r_i = lax.broadcasted_iota(jnp.int32, (tm, tm), 0)
    c_i = lax.broadcasted_iota(jnp.int32, (tm, tm), 1)
    before = jnp.where(r_i < c_i, 1.0, 0.0).astype(BF16)
    base = run_scr[...] + _dot(chosen.astype(BF16), before)
    for kk in range(TOP_K):
        rk = jnp.sum(jnp.where(e_iota == sels[kk], base, 0.0), axis=0, keepdims=True)
        rank_ref[kk:kk + 1, :] = rk.astype(jnp.int32)
        idx_ref[kk:kk + 1, :] = sels[kk]
        gate_ref[kk:kk + 1, :] = gates[kk]
    run_scr[...] = run_scr[...] + jnp.sum(chosen, axis=1, keepdims=True)


def _mixer_kernel(q_ref, k_ref, v_ref, g_ref, p_ref, ga_ref, gb_ref, x_ref,
                  kmeta_ref, vmeta_ref, pmeta_ref, mask_ref, qdec_ref, kdec_ref, cdec_ref, gn_ref,
                  poolw_ref, pscale_ref, wret_ref, wpool_ref, wout_ref, nffn_ref, wrt_ref, br_ref,
                  cnt0_ref,
                  h1_ref, xn2_ref, idx_ref, gate_ref, rank_ref, cnt_ref, sfin_ref, pfin_ref,
                  s_scr, ext_scr, o_scr, run_scr):
    b = pl.program_id(0)
    j = pl.program_id(1)
    nj = pl.num_programs(1)
    tm = q_ref.shape[0]

    def state_update(s_old, kc, vc, h):
        kd = (kc.astype(F32) * kdec_ref[h]).astype(BF16)
        upd = lax.dot_general(kd, vc, (((0,), (0,)), ((), ())), preferred_element_type=F32)
        return s_old * cdec_ref[h] + upd

    @pl.when(jnp.logical_and(b == 0, j == 0))
    def _():
        run_scr[...] = cnt0_ref[:, 0:1]

    @pl.when(j == 0)
    def _():
        for h in range(RET_HEADS):
            kc = kmeta_ref[:, h * RET_DK:(h + 1) * RET_DK]
            vc = vmeta_ref[:, h * RET_DV:(h + 1) * RET_DV]
            s_scr[h] = state_update(jnp.zeros((RET_DK, RET_DV), F32), kc, vc, h)
        ext_scr[0:N_META, :] = pmeta_ref[...]

    for c in range(tm // CHUNK):
        rows = slice(c * CHUNK, (c + 1) * CHUNK)
        for h in range(RET_HEADS):
            qc = q_ref[rows, h * RET_DK:(h + 1) * RET_DK]
            kc = k_ref[rows, h * RET_DK:(h + 1) * RET_DK]
            vc = v_ref[rows, h * RET_DV:(h + 1) * RET_DV]
            s_old = s_scr[h]
            scores = lax.dot_general(qc, kc, (((1,), (1,)), ((), ())),
                                     preferred_element_type=F32) * mask_ref[h]
            qd = (qc.astype(F32) * qdec_ref[h]).astype(BF16)
            lhs = jnp.concatenate([scores.astype(BF16), qd], axis=1)
            rhs = jnp.concatenate([vc, s_old.astype(BF16)], axis=0)
            o = _dot(lhs, rhs)
            s_scr[h] = state_update(s_old, kc, vc, h)
            o_scr[rows, h * RET_DV:(h + 1) * RET_DV] = _group_norm(
                o, gn_ref[:, h * RET_DV:(h + 1) * RET_DV])

    p = p_ref[...]
    ext_scr[N_META:N_META + tm, :] = p
    a = ext_scr[...]
    g1 = POOL_GROUP_DIM
    s2 = a + pltpu.roll(a, 1, 0)
    s4 = s2[:, g1:] + pltpu.roll(s2[:, g1:], 2, 0)
    s8 = s4[:, g1:] + pltpu.roll(s4[:, g1:], 4, 0)
    s16 = s8[:, g1:] + pltpu.roll(s8[:, g1:], 8, 0)
    sums = (s2[N_META:, :g1], s4[N_META:, :g1], s8[N_META:, :g1], s16[N_META:, :])
    groups = [sums[i] * (1.0 / POOL_WINDOWS[i]) - p[:, i * g1:(i + 1) * g1] for i in range(POOL_GROUPS)]
    ext_scr[0:N_META, :] = ext_scr[tm:tm + N_META, :]

    yb = _pool_branch(groups, poolw_ref, pscale_ref, wpool_ref)
    h1 = _merge_tail(o_scr[...], g_ref[...], ga_ref[...], gb_ref[...], yb, x_ref[...], wret_ref, wout_ref)
    h1_ref[...] = h1
    _route(h1, nffn_ref, wrt_ref, br_ref, run_scr, xn2_ref, idx_ref, gate_ref, rank_ref)
    cnt_ref[...] = jnp.broadcast_to(run_scr[...], cnt_ref.shape)

    @pl.when(j == nj - 1)
    def _():
        for h in range(RET_HEADS):
            sfin_ref[0, h] = s_scr[h]
        pfin_ref[0] = ext_scr[0:N_META, :]


def _mixer(proj, x2d, kmeta, vmeta, pmeta, dec, wts, cnt0, batch, seq):
    q, k, v, g, p, ga, gb = proj
    tm = MIXER_TILE
    nj = seq // tm
    rows = batch * seq
    row_map = lambda b, j: (b * nj + j, 0)
    lane_map = lambda b, j: (0, b * nj + j)
    c2 = lambda b, j: (0, 0)
    c3 = lambda b, j: (0, 0, 0)

    def whole(a):
        return pl.BlockSpec(a.shape, c2 if a.ndim == 2 else c3)

    mask, qdec, kdec, cdec = dec
    gn, poolw, pscale, wret, wpool, wout, nffn, wrt, br = wts
    in_arrays = [q, k, v, g, p, ga, gb, x2d, kmeta, vmeta, pmeta, mask, qdec, kdec, cdec, gn,
                 poolw, pscale, wret, wpool, wout, nffn, wrt, br, cnt0]
    in_specs = [pl.BlockSpec((tm, a.shape[1]), row_map) for a in in_arrays[:8]]
    in_specs += [whole(a) for a in in_arrays[8:]]
    out_shape = [
        jax.ShapeDtypeStruct((rows, D_MODEL), F32),
        jax.ShapeDtypeStruct((rows, D_MODEL), F32),
        jax.ShapeDtypeStruct((TOP_K, rows), jnp.int32),
        jax.ShapeDtypeStruct((TOP_K, rows), F32),
        jax.ShapeDtypeStruct((TOP_K, rows), jnp.int32),
        jax.ShapeDtypeStruct((N_EXPERTS, LANES), F32),
        jax.ShapeDtypeStruct((batch, RET_HEADS, RET_DK, RET_DV), F32),
        jax.ShapeDtypeStruct((batch, N_META, POOL_WIDTH), F32),
    ]
    out_specs = [
        pl.BlockSpec((tm, D_MODEL), row_map),
        pl.BlockSpec((tm, D_MODEL), row_map),
        pl.BlockSpec((TOP_K, tm), lane_map),
        pl.BlockSpec((TOP_K, tm), lane_map),
        pl.BlockSpec((TOP_K, tm), lane_map),
        pl.BlockSpec((N_EXPERTS, LANES), c2),
        pl.BlockSpec((1, RET_HEADS, RET_DK, RET_DV), lambda b, j: (b, 0, 0, 0)),
        pl.BlockSpec((1, N_META, POOL_WIDTH), lambda b, j: (b, 0, 0)),
    ]
    return pl.pallas_call(
        _mixer_kernel,
        grid=(batch, nj),
        in_specs=in_specs,
        out_specs=out_specs,
        out_shape=out_shape,
        scratch_shapes=[
            pltpu.VMEM((RET_HEADS, RET_DK, RET_DV), F32),
            pltpu.VMEM((N_META + tm, POOL_WIDTH), F32),
            pltpu.VMEM((tm, RET_V), F32),
            pltpu.VMEM((N_EXPERTS, 1), F32),
        ],
        compiler_params=pltpu.CompilerParams(
            dimension_semantics=("arbitrary", "arbitrary"), vmem_limit_bytes=VMEM_LIMIT),
        name="mixer",
    )(*in_arrays)


def _sample_kernel(sdec_ref, qt_ref, kt_ref, q_ref, k_ref, v_ref, g_ref, p_ref, ga_ref, gb_ref, x_ref,
                   st_ref, pool_ref, gn_ref,
                   poolw_ref, pscale_ref, wret_ref, wpool_ref, wout_ref, nffn_ref, wrt_ref, br_ref,
                   stout_ref, poolout_ref, h1_ref, xn2_ref, idx_ref, gate_ref, rank_ref,
                   cnt_ref,
                   o_scr, run_scr):
    i = pl.program_id(0)
    n = pl.num_programs(0)
    grp = st_ref.shape[0]
    row0 = pl.multiple_of(i * grp, grp)

    @pl.when(i == 0)
    def _():
        run_scr[...] = jnp.zeros_like(run_scr)

    q8 = q_ref[pl.ds(row0, grp), :]
    k8 = k_ref[pl.ds(row0, grp), :]
    v8 = v_ref[pl.ds(row0, grp), :]
    for h in range(RET_HEADS):
        ksl = slice(h * RET_DK, (h + 1) * RET_DK)
        vsl = slice(h * RET_DV, (h + 1) * RET_DV)
        score = jnp.sum(q8[:, ksl] * k8[:, ksl], axis=1, keepdims=True) * sdec_ref[h, 0]
        intra = score * v8[:, vsl]
        for bb in range(grp):
            s_old = st_ref[bb, h]
            qcol = qt_ref[0, ksl, bb:bb + 1] * sdec_ref[h, 1]
            kcol = kt_ref[0, ksl, bb:bb + 1] * sdec_ref[h, 2]
            cross = jnp.sum(s_old * qcol, axis=0, keepdims=True)
            o_scr[pl.ds(row0 + bb, 1), vsl] = intra[bb:bb + 1, :] + cross
            stout_ref[bb, h] = s_old * sdec_ref[h, 3] + kcol * v8[bb:bb + 1, vsl]

    @pl.when(i == n - 1)
    def _():
        o = o_scr[...]
        o_norm = jnp.concatenate(
            [_group_norm(o[:, h * RET_DV:(h + 1) * RET_DV], gn_ref[:, h * RET_DV:(h + 1) * RET_DV])
             for h in range(RET_HEADS)], axis=1)
        p = p_ref[...]
        w = POOL_WIDTH
        g1 = POOL_GROUP_DIM

        def prev(r, lo):
            return pool_ref[:, r * w + lo:(r + 1) * w]

        s2 = p + prev(14, 0)
        s4 = s2[:, g1:] + prev(13, g1) + prev(12, g1)
        s8 = s4[:, g1:] + prev(11, 2 * g1) + prev(10, 2 * g1) + prev(9, 2 * g1) + prev(8, 2 * g1)
        s16 = s8[:, g1:]
        for r in range(7, -1, -1):
            s16 = s16 + prev(r, 3 * g1)
        sums = (s2[:, :g1], s4[:, :g1], s8[:, :g1], s16)
        groups = [sums[t] * (1.0 / POOL_WINDOWS[t]) - p[:, t * g1:(t + 1) * g1] for t in range(POOL_GROUPS)]
        poolout_ref[:, 0:(POOL_BUF - 1) * w] = pool_ref[:, w:POOL_BUF * w]
        poolout_ref[:, (POOL_BUF - 1) * w:] = p
        yb = _pool_branch(groups, poolw_ref, pscale_ref, wpool_ref)
        h1 = _merge_tail(o_norm, g_ref[...], ga_ref[...], gb_ref[...], yb, x_ref[...], wret_ref, wout_ref)
        h1_ref[...] = h1
        _route(h1, nffn_ref, wrt_ref, br_ref, run_scr, xn2_ref, idx_ref, gate_ref, rank_ref)
        cnt_ref[...] = jnp.broadcast_to(run_scr[...], cnt_ref.shape)


def _sample_mixer(sdec, qt, kt, proj, x2d, state, pool2d, wts):
    q, k, v, g, p, ga, gb = proj
    nb = x2d.shape[0]
    grp = SAMPLE_GROUP
    c2 = lambda i: (0, 0)

    def whole(a):
        return pl.BlockSpec(a.shape, c2)

    gn, poolw, pscale, wret, wpool, wout, nffn, wrt, br = wts
    in_arrays = [sdec, qt, kt, q, k, v, g, p, ga, gb, x2d, state, pool2d, gn,
                 poolw, pscale, wret, wpool, wout, nffn, wrt, br]
    in_specs = [pl.BlockSpec(memory_space=pltpu.SMEM),
                pl.BlockSpec((1, RET_QK, grp), lambda i: (i, 0, 0)),
                pl.BlockSpec((1, RET_QK, grp), lambda i: (i, 0, 0))]
    in_specs += [whole(a) for a in (q, k, v, g, p, ga, gb, x2d)]
    in_specs += [pl.BlockSpec((grp, RET_HEADS, RET_DK, RET_DV), lambda i: (i, 0, 0, 0)), whole(pool2d), whole(gn),
                 pl.BlockSpec(poolw.shape, lambda i: (0, 0, 0))]
    in_specs += [whole(a) for a in (pscale, wret, wpool, wout, nffn, wrt, br)]
    out_shape = [
        jax.ShapeDtypeStruct(state.shape, F32),
        jax.ShapeDtypeStruct(pool2d.shape, F32),
        jax.ShapeDtypeStruct((nb, D_MODEL), F32),
        jax.ShapeDtypeStruct((nb, D_MODEL), F32),
        jax.ShapeDtypeStruct((TOP_K, nb), jnp.int32),
        jax.ShapeDtypeStruct((TOP_K, nb), F32),
        jax.ShapeDtypeStruct((TOP_K, nb), jnp.int32),
        jax.ShapeDtypeStruct((N_EXPERTS, LANES), F32),
    ]
    out_specs = [pl.BlockSpec((grp, RET_HEADS, RET_DK, RET_DV), lambda i: (i, 0, 0, 0))]
    out_specs += [pl.BlockSpec(s.shape, c2) for s in out_shape[1:]]
    return pl.pallas_call(
        _sample_kernel,
        grid=(nb // grp,),
        in_specs=in_specs,
        out_specs=out_specs,
        out_shape=out_shape,
        scratch_shapes=[pltpu.VMEM((nb, RET_V), F32), pltpu.VMEM((N_EXPERTS, 1), F32)],
        compiler_params=pltpu.CompilerParams(
            dimension_semantics=("arbitrary",), vmem_limit_bytes=VMEM_LIMIT),
        name="sample_mixer",
    )(*in_arrays)


TILE_ROWS = ROW_TILE * TOP_K


def _run_copy(src_ref, src_row, dst_ref, dst_row, n_rows, sem):
    return pltpu.make_async_copy(src_ref.at[pl.ds(src_row, n_rows)], dst_ref.at[pl.ds(dst_row, n_rows)], sem)


def _for_each_run(tcnt_ref, tile, fn):
    def body(e, off):
        n = tcnt_ref[tile * N_EXPERTS + e]

        @pl.when(n > 0)
        def _():
            fn(e, off, n)

        return off + n

    lax.fori_loop(0, N_EXPERTS, body, 0)


def _dispatch_kernel(tcnt_ref, tdst_ref, zrow_ref, zcnt_ref, nblk_ref, pos_ref, xp_ref, xs_ref, out_ref,
                     sorted_scr, zero_scr, sem, zsem):
    i = pl.program_id(0)
    n_prompt = pl.num_programs(0) - 1
    n_blocks = out_ref.shape[0] // MOE_BLOCK
    slot = i % 2

    def wait_tile(s):
        pltpu.make_async_copy(out_ref.at[pl.ds(0, TILE_ROWS)], sorted_scr.at[s], sem.at[s]).wait()

    def for_each_pad(fn):
        def body(e, carry):
            n = zcnt_ref[e]

            @pl.when(n > 0)
            def _():
                fn(_run_copy(zero_scr, 0, out_ref, zrow_ref[e], n, zsem))

            dead = nblk_ref[0] + e

            @pl.when(dead < n_blocks)
            def _():
                fn(_run_copy(zero_scr, 0, out_ref, dead * MOE_BLOCK, MOE_BLOCK, zsem))

            return carry

        lax.fori_loop(0, N_EXPERTS, body, 0)

    @pl.when(i == 0)
    def _():
        zero_scr[...] = jnp.zeros_like(zero_scr)
        for_each_pad(lambda cp: cp.start())

    def sort_tile(x_ref):
        r_iota = lax.broadcasted_iota(jnp.int32, (TILE_ROWS, ROW_TILE), 0)
        hit = r_iota == pos_ref[0:1, :]
        for kk in range(1, TOP_K):
            hit = jnp.logical_or(hit, r_iota == pos_ref[kk:kk + 1, :])
        perm = jnp.where(hit, 1.0, 0.0).astype(BF16)
        xs = _dot(perm, x_ref[...].astype(BF16))

        @pl.when(i >= 2)
        def _():
            wait_tile(slot)

        _store_rows_as_tiles(sorted_scr.at[slot], xs)

    @pl.when(i < n_prompt)
    def _():
        sort_tile(xp_ref)

    @pl.when(i == n_prompt)
    def _():
        sort_tile(xs_ref)

    _for_each_run(tcnt_ref, i, lambda e, off, n: _run_copy(
        sorted_scr.at[slot], off, out_ref, tdst_ref[i * N_EXPERTS + e], n, sem.at[slot]).start())

    @pl.when(i == n_prompt)
    def _():
        wait_tile(1 - slot)
        wait_tile(slot)
        for_each_pad(lambda cp: cp.wait())


def _dispatch(tcnt, tdst, zrow, zcnt, nblk, pos, xn2_p, xn2_s, n_sorted):
    n_prompt = xn2_p.shape[0] // ROW_TILE
    grid_spec = pltpu.PrefetchScalarGridSpec(
        num_scalar_prefetch=5,
        grid=(n_prompt + 1,),
        in_specs=[
            pl.BlockSpec((TOP_K, ROW_TILE), lambda i, *_: (0, i)),
            pl.BlockSpec((ROW_TILE, D_MODEL), lambda i, *_: (jnp.minimum(i, n_prompt - 1), 0)),
            pl.BlockSpec((ROW_TILE, D_MODEL), lambda i, *_: (0, 0)),
        ],
        out_specs=pl.BlockSpec(memory_space=pl.ANY),
        scratch_shapes=[
            pltpu.VMEM((2, TILE_ROWS, ROW_CHUNKS, LANES), jnp.uint32),
            pltpu.VMEM((MOE_BLOCK, ROW_CHUNKS, LANES), jnp.uint32),
            pltpu.SemaphoreType.DMA((2,)), pltpu.SemaphoreType.DMA(()),
        ],
    )
    return pl.pallas_call(
        _dispatch_kernel,
        grid_spec=grid_spec,
        out_shape=jax.ShapeDtypeStruct((n_sorted, ROW_CHUNKS, LANES), jnp.uint32),
        compiler_params=pltpu.CompilerParams(dimension_semantics=("arbitrary",)),
        name="moe_dispatch",
    )(tcnt, tdst, zrow, zcnt, nblk, pos, xn2_p, xn2_s)


def _expert_kernel(be_ref, nblk_ref, eord_ref, enext_ref, x_ref, wgu_ref, bgu_ref, wd_ref, bd_ref, y_ref,
                   wgu_f32, wd_f32, wgu_bf, wd_bf, wsem):
    i = pl.program_id(0)
    live = i < nblk_ref[0]
    new_expert = jnp.logical_or(i == 0, be_ref[i] != be_ref[jnp.maximum(i - 1, 0)])

    def weight_copies(e, s):
        return (pltpu.make_async_copy(wgu_ref.at[e], wgu_f32.at[s], wsem.at[0, s]),
                pltpu.make_async_copy(wd_ref.at[e], wd_f32.at[s], wsem.at[1, s]))

    @pl.when(jnp.logical_and(live, new_expert))
    def _():
        slot = eord_ref[i] % 2

        @pl.when(i == 0)
        def _():
            for cp in weight_copies(be_ref[0], 0):
                cp.start()

        for cp in weight_copies(be_ref[i], slot):
            cp.wait()

        @pl.when(enext_ref[i] >= 0)
        def _():
            for cp in weight_copies(enext_ref[i], 1 - slot):
                cp.start()

        def cast(c, carry):
            rows = pl.ds(pl.multiple_of(c * WEIGHT_CAST_ROWS, WEIGHT_CAST_ROWS), WEIGHT_CAST_ROWS)
            wgu_bf[rows, :] = wgu_f32[slot, rows, :].astype(BF16)
            wd_bf[rows, :] = wd_f32[slot, rows, :].astype(BF16)
            return carry

        lax.fori_loop(0, D_MODEL // WEIGHT_CAST_ROWS, cast, 0)

    @pl.when(jnp.logical_not(live))
    def _():
        y_ref[...] = jnp.zeros_like(y_ref)

    @pl.when(live)
    def _():
        x = _load_rows_from_tiles(x_ref, MOE_BLOCK)
        h = _dot(x.astype(BF16), wgu_bf[...]) + bgu_ref[0]
        gate = jnp.minimum(h[:, :D_FF], SWIGLU_LIMIT)
        up = jnp.clip(h[:, D_FF:], -SWIGLU_LIMIT, SWIGLU_LIMIT)
        glu = gate * _sigmoid(gate * SWIGLU_ALPHA)
        y = _dot(((up + 1.0) * glu).astype(BF16), wd_bf[...]) + bd_ref[0]
        _store_rows_as_tiles(y_ref, y.astype(BF16).astype(F32))


def _experts(block_e, nblk, block_eord, block_enext, x_sorted, wgu, bgu, wd, bd):
    n_blocks = x_sorted.shape[0] // MOE_BLOCK
    wmap = lambda i, be, *_: (be[i], 0, 0)
    rmap = lambda i, *_: (i, 0, 0)
    hbm = pl.BlockSpec(memory_space=pl.ANY)
    grid_spec = pltpu.PrefetchScalarGridSpec(
        num_scalar_prefetch=4,
        grid=(n_blocks,),
        in_specs=[
            pl.BlockSpec((MOE_BLOCK, ROW_CHUNKS, LANES), rmap),
            hbm,
            pl.BlockSpec((1, 1, 2 * D_FF), wmap),
            hbm,
            pl.BlockSpec((1, 1, D_MODEL), wmap),
        ],
        out_specs=pl.BlockSpec((MOE_BLOCK, ROW_CHUNKS, LANES), rmap),
        scratch_shapes=[
            pltpu.VMEM((2, D_MODEL, 2 * D_FF), F32), pltpu.VMEM((2, D_FF, D_MODEL), F32),
            pltpu.VMEM((D_MODEL, 2 * D_FF), BF16), pltpu.VMEM((D_FF, D_MODEL), BF16),
            pltpu.SemaphoreType.DMA((2, 2)),
        ],
    )
    return pl.pallas_call(
        _expert_kernel,
        grid_spec=grid_spec,
        out_shape=jax.ShapeDtypeStruct(x_sorted.shape, jnp.uint32),
        compiler_params=pltpu.CompilerParams(
            dimension_semantics=("arbitrary",), vmem_limit_bytes=VMEM_LIMIT),
        name="moe_experts",
    )(block_e, nblk, block_eord, block_enext, x_sorted, wgu, bgu, wd, bd)


def _combine_kernel(tcnt_ref, tdst_ref, pos_ref, gate_ref, ys_ref, hp_ref, hs_ref, nf_ref, yp_ref, ysmp_ref,
                    runs_scr, sem):
    i = pl.program_id(0)
    n_tiles = pl.num_programs(0)
    n_prompt = n_tiles - 1
    slot = i % 2

    def start_runs(tile, s):
        _for_each_run(tcnt_ref, tile, lambda e, off, n: _run_copy(
            ys_ref, tdst_ref[tile * N_EXPERTS + e], runs_scr.at[s], off, n, sem.at[s]).start())

    @pl.when(i == 0)
    def _():
        start_runs(0, 0)

    @pl.when(i + 1 < n_tiles)
    def _():
        start_runs(i + 1, 1 - slot)

    pltpu.make_async_copy(ys_ref.at[pl.ds(0, TILE_ROWS)], runs_scr.at[slot], sem.at[slot]).wait()
    ys = _load_rows_from_tiles(runs_scr.at[slot], TILE_ROWS).astype(BF16)
    r_iota = lax.broadcasted_iota(jnp.int32, (TILE_ROWS, ROW_TILE), 0)
    gmat = jnp.zeros((TILE_ROWS, ROW_TILE), F32)
    for kk in range(TOP_K):
        gmat = jnp.where(r_iota == pos_ref[kk:kk + 1, :], gate_ref[kk:kk + 1, :], gmat)
    g_hi = gmat.astype(BF16)
    g_lo = (gmat - g_hi.astype(F32)).astype(BF16)
    contract0 = (((0,), (0,)), ((), ()))
    moe = (lax.dot_general(g_hi, ys, contract0, preferred_element_type=F32)
           + lax.dot_general(g_lo, ys, contract0, preferred_element_type=F32))

    @pl.when(i < n_prompt)
    def _():
        yp_ref[...] = _rmsnorm(hp_ref[...] + moe, nf_ref[...])

    @pl.when(i == n_prompt)
    def _():
        ysmp_ref[...] = _rmsnorm(hs_ref[...] + moe, nf_ref[...])


def _combine(tcnt, tdst, pos, gates, y_sorted, h1_p, h1_s, nf):
    n_prompt = h1_p.shape[0] // ROW_TILE
    pmap = lambda i, *_: (jnp.minimum(i, n_prompt - 1), 0)
    smap = lambda i, *_: (0, 0)
    lmap = lambda i, *_: (0, i)
    grid_spec = pltpu.PrefetchScalarGridSpec(
        num_scalar_prefetch=2,
        grid=(n_prompt + 1,),
        in_specs=[
            pl.BlockSpec((TOP_K, ROW_TILE), lmap),
            pl.BlockSpec((TOP_K, ROW_TILE), lmap),
            pl.BlockSpec(memory_space=pl.ANY),
            pl.BlockSpec((ROW_TILE, D_MODEL), pmap),
            pl.BlockSpec((ROW_TILE, D_MODEL), smap),
            pl.BlockSpec((1, D_MODEL), smap),
        ],
        out_specs=[pl.BlockSpec((ROW_TILE, D_MODEL), pmap), pl.BlockSpec((ROW_TILE, D_MODEL), smap)],
        scratch_shapes=[pltpu.VMEM((2, TILE_ROWS, ROW_CHUNKS, LANES), jnp.uint32), pltpu.SemaphoreType.DMA((2,))],
    )
    return pl.pallas_call(
        _combine_kernel,
        grid_spec=grid_spec,
        out_shape=[jax.ShapeDtypeStruct(h1_p.shape, F32), jax.ShapeDtypeStruct(h1_s.shape, F32)],
        compiler_params=pltpu.CompilerParams(dimension_semantics=("arbitrary",)),
        name="moe_combine",
    )(tcnt, tdst, pos, gates, y_sorted, h1_p, h1_s, nf)


def _rotary_tables(pos):
    f = np.float32
    inv = np.power(f(ROPE_BASE), -np.arange(0, RET_DK, 2, dtype=f) / f(RET_DK)).astype(f)
    ang = (np.asarray(pos, f)[:, None] * inv[None, :]).astype(f)
    cos, sin = np.cos(ang).astype(f), np.sin(ang).astype(f)
    return np.concatenate([cos, cos], axis=1), np.concatenate([-sin, sin], axis=1)


def _decay_tables(chunk):
    f = np.float32
    log_g = np.log1p(-np.exp2(f(-5.0) - np.arange(RET_HEADS, dtype=f))).astype(f)
    i = np.arange(chunk, dtype=f)
    diff = i[:, None] - i[None, :]
    mask = np.where(diff[None] >= 0, np.exp(np.maximum(diff, f(0.0))[None] * log_g[:, None, None]), f(0.0)).astype(f)
    q_dec = np.exp((i + f(1.0))[None, :] * log_g[:, None]).astype(f)
    k_dec = np.exp((f(chunk) - f(1.0) - i)[None, :] * log_g[:, None]).astype(f)
    c_dec = np.exp(f(chunk) * log_g).astype(f)
    return mask, q_dec, k_dec, c_dec


def kernel(x_prompt, x_sample, state_ret, state_pool, meta_tokens, norm_mix, w_in, ret_gn, pool_w, pool_scale,
           w_ret_branch, w_pool_branch, w_out, norm_ffn, w_router, b_router, w_gate_up, b_gate_up, w_down, b_down,
           norm_final):
    batch, seq, _ = x_prompt.shape
    nb = x_sample.shape[0]
    past_len = 16384
    n_prompt_tok = batch * seq
    n_tok = n_prompt_tok + nb

    w_in_bf = w_in[0].astype(BF16)
    wts = (ret_gn[0][None, :], pool_w[0].astype(BF16), pool_scale[0][None, :],
           w_ret_branch[0].astype(BF16), w_pool_branch[0].astype(BF16), w_out[0].astype(BF16),
           norm_ffn[0][None, :], w_router[0].T.astype(BF16), b_router[0][:, None])
    wgu = w_gate_up[0]
    wd = w_down[0]
    bgu = b_gate_up[0][:, None, :]
    bd = b_down[0][:, None, :]
    nmix = norm_mix[0][None, :]

    cos_p, sin_p = _rotary_tables(N_META + np.arange(seq))
    cos_s, sin_s = _rotary_tables(np.concatenate([np.arange(N_META), np.full((nb,), past_len)]))
    mask, q_dec, k_dec, c_dec = _decay_tables(CHUNK)
    dec = (mask,
           np.ascontiguousarray(np.broadcast_to(q_dec[:, :, None], (RET_HEADS, CHUNK, RET_DK))),
           np.ascontiguousarray(np.broadcast_to(k_dec[:, :, None], (RET_HEADS, CHUNK, RET_DK))),
           np.ascontiguousarray(np.broadcast_to(c_dec[:, None, None], (RET_HEADS, 1, RET_DV))))
    m1, q1, k1, c1 = _decay_tables(1)
    sdec = np.stack([m1[:, 0, 0], q1[:, 0], k1[:, 0], c1], axis=1)

    x2d = x_prompt.reshape(n_prompt_tok, D_MODEL)
    proj_p = _inproj(x2d, nmix, w_in_bf, cos_p, sin_p, INPROJ_TILE, batch, BF16)
    xs2d = x_sample.reshape(nb, D_MODEL)
    x_small = jnp.concatenate([meta_tokens, xs2d], axis=0)
    proj_small = _inproj(x_small, nmix, w_in_bf, cos_s, sin_s, N_META + nb, 1, F32)
    proj_s = tuple(a[N_META:] for a in proj_small)

    lead = CHUNK - N_META
    kmeta = jnp.pad(proj_small[1][:N_META], ((lead, 0), (0, 0))).astype(BF16)
    vmeta = jnp.pad(proj_small[2][:N_META], ((lead, 0), (0, 0))).astype(BF16)
    pmeta = proj_small[4][:N_META]

    grp = SAMPLE_GROUP

    def cols(a):
        return a.T.reshape(RET_QK, nb // grp, grp).transpose(1, 0, 2)

    pool2d = state_pool[0].reshape(nb, POOL_BUF * POOL_WIDTH)
    (st_s, pool_s, h1_s, xn2_s, idx_s, gate_s, rank_s, cnt_s) = _sample_mixer(
        sdec, cols(proj_s[0]), cols(proj_s[1]), proj_s, xs2d, state_ret[0], pool2d, wts)

    (h1_p, xn2_p, idx_p, gate_p, rank_p, cnt, s_fin, p_fin) = _mixer(
        proj_p, x2d, kmeta, vmeta, pmeta, dec, wts, cnt_s, batch, seq)

    assert n_tok % ROW_TILE == 0 and nb == ROW_TILE
    n_tiles = n_tok // ROW_TILE
    i32 = jnp.int32
    counts = cnt[:, 0].astype(i32)
    padded = ((counts + MOE_BLOCK - 1) // MOE_BLOCK) * MOE_BLOCK
    pad_end = jnp.cumsum(padded)
    pad_start = pad_end - padded
    n_blocks = (n_tok * TOP_K) // MOE_BLOCK + N_EXPERTS
    block_row = jnp.arange(n_blocks, dtype=i32) * MOE_BLOCK
    block_e = jnp.minimum(jnp.sum((pad_end[None, :] <= block_row[:, None]).astype(i32), axis=1), N_EXPERTS - 1)
    nblk = (pad_end[-1:] // MOE_BLOCK).astype(i32)
    e_row = jnp.arange(N_EXPERTS, dtype=i32)
    used = padded > 0
    e_ord = jnp.cumsum(used.astype(i32)) - 1
    later_used = jnp.logical_and(e_row[None, :] > e_row[:, None], used[None, :])
    e_next = jnp.min(jnp.where(later_used, e_row[None, :], N_EXPERTS), axis=1)
    e_next = jnp.where(e_next == N_EXPERTS, -1, e_next)
    of_block = block_e[:, None] == e_row[None, :]
    block_eord = jnp.sum(jnp.where(of_block, e_ord[None, :], 0), axis=1)
    block_enext = jnp.sum(jnp.where(of_block, e_next[None, :], 0), axis=1)
    idx = jnp.concatenate([idx_p, idx_s], axis=1)
    rank = jnp.concatenate([rank_p, rank_s], axis=1)
    gates = jnp.concatenate([gate_p, gate_s], axis=1)
    onehot = idx[None] == jnp.arange(N_EXPERTS, dtype=i32)[:, None, None]
    tile_cnt = jnp.sum(onehot.reshape(N_EXPERTS, TOP_K, n_tiles, ROW_TILE).astype(i32), axis=(1, 3)).T
    by_time = jnp.concatenate([tile_cnt[-1:], tile_cnt[:-1]], axis=0)
    before_time = jnp.cumsum(by_time, axis=0) - by_time
    run_before = jnp.concatenate([before_time[1:], before_time[:1]], axis=0)
    tile_off = jnp.cumsum(tile_cnt, axis=1) - tile_cnt
    tile_dst = pad_start[None, :] + run_before
    delta = jnp.repeat((tile_off - run_before).T, ROW_TILE, axis=1)
    pos = rank + jnp.sum(jnp.where(onehot, delta[:, None, :], 0), axis=0)
    tcnt, tdst = tile_cnt.reshape(-1), tile_dst.reshape(-1)

    x_sorted = _dispatch(tcnt, tdst, pad_start + counts, padded - counts, nblk, pos, xn2_p, xn2_s,
                         n_blocks * MOE_BLOCK)
    y_sorted = _experts(block_e, nblk, block_eord, block_enext, x_sorted, wgu, bgu, wd, bd)
    y_p, y_s = _combine(tcnt, tdst, pos, gates, y_sorted, h1_p, h1_s, norm_final[None, :])

    y_prompt = y_p.reshape(batch, seq, D_MODEL)
    y_sample = y_s.reshape(nb, 1, D_MODEL)
    ret_state_prompt = s_fin[None]
    pool_state_prompt = p_fin[:, 1:, :][None]
    ret_state_sample = st_s[None]
    pool_state_sample = pool_s.reshape(nb, POOL_BUF, POOL_WIDTH)[None]
    return (y_prompt, y_sample, ret_state_prompt, pool_state_prompt, ret_state_sample, pool_state_sample)
```

```python
import functools

import jax
import jax.numpy as jnp
import numpy as np
from jax import lax
from jax.experimental import pallas as pl
from jax.experimental.pallas import tpu as pltpu

F32 = jnp.float32
BF16 = jnp.bfloat16

D_MODEL = 1024
N_META = 16
RET_HEADS = 4
RET_DK = 128
RET_DV = 256
RET_QK = RET_HEADS * RET_DK
RET_V = RET_HEADS * RET_DV
CHUNK = 128
ROPE_BASE = 10000.0
POOL_WINDOWS = (2, 4, 8, 16)
POOL_GROUPS = 4
POOL_GROUP_DIM = 128
POOL_WIDTH = POOL_GROUPS * POOL_GROUP_DIM
POOL_BUF = max(POOL_WINDOWS) - 1
N_EXPERTS = 32
TOP_K = 4
D_FF = D_MODEL
SWIGLU_LIMIT = 7.0
SWIGLU_ALPHA = 1.702
EPS = 1e-6
IN_WIDTHS = (RET_QK, RET_QK, RET_V, RET_V, POOL_WIDTH, D_MODEL, D_MODEL)
IN_TOTAL = sum(IN_WIDTHS)
IN_OFFS = tuple(int(s) for s in np.cumsum((0,) + IN_WIDTHS))

LANES = 128
ROW_CHUNKS = D_MODEL // (2 * LANES)
INPROJ_TILE = 512
MIXER_TILE = 512
MOE_BLOCK = 512
ROW_TILE = 128
SAMPLE_GROUP = 8
WEIGHT_CAST_ROWS = 128
VMEM_LIMIT = 56 * 1024 * 1024

assert N_META + 1 >= max(POOL_WINDOWS)
assert POOL_WINDOWS == (2, 4, 8, 16)


def _dot(a, b):
    return jnp.dot(a, b, preferred_element_type=F32)


def _rmsnorm(x, w):
    return x * lax.rsqrt(jnp.mean(x * x, axis=-1, keepdims=True) + EPS) * w


def _sigmoid(x):
    return 0.5 * jnp.tanh(0.5 * x) + 0.5


def _store_rows_as_tiles(ref, x):
    half = D_MODEL // 2
    hi = pltpu.bitcast(x[:, :half], jnp.uint32) & jnp.uint32(0xFFFF0000)
    lo = pltpu.bitcast(x[:, half:], jnp.uint32) >> 16
    ref[...] = (hi | lo).reshape(x.shape[0], ROW_CHUNKS, LANES)


def _load_rows_from_tiles(ref, rows):
    w = ref[...].reshape(rows, D_MODEL // 2)
    hi = pltpu.bitcast(w & jnp.uint32(0xFFFF0000), F32)
    lo = pltpu.bitcast(w << 16, F32)
    return jnp.concatenate([hi, lo], axis=1)


def _inproj_kernel(x_ref, nw_ref, w_ref, cos_ref, sin_ref,
                   q_ref, k_ref, v_ref, g_ref, p_ref, ga_ref, gb_ref):
    xn = _rmsnorm(x_ref[...], nw_ref[...]).astype(BF16)
    cos = cos_ref[...]
    sin = sin_ref[...]

    def seg(i):
        return _dot(xn, w_ref[:, IN_OFFS[i]:IN_OFFS[i + 1]])

    def rot(a):
        return a * cos + pltpu.roll(a, RET_DK // 2, 1) * sin

    q = seg(0)
    k = seg(1)
    for h in range(RET_HEADS):
        sl = slice(h * RET_DK, (h + 1) * RET_DK)
        q_ref[:, sl] = rot(q[:, sl]).astype(q_ref.dtype)
        k_ref[:, sl] = (rot(k[:, sl]) * (RET_DK ** -0.5)).astype(k_ref.dtype)
    v_ref[...] = seg(2).astype(v_ref.dtype)
    g_ref[...] = seg(3).astype(g_ref.dtype)
    p_ref[...] = seg(4)
    ga_ref[...] = seg(5).astype(ga_ref.dtype)
    gb_ref[...] = seg(6).astype(gb_ref.dtype)


def _inproj(x2d, nw, w_in_bf, cosf, sinf, tile, n_outer, act_dtype):
    rows = x2d.shape[0]
    n_inner = rows // (tile * n_outer)
    row_map = lambda b, j: (b * n_inner + j, 0)
    tab_map = lambda b, j: (j, 0)
    const = lambda b, j: (0, 0)
    widths = IN_WIDTHS
    dts = (act_dtype, act_dtype, act_dtype, act_dtype, F32, act_dtype, act_dtype)
    return pl.pallas_call(
        _inproj_kernel,
        grid=(n_outer, n_inner),
        in_specs=[
            pl.BlockSpec((tile, D_MODEL), row_map),
            pl.BlockSpec((1, D_MODEL), const),
            pl.BlockSpec((D_MODEL, IN_TOTAL), const, pipeline_mode=pl.Buffered(1)),
            pl.BlockSpec((tile, RET_DK), tab_map),
            pl.BlockSpec((tile, RET_DK), tab_map),
        ],
        out_specs=[pl.BlockSpec((tile, w), row_map) for w in widths],
        out_shape=[jax.ShapeDtypeStruct((rows, w), dt) for w, dt in zip(widths, dts)],
        compiler_params=pltpu.CompilerParams(
            dimension_semantics=("arbitrary", "arbitrary"), vmem_limit_bytes=VMEM_LIMIT),
        name="inproj",
    )(x2d, nw, w_in_bf, cosf, sinf)


def _group_norm(o, gn_row):
    mu = jnp.mean(o, axis=-1, keepdims=True)
    var = jnp.mean(jnp.square(o - mu), axis=-1, keepdims=True)
    return (o - mu) * lax.rsqrt(var + EPS) * gn_row


def _pool_branch(groups, poolw_ref, pscale_ref, wpool_ref):
    pm = [_dot(g.astype(BF16), poolw_ref[i]) for i, g in enumerate(groups)]
    pm = jnp.concatenate(pm, axis=1) * pscale_ref[...]
    return _dot(pm.astype(BF16), wpool_ref[...])


def _merge_tail(o_norm, g, ga, gb, yb, x, wret_ref, wout_ref):
    gf = g.astype(F32)
    ya = _dot((gf * _sigmoid(gf) * o_norm).astype(BF16), wret_ref[...])
    merged = _sigmoid(ga.astype(F32)) * ya + _sigmoid(gb.astype(F32)) * yb
    return x + _dot(merged.astype(BF16), wout_ref[...])


def _route(h1, nffn_ref, wrt_ref, br_ref, run_scr,
           xn2_ref, idx_ref, gate_ref, rank_ref):
    tm = h1.shape[0]
    xn2 = _rmsnorm(h1, nffn_ref[...]).astype(BF16)
    xn2_ref[...] = xn2
    logits = lax.dot_general(wrt_ref[...], xn2, (((1,), (1,)), ((), ())),
                             preferred_element_type=F32) + br_ref[...]
    e_iota = lax.broadcasted_iota(jnp.int32, (N_EXPERTS, tm), 0)
    work = logits
    vals, sels = [], []
    chosen = jnp.zeros((N_EXPERTS, tm), F32)
    for _ in range(TOP_K):
        m = jnp.max(work, axis=0, keepdims=True)
        sel = jnp.min(jnp.where(work == m, e_iota, N_EXPERTS), axis=0, keepdims=True)
        hit = e_iota == sel
        vals.append(m)
        sels.append(sel)
        chosen = jnp.where(hit, 1.0, chosen)
        work = jnp.where(hit, -jnp.inf, work)
    exps = [jnp.exp(v - vals[0]) for v in vals]
    denom = exps[0] + exps[1] + exps[2] + exps[3]
    gates = [e / denom for e in exps]
    r_i = lax.broadcasted_iota(jnp.int32, (tm, tm), 0)
    c_i = lax.broadcasted_iota(jnp.int32, (tm, tm), 1)
    before = jnp.where(r_i < c_i, 1.0, 0.0).astype(BF16)
    base = run_scr[...] + _dot(chosen.astype(BF16), before)
    for kk in range(TOP_K):
        rk = jnp.sum(jnp.where(e_iota == sels[kk], base, 0.0), axis=0, keepdims=True)
        rank_ref[kk:kk + 1, :] = rk.astype(jnp.int32)
        idx_ref[kk:kk + 1, :] = sels[kk]
        gate_ref[kk:kk + 1, :] = gates[kk]
    run_scr[...] = run_scr[...] + jnp.sum(chosen, axis=1, keepdims=True)


def _mixer_kernel(q_ref, k_ref, v_ref, g_ref, p_ref, ga_ref, gb_ref, x_ref,
                  kmeta_ref, vmeta_ref, pmeta_ref, mask_ref, qdec_ref, kdec_ref, cdec_ref, gn_ref,
                  poolw_ref, pscale_ref, wret_ref, wpool_ref, wout_ref, nffn_ref, wrt_ref, br_ref,
                  cnt0_ref,
                  h1_ref, xn2_ref, idx_ref, gate_ref, rank_ref, cnt_ref, sfin_ref, pfin_ref,
                  s_scr, ext_scr, o_scr, run_scr):
    b = pl.program_id(0)
    j = pl.program_id(1)
    nj = pl.num_programs(1)
    tm = q_ref.shape[0]

    def state_update(s_old, kc, vc, h):
        kd = (kc.astype(F32) * kdec_ref[h]).astype(BF16)
        upd = lax.dot_general(kd, vc, (((0,), (0,)), ((), ())), preferred_element_type=F32)
        return s_old * cdec_ref[h] + upd

    @pl.when(jnp.logical_and(b == 0, j == 0))
    def _():
        run_scr[...] = cnt0_ref[:, 0:1]

    @pl.when(j == 0)
    def _():
        for h in range(RET_HEADS):
            kc = kmeta_ref[:, h * RET_DK:(h + 1) * RET_DK]
            vc = vmeta_ref[:, h * RET_DV:(h + 1) * RET_DV]
            s_scr[h] = state_update(jnp.zeros((RET_DK, RET_DV), F32), kc, vc, h)
        ext_scr[0:N_META, :] = pmeta_ref[...]

    for c in range(tm // CHUNK):
        rows = slice(c * CHUNK, (c + 1) * CHUNK)
        for h in range(RET_HEADS):
            qc = q_ref[rows, h * RET_DK:(h + 1) * RET_DK]
            kc = k_ref[rows, h * RET_DK:(h + 1) * RET_DK]
            vc = v_ref[rows, h * RET_DV:(h + 1) * RET_DV]
            s_old = s_scr[h]
            scores = lax.dot_general(qc, kc, (((1,), (1,)), ((), ())),
                                     preferred_element_type=F32) * mask_ref[h]
            qd = (qc.astype(F32) * qdec_ref[h]).astype(BF16)
            lhs = jnp.concatenate([scores.astype(BF16), qd], axis=1)
            rhs = jnp.concatenate([vc, s_old.astype(BF16)], axis=0)
            o = _dot(lhs, rhs)
            s_scr[h] = state_update(s_old, kc, vc, h)
            o_scr[rows, h * RET_DV:(h + 1) * RET_DV] = _group_norm(
                o, gn_ref[:, h * RET_DV:(h + 1) * RET_DV])

    p = p_ref[...]
    ext_scr[N_META:N_META + tm, :] = p
    a = ext_scr[...]
    g1 = POOL_GROUP_DIM
    s2 = a + pltpu.roll(a, 1, 0)
    s4 = s2[:, g1:] + pltpu.roll(s2[:, g1:], 2, 0)
    s8 = s4[:, g1:] + pltpu.roll(s4[:, g1:], 4, 0)
    s16 = s8[:, g1:] + pltpu.roll(s8[:, g1:], 8, 0)
    sums = (s2[N_META:, :g1], s4[N_META:, :g1], s8[N_META:, :g1], s16[N_META:, :])
    groups = [sums[i] * (1.0 / POOL_WINDOWS[i]) - p[:, i * g1:(i + 1) * g1] for i in range(POOL_GROUPS)]
    ext_scr[0:N_META, :] = ext_scr[tm:tm + N_META, :]

    yb = _pool_branch(groups, poolw_ref, pscale_ref, wpool_ref)
    h1 = _merge_tail(o_scr[...], g_ref[...], ga_ref[...], gb_ref[...], yb, x_ref[...], wret_ref, wout_ref)
    h1_ref[...] = h1
    _route(h1, nffn_ref, wrt_ref, br_ref, run_scr, xn2_ref, idx_ref, gate_ref, rank_ref)
    cnt_ref[...] = jnp.broadcast_to(run_scr[...], cnt_ref.shape)

    @pl.when(j == nj - 1)
    def _():
        for h in range(RET_HEADS):
            sfin_ref[0, h] = s_scr[h]
        pfin_ref[0] = ext_scr[0:N_META, :]


def _mixer(proj, x2d, kmeta, vmeta, pmeta, dec, wts, cnt0, batch, seq):
    q, k, v, g, p, ga, gb = proj
    tm = MIXER_TILE
    nj = seq // tm
    rows = batch * seq
    row_map = lambda b, j: (b * nj + j, 0)
    lane_map = lambda b, j: (0, b * nj + j)
    c2 = lambda b, j: (0, 0)
    c3 = lambda b, j: (0, 0, 0)

    def whole(a):
        return pl.BlockSpec(a.shape, c2 if a.ndim == 2 else c3)

    mask, qdec, kdec, cdec = dec
    gn, poolw, pscale, wret, wpool, wout, nffn, wrt, br = wts
    in_arrays = [q, k, v, g, p, ga, gb, x2d, kmeta, vmeta, pmeta, mask, qdec, kdec, cdec, gn,
                 poolw, pscale, wret, wpool, wout, nffn, wrt, br, cnt0]
    in_specs = [pl.BlockSpec((tm, a.shape[1]), row_map) for a in in_arrays[:8]]
    in_specs += [whole(a) for a in in_arrays[8:]]
    out_shape = [
        jax.ShapeDtypeStruct((rows, D_MODEL), F32),
        jax.ShapeDtypeStruct((rows, D_MODEL), BF16),
        jax.ShapeDtypeStruct((TOP_K, rows), jnp.int32),
        jax.ShapeDtypeStruct((TOP_K, rows), F32),
        jax.ShapeDtypeStruct((TOP_K, rows), jnp.int32),
        jax.ShapeDtypeStruct((N_EXPERTS, LANES), F32),
        jax.ShapeDtypeStruct((batch, RET_HEADS, RET_DK, RET_DV), F32),
        jax.ShapeDtypeStruct((batch, N_META, POOL_WIDTH), F32),
    ]
    out_specs = [
        pl.BlockSpec((tm, D_MODEL), row_map),
        pl.BlockSpec((tm, D_MODEL), row_map),
        pl.BlockSpec((TOP_K, tm), lane_map),
        pl.BlockSpec((TOP_K, tm), lane_map),
        pl.BlockSpec((TOP_K, tm), lane_map),
        pl.BlockSpec((N_EXPERTS, LANES), c2),
        pl.BlockSpec((1, RET_HEADS, RET_DK, RET_DV), lambda b, j: (b, 0, 0, 0)),
        pl.BlockSpec((1, N_META, POOL_WIDTH), lambda b, j: (b, 0, 0)),
    ]
    return pl.pallas_call(
        _mixer_kernel,
        grid=(batch, nj),
        in_specs=in_specs,
        out_specs=out_specs,
        out_shape=out_shape,
        scratch_shapes=[
            pltpu.VMEM((RET_HEADS, RET_DK, RET_DV), F32),
            pltpu.VMEM((N_META + tm, POOL_WIDTH), F32),
            pltpu.VMEM((tm, RET_V), F32),
            pltpu.VMEM((N_EXPERTS, 1), F32),
        ],
        compiler_params=pltpu.CompilerParams(
            dimension_semantics=("arbitrary", "arbitrary"), vmem_limit_bytes=VMEM_LIMIT),
        name="mixer",
    )(*in_arrays)


def _sample_kernel(sdec_ref, qt_ref, kt_ref, q_ref, k_ref, v_ref, g_ref, p_ref, ga_ref, gb_ref, x_ref,
                   st_ref, pool_ref, gn_ref,
                   poolw_ref, pscale_ref, wret_ref, wpool_ref, wout_ref, nffn_ref, wrt_ref, br_ref,
                   stout_ref, poolout_ref, h1_ref, xn2_ref, idx_ref, gate_ref, rank_ref,
                   cnt_ref,
                   o_scr, run_scr):
    i = pl.program_id(0)
    n = pl.num_programs(0)
    grp = st_ref.shape[0]
    row0 = pl.multiple_of(i * grp, grp)

    @pl.when(i == 0)
    def _():
        run_scr[...] = jnp.zeros_like(run_scr)

    q8 = q_ref[pl.ds(row0, grp), :]
    k8 = k_ref[pl.ds(row0, grp), :]
    v8 = v_ref[pl.ds(row0, grp), :]
    for h in range(RET_HEADS):
        ksl = slice(h * RET_DK, (h + 1) * RET_DK)
        vsl = slice(h * RET_DV, (h + 1) * RET_DV)
        score = jnp.sum(q8[:, ksl] * k8[:, ksl], axis=1, keepdims=True) * sdec_ref[h, 0]
        intra = score * v8[:, vsl]
        for bb in range(grp):
            s_old = st_ref[bb, h]
            qcol = qt_ref[0, ksl, bb:bb + 1] * sdec_ref[h, 1]
            kcol = kt_ref[0, ksl, bb:bb + 1] * sdec_ref[h, 2]
            cross = jnp.sum(s_old * qcol, axis=0, keepdims=True)
            o_scr[pl.ds(row0 + bb, 1), vsl] = intra[bb:bb + 1, :] + cross
            stout_ref[bb, h] = s_old * sdec_ref[h, 3] + kcol * v8[bb:bb + 1, vsl]

    @pl.when(i == n - 1)
    def _():
        o = o_scr[...]
        o_norm = jnp.concatenate(
            [_group_norm(o[:, h * RET_DV:(h + 1) * RET_DV], gn_ref[:, h * RET_DV:(h + 1) * RET_DV])
             for h in range(RET_HEADS)], axis=1)
        p = p_ref[...]
        w = POOL_WIDTH
        g1 = POOL_GROUP_DIM

        def prev(r, lo):
            return pool_ref[:, r * w + lo:(r + 1) * w]

        s2 = p + prev(14, 0)
        s4 = s2[:, g1:] + prev(13, g1) + prev(12, g1)
        s8 = s4[:, g1:] + prev(11, 2 * g1) + prev(10, 2 * g1) + prev(9, 2 * g1) + prev(8, 2 * g1)
        s16 = s8[:, g1:]
        for r in range(7, -1, -1):
            s16 = s16 + prev(r, 3 * g1)
        sums = (s2[:, :g1], s4[:, :g1], s8[:, :g1], s16)
        groups = [sums[t] * (1.0 / POOL_WINDOWS[t]) - p[:, t * g1:(t + 1) * g1] for t in range(POOL_GROUPS)]
        poolout_ref[:, 0:(POOL_BUF - 1) * w] = pool_ref[:, w:POOL_BUF * w]
        poolout_ref[:, (POOL_BUF - 1) * w:] = p
        yb = _pool_branch(groups, poolw_ref, pscale_ref, wpool_ref)
        h1 = _merge_tail(o_norm, g_ref[...], ga_ref[...], gb_ref[...], yb, x_ref[...], wret_ref, wout_ref)
        h1_ref[...] = h1
        _route(h1, nffn_ref, wrt_ref, br_ref, run_scr, xn2_ref, idx_ref, gate_ref, rank_ref)
        cnt_ref[...] = jnp.broadcast_to(run_scr[...], cnt_ref.shape)


def _sample_mixer(sdec, qt, kt, proj, x2d, state, pool2d, wts):
    q, k, v, g, p, ga, gb = proj
    nb = x2d.shape[0]
    grp = SAMPLE_GROUP
    c2 = lambda i: (0, 0)

    def whole(a):
        return pl.BlockSpec(a.shape, c2)

    gn, poolw, pscale, wret, wpool, wout, nffn, wrt, br = wts
    in_arrays = [sdec, qt, kt, q, k, v, g, p, ga, gb, x2d, state, pool2d, gn,
                 poolw, pscale, wret, wpool, wout, nffn, wrt, br]
    in_specs = [pl.BlockSpec(memory_space=pltpu.SMEM),
                pl.BlockSpec((1, RET_QK, grp), lambda i: (i, 0, 0)),
                pl.BlockSpec((1, RET_QK, grp), lambda i: (i, 0, 0))]
    in_specs += [whole(a) for a in (q, k, v, g, p, ga, gb, x2d)]
    in_specs += [pl.BlockSpec((grp, RET_HEADS, RET_DK, RET_DV), lambda i: (i, 0, 0, 0)), whole(pool2d), whole(gn),
                 pl.BlockSpec(poolw.shape, lambda i: (0, 0, 0))]
    in_specs += [whole(a) for a in (pscale, wret, wpool, wout, nffn, wrt, br)]
    out_shape = [
        jax.ShapeDtypeStruct(state.shape, F32),
        jax.ShapeDtypeStruct(pool2d.shape, F32),
        jax.ShapeDtypeStruct((nb, D_MODEL), F32),
        jax.ShapeDtypeStruct((nb, D_MODEL), BF16),
        jax.ShapeDtypeStruct((TOP_K, nb), jnp.int32),
        jax.ShapeDtypeStruct((TOP_K, nb), F32),
        jax.ShapeDtypeStruct((TOP_K, nb), jnp.int32),
        jax.ShapeDtypeStruct((N_EXPERTS, LANES), F32),
    ]
    out_specs = [pl.BlockSpec((grp, RET_HEADS, RET_DK, RET_DV), lambda i: (i, 0, 0, 0))]
    out_specs += [pl.BlockSpec(s.shape, c2) for s in out_shape[1:]]
    return pl.pallas_call(
        _sample_kernel,
        grid=(nb // grp,),
        in_specs=in_specs,
        out_specs=out_specs,
        out_shape=out_shape,
        scratch_shapes=[pltpu.VMEM((nb, RET_V), F32), pltpu.VMEM((N_EXPERTS, 1), F32)],
        compiler_params=pltpu.CompilerParams(
            dimension_semantics=("arbitrary",), vmem_limit_bytes=VMEM_LIMIT),
        name="sample_mixer",
    )(*in_arrays)


TILE_ROWS = ROW_TILE * TOP_K


def _run_copy(src_ref, src_row, dst_ref, dst_row, n_rows, sem):
    return pltpu.make_async_copy(src_ref.at[pl.ds(src_row, n_rows)], dst_ref.at[pl.ds(dst_row, n_rows)], sem)


def _for_each_run(tcnt_ref, tile, fn):
    def body(e, off):
        n = tcnt_ref[tile * N_EXPERTS + e]

        @pl.when(n > 0)
        def _():
            fn(e, off, n)

        return off + n

    lax.fori_loop(0, N_EXPERTS, body, 0)


def _dispatch_kernel(tcnt_ref, tdst_ref, zrow_ref, zcnt_ref, nblk_ref, pos_ref, xp_ref, xs_ref, out_ref,
                     sorted_scr, zero_scr, sem, zsem):
    i = pl.program_id(0)
    n_prompt = pl.num_programs(0) - 1
    n_blocks = out_ref.shape[0] // MOE_BLOCK
    slot = i % 2

    def wait_tile(s):
        pltpu.make_async_copy(out_ref.at[pl.ds(0, TILE_ROWS)], sorted_scr.at[s], sem.at[s]).wait()

    def for_each_pad(fn):
        def body(e, carry):
            n = zcnt_ref[e]

            @pl.when(n > 0)
            def _():
                fn(_run_copy(zero_scr, 0, out_ref, zrow_ref[e], n, zsem))

            dead = nblk_ref[0] + e

            @pl.when(dead < n_blocks)
            def _():
                fn(_run_copy(zero_scr, 0, out_ref, dead * MOE_BLOCK, MOE_BLOCK, zsem))

            return carry

        lax.fori_loop(0, N_EXPERTS, body, 0)

    @pl.when(i == 0)
    def _():
        zero_scr[...] = jnp.zeros_like(zero_scr)
        for_each_pad(lambda cp: cp.start())

    def sort_tile(x_ref):
        r_iota = lax.broadcasted_iota(jnp.int32, (TILE_ROWS, ROW_TILE), 0)
        hit = r_iota == pos_ref[0:1, :]
        for kk in range(1, TOP_K):
            hit = jnp.logical_or(hit, r_iota == pos_ref[kk:kk + 1, :])
        perm = jnp.where(hit, 1.0, 0.0).astype(BF16)
        xs = _dot(perm, x_ref[...])

        @pl.when(i >= 2)
        def _():
            wait_tile(slot)

        _store_rows_as_tiles(sorted_scr.at[slot], xs)

    @pl.when(i < n_prompt)
    def _():
        sort_tile(xp_ref)

    @pl.when(i == n_prompt)
    def _():
        sort_tile(xs_ref)

    _for_each_run(tcnt_ref, i, lambda e, off, n: _run_copy(
        sorted_scr.at[slot], off, out_ref, tdst_ref[i * N_EXPERTS + e], n, sem.at[slot]).start())

    @pl.when(i == n_prompt)
    def _():
        wait_tile(1 - slot)
        wait_tile(slot)
        for_each_pad(lambda cp: cp.wait())


def _dispatch(tcnt, tdst, zrow, zcnt, nblk, pos, xn2_p, xn2_s, n_sorted):
    n_prompt = xn2_p.shape[0] // ROW_TILE
    grid_spec = pltpu.PrefetchScalarGridSpec(
        num_scalar_prefetch=5,
        grid=(n_prompt + 1,),
        in_specs=[
            pl.BlockSpec((TOP_K, ROW_TILE), lambda i, *_: (0, i)),
            pl.BlockSpec((ROW_TILE, D_MODEL), lambda i, *_: (jnp.minimum(i, n_prompt - 1), 0)),
            pl.BlockSpec((ROW_TILE, D_MODEL), lambda i, *_: (0, 0)),
        ],
        out_specs=pl.BlockSpec(memory_space=pl.ANY),
        scratch_shapes=[
            pltpu.VMEM((2, TILE_ROWS, ROW_CHUNKS, LANES), jnp.uint32),
            pltpu.VMEM((MOE_BLOCK, ROW_CHUNKS, LANES), jnp.uint32),
            pltpu.SemaphoreType.DMA((2,)), pltpu.SemaphoreType.DMA(()),
        ],
    )
    return pl.pallas_call(
        _dispatch_kernel,
        grid_spec=grid_spec,
        out_shape=jax.ShapeDtypeStruct((n_sorted, ROW_CHUNKS, LANES), jnp.uint32),
        compiler_params=pltpu.CompilerParams(dimension_semantics=("arbitrary",)),
        name="moe_dispatch",
    )(tcnt, tdst, zrow, zcnt, nblk, pos, xn2_p, xn2_s)


def _expert_kernel(be_ref, nblk_ref, eord_ref, enext_ref, x_ref, wgu_ref, bgu_ref, wd_ref, bd_ref, y_ref,
                   wgu_f32, wd_f32, wgu_bf, wd_bf, wsem):
    i = pl.program_id(0)
    live = i < nblk_ref[0]
    new_expert = jnp.logical_or(i == 0, be_ref[i] != be_ref[jnp.maximum(i - 1, 0)])

    def weight_copies(e, s):
        return (pltpu.make_async_copy(wgu_ref.at[e], wgu_f32.at[s], wsem.at[0, s]),
                pltpu.make_async_copy(wd_ref.at[e], wd_f32.at[s], wsem.at[1, s]))

    @pl.when(jnp.logical_and(live, new_expert))
    def _():
        slot = eord_ref[i] % 2

        @pl.when(i == 0)
        def _():
            for cp in weight_copies(be_ref[0], 0):
                cp.start(priority=1)

        for cp in weight_copies(be_ref[i], slot):
            cp.wait()

        @pl.when(enext_ref[i] >= 0)
        def _():
            for cp in weight_copies(enext_ref[i], 1 - slot):
                cp.start(priority=1)

        def cast(c, carry):
            rows = pl.ds(pl.multiple_of(c * WEIGHT_CAST_ROWS, WEIGHT_CAST_ROWS), WEIGHT_CAST_ROWS)
            wgu_bf[rows, :] = wgu_f32[slot, rows, :].astype(BF16)
            wd_bf[rows, :] = wd_f32[slot, rows, :].astype(BF16)
            return carry

        lax.fori_loop(0, D_MODEL // WEIGHT_CAST_ROWS, cast, 0)

    @pl.when(jnp.logical_not(live))
    def _():
        y_ref[...] = jnp.zeros_like(y_ref)

    @pl.when(live)
    def _():
        x = _load_rows_from_tiles(x_ref, MOE_BLOCK)
        h = _dot(x.astype(BF16), wgu_bf[...]) + bgu_ref[0]
        gate = jnp.minimum(h[:, :D_FF], SWIGLU_LIMIT)
        up = jnp.clip(h[:, D_FF:], -SWIGLU_LIMIT, SWIGLU_LIMIT)
        glu = gate * _sigmoid(gate * SWIGLU_ALPHA)
        y = _dot(((up + 1.0) * glu).astype(BF16), wd_bf[...]) + bd_ref[0]
        _store_rows_as_tiles(y_ref, y.astype(BF16).astype(F32))


def _experts(block_e, nblk, block_eord, block_enext, x_sorted, wgu, bgu, wd, bd):
    n_blocks = x_sorted.shape[0] // MOE_BLOCK
    wmap = lambda i, be, *_: (be[i], 0, 0)
    rmap = lambda i, *_: (i, 0, 0)
    hbm = pl.BlockSpec(memory_space=pl.ANY)
    grid_spec = pltpu.PrefetchScalarGridSpec(
        num_scalar_prefetch=4,
        grid=(n_blocks,),
        in_specs=[
            pl.BlockSpec((MOE_BLOCK, ROW_CHUNKS, LANES), rmap),
            hbm,
            pl.BlockSpec((1, 1, 2 * D_FF), wmap),
            hbm,
            pl.BlockSpec((1, 1, D_MODEL), wmap),
        ],
        out_specs=pl.BlockSpec((MOE_BLOCK, ROW_CHUNKS, LANES), rmap),
        scratch_shapes=[
            pltpu.VMEM((2, D_MODEL, 2 * D_FF), F32), pltpu.VMEM((2, D_FF, D_MODEL), F32),
            pltpu.VMEM((D_MODEL, 2 * D_FF), BF16), pltpu.VMEM((D_FF, D_MODEL), BF16),
            pltpu.SemaphoreType.DMA((2, 2)),
        ],
    )
    return pl.pallas_call(
        _expert_kernel,
        grid_spec=grid_spec,
        out_shape=jax.ShapeDtypeStruct(x_sorted.shape, jnp.uint32),
        compiler_params=pltpu.CompilerParams(
            dimension_semantics=("arbitrary",), vmem_limit_bytes=VMEM_LIMIT),
        name="moe_experts",
    )(block_e, nblk, block_eord, block_enext, x_sorted, wgu, bgu, wd, bd)


def _combine_kernel(tcnt_ref, tdst_ref, pos_ref, gate_ref, ys_ref, hp_ref, hs_ref, nf_ref, yp_ref, ysmp_ref,
                    runs_scr, sem):
    i = pl.program_id(0)
    n_tiles = pl.num_programs(0)
    n_prompt = n_tiles - 1
    slot = i % 2

    def start_runs(tile, s):
        _for_each_run(tcnt_ref, tile, lambda e, off, n: _run_copy(
            ys_ref, tdst_ref[tile * N_EXPERTS + e], runs_scr.at[s], off, n, sem.at[s]).start())

    @pl.when(i == 0)
    def _():
        start_runs(0, 0)

    @pl.when(i + 1 < n_tiles)
    def _():
        start_runs(i + 1, 1 - slot)

    pltpu.make_async_copy(ys_ref.at[pl.ds(0, TILE_ROWS)], runs_scr.at[slot], sem.at[slot]).wait()
    ys = _load_rows_from_tiles(runs_scr.at[slot], TILE_ROWS).astype(BF16)
    r_iota = lax.broadcasted_iota(jnp.int32, (TILE_ROWS, ROW_TILE), 0)
    gmat = jnp.zeros((TILE_ROWS, ROW_TILE), F32)
    for kk in range(TOP_K):
        gmat = jnp.where(r_iota == pos_ref[kk:kk + 1, :], gate_ref[kk:kk + 1, :], gmat)
    g_hi = gmat.astype(BF16)
    g_lo = (gmat - g_hi.astype(F32)).astype(BF16)
    contract0 = (((0,), (0,)), ((), ()))
    moe = (lax.dot_general(g_hi, ys, contract0, preferred_element_type=F32)
           + lax.dot_general(g_lo, ys, contract0, preferred_element_type=F32))

    @pl.when(i < n_prompt)
    def _():
        yp_ref[...] = _rmsnorm(hp_ref[...] + moe, nf_ref[...])

    @pl.when(i == n_prompt)
    def _():
        ysmp_ref[...] = _rmsnorm(hs_ref[...] + moe, nf_ref[...])


def _combine(tcnt, tdst, pos, gates, y_sorted, h1_p, h1_s, nf):
    n_prompt = h1_p.shape[0] // ROW_TILE
    pmap = lambda i, *_: (jnp.minimum(i, n_prompt - 1), 0)
    smap = lambda i, *_: (0, 0)
    lmap = lambda i, *_: (0, i)
    grid_spec = pltpu.PrefetchScalarGridSpec(
        num_scalar_prefetch=2,
        grid=(n_prompt + 1,),
        in_specs=[
            pl.BlockSpec((TOP_K, ROW_TILE), lmap),
            pl.BlockSpec((TOP_K, ROW_TILE), lmap),
            pl.BlockSpec(memory_space=pl.ANY),
            pl.BlockSpec((ROW_TILE, D_MODEL), pmap),
            pl.BlockSpec((ROW_TILE, D_MODEL), smap),
            pl.BlockSpec((1, D_MODEL), smap),
        ],
        out_specs=[pl.BlockSpec((ROW_TILE, D_MODEL), pmap), pl.BlockSpec((ROW_TILE, D_MODEL), smap)],
        scratch_shapes=[pltpu.VMEM((2, TILE_ROWS, ROW_CHUNKS, LANES), jnp.uint32), pltpu.SemaphoreType.DMA((2,))],
    )
    return pl.pallas_call(
        _combine_kernel,
        grid_spec=grid_spec,
        out_shape=[jax.ShapeDtypeStruct(h1_p.shape, F32), jax.ShapeDtypeStruct(h1_s.shape, F32)],
        compiler_params=pltpu.CompilerParams(dimension_semantics=("arbitrary",)),
        name="moe_combine",
    )(tcnt, tdst, pos, gates, y_sorted, h1_p, h1_s, nf)


def _rotary_tables(pos):
    f = np.float32
    inv = np.power(f(ROPE_BASE), -np.arange(0, RET_DK, 2, dtype=f) / f(RET_DK)).astype(f)
    ang = (np.asarray(pos, f)[:, None] * inv[None, :]).astype(f)
    cos, sin = np.cos(ang).astype(f), np.sin(ang).astype(f)
    return np.concatenate([cos, cos], axis=1), np.concatenate([-sin, sin], axis=1)


def _decay_tables(chunk):
    f = np.float32
    log_g = np.log1p(-np.exp2(f(-5.0) - np.arange(RET_HEADS, dtype=f))).astype(f)
    i = np.arange(chunk, dtype=f)
    diff = i[:, None] - i[None, :]
    mask = np.where(diff[None] >= 0, np.exp(np.maximum(diff, f(0.0))[None] * log_g[:, None, None]), f(0.0)).astype(f)
    q_dec = np.exp((i + f(1.0))[None, :] * log_g[:, None]).astype(f)
    k_dec = np.exp((f(chunk) - f(1.0) - i)[None, :] * log_g[:, None]).astype(f)
    c_dec = np.exp(f(chunk) * log_g).astype(f)
    return mask, q_dec, k_dec, c_dec


def kernel(x_prompt, x_sample, state_ret, state_pool, meta_tokens, norm_mix, w_in, ret_gn, pool_w, pool_scale,
           w_ret_branch, w_pool_branch, w_out, norm_ffn, w_router, b_router, w_gate_up, b_gate_up, w_down, b_down,
           norm_final):
    batch, seq, _ = x_prompt.shape
    nb = x_sample.shape[0]
    past_len = 16384
    n_prompt_tok = batch * seq
    n_tok = n_prompt_tok + nb

    w_in_bf = w_in[0].astype(BF16)
    wts = (ret_gn[0][None, :], pool_w[0].astype(BF16), pool_scale[0][None, :],
           w_ret_branch[0].astype(BF16), w_pool_branch[0].astype(BF16), w_out[0].astype(BF16),
           norm_ffn[0][None, :], w_router[0].T.astype(BF16), b_router[0][:, None])
    wgu = w_gate_up[0]
    wd = w_down[0]
    bgu = b_gate_up[0][:, None, :]
    bd = b_down[0][:, None, :]
    nmix = norm_mix[0][None, :]

    cos_p, sin_p = _rotary_tables(N_META + np.arange(seq))
    cos_s, sin_s = _rotary_tables(np.concatenate([np.arange(N_META), np.full((nb,), past_len)]))
    mask, q_dec, k_dec, c_dec = _decay_tables(CHUNK)
    dec = (mask,
           np.ascontiguousarray(np.broadcast_to(q_dec[:, :, None], (RET_HEADS, CHUNK, RET_DK))),
           np.ascontiguousarray(np.broadcast_to(k_dec[:, :, None], (RET_HEADS, CHUNK, RET_DK))),
           np.ascontiguousarray(np.broadcast_to(c_dec[:, None, None], (RET_HEADS, 1, RET_DV))))
    m1, q1, k1, c1 = _decay_tables(1)
    sdec = np.stack([m1[:, 0, 0], q1[:, 0], k1[:, 0], c1], axis=1)

    x2d = x_prompt.reshape(n_prompt_tok, D_MODEL)
    proj_p = _inproj(x2d, nmix, w_in_bf, cos_p, sin_p, INPROJ_TILE, batch, BF16)
    xs2d = x_sample.reshape(nb, D_MODEL)
    x_small = jnp.concatenate([meta_tokens, xs2d], axis=0)
    proj_small = _inproj(x_small, nmix, w_in_bf, cos_s, sin_s, N_META + nb, 1, F32)
    proj_s = tuple(a[N_META:] for a in proj_small)

    lead = CHUNK - N_META
    kmeta = jnp.pad(proj_small[1][:N_META], ((lead, 0), (0, 0))).astype(BF16)
    vmeta = jnp.pad(proj_small[2][:N_META], ((lead, 0), (0, 0))).astype(BF16)
    pmeta = proj_small[4][:N_META]

    grp = SAMPLE_GROUP

    def cols(a):
        return a.T.reshape(RET_QK, nb // grp, grp).transpose(1, 0, 2)

    pool2d = state_pool[0].reshape(nb, POOL_BUF * POOL_WIDTH)
    (st_s, pool_s, h1_s, xn2_s, idx_s, gate_s, rank_s, cnt_s) = _sample_mixer(
        sdec, cols(proj_s[0]), cols(proj_s[1]), proj_s, xs2d, state_ret[0], pool2d, wts)

    (h1_p, xn2_p, idx_p, gate_p, rank_p, cnt, s_fin, p_fin) = _mixer(
        proj_p, x2d, kmeta, vmeta, pmeta, dec, wts, cnt_s, batch, seq)

    assert n_tok % ROW_TILE == 0 and nb == ROW_TILE
    n_tiles = n_tok // ROW_TILE
    i32 = jnp.int32
    counts = cnt[:, 0].astype(i32)
    padded = ((counts + MOE_BLOCK - 1) // MOE_BLOCK) * MOE_BLOCK
    pad_end = jnp.cumsum(padded)
    pad_start = pad_end - padded
    n_blocks = (n_tok * TOP_K) // MOE_BLOCK + N_EXPERTS
    block_row = jnp.arange(n_blocks, dtype=i32) * MOE_BLOCK
    block_e = jnp.minimum(jnp.sum((pad_end[None, :] <= block_row[:, None]).astype(i32), axis=1), N_EXPERTS - 1)
    nblk = (pad_end[-1:] // MOE_BLOCK).astype(i32)
    e_row = jnp.arange(N_EXPERTS, dtype=i32)
    used = padded > 0
    e_ord = jnp.cumsum(used.astype(i32)) - 1
    later_used = jnp.logical_and(e_row[None, :] > e_row[:, None], used[None, :])
    e_next = jnp.min(jnp.where(later_used, e_row[None, :], N_EXPERTS), axis=1)
    e_next = jnp.where(e_next == N_EXPERTS, -1, e_next)
    of_block = block_e[:, None] == e_row[None, :]
    block_eord = jnp.sum(jnp.where(of_block, e_ord[None, :], 0), axis=1)
    block_enext = jnp.sum(jnp.where(of_block, e_next[None, :], 0), axis=1)
    idx = jnp.concatenate([idx_p, idx_s], axis=1)
    rank = jnp.concatenate([rank_p, rank_s], axis=1)
    gates = jnp.concatenate([gate_p, gate_s], axis=1)
    onehot = idx[None] == jnp.arange(N_EXPERTS, dtype=i32)[:, None, None]
    tile_cnt = jnp.sum(onehot.reshape(N_EXPERTS, TOP_K, n_tiles, ROW_TILE).astype(i32), axis=(1, 3)).T
    by_time = jnp.concatenate([tile_cnt[-1:], tile_cnt[:-1]], axis=0)
    before_time = jnp.cumsum(by_time, axis=0) - by_time
    run_before = jnp.concatenate([before_time[1:], before_time[:1]], axis=0)
    tile_off = jnp.cumsum(tile_cnt, axis=1) - tile_cnt
    tile_dst = pad_start[None, :] + run_before
    delta = jnp.repeat((tile_off - run_before).T, ROW_TILE, axis=1)
    pos = rank + jnp.sum(jnp.where(onehot, delta[:, None, :], 0), axis=0)
    tcnt, tdst = tile_cnt.reshape(-1), tile_dst.reshape(-1)

    x_sorted = _dispatch(tcnt, tdst, pad_start + counts, padded - counts, nblk, pos, xn2_p, xn2_s,
                         n_blocks * MOE_BLOCK)
    y_sorted = _experts(block_e, nblk, block_eord, block_enext, x_sorted, wgu, bgu, wd, bd)
    y_p, y_s = _combine(tcnt, tdst, pos, gates, y_sorted, h1_p, h1_s, norm_final[None, :])

    y_prompt = y_p.reshape(batch, seq, D_MODEL)
    y_sample = y_s.reshape(nb, 1, D_MODEL)
    ret_state_prompt = s_fin[None]
    pool_state_prompt = p_fin[:, 1:, :][None]
    ret_state_sample = st_s[None]
    pool_state_sample = pool_s.reshape(nb, POOL_BUF, POOL_WIDTH)[None]
    return (y_prompt, y_sample, ret_state_prompt, pool_state_prompt, ret_state_sample, pool_state_sample)
```

```python
import functools

import jax
import jax.numpy as jnp
import numpy as np
from jax import lax
from jax.experimental import pallas as pl
from jax.experimental.pallas import tpu as pltpu

F32 = jnp.float32
BF16 = jnp.bfloat16

D_MODEL = 1024
N_META = 16
RET_HEADS = 4
RET_DK = 128
RET_DV = 256
RET_QK = RET_HEADS * RET_DK
RET_V = RET_HEADS * RET_DV
CHUNK = 128
ROPE_BASE = 10000.0
POOL_WINDOWS = (2, 4, 8, 16)
POOL_GROUPS = 4
POOL_GROUP_DIM = 128
POOL_WIDTH = POOL_GROUPS * POOL_GROUP_DIM
POOL_BUF = max(POOL_WINDOWS) - 1
N_EXPERTS = 32
TOP_K = 4
D_FF = D_MODEL
SWIGLU_LIMIT = 7.0
SWIGLU_ALPHA = 1.702
EPS = 1e-6
IN_WIDTHS = (RET_QK, RET_QK, RET_V, RET_V, POOL_WIDTH, D_MODEL, D_MODEL)
IN_TOTAL = sum(IN_WIDTHS)
IN_OFFS = tuple(int(s) for s in np.cumsum((0,) + IN_WIDTHS))

LANES = 128
ROW_CHUNKS = D_MODEL // (2 * LANES)
INPROJ_TILE = 512
MIXER_TILE = 512
MOE_BLOCK = 512
ROW_TILE = 256
SAMPLE_GROUP = 8
WEIGHT_CAST_ROWS = 128
VMEM_LIMIT = 56 * 1024 * 1024

assert N_META + 1 >= max(POOL_WINDOWS)
assert POOL_WINDOWS == (2, 4, 8, 16)


def _dot(a, b):
    return jnp.dot(a, b, preferred_element_type=F32)


def _rmsnorm(x, w):
    return x * lax.rsqrt(jnp.mean(x * x, axis=-1, keepdims=True) + EPS) * w


def _sigmoid(x):
    return 0.5 * jnp.tanh(0.5 * x) + 0.5


def _store_rows_as_tiles(ref, x):
    half = D_MODEL // 2
    hi = pltpu.bitcast(x[:, :half], jnp.uint32) & jnp.uint32(0xFFFF0000)
    lo = pltpu.bitcast(x[:, half:], jnp.uint32) >> 16
    ref[...] = (hi | lo).reshape(x.shape[0], ROW_CHUNKS, LANES)


def _load_rows_from_tiles(ref, rows):
    w = ref[...].reshape(rows, D_MODEL // 2)
    hi = pltpu.bitcast(w & jnp.uint32(0xFFFF0000), F32)
    lo = pltpu.bitcast(w << 16, F32)
    return jnp.concatenate([hi, lo], axis=1)


def _inproj_kernel(x_ref, nw_ref, w_ref, cos_ref, sin_ref,
                   q_ref, k_ref, v_ref, g_ref, p_ref, ga_ref, gb_ref):
    xn = _rmsnorm(x_ref[...], nw_ref[...]).astype(BF16)
    cos = cos_ref[...]
    sin = sin_ref[...]

    def seg(i):
        return _dot(xn, w_ref[:, IN_OFFS[i]:IN_OFFS[i + 1]])

    def rot(a):
        return a * cos + pltpu.roll(a, RET_DK // 2, 1) * sin

    q = seg(0)
    k = seg(1)
    for h in range(RET_HEADS):
        sl = slice(h * RET_DK, (h + 1) * RET_DK)
        q_ref[:, sl] = rot(q[:, sl]).astype(q_ref.dtype)
        k_ref[:, sl] = (rot(k[:, sl]) * (RET_DK ** -0.5)).astype(k_ref.dtype)
    v_ref[...] = seg(2).astype(v_ref.dtype)
    g_ref[...] = seg(3).astype(g_ref.dtype)
    p_ref[...] = seg(4)
    ga_ref[...] = seg(5).astype(ga_ref.dtype)
    gb_ref[...] = seg(6).astype(gb_ref.dtype)


def _inproj(x2d, nw, w_in_bf, cosf, sinf, tile, n_outer, act_dtype):
    rows = x2d.shape[0]
    n_inner = rows // (tile * n_outer)
    row_map = lambda b, j: (b * n_inner + j, 0)
    tab_map = lambda b, j: (j, 0)
    const = lambda b, j: (0, 0)
    widths = IN_WIDTHS
    dts = (act_dtype, act_dtype, act_dtype, act_dtype, F32, act_dtype, act_dtype)
    return pl.pallas_call(
        _inproj_kernel,
        grid=(n_outer, n_inner),
        in_specs=[
            pl.BlockSpec((tile, D_MODEL), row_map),
            pl.BlockSpec((1, D_MODEL), const),
            pl.BlockSpec((D_MODEL, IN_TOTAL), const, pipeline_mode=pl.Buffered(1)),
            pl.BlockSpec((tile, RET_DK), tab_map),
            pl.BlockSpec((tile, RET_DK), tab_map),
        ],
        out_specs=[pl.BlockSpec((tile, w), row_map) for w in widths],
        out_shape=[jax.ShapeDtypeStruct((rows, w), dt) for w, dt in zip(widths, dts)],
        compiler_params=pltpu.CompilerParams(
            dimension_semantics=("arbitrary", "arbitrary"), vmem_limit_bytes=VMEM_LIMIT),
        name="inproj",
    )(x2d, nw, w_in_bf, cosf, sinf)


def _group_norm(o, gn_row):
    mu = jnp.mean(o, axis=-1, keepdims=True)
    var = jnp.mean(jnp.square(o - mu), axis=-1, keepdims=True)
    return (o - mu) * lax.rsqrt(var + EPS) * gn_row


def _pool_branch(groups, poolw_ref, pscale_ref, wpool_ref):
    pm = [_dot(g.astype(BF16), poolw_ref[i]) for i, g in enumerate(groups)]
    pm = jnp.concatenate(pm, axis=1) * pscale_ref[...]
    return _dot(pm.astype(BF16), wpool_ref[...])


def _merge_tail(o_norm, g, ga, gb, yb, x, wret_ref, wout_ref):
    gf = g.astype(F32)
    ya = _dot((gf * _sigmoid(gf) * o_norm).astype(BF16), wret_ref[...])
    merged = _sigmoid(ga.astype(F32)) * ya + _sigmoid(gb.astype(F32)) * yb
    return x + _dot(merged.astype(BF16), wout_ref[...])


def _route(h1, nffn_ref, wrt_ref, br_ref, run_scr,
           xn2_ref, idx_ref, gate_ref, rank_ref):
    tm = h1.shape[0]
    xn2 = _rmsnorm(h1, nffn_ref[...]).astype(BF16)
    xn2_ref[...] = xn2
    logits = lax.dot_general(wrt_ref[...], xn2, (((1,), (1,)), ((), ())),
                             preferred_element_type=F32) + br_ref[...]
    e_iota = lax.broadcasted_iota(jnp.int32, (N_EXPERTS, tm), 0)
    work = logits
    vals, sels = [], []
    chosen = jnp.zeros((N_EXPERTS, tm), F32)
    for _ in range(TOP_K):
        m = jnp.max(work, axis=0, keepdims=True)
        sel = jnp.min(jnp.where(work == m, e_iota, N_EXPERTS), axis=0, keepdims=True)
        hit = e_iota == sel
        vals.append(m)
        sels.append(sel)
        chosen = jnp.where(hit, 1.0, chosen)
        work = jnp.where(hit, -jnp.inf, work)
    exps = [jnp.exp(v - vals[0]) for v in vals]
    denom = exps[0] + exps[1] + exps[2] + exps[3]
    gates = [e / denom for e in exps]
    r_i = lax.broadcasted_iota(jnp.int32, (tm, tm), 0)
    c_i = lax.broadcasted_iota(jnp.int32, (tm, tm), 1)
    before = jnp.where(r_i < c_i, 1.0, 0.0).astype(BF16)
    base = run_scr[...] + _dot(chosen.astype(BF16), before)
    for kk in range(TOP_K):
        rk = jnp.sum(jnp.where(e_iota == sels[kk], base, 0.0), axis=0, keepdims=True)
        rank_ref[kk:kk + 1, :] = rk.astype(jnp.int32)
        idx_ref[kk:kk + 1, :] = sels[kk]
        gate_ref[kk:kk + 1, :] = gates[kk]
    run_scr[...] = run_scr[...] + jnp.sum(chosen, axis=1, keepdims=True)


def _mixer_kernel(q_ref, k_ref, v_ref, g_ref, p_ref, ga_ref, gb_ref, x_ref,
                  kmeta_ref, vmeta_ref, pmeta_ref, mask_ref, qdec_ref, kdec_ref, cdec_ref, gn_ref,
                  poolw_ref, pscale_ref, wret_ref, wpool_ref, wout_ref, nffn_ref, wrt_ref, br_ref,
                  cnt0_ref,
                  h1_ref, xn2_ref, idx_ref, gate_ref, rank_ref, cnt_ref, sfin_ref, pfin_ref,
                  s_scr, ext_scr, o_scr, run_scr):
    b = pl.program_id(0)
    j = pl.program_id(1)
    nj = pl.num_programs(1)
    tm = q_ref.shape[0]

    def state_update(s_old, kc, vc, h):
        kd = (kc.astype(F32) * kdec_ref[h]).astype(BF16)
        upd = lax.dot_general(kd, vc, (((0,), (0,)), ((), ())), preferred_element_type=F32)
        return s_old * cdec_ref[h] + upd

    @pl.when(jnp.logical_and(b == 0, j == 0))
    def _():
        run_scr[...] = cnt0_ref[:, 0:1]

    @pl.when(j == 0)
    def _():
        for h in range(RET_HEADS):
            kc = kmeta_ref[:, h * RET_DK:(h + 1) * RET_DK]
            vc = vmeta_ref[:, h * RET_DV:(h + 1) * RET_DV]
            s_scr[h] = state_update(jnp.zeros((RET_DK, RET_DV), F32), kc, vc, h)
        ext_scr[0:N_META, :] = pmeta_ref[...]

    for c in range(tm // CHUNK):
        rows = slice(c * CHUNK, (c + 1) * CHUNK)
        for h in range(RET_HEADS):
            qc = q_ref[rows, h * RET_DK:(h + 1) * RET_DK]
            kc = k_ref[rows, h * RET_DK:(h + 1) * RET_DK]
            vc = v_ref[rows, h * RET_DV:(h + 1) * RET_DV]
            s_old = s_scr[h]
            scores = lax.dot_general(qc, kc, (((1,), (1,)), ((), ())),
                                     preferred_element_type=F32) * mask_ref[h]
            qd = (qc.astype(F32) * qdec_ref[h]).astype(BF16)
            lhs = jnp.concatenate([scores.astype(BF16), qd], axis=1)
            rhs = jnp.concatenate([vc, s_old.astype(BF16)], axis=0)
            o = _dot(lhs, rhs)
            s_scr[h] = state_update(s_old, kc, vc, h)
            o_scr[rows, h * RET_DV:(h + 1) * RET_DV] = _group_norm(
                o, gn_ref[:, h * RET_DV:(h + 1) * RET_DV])

    p = p_ref[...]
    ext_scr[N_META:N_META + tm, :] = p
    a = ext_scr[...]
    g1 = POOL_GROUP_DIM
    s2 = a + pltpu.roll(a, 1, 0)
    s4 = s2[:, g1:] + pltpu.roll(s2[:, g1:], 2, 0)
    s8 = s4[:, g1:] + pltpu.roll(s4[:, g1:], 4, 0)
    s16 = s8[:, g1:] + pltpu.roll(s8[:, g1:], 8, 0)
    sums = (s2[N_META:, :g1], s4[N_META:, :g1], s8[N_META:, :g1], s16[N_META:, :])
    groups = [sums[i] * (1.0 / POOL_WINDOWS[i]) - p[:, i * g1:(i + 1) * g1] for i in range(POOL_GROUPS)]
    ext_scr[0:N_META, :] = ext_scr[tm:tm + N_META, :]

    yb = _pool_branch(groups, poolw_ref, pscale_ref, wpool_ref)
    h1 = _merge_tail(o_scr[...], g_ref[...], ga_ref[...], gb_ref[...], yb, x_ref[...], wret_ref, wout_ref)
    h1_ref[...] = h1
    _route(h1, nffn_ref, wrt_ref, br_ref, run_scr, xn2_ref, idx_ref, gate_ref, rank_ref)
    cnt_ref[...] = jnp.broadcast_to(run_scr[...], cnt_ref.shape)

    @pl.when(j == nj - 1)
    def _():
        for h in range(RET_HEADS):
            sfin_ref[0, h] = s_scr[h]
        pfin_ref[0] = ext_scr[0:N_META, :]


def _mixer(proj, x2d, kmeta, vmeta, pmeta, dec, wts, cnt0, batch, seq):
    q, k, v, g, p, ga, gb = proj
    tm = MIXER_TILE
    nj = seq // tm
    rows = batch * seq
    row_map = lambda b, j: (b * nj + j, 0)
    lane_map = lambda b, j: (0, b * nj + j)
    c2 = lambda b, j: (0, 0)
    c3 = lambda b, j: (0, 0, 0)

    def whole(a):
        return pl.BlockSpec(a.shape, c2 if a.ndim == 2 else c3)

    mask, qdec, kdec, cdec = dec
    gn, poolw, pscale, wret, wpool, wout, nffn, wrt, br = wts
    in_arrays = [q, k, v, g, p, ga, gb, x2d, kmeta, vmeta, pmeta, mask, qdec, kdec, cdec, gn,
                 poolw, pscale, wret, wpool, wout, nffn, wrt, br, cnt0]
    in_specs = [pl.BlockSpec((tm, a.shape[1]), row_map) for a in in_arrays[:8]]
    in_specs += [whole(a) for a in in_arrays[8:]]
    out_shape = [
        jax.ShapeDtypeStruct((rows, D_MODEL), F32),
        jax.ShapeDtypeStruct((rows, D_MODEL), BF16),
        jax.ShapeDtypeStruct((TOP_K, rows), jnp.int32),
        jax.ShapeDtypeStruct((TOP_K, rows), F32),
        jax.ShapeDtypeStruct((TOP_K, rows), jnp.int32),
        jax.ShapeDtypeStruct((N_EXPERTS, LANES), F32),
        jax.ShapeDtypeStruct((batch, RET_HEADS, RET_DK, RET_DV), F32),
        jax.ShapeDtypeStruct((batch, N_META, POOL_WIDTH), F32),
    ]
    out_specs = [
        pl.BlockSpec((tm, D_MODEL), row_map),
        pl.BlockSpec((tm, D_MODEL), row_map),
        pl.BlockSpec((TOP_K, tm), lane_map),
        pl.BlockSpec((TOP_K, tm), lane_map),
        pl.BlockSpec((TOP_K, tm), lane_map),
        pl.BlockSpec((N_EXPERTS, LANES), c2),
        pl.BlockSpec((1, RET_HEADS, RET_DK, RET_DV), lambda b, j: (b, 0, 0, 0)),
        pl.BlockSpec((1, N_META, POOL_WIDTH), lambda b, j: (b, 0, 0)),
    ]
    return pl.pallas_call(
        _mixer_kernel,
        grid=(batch, nj),
        in_specs=in_specs,
        out_specs=out_specs,
        out_shape=out_shape,
        scratch_shapes=[
            pltpu.VMEM((RET_HEADS, RET_DK, RET_DV), F32),
            pltpu.VMEM((N_META + tm, POOL_WIDTH), F32),
            pltpu.VMEM((tm, RET_V), F32),
            pltpu.VMEM((N_EXPERTS, 1), F32),
        ],
        compiler_params=pltpu.CompilerParams(
            dimension_semantics=("arbitrary", "arbitrary"), vmem_limit_bytes=VMEM_LIMIT),
        name="mixer",
    )(*in_arrays)


def _sample_kernel(sdec_ref, qt_ref, kt_ref, q_ref, k_ref, v_ref, g_ref, p_ref, ga_ref, gb_ref, x_ref,
                   st_ref, pool_ref, gn_ref,
                   poolw_ref, pscale_ref, wret_ref, wpool_ref, wout_ref, nffn_ref, wrt_ref, br_ref,
                   stout_ref, poolout_ref, h1_ref, xn2_ref, idx_ref, gate_ref, rank_ref,
                   cnt_ref,
                   o_scr, run_scr):
    i = pl.program_id(0)
    n = pl.num_programs(0)
    grp = st_ref.shape[0]
    row0 = pl.multiple_of(i * grp, grp)

    @pl.when(i == 0)
    def _():
        run_scr[...] = jnp.zeros_like(run_scr)

    q8 = q_ref[pl.ds(row0, grp), :]
    k8 = k_ref[pl.ds(row0, grp), :]
    v8 = v_ref[pl.ds(row0, grp), :]
    for h in range(RET_HEADS):
        ksl = slice(h * RET_DK, (h + 1) * RET_DK)
        vsl = slice(h * RET_DV, (h + 1) * RET_DV)
        score = jnp.sum(q8[:, ksl] * k8[:, ksl], axis=1, keepdims=True) * sdec_ref[h, 0]
        intra = score * v8[:, vsl]
        for bb in range(grp):
            s_old = st_ref[bb, h]
            qcol = qt_ref[0, ksl, bb:bb + 1] * sdec_ref[h, 1]
            kcol = kt_ref[0, ksl, bb:bb + 1] * sdec_ref[h, 2]
            cross = jnp.sum(s_old * qcol, axis=0, keepdims=True)
            o_scr[pl.ds(row0 + bb, 1), vsl] = intra[bb:bb + 1, :] + cross
            stout_ref[bb, h] = s_old * sdec_ref[h, 3] + kcol * v8[bb:bb + 1, vsl]

    @pl.when(i == n - 1)
    def _():
        o = o_scr[...]
        o_norm = jnp.concatenate(
            [_group_norm(o[:, h * RET_DV:(h + 1) * RET_DV], gn_ref[:, h * RET_DV:(h + 1) * RET_DV])
             for h in range(RET_HEADS)], axis=1)
        p = p_ref[...]
        w = POOL_WIDTH
        g1 = POOL_GROUP_DIM

        def prev(r, lo):
            return pool_ref[:, r * w + lo:(r + 1) * w]

        s2 = p + prev(14, 0)
        s4 = s2[:, g1:] + prev(13, g1) + prev(12, g1)
        s8 = s4[:, g1:] + prev(11, 2 * g1) + prev(10, 2 * g1) + prev(9, 2 * g1) + prev(8, 2 * g1)
        s16 = s8[:, g1:]
        for r in range(7, -1, -1):
            s16 = s16 + prev(r, 3 * g1)
        sums = (s2[:, :g1], s4[:, :g1], s8[:, :g1], s16)
        groups = [sums[t] * (1.0 / POOL_WINDOWS[t]) - p[:, t * g1:(t + 1) * g1] for t in range(POOL_GROUPS)]
        poolout_ref[:, 0:(POOL_BUF - 1) * w] = pool_ref[:, w:POOL_BUF * w]
        poolout_ref[:, (POOL_BUF - 1) * w:] = p
        yb = _pool_branch(groups, poolw_ref, pscale_ref, wpool_ref)
        h1 = _merge_tail(o_norm, g_ref[...], ga_ref[...], gb_ref[...], yb, x_ref[...], wret_ref, wout_ref)
        h1_ref[...] = h1
        _route(h1, nffn_ref, wrt_ref, br_ref, run_scr, xn2_ref, idx_ref, gate_ref, rank_ref)
        cnt_ref[...] = jnp.broadcast_to(run_scr[...], cnt_ref.shape)


def _sample_mixer(sdec, qt, kt, proj, x2d, state, pool2d, wts):
    q, k, v, g, p, ga, gb = proj
    nb = x2d.shape[0]
    grp = SAMPLE_GROUP
    c2 = lambda i: (0, 0)

    def whole(a):
        return pl.BlockSpec(a.shape, c2)

    gn, poolw, pscale, wret, wpool, wout, nffn, wrt, br = wts
    in_arrays = [sdec, qt, kt, q, k, v, g, p, ga, gb, x2d, state, pool2d, gn,
                 poolw, pscale, wret, wpool, wout, nffn, wrt, br]
    in_specs = [pl.BlockSpec(memory_space=pltpu.SMEM),
                pl.BlockSpec((1, RET_QK, grp), lambda i: (i, 0, 0)),
                pl.BlockSpec((1, RET_QK, grp), lambda i: (i, 0, 0))]
    in_specs += [whole(a) for a in (q, k, v, g, p, ga, gb, x2d)]
    in_specs += [pl.BlockSpec((grp, RET_HEADS, RET_DK, RET_DV), lambda i: (i, 0, 0, 0)), whole(pool2d), whole(gn),
                 pl.BlockSpec(poolw.shape, lambda i: (0, 0, 0))]
    in_specs += [whole(a) for a in (pscale, wret, wpool, wout, nffn, wrt, br)]
    out_shape = [
        jax.ShapeDtypeStruct(state.shape, F32),
        jax.ShapeDtypeStruct(pool2d.shape, F32),
        jax.ShapeDtypeStruct((nb, D_MODEL), F32),
        jax.ShapeDtypeStruct((nb, D_MODEL), BF16),
        jax.ShapeDtypeStruct((TOP_K, nb), jnp.int32),
        jax.ShapeDtypeStruct((TOP_K, nb), F32),
        jax.ShapeDtypeStruct((TOP_K, nb), jnp.int32),
        jax.ShapeDtypeStruct((N_EXPERTS, LANES), F32),
    ]
    out_specs = [pl.BlockSpec((grp, RET_HEADS, RET_DK, RET_DV), lambda i: (i, 0, 0, 0))]
    out_specs += [pl.BlockSpec(s.shape, c2) for s in out_shape[1:]]
    return pl.pallas_call(
        _sample_kernel,
        grid=(nb // grp,),
        in_specs=in_specs,
        out_specs=out_specs,
        out_shape=out_shape,
        scratch_shapes=[pltpu.VMEM((nb, RET_V), F32), pltpu.VMEM((N_EXPERTS, 1), F32)],
        compiler_params=pltpu.CompilerParams(
            dimension_semantics=("arbitrary",), vmem_limit_bytes=VMEM_LIMIT),
        name="sample_mixer",
    )(*in_arrays)


TILE_ROWS = ROW_TILE * TOP_K


def _run_copy(src_ref, src_row, dst_ref, dst_row, n_rows, sem):
    return pltpu.make_async_copy(src_ref.at[pl.ds(src_row, n_rows)], dst_ref.at[pl.ds(dst_row, n_rows)], sem)


def _for_each_run(tcnt_ref, tile, fn):
    def body(e, off):
        n = tcnt_ref[tile * N_EXPERTS + e]

        @pl.when(n > 0)
        def _():
            fn(e, off, n)

        return off + n

    lax.fori_loop(0, N_EXPERTS, body, 0)


def _dispatch_kernel(tcnt_ref, tdst_ref, ttot_ref, zrow_ref, zcnt_ref, nblk_ref, pos_ref, xp_ref, xs_ref, out_ref,
                     sorted_scr, zero_scr, sem, zsem):
    i = pl.program_id(0)
    n_prompt = pl.num_programs(0) - 1
    n_blocks = out_ref.shape[0] // MOE_BLOCK
    slot = i % 2

    def wait_tile(tile, s):
        _run_copy(out_ref, 0, sorted_scr.at[s], 0, ttot_ref[tile], sem.at[s]).wait()

    def for_each_pad(fn):
        def body(e, carry):
            n = zcnt_ref[e]

            @pl.when(n > 0)
            def _():
                fn(_run_copy(zero_scr, 0, out_ref, zrow_ref[e], n, zsem))

            dead = nblk_ref[0] + e

            @pl.when(dead < n_blocks)
            def _():
                fn(_run_copy(zero_scr, 0, out_ref, dead * MOE_BLOCK, MOE_BLOCK, zsem))

            return carry

        lax.fori_loop(0, N_EXPERTS, body, 0)

    @pl.when(i == 0)
    def _():
        zero_scr[...] = jnp.zeros_like(zero_scr)
        for_each_pad(lambda cp: cp.start())

    def sort_tile(x_ref):
        r_iota = lax.broadcasted_iota(jnp.int32, (TILE_ROWS, ROW_TILE), 0)
        hit = r_iota == pos_ref[0:1, :]
        for kk in range(1, TOP_K):
            hit = jnp.logical_or(hit, r_iota == pos_ref[kk:kk + 1, :])
        perm = jnp.where(hit, 1.0, 0.0).astype(BF16)
        xs = _dot(perm, x_ref[...])

        @pl.when(i >= 2)
        def _():
            wait_tile(i - 2, slot)

        _store_rows_as_tiles(sorted_scr.at[slot], xs)

    @pl.when(i < n_prompt)
    def _():
        sort_tile(xp_ref)

    @pl.when(i == n_prompt)
    def _():
        sort_tile(xs_ref)

    _for_each_run(tcnt_ref, i, lambda e, off, n: _run_copy(
        sorted_scr.at[slot], off, out_ref, tdst_ref[i * N_EXPERTS + e], n, sem.at[slot]).start())

    @pl.when(i == n_prompt)
    def _():
        wait_tile(i - 1, 1 - slot)
        wait_tile(i, slot)
        for_each_pad(lambda cp: cp.wait())


def _dispatch(tcnt, tdst, ttot, zrow, zcnt, nblk, pos, xn2_p, xn2_s, n_sorted):
    n_prompt = xn2_p.shape[0] // ROW_TILE
    grid_spec = pltpu.PrefetchScalarGridSpec(
        num_scalar_prefetch=6,
        grid=(n_prompt + 1,),
        in_specs=[
            pl.BlockSpec((TOP_K, ROW_TILE), lambda i, *_: (0, i)),
            pl.BlockSpec((ROW_TILE, D_MODEL), lambda i, *_: (jnp.minimum(i, n_prompt - 1), 0)),
            pl.BlockSpec((ROW_TILE, D_MODEL), lambda i, *_: (0, 0)),
        ],
        out_specs=pl.BlockSpec(memory_space=pl.ANY),
        scratch_shapes=[
            pltpu.VMEM((2, TILE_ROWS, ROW_CHUNKS, LANES), jnp.uint32),
            pltpu.VMEM((MOE_BLOCK, ROW_CHUNKS, LANES), jnp.uint32),
            pltpu.SemaphoreType.DMA((2,)), pltpu.SemaphoreType.DMA(()),
        ],
    )
    return pl.pallas_call(
        _dispatch_kernel,
        grid_spec=grid_spec,
        out_shape=jax.ShapeDtypeStruct((n_sorted, ROW_CHUNKS, LANES), jnp.uint32),
        compiler_params=pltpu.CompilerParams(dimension_semantics=("arbitrary",)),
        name="moe_dispatch",
    )(tcnt, tdst, ttot, zrow, zcnt, nblk, pos, xn2_p, xn2_s)


def _expert_kernel(be_ref, nblk_ref, eord_ref, enext_ref, x_ref, wgu_ref, bgu_ref, wd_ref, bd_ref, y_ref,
                   wgu_f32, wd_f32, wgu_bf, wd_bf, wsem):
    i = pl.program_id(0)
    live = i < nblk_ref[0]
    new_expert = jnp.logical_or(i == 0, be_ref[i] != be_ref[jnp.maximum(i - 1, 0)])

    def weight_copies(e, s):
        return (pltpu.make_async_copy(wgu_ref.at[e], wgu_f32.at[s], wsem.at[0, s]),
                pltpu.make_async_copy(wd_ref.at[e], wd_f32.at[s], wsem.at[1, s]))

    @pl.when(jnp.logical_and(live, new_expert))
    def _():
        slot = eord_ref[i] % 2

        @pl.when(i == 0)
        def _():
            for cp in weight_copies(be_ref[0], 0):
                cp.start(priority=1)

        for cp in weight_copies(be_ref[i], slot):
            cp.wait()

        @pl.when(enext_ref[i] >= 0)
        def _():
            for cp in weight_copies(enext_ref[i], 1 - slot):
                cp.start(priority=1)

        def cast(c, carry):
            rows = pl.ds(pl.multiple_of(c * WEIGHT_CAST_ROWS, WEIGHT_CAST_ROWS), WEIGHT_CAST_ROWS)
            wgu_bf[rows, :] = wgu_f32[slot, rows, :].astype(BF16)
            wd_bf[rows, :] = wd_f32[slot, rows, :].astype(BF16)
            return carry

        lax.fori_loop(0, D_MODEL // WEIGHT_CAST_ROWS, cast, 0)

    @pl.when(jnp.logical_not(live))
    def _():
        y_ref[...] = jnp.zeros_like(y_ref)

    @pl.when(live)
    def _():
        x = _load_rows_from_tiles(x_ref, MOE_BLOCK)
        h = _dot(x.astype(BF16), wgu_bf[...]) + bgu_ref[0]
        gate = jnp.minimum(h[:, :D_FF], SWIGLU_LIMIT)
        up = jnp.clip(h[:, D_FF:], -SWIGLU_LIMIT, SWIGLU_LIMIT)
        glu = gate * _sigmoid(gate * SWIGLU_ALPHA)
        y = _dot(((up + 1.0) * glu).astype(BF16), wd_bf[...]) + bd_ref[0]
        _store_rows_as_tiles(y_ref, y.astype(BF16).astype(F32))


def _experts(block_e, nblk, block_eord, block_enext, x_sorted, wgu, bgu, wd, bd):
    n_blocks = x_sorted.shape[0] // MOE_BLOCK
    wmap = lambda i, be, *_: (be[i], 0, 0)
    rmap = lambda i, *_: (i, 0, 0)
    hbm = pl.BlockSpec(memory_space=pl.ANY)
    grid_spec = pltpu.PrefetchScalarGridSpec(
        num_scalar_prefetch=4,
        grid=(n_blocks,),
        in_specs=[
            pl.BlockSpec((MOE_BLOCK, ROW_CHUNKS, LANES), rmap),
            hbm,
            pl.BlockSpec((1, 1, 2 * D_FF), wmap),
            hbm,
            pl.BlockSpec((1, 1, D_MODEL), wmap),
        ],
        out_specs=pl.BlockSpec((MOE_BLOCK, ROW_CHUNKS, LANES), rmap),
        scratch_shapes=[
            pltpu.VMEM((2, D_MODEL, 2 * D_FF), F32), pltpu.VMEM((2, D_FF, D_MODEL), F32),
            pltpu.VMEM((D_MODEL, 2 * D_FF), BF16), pltpu.VMEM((D_FF, D_MODEL), BF16),
            pltpu.SemaphoreType.DMA((2, 2)),
        ],
    )
    return pl.pallas_call(
        _expert_kernel,
        grid_spec=grid_spec,
        out_shape=jax.ShapeDtypeStruct(x_sorted.shape, jnp.uint32),
        compiler_params=pltpu.CompilerParams(
            dimension_semantics=("arbitrary",), vmem_limit_bytes=VMEM_LIMIT),
        name="moe_experts",
    )(block_e, nblk, block_eord, block_enext, x_sorted, wgu, bgu, wd, bd)


def _combine_kernel(tcnt_ref, tdst_ref, ttot_ref, pos_ref, gate_ref, ys_ref, hp_ref, hs_ref, nf_ref, yp_ref, ysmp_ref,
                    runs_scr, sem):
    i = pl.program_id(0)
    n_tiles = pl.num_programs(0)
    n_prompt = n_tiles - 1
    slot = i % 2

    def start_runs(tile, s):
        _for_each_run(tcnt_ref, tile, lambda e, off, n: _run_copy(
            ys_ref, tdst_ref[tile * N_EXPERTS + e], runs_scr.at[s], off, n, sem.at[s]).start())

    @pl.when(i == 0)
    def _():
        start_runs(0, 0)

    @pl.when(i + 1 < n_tiles)
    def _():
        start_runs(i + 1, 1 - slot)

    _run_copy(ys_ref, 0, runs_scr.at[slot], 0, ttot_ref[i], sem.at[slot]).wait()
    ys = _load_rows_from_tiles(runs_scr.at[slot], TILE_ROWS).astype(BF16)
    r_iota = lax.broadcasted_iota(jnp.int32, (TILE_ROWS, ROW_TILE), 0)
    gmat = jnp.zeros((TILE_ROWS, ROW_TILE), F32)
    for kk in range(TOP_K):
        gmat = jnp.where(r_iota == pos_ref[kk:kk + 1, :], gate_ref[kk:kk + 1, :], gmat)
    g_hi = gmat.astype(BF16)
    g_lo = (gmat - g_hi.astype(F32)).astype(BF16)
    contract0 = (((0,), (0,)), ((), ()))
    moe = (lax.dot_general(g_hi, ys, contract0, preferred_element_type=F32)
           + lax.dot_general(g_lo, ys, contract0, preferred_element_type=F32))

    @pl.when(i < n_prompt)
    def _():
        yp_ref[...] = _rmsnorm(hp_ref[...] + moe, nf_ref[...])

    @pl.when(i == n_prompt)
    def _():
        ysmp_ref[...] = _rmsnorm(hs_ref[...] + moe, nf_ref[...])


def _combine(tcnt, tdst, ttot, pos, gates, y_sorted, h1_p, h1_s, nf):
    n_prompt = h1_p.shape[0] // ROW_TILE
    pmap = lambda i, *_: (jnp.minimum(i, n_prompt - 1), 0)
    smap = lambda i, *_: (0, 0)
    lmap = lambda i, *_: (0, i)
    grid_spec = pltpu.PrefetchScalarGridSpec(
        num_scalar_prefetch=3,
        grid=(n_prompt + 1,),
        in_specs=[
            pl.BlockSpec((TOP_K, ROW_TILE), lmap),
            pl.BlockSpec((TOP_K, ROW_TILE), lmap),
            pl.BlockSpec(memory_space=pl.ANY),
            pl.BlockSpec((ROW_TILE, D_MODEL), pmap),
            pl.BlockSpec((ROW_TILE, D_MODEL), smap),
            pl.BlockSpec((1, D_MODEL), smap),
        ],
        out_specs=[pl.BlockSpec((ROW_TILE, D_MODEL), pmap), pl.BlockSpec((ROW_TILE, D_MODEL), smap)],
        scratch_shapes=[pltpu.VMEM((2, TILE_ROWS, ROW_CHUNKS, LANES), jnp.uint32), pltpu.SemaphoreType.DMA((2,))],
    )
    return pl.pallas_call(
        _combine_kernel,
        grid_spec=grid_spec,
        out_shape=[jax.ShapeDtypeStruct(h1_p.shape, F32), jax.ShapeDtypeStruct(h1_s.shape, F32)],
        compiler_params=pltpu.CompilerParams(dimension_semantics=("arbitrary",)),
        name="moe_combine",
    )(tcnt, tdst, ttot, pos, gates, y_sorted, h1_p, h1_s, nf)


def _rotary_tables(pos):
    f = np.float32
    inv = np.power(f(ROPE_BASE), -np.arange(0, RET_DK, 2, dtype=f) / f(RET_DK)).astype(f)
    ang = (np.asarray(pos, f)[:, None] * inv[None, :]).astype(f)
    cos, sin = np.cos(ang).astype(f), np.sin(ang).astype(f)
    return np.concatenate([cos, cos], axis=1), np.concatenate([-sin, sin], axis=1)


def _decay_tables(chunk):
    f = np.float32
    log_g = np.log1p(-np.exp2(f(-5.0) - np.arange(RET_HEADS, dtype=f))).astype(f)
    i = np.arange(chunk, dtype=f)
    diff = i[:, None] - i[None, :]
    mask = np.where(diff[None] >= 0, np.exp(np.maximum(diff, f(0.0))[None] * log_g[:, None, None]), f(0.0)).astype(f)
    q_dec = np.exp((i + f(1.0))[None, :] * log_g[:, None]).astype(f)
    k_dec = np.exp((f(chunk) - f(1.0) - i)[None, :] * log_g[:, None]).astype(f)
    c_dec = np.exp(f(chunk) * log_g).astype(f)
    return mask, q_dec, k_dec, c_dec


def kernel(x_prompt, x_sample, state_ret, state_pool, meta_tokens, norm_mix, w_in, ret_gn, pool_w, pool_scale,
           w_ret_branch, w_pool_branch, w_out, norm_ffn, w_router, b_router, w_gate_up, b_gate_up, w_down, b_down,
           norm_final):
    batch, seq, _ = x_prompt.shape
    nb = x_sample.shape[0]
    past_len = 16384
    n_prompt_tok = batch * seq
    n_tok = n_prompt_tok + nb

    w_in_bf = w_in[0].astype(BF16)
    wts = (ret_gn[0][None, :], pool_w[0].astype(BF16), pool_scale[0][None, :],
           w_ret_branch[0].astype(BF16), w_pool_branch[0].astype(BF16), w_out[0].astype(BF16),
           norm_ffn[0][None, :], w_router[0].T.astype(BF16), b_router[0][:, None])
    wgu = w_gate_up[0]
    wd = w_down[0]
    bgu = b_gate_up[0][:, None, :]
    bd = b_down[0][:, None, :]
    nmix = norm_mix[0][None, :]

    cos_p, sin_p = _rotary_tables(N_META + np.arange(seq))
    cos_s, sin_s = _rotary_tables(np.concatenate([np.arange(N_META), np.full((nb,), past_len)]))
    mask, q_dec, k_dec, c_dec = _decay_tables(CHUNK)
    dec = (mask,
           np.ascontiguousarray(np.broadcast_to(q_dec[:, :, None], (RET_HEADS, CHUNK, RET_DK))),
           np.ascontiguousarray(np.broadcast_to(k_dec[:, :, None], (RET_HEADS, CHUNK, RET_DK))),
           np.ascontiguousarray(np.broadcast_to(c_dec[:, None, None], (RET_HEADS, 1, RET_DV))))
    m1, q1, k1, c1 = _decay_tables(1)
    sdec = np.stack([m1[:, 0, 0], q1[:, 0], k1[:, 0], c1], axis=1)

    x2d = x_prompt.reshape(n_prompt_tok, D_MODEL)
    proj_p = _inproj(x2d, nmix, w_in_bf, cos_p, sin_p, INPROJ_TILE, batch, BF16)
    xs2d = x_sample.reshape(nb, D_MODEL)
    x_small = jnp.concatenate([meta_tokens, xs2d], axis=0)
    proj_small = _inproj(x_small, nmix, w_in_bf, cos_s, sin_s, N_META + nb, 1, F32)
    proj_s = tuple(a[N_META:] for a in proj_small)

    lead = CHUNK - N_META
    kmeta = jnp.pad(proj_small[1][:N_META], ((lead, 0), (0, 0))).astype(BF16)
    vmeta = jnp.pad(proj_small[2][:N_META], ((lead, 0), (0, 0))).astype(BF16)
    pmeta = proj_small[4][:N_META]

    grp = SAMPLE_GROUP

    def cols(a):
        return a.T.reshape(RET_QK, nb // grp, grp).transpose(1, 0, 2)

    pool2d = state_pool[0].reshape(nb, POOL_BUF * POOL_WIDTH)
    (st_s, pool_s, h1_s, xn2_s, idx_s, gate_s, rank_s, cnt_s) = _sample_mixer(
        sdec, cols(proj_s[0]), cols(proj_s[1]), proj_s, xs2d, state_ret[0], pool2d, wts)

    (h1_p, xn2_p, idx_p, gate_p, rank_p, cnt, s_fin, p_fin) = _mixer(
        proj_p, x2d, kmeta, vmeta, pmeta, dec, wts, cnt_s, batch, seq)

    assert n_prompt_tok % ROW_TILE == 0 and nb <= ROW_TILE
    n_phantom = ROW_TILE - nb
    n_tiles = n_prompt_tok // ROW_TILE + 1
    i32 = jnp.int32
    counts = cnt[:, 0].astype(i32)
    padded = ((counts + MOE_BLOCK - 1) // MOE_BLOCK) * MOE_BLOCK
    pad_end = jnp.cumsum(padded)
    pad_start = pad_end - padded
    n_blocks = (n_tok * TOP_K) // MOE_BLOCK + N_EXPERTS
    block_row = jnp.arange(n_blocks, dtype=i32) * MOE_BLOCK
    block_e = jnp.minimum(jnp.sum((pad_end[None, :] <= block_row[:, None]).astype(i32), axis=1), N_EXPERTS - 1)
    nblk = (pad_end[-1:] // MOE_BLOCK).astype(i32)
    e_row = jnp.arange(N_EXPERTS, dtype=i32)
    used = padded > 0
    e_ord = jnp.cumsum(used.astype(i32)) - 1
    later_used = jnp.logical_and(e_row[None, :] > e_row[:, None], used[None, :])
    e_next = jnp.min(jnp.where(later_used, e_row[None, :], N_EXPERTS), axis=1)
    e_next = jnp.where(e_next == N_EXPERTS, -1, e_next)
    of_block = block_e[:, None] == e_row[None, :]
    block_eord = jnp.sum(jnp.where(of_block, e_ord[None, :], 0), axis=1)
    block_enext = jnp.sum(jnp.where(of_block, e_next[None, :], 0), axis=1)
    phantom = lambda fill, dt: jnp.full((TOP_K, n_phantom), fill, dt)
    idx = jnp.concatenate([idx_p, idx_s, phantom(-1, i32)], axis=1)
    rank = jnp.concatenate([rank_p, rank_s, phantom(0, i32)], axis=1)
    gates = jnp.concatenate([gate_p, gate_s, phantom(0.0, F32)], axis=1)
    onehot = idx[None] == jnp.arange(N_EXPERTS, dtype=i32)[:, None, None]
    tile_cnt = jnp.sum(onehot.reshape(N_EXPERTS, TOP_K, n_tiles, ROW_TILE).astype(i32), axis=(1, 3)).T
    by_time = jnp.concatenate([tile_cnt[-1:], tile_cnt[:-1]], axis=0)
    before_time = jnp.cumsum(by_time, axis=0) - by_time
    run_before = jnp.concatenate([before_time[1:], before_time[:1]], axis=0)
    tile_off = jnp.cumsum(tile_cnt, axis=1) - tile_cnt
    tile_dst = pad_start[None, :] + run_before
    delta = jnp.repeat((tile_off - run_before).T, ROW_TILE, axis=1)
    pos = rank + jnp.sum(jnp.where(onehot, delta[:, None, :], 0), axis=0)
    pos = jnp.where(idx >= 0, pos, -1)
    tcnt, tdst, ttot = tile_cnt.reshape(-1), tile_dst.reshape(-1), jnp.sum(tile_cnt, axis=1)

    tail = ((0, n_phantom), (0, 0))
    x_sorted = _dispatch(tcnt, tdst, ttot, pad_start + counts, padded - counts, nblk, pos, xn2_p,
                         jnp.pad(xn2_s, tail), n_blocks * MOE_BLOCK)
    y_sorted = _experts(block_e, nblk, block_eord, block_enext, x_sorted, wgu, bgu, wd, bd)
    y_p, y_s = _combine(tcnt, tdst, ttot, pos, gates, y_sorted, h1_p, jnp.pad(h1_s, tail), norm_final[None, :])

    y_prompt = y_p.reshape(batch, seq, D_MODEL)
    y_sample = y_s[:nb].reshape(nb, 1, D_MODEL)
    ret_state_prompt = s_fin[None]
    pool_state_prompt = p_fin[:, 1:, :][None]
    ret_state_sample = st_s[None]
    pool_state_sample = pool_s.reshape(nb, POOL_BUF, POOL_WIDTH)[None]
    return (y_prompt, y_sample, ret_state_prompt, pool_state_prompt, ret_state_sample, pool_state_sample)
```

```python
import functools

import jax
import jax.numpy as jnp
import numpy as np
from jax import lax
from jax.experimental import pallas as pl
from jax.experimental.pallas import tpu as pltpu

F32 = jnp.float32
BF16 = jnp.bfloat16

D_MODEL = 1024
N_META = 16
RET_HEADS = 4
RET_DK = 128
RET_DV = 256
RET_QK = RET_HEADS * RET_DK
RET_V = RET_HEADS * RET_DV
CHUNK = 128
ROPE_BASE = 10000.0
POOL_WINDOWS = (2, 4, 8, 16)
POOL_GROUPS = 4
POOL_GROUP_DIM = 128
POOL_WIDTH = POOL_GROUPS * POOL_GROUP_DIM
POOL_BUF = max(POOL_WINDOWS) - 1
N_EXPERTS = 32
TOP_K = 4
D_FF = D_MODEL
SWIGLU_LIMIT = 7.0
SWIGLU_ALPHA = 1.702
EPS = 1e-6
IN_WIDTHS = (RET_QK, RET_QK, RET_V, RET_V, POOL_WIDTH, D_MODEL, D_MODEL)
IN_TOTAL = sum(IN_WIDTHS)
IN_OFFS = tuple(int(s) for s in np.cumsum((0,) + IN_WIDTHS))

LANES = 128
ROW_CHUNKS = D_MODEL // (2 * LANES)
INPROJ_TILE = 512
MIXER_TILE = 512
MOE_BLOCK = 512
ROW_TILE = 256
SAMPLE_GROUP = 8
WEIGHT_CAST_ROWS = 128
FF_CHUNK = 256
VMEM_LIMIT = 56 * 1024 * 1024

assert N_META + 1 >= max(POOL_WINDOWS)
assert POOL_WINDOWS == (2, 4, 8, 16)


def _dot(a, b):
    return jnp.dot(a, b, preferred_element_type=F32)


def _rmsnorm(x, w):
    return x * lax.rsqrt(jnp.mean(x * x, axis=-1, keepdims=True) + EPS) * w


def _sigmoid(x):
    return 0.5 * jnp.tanh(0.5 * x) + 0.5


def _store_rows_as_tiles(ref, x):
    half = D_MODEL // 2
    hi = pltpu.bitcast(x[:, :half], jnp.uint32) & jnp.uint32(0xFFFF0000)
    lo = pltpu.bitcast(x[:, half:], jnp.uint32) >> 16
    ref[...] = (hi | lo).reshape(x.shape[0], ROW_CHUNKS, LANES)


def _load_rows_from_tiles(ref, rows):
    w = ref[...].reshape(rows, D_MODEL // 2)
    hi = pltpu.bitcast(w & jnp.uint32(0xFFFF0000), F32)
    lo = pltpu.bitcast(w << 16, F32)
    return jnp.concatenate([hi, lo], axis=1)


def _inproj_kernel(x_ref, nw_ref, w_ref, cos_ref, sin_ref,
                   q_ref, k_ref, v_ref, g_ref, p_ref, ga_ref, gb_ref):
    xn = _rmsnorm(x_ref[...], nw_ref[...]).astype(BF16)
    cos = cos_ref[...]
    sin = sin_ref[...]

    def seg(i):
        return _dot(xn, w_ref[:, IN_OFFS[i]:IN_OFFS[i + 1]])

    def rot(a):
        return a * cos + pltpu.roll(a, RET_DK // 2, 1) * sin

    q = seg(0)
    k = seg(1)
    for h in range(RET_HEADS):
        sl = slice(h * RET_DK, (h + 1) * RET_DK)
        q_ref[:, sl] = rot(q[:, sl]).astype(q_ref.dtype)
        k_ref[:, sl] = (rot(k[:, sl]) * (RET_DK ** -0.5)).astype(k_ref.dtype)
    v_ref[...] = seg(2).astype(v_ref.dtype)
    g_ref[...] = seg(3).astype(g_ref.dtype)
    p_ref[...] = seg(4)
    ga_ref[...] = seg(5).astype(ga_ref.dtype)
    gb_ref[...] = seg(6).astype(gb_ref.dtype)


def _inproj(x2d, nw, w_in_bf, cosf, sinf, tile, n_outer, act_dtype):
    rows = x2d.shape[0]
    n_inner = rows // (tile * n_outer)
    row_map = lambda b, j: (b * n_inner + j, 0)
    tab_map = lambda b, j: (j, 0)
    const = lambda b, j: (0, 0)
    widths = IN_WIDTHS
    dts = (act_dtype, act_dtype, act_dtype, act_dtype, F32, act_dtype, act_dtype)
    return pl.pallas_call(
        _inproj_kernel,
        grid=(n_outer, n_inner),
        in_specs=[
            pl.BlockSpec((tile, D_MODEL), row_map),
            pl.BlockSpec((1, D_MODEL), const),
            pl.BlockSpec((D_MODEL, IN_TOTAL), const, pipeline_mode=pl.Buffered(1)),
            pl.BlockSpec((tile, RET_DK), tab_map),
            pl.BlockSpec((tile, RET_DK), tab_map),
        ],
        out_specs=[pl.BlockSpec((tile, w), row_map) for w in widths],
        out_shape=[jax.ShapeDtypeStruct((rows, w), dt) for w, dt in zip(widths, dts)],
        compiler_params=pltpu.CompilerParams(
            dimension_semantics=("arbitrary", "arbitrary"), vmem_limit_bytes=VMEM_LIMIT),
        name="inproj",
    )(x2d, nw, w_in_bf, cosf, sinf)


def _group_norm(o, gn_row):
    mu = jnp.mean(o, axis=-1, keepdims=True)
    var = jnp.mean(jnp.square(o - mu), axis=-1, keepdims=True)
    return (o - mu) * lax.rsqrt(var + EPS) * gn_row


def _pool_branch(groups, poolw_ref, pscale_ref, wpool_ref):
    pm = [_dot(g.astype(BF16), poolw_ref[i]) for i, g in enumerate(groups)]
    pm = jnp.concatenate(pm, axis=1) * pscale_ref[...]
    return _dot(pm.astype(BF16), wpool_ref[...])


def _merge_tail(o_norm, g, ga, gb, yb, x, wret_ref, wout_ref):
    gf = g.astype(F32)
    ya = _dot((gf * _sigmoid(gf) * o_norm).astype(BF16), wret_ref[...])
    merged = _sigmoid(ga.astype(F32)) * ya + _sigmoid(gb.astype(F32)) * yb
    return x + _dot(merged.astype(BF16), wout_ref[...])


def _route(h1, nffn_ref, wrt_ref, br_ref, run_scr,
           xn2_ref, idx_ref, gate_ref, rank_ref):
    tm = h1.shape[0]
    xn2 = _rmsnorm(h1, nffn_ref[...]).astype(BF16)
    xn2_ref[...] = xn2
    logits = lax.dot_general(wrt_ref[...], xn2, (((1,), (1,)), ((), ())),
                             preferred_element_type=F32) + br_ref[...]
    e_iota = lax.broadcasted_iota(jnp.int32, (N_EXPERTS, tm), 0)
    work = logits
    vals, sels = [], []
    chosen = jnp.zeros((N_EXPERTS, tm), F32)
    for _ in range(TOP_K):
        m = jnp.max(work, axis=0, keepdims=True)
        sel = jnp.min(jnp.where(work == m, e_iota, N_EXPERTS), axis=0, keepdims=True)
        hit = e_iota == sel
        vals.append(m)
        sels.append(sel)
        chosen = jnp.where(hit, 1.0, chosen)
        work = jnp.where(hit, -jnp.inf, work)
    exps = [jnp.exp(v - vals[0]) for v in vals]
    denom = exps[0] + exps[1] + exps[2] + exps[3]
    gates = [e / denom for e in exps]
    r_i = lax.broadcasted_iota(jnp.int32, (tm, tm), 0)
    c_i = lax.broadcasted_iota(jnp.int32, (tm, tm), 1)
    before = jnp.where(r_i < c_i, 1.0, 0.0).astype(BF16)
    base = run_scr[...] + _dot(chosen.astype(BF16), before)
    for kk in range(TOP_K):
        rk = jnp.sum(jnp.where(e_iota == sels[kk], base, 0.0), axis=0, keepdims=True)
        rank_ref[kk:kk + 1, :] = rk.astype(jnp.int32)
        idx_ref[kk:kk + 1, :] = sels[kk]
        gate_ref[kk:kk + 1, :] = gates[kk]
    run_scr[...] = run_scr[...] + jnp.sum(chosen, axis=1, keepdims=True)


def _mixer_kernel(q_ref, k_ref, v_ref, g_ref, p_ref, ga_ref, gb_ref, x_ref,
                  kmeta_ref, vmeta_ref, pmeta_ref, mask_ref, qdec_ref, kdec_ref, cdec_ref, gn_ref,
                  poolw_ref, pscale_ref, wret_ref, wpool_ref, wout_ref, nffn_ref, wrt_ref, br_ref,
                  cnt0_ref,
                  h1_ref, xn2_ref, idx_ref, gate_ref, rank_ref, cnt_ref, sfin_ref, pfin_ref,
                  s_scr, ext_scr, o_scr, run_scr):
    b = pl.program_id(0)
    j = pl.program_id(1)
    nj = pl.num_programs(1)
    tm = q_ref.shape[0]

    def state_update(s_old, kc, vc, h):
        kd = (kc.astype(F32) * kdec_ref[h]).astype(BF16)
        upd = lax.dot_general(kd, vc, (((0,), (0,)), ((), ())), preferred_element_type=F32)
        return s_old * cdec_ref[h] + upd

    @pl.when(jnp.logical_and(b == 0, j == 0))
    def _():
        run_scr[...] = cnt0_ref[:, 0:1]

    @pl.when(j == 0)
    def _():
        for h in range(RET_HEADS):
            kc = kmeta_ref[:, h * RET_DK:(h + 1) * RET_DK]
            vc = vmeta_ref[:, h * RET_DV:(h + 1) * RET_DV]
            s_scr[h] = state_update(jnp.zeros((RET_DK, RET_DV), F32), kc, vc, h)
        ext_scr[0:N_META, :] = pmeta_ref[...]

    for c in range(tm // CHUNK):
        rows = slice(c * CHUNK, (c + 1) * CHUNK)
        for h in range(RET_HEADS):
            qc = q_ref[rows, h * RET_DK:(h + 1) * RET_DK]
            kc = k_ref[rows, h * RET_DK:(h + 1) * RET_DK]
            vc = v_ref[rows, h * RET_DV:(h + 1) * RET_DV]
            s_old = s_scr[h]
            scores = lax.dot_general(qc, kc, (((1,), (1,)), ((), ())),
                                     preferred_element_type=F32) * mask_ref[h]
            qd = (qc.astype(F32) * qdec_ref[h]).astype(BF16)
            lhs = jnp.concatenate([scores.astype(BF16), qd], axis=1)
            rhs = jnp.concatenate([vc, s_old.astype(BF16)], axis=0)
            o = _dot(lhs, rhs)
            s_scr[h] = state_update(s_old, kc, vc, h)
            o_scr[rows, h * RET_DV:(h + 1) * RET_DV] = _group_norm(
                o, gn_ref[:, h * RET_DV:(h + 1) * RET_DV])

    p = p_ref[...]
    ext_scr[N_META:N_META + tm, :] = p
    a = ext_scr[...]
    g1 = POOL_GROUP_DIM
    s2 = a + pltpu.roll(a, 1, 0)
    s4 = s2[:, g1:] + pltpu.roll(s2[:, g1:], 2, 0)
    s8 = s4[:, g1:] + pltpu.roll(s4[:, g1:], 4, 0)
    s16 = s8[:, g1:] + pltpu.roll(s8[:, g1:], 8, 0)
    sums = (s2[N_META:, :g1], s4[N_META:, :g1], s8[N_META:, :g1], s16[N_META:, :])
    groups = [sums[i] * (1.0 / POOL_WINDOWS[i]) - p[:, i * g1:(i + 1) * g1] for i in range(POOL_GROUPS)]
    ext_scr[0:N_META, :] = ext_scr[tm:tm + N_META, :]

    yb = _pool_branch(groups, poolw_ref, pscale_ref, wpool_ref)
    h1 = _merge_tail(o_scr[...], g_ref[...], ga_ref[...], gb_ref[...], yb, x_ref[...], wret_ref, wout_ref)
    h1_ref[...] = h1
    _route(h1, nffn_ref, wrt_ref, br_ref, run_scr, xn2_ref, idx_ref, gate_ref, rank_ref)
    cnt_ref[...] = jnp.broadcast_to(run_scr[...], cnt_ref.shape)

    @pl.when(j == nj - 1)
    def _():
        for h in range(RET_HEADS):
            sfin_ref[0, h] = s_scr[h]
        pfin_ref[0] = ext_scr[0:N_META, :]


def _mixer(proj, x2d, kmeta, vmeta, pmeta, dec, wts, cnt0, batch, seq):
    q, k, v, g, p, ga, gb = proj
    tm = MIXER_TILE
    nj = seq // tm
    rows = batch * seq
    row_map = lambda b, j: (b * nj + j, 0)
    lane_map = lambda b, j: (0, b * nj + j)
    c2 = lambda b, j: (0, 0)
    c3 = lambda b, j: (0, 0, 0)

    def whole(a):
        return pl.BlockSpec(a.shape, c2 if a.ndim == 2 else c3)

    mask, qdec, kdec, cdec = dec
    gn, poolw, pscale, wret, wpool, wout, nffn, wrt, br = wts
    in_arrays = [q, k, v, g, p, ga, gb, x2d, kmeta, vmeta, pmeta, mask, qdec, kdec, cdec, gn,
                 poolw, pscale, wret, wpool, wout, nffn, wrt, br, cnt0]
    in_specs = [pl.BlockSpec((tm, a.shape[1]), row_map) for a in in_arrays[:8]]
    in_specs += [whole(a) for a in in_arrays[8:]]
    out_shape = [
        jax.ShapeDtypeStruct((rows, D_MODEL), F32),
        jax.ShapeDtypeStruct((rows, D_MODEL), BF16),
        jax.ShapeDtypeStruct((TOP_K, rows), jnp.int32),
        jax.ShapeDtypeStruct((TOP_K, rows), F32),
        jax.ShapeDtypeStruct((TOP_K, rows), jnp.int32),
        jax.ShapeDtypeStruct((N_EXPERTS, LANES), F32),
        jax.ShapeDtypeStruct((batch, RET_HEADS, RET_DK, RET_DV), F32),
        jax.ShapeDtypeStruct((batch, N_META, POOL_WIDTH), F32),
    ]
    out_specs = [
        pl.BlockSpec((tm, D_MODEL), row_map),
        pl.BlockSpec((tm, D_MODEL), row_map),
        pl.BlockSpec((TOP_K, tm), lane_map),
        pl.BlockSpec((TOP_K, tm), lane_map),
        pl.BlockSpec((TOP_K, tm), lane_map),
        pl.BlockSpec((N_EXPERTS, LANES), c2),
        pl.BlockSpec((1, RET_HEADS, RET_DK, RET_DV), lambda b, j: (b, 0, 0, 0)),
        pl.BlockSpec((1, N_META, POOL_WIDTH), lambda b, j: (b, 0, 0)),
    ]
    return pl.pallas_call(
        _mixer_kernel,
        grid=(batch, nj),
        in_specs=in_specs,
        out_specs=out_specs,
        out_shape=out_shape,
        scratch_shapes=[
            pltpu.VMEM((RET_HEADS, RET_DK, RET_DV), F32),
            pltpu.VMEM((N_META + tm, POOL_WIDTH), F32),
            pltpu.VMEM((tm, RET_V), F32),
            pltpu.VMEM((N_EXPERTS, 1), F32),
        ],
        compiler_params=pltpu.CompilerParams(
            dimension_semantics=("arbitrary", "arbitrary"), vmem_limit_bytes=VMEM_LIMIT),
        name="mixer",
    )(*in_arrays)


def _sample_kernel(sdec_ref, qt_ref, kt_ref, q_ref, k_ref, v_ref, g_ref, p_ref, ga_ref, gb_ref, x_ref,
                   st_ref, pool_ref, gn_ref,
                   poolw_ref, pscale_ref, wret_ref, wpool_ref, wout_ref, nffn_ref, wrt_ref, br_ref,
                   stout_ref, poolout_ref, h1_ref, xn2_ref, idx_ref, gate_ref, rank_ref,
                   cnt_ref,
                   o_scr, run_scr):
    i = pl.program_id(0)
    n = pl.num_programs(0)
    grp = st_ref.shape[0]
    row0 = pl.multiple_of(i * grp, grp)

    @pl.when(i == 0)
    def _():
        run_scr[...] = jnp.zeros_like(run_scr)

    q8 = q_ref[pl.ds(row0, grp), :]
    k8 = k_ref[pl.ds(row0, grp), :]
    v8 = v_ref[pl.ds(row0, grp), :]
    for h in range(RET_HEADS):
        ksl = slice(h * RET_DK, (h + 1) * RET_DK)
        vsl = slice(h * RET_DV, (h + 1) * RET_DV)
        score = jnp.sum(q8[:, ksl] * k8[:, ksl], axis=1, keepdims=True) * sdec_ref[h, 0]
        intra = score * v8[:, vsl]
        for bb in range(grp):
            s_old = st_ref[bb, h]
            qcol = qt_ref[0, ksl, bb:bb + 1] * sdec_ref[h, 1]
            kcol = kt_ref[0, ksl, bb:bb + 1] * sdec_ref[h, 2]
            cross = jnp.sum(s_old * qcol, axis=0, keepdims=True)
            o_scr[pl.ds(row0 + bb, 1), vsl] = intra[bb:bb + 1, :] + cross
            stout_ref[bb, h] = s_old * sdec_ref[h, 3] + kcol * v8[bb:bb + 1, vsl]

    @pl.when(i == n - 1)
    def _():
        o = o_scr[...]
        o_norm = jnp.concatenate(
            [_group_norm(o[:, h * RET_DV:(h + 1) * RET_DV], gn_ref[:, h * RET_DV:(h + 1) * RET_DV])
             for h in range(RET_HEADS)], axis=1)
        p = p_ref[...]
        w = POOL_WIDTH
        g1 = POOL_GROUP_DIM

        def prev(r, lo):
            return pool_ref[:, r * w + lo:(r + 1) * w]

        s2 = p + prev(14, 0)
        s4 = s2[:, g1:] + prev(13, g1) + prev(12, g1)
        s8 = s4[:, g1:] + prev(11, 2 * g1) + prev(10, 2 * g1) + prev(9, 2 * g1) + prev(8, 2 * g1)
        s16 = s8[:, g1:]
        for r in range(7, -1, -1):
            s16 = s16 + prev(r, 3 * g1)
        sums = (s2[:, :g1], s4[:, :g1], s8[:, :g1], s16)
        groups = [sums[t] * (1.0 / POOL_WINDOWS[t]) - p[:, t * g1:(t + 1) * g1] for t in range(POOL_GROUPS)]
        poolout_ref[:, 0:(POOL_BUF - 1) * w] = pool_ref[:, w:POOL_BUF * w]
        poolout_ref[:, (POOL_BUF - 1) * w:] = p
        yb = _pool_branch(groups, poolw_ref, pscale_ref, wpool_ref)
        h1 = _merge_tail(o_norm, g_ref[...], ga_ref[...], gb_ref[...], yb, x_ref[...], wret_ref, wout_ref)
        h1_ref[...] = h1
        _route(h1, nffn_ref, wrt_ref, br_ref, run_scr, xn2_ref, idx_ref, gate_ref, rank_ref)
        cnt_ref[...] = jnp.broadcast_to(run_scr[...], cnt_ref.shape)


def _sample_mixer(sdec, qt, kt, proj, x2d, state, pool2d, wts):
    q, k, v, g, p, ga, gb = proj
    nb = x2d.shape[0]
    grp = SAMPLE_GROUP
    c2 = lambda i: (0, 0)

    def whole(a):
        return pl.BlockSpec(a.shape, c2)

    gn, poolw, pscale, wret, wpool, wout, nffn, wrt, br = wts
    in_arrays = [sdec, qt, kt, q, k, v, g, p, ga, gb, x2d, state, pool2d, gn,
                 poolw, pscale, wret, wpool, wout, nffn, wrt, br]
    in_specs = [pl.BlockSpec(memory_space=pltpu.SMEM),
                pl.BlockSpec((1, RET_QK, grp), lambda i: (i, 0, 0)),
                pl.BlockSpec((1, RET_QK, grp), lambda i: (i, 0, 0))]
    in_specs += [whole(a) for a in (q, k, v, g, p, ga, gb, x2d)]
    in_specs += [pl.BlockSpec((grp, RET_HEADS, RET_DK, RET_DV), lambda i: (i, 0, 0, 0)), whole(pool2d), whole(gn),
                 pl.BlockSpec(poolw.shape, lambda i: (0, 0, 0))]
    in_specs += [whole(a) for a in (pscale, wret, wpool, wout, nffn, wrt, br)]
    out_shape = [
        jax.ShapeDtypeStruct(state.shape, F32),
        jax.ShapeDtypeStruct(pool2d.shape, F32),
        jax.ShapeDtypeStruct((nb, D_MODEL), F32),
        jax.ShapeDtypeStruct((nb, D_MODEL), BF16),
        jax.ShapeDtypeStruct((TOP_K, nb), jnp.int32),
        jax.ShapeDtypeStruct((TOP_K, nb), F32),
        jax.ShapeDtypeStruct((TOP_K, nb), jnp.int32),
        jax.ShapeDtypeStruct((N_EXPERTS, LANES), F32),
    ]
    out_specs = [pl.BlockSpec((grp, RET_HEADS, RET_DK, RET_DV), lambda i: (i, 0, 0, 0))]
    out_specs += [pl.BlockSpec(s.shape, c2) for s in out_shape[1:]]
    return pl.pallas_call(
        _sample_kernel,
        grid=(nb // grp,),
        in_specs=in_specs,
        out_specs=out_specs,
        out_shape=out_shape,
        scratch_shapes=[pltpu.VMEM((nb, RET_V), F32), pltpu.VMEM((N_EXPERTS, 1), F32)],
        compiler_params=pltpu.CompilerParams(
            dimension_semantics=("arbitrary",), vmem_limit_bytes=VMEM_LIMIT),
        name="sample_mixer",
    )(*in_arrays)


TILE_ROWS = ROW_TILE * TOP_K


def _run_copy(src_ref, src_row, dst_ref, dst_row, n_rows, sem):
    return pltpu.make_async_copy(src_ref.at[pl.ds(src_row, n_rows)], dst_ref.at[pl.ds(dst_row, n_rows)], sem)


def _for_each_run(tcnt_ref, tile, fn):
    def body(e, off):
        n = tcnt_ref[tile * N_EXPERTS + e]

        @pl.when(n > 0)
        def _():
            fn(e, off, n)

        return off + n

    lax.fori_loop(0, N_EXPERTS, body, 0)


def _dispatch_kernel(tcnt_ref, tdst_ref, ttot_ref, zrow_ref, zcnt_ref, nblk_ref, pos_ref, xp_ref, xs_ref, out_ref,
                     sorted_scr, zero_scr, sem, zsem):
    i = pl.program_id(0)
    n_prompt = pl.num_programs(0) - 1
    n_blocks = out_ref.shape[0] // MOE_BLOCK
    slot = i % 2

    def wait_tile(tile, s):
        _run_copy(out_ref, 0, sorted_scr.at[s], 0, ttot_ref[tile], sem.at[s]).wait()

    def for_each_pad(fn):
        def body(e, carry):
            n = zcnt_ref[e]

            @pl.when(n > 0)
            def _():
                fn(_run_copy(zero_scr, 0, out_ref, zrow_ref[e], n, zsem))

            dead = nblk_ref[0] + e

            @pl.when(dead < n_blocks)
            def _():
                fn(_run_copy(zero_scr, 0, out_ref, dead * MOE_BLOCK, MOE_BLOCK, zsem))

            return carry

        lax.fori_loop(0, N_EXPERTS, body, 0)

    @pl.when(i == 0)
    def _():
        zero_scr[...] = jnp.zeros_like(zero_scr)
        for_each_pad(lambda cp: cp.start())

    def sort_tile(x_ref):
        r_iota = lax.broadcasted_iota(jnp.int32, (TILE_ROWS, ROW_TILE), 0)
        hit = r_iota == pos_ref[0:1, :]
        for kk in range(1, TOP_K):
            hit = jnp.logical_or(hit, r_iota == pos_ref[kk:kk + 1, :])
        perm = jnp.where(hit, 1.0, 0.0).astype(BF16)
        xs = _dot(perm, x_ref[...])

        @pl.when(i >= 2)
        def _():
            wait_tile(i - 2, slot)

        _store_rows_as_tiles(sorted_scr.at[slot], xs)

    @pl.when(i < n_prompt)
    def _():
        sort_tile(xp_ref)

    @pl.when(i == n_prompt)
    def _():
        sort_tile(xs_ref)

    _for_each_run(tcnt_ref, i, lambda e, off, n: _run_copy(
        sorted_scr.at[slot], off, out_ref, tdst_ref[i * N_EXPERTS + e], n, sem.at[slot]).start())

    @pl.when(i == n_prompt)
    def _():
        wait_tile(i - 1, 1 - slot)
        wait_tile(i, slot)
        for_each_pad(lambda cp: cp.wait())


def _dispatch(tcnt, tdst, ttot, zrow, zcnt, nblk, pos, xn2_p, xn2_s, n_sorted):
    n_prompt = xn2_p.shape[0] // ROW_TILE
    grid_spec = pltpu.PrefetchScalarGridSpec(
        num_scalar_prefetch=6,
        grid=(n_prompt + 1,),
        in_specs=[
            pl.BlockSpec((TOP_K, ROW_TILE), lambda i, *_: (0, i)),
            pl.BlockSpec((ROW_TILE, D_MODEL), lambda i, *_: (jnp.minimum(i, n_prompt - 1), 0)),
            pl.BlockSpec((ROW_TILE, D_MODEL), lambda i, *_: (0, 0)),
        ],
        out_specs=pl.BlockSpec(memory_space=pl.ANY),
        scratch_shapes=[
            pltpu.VMEM((2, TILE_ROWS, ROW_CHUNKS, LANES), jnp.uint32),
            pltpu.VMEM((MOE_BLOCK, ROW_CHUNKS, LANES), jnp.uint32),
            pltpu.SemaphoreType.DMA((2,)), pltpu.SemaphoreType.DMA(()),
        ],
    )
    return pl.pallas_call(
        _dispatch_kernel,
        grid_spec=grid_spec,
        out_shape=jax.ShapeDtypeStruct((n_sorted, ROW_CHUNKS, LANES), jnp.uint32),
        compiler_params=pltpu.CompilerParams(dimension_semantics=("arbitrary",)),
        name="moe_dispatch",
    )(tcnt, tdst, ttot, zrow, zcnt, nblk, pos, xn2_p, xn2_s)


def _expert_kernel(be_ref, nblk_ref, eord_ref, enext_ref, x_ref, wgu_ref, bgu_ref, wd_ref, bd_ref, y_ref,
                   wgu_f32, wd_f32, wgu_bf, wd_bf, wsem):
    i = pl.program_id(0)
    live = i < nblk_ref[0]
    new_expert = jnp.logical_or(i == 0, be_ref[i] != be_ref[jnp.maximum(i - 1, 0)])

    def weight_copies(e, s):
        return (pltpu.make_async_copy(wgu_ref.at[e], wgu_f32.at[s], wsem.at[0, s]),
                pltpu.make_async_copy(wd_ref.at[e], wd_f32.at[s], wsem.at[1, s]))

    @pl.when(jnp.logical_and(live, new_expert))
    def _():
        slot = eord_ref[i] % 2

        @pl.when(i == 0)
        def _():
            for cp in weight_copies(be_ref[0], 0):
                cp.start(priority=1)

        for cp in weight_copies(be_ref[i], slot):
            cp.wait()

        @pl.when(enext_ref[i] >= 0)
        def _():
            for cp in weight_copies(enext_ref[i], 1 - slot):
                cp.start(priority=1)

        def cast(c, carry):
            rows = pl.ds(pl.multiple_of(c * WEIGHT_CAST_ROWS, WEIGHT_CAST_ROWS), WEIGHT_CAST_ROWS)
            wgu_bf[rows, :] = wgu_f32[slot, rows, :].astype(BF16)
            wd_bf[rows, :] = wd_f32[slot, rows, :].astype(BF16)
            return carry

        lax.fori_loop(0, D_MODEL // WEIGHT_CAST_ROWS, cast, 0)

    @pl.when(jnp.logical_not(live))
    def _():
        y_ref[...] = jnp.zeros_like(y_ref)

    @pl.when(live)
    def _():
        x = _load_rows_from_tiles(x_ref, MOE_BLOCK).astype(BF16)
        acts = []
        for c0 in range(0, D_FF, FF_CHUNK):
            gate = _dot(x, wgu_bf[:, c0:c0 + FF_CHUNK]) + bgu_ref[0, :, c0:c0 + FF_CHUNK]
            up = _dot(x, wgu_bf[:, D_FF + c0:D_FF + c0 + FF_CHUNK]) + bgu_ref[0, :, D_FF + c0:D_FF + c0 + FF_CHUNK]
            gate = jnp.minimum(gate, SWIGLU_LIMIT)
            up = jnp.clip(up, -SWIGLU_LIMIT, SWIGLU_LIMIT)
            acts.append(((up + 1.0) * (gate * _sigmoid(gate * SWIGLU_ALPHA))).astype(BF16))
        y = _dot(jnp.concatenate(acts, axis=1), wd_bf[...]) + bd_ref[0]
        _store_rows_as_tiles(y_ref, y.astype(BF16).astype(F32))


def _experts(block_e, nblk, block_eord, block_enext, x_sorted, wgu, bgu, wd, bd):
    n_blocks = x_sorted.shape[0] // MOE_BLOCK
    wmap = lambda i, be, *_: (be[i], 0, 0)
    rmap = lambda i, *_: (i, 0, 0)
    hbm = pl.BlockSpec(memory_space=pl.ANY)
    grid_spec = pltpu.PrefetchScalarGridSpec(
        num_scalar_prefetch=4,
        grid=(n_blocks,),
        in_specs=[
            pl.BlockSpec((MOE_BLOCK, ROW_CHUNKS, LANES), rmap),
            hbm,
            pl.BlockSpec((1, 1, 2 * D_FF), wmap),
            hbm,
            pl.BlockSpec((1, 1, D_MODEL), wmap),
        ],
        out_specs=pl.BlockSpec((MOE_BLOCK, ROW_CHUNKS, LANES), rmap),
        scratch_shapes=[
            pltpu.VMEM((2, D_MODEL, 2 * D_FF), F32), pltpu.VMEM((2, D_FF, D_MODEL), F32),
            pltpu.VMEM((D_MODEL, 2 * D_FF), BF16), pltpu.VMEM((D_FF, D_MODEL), BF16),
            pltpu.SemaphoreType.DMA((2, 2)),
        ],
    )
    return pl.pallas_call(
        _expert_kernel,
        grid_spec=grid_spec,
        out_shape=jax.ShapeDtypeStruct(x_sorted.shape, jnp.uint32),
        compiler_params=pltpu.CompilerParams(
            dimension_semantics=("arbitrary",), vmem_limit_bytes=VMEM_LIMIT),
        name="moe_experts",
    )(block_e, nblk, block_eord, block_enext, x_sorted, wgu, bgu, wd, bd)


def _combine_kernel(tcnt_ref, tdst_ref, ttot_ref, pos_ref, gate_ref, ys_ref, hp_ref, hs_ref, nf_ref, yp_ref, ysmp_ref,
                    runs_scr, sem):
    i = pl.program_id(0)
    n_tiles = pl.num_programs(0)
    n_prompt = n_tiles - 1
    slot = i % 2

    def start_runs(tile, s):
        _for_each_run(tcnt_ref, tile, lambda e, off, n: _run_copy(
            ys_ref, tdst_ref[tile * N_EXPERTS + e], runs_scr.at[s], off, n, sem.at[s]).start())

    @pl.when(i == 0)
    def _():
        start_runs(0, 0)

    @pl.when(i + 1 < n_tiles)
    def _():
        start_runs(i + 1, 1 - slot)

    _run_copy(ys_ref, 0, runs_scr.at[slot], 0, ttot_ref[i], sem.at[slot]).wait()
    ys = _load_rows_from_tiles(runs_scr.at[slot], TILE_ROWS).astype(BF16)
    r_iota = lax.broadcasted_iota(jnp.int32, (TILE_ROWS, ROW_TILE), 0)
    gmat = jnp.zeros((TILE_ROWS, ROW_TILE), F32)
    for kk in range(TOP_K):
        gmat = jnp.where(r_iota == pos_ref[kk:kk + 1, :], gate_ref[kk:kk + 1, :], gmat)
    moe = lax.dot_general(gmat.astype(BF16), ys, (((0,), (0,)), ((), ())), preferred_element_type=F32)

    @pl.when(i < n_prompt)
    def _():
        yp_ref[...] = _rmsnorm(hp_ref[...] + moe, nf_ref[...])

    @pl.when(i == n_prompt)
    def _():
        ysmp_ref[...] = _rmsnorm(hs_ref[...] + moe, nf_ref[...])


def _combine(tcnt, tdst, ttot, pos, gates, y_sorted, h1_p, h1_s, nf):
    n_prompt = h1_p.shape[0] // ROW_TILE
    pmap = lambda i, *_: (jnp.minimum(i, n_prompt - 1), 0)
    smap = lambda i, *_: (0, 0)
    lmap = lambda i, *_: (0, i)
    grid_spec = pltpu.PrefetchScalarGridSpec(
        num_scalar_prefetch=3,
        grid=(n_prompt + 1,),
        in_specs=[
            pl.BlockSpec((TOP_K, ROW_TILE), lmap),
            pl.BlockSpec((TOP_K, ROW_TILE), lmap),
            pl.BlockSpec(memory_space=pl.ANY),
            pl.BlockSpec((ROW_TILE, D_MODEL), pmap),
            pl.BlockSpec((ROW_TILE, D_MODEL), smap),
            pl.BlockSpec((1, D_MODEL), smap),
        ],
        out_specs=[pl.BlockSpec((ROW_TILE, D_MODEL), pmap), pl.BlockSpec((ROW_TILE, D_MODEL), smap)],
        scratch_shapes=[pltpu.VMEM((2, TILE_ROWS, ROW_CHUNKS, LANES), jnp.uint32), pltpu.SemaphoreType.DMA((2,))],
    )
    return pl.pallas_call(
        _combine_kernel,
        grid_spec=grid_spec,
        out_shape=[jax.ShapeDtypeStruct(h1_p.shape, F32), jax.ShapeDtypeStruct(h1_s.shape, F32)],
        compiler_params=pltpu.CompilerParams(dimension_semantics=("arbitrary",)),
        name="moe_combine",
    )(tcnt, tdst, ttot, pos, gates, y_sorted, h1_p, h1_s, nf)


def _rotary_tables(pos):
    f = np.float32
    inv = np.power(f(ROPE_BASE), -np.arange(0, RET_DK, 2, dtype=f) / f(RET_DK)).astype(f)
    ang = (np.asarray(pos, f)[:, None] * inv[None, :]).astype(f)
    cos, sin = np.cos(ang).astype(f), np.sin(ang).astype(f)
    return np.concatenate([cos, cos], axis=1), np.concatenate([-sin, sin], axis=1)


def _decay_tables(chunk):
    f = np.float32
    log_g = np.log1p(-np.exp2(f(-5.0) - np.arange(RET_HEADS, dtype=f))).astype(f)
    i = np.arange(chunk, dtype=f)
    diff = i[:, None] - i[None, :]
    mask = np.where(diff[None] >= 0, np.exp(np.maximum(diff, f(0.0))[None] * log_g[:, None, None]), f(0.0)).astype(f)
    q_dec = np.exp((i + f(1.0))[None, :] * log_g[:, None]).astype(f)
    k_dec = np.exp((f(chunk) - f(1.0) - i)[None, :] * log_g[:, None]).astype(f)
    c_dec = np.exp(f(chunk) * log_g).astype(f)
    return mask, q_dec, k_dec, c_dec


def kernel(x_prompt, x_sample, state_ret, state_pool, meta_tokens, norm_mix, w_in, ret_gn, pool_w, pool_scale,
           w_ret_branch, w_pool_branch, w_out, norm_ffn, w_router, b_router, w_gate_up, b_gate_up, w_down, b_down,
           norm_final):
    batch, seq, _ = x_prompt.shape
    nb = x_sample.shape[0]
    past_len = 16384
    n_prompt_tok = batch * seq
    n_tok = n_prompt_tok + nb

    w_in_bf = w_in[0].astype(BF16)
    wts = (ret_gn[0][None, :], pool_w[0].astype(BF16), pool_scale[0][None, :],
           w_ret_branch[0].astype(BF16), w_pool_branch[0].astype(BF16), w_out[0].astype(BF16),
           norm_ffn[0][None, :], w_router[0].T.astype(BF16), b_router[0][:, None])
    wgu = w_gate_up[0]
    wd = w_down[0]
    bgu = b_gate_up[0][:, None, :]
    bd = b_down[0][:, None, :]
    nmix = norm_mix[0][None, :]

    cos_p, sin_p = _rotary_tables(N_META + np.arange(seq))
    cos_s, sin_s = _rotary_tables(np.concatenate([np.arange(N_META), np.full((nb,), past_len)]))
    mask, q_dec, k_dec, c_dec = _decay_tables(CHUNK)
    dec = (mask,
           np.ascontiguousarray(np.broadcast_to(q_dec[:, :, None], (RET_HEADS, CHUNK, RET_DK))),
           np.ascontiguousarray(np.broadcast_to(k_dec[:, :, None], (RET_HEADS, CHUNK, RET_DK))),
           np.ascontiguousarray(np.broadcast_to(c_dec[:, None, None], (RET_HEADS, 1, RET_DV))))
    m1, q1, k1, c1 = _decay_tables(1)
    sdec = np.stack([m1[:, 0, 0], q1[:, 0], k1[:, 0], c1], axis=1)

    x2d = x_prompt.reshape(n_prompt_tok, D_MODEL)
    proj_p = _inproj(x2d, nmix, w_in_bf, cos_p, sin_p, INPROJ_TILE, batch, BF16)
    xs2d = x_sample.reshape(nb, D_MODEL)
    x_small = jnp.concatenate([meta_tokens, xs2d], axis=0)
    proj_small = _inproj(x_small, nmix, w_in_bf, cos_s, sin_s, N_META + nb, 1, F32)
    proj_s = tuple(a[N_META:] for a in proj_small)

    lead = CHUNK - N_META
    kmeta = jnp.pad(proj_small[1][:N_META], ((lead, 0), (0, 0))).astype(BF16)
    vmeta = jnp.pad(proj_small[2][:N_META], ((lead, 0), (0, 0))).astype(BF16)
    pmeta = proj_small[4][:N_META]

    grp = SAMPLE_GROUP

    def cols(a):
        return a.T.reshape(RET_QK, nb // grp, grp).transpose(1, 0, 2)

    pool2d = state_pool[0].reshape(nb, POOL_BUF * POOL_WIDTH)
    (st_s, pool_s, h1_s, xn2_s, idx_s, gate_s, rank_s, cnt_s) = _sample_mixer(
        sdec, cols(proj_s[0]), cols(proj_s[1]), proj_s, xs2d, state_ret[0], pool2d, wts)

    (h1_p, xn2_p, idx_p, gate_p, rank_p, cnt, s_fin, p_fin) = _mixer(
        proj_p, x2d, kmeta, vmeta, pmeta, dec, wts, cnt_s, batch, seq)

    assert n_prompt_tok % ROW_TILE == 0 and nb <= ROW_TILE
    n_phantom = ROW_TILE - nb
    n_tiles = n_prompt_tok // ROW_TILE + 1
    i32 = jnp.int32
    counts = cnt[:, 0].astype(i32)
    padded = ((counts + MOE_BLOCK - 1) // MOE_BLOCK) * MOE_BLOCK
    pad_end = jnp.cumsum(padded)
    pad_start = pad_end - padded
    n_blocks = (n_tok * TOP_K) // MOE_BLOCK + N_EXPERTS
    block_row = jnp.arange(n_blocks, dtype=i32) * MOE_BLOCK
    block_e = jnp.minimum(jnp.sum((pad_end[None, :] <= block_row[:, None]).astype(i32), axis=1), N_EXPERTS - 1)
    nblk = (pad_end[-1:] // MOE_BLOCK).astype(i32)
    e_row = jnp.arange(N_EXPERTS, dtype=i32)
    used = padded > 0
    e_ord = jnp.cumsum(used.astype(i32)) - 1
    later_used = jnp.logical_and(e_row[None, :] > e_row[:, None], used[None, :])
    e_next = jnp.min(jnp.where(later_used, e_row[None, :], N_EXPERTS), axis=1)
    e_next = jnp.where(e_next == N_EXPERTS, -1, e_next)
    of_block = block_e[:, None] == e_row[None, :]
    block_eord = jnp.sum(jnp.where(of_block, e_ord[None, :], 0), axis=1)
    block_enext = jnp.sum(jnp.where(of_block, e_next[None, :], 0), axis=1)
    phantom = lambda fill, dt: jnp.full((TOP_K, n_phantom), fill, dt)
    idx = jnp.concatenate([idx_p, idx_s, phantom(-1, i32)], axis=1)
    rank = jnp.concatenate([rank_p, rank_s, phantom(0, i32)], axis=1)
    gates = jnp.concatenate([gate_p, gate_s, phantom(0.0, F32)], axis=1)
    onehot = idx[None] == jnp.arange(N_EXPERTS, dtype=i32)[:, None, None]
    tile_cnt = jnp.sum(onehot.reshape(N_EXPERTS, TOP_K, n_tiles, ROW_TILE).astype(i32), axis=(1, 3)).T
    by_time = jnp.concatenate([tile_cnt[-1:], tile_cnt[:-1]], axis=0)
    before_time = jnp.cumsum(by_time, axis=0) - by_time
    run_before = jnp.concatenate([before_time[1:], before_time[:1]], axis=0)
    tile_off = jnp.cumsum(tile_cnt, axis=1) - tile_cnt
    tile_dst = pad_start[None, :] + run_before
    delta = jnp.repeat((tile_off - run_before).T, ROW_TILE, axis=1)
    pos = rank + jnp.sum(jnp.where(onehot, delta[:, None, :], 0), axis=0)
    pos = jnp.where(idx >= 0, pos, -1)
    tcnt, tdst, ttot = tile_cnt.reshape(-1), tile_dst.reshape(-1), jnp.sum(tile_cnt, axis=1)

    tail = ((0, n_phantom), (0, 0))
    x_sorted = _dispatch(tcnt, tdst, ttot, pad_start + counts, padded - counts, nblk, pos, xn2_p,
                         jnp.pad(xn2_s, tail), n_blocks * MOE_BLOCK)
    y_sorted = _experts(block_e, nblk, block_eord, block_enext, x_sorted, wgu, bgu, wd, bd)
    y_p, y_s = _combine(tcnt, tdst, ttot, pos, gates, y_sorted, h1_p, jnp.pad(h1_s, tail), norm_final[None, :])

    y_prompt = y_p.reshape(batch, seq, D_MODEL)
    y_sample = y_s[:nb].reshape(nb, 1, D_MODEL)
    ret_state_prompt = s_fin[None]
    pool_state_prompt = p_fin[:, 1:, :][None]
    ret_state_sample = st_s[None]
    pool_state_sample = pool_s.reshape(nb, POOL_BUF, POOL_WIDTH)[None]
    return (y_prompt, y_sample, ret_state_prompt, pool_state_prompt, ret_state_sample, pool_state_sample)
```

```python
import functools

import jax
import jax.numpy as jnp
import numpy as np
from jax import lax
from jax.experimental import pallas as pl
from jax.experimental.pallas import tpu as pltpu

F32 = jnp.float32
BF16 = jnp.bfloat16

D_MODEL = 1024
N_META = 16
RET_HEADS = 4
RET_DK = 128
RET_DV = 256
RET_QK = RET_HEADS * RET_DK
RET_V = RET_HEADS * RET_DV
CHUNK = 128
ROPE_BASE = 10000.0
POOL_WINDOWS = (2, 4, 8, 16)
POOL_GROUPS = 4
POOL_GROUP_DIM = 128
POOL_WIDTH = POOL_GROUPS * POOL_GROUP_DIM
POOL_BUF = max(POOL_WINDOWS) - 1
N_EXPERTS = 32
TOP_K = 4
D_FF = D_MODEL
SWIGLU_LIMIT = 7.0
SWIGLU_ALPHA = 1.702
EPS = 1e-6
IN_WIDTHS = (RET_QK, RET_QK, RET_V, RET_V, POOL_WIDTH, D_MODEL, D_MODEL)
IN_TOTAL = sum(IN_WIDTHS)
IN_OFFS = tuple(int(s) for s in np.cumsum((0,) + IN_WIDTHS))

LANES = 128
ROW_CHUNKS = D_MODEL // (2 * LANES)
INPROJ_TILE = 512
MIXER_TILE = 512
MOE_BLOCK = 512
ROW_TILE = 256
SAMPLE_GROUP = 8
WEIGHT_CAST_ROWS = 128
VMEM_LIMIT = 56 * 1024 * 1024

assert N_META + 1 >= max(POOL_WINDOWS)
assert POOL_WINDOWS == (2, 4, 8, 16)


def _dot(a, b):
    return jnp.dot(a, b, preferred_element_type=F32)


def _rmsnorm(x, w):
    return x * lax.rsqrt(jnp.mean(x * x, axis=-1, keepdims=True) + EPS) * w


def _sigmoid(x):
    return 0.5 * jnp.tanh(0.5 * x) + 0.5


def _store_rows_as_tiles(ref, x):
    half = D_MODEL // 2
    hi = pltpu.bitcast(x[:, :half], jnp.uint32) & jnp.uint32(0xFFFF0000)
    lo = pltpu.bitcast(x[:, half:], jnp.uint32) >> 16
    ref[...] = (hi | lo).reshape(x.shape[0], ROW_CHUNKS, LANES)


def _load_rows_from_tiles(ref, rows):
    w = ref[...].reshape(rows, D_MODEL // 2)
    hi = pltpu.bitcast(w & jnp.uint32(0xFFFF0000), F32)
    lo = pltpu.bitcast(w << 16, F32)
    return jnp.concatenate([hi, lo], axis=1)


def _inproj_kernel(x_ref, nw_ref, w_ref, cos_ref, sin_ref,
                   q_ref, k_ref, v_ref, g_ref, p_ref, ga_ref, gb_ref):
    xn = _rmsnorm(x_ref[...], nw_ref[...]).astype(BF16)
    cos = cos_ref[...]
    sin = sin_ref[...]

    def seg(i):
        return _dot(xn, w_ref[:, IN_OFFS[i]:IN_OFFS[i + 1]])

    def rot(a):
        return a * cos + pltpu.roll(a, RET_DK // 2, 1) * sin

    q = seg(0)
    k = seg(1)
    for h in range(RET_HEADS):
        sl = slice(h * RET_DK, (h + 1) * RET_DK)
        q_ref[:, sl] = rot(q[:, sl]).astype(q_ref.dtype)
        k_ref[:, sl] = (rot(k[:, sl]) * (RET_DK ** -0.5)).astype(k_ref.dtype)
    v_ref[...] = seg(2).astype(v_ref.dtype)
    g_ref[...] = seg(3).astype(g_ref.dtype)
    p_ref[...] = seg(4)
    ga_ref[...] = seg(5).astype(ga_ref.dtype)
    gb_ref[...] = seg(6).astype(gb_ref.dtype)


def _sample_state_step(step, sdec_ref, qt_ref, kt_ref, q_ref, k_ref, v_ref, st_ref, stout_ref, o_ref):
    grp = st_ref.shape[0]
    row0 = pl.multiple_of(step * grp, grp)
    q8 = q_ref[pl.ds(row0, grp), :]
    k8 = k_ref[pl.ds(row0, grp), :]
    v8 = v_ref[pl.ds(row0, grp), :]
    for h in range(RET_HEADS):
        ksl = slice(h * RET_DK, (h + 1) * RET_DK)
        vsl = slice(h * RET_DV, (h + 1) * RET_DV)
        score = jnp.sum(q8[:, ksl] * k8[:, ksl], axis=1, keepdims=True) * sdec_ref[h, 0]
        intra = score * v8[:, vsl]
        for bb in range(grp):
            s_old = st_ref[bb, h]
            qcol = qt_ref[0, ksl, bb:bb + 1] * sdec_ref[h, 1]
            kcol = kt_ref[0, ksl, bb:bb + 1] * sdec_ref[h, 2]
            cross = jnp.sum(s_old * qcol, axis=0, keepdims=True)
            o_ref[pl.ds(row0 + bb, 1), vsl] = intra[bb:bb + 1, :] + cross
            stout_ref[bb, h] = s_old * sdec_ref[h, 3] + kcol * v8[bb:bb + 1, vsl]


def _inproj_sample_kernel(x_ref, nw_ref, w_ref, cos_ref, sin_ref,
                          sdec_ref, qt_ref, kt_ref, qs_ref, ks_ref, vs_ref, st_ref,
                          q_ref, k_ref, v_ref, g_ref, p_ref, ga_ref, gb_ref, stout_ref, os_ref):
    _inproj_kernel(x_ref, nw_ref, w_ref, cos_ref, sin_ref, q_ref, k_ref, v_ref, g_ref, p_ref, ga_ref, gb_ref)
    step = pl.program_id(0) * pl.num_programs(1) + pl.program_id(1)
    n_steps = pl.num_programs(0) * pl.num_programs(1)
    period = n_steps // (qs_ref.shape[0] // st_ref.shape[0])

    @pl.when(step % period == 0)
    def _():
        _sample_state_step(step // period, sdec_ref, qt_ref, kt_ref, qs_ref, ks_ref, vs_ref, st_ref, stout_ref,
                           os_ref)


def _inproj(x2d, nw, w_in_bf, cosf, sinf, tile, n_outer, act_dtype, sample=None):
    rows = x2d.shape[0]
    n_inner = rows // (tile * n_outer)
    row_map = lambda b, j: (b * n_inner + j, 0)
    tab_map = lambda b, j: (j, 0)
    const = lambda b, j: (0, 0)
    widths = IN_WIDTHS
    dts = (act_dtype, act_dtype, act_dtype, act_dtype, F32, act_dtype, act_dtype)
    in_arrays = [x2d, nw, w_in_bf, cosf, sinf]
    in_specs = [
        pl.BlockSpec((tile, D_MODEL), row_map),
        pl.BlockSpec((1, D_MODEL), const),
        pl.BlockSpec((D_MODEL, IN_TOTAL), const, pipeline_mode=pl.Buffered(1)),
        pl.BlockSpec((tile, RET_DK), tab_map),
        pl.BlockSpec((tile, RET_DK), tab_map),
    ]
    out_specs = [pl.BlockSpec((tile, w), row_map) for w in widths]
    out_shape = [jax.ShapeDtypeStruct((rows, w), dt) for w, dt in zip(widths, dts)]
    body = _inproj_kernel
    if sample is not None:
        sdec, qt, kt, qs, ks, vs, state = sample
        nb = state.shape[0]
        grp = SAMPLE_GROUP
        period = (n_outer * n_inner) // (nb // grp)
        assert period * (nb // grp) == n_outer * n_inner and qt.shape == (nb // grp, RET_QK, grp)
        step_map = lambda b, j: ((b * n_inner + j) // period, 0, 0)
        st_spec = pl.BlockSpec((grp, RET_HEADS, RET_DK, RET_DV), lambda b, j: ((b * n_inner + j) // period, 0, 0, 0))
        in_arrays += [sdec, qt, kt, qs, ks, vs, state]
        in_specs += [pl.BlockSpec(memory_space=pltpu.SMEM),
                     pl.BlockSpec((1, RET_QK, grp), step_map), pl.BlockSpec((1, RET_QK, grp), step_map),
                     pl.BlockSpec(qs.shape, const), pl.BlockSpec(ks.shape, const), pl.BlockSpec(vs.shape, const),
                     st_spec]
        out_specs += [st_spec, pl.BlockSpec((nb, RET_V), const)]
        out_shape += [jax.ShapeDtypeStruct(state.shape, F32), jax.ShapeDtypeStruct((nb, RET_V), F32)]
        body = _inproj_sample_kernel
    return pl.pallas_call(
        body,
        grid=(n_outer, n_inner),
        in_specs=in_specs,
        out_specs=out_specs,
        out_shape=out_shape,
        compiler_params=pltpu.CompilerParams(
            dimension_semantics=("arbitrary", "arbitrary"), vmem_limit_bytes=VMEM_LIMIT),
        name="inproj",
    )(*in_arrays)


def _group_norm(o, gn_row):
    mu = jnp.mean(o, axis=-1, keepdims=True)
    var = jnp.mean(jnp.square(o - mu), axis=-1, keepdims=True)
    return (o - mu) * lax.rsqrt(var + EPS) * gn_row


def _pool_branch(groups, poolw_ref, pscale_ref, wpool_ref):
    pm = [_dot(g.astype(BF16), poolw_ref[i]) for i, g in enumerate(groups)]
    pm = jnp.concatenate(pm, axis=1) * pscale_ref[...]
    return _dot(pm.astype(BF16), wpool_ref[...])


def _merge_tail(o_norm, g, ga, gb, yb, x, wret_ref, wout_ref):
    gf = g.astype(F32)
    ya = _dot((gf * _sigmoid(gf) * o_norm).astype(BF16), wret_ref[...])
    merged = _sigmoid(ga.astype(F32)) * ya + _sigmoid(gb.astype(F32)) * yb
    return x + _dot(merged.astype(BF16), wout_ref[...])


def _route(h1, nffn_ref, wrt_ref, br_ref, run_scr,
           xn2_ref, idx_ref, gate_ref, rank_ref):
    tm = h1.shape[0]
    xn2 = _rmsnorm(h1, nffn_ref[...]).astype(BF16)
    xn2_ref[...] = xn2
    logits = lax.dot_general(wrt_ref[...], xn2, (((1,), (1,)), ((), ())),
                             preferred_element_type=F32) + br_ref[...]
    e_iota = lax.broadcasted_iota(jnp.int32, (N_EXPERTS, tm), 0)
    work = logits
    vals, sels = [], []
    chosen = jnp.zeros((N_EXPERTS, tm), F32)
    for _ in range(TOP_K):
        m = jnp.max(work, axis=0, keepdims=True)
        sel = jnp.min(jnp.where(work == m, e_iota, N_EXPERTS), axis=0, keepdims=True)
        hit = e_iota == sel
        vals.append(m)
        sels.append(sel)
        chosen = jnp.where(hit, 1.0, chosen)
        work = jnp.where(hit, -jnp.inf, work)
    exps = [jnp.exp(v - vals[0]) for v in vals]
    denom = exps[0] + exps[1] + exps[2] + exps[3]
    gates = [e / denom for e in exps]
    r_i = lax.broadcasted_iota(jnp.int32, (tm, tm), 0)
    c_i = lax.broadcasted_iota(jnp.int32, (tm, tm), 1)
    before = jnp.where(r_i < c_i, 1.0, 0.0).astype(BF16)
    base = run_scr[...] + _dot(chosen.astype(BF16), before)
    for kk in range(TOP_K):
        rk = jnp.sum(jnp.where(e_iota == sels[kk], base, 0.0), axis=0, keepdims=True)
        rank_ref[kk:kk + 1, :] = rk.astype(jnp.int32)
        idx_ref[kk:kk + 1, :] = sels[kk]
        gate_ref[kk:kk + 1, :] = gates[kk]
    run_scr[...] = run_scr[...] + jnp.sum(chosen, axis=1, keepdims=True)


def _mixer_kernel(q_ref, k_ref, v_ref, g_ref, p_ref, ga_ref, gb_ref, x_ref,
                  kmeta_ref, vmeta_ref, pmeta_ref, mask_ref, qdec_ref, kdec_ref, cdec_ref, gn_ref,
                  poolw_ref, pscale_ref, wret_ref, wpool_ref, wout_ref, nffn_ref, wrt_ref, br_ref,
                  cnt0_ref,
                  h1_ref, xn2_ref, idx_ref, gate_ref, rank_ref, cnt_ref, sfin_ref, pfin_ref,
                  s_scr, ext_scr, o_scr, run_scr):
    b = pl.program_id(0)
    j = pl.program_id(1)
    nj = pl.num_programs(1)
    tm = q_ref.shape[0]

    def state_update(s_old, kc, vc, h):
        kd = (kc.astype(F32) * kdec_ref[h]).astype(BF16)
        upd = lax.dot_general(kd, vc, (((0,), (0,)), ((), ())), preferred_element_type=F32)
        return s_old * cdec_ref[h] + upd

    @pl.when(jnp.logical_and(b == 0, j == 0))
    def _():
        run_scr[...] = cnt0_ref[:, 0:1]

    @pl.when(j == 0)
    def _():
        for h in range(RET_HEADS):
            kc = kmeta_ref[:, h * RET_DK:(h + 1) * RET_DK]
            vc = vmeta_ref[:, h * RET_DV:(h + 1) * RET_DV]
            s_scr[h] = state_update(jnp.zeros((RET_DK, RET_DV), F32), kc, vc, h)
        ext_scr[0:N_META, :] = pmeta_ref[...]

    for c in range(tm // CHUNK):
        rows = slice(c * CHUNK, (c + 1) * CHUNK)
        for h in range(RET_HEADS):
            qc = q_ref[rows, h * RET_DK:(h + 1) * RET_DK]
            kc = k_ref[rows, h * RET_DK:(h + 1) * RET_DK]
            vc = v_ref[rows, h * RET_DV:(h + 1) * RET_DV]
            s_old = s_scr[h]
            scores = lax.dot_general(qc, kc, (((1,), (1,)), ((), ())),
                                     preferred_element_type=F32) * mask_ref[h]
            qd = (qc.astype(F32) * qdec_ref[h]).astype(BF16)
            lhs = jnp.concatenate([scores.astype(BF16), qd], axis=1)
            rhs = jnp.concatenate([vc, s_old.astype(BF16)], axis=0)
            o = _dot(lhs, rhs)
            s_scr[h] = state_update(s_old, kc, vc, h)
            o_scr[rows, h * RET_DV:(h + 1) * RET_DV] = _group_norm(
                o, gn_ref[:, h * RET_DV:(h + 1) * RET_DV])

    p = p_ref[...]
    ext_scr[N_META:N_META + tm, :] = p
    a = ext_scr[...]
    g1 = POOL_GROUP_DIM
    s2 = a + pltpu.roll(a, 1, 0)
    s4 = s2[:, g1:] + pltpu.roll(s2[:, g1:], 2, 0)
    s8 = s4[:, g1:] + pltpu.roll(s4[:, g1:], 4, 0)
    s16 = s8[:, g1:] + pltpu.roll(s8[:, g1:], 8, 0)
    sums = (s2[N_META:, :g1], s4[N_META:, :g1], s8[N_META:, :g1], s16[N_META:, :])
    groups = [sums[i] * (1.0 / POOL_WINDOWS[i]) - p[:, i * g1:(i + 1) * g1] for i in range(POOL_GROUPS)]
    ext_scr[0:N_META, :] = ext_scr[tm:tm + N_META, :]

    yb = _pool_branch(groups, poolw_ref, pscale_ref, wpool_ref)
    h1 = _merge_tail(o_scr[...], g_ref[...], ga_ref[...], gb_ref[...], yb, x_ref[...], wret_ref, wout_ref)
    h1_ref[...] = h1
    _route(h1, nffn_ref, wrt_ref, br_ref, run_scr, xn2_ref, idx_ref, gate_ref, rank_ref)
    cnt_ref[...] = jnp.broadcast_to(run_scr[...], cnt_ref.shape)

    @pl.when(j == nj - 1)
    def _():
        for h in range(RET_HEADS):
            sfin_ref[0, h] = s_scr[h]
        pfin_ref[0] = ext_scr[0:N_META, :]


def _mixer(proj, x2d, kmeta, vmeta, pmeta, dec, wts, cnt0, batch, seq):
    q, k, v, g, p, ga, gb = proj
    tm = MIXER_TILE
    nj = seq // tm
    rows = batch * seq
    row_map = lambda b, j: (b * nj + j, 0)
    lane_map = lambda b, j: (0, b * nj + j)
    c2 = lambda b, j: (0, 0)
    c3 = lambda b, j: (0, 0, 0)

    def whole(a):
        return pl.BlockSpec(a.shape, c2 if a.ndim == 2 else c3)

    mask, qdec, kdec, cdec = dec
    gn, poolw, pscale, wret, wpool, wout, nffn, wrt, br = wts
    in_arrays = [q, k, v, g, p, ga, gb, x2d, kmeta, vmeta, pmeta, mask, qdec, kdec, cdec, gn,
                 poolw, pscale, wret, wpool, wout, nffn, wrt, br, cnt0]
    in_specs = [pl.BlockSpec((tm, a.shape[1]), row_map) for a in in_arrays[:8]]
    in_specs += [whole(a) for a in in_arrays[8:]]
    out_shape = [
        jax.ShapeDtypeStruct((rows, D_MODEL), F32),
        jax.ShapeDtypeStruct((rows, D_MODEL), BF16),
        jax.ShapeDtypeStruct((TOP_K, rows), jnp.int32),
        jax.ShapeDtypeStruct((TOP_K, rows), F32),
        jax.ShapeDtypeStruct((TOP_K, rows), jnp.int32),
        jax.ShapeDtypeStruct((N_EXPERTS, LANES), F32),
        jax.ShapeDtypeStruct((batch, RET_HEADS, RET_DK, RET_DV), F32),
        jax.ShapeDtypeStruct((batch, N_META, POOL_WIDTH), F32),
    ]
    out_specs = [
        pl.BlockSpec((tm, D_MODEL), row_map),
        pl.BlockSpec((tm, D_MODEL), row_map),
        pl.BlockSpec((TOP_K, tm), lane_map),
        pl.BlockSpec((TOP_K, tm), lane_map),
        pl.BlockSpec((TOP_K, tm), lane_map),
        pl.BlockSpec((N_EXPERTS, LANES), c2),
        pl.BlockSpec((1, RET_HEADS, RET_DK, RET_DV), lambda b, j: (b, 0, 0, 0)),
        pl.BlockSpec((1, N_META, POOL_WIDTH), lambda b, j: (b, 0, 0)),
    ]
    return pl.pallas_call(
        _mixer_kernel,
        grid=(batch, nj),
        in_specs=in_specs,
        out_specs=out_specs,
        out_shape=out_shape,
        scratch_shapes=[
            pltpu.VMEM((RET_HEADS, RET_DK, RET_DV), F32),
            pltpu.VMEM((N_META + tm, POOL_WIDTH), F32),
            pltpu.VMEM((tm, RET_V), F32),
            pltpu.VMEM((N_EXPERTS, 1), F32),
        ],
        compiler_params=pltpu.CompilerParams(
            dimension_semantics=("arbitrary", "arbitrary"), vmem_limit_bytes=VMEM_LIMIT),
        name="mixer",
    )(*in_arrays)


def _sample_kernel(o_ref, g_ref, p_ref, ga_ref, gb_ref, x_ref, pool_ref, gn_ref,
                   poolw_ref, pscale_ref, wret_ref, wpool_ref, wout_ref, nffn_ref, wrt_ref, br_ref,
                   poolout_ref, h1_ref, xn2_ref, idx_ref, gate_ref, rank_ref, cnt_ref,
                   run_scr):
    run_scr[...] = jnp.zeros_like(run_scr)
    o = o_ref[...]
    o_norm = jnp.concatenate(
        [_group_norm(o[:, h * RET_DV:(h + 1) * RET_DV], gn_ref[:, h * RET_DV:(h + 1) * RET_DV])
         for h in range(RET_HEADS)], axis=1)
    p = p_ref[...]
    w = POOL_WIDTH
    g1 = POOL_GROUP_DIM

    def prev(r, lo):
        return pool_ref[:, r * w + lo:(r + 1) * w]

    s2 = p + prev(14, 0)
    s4 = s2[:, g1:] + prev(13, g1) + prev(12, g1)
    s8 = s4[:, g1:] + prev(11, 2 * g1) + prev(10, 2 * g1) + prev(9, 2 * g1) + prev(8, 2 * g1)
    s16 = s8[:, g1:]
    for r in range(7, -1, -1):
        s16 = s16 + prev(r, 3 * g1)
    sums = (s2[:, :g1], s4[:, :g1], s8[:, :g1], s16)
    groups = [sums[t] * (1.0 / POOL_WINDOWS[t]) - p[:, t * g1:(t + 1) * g1] for t in range(POOL_GROUPS)]
    poolout_ref[:, 0:(POOL_BUF - 1) * w] = pool_ref[:, w:POOL_BUF * w]
    poolout_ref[:, (POOL_BUF - 1) * w:] = p
    yb = _pool_branch(groups, poolw_ref, pscale_ref, wpool_ref)
    h1 = _merge_tail(o_norm, g_ref[...], ga_ref[...], gb_ref[...], yb, x_ref[...], wret_ref, wout_ref)
    h1_ref[...] = h1
    _route(h1, nffn_ref, wrt_ref, br_ref, run_scr, xn2_ref, idx_ref, gate_ref, rank_ref)
    cnt_ref[...] = jnp.broadcast_to(run_scr[...], cnt_ref.shape)


def _sample_mixer(o_s, proj, x2d, pool2d, wts):
    _, _, _, g, p, ga, gb = proj
    nb = x2d.shape[0]
    c2 = lambda i: (0, 0)

    def whole(a):
        return pl.BlockSpec(a.shape, c2)

    gn, poolw, pscale, wret, wpool, wout, nffn, wrt, br = wts
    in_arrays = [o_s, g, p, ga, gb, x2d, pool2d, gn, poolw, pscale, wret, wpool, wout, nffn, wrt, br]
    in_specs = [whole(a) for a in in_arrays[:8]]
    in_specs += [pl.BlockSpec(poolw.shape, lambda i: (0, 0, 0))]
    in_specs += [whole(a) for a in in_arrays[9:]]
    out_shape = [
        jax.ShapeDtypeStruct(pool2d.shape, F32),
        jax.ShapeDtypeStruct((nb, D_MODEL), F32),
        jax.ShapeDtypeStruct((nb, D_MODEL), BF16),
        jax.ShapeDtypeStruct((TOP_K, nb), jnp.int32),
        jax.ShapeDtypeStruct((TOP_K, nb), F32),
        jax.ShapeDtypeStruct((TOP_K, nb), jnp.int32),
        jax.ShapeDtypeStruct((N_EXPERTS, LANES), F32),
    ]
    return pl.pallas_call(
        _sample_kernel,
        grid=(1,),
        in_specs=in_specs,
        out_specs=[pl.BlockSpec(s.shape, c2) for s in out_shape],
        out_shape=out_shape,
        scratch_shapes=[pltpu.VMEM((N_EXPERTS, 1), F32)],
        compiler_params=pltpu.CompilerParams(
            dimension_semantics=("arbitrary",), vmem_limit_bytes=VMEM_LIMIT),
        name="sample_mixer",
    )(*in_arrays)


TILE_ROWS = ROW_TILE * TOP_K


def _run_copy(src_ref, src_row, dst_ref, dst_row, n_rows, sem):
    return pltpu.make_async_copy(src_ref.at[pl.ds(src_row, n_rows)], dst_ref.at[pl.ds(dst_row, n_rows)], sem)


def _for_each_run(tcnt_ref, tile, fn):
    def body(e, off):
        n = tcnt_ref[tile * N_EXPERTS + e]

        @pl.when(n > 0)
        def _():
            fn(e, off, n)

        return off + n

    lax.fori_loop(0, N_EXPERTS, body, 0)


def _dispatch_kernel(tcnt_ref, tdst_ref, ttot_ref, zrow_ref, zcnt_ref, nblk_ref, pos_ref, xp_ref, xs_ref, out_ref,
                     sorted_scr, zero_scr, sem, zsem):
    i = pl.program_id(0)
    n_prompt = pl.num_programs(0) - 1
    n_blocks = out_ref.shape[0] // MOE_BLOCK
    slot = i % 2

    def wait_tile(tile, s):
        _run_copy(out_ref, 0, sorted_scr.at[s], 0, ttot_ref[tile], sem.at[s]).wait()

    def for_each_pad(fn):
        def body(e, carry):
            n = zcnt_ref[e]

            @pl.when(n > 0)
            def _():
                fn(_run_copy(zero_scr, 0, out_ref, zrow_ref[e], n, zsem))

            dead = nblk_ref[0] + e

            @pl.when(dead < n_blocks)
            def _():
                fn(_run_copy(zero_scr, 0, out_ref, dead * MOE_BLOCK, MOE_BLOCK, zsem))

            return carry

        lax.fori_loop(0, N_EXPERTS, body, 0)

    @pl.when(i == 0)
    def _():
        zero_scr[...] = jnp.zeros_like(zero_scr)
        for_each_pad(lambda cp: cp.start())

    def sort_tile(x_ref):
        r_iota = lax.broadcasted_iota(jnp.int32, (TILE_ROWS, ROW_TILE), 0)
        hit = r_iota == pos_ref[0:1, :]
        for kk in range(1, TOP_K):
            hit = jnp.logical_or(hit, r_iota == pos_ref[kk:kk + 1, :])
        perm = jnp.where(hit, 1.0, 0.0).astype(BF16)
        xs = _dot(perm, x_ref[...])

        @pl.when(i >= 2)
        def _():
            wait_tile(i - 2, slot)

        _store_rows_as_tiles(sorted_scr.at[slot], xs)

    @pl.when(i < n_prompt)
    def _():
        sort_tile(xp_ref)

    @pl.when(i == n_prompt)
    def _():
        sort_tile(xs_ref)

    _for_each_run(tcnt_ref, i, lambda e, off, n: _run_copy(
        sorted_scr.at[slot], off, out_ref, tdst_ref[i * N_EXPERTS + e], n, sem.at[slot]).start())

    @pl.when(i == n_prompt)
    def _():
        wait_tile(i - 1, 1 - slot)
        wait_tile(i, slot)
        for_each_pad(lambda cp: cp.wait())


def _dispatch(tcnt, tdst, ttot, zrow, zcnt, nblk, pos, xn2_p, xn2_s, n_sorted):
    n_prompt = xn2_p.shape[0] // ROW_TILE
    grid_spec = pltpu.PrefetchScalarGridSpec(
        num_scalar_prefetch=6,
        grid=(n_prompt + 1,),
        in_specs=[
            pl.BlockSpec((TOP_K, ROW_TILE), lambda i, *_: (0, i)),
            pl.BlockSpec((ROW_TILE, D_MODEL), lambda i, *_: (jnp.minimum(i, n_prompt - 1), 0)),
            pl.BlockSpec((ROW_TILE, D_MODEL), lambda i, *_: (0, 0)),
        ],
        out_specs=pl.BlockSpec(memory_space=pl.ANY),
        scratch_shapes=[
            pltpu.VMEM((2, TILE_ROWS, ROW_CHUNKS, LANES), jnp.uint32),
            pltpu.VMEM((MOE_BLOCK, ROW_CHUNKS, LANES), jnp.uint32),
            pltpu.SemaphoreType.DMA((2,)), pltpu.SemaphoreType.DMA(()),
        ],
    )
    return pl.pallas_call(
        _dispatch_kernel,
        grid_spec=grid_spec,
        out_shape=jax.ShapeDtypeStruct((n_sorted, ROW_CHUNKS, LANES), jnp.uint32),
        compiler_params=pltpu.CompilerParams(dimension_semantics=("arbitrary",)),
        name="moe_dispatch",
    )(tcnt, tdst, ttot, zrow, zcnt, nblk, pos, xn2_p, xn2_s)


def _expert_kernel(be_ref, nblk_ref, eord_ref, enext_ref, x_ref, wgu_ref, bgu_ref, wd_ref, bd_ref, y_ref,
                   wgu_f32, wd_f32, wgu_bf, wd_bf, wsem):
    i = pl.program_id(0)
    live = i < nblk_ref[0]
    new_expert = jnp.logical_or(i == 0, be_ref[i] != be_ref[jnp.maximum(i - 1, 0)])

    def weight_copies(e, s):
        return (pltpu.make_async_copy(wgu_ref.at[e], wgu_f32.at[s], wsem.at[0, s]),
                pltpu.make_async_copy(wd_ref.at[e], wd_f32.at[s], wsem.at[1, s]))

    @pl.when(jnp.logical_and(live, new_expert))
    def _():
        slot = eord_ref[i] % 2

        @pl.when(i == 0)
        def _():
            for cp in weight_copies(be_ref[0], 0):
                cp.start(priority=1)

        for cp in weight_copies(be_ref[i], slot):
            cp.wait()

        @pl.when(enext_ref[i] >= 0)
        def _():
            for cp in weight_copies(enext_ref[i], 1 - slot):
                cp.start(priority=1)

        def cast(c, carry):
            rows = pl.ds(pl.multiple_of(c * WEIGHT_CAST_ROWS, WEIGHT_CAST_ROWS), WEIGHT_CAST_ROWS)
            wgu_bf[rows, :] = wgu_f32[slot, rows, :].astype(BF16)
            wd_bf[rows, :] = wd_f32[slot, rows, :].astype(BF16)
            return carry

        lax.fori_loop(0, D_MODEL // WEIGHT_CAST_ROWS, cast, 0)

    @pl.when(jnp.logical_not(live))
    def _():
        y_ref[...] = jnp.zeros_like(y_ref)

    @pl.when(live)
    def _():
        x = _load_rows_from_tiles(x_ref, MOE_BLOCK)
        h = _dot(x.astype(BF16), wgu_bf[...]) + bgu_ref[0]
        gate = jnp.minimum(h[:, :D_FF], SWIGLU_LIMIT)
        up = jnp.clip(h[:, D_FF:], -SWIGLU_LIMIT, SWIGLU_LIMIT)
        glu = gate * _sigmoid(gate * SWIGLU_ALPHA)
        y = _dot(((up + 1.0) * glu).astype(BF16), wd_bf[...]) + bd_ref[0]
        _store_rows_as_tiles(y_ref, y.astype(BF16).astype(F32))


def _experts(block_e, nblk, block_eord, block_enext, x_sorted, wgu, bgu, wd, bd):
    n_blocks = x_sorted.shape[0] // MOE_BLOCK
    wmap = lambda i, be, *_: (be[i], 0, 0)
    rmap = lambda i, *_: (i, 0, 0)
    hbm = pl.BlockSpec(memory_space=pl.ANY)
    grid_spec = pltpu.PrefetchScalarGridSpec(
        num_scalar_prefetch=4,
        grid=(n_blocks,),
        in_specs=[
            pl.BlockSpec((MOE_BLOCK, ROW_CHUNKS, LANES), rmap),
            hbm,
            pl.BlockSpec((1, 1, 2 * D_FF), wmap),
            hbm,
            pl.BlockSpec((1, 1, D_MODEL), wmap),
        ],
        out_specs=pl.BlockSpec((MOE_BLOCK, ROW_CHUNKS, LANES), rmap),
        scratch_shapes=[
            pltpu.VMEM((2, D_MODEL, 2 * D_FF), F32), pltpu.VMEM((2, D_FF, D_MODEL), F32),
            pltpu.VMEM((D_MODEL, 2 * D_FF), BF16), pltpu.VMEM((D_FF, D_MODEL), BF16),
            pltpu.SemaphoreType.DMA((2, 2)),
        ],
    )
    return pl.pallas_call(
        _expert_kernel,
        grid_spec=grid_spec,
        out_shape=jax.ShapeDtypeStruct(x_sorted.shape, jnp.uint32),
        compiler_params=pltpu.CompilerParams(
            dimension_semantics=("arbitrary",), vmem_limit_bytes=VMEM_LIMIT),
        name="moe_experts",
    )(block_e, nblk, block_eord, block_enext, x_sorted, wgu, bgu, wd, bd)


def _combine_kernel(tcnt_ref, tdst_ref, ttot_ref, pos_ref, gate_ref, ys_ref, hp_ref, hs_ref, nf_ref, yp_ref, ysmp_ref,
                    runs_scr, sem):
    i = pl.program_id(0)
    n_tiles = pl.num_programs(0)
    n_prompt = n_tiles - 1
    slot = i % 2

    def start_runs(tile, s):
        _for_each_run(tcnt_ref, tile, lambda e, off, n: _run_copy(
            ys_ref, tdst_ref[tile * N_EXPERTS + e], runs_scr.at[s], off, n, sem.at[s]).start())

    @pl.when(i == 0)
    def _():
        start_runs(0, 0)

    @pl.when(i + 1 < n_tiles)
    def _():
        start_runs(i + 1, 1 - slot)

    _run_copy(ys_ref, 0, runs_scr.at[slot], 0, ttot_ref[i], sem.at[slot]).wait()
    ys = _load_rows_from_tiles(runs_scr.at[slot], TILE_ROWS).astype(BF16)
    r_iota = lax.broadcasted_iota(jnp.int32, (TILE_ROWS, ROW_TILE), 0)
    gmat = jnp.zeros((TILE_ROWS, ROW_TILE), F32)
    for kk in range(TOP_K):
        gmat = jnp.where(r_iota == pos_ref[kk:kk + 1, :], gate_ref[kk:kk + 1, :], gmat)
    moe = lax.dot_general(gmat.astype(BF16), ys, (((0,), (0,)), ((), ())), preferred_element_type=F32)

    @pl.when(i < n_prompt)
    def _():
        yp_ref[...] = _rmsnorm(hp_ref[...] + moe, nf_ref[...])

    @pl.when(i == n_prompt)
    def _():
        ysmp_ref[...] = _rmsnorm(hs_ref[...] + moe, nf_ref[...])


def _combine(tcnt, tdst, ttot, pos, gates, y_sorted, h1_p, h1_s, nf):
    n_prompt = h1_p.shape[0] // ROW_TILE
    pmap = lambda i, *_: (jnp.minimum(i, n_prompt - 1), 0)
    smap = lambda i, *_: (0, 0)
    lmap = lambda i, *_: (0, i)
    grid_spec = pltpu.PrefetchScalarGridSpec(
        num_scalar_prefetch=3,
        grid=(n_prompt + 1,),
        in_specs=[
            pl.BlockSpec((TOP_K, ROW_TILE), lmap),
            pl.BlockSpec((TOP_K, ROW_TILE), lmap),
            pl.BlockSpec(memory_space=pl.ANY),
            pl.BlockSpec((ROW_TILE, D_MODEL), pmap),
            pl.BlockSpec((ROW_TILE, D_MODEL), smap),
            pl.BlockSpec((1, D_MODEL), smap),
        ],
        out_specs=[pl.BlockSpec((ROW_TILE, D_MODEL), pmap), pl.BlockSpec((ROW_TILE, D_MODEL), smap)],
        scratch_shapes=[pltpu.VMEM((2, TILE_ROWS, ROW_CHUNKS, LANES), jnp.uint32), pltpu.SemaphoreType.DMA((2,))],
    )
    return pl.pallas_call(
        _combine_kernel,
        grid_spec=grid_spec,
        out_shape=[jax.ShapeDtypeStruct(h1_p.shape, F32), jax.ShapeDtypeStruct(h1_s.shape, F32)],
        compiler_params=pltpu.CompilerParams(dimension_semantics=("arbitrary",)),
        name="moe_combine",
    )(tcnt, tdst, ttot, pos, gates, y_sorted, h1_p, h1_s, nf)


def _rotary_tables(pos):
    f = np.float32
    inv = np.power(f(ROPE_BASE), -np.arange(0, RET_DK, 2, dtype=f) / f(RET_DK)).astype(f)
    ang = (np.asarray(pos, f)[:, None] * inv[None, :]).astype(f)
    cos, sin = np.cos(ang).astype(f), np.sin(ang).astype(f)
    return np.concatenate([cos, cos], axis=1), np.concatenate([-sin, sin], axis=1)


def _decay_tables(chunk):
    f = np.float32
    log_g = np.log1p(-np.exp2(f(-5.0) - np.arange(RET_HEADS, dtype=f))).astype(f)
    i = np.arange(chunk, dtype=f)
    diff = i[:, None] - i[None, :]
    mask = np.where(diff[None] >= 0, np.exp(np.maximum(diff, f(0.0))[None] * log_g[:, None, None]), f(0.0)).astype(f)
    q_dec = np.exp((i + f(1.0))[None, :] * log_g[:, None]).astype(f)
    k_dec = np.exp((f(chunk) - f(1.0) - i)[None, :] * log_g[:, None]).astype(f)
    c_dec = np.exp(f(chunk) * log_g).astype(f)
    return mask, q_dec, k_dec, c_dec


def kernel(x_prompt, x_sample, state_ret, state_pool, meta_tokens, norm_mix, w_in, ret_gn, pool_w, pool_scale,
           w_ret_branch, w_pool_branch, w_out, norm_ffn, w_router, b_router, w_gate_up, b_gate_up, w_down, b_down,
           norm_final):
    batch, seq, _ = x_prompt.shape
    nb = x_sample.shape[0]
    past_len = 16384
    n_prompt_tok = batch * seq
    n_tok = n_prompt_tok + nb

    w_in_bf = w_in[0].astype(BF16)
    wts = (ret_gn[0][None, :], pool_w[0].astype(BF16), pool_scale[0][None, :],
           w_ret_branch[0].astype(BF16), w_pool_branch[0].astype(BF16), w_out[0].astype(BF16),
           norm_ffn[0][None, :], w_router[0].T.astype(BF16), b_router[0][:, None])
    wgu = w_gate_up[0]
    wd = w_down[0]
    bgu = b_gate_up[0][:, None, :]
    bd = b_down[0][:, None, :]
    nmix = norm_mix[0][None, :]

    cos_p, sin_p = _rotary_tables(N_META + np.arange(seq))
    cos_s, sin_s = _rotary_tables(np.concatenate([np.arange(N_META), np.full((nb,), past_len)]))
    mask, q_dec, k_dec, c_dec = _decay_tables(CHUNK)
    dec = (mask,
           np.ascontiguousarray(np.broadcast_to(q_dec[:, :, None], (RET_HEADS, CHUNK, RET_DK))),
           np.ascontiguousarray(np.broadcast_to(k_dec[:, :, None], (RET_HEADS, CHUNK, RET_DK))),
           np.ascontiguousarray(np.broadcast_to(c_dec[:, None, None], (RET_HEADS, 1, RET_DV))))
    m1, q1, k1, c1 = _decay_tables(1)
    sdec = np.stack([m1[:, 0, 0], q1[:, 0], k1[:, 0], c1], axis=1)

    x2d = x_prompt.reshape(n_prompt_tok, D_MODEL)
    xs2d = x_sample.reshape(nb, D_MODEL)
    x_small = jnp.concatenate([meta_tokens, xs2d], axis=0)
    proj_small = _inproj(x_small, nmix, w_in_bf, cos_s, sin_s, N_META + nb, 1, F32)
    proj_s = tuple(a[N_META:] for a in proj_small)

    grp = SAMPLE_GROUP

    def cols(a):
        return a.T.reshape(RET_QK, nb // grp, grp).transpose(1, 0, 2)

    *proj_p, st_s, o_s = _inproj(
        x2d, nmix, w_in_bf, cos_p, sin_p, INPROJ_TILE, batch, BF16,
        sample=(sdec, cols(proj_s[0]), cols(proj_s[1]), proj_s[0], proj_s[1], proj_s[2], state_ret[0]))

    lead = CHUNK - N_META
    kmeta = jnp.pad(proj_small[1][:N_META], ((lead, 0), (0, 0))).astype(BF16)
    vmeta = jnp.pad(proj_small[2][:N_META], ((lead, 0), (0, 0))).astype(BF16)
    pmeta = proj_small[4][:N_META]

    pool2d = state_pool[0].reshape(nb, POOL_BUF * POOL_WIDTH)
    (pool_s, h1_s, xn2_s, idx_s, gate_s, rank_s, cnt_s) = _sample_mixer(o_s, proj_s, xs2d, pool2d, wts)

    (h1_p, xn2_p, idx_p, gate_p, rank_p, cnt, s_fin, p_fin) = _mixer(
        proj_p, x2d, kmeta, vmeta, pmeta, dec, wts, cnt_s, batch, seq)

    assert n_prompt_tok % ROW_TILE == 0 and nb <= ROW_TILE
    n_phantom = ROW_TILE - nb
    n_tiles = n_prompt_tok // ROW_TILE + 1
    i32 = jnp.int32
    counts = cnt[:, 0].astype(i32)
    padded = ((counts + MOE_BLOCK - 1) // MOE_BLOCK) * MOE_BLOCK
    pad_end = jnp.cumsum(padded)
    pad_start = pad_end - padded
    n_blocks = (n_tok * TOP_K) // MOE_BLOCK + N_EXPERTS
    block_row = jnp.arange(n_blocks, dtype=i32) * MOE_BLOCK
    block_e = jnp.minimum(jnp.sum((pad_end[None, :] <= block_row[:, None]).astype(i32), axis=1), N_EXPERTS - 1)
    nblk = (pad_end[-1:] // MOE_BLOCK).astype(i32)
    e_row = jnp.arange(N_EXPERTS, dtype=i32)
    used = padded > 0
    e_ord = jnp.cumsum(used.astype(i32)) - 1
    later_used = jnp.logical_and(e_row[None, :] > e_row[:, None], used[None, :])
    e_next = jnp.min(jnp.where(later_used, e_row[None, :], N_EXPERTS), axis=1)
    e_next = jnp.where(e_next == N_EXPERTS, -1, e_next)
    of_block = block_e[:, None] == e_row[None, :]
    block_eord = jnp.sum(jnp.where(of_block, e_ord[None, :], 0), axis=1)
    block_enext = jnp.sum(jnp.where(of_block, e_next[None, :], 0), axis=1)
    phantom = lambda fill, dt: jnp.full((TOP_K, n_phantom), fill, dt)
    idx = jnp.concatenate([idx_p, idx_s, phantom(-1, i32)], axis=1)
    rank = jnp.concatenate([rank_p, rank_s, phantom(0, i32)], axis=1)
    gates = jnp.concatenate([gate_p, gate_s, phantom(0.0, F32)], axis=1)
    onehot = idx[None] == jnp.arange(N_EXPERTS, dtype=i32)[:, None, None]
    tile_cnt = jnp.sum(onehot.reshape(N_EXPERTS, TOP_K, n_tiles, ROW_TILE).astype(i32), axis=(1, 3)).T
    by_time = jnp.concatenate([tile_cnt[-1:], tile_cnt[:-1]], axis=0)
    before_time = jnp.cumsum(by_time, axis=0) - by_time
    run_before = jnp.concatenate([before_time[1:], before_time[:1]], axis=0)
    tile_off = jnp.cumsum(tile_cnt, axis=1) - tile_cnt
    tile_dst = pad_start[None, :] + run_before
    delta = jnp.repeat((tile_off - run_before).T, ROW_TILE, axis=1)
    pos = rank + jnp.sum(jnp.where(onehot, delta[:, None, :], 0), axis=0)
    pos = jnp.where(idx >= 0, pos, -1)
    tcnt, tdst, ttot = tile_cnt.reshape(-1), tile_dst.reshape(-1), jnp.sum(tile_cnt, axis=1)

    tail = ((0, n_phantom), (0, 0))
    x_sorted = _dispatch(tcnt, tdst, ttot, pad_start + counts, padded - counts, nblk, pos, xn2_p,
                         jnp.pad(xn2_s, tail), n_blocks * MOE_BLOCK)
    y_sorted = _experts(block_e, nblk, block_eord, block_enext, x_sorted, wgu, bgu, wd, bd)
    y_p, y_s = _combine(tcnt, tdst, ttot, pos, gates, y_sorted, h1_p, jnp.pad(h1_s, tail), norm_final[None, :])

    y_prompt = y_p.reshape(batch, seq, D_MODEL)
    y_sample = y_s[:nb].reshape(nb, 1, D_MODEL)
    ret_state_prompt = s_fin[None]
    pool_state_prompt = p_fin[:, 1:, :][None]
    ret_state_sample = st_s[None]
    pool_state_sample = pool_s.reshape(nb, POOL_BUF, POOL_WIDTH)[None]
    return (y_prompt, y_sample, ret_state_prompt, pool_state_prompt, ret_state_sample, pool_state_sample)
```

```python
import functools

import jax
import jax.numpy as jnp
import numpy as np
from jax import lax
from jax.experimental import pallas as pl
from jax.experimental.pallas import tpu as pltpu

F32 = jnp.float32
BF16 = jnp.bfloat16

D_MODEL = 1024
N_META = 16
RET_HEADS = 4
RET_DK = 128
RET_DV = 256
RET_QK = RET_HEADS * RET_DK
RET_V = RET_HEADS * RET_DV
CHUNK = 128
ROPE_BASE = 10000.0
POOL_WINDOWS = (2, 4, 8, 16)
POOL_GROUPS = 4
POOL_GROUP_DIM = 128
POOL_WIDTH = POOL_GROUPS * POOL_GROUP_DIM
POOL_BUF = max(POOL_WINDOWS) - 1
N_EXPERTS = 32
TOP_K = 4
D_FF = D_MODEL
SWIGLU_LIMIT = 7.0
SWIGLU_ALPHA = 1.702
EPS = 1e-6
IN_WIDTHS = (RET_QK, RET_QK, RET_V, RET_V, POOL_WIDTH, D_MODEL, D_MODEL)
IN_TOTAL = sum(IN_WIDTHS)
IN_OFFS = tuple(int(s) for s in np.cumsum((0,) + IN_WIDTHS))

LANES = 128
ROW_CHUNKS = D_MODEL // (2 * LANES)
INPROJ_TILE = 512
MIXER_TILE = 512
MOE_BLOCK = 512
ROW_TILE = 256
SUBLANES = 8
WEIGHT_CAST_ROWS = 128
VMEM_LIMIT = 56 * 1024 * 1024

assert N_META + 1 >= max(POOL_WINDOWS)
assert POOL_WINDOWS == (2, 4, 8, 16)


def _dot(a, b):
    return jnp.dot(a, b, preferred_element_type=F32)


def _rmsnorm(x, w):
    return x * lax.rsqrt(jnp.mean(x * x, axis=-1, keepdims=True) + EPS) * w


def _sigmoid(x):
    return 0.5 * jnp.tanh(0.5 * x) + 0.5


def _store_rows_as_tiles(ref, x):
    half = D_MODEL // 2
    hi = pltpu.bitcast(x[:, :half], jnp.uint32) & jnp.uint32(0xFFFF0000)
    lo = pltpu.bitcast(x[:, half:], jnp.uint32) >> 16
    ref[...] = (hi | lo).reshape(x.shape[0], ROW_CHUNKS, LANES)


def _load_rows_from_tiles(ref, rows):
    w = ref[...].reshape(rows, D_MODEL // 2)
    hi = pltpu.bitcast(w & jnp.uint32(0xFFFF0000), F32)
    lo = pltpu.bitcast(w << 16, F32)
    return jnp.concatenate([hi, lo], axis=1)


def _inproj_kernel(x_ref, nw_ref, w_ref, cos_ref, sin_ref,
                   q_ref, k_ref, v_ref, g_ref, p_ref, ga_ref, gb_ref):
    xn = _rmsnorm(x_ref[...], nw_ref[...]).astype(BF16)
    cos = cos_ref[...]
    sin = sin_ref[...]

    def seg(i):
        return _dot(xn, w_ref[:, IN_OFFS[i]:IN_OFFS[i + 1]])

    def rot(a):
        return a * cos + pltpu.roll(a, RET_DK // 2, 1) * sin

    q = seg(0)
    k = seg(1)
    for h in range(RET_HEADS):
        sl = slice(h * RET_DK, (h + 1) * RET_DK)
        q_ref[:, sl] = rot(q[:, sl]).astype(q_ref.dtype)
        k_ref[:, sl] = (rot(k[:, sl]) * (RET_DK ** -0.5)).astype(k_ref.dtype)
    v_ref[...] = seg(2).astype(v_ref.dtype)
    g_ref[...] = seg(3).astype(g_ref.dtype)
    p_ref[...] = seg(4)
    ga_ref[...] = seg(5).astype(ga_ref.dtype)
    gb_ref[...] = seg(6).astype(gb_ref.dtype)


def _sample_state_step(step, sdec_ref, qt_ref, kt_ref, q_ref, k_ref, v_ref, st_ref, stout_ref, o_ref):
    grp = st_ref.shape[0]
    row0 = step * grp
    if grp % SUBLANES == 0:
        rows = pl.ds(pl.multiple_of(row0, SUBLANES), grp)
        q8, k8, v8 = q_ref[rows, :], k_ref[rows, :], v_ref[rows, :]
    else:
        assert 2 * grp == SUBLANES
        rows = pl.ds(pl.multiple_of((step // 2) * SUBLANES, SUBLANES), SUBLANES)
        first = step % 2 == 0
        q8, k8, v8 = (jnp.where(first, a[:grp], a[grp:]) for a in (q_ref[rows, :], k_ref[rows, :], v_ref[rows, :]))
    for h in range(RET_HEADS):
        ksl = slice(h * RET_DK, (h + 1) * RET_DK)
        vsl = slice(h * RET_DV, (h + 1) * RET_DV)
        score = jnp.sum(q8[:, ksl] * k8[:, ksl], axis=1, keepdims=True) * sdec_ref[h, 0]
        intra = score * v8[:, vsl]
        for bb in range(grp):
            s_old = st_ref[bb, h]
            qcol = qt_ref[0, ksl, bb:bb + 1] * sdec_ref[h, 1]
            kcol = kt_ref[0, ksl, bb:bb + 1] * sdec_ref[h, 2]
            cross = jnp.sum(s_old * qcol, axis=0, keepdims=True)
            o_ref[pl.ds(row0 + bb, 1), vsl] = intra[bb:bb + 1, :] + cross
            stout_ref[bb, h] = s_old * sdec_ref[h, 3] + kcol * v8[bb:bb + 1, vsl]


def _inproj_sample_kernel(x_ref, nw_ref, w_ref, cos_ref, sin_ref,
                          sdec_ref, qt_ref, kt_ref, qs_ref, ks_ref, vs_ref, st_ref,
                          q_ref, k_ref, v_ref, g_ref, p_ref, ga_ref, gb_ref, stout_ref, os_ref):
    _inproj_kernel(x_ref, nw_ref, w_ref, cos_ref, sin_ref, q_ref, k_ref, v_ref, g_ref, p_ref, ga_ref, gb_ref)
    step = pl.program_id(0) * pl.num_programs(1) + pl.program_id(1)
    _sample_state_step(step, sdec_ref, qt_ref, kt_ref, qs_ref, ks_ref, vs_ref, st_ref, stout_ref, os_ref)


def _inproj(x2d, nw, w_in_bf, cosf, sinf, tile, n_outer, act_dtype, sample=None):
    rows = x2d.shape[0]
    n_inner = rows // (tile * n_outer)
    row_map = lambda b, j: (b * n_inner + j, 0)
    tab_map = lambda b, j: (j, 0)
    const = lambda b, j: (0, 0)
    widths = IN_WIDTHS
    dts = (act_dtype, act_dtype, act_dtype, act_dtype, F32, act_dtype, act_dtype)
    in_arrays = [x2d, nw, w_in_bf, cosf, sinf]
    in_specs = [
        pl.BlockSpec((tile, D_MODEL), row_map),
        pl.BlockSpec((1, D_MODEL), const),
        pl.BlockSpec((D_MODEL, IN_TOTAL), const, pipeline_mode=pl.Buffered(1)),
        pl.BlockSpec((tile, RET_DK), tab_map),
        pl.BlockSpec((tile, RET_DK), tab_map),
    ]
    out_specs = [pl.BlockSpec((tile, w), row_map) for w in widths]
    out_shape = [jax.ShapeDtypeStruct((rows, w), dt) for w, dt in zip(widths, dts)]
    body = _inproj_kernel
    if sample is not None:
        sdec, qt, kt, qs, ks, vs, state = sample
        nb = state.shape[0]
        grp = nb // (n_outer * n_inner)
        assert grp * n_outer * n_inner == nb and qt.shape == (nb // grp, RET_QK, grp)
        step_map = lambda b, j: (b * n_inner + j, 0, 0)
        st_spec = pl.BlockSpec((grp, RET_HEADS, RET_DK, RET_DV), lambda b, j: (b * n_inner + j, 0, 0, 0))
        in_arrays += [sdec, qt, kt, qs, ks, vs, state]
        in_specs += [pl.BlockSpec(memory_space=pltpu.SMEM),
                     pl.BlockSpec((1, RET_QK, grp), step_map), pl.BlockSpec((1, RET_QK, grp), step_map),
                     pl.BlockSpec(qs.shape, const), pl.BlockSpec(ks.shape, const), pl.BlockSpec(vs.shape, const),
                     st_spec]
        out_specs += [st_spec, pl.BlockSpec((nb, RET_V), const)]
        out_shape += [jax.ShapeDtypeStruct(state.shape, F32), jax.ShapeDtypeStruct((nb, RET_V), F32)]
        body = _inproj_sample_kernel
    return pl.pallas_call(
        body,
        grid=(n_outer, n_inner),
        in_specs=in_specs,
        out_specs=out_specs,
        out_shape=out_shape,
        compiler_params=pltpu.CompilerParams(
            dimension_semantics=("arbitrary", "arbitrary"), vmem_limit_bytes=VMEM_LIMIT),
        name="inproj",
    )(*in_arrays)


def _group_norm(o, gn_row):
    mu = jnp.mean(o, axis=-1, keepdims=True)
    var = jnp.mean(jnp.square(o - mu), axis=-1, keepdims=True)
    return (o - mu) * lax.rsqrt(var + EPS) * gn_row


def _pool_branch(groups, poolw_ref, pscale_ref, wpool_ref):
    pm = [_dot(g.astype(BF16), poolw_ref[i]) for i, g in enumerate(groups)]
    pm = jnp.concatenate(pm, axis=1) * pscale_ref[...]
    return _dot(pm.astype(BF16), wpool_ref[...])


def _merge_tail(o_norm, g, ga, gb, yb, x, wret_ref, wout_ref):
    gf = g.astype(F32)
    ya = _dot((gf * _sigmoid(gf) * o_norm).astype(BF16), wret_ref[...])
    merged = _sigmoid(ga.astype(F32)) * ya + _sigmoid(gb.astype(F32)) * yb
    return x + _dot(merged.astype(BF16), wout_ref[...])


def _route(h1, nffn_ref, wrt_ref, br_ref, run_scr,
           xn2_ref, idx_ref, gate_ref, rank_ref):
    tm = h1.shape[0]
    xn2 = _rmsnorm(h1, nffn_ref[...]).astype(BF16)
    xn2_ref[...] = xn2
    logits = lax.dot_general(wrt_ref[...], xn2, (((1,), (1,)), ((), ())),
                             preferred_element_type=F32) + br_ref[...]
    e_iota = lax.broadcasted_iota(jnp.int32, (N_EXPERTS, tm), 0)
    work = logits
    vals, sels = [], []
    chosen = jnp.zeros((N_EXPERTS, tm), F32)
    for _ in range(TOP_K):
        m = jnp.max(work, axis=0, keepdims=True)
        sel = jnp.min(jnp.where(work == m, e_iota, N_EXPERTS), axis=0, keepdims=True)
        hit = e_iota == sel
        vals.append(m)
        sels.append(sel)
        chosen = jnp.where(hit, 1.0, chosen)
        work = jnp.where(hit, -jnp.inf, work)
    exps = [jnp.exp(v - vals[0]) for v in vals]
    denom = exps[0] + exps[1] + exps[2] + exps[3]
    gates = [e / denom for e in exps]
    r_i = lax.broadcasted_iota(jnp.int32, (tm, tm), 0)
    c_i = lax.broadcasted_iota(jnp.int32, (tm, tm), 1)
    before = jnp.where(r_i < c_i, 1.0, 0.0).astype(BF16)
    base = run_scr[...] + _dot(chosen.astype(BF16), before)
    for kk in range(TOP_K):
        rk = jnp.sum(jnp.where(e_iota == sels[kk], base, 0.0), axis=0, keepdims=True)
        rank_ref[kk:kk + 1, :] = rk.astype(jnp.int32)
        idx_ref[kk:kk + 1, :] = sels[kk]
        gate_ref[kk:kk + 1, :] = gates[kk]
    run_scr[...] = run_scr[...] + jnp.sum(chosen, axis=1, keepdims=True)


def _mixer_kernel(q_ref, k_ref, v_ref, g_ref, p_ref, ga_ref, gb_ref, x_ref,
                  kmeta_ref, vmeta_ref, pmeta_ref, mask_ref, qdec_ref, kdec_ref, cdec_ref, gn_ref,
                  poolw_ref, pscale_ref, wret_ref, wpool_ref, wout_ref, nffn_ref, wrt_ref, br_ref,
                  cnt0_ref,
                  h1_ref, xn2_ref, idx_ref, gate_ref, rank_ref, cnt_ref, sfin_ref, pfin_ref,
                  s_scr, ext_scr, o_scr, run_scr):
    b = pl.program_id(0)
    j = pl.program_id(1)
    nj = pl.num_programs(1)
    tm = q_ref.shape[0]

    def state_update(s_old, kc, vc, h):
        kd = (kc.astype(F32) * kdec_ref[h]).astype(BF16)
        upd = lax.dot_general(kd, vc, (((0,), (0,)), ((), ())), preferred_element_type=F32)
        return s_old * cdec_ref[h] + upd

    @pl.when(jnp.logical_and(b == 0, j == 0))
    def _():
        run_scr[...] = cnt0_ref[:, 0:1]

    @pl.when(j == 0)
    def _():
        for h in range(RET_HEADS):
            kc = kmeta_ref[:, h * RET_DK:(h + 1) * RET_DK]
            vc = vmeta_ref[:, h * RET_DV:(h + 1) * RET_DV]
            s_scr[h] = state_update(jnp.zeros((RET_DK, RET_DV), F32), kc, vc, h)
        ext_scr[0:N_META, :] = pmeta_ref[...]

    for c in range(tm // CHUNK):
        rows = slice(c * CHUNK, (c + 1) * CHUNK)
        for h in range(RET_HEADS):
            qc = q_ref[rows, h * RET_DK:(h + 1) * RET_DK]
            kc = k_ref[rows, h * RET_DK:(h + 1) * RET_DK]
            vc = v_ref[rows, h * RET_DV:(h + 1) * RET_DV]
            s_old = s_scr[h]
            scores = lax.dot_general(qc, kc, (((1,), (1,)), ((), ())),
                                     preferred_element_type=F32) * mask_ref[h]
            qd = (qc.astype(F32) * qdec_ref[h]).astype(BF16)
            lhs = jnp.concatenate([scores.astype(BF16), qd], axis=1)
            rhs = jnp.concatenate([vc, s_old.astype(BF16)], axis=0)
            o = _dot(lhs, rhs)
            s_scr[h] = state_update(s_old, kc, vc, h)
            o_scr[rows, h * RET_DV:(h + 1) * RET_DV] = _group_norm(
                o, gn_ref[:, h * RET_DV:(h + 1) * RET_DV])

    p = p_ref[...]
    ext_scr[N_META:N_META + tm, :] = p
    a = ext_scr[...]
    g1 = POOL_GROUP_DIM
    s2 = a + pltpu.roll(a, 1, 0)
    s4 = s2[:, g1:] + pltpu.roll(s2[:, g1:], 2, 0)
    s8 = s4[:, g1:] + pltpu.roll(s4[:, g1:], 4, 0)
    s16 = s8[:, g1:] + pltpu.roll(s8[:, g1:], 8, 0)
    sums = (s2[N_META:, :g1], s4[N_META:, :g1], s8[N_META:, :g1], s16[N_META:, :])
    groups = [sums[i] * (1.0 / POOL_WINDOWS[i]) - p[:, i * g1:(i + 1) * g1] for i in range(POOL_GROUPS)]
    ext_scr[0:N_META, :] = ext_scr[tm:tm + N_META, :]

    yb = _pool_branch(groups, poolw_ref, pscale_ref, wpool_ref)
    h1 = _merge_tail(o_scr[...], g_ref[...], ga_ref[...], gb_ref[...], yb, x_ref[...], wret_ref, wout_ref)
    h1_ref[...] = h1
    _route(h1, nffn_ref, wrt_ref, br_ref, run_scr, xn2_ref, idx_ref, gate_ref, rank_ref)
    cnt_ref[...] = jnp.broadcast_to(run_scr[...], cnt_ref.shape)

    @pl.when(j == nj - 1)
    def _():
        for h in range(RET_HEADS):
            sfin_ref[0, h] = s_scr[h]
        pfin_ref[0] = ext_scr[0:N_META, :]


def _mixer(proj, x2d, kmeta, vmeta, pmeta, dec, wts, cnt0, batch, seq):
    q, k, v, g, p, ga, gb = proj
    tm = MIXER_TILE
    nj = seq // tm
    rows = batch * seq
    row_map = lambda b, j: (b * nj + j, 0)
    lane_map = lambda b, j: (0, b * nj + j)
    c2 = lambda b, j: (0, 0)
    c3 = lambda b, j: (0, 0, 0)

    def whole(a):
        return pl.BlockSpec(a.shape, c2 if a.ndim == 2 else c3)

    mask, qdec, kdec, cdec = dec
    gn, poolw, pscale, wret, wpool, wout, nffn, wrt, br = wts
    in_arrays = [q, k, v, g, p, ga, gb, x2d, kmeta, vmeta, pmeta, mask, qdec, kdec, cdec, gn,
                 poolw, pscale, wret, wpool, wout, nffn, wrt, br, cnt0]
    in_specs = [pl.BlockSpec((tm, a.shape[1]), row_map) for a in in_arrays[:8]]
    in_specs += [whole(a) for a in in_arrays[8:]]
    out_shape = [
        jax.ShapeDtypeStruct((rows, D_MODEL), F32),
        jax.ShapeDtypeStruct((rows, D_MODEL), BF16),
        jax.ShapeDtypeStruct((TOP_K, rows), jnp.int32),
        jax.ShapeDtypeStruct((TOP_K, rows), F32),
        jax.ShapeDtypeStruct((TOP_K, rows), jnp.int32),
        jax.ShapeDtypeStruct((N_EXPERTS, LANES), F32),
        jax.ShapeDtypeStruct((batch, RET_HEADS, RET_DK, RET_DV), F32),
        jax.ShapeDtypeStruct((batch, N_META, POOL_WIDTH), F32),
    ]
    out_specs = [
        pl.BlockSpec((tm, D_MODEL), row_map),
        pl.BlockSpec((tm, D_MODEL), row_map),
        pl.BlockSpec((TOP_K, tm), lane_map),
        pl.BlockSpec((TOP_K, tm), lane_map),
        pl.BlockSpec((TOP_K, tm), lane_map),
        pl.BlockSpec((N_EXPERTS, LANES), c2),
        pl.BlockSpec((1, RET_HEADS, RET_DK, RET_DV), lambda b, j: (b, 0, 0, 0)),
        pl.BlockSpec((1, N_META, POOL_WIDTH), lambda b, j: (b, 0, 0)),
    ]
    return pl.pallas_call(
        _mixer_kernel,
        grid=(batch, nj),
        in_specs=in_specs,
        out_specs=out_specs,
        out_shape=out_shape,
        scratch_shapes=[
            pltpu.VMEM((RET_HEADS, RET_DK, RET_DV), F32),
            pltpu.VMEM((N_META + tm, POOL_WIDTH), F32),
            pltpu.VMEM((tm, RET_V), F32),
            pltpu.VMEM((N_EXPERTS, 1), F32),
        ],
        compiler_params=pltpu.CompilerParams(
            dimension_semantics=("arbitrary", "arbitrary"), vmem_limit_bytes=VMEM_LIMIT),
        name="mixer",
    )(*in_arrays)


def _sample_kernel(o_ref, g_ref, p_ref, ga_ref, gb_ref, x_ref, pool_ref, gn_ref,
                   poolw_ref, pscale_ref, wret_ref, wpool_ref, wout_ref, nffn_ref, wrt_ref, br_ref,
                   poolout_ref, h1_ref, xn2_ref, idx_ref, gate_ref, rank_ref, cnt_ref,
                   run_scr):
    run_scr[...] = jnp.zeros_like(run_scr)
    o = o_ref[...]
    o_norm = jnp.concatenate(
        [_group_norm(o[:, h * RET_DV:(h + 1) * RET_DV], gn_ref[:, h * RET_DV:(h + 1) * RET_DV])
         for h in range(RET_HEADS)], axis=1)
    p = p_ref[...]
    w = POOL_WIDTH
    g1 = POOL_GROUP_DIM

    def prev(r, lo):
        return pool_ref[:, r * w + lo:(r + 1) * w]

    s2 = p + prev(14, 0)
    s4 = s2[:, g1:] + prev(13, g1) + prev(12, g1)
    s8 = s4[:, g1:] + prev(11, 2 * g1) + prev(10, 2 * g1) + prev(9, 2 * g1) + prev(8, 2 * g1)
    s16 = s8[:, g1:]
    for r in range(7, -1, -1):
        s16 = s16 + prev(r, 3 * g1)
    sums = (s2[:, :g1], s4[:, :g1], s8[:, :g1], s16)
    groups = [sums[t] * (1.0 / POOL_WINDOWS[t]) - p[:, t * g1:(t + 1) * g1] for t in range(POOL_GROUPS)]
    poolout_ref[:, 0:(POOL_BUF - 1) * w] = pool_ref[:, w:POOL_BUF * w]
    poolout_ref[:, (POOL_BUF - 1) * w:] = p
    yb = _pool_branch(groups, poolw_ref, pscale_ref, wpool_ref)
    h1 = _merge_tail(o_norm, g_ref[...], ga_ref[...], gb_ref[...], yb, x_ref[...], wret_ref, wout_ref)
    h1_ref[...] = h1
    _route(h1, nffn_ref, wrt_ref, br_ref, run_scr, xn2_ref, idx_ref, gate_ref, rank_ref)
    cnt_ref[...] = jnp.broadcast_to(run_scr[...], cnt_ref.shape)


def _sample_mixer(o_s, proj, x2d, pool2d, wts):
    _, _, _, g, p, ga, gb = proj
    nb = x2d.shape[0]
    c2 = lambda i: (0, 0)

    def whole(a):
        return pl.BlockSpec(a.shape, c2)

    gn, poolw, pscale, wret, wpool, wout, nffn, wrt, br = wts
    in_arrays = [o_s, g, p, ga, gb, x2d, pool2d, gn, poolw, pscale, wret, wpool, wout, nffn, wrt, br]
    in_specs = [whole(a) for a in in_arrays[:8]]
    in_specs += [pl.BlockSpec(poolw.shape, lambda i: (0, 0, 0))]
    in_specs += [whole(a) for a in in_arrays[9:]]
    out_shape = [
        jax.ShapeDtypeStruct(pool2d.shape, F32),
        jax.ShapeDtypeStruct((nb, D_MODEL), F32),
        jax.ShapeDtypeStruct((nb, D_MODEL), BF16),
        jax.ShapeDtypeStruct((TOP_K, nb), jnp.int32),
        jax.ShapeDtypeStruct((TOP_K, nb), F32),
        jax.ShapeDtypeStruct((TOP_K, nb), jnp.int32),
        jax.ShapeDtypeStruct((N_EXPERTS, LANES), F32),
    ]
    return pl.pallas_call(
        _sample_kernel,
        grid=(1,),
        in_specs=in_specs,
        out_specs=[pl.BlockSpec(s.shape, c2) for s in out_shape],
        out_shape=out_shape,
        scratch_shapes=[pltpu.VMEM((N_EXPERTS, 1), F32)],
        compiler_params=pltpu.CompilerParams(
            dimension_semantics=("arbitrary",), vmem_limit_bytes=VMEM_LIMIT),
        name="sample_mixer",
    )(*in_arrays)


TILE_ROWS = ROW_TILE * TOP_K


def _run_copy(src_ref, src_row, dst_ref, dst_row, n_rows, sem):
    return pltpu.make_async_copy(src_ref.at[pl.ds(src_row, n_rows)], dst_ref.at[pl.ds(dst_row, n_rows)], sem)


def _for_each_run(tcnt_ref, tile, fn):
    def body(e, off):
        n = tcnt_ref[tile * N_EXPERTS + e]

        @pl.when(n > 0)
        def _():
            fn(e, off, n)

        return off + n

    lax.fori_loop(0, N_EXPERTS, body, 0)


def _dispatch_kernel(tcnt_ref, tdst_ref, ttot_ref, zrow_ref, zcnt_ref, nblk_ref, pos_ref, xp_ref, xs_ref, out_ref,
                     sorted_scr, zero_scr, sem, zsem):
    i = pl.program_id(0)
    n_prompt = pl.num_programs(0) - 1
    n_blocks = out_ref.shape[0] // MOE_BLOCK
    slot = i % 2

    def wait_tile(tile, s):
        _run_copy(out_ref, 0, sorted_scr.at[s], 0, ttot_ref[tile], sem.at[s]).wait()

    def for_each_pad(fn):
        def body(e, carry):
            n = zcnt_ref[e]

            @pl.when(n > 0)
            def _():
                fn(_run_copy(zero_scr, 0, out_ref, zrow_ref[e], n, zsem))

            dead = nblk_ref[0] + e

            @pl.when(dead < n_blocks)
            def _():
                fn(_run_copy(zero_scr, 0, out_ref, dead * MOE_BLOCK, MOE_BLOCK, zsem))

            return carry

        lax.fori_loop(0, N_EXPERTS, body, 0)

    @pl.when(i == 0)
    def _():
        zero_scr[...] = jnp.zeros_like(zero_scr)
        for_each_pad(lambda cp: cp.start())

    def sort_tile(x_ref):
        r_iota = lax.broadcasted_iota(jnp.int32, (TILE_ROWS, ROW_TILE), 0)
        hit = r_iota == pos_ref[0:1, :]
        for kk in range(1, TOP_K):
            hit = jnp.logical_or(hit, r_iota == pos_ref[kk:kk + 1, :])
        perm = jnp.where(hit, 1.0, 0.0).astype(BF16)
        xs = _dot(perm, x_ref[...])

        @pl.when(i >= 2)
        def _():
            wait_tile(i - 2, slot)

        _store_rows_as_tiles(sorted_scr.at[slot], xs)

    @pl.when(i < n_prompt)
    def _():
        sort_tile(xp_ref)

    @pl.when(i == n_prompt)
    def _():
        sort_tile(xs_ref)

    _for_each_run(tcnt_ref, i, lambda e, off, n: _run_copy(
        sorted_scr.at[slot], off, out_ref, tdst_ref[i * N_EXPERTS + e], n, sem.at[slot]).start())

    @pl.when(i == n_prompt)
    def _():
        wait_tile(i - 1, 1 - slot)
        wait_tile(i, slot)
        for_each_pad(lambda cp: cp.wait())


def _dispatch(tcnt, tdst, ttot, zrow, zcnt, nblk, pos, xn2_p, xn2_s, n_sorted):
    n_prompt = xn2_p.shape[0] // ROW_TILE
    grid_spec = pltpu.PrefetchScalarGridSpec(
        num_scalar_prefetch=6,
        grid=(n_prompt + 1,),
        in_specs=[
            pl.BlockSpec((TOP_K, ROW_TILE), lambda i, *_: (0, i)),
            pl.BlockSpec((ROW_TILE, D_MODEL), lambda i, *_: (jnp.minimum(i, n_prompt - 1), 0)),
            pl.BlockSpec((ROW_TILE, D_MODEL), lambda i, *_: (0, 0)),
        ],
        out_specs=pl.BlockSpec(memory_space=pl.ANY),
        scratch_shapes=[
            pltpu.VMEM((2, TILE_ROWS, ROW_CHUNKS, LANES), jnp.uint32),
            pltpu.VMEM((MOE_BLOCK, ROW_CHUNKS, LANES), jnp.uint32),
            pltpu.SemaphoreType.DMA((2,)), pltpu.SemaphoreType.DMA(()),
        ],
    )
    return pl.pallas_call(
        _dispatch_kernel,
        grid_spec=grid_spec,
        out_shape=jax.ShapeDtypeStruct((n_sorted, ROW_CHUNKS, LANES), jnp.uint32),
        compiler_params=pltpu.CompilerParams(dimension_semantics=("arbitrary",)),
        name="moe_dispatch",
    )(tcnt, tdst, ttot, zrow, zcnt, nblk, pos, xn2_p, xn2_s)


def _expert_kernel(be_ref, nblk_ref, eord_ref, enext_ref, x_ref, wgu_ref, bgu_ref, wd_ref, bd_ref, y_ref,
                   wgu_f32, wd_f32, wgu_bf, wd_bf, wsem):
    i = pl.program_id(0)
    live = i < nblk_ref[0]
    new_expert = jnp.logical_or(i == 0, be_ref[i] != be_ref[jnp.maximum(i - 1, 0)])

    def weight_copies(e, s):
        return (pltpu.make_async_copy(wgu_ref.at[e], wgu_f32.at[s], wsem.at[0, s]),
                pltpu.make_async_copy(wd_ref.at[e], wd_f32.at[s], wsem.at[1, s]))

    @pl.when(jnp.logical_and(live, new_expert))
    def _():
        slot = eord_ref[i] % 2

        @pl.when(i == 0)
        def _():
            for cp in weight_copies(be_ref[0], 0):
                cp.start(priority=1)

        for cp in weight_copies(be_ref[i], slot):
            cp.wait()

        @pl.when(enext_ref[i] >= 0)
        def _():
            for cp in weight_copies(enext_ref[i], 1 - slot):
                cp.start(priority=1)

        def cast(c, carry):
            rows = pl.ds(pl.multiple_of(c * WEIGHT_CAST_ROWS, WEIGHT_CAST_ROWS), WEIGHT_CAST_ROWS)
            wgu_bf[rows, :] = wgu_f32[slot, rows, :].astype(BF16)
            wd_bf[rows, :] = wd_f32[slot, rows, :].astype(BF16)
            return carry

        lax.fori_loop(0, D_MODEL // WEIGHT_CAST_ROWS, cast, 0)

    @pl.when(jnp.logical_not(live))
    def _():
        y_ref[...] = jnp.zeros_like(y_ref)

    @pl.when(live)
    def _():
        x = _load_rows_from_tiles(x_ref, MOE_BLOCK)
        h = _dot(x.astype(BF16), wgu_bf[...]) + bgu_ref[0]
        gate = jnp.minimum(h[:, :D_FF], SWIGLU_LIMIT)
        up = jnp.clip(h[:, D_FF:], -SWIGLU_LIMIT, SWIGLU_LIMIT)
        glu = gate * _sigmoid(gate * SWIGLU_ALPHA)
        y = _dot(((up + 1.0) * glu).astype(BF16), wd_bf[...]) + bd_ref[0]
        _store_rows_as_tiles(y_ref, y.astype(BF16).astype(F32))


def _experts(block_e, nblk, block_eord, block_enext, x_sorted, wgu, bgu, wd, bd):
    n_blocks = x_sorted.shape[0] // MOE_BLOCK
    wmap = lambda i, be, *_: (be[i], 0, 0)
    rmap = lambda i, *_: (i, 0, 0)
    hbm = pl.BlockSpec(memory_space=pl.ANY)
    grid_spec = pltpu.PrefetchScalarGridSpec(
        num_scalar_prefetch=4,
        grid=(n_blocks,),
        in_specs=[
            pl.BlockSpec((MOE_BLOCK, ROW_CHUNKS, LANES), rmap),
            hbm,
            pl.BlockSpec((1, 1, 2 * D_FF), wmap),
            hbm,
            pl.BlockSpec((1, 1, D_MODEL), wmap),
        ],
        out_specs=pl.BlockSpec((MOE_BLOCK, ROW_CHUNKS, LANES), rmap),
        scratch_shapes=[
            pltpu.VMEM((2, D_MODEL, 2 * D_FF), F32), pltpu.VMEM((2, D_FF, D_MODEL), F32),
            pltpu.VMEM((D_MODEL, 2 * D_FF), BF16), pltpu.VMEM((D_FF, D_MODEL), BF16),
            pltpu.SemaphoreType.DMA((2, 2)),
        ],
    )
    return pl.pallas_call(
        _expert_kernel,
        grid_spec=grid_spec,
        out_shape=jax.ShapeDtypeStruct(x_sorted.shape, jnp.uint32),
        compiler_params=pltpu.CompilerParams(
            dimension_semantics=("arbitrary",), vmem_limit_bytes=VMEM_LIMIT),
        name="moe_experts",
    )(block_e, nblk, block_eord, block_enext, x_sorted, wgu, bgu, wd, bd)


def _combine_kernel(tcnt_ref, tdst_ref, ttot_ref, pos_ref, gate_ref, ys_ref, hp_ref, hs_ref, nf_ref, yp_ref, ysmp_ref,
                    runs_scr, sem):
    i = pl.program_id(0)
    n_tiles = pl.num_programs(0)
    n_prompt = n_tiles - 1
    slot = i % 2

    def start_runs(tile, s):
        _for_each_run(tcnt_ref, tile, lambda e, off, n: _run_copy(
            ys_ref, tdst_ref[tile * N_EXPERTS + e], runs_scr.at[s], off, n, sem.at[s]).start())

    @pl.when(i == 0)
    def _():
        start_runs(0, 0)

    @pl.when(i + 1 < n_tiles)
    def _():
        start_runs(i + 1, 1 - slot)

    _run_copy(ys_ref, 0, runs_scr.at[slot], 0, ttot_ref[i], sem.at[slot]).wait()
    ys = _load_rows_from_tiles(runs_scr.at[slot], TILE_ROWS).astype(BF16)
    r_iota = lax.broadcasted_iota(jnp.int32, (TILE_ROWS, ROW_TILE), 0)
    gmat = jnp.zeros((TILE_ROWS, ROW_TILE), F32)
    for kk in range(TOP_K):
        gmat = jnp.where(r_iota == pos_ref[kk:kk + 1, :], gate_ref[kk:kk + 1, :], gmat)
    moe = lax.dot_general(gmat.astype(BF16), ys, (((0,), (0,)), ((), ())), preferred_element_type=F32)

    @pl.when(i < n_prompt)
    def _():
        yp_ref[...] = _rmsnorm(hp_ref[...] + moe, nf_ref[...])

    @pl.when(i == n_prompt)
    def _():
        ysmp_ref[...] = _rmsnorm(hs_ref[...] + moe, nf_ref[...])


def _combine(tcnt, tdst, ttot, pos, gates, y_sorted, h1_p, h1_s, nf):
    n_prompt = h1_p.shape[0] // ROW_TILE
    pmap = lambda i, *_: (jnp.minimum(i, n_prompt - 1), 0)
    smap = lambda i, *_: (0, 0)
    lmap = lambda i, *_: (0, i)
    grid_spec = pltpu.PrefetchScalarGridSpec(
        num_scalar_prefetch=3,
        grid=(n_prompt + 1,),
        in_specs=[
            pl.BlockSpec((TOP_K, ROW_TILE), lmap),
            pl.BlockSpec((TOP_K, ROW_TILE), lmap),
            pl.BlockSpec(memory_space=pl.ANY),
            pl.BlockSpec((ROW_TILE, D_MODEL), pmap),
            pl.BlockSpec((ROW_TILE, D_MODEL), smap),
            pl.BlockSpec((1, D_MODEL), smap),
        ],
        out_specs=[pl.BlockSpec((ROW_TILE, D_MODEL), pmap), pl.BlockSpec((ROW_TILE, D_MODEL), smap)],
        scratch_shapes=[pltpu.VMEM((2, TILE_ROWS, ROW_CHUNKS, LANES), jnp.uint32), pltpu.SemaphoreType.DMA((2,))],
    )
    return pl.pallas_call(
        _combine_kernel,
        grid_spec=grid_spec,
        out_shape=[jax.ShapeDtypeStruct(h1_p.shape, F32), jax.ShapeDtypeStruct(h1_s.shape, F32)],
        compiler_params=pltpu.CompilerParams(dimension_semantics=("arbitrary",)),
        name="moe_combine",
    )(tcnt, tdst, ttot, pos, gates, y_sorted, h1_p, h1_s, nf)


def _rotary_tables(pos):
    f = np.float32
    inv = np.power(f(ROPE_BASE), -np.arange(0, RET_DK, 2, dtype=f) / f(RET_DK)).astype(f)
    ang = (np.asarray(pos, f)[:, None] * inv[None, :]).astype(f)
    cos, sin = np.cos(ang).astype(f), np.sin(ang).astype(f)
    return np.concatenate([cos, cos], axis=1), np.concatenate([-sin, sin], axis=1)


def _decay_tables(chunk):
    f = np.float32
    log_g = np.log1p(-np.exp2(f(-5.0) - np.arange(RET_HEADS, dtype=f))).astype(f)
    i = np.arange(chunk, dtype=f)
    diff = i[:, None] - i[None, :]
    mask = np.where(diff[None] >= 0, np.exp(np.maximum(diff, f(0.0))[None] * log_g[:, None, None]), f(0.0)).astype(f)
    q_dec = np.exp((i + f(1.0))[None, :] * log_g[:, None]).astype(f)
    k_dec = np.exp((f(chunk) - f(1.0) - i)[None, :] * log_g[:, None]).astype(f)
    c_dec = np.exp(f(chunk) * log_g).astype(f)
    return mask, q_dec, k_dec, c_dec


def kernel(x_prompt, x_sample, state_ret, state_pool, meta_tokens, norm_mix, w_in, ret_gn, pool_w, pool_scale,
           w_ret_branch, w_pool_branch, w_out, norm_ffn, w_router, b_router, w_gate_up, b_gate_up, w_down, b_down,
           norm_final):
    batch, seq, _ = x_prompt.shape
    nb = x_sample.shape[0]
    past_len = 16384
    n_prompt_tok = batch * seq
    n_tok = n_prompt_tok + nb

    w_in_bf = w_in[0].astype(BF16)
    wts = (ret_gn[0][None, :], pool_w[0].astype(BF16), pool_scale[0][None, :],
           w_ret_branch[0].astype(BF16), w_pool_branch[0].astype(BF16), w_out[0].astype(BF16),
           norm_ffn[0][None, :], w_router[0].T.astype(BF16), b_router[0][:, None])
    wgu = w_gate_up[0]
    wd = w_down[0]
    bgu = b_gate_up[0][:, None, :]
    bd = b_down[0][:, None, :]
    nmix = norm_mix[0][None, :]

    cos_p, sin_p = _rotary_tables(N_META + np.arange(seq))
    cos_s, sin_s = _rotary_tables(np.concatenate([np.arange(N_META), np.full((nb,), past_len)]))
    mask, q_dec, k_dec, c_dec = _decay_tables(CHUNK)
    dec = (mask,
           np.ascontiguousarray(np.broadcast_to(q_dec[:, :, None], (RET_HEADS, CHUNK, RET_DK))),
           np.ascontiguousarray(np.broadcast_to(k_dec[:, :, None], (RET_HEADS, CHUNK, RET_DK))),
           np.ascontiguousarray(np.broadcast_to(c_dec[:, None, None], (RET_HEADS, 1, RET_DV))))
    m1, q1, k1, c1 = _decay_tables(1)
    sdec = np.stack([m1[:, 0, 0], q1[:, 0], k1[:, 0], c1], axis=1)

    x2d = x_prompt.reshape(n_prompt_tok, D_MODEL)
    xs2d = x_sample.reshape(nb, D_MODEL)
    x_small = jnp.concatenate([meta_tokens, xs2d], axis=0)
    proj_small = _inproj(x_small, nmix, w_in_bf, cos_s, sin_s, N_META + nb, 1, F32)
    proj_s = tuple(a[N_META:] for a in proj_small)

    grp = nb // (n_prompt_tok // INPROJ_TILE)

    def cols(a):
        return a.T.reshape(RET_QK, nb // grp, grp).transpose(1, 0, 2)

    *proj_p, st_s, o_s = _inproj(
        x2d, nmix, w_in_bf, cos_p, sin_p, INPROJ_TILE, batch, BF16,
        sample=(sdec, cols(proj_s[0]), cols(proj_s[1]), proj_s[0], proj_s[1], proj_s[2], state_ret[0]))

    lead = CHUNK - N_META
    kmeta = jnp.pad(proj_small[1][:N_META], ((lead, 0), (0, 0))).astype(BF16)
    vmeta = jnp.pad(proj_small[2][:N_META], ((lead, 0), (0, 0))).astype(BF16)
    pmeta = proj_small[4][:N_META]

    pool2d = state_pool[0].reshape(nb, POOL_BUF * POOL_WIDTH)
    (pool_s, h1_s, xn2_s, idx_s, gate_s, rank_s, cnt_s) = _sample_mixer(o_s, proj_s, xs2d, pool2d, wts)

    (h1_p, xn2_p, idx_p, gate_p, rank_p, cnt, s_fin, p_fin) = _mixer(
        proj_p, x2d, kmeta, vmeta, pmeta, dec, wts, cnt_s, batch, seq)

    assert n_prompt_tok % ROW_TILE == 0 and nb <= ROW_TILE
    n_phantom = ROW_TILE - nb
    n_tiles = n_prompt_tok // ROW_TILE + 1
    i32 = jnp.int32
    counts = cnt[:, 0].astype(i32)
    padded = ((counts + MOE_BLOCK - 1) // MOE_BLOCK) * MOE_BLOCK
    pad_end = jnp.cumsum(padded)
    pad_start = pad_end - padded
    n_blocks = (n_tok * TOP_K) // MOE_BLOCK + N_EXPERTS
    block_row = jnp.arange(n_blocks, dtype=i32) * MOE_BLOCK
    block_e = jnp.minimum(jnp.sum((pad_end[None, :] <= block_row[:, None]).astype(i32), axis=1), N_EXPERTS - 1)
    nblk = (pad_end[-1:] // MOE_BLOCK).astype(i32)
    e_row = jnp.arange(N_EXPERTS, dtype=i32)
    used = padded > 0
    e_ord = jnp.cumsum(used.astype(i32)) - 1
    later_used = jnp.logical_and(e_row[None, :] > e_row[:, None], used[None, :])
    e_next = jnp.min(jnp.where(later_used, e_row[None, :], N_EXPERTS), axis=1)
    e_next = jnp.where(e_next == N_EXPERTS, -1, e_next)
    of_block = block_e[:, None] == e_row[None, :]
    block_eord = jnp.sum(jnp.where(of_block, e_ord[None, :], 0), axis=1)
    block_enext = jnp.sum(jnp.where(of_block, e_next[None, :], 0), axis=1)
    phantom = lambda fill, dt: jnp.full((TOP_K, n_phantom), fill, dt)
    idx = jnp.concatenate([idx_p, idx_s, phantom(-1, i32)], axis=1)
    rank = jnp.concatenate([rank_p, rank_s, phantom(0, i32)], axis=1)
    gates = jnp.concatenate([gate_p, gate_s, phantom(0.0, F32)], axis=1)
    onehot = idx[None] == jnp.arange(N_EXPERTS, dtype=i32)[:, None, None]
    tile_cnt = jnp.sum(onehot.reshape(N_EXPERTS, TOP_K, n_tiles, ROW_TILE).astype(i32), axis=(1, 3)).T
    by_time = jnp.concatenate([tile_cnt[-1:], tile_cnt[:-1]], axis=0)
    before_time = jnp.cumsum(by_time, axis=0) - by_time
    run_before = jnp.concatenate([before_time[1:], before_time[:1]], axis=0)
    tile_off = jnp.cumsum(tile_cnt, axis=1) - tile_cnt
    tile_dst = pad_start[None, :] + run_before
    delta = jnp.repeat((tile_off - run_before).T, ROW_TILE, axis=1)
    pos = rank + jnp.sum(jnp.where(onehot, delta[:, None, :], 0), axis=0)
    pos = jnp.where(idx >= 0, pos, -1)
    tcnt, tdst, ttot = tile_cnt.reshape(-1), tile_dst.reshape(-1), jnp.sum(tile_cnt, axis=1)

    tail = ((0, n_phantom), (0, 0))
    x_sorted = _dispatch(tcnt, tdst, ttot, pad_start + counts, padded - counts, nblk, pos, xn2_p,
                         jnp.pad(xn2_s, tail), n_blocks * MOE_BLOCK)
    y_sorted = _experts(block_e, nblk, block_eord, block_enext, x_sorted, wgu, bgu, wd, bd)
    y_p, y_s = _combine(tcnt, tdst, ttot, pos, gates, y_sorted, h1_p, jnp.pad(h1_s, tail), norm_final[None, :])

    y_prompt = y_p.reshape(batch, seq, D_MODEL)
    y_sample = y_s[:nb].reshape(nb, 1, D_MODEL)
    ret_state_prompt = s_fin[None]
    pool_state_prompt = p_fin[:, 1:, :][None]
    ret_state_sample = st_s[None]
    pool_state_sample = pool_s.reshape(nb, POOL_BUF, POOL_WIDTH)[None]
    return (y_prompt, y_sample, ret_state_prompt, pool_state_prompt, ret_state_sample, pool_state_sample)
```

```python
import functools

import jax
import jax.numpy as jnp
import numpy as np
from jax import lax
from jax.experimental import pallas as pl
from jax.experimental.pallas import tpu as pltpu

F32 = jnp.float32
BF16 = jnp.bfloat16

D_MODEL = 1024
N_META = 16
RET_HEADS = 4
RET_DK = 128
RET_DV = 256
RET_QK = RET_HEADS * RET_DK
RET_V = RET_HEADS * RET_DV
CHUNK = 128
ROPE_BASE = 10000.0
POOL_WINDOWS = (2, 4, 8, 16)
POOL_GROUPS = 4
POOL_GROUP_DIM = 128
POOL_WIDTH = POOL_GROUPS * POOL_GROUP_DIM
POOL_BUF = max(POOL_WINDOWS) - 1
N_EXPERTS = 32
TOP_K = 4
D_FF = D_MODEL
SWIGLU_LIMIT = 7.0
SWIGLU_ALPHA = 1.702
EPS = 1e-6
IN_WIDTHS = (RET_QK, RET_QK, RET_V, RET_V, POOL_WIDTH, D_MODEL, D_MODEL)
IN_TOTAL = sum(IN_WIDTHS)
IN_OFFS = tuple(int(s) for s in np.cumsum((0,) + IN_WIDTHS))

LANES = 128
ROW_CHUNKS = D_MODEL // (2 * LANES)
INPROJ_TILE = 512
MIXER_TILE = 512
MOE_BLOCK = 512
ROW_TILE = 256
SUBLANES = 8
WEIGHT_CAST_ROWS = 128
VMEM_LIMIT = 56 * 1024 * 1024

assert N_META + 1 >= max(POOL_WINDOWS)
assert POOL_WINDOWS == (2, 4, 8, 16)


def _dot(a, b):
    return jnp.dot(a, b, preferred_element_type=F32)


def _rmsnorm(x, w):
    return x * lax.rsqrt(jnp.mean(x * x, axis=-1, keepdims=True) + EPS) * w


def _sigmoid(x):
    return 0.5 * jnp.tanh(0.5 * x) + 0.5


def _store_rows_as_tiles(ref, x):
    half = D_MODEL // 2
    hi = pltpu.bitcast(x[:, :half], jnp.uint32) & jnp.uint32(0xFFFF0000)
    lo = pltpu.bitcast(x[:, half:], jnp.uint32) >> 16
    ref[...] = (hi | lo).reshape(x.shape[0], ROW_CHUNKS, LANES)


def _load_rows_from_tiles(ref, rows):
    w = ref[...].reshape(rows, D_MODEL // 2)
    hi = pltpu.bitcast(w & jnp.uint32(0xFFFF0000), F32)
    lo = pltpu.bitcast(w << 16, F32)
    return jnp.concatenate([hi, lo], axis=1)


def _inproj_kernel(x_ref, nw_ref, w_ref, cos_ref, sin_ref,
                   q_ref, k_ref, v_ref, g_ref, p_ref, ga_ref, gb_ref):
    xn = _rmsnorm(x_ref[...], nw_ref[...]).astype(BF16)
    cos = cos_ref[...]
    sin = sin_ref[...]

    def seg(i):
        return _dot(xn, w_ref[:, IN_OFFS[i]:IN_OFFS[i + 1]])

    def rot(a):
        return a * cos + pltpu.roll(a, RET_DK // 2, 1) * sin

    q = seg(0)
    k = seg(1)
    for h in range(RET_HEADS):
        sl = slice(h * RET_DK, (h + 1) * RET_DK)
        q_ref[:, sl] = rot(q[:, sl]).astype(q_ref.dtype)
        k_ref[:, sl] = (rot(k[:, sl]) * (RET_DK ** -0.5)).astype(k_ref.dtype)
    v_ref[...] = seg(2).astype(v_ref.dtype)
    g = seg(3)
    g_ref[...] = (g * _sigmoid(g)).astype(g_ref.dtype)
    p_ref[...] = seg(4)
    ga_ref[...] = _sigmoid(seg(5)).astype(ga_ref.dtype)
    gb_ref[...] = _sigmoid(seg(6)).astype(gb_ref.dtype)


def _sample_state_step(step, sdec_ref, qt_ref, kt_ref, q_ref, k_ref, v_ref, st_ref, stout_ref, o_ref):
    grp = st_ref.shape[0]
    row0 = step * grp
    if grp % SUBLANES == 0:
        rows = pl.ds(pl.multiple_of(row0, SUBLANES), grp)
        q8, k8, v8 = q_ref[rows, :], k_ref[rows, :], v_ref[rows, :]
    else:
        assert 2 * grp == SUBLANES
        rows = pl.ds(pl.multiple_of((step // 2) * SUBLANES, SUBLANES), SUBLANES)
        first = step % 2 == 0
        q8, k8, v8 = (jnp.where(first, a[:grp], a[grp:]) for a in (q_ref[rows, :], k_ref[rows, :], v_ref[rows, :]))
    for h in range(RET_HEADS):
        ksl = slice(h * RET_DK, (h + 1) * RET_DK)
        vsl = slice(h * RET_DV, (h + 1) * RET_DV)
        score = jnp.sum(q8[:, ksl] * k8[:, ksl], axis=1, keepdims=True) * sdec_ref[h, 0]
        intra = score * v8[:, vsl]
        for bb in range(grp):
            s_old = st_ref[bb, h]
            qcol = qt_ref[0, ksl, bb:bb + 1] * sdec_ref[h, 1]
            kcol = kt_ref[0, ksl, bb:bb + 1] * sdec_ref[h, 2]
            cross = jnp.sum(s_old * qcol, axis=0, keepdims=True)
            o_ref[pl.ds(row0 + bb, 1), vsl] = intra[bb:bb + 1, :] + cross
            stout_ref[bb, h] = s_old * sdec_ref[h, 3] + kcol * v8[bb:bb + 1, vsl]


def _inproj_sample_kernel(x_ref, nw_ref, w_ref, cos_ref, sin_ref,
                          sdec_ref, qt_ref, kt_ref, qs_ref, ks_ref, vs_ref, st_ref,
                          q_ref, k_ref, v_ref, g_ref, p_ref, ga_ref, gb_ref, stout_ref, os_ref):
    _inproj_kernel(x_ref, nw_ref, w_ref, cos_ref, sin_ref, q_ref, k_ref, v_ref, g_ref, p_ref, ga_ref, gb_ref)
    step = pl.program_id(0) * pl.num_programs(1) + pl.program_id(1)
    _sample_state_step(step, sdec_ref, qt_ref, kt_ref, qs_ref, ks_ref, vs_ref, st_ref, stout_ref, os_ref)


def _inproj(x2d, nw, w_in_bf, cosf, sinf, tile, n_outer, act_dtype, sample=None):
    rows = x2d.shape[0]
    n_inner = rows // (tile * n_outer)
    row_map = lambda b, j: (b * n_inner + j, 0)
    tab_map = lambda b, j: (j, 0)
    const = lambda b, j: (0, 0)
    widths = IN_WIDTHS
    dts = (act_dtype, act_dtype, act_dtype, act_dtype, F32, act_dtype, act_dtype)
    in_arrays = [x2d, nw, w_in_bf, cosf, sinf]
    in_specs = [
        pl.BlockSpec((tile, D_MODEL), row_map),
        pl.BlockSpec((1, D_MODEL), const),
        pl.BlockSpec((D_MODEL, IN_TOTAL), const, pipeline_mode=pl.Buffered(1)),
        pl.BlockSpec((tile, RET_DK), tab_map),
        pl.BlockSpec((tile, RET_DK), tab_map),
    ]
    out_specs = [pl.BlockSpec((tile, w), row_map) for w in widths]
    out_shape = [jax.ShapeDtypeStruct((rows, w), dt) for w, dt in zip(widths, dts)]
    body = _inproj_kernel
    if sample is not None:
        sdec, qt, kt, qs, ks, vs, state = sample
        nb = state.shape[0]
        grp = nb // (n_outer * n_inner)
        assert grp * n_outer * n_inner == nb and qt.shape == (nb // grp, RET_QK, grp)
        step_map = lambda b, j: (b * n_inner + j, 0, 0)
        st_spec = pl.BlockSpec((grp, RET_HEADS, RET_DK, RET_DV), lambda b, j: (b * n_inner + j, 0, 0, 0))
        in_arrays += [sdec, qt, kt, qs, ks, vs, state]
        in_specs += [pl.BlockSpec(memory_space=pltpu.SMEM),
                     pl.BlockSpec((1, RET_QK, grp), step_map), pl.BlockSpec((1, RET_QK, grp), step_map),
                     pl.BlockSpec(qs.shape, const), pl.BlockSpec(ks.shape, const), pl.BlockSpec(vs.shape, const),
                     st_spec]
        out_specs += [st_spec, pl.BlockSpec((nb, RET_V), const)]
        out_shape += [jax.ShapeDtypeStruct(state.shape, F32), jax.ShapeDtypeStruct((nb, RET_V), F32)]
        body = _inproj_sample_kernel
    return pl.pallas_call(
        body,
        grid=(n_outer, n_inner),
        in_specs=in_specs,
        out_specs=out_specs,
        out_shape=out_shape,
        compiler_params=pltpu.CompilerParams(
            dimension_semantics=("arbitrary", "arbitrary"), vmem_limit_bytes=VMEM_LIMIT),
        name="inproj",
    )(*in_arrays)


def _group_norm(o, gn_row):
    mu = jnp.mean(o, axis=-1, keepdims=True)
    var = jnp.mean(jnp.square(o - mu), axis=-1, keepdims=True)
    return (o - mu) * lax.rsqrt(var + EPS) * gn_row


def _pool_branch(groups, poolw_ref, pscale_ref, wpool_ref):
    pm = [_dot(g.astype(BF16), poolw_ref[i]) for i, g in enumerate(groups)]
    pm = jnp.concatenate(pm, axis=1) * pscale_ref[...]
    return _dot(pm.astype(BF16), wpool_ref[...])


def _merge_tail(o_norm, silu_g, sig_a, sig_b, yb, x, wret_ref, wout_ref):
    ya = _dot((silu_g.astype(F32) * o_norm).astype(BF16), wret_ref[...])
    merged = sig_a.astype(F32) * ya + sig_b.astype(F32) * yb
    return x + _dot(merged.astype(BF16), wout_ref[...])


def _route(h1, nffn_ref, wrt_ref, br_ref, run_scr,
           xn2_ref, idx_ref, gate_ref, rank_ref):
    tm = h1.shape[0]
    xn2 = _rmsnorm(h1, nffn_ref[...]).astype(BF16)
    xn2_ref[...] = xn2
    logits = lax.dot_general(wrt_ref[...], xn2, (((1,), (1,)), ((), ())),
                             preferred_element_type=F32) + br_ref[...]
    e_iota = lax.broadcasted_iota(jnp.int32, (N_EXPERTS, tm), 0)
    work = logits
    vals, sels = [], []
    chosen = jnp.zeros((N_EXPERTS, tm), F32)
    for _ in range(TOP_K):
        m = jnp.max(work, axis=0, keepdims=True)
        sel = jnp.min(jnp.where(work == m, e_iota, N_EXPERTS), axis=0, keepdims=True)
        hit = e_iota == sel
        vals.append(m)
        sels.append(sel)
        chosen = jnp.where(hit, 1.0, chosen)
        work = jnp.where(hit, -jnp.inf, work)
    exps = [jnp.exp(v - vals[0]) for v in vals]
    denom = exps[0] + exps[1] + exps[2] + exps[3]
    gates = [e / denom for e in exps]
    r_i = lax.broadcasted_iota(jnp.int32, (tm, tm), 0)
    c_i = lax.broadcasted_iota(jnp.int32, (tm, tm), 1)
    before = jnp.where(r_i < c_i, 1.0, 0.0).astype(BF16)
    base = run_scr[...] + _dot(chosen.astype(BF16), before)
    for kk in range(TOP_K):
        rk = jnp.sum(jnp.where(e_iota == sels[kk], base, 0.0), axis=0, keepdims=True)
        rank_ref[kk:kk + 1, :] = rk.astype(jnp.int32)
        idx_ref[kk:kk + 1, :] = sels[kk]
        gate_ref[kk:kk + 1, :] = gates[kk]
    run_scr[...] = run_scr[...] + jnp.sum(chosen, axis=1, keepdims=True)


def _mixer_kernel(q_ref, k_ref, v_ref, g_ref, p_ref, ga_ref, gb_ref, x_ref,
                  kmeta_ref, vmeta_ref, pmeta_ref, mask_ref, qdec_ref, kdec_ref, cdec_ref, gn_ref,
                  poolw_ref, pscale_ref, wret_ref, wpool_ref, wout_ref, nffn_ref, wrt_ref, br_ref,
                  cnt0_ref,
                  h1_ref, xn2_ref, idx_ref, gate_ref, rank_ref, cnt_ref, sfin_ref, pfin_ref,
                  s_scr, ext_scr, o_scr, run_scr):
    b = pl.program_id(0)
    j = pl.program_id(1)
    nj = pl.num_programs(1)
    tm = q_ref.shape[0]

    def state_update(s_old, kc, vc, h):
        kd = (kc.astype(F32) * kdec_ref[h]).astype(BF16)
        upd = lax.dot_general(kd, vc, (((0,), (0,)), ((), ())), preferred_element_type=F32)
        return s_old * cdec_ref[h] + upd

    @pl.when(jnp.logical_and(b == 0, j == 0))
    def _():
        run_scr[...] = cnt0_ref[:, 0:1]

    @pl.when(j == 0)
    def _():
        for h in range(RET_HEADS):
            kc = kmeta_ref[:, h * RET_DK:(h + 1) * RET_DK]
            vc = vmeta_ref[:, h * RET_DV:(h + 1) * RET_DV]
            s_scr[h] = state_update(jnp.zeros((RET_DK, RET_DV), F32), kc, vc, h)
        ext_scr[0:N_META, :] = pmeta_ref[...]

    for c in range(tm // CHUNK):
        rows = slice(c * CHUNK, (c + 1) * CHUNK)
        for h in range(RET_HEADS):
            qc = q_ref[rows, h * RET_DK:(h + 1) * RET_DK]
            kc = k_ref[rows, h * RET_DK:(h + 1) * RET_DK]
            vc = v_ref[rows, h * RET_DV:(h + 1) * RET_DV]
            s_old = s_scr[h]
            scores = lax.dot_general(qc, kc, (((1,), (1,)), ((), ())),
                                     preferred_element_type=F32) * mask_ref[h]
            qd = (qc.astype(F32) * qdec_ref[h]).astype(BF16)
            lhs = jnp.concatenate([scores.astype(BF16), qd], axis=1)
            rhs = jnp.concatenate([vc, s_old.astype(BF16)], axis=0)
            o = _dot(lhs, rhs)
            s_scr[h] = state_update(s_old, kc, vc, h)
            o_scr[rows, h * RET_DV:(h + 1) * RET_DV] = _group_norm(
                o, gn_ref[:, h * RET_DV:(h + 1) * RET_DV])

    p = p_ref[...]
    ext_scr[N_META:N_META + tm, :] = p
    a = ext_scr[...]
    g1 = POOL_GROUP_DIM
    s2 = a + pltpu.roll(a, 1, 0)
    s4 = s2[:, g1:] + pltpu.roll(s2[:, g1:], 2, 0)
    s8 = s4[:, g1:] + pltpu.roll(s4[:, g1:], 4, 0)
    s16 = s8[:, g1:] + pltpu.roll(s8[:, g1:], 8, 0)
    sums = (s2[N_META:, :g1], s4[N_META:, :g1], s8[N_META:, :g1], s16[N_META:, :])
    groups = [sums[i] * (1.0 / POOL_WINDOWS[i]) - p[:, i * g1:(i + 1) * g1] for i in range(POOL_GROUPS)]
    ext_scr[0:N_META, :] = ext_scr[tm:tm + N_META, :]

    yb = _pool_branch(groups, poolw_ref, pscale_ref, wpool_ref)
    h1 = _merge_tail(o_scr[...], g_ref[...], ga_ref[...], gb_ref[...], yb, x_ref[...], wret_ref, wout_ref)
    h1_ref[...] = h1
    _route(h1, nffn_ref, wrt_ref, br_ref, run_scr, xn2_ref, idx_ref, gate_ref, rank_ref)
    cnt_ref[...] = jnp.broadcast_to(run_scr[...], cnt_ref.shape)

    @pl.when(j == nj - 1)
    def _():
        for h in range(RET_HEADS):
            sfin_ref[0, h] = s_scr[h]
        pfin_ref[0] = ext_scr[0:N_META, :]


def _mixer(proj, x2d, kmeta, vmeta, pmeta, dec, wts, cnt0, batch, seq):
    q, k, v, g, p, ga, gb = proj
    tm = MIXER_TILE
    nj = seq // tm
    rows = batch * seq
    row_map = lambda b, j: (b * nj + j, 0)
    lane_map = lambda b, j: (0, b * nj + j)
    c2 = lambda b, j: (0, 0)
    c3 = lambda b, j: (0, 0, 0)

    def whole(a):
        return pl.BlockSpec(a.shape, c2 if a.ndim == 2 else c3)

    mask, qdec, kdec, cdec = dec
    gn, poolw, pscale, wret, wpool, wout, nffn, wrt, br = wts
    in_arrays = [q, k, v, g, p, ga, gb, x2d, kmeta, vmeta, pmeta, mask, qdec, kdec, cdec, gn,
                 poolw, pscale, wret, wpool, wout, nffn, wrt, br, cnt0]
    in_specs = [pl.BlockSpec((tm, a.shape[1]), row_map) for a in in_arrays[:8]]
    in_specs += [whole(a) for a in in_arrays[8:]]
    out_shape = [
        jax.ShapeDtypeStruct((rows, D_MODEL), F32),
        jax.ShapeDtypeStruct((rows, D_MODEL), BF16),
        jax.ShapeDtypeStruct((TOP_K, rows), jnp.int32),
        jax.ShapeDtypeStruct((TOP_K, rows), F32),
        jax.ShapeDtypeStruct((TOP_K, rows), jnp.int32),
        jax.ShapeDtypeStruct((N_EXPERTS, LANES), F32),
        jax.ShapeDtypeStruct((batch, RET_HEADS, RET_DK, RET_DV), F32),
        jax.ShapeDtypeStruct((batch, N_META, POOL_WIDTH), F32),
    ]
    out_specs = [
        pl.BlockSpec((tm, D_MODEL), row_map),
        pl.BlockSpec((tm, D_MODEL), row_map),
        pl.BlockSpec((TOP_K, tm), lane_map),
        pl.BlockSpec((TOP_K, tm), lane_map),
        pl.BlockSpec((TOP_K, tm), lane_map),
        pl.BlockSpec((N_EXPERTS, LANES), c2),
        pl.BlockSpec((1, RET_HEADS, RET_DK, RET_DV), lambda b, j: (b, 0, 0, 0)),
        pl.BlockSpec((1, N_META, POOL_WIDTH), lambda b, j: (b, 0, 0)),
    ]
    return pl.pallas_call(
        _mixer_kernel,
        grid=(batch, nj),
        in_specs=in_specs,
        out_specs=out_specs,
        out_shape=out_shape,
        scratch_shapes=[
            pltpu.VMEM((RET_HEADS, RET_DK, RET_DV), F32),
            pltpu.VMEM((N_META + tm, POOL_WIDTH), F32),
            pltpu.VMEM((tm, RET_V), F32),
            pltpu.VMEM((N_EXPERTS, 1), F32),
        ],
        compiler_params=pltpu.CompilerParams(
            dimension_semantics=("arbitrary", "arbitrary"), vmem_limit_bytes=VMEM_LIMIT),
        name="mixer",
    )(*in_arrays)


def _sample_kernel(o_ref, g_ref, p_ref, ga_ref, gb_ref, x_ref, pool_ref, gn_ref,
                   poolw_ref, pscale_ref, wret_ref, wpool_ref, wout_ref, nffn_ref, wrt_ref, br_ref,
                   poolout_ref, h1_ref, xn2_ref, idx_ref, gate_ref, rank_ref, cnt_ref,
                   run_scr):
    run_scr[...] = jnp.zeros_like(run_scr)
    o = o_ref[...]
    o_norm = jnp.concatenate(
        [_group_norm(o[:, h * RET_DV:(h + 1) * RET_DV], gn_ref[:, h * RET_DV:(h + 1) * RET_DV])
         for h in range(RET_HEADS)], axis=1)
    p = p_ref[...]
    w = POOL_WIDTH
    g1 = POOL_GROUP_DIM

    def prev(r, lo):
        return pool_ref[:, r * w + lo:(r + 1) * w]

    s2 = p + prev(14, 0)
    s4 = s2[:, g1:] + prev(13, g1) + prev(12, g1)
    s8 = s4[:, g1:] + prev(11, 2 * g1) + prev(10, 2 * g1) + prev(9, 2 * g1) + prev(8, 2 * g1)
    s16 = s8[:, g1:]
    for r in range(7, -1, -1):
        s16 = s16 + prev(r, 3 * g1)
    sums = (s2[:, :g1], s4[:, :g1], s8[:, :g1], s16)
    groups = [sums[t] * (1.0 / POOL_WINDOWS[t]) - p[:, t * g1:(t + 1) * g1] for t in range(POOL_GROUPS)]
    poolout_ref[:, 0:(POOL_BUF - 1) * w] = pool_ref[:, w:POOL_BUF * w]
    poolout_ref[:, (POOL_BUF - 1) * w:] = p
    yb = _pool_branch(groups, poolw_ref, pscale_ref, wpool_ref)
    h1 = _merge_tail(o_norm, g_ref[...], ga_ref[...], gb_ref[...], yb, x_ref[...], wret_ref, wout_ref)
    h1_ref[...] = h1
    _route(h1, nffn_ref, wrt_ref, br_ref, run_scr, xn2_ref, idx_ref, gate_ref, rank_ref)
    cnt_ref[...] = jnp.broadcast_to(run_scr[...], cnt_ref.shape)


def _sample_mixer(o_s, proj, x2d, pool2d, wts):
    _, _, _, g, p, ga, gb = proj
    nb = x2d.shape[0]
    c2 = lambda i: (0, 0)

    def whole(a):
        return pl.BlockSpec(a.shape, c2)

    gn, poolw, pscale, wret, wpool, wout, nffn, wrt, br = wts
    in_arrays = [o_s, g, p, ga, gb, x2d, pool2d, gn, poolw, pscale, wret, wpool, wout, nffn, wrt, br]
    in_specs = [whole(a) for a in in_arrays[:8]]
    in_specs += [pl.BlockSpec(poolw.shape, lambda i: (0, 0, 0))]
    in_specs += [whole(a) for a in in_arrays[9:]]
    out_shape = [
        jax.ShapeDtypeStruct(pool2d.shape, F32),
        jax.ShapeDtypeStruct((nb, D_MODEL), F32),
        jax.ShapeDtypeStruct((nb, D_MODEL), BF16),
        jax.ShapeDtypeStruct((TOP_K, nb), jnp.int32),
        jax.ShapeDtypeStruct((TOP_K, nb), F32),
        jax.ShapeDtypeStruct((TOP_K, nb), jnp.int32),
        jax.ShapeDtypeStruct((N_EXPERTS, LANES), F32),
    ]
    return pl.pallas_call(
        _sample_kernel,
        grid=(1,),
        in_specs=in_specs,
        out_specs=[pl.BlockSpec(s.shape, c2) for s in out_shape],
        out_shape=out_shape,
        scratch_shapes=[pltpu.VMEM((N_EXPERTS, 1), F32)],
        compiler_params=pltpu.CompilerParams(
            dimension_semantics=("arbitrary",), vmem_limit_bytes=VMEM_LIMIT),
        name="sample_mixer",
    )(*in_arrays)


TILE_ROWS = ROW_TILE * TOP_K


def _run_copy(src_ref, src_row, dst_ref, dst_row, n_rows, sem):
    return pltpu.make_async_copy(src_ref.at[pl.ds(src_row, n_rows)], dst_ref.at[pl.ds(dst_row, n_rows)], sem)


def _for_each_run(tcnt_ref, tile, fn):
    def body(e, off):
        n = tcnt_ref[tile * N_EXPERTS + e]

        @pl.when(n > 0)
        def _():
            fn(e, off, n)

        return off + n

    lax.fori_loop(0, N_EXPERTS, body, 0)


def _dispatch_kernel(tcnt_ref, tdst_ref, ttot_ref, zrow_ref, zcnt_ref, nblk_ref, pos_ref, xp_ref, xs_ref, out_ref,
                     sorted_scr, zero_scr, sem, zsem):
    i = pl.program_id(0)
    n_prompt = pl.num_programs(0) - 1
    n_blocks = out_ref.shape[0] // MOE_BLOCK
    slot = i % 2

    def wait_tile(tile, s):
        _run_copy(out_ref, 0, sorted_scr.at[s], 0, ttot_ref[tile], sem.at[s]).wait()

    def for_each_pad(fn):
        def body(e, carry):
            n = zcnt_ref[e]

            @pl.when(n > 0)
            def _():
                fn(_run_copy(zero_scr, 0, out_ref, zrow_ref[e], n, zsem))

            dead = nblk_ref[0] + e

            @pl.when(dead < n_blocks)
            def _():
                fn(_run_copy(zero_scr, 0, out_ref, dead * MOE_BLOCK, MOE_BLOCK, zsem))

            return carry

        lax.fori_loop(0, N_EXPERTS, body, 0)

    @pl.when(i == 0)
    def _():
        zero_scr[...] = jnp.zeros_like(zero_scr)
        for_each_pad(lambda cp: cp.start())

    def sort_tile(x_ref):
        r_iota = lax.broadcasted_iota(jnp.int32, (TILE_ROWS, ROW_TILE), 0)
        hit = r_iota == pos_ref[0:1, :]
        for kk in range(1, TOP_K):
            hit = jnp.logical_or(hit, r_iota == pos_ref[kk:kk + 1, :])
        perm = jnp.where(hit, 1.0, 0.0).astype(BF16)
        xs = _dot(perm, x_ref[...])

        @pl.when(i >= 2)
        def _():
            wait_tile(i - 2, slot)

        _store_rows_as_tiles(sorted_scr.at[slot], xs)

    @pl.when(i < n_prompt)
    def _():
        sort_tile(xp_ref)

    @pl.when(i == n_prompt)
    def _():
        sort_tile(xs_ref)

    _for_each_run(tcnt_ref, i, lambda e, off, n: _run_copy(
        sorted_scr.at[slot], off, out_ref, tdst_ref[i * N_EXPERTS + e], n, sem.at[slot]).start())

    @pl.when(i == n_prompt)
    def _():
        wait_tile(i - 1, 1 - slot)
        wait_tile(i, slot)
        for_each_pad(lambda cp: cp.wait())


def _dispatch(tcnt, tdst, ttot, zrow, zcnt, nblk, pos, xn2_p, xn2_s, n_sorted):
    n_prompt = xn2_p.shape[0] // ROW_TILE
    grid_spec = pltpu.PrefetchScalarGridSpec(
        num_scalar_prefetch=6,
        grid=(n_prompt + 1,),
        in_specs=[
            pl.BlockSpec((TOP_K, ROW_TILE), lambda i, *_: (0, i)),
            pl.BlockSpec((ROW_TILE, D_MODEL), lambda i, *_: (jnp.minimum(i, n_prompt - 1), 0)),
            pl.BlockSpec((ROW_TILE, D_MODEL), lambda i, *_: (0, 0)),
        ],
        out_specs=pl.BlockSpec(memory_space=pl.ANY),
        scratch_shapes=[
            pltpu.VMEM((2, TILE_ROWS, ROW_CHUNKS, LANES), jnp.uint32),
            pltpu.VMEM((MOE_BLOCK, ROW_CHUNKS, LANES), jnp.uint32),
            pltpu.SemaphoreType.DMA((2,)), pltpu.SemaphoreType.DMA(()),
        ],
    )
    return pl.pallas_call(
        _dispatch_kernel,
        grid_spec=grid_spec,
        out_shape=jax.ShapeDtypeStruct((n_sorted, ROW_CHUNKS, LANES), jnp.uint32),
        compiler_params=pltpu.CompilerParams(dimension_semantics=("arbitrary",)),
        name="moe_dispatch",
    )(tcnt, tdst, ttot, zrow, zcnt, nblk, pos, xn2_p, xn2_s)


def _expert_kernel(be_ref, nblk_ref, eord_ref, enext_ref, bvalid_ref, x_ref, wgu_ref, bgu_ref, wd_ref, bd_ref, y_ref,
                   wgu_f32, wd_f32, wgu_bf, wd_bf, wsem):
    i = pl.program_id(0)
    live = i < nblk_ref[0]
    new_expert = jnp.logical_or(i == 0, be_ref[i] != be_ref[jnp.maximum(i - 1, 0)])

    def weight_copies(e, s):
        return (pltpu.make_async_copy(wgu_ref.at[e], wgu_f32.at[s], wsem.at[0, s]),
                pltpu.make_async_copy(wd_ref.at[e], wd_f32.at[s], wsem.at[1, s]))

    @pl.when(jnp.logical_and(live, new_expert))
    def _():
        slot = eord_ref[i] % 2

        @pl.when(i == 0)
        def _():
            for cp in weight_copies(be_ref[0], 0):
                cp.start(priority=1)

        for cp in weight_copies(be_ref[i], slot):
            cp.wait()

        @pl.when(enext_ref[i] >= 0)
        def _():
            for cp in weight_copies(enext_ref[i], 1 - slot):
                cp.start(priority=1)

        def cast(c, carry):
            rows = pl.ds(pl.multiple_of(c * WEIGHT_CAST_ROWS, WEIGHT_CAST_ROWS), WEIGHT_CAST_ROWS)
            wgu_bf[rows, :] = wgu_f32[slot, rows, :].astype(BF16)
            wd_bf[rows, :] = wd_f32[slot, rows, :].astype(BF16)
            return carry

        lax.fori_loop(0, D_MODEL // WEIGHT_CAST_ROWS, cast, 0)

    @pl.when(jnp.logical_not(live))
    def _():
        y_ref[...] = jnp.zeros_like(y_ref)

    def ffn(rows):
        x = _load_rows_from_tiles(x_ref.at[pl.ds(0, rows)], rows)
        h = _dot(x.astype(BF16), wgu_bf[...]) + bgu_ref[0]
        gate = jnp.minimum(h[:, :D_FF], SWIGLU_LIMIT)
        up = jnp.clip(h[:, D_FF:], -SWIGLU_LIMIT, SWIGLU_LIMIT)
        glu = gate * _sigmoid(gate * SWIGLU_ALPHA)
        y = _dot(((up + 1.0) * glu).astype(BF16), wd_bf[...]) + bd_ref[0]
        _store_rows_as_tiles(y_ref.at[pl.ds(0, rows)], y.astype(BF16).astype(F32))

    half = MOE_BLOCK // 2
    short = bvalid_ref[i] <= half

    @pl.when(jnp.logical_and(live, jnp.logical_not(short)))
    def _():
        ffn(MOE_BLOCK)

    @pl.when(jnp.logical_and(live, short))
    def _():
        ffn(half)
        y_ref[pl.ds(half, half)] = jnp.zeros((half,) + y_ref.shape[1:], y_ref.dtype)


def _experts(block_e, nblk, block_eord, block_enext, block_valid, x_sorted, wgu, bgu, wd, bd):
    n_blocks = x_sorted.shape[0] // MOE_BLOCK
    wmap = lambda i, be, *_: (be[i], 0, 0)
    rmap = lambda i, *_: (i, 0, 0)
    hbm = pl.BlockSpec(memory_space=pl.ANY)
    grid_spec = pltpu.PrefetchScalarGridSpec(
        num_scalar_prefetch=5,
        grid=(n_blocks,),
        in_specs=[
            pl.BlockSpec((MOE_BLOCK, ROW_CHUNKS, LANES), rmap),
            hbm,
            pl.BlockSpec((1, 1, 2 * D_FF), wmap),
            hbm,
            pl.BlockSpec((1, 1, D_MODEL), wmap),
        ],
        out_specs=pl.BlockSpec((MOE_BLOCK, ROW_CHUNKS, LANES), rmap),
        scratch_shapes=[
            pltpu.VMEM((2, D_MODEL, 2 * D_FF), F32), pltpu.VMEM((2, D_FF, D_MODEL), F32),
            pltpu.VMEM((D_MODEL, 2 * D_FF), BF16), pltpu.VMEM((D_FF, D_MODEL), BF16),
            pltpu.SemaphoreType.DMA((2, 2)),
        ],
    )
    return pl.pallas_call(
        _expert_kernel,
        grid_spec=grid_spec,
        out_shape=jax.ShapeDtypeStruct(x_sorted.shape, jnp.uint32),
        compiler_params=pltpu.CompilerParams(
            dimension_semantics=("arbitrary",), vmem_limit_bytes=VMEM_LIMIT),
        name="moe_experts",
    )(block_e, nblk, block_eord, block_enext, block_valid, x_sorted, wgu, bgu, wd, bd)


def _combine_kernel(tcnt_ref, tdst_ref, ttot_ref, pos_ref, gate_ref, ys_ref, hp_ref, hs_ref, nf_ref, yp_ref, ysmp_ref,
                    runs_scr, sem):
    i = pl.program_id(0)
    n_tiles = pl.num_programs(0)
    n_prompt = n_tiles - 1
    slot = i % 2

    def start_runs(tile, s):
        _for_each_run(tcnt_ref, tile, lambda e, off, n: _run_copy(
            ys_ref, tdst_ref[tile * N_EXPERTS + e], runs_scr.at[s], off, n, sem.at[s]).start())

    @pl.when(i == 0)
    def _():
        start_runs(0, 0)

    @pl.when(i + 1 < n_tiles)
    def _():
        start_runs(i + 1, 1 - slot)

    _run_copy(ys_ref, 0, runs_scr.at[slot], 0, ttot_ref[i], sem.at[slot]).wait()
    ys = _load_rows_from_tiles(runs_scr.at[slot], TILE_ROWS).astype(BF16)
    r_iota = lax.broadcasted_iota(jnp.int32, (TILE_ROWS, ROW_TILE), 0)
    gmat = jnp.zeros((TILE_ROWS, ROW_TILE), F32)
    for kk in range(TOP_K):
        gmat = jnp.where(r_iota == pos_ref[kk:kk + 1, :], gate_ref[kk:kk + 1, :], gmat)
    moe = lax.dot_general(gmat.astype(BF16), ys, (((0,), (0,)), ((), ())), preferred_element_type=F32)

    @pl.when(i < n_prompt)
    def _():
        yp_ref[...] = _rmsnorm(hp_ref[...] + moe, nf_ref[...])

    @pl.when(i == n_prompt)
    def _():
        ysmp_ref[...] = _rmsnorm(hs_ref[...] + moe, nf_ref[...])


def _combine(tcnt, tdst, ttot, pos, gates, y_sorted, h1_p, h1_s, nf):
    n_prompt = h1_p.shape[0] // ROW_TILE
    pmap = lambda i, *_: (jnp.minimum(i, n_prompt - 1), 0)
    smap = lambda i, *_: (0, 0)
    lmap = lambda i, *_: (0, i)
    grid_spec = pltpu.PrefetchScalarGridSpec(
        num_scalar_prefetch=3,
        grid=(n_prompt + 1,),
        in_specs=[
            pl.BlockSpec((TOP_K, ROW_TILE), lmap),
            pl.BlockSpec((TOP_K, ROW_TILE), lmap),
            pl.BlockSpec(memory_space=pl.ANY),
            pl.BlockSpec((ROW_TILE, D_MODEL), pmap),
            pl.BlockSpec((ROW_TILE, D_MODEL), smap),
            pl.BlockSpec((1, D_MODEL), smap),
        ],
        out_specs=[pl.BlockSpec((ROW_TILE, D_MODEL), pmap), pl.BlockSpec((ROW_TILE, D_MODEL), smap)],
        scratch_shapes=[pltpu.VMEM((2, TILE_ROWS, ROW_CHUNKS, LANES), jnp.uint32), pltpu.SemaphoreType.DMA((2,))],
    )
    return pl.pallas_call(
        _combine_kernel,
        grid_spec=grid_spec,
        out_shape=[jax.ShapeDtypeStruct(h1_p.shape, F32), jax.ShapeDtypeStruct(h1_s.shape, F32)],
        compiler_params=pltpu.CompilerParams(dimension_semantics=("arbitrary",)),
        name="moe_combine",
    )(tcnt, tdst, ttot, pos, gates, y_sorted, h1_p, h1_s, nf)


def _rotary_tables(pos):
    f = np.float32
    inv = np.power(f(ROPE_BASE), -np.arange(0, RET_DK, 2, dtype=f) / f(RET_DK)).astype(f)
    ang = (np.asarray(pos, f)[:, None] * inv[None, :]).astype(f)
    cos, sin = np.cos(ang).astype(f), np.sin(ang).astype(f)
    return np.concatenate([cos, cos], axis=1), np.concatenate([-sin, sin], axis=1)


def _decay_tables(chunk):
    f = np.float32
    log_g = np.log1p(-np.exp2(f(-5.0) - np.arange(RET_HEADS, dtype=f))).astype(f)
    i = np.arange(chunk, dtype=f)
    diff = i[:, None] - i[None, :]
    mask = np.where(diff[None] >= 0, np.exp(np.maximum(diff, f(0.0))[None] * log_g[:, None, None]), f(0.0)).astype(f)
    q_dec = np.exp((i + f(1.0))[None, :] * log_g[:, None]).astype(f)
    k_dec = np.exp((f(chunk) - f(1.0) - i)[None, :] * log_g[:, None]).astype(f)
    c_dec = np.exp(f(chunk) * log_g).astype(f)
    return mask, q_dec, k_dec, c_dec


def kernel(x_prompt, x_sample, state_ret, state_pool, meta_tokens, norm_mix, w_in, ret_gn, pool_w, pool_scale,
           w_ret_branch, w_pool_branch, w_out, norm_ffn, w_router, b_router, w_gate_up, b_gate_up, w_down, b_down,
           norm_final):
    batch, seq, _ = x_prompt.shape
    nb = x_sample.shape[0]
    past_len = 16384
    n_prompt_tok = batch * seq
    n_tok = n_prompt_tok + nb

    w_in_bf = w_in[0].astype(BF16)
    wts = (ret_gn[0][None, :], pool_w[0].astype(BF16), pool_scale[0][None, :],
           w_ret_branch[0].astype(BF16), w_pool_branch[0].astype(BF16), w_out[0].astype(BF16),
           norm_ffn[0][None, :], w_router[0].T.astype(BF16), b_router[0][:, None])
    wgu = w_gate_up[0]
    wd = w_down[0]
    bgu = b_gate_up[0][:, None, :]
    bd = b_down[0][:, None, :]
    nmix = norm_mix[0][None, :]

    cos_p, sin_p = _rotary_tables(N_META + np.arange(seq))
    cos_s, sin_s = _rotary_tables(np.concatenate([np.arange(N_META), np.full((nb,), past_len)]))
    mask, q_dec, k_dec, c_dec = _decay_tables(CHUNK)
    dec = (mask,
           np.ascontiguousarray(np.broadcast_to(q_dec[:, :, None], (RET_HEADS, CHUNK, RET_DK))),
           np.ascontiguousarray(np.broadcast_to(k_dec[:, :, None], (RET_HEADS, CHUNK, RET_DK))),
           np.ascontiguousarray(np.broadcast_to(c_dec[:, None, None], (RET_HEADS, 1, RET_DV))))
    m1, q1, k1, c1 = _decay_tables(1)
    sdec = np.stack([m1[:, 0, 0], q1[:, 0], k1[:, 0], c1], axis=1)

    x2d = x_prompt.reshape(n_prompt_tok, D_MODEL)
    xs2d = x_sample.reshape(nb, D_MODEL)
    x_small = jnp.concatenate([meta_tokens, xs2d], axis=0)
    proj_small = _inproj(x_small, nmix, w_in_bf, cos_s, sin_s, N_META + nb, 1, F32)
    proj_s = tuple(a[N_META:] for a in proj_small)

    grp = nb // (n_prompt_tok // INPROJ_TILE)

    def cols(a):
        return a.T.reshape(RET_QK, nb // grp, grp).transpose(1, 0, 2)

    *proj_p, st_s, o_s = _inproj(
        x2d, nmix, w_in_bf, cos_p, sin_p, INPROJ_TILE, batch, BF16,
        sample=(sdec, cols(proj_s[0]), cols(proj_s[1]), proj_s[0], proj_s[1], proj_s[2], state_ret[0]))

    lead = CHUNK - N_META
    kmeta = jnp.pad(proj_small[1][:N_META], ((lead, 0), (0, 0))).astype(BF16)
    vmeta = jnp.pad(proj_small[2][:N_META], ((lead, 0), (0, 0))).astype(BF16)
    pmeta = proj_small[4][:N_META]

    pool2d = state_pool[0].reshape(nb, POOL_BUF * POOL_WIDTH)
    (pool_s, h1_s, xn2_s, idx_s, gate_s, rank_s, cnt_s) = _sample_mixer(o_s, proj_s, xs2d, pool2d, wts)

    (h1_p, xn2_p, idx_p, gate_p, rank_p, cnt, s_fin, p_fin) = _mixer(
        proj_p, x2d, kmeta, vmeta, pmeta, dec, wts, cnt_s, batch, seq)

    assert n_prompt_tok % ROW_TILE == 0 and nb <= ROW_TILE
    n_phantom = ROW_TILE - nb
    n_tiles = n_prompt_tok // ROW_TILE + 1
    i32 = jnp.int32
    counts = cnt[:, 0].astype(i32)
    padded = ((counts + MOE_BLOCK - 1) // MOE_BLOCK) * MOE_BLOCK
    pad_end = jnp.cumsum(padded)
    pad_start = pad_end - padded
    n_blocks = (n_tok * TOP_K) // MOE_BLOCK + N_EXPERTS
    block_row = jnp.arange(n_blocks, dtype=i32) * MOE_BLOCK
    block_e = jnp.minimum(jnp.sum((pad_end[None, :] <= block_row[:, None]).astype(i32), axis=1), N_EXPERTS - 1)
    nblk = (pad_end[-1:] // MOE_BLOCK).astype(i32)
    e_row = jnp.arange(N_EXPERTS, dtype=i32)
    used = padded > 0
    e_ord = jnp.cumsum(used.astype(i32)) - 1
    later_used = jnp.logical_and(e_row[None, :] > e_row[:, None], used[None, :])
    e_next = jnp.min(jnp.where(later_used, e_row[None, :], N_EXPERTS), axis=1)
    e_next = jnp.where(e_next == N_EXPERTS, -1, e_next)
    of_block = block_e[:, None] == e_row[None, :]
    block_eord = jnp.sum(jnp.where(of_block, e_ord[None, :], 0), axis=1)
    block_enext = jnp.sum(jnp.where(of_block, e_next[None, :], 0), axis=1)
    block_left = jnp.sum(jnp.where(of_block, (pad_start + counts)[None, :], 0), axis=1) - block_row
    block_valid = jnp.clip(block_left, 0, MOE_BLOCK)
    phantom = lambda fill, dt: jnp.full((TOP_K, n_phantom), fill, dt)
    idx = jnp.concatenate([idx_p, idx_s, phantom(-1, i32)], axis=1)
    rank = jnp.concatenate([rank_p, rank_s, phantom(0, i32)], axis=1)
    gates = jnp.concatenate([gate_p, gate_s, phantom(0.0, F32)], axis=1)
    onehot = idx[None] == jnp.arange(N_EXPERTS, dtype=i32)[:, None, None]
    tile_cnt = jnp.sum(onehot.reshape(N_EXPERTS, TOP_K, n_tiles, ROW_TILE).astype(i32), axis=(1, 3)).T
    by_time = jnp.concatenate([tile_cnt[-1:], tile_cnt[:-1]], axis=0)
    before_time = jnp.cumsum(by_time, axis=0) - by_time
    run_before = jnp.concatenate([before_time[1:], before_time[:1]], axis=0)
    tile_off = jnp.cumsum(tile_cnt, axis=1) - tile_cnt
    tile_dst = pad_start[None, :] + run_before
    delta = jnp.repeat((tile_off - run_before).T, ROW_TILE, axis=1)
    pos = rank + jnp.sum(jnp.where(onehot, delta[:, None, :], 0), axis=0)
    pos = jnp.where(idx >= 0, pos, -1)
    tcnt, tdst, ttot = tile_cnt.reshape(-1), tile_dst.reshape(-1), jnp.sum(tile_cnt, axis=1)

    tail = ((0, n_phantom), (0, 0))
    x_sorted = _dispatch(tcnt, tdst, ttot, pad_start + counts, padded - counts, nblk, pos, xn2_p,
                         jnp.pad(xn2_s, tail), n_blocks * MOE_BLOCK)
    y_sorted = _experts(block_e, nblk, block_eord, block_enext, block_valid, x_sorted, wgu, bgu, wd, bd)
    y_p, y_s = _combine(tcnt, tdst, ttot, pos, gates, y_sorted, h1_p, jnp.pad(h1_s, tail), norm_final[None, :])

    y_prompt = y_p.reshape(batch, seq, D_MODEL)
    y_sample = y_s[:nb].reshape(nb, 1, D_MODEL)
    ret_state_prompt = s_fin[None]
    pool_state_prompt = p_fin[:, 1:, :][None]
    ret_state_sample = st_s[None]
    pool_state_sample = pool_s.reshape(nb, POOL_BUF, POOL_WIDTH)[None]
    return (y_prompt, y_sample, ret_state_prompt, pool_state_prompt, ret_state_sample, pool_state_sample)
```

```python
import functools

import jax
import jax.numpy as jnp
import numpy as np
from jax import lax
from jax.experimental import pallas as pl
from jax.experimental.pallas import tpu as pltpu

F32 = jnp.float32
BF16 = jnp.bfloat16

D_MODEL = 1024
N_META = 16
RET_HEADS = 4
RET_DK = 128
RET_DV = 256
RET_QK = RET_HEADS * RET_DK
RET_V = RET_HEADS * RET_DV
CHUNK = 128
ROPE_BASE = 10000.0
POOL_WINDOWS = (2, 4, 8, 16)
POOL_GROUPS = 4
POOL_GROUP_DIM = 128
POOL_WIDTH = POOL_GROUPS * POOL_GROUP_DIM
POOL_BUF = max(POOL_WINDOWS) - 1
N_EXPERTS = 32
TOP_K = 4
D_FF = D_MODEL
SWIGLU_LIMIT = 7.0
SWIGLU_ALPHA = 1.702
EPS = 1e-6
IN_WIDTHS = (RET_QK, RET_QK, RET_V, RET_V, POOL_WIDTH, D_MODEL, D_MODEL)
IN_TOTAL = sum(IN_WIDTHS)
IN_OFFS = tuple(int(s) for s in np.cumsum((0,) + IN_WIDTHS))

LANES = 128
ROW_CHUNKS = D_MODEL // (2 * LANES)
INPROJ_TILE = 512
MIXER_TILE = 512
MOE_BLOCK = 512
ROW_TILE = 256
SUBLANES = 8
WEIGHT_CAST_ROWS = 128
VMEM_LIMIT = 56 * 1024 * 1024

assert N_META + 1 >= max(POOL_WINDOWS)
assert POOL_WINDOWS == (2, 4, 8, 16)


def _dot(a, b):
    return jnp.dot(a, b, preferred_element_type=F32)


def _rmsnorm(x, w):
    return x * lax.rsqrt(jnp.mean(x * x, axis=-1, keepdims=True) + EPS) * w


def _sigmoid(x):
    return 0.5 * jnp.tanh(0.5 * x) + 0.5


def _store_rows_as_tiles(ref, x):
    half = D_MODEL // 2
    hi = pltpu.bitcast(x[:, :half], jnp.uint32) & jnp.uint32(0xFFFF0000)
    lo = pltpu.bitcast(x[:, half:], jnp.uint32) >> 16
    ref[...] = (hi | lo).reshape(x.shape[0], ROW_CHUNKS, LANES)


def _load_rows_from_tiles(ref, rows):
    w = ref[...].reshape(rows, D_MODEL // 2)
    hi = pltpu.bitcast(w & jnp.uint32(0xFFFF0000), F32)
    lo = pltpu.bitcast(w << 16, F32)
    return jnp.concatenate([hi, lo], axis=1)


def _inproj_kernel(x_ref, nw_ref, w_ref, cos_ref, sin_ref,
                   q_ref, k_ref, v_ref, g_ref, p_ref, ga_ref, gb_ref):
    xn = _rmsnorm(x_ref[...], nw_ref[...]).astype(BF16)
    cos = cos_ref[...]
    sin = sin_ref[...]

    def seg(i):
        return _dot(xn, w_ref[:, IN_OFFS[i]:IN_OFFS[i + 1]])

    def rot(a):
        return a * cos + pltpu.roll(a, RET_DK // 2, 1) * sin

    q = seg(0)
    k = seg(1)
    for h in range(RET_HEADS):
        sl = slice(h * RET_DK, (h + 1) * RET_DK)
        q_ref[:, sl] = rot(q[:, sl]).astype(q_ref.dtype)
        k_ref[:, sl] = (rot(k[:, sl]) * (RET_DK ** -0.5)).astype(k_ref.dtype)
    v_ref[...] = seg(2).astype(v_ref.dtype)
    g = seg(3)
    g_ref[...] = (g * _sigmoid(g)).astype(g_ref.dtype)
    p_ref[...] = seg(4)
    ga_ref[...] = _sigmoid(seg(5)).astype(ga_ref.dtype)
    gb_ref[...] = _sigmoid(seg(6)).astype(gb_ref.dtype)


def _sample_state_step(step, sdec_ref, qt_ref, kt_ref, q_ref, k_ref, v_ref, st_ref, stout_ref, o_ref):
    grp = st_ref.shape[0]
    row0 = step * grp
    if grp % SUBLANES == 0:
        rows = pl.ds(pl.multiple_of(row0, SUBLANES), grp)
        q8, k8, v8 = q_ref[rows, :], k_ref[rows, :], v_ref[rows, :]
    else:
        assert 2 * grp == SUBLANES
        rows = pl.ds(pl.multiple_of((step // 2) * SUBLANES, SUBLANES), SUBLANES)
        first = step % 2 == 0
        q8, k8, v8 = (jnp.where(first, a[:grp], a[grp:]) for a in (q_ref[rows, :], k_ref[rows, :], v_ref[rows, :]))
    for h in range(RET_HEADS):
        ksl = slice(h * RET_DK, (h + 1) * RET_DK)
        vsl = slice(h * RET_DV, (h + 1) * RET_DV)
        score = jnp.sum(q8[:, ksl] * k8[:, ksl], axis=1, keepdims=True) * sdec_ref[h, 0]
        intra = score * v8[:, vsl]
        for bb in range(grp):
            s_old = st_ref[bb, h]
            qcol = qt_ref[0, ksl, bb:bb + 1] * sdec_ref[h, 1]
            kcol = kt_ref[0, ksl, bb:bb + 1] * sdec_ref[h, 2]
            cross = jnp.sum(s_old * qcol, axis=0, keepdims=True)
            o_ref[pl.ds(row0 + bb, 1), vsl] = intra[bb:bb + 1, :] + cross
            stout_ref[bb, h] = s_old * sdec_ref[h, 3] + kcol * v8[bb:bb + 1, vsl]


def _inproj_sample_kernel(x_ref, nw_ref, w_ref, cos_ref, sin_ref,
                          sdec_ref, qt_ref, kt_ref, qs_ref, ks_ref, vs_ref, st_ref,
                          q_ref, k_ref, v_ref, g_ref, p_ref, ga_ref, gb_ref, stout_ref, os_ref):
    _inproj_kernel(x_ref, nw_ref, w_ref, cos_ref, sin_ref, q_ref, k_ref, v_ref, g_ref, p_ref, ga_ref, gb_ref)
    step = pl.program_id(0) * pl.num_programs(1) + pl.program_id(1)
    _sample_state_step(step, sdec_ref, qt_ref, kt_ref, qs_ref, ks_ref, vs_ref, st_ref, stout_ref, os_ref)


def _inproj(x2d, nw, w_in_bf, cosf, sinf, tile, n_outer, act_dtype, sample=None):
    rows = x2d.shape[0]
    n_inner = rows // (tile * n_outer)
    row_map = lambda b, j: (b * n_inner + j, 0)
    tab_map = lambda b, j: (j, 0)
    const = lambda b, j: (0, 0)
    widths = IN_WIDTHS
    dts = (act_dtype, act_dtype, act_dtype, act_dtype, F32, act_dtype, act_dtype)
    in_arrays = [x2d, nw, w_in_bf, cosf, sinf]
    in_specs = [
        pl.BlockSpec((tile, D_MODEL), row_map),
        pl.BlockSpec((1, D_MODEL), const),
        pl.BlockSpec((D_MODEL, IN_TOTAL), const, pipeline_mode=pl.Buffered(1)),
        pl.BlockSpec((tile, RET_DK), tab_map),
        pl.BlockSpec((tile, RET_DK), tab_map),
    ]
    out_specs = [pl.BlockSpec((tile, w), row_map) for w in widths]
    out_shape = [jax.ShapeDtypeStruct((rows, w), dt) for w, dt in zip(widths, dts)]
    body = _inproj_kernel
    if sample is not None:
        sdec, qt, kt, qs, ks, vs, state = sample
        nb = state.shape[0]
        grp = nb // (n_outer * n_inner)
        assert grp * n_outer * n_inner == nb and qt.shape == (nb // grp, RET_QK, grp)
        step_map = lambda b, j: (b * n_inner + j, 0, 0)
        st_spec = pl.BlockSpec((grp, RET_HEADS, RET_DK, RET_DV), lambda b, j: (b * n_inner + j, 0, 0, 0))
        in_arrays += [sdec, qt, kt, qs, ks, vs, state]
        in_specs += [pl.BlockSpec(memory_space=pltpu.SMEM),
                     pl.BlockSpec((1, RET_QK, grp), step_map), pl.BlockSpec((1, RET_QK, grp), step_map),
                     pl.BlockSpec(qs.shape, const), pl.BlockSpec(ks.shape, const), pl.BlockSpec(vs.shape, const),
                     st_spec]
        out_specs += [st_spec, pl.BlockSpec((nb, RET_V), const)]
        out_shape += [jax.ShapeDtypeStruct(state.shape, F32), jax.ShapeDtypeStruct((nb, RET_V), F32)]
        body = _inproj_sample_kernel
    return pl.pallas_call(
        body,
        grid=(n_outer, n_inner),
        in_specs=in_specs,
        out_specs=out_specs,
        out_shape=out_shape,
        compiler_params=pltpu.CompilerParams(
            dimension_semantics=("arbitrary", "arbitrary"), vmem_limit_bytes=VMEM_LIMIT),
        name="inproj",
    )(*in_arrays)


def _group_norm(o, gn_row):
    mu = jnp.mean(o, axis=-1, keepdims=True)
    var = jnp.mean(jnp.square(o - mu), axis=-1, keepdims=True)
    return (o - mu) * lax.rsqrt(var + EPS) * gn_row


def _pool_branch(groups, poolw_ref, pscale_ref, wpool_ref):
    pm = [_dot(g.astype(BF16), poolw_ref[i]) for i, g in enumerate(groups)]
    pm = jnp.concatenate(pm, axis=1) * pscale_ref[...]
    return _dot(pm.astype(BF16), wpool_ref[...])


def _merge_tail(o_norm, silu_g, sig_a, sig_b, yb, x, wret_ref, wout_ref):
    ya = _dot((silu_g.astype(F32) * o_norm).astype(BF16), wret_ref[...])
    merged = sig_a.astype(F32) * ya + sig_b.astype(F32) * yb
    return x + _dot(merged.astype(BF16), wout_ref[...])


def _route(h1, nffn_ref, wrt_ref, br_ref, run_scr,
           xn2_ref, idx_ref, gate_ref, rank_ref):
    tm = h1.shape[0]
    xn2 = _rmsnorm(h1, nffn_ref[...]).astype(BF16)
    xn2_ref[...] = xn2
    logits = lax.dot_general(wrt_ref[...], xn2, (((1,), (1,)), ((), ())),
                             preferred_element_type=F32) + br_ref[...]
    e_iota = lax.broadcasted_iota(jnp.int32, (N_EXPERTS, tm), 0)
    work = logits
    vals, sels = [], []
    chosen = jnp.zeros((N_EXPERTS, tm), F32)
    for _ in range(TOP_K):
        m = jnp.max(work, axis=0, keepdims=True)
        sel = jnp.min(jnp.where(work == m, e_iota, N_EXPERTS), axis=0, keepdims=True)
        hit = e_iota == sel
        vals.append(m)
        sels.append(sel)
        chosen = jnp.where(hit, 1.0, chosen)
        work = jnp.where(hit, -jnp.inf, work)
    exps = [jnp.exp(v - vals[0]) for v in vals]
    denom = exps[0] + exps[1] + exps[2] + exps[3]
    gates = [e / denom for e in exps]
    r_i = lax.broadcasted_iota(jnp.int32, (tm, tm), 0)
    c_i = lax.broadcasted_iota(jnp.int32, (tm, tm), 1)
    before = jnp.where(r_i < c_i, 1.0, 0.0).astype(BF16)
    base = run_scr[...] + _dot(chosen.astype(BF16), before)
    for kk in range(TOP_K):
        rk = jnp.sum(jnp.where(e_iota == sels[kk], base, 0.0), axis=0, keepdims=True)
        rank_ref[kk:kk + 1, :] = rk.astype(jnp.int32)
        idx_ref[kk:kk + 1, :] = sels[kk]
        gate_ref[kk:kk + 1, :] = gates[kk]
    run_scr[...] = run_scr[...] + jnp.sum(chosen, axis=1, keepdims=True)


def _mixer_kernel(q_ref, k_ref, v_ref, g_ref, p_ref, ga_ref, gb_ref, x_ref,
                  kmeta_ref, vmeta_ref, pmeta_ref, mask_ref, qdec_ref, kdec_ref, cdec_ref, gn_ref,
                  poolw_ref, pscale_ref, wret_ref, wpool_ref, wout_ref, nffn_ref, wrt_ref, br_ref,
                  cnt0_ref,
                  h1_ref, xn2_ref, idx_ref, gate_ref, rank_ref, cnt_ref, sfin_ref, pfin_ref,
                  s_scr, ext_scr, o_scr, run_scr):
    b = pl.program_id(0)
    j = pl.program_id(1)
    nj = pl.num_programs(1)
    tm = q_ref.shape[0]

    def state_update(s_old, kc, vc, h):
        kd = (kc.astype(F32) * kdec_ref[h]).astype(BF16)
        upd = lax.dot_general(kd, vc, (((0,), (0,)), ((), ())), preferred_element_type=F32)
        return s_old * cdec_ref[h] + upd

    @pl.when(jnp.logical_and(b == 0, j == 0))
    def _():
        run_scr[...] = cnt0_ref[:, 0:1]

    @pl.when(j == 0)
    def _():
        for h in range(RET_HEADS):
            kc = kmeta_ref[:, h * RET_DK:(h + 1) * RET_DK]
            vc = vmeta_ref[:, h * RET_DV:(h + 1) * RET_DV]
            s_scr[h] = state_update(jnp.zeros((RET_DK, RET_DV), F32), kc, vc, h)
        ext_scr[0:N_META, :] = pmeta_ref[...]

    n_chunks = tm // CHUNK
    units = [(c, h) for c in range(n_chunks) for h in range(RET_HEADS)]

    def operands(c, h):
        rows = slice(c * CHUNK, (c + 1) * CHUNK)
        return (q_ref[rows, h * RET_DK:(h + 1) * RET_DK], k_ref[rows, h * RET_DK:(h + 1) * RET_DK],
                v_ref[rows, h * RET_DV:(h + 1) * RET_DV])

    lhs, upd = {}, {}
    for c, h in units:
        qc, kc, vc = operands(c, h)
        scores = lax.dot_general(qc, kc, (((1,), (1,)), ((), ())), preferred_element_type=F32) * mask_ref[h]
        qd = (qc.astype(F32) * qdec_ref[h]).astype(BF16)
        lhs[c, h] = jnp.concatenate([scores.astype(BF16), qd], axis=1)
        kd = (kc.astype(F32) * kdec_ref[h]).astype(BF16)
        upd[c, h] = lax.dot_general(kd, vc, (((0,), (0,)), ((), ())), preferred_element_type=F32)
    state = {}
    for h in range(RET_HEADS):
        s = s_scr[h]
        for c in range(n_chunks):
            state[c, h] = s
            s = s * cdec_ref[h] + upd[c, h]
        s_scr[h] = s
    for c, h in units:
        rhs = jnp.concatenate([operands(c, h)[2], state[c, h].astype(BF16)], axis=0)
        o = _dot(lhs[c, h], rhs)
        o_scr[c * CHUNK:(c + 1) * CHUNK, h * RET_DV:(h + 1) * RET_DV] = _group_norm(
            o, gn_ref[:, h * RET_DV:(h + 1) * RET_DV])

    p = p_ref[...]
    ext_scr[N_META:N_META + tm, :] = p
    a = ext_scr[...]
    g1 = POOL_GROUP_DIM
    s2 = a + pltpu.roll(a, 1, 0)
    s4 = s2[:, g1:] + pltpu.roll(s2[:, g1:], 2, 0)
    s8 = s4[:, g1:] + pltpu.roll(s4[:, g1:], 4, 0)
    s16 = s8[:, g1:] + pltpu.roll(s8[:, g1:], 8, 0)
    sums = (s2[N_META:, :g1], s4[N_META:, :g1], s8[N_META:, :g1], s16[N_META:, :])
    groups = [sums[i] * (1.0 / POOL_WINDOWS[i]) - p[:, i * g1:(i + 1) * g1] for i in range(POOL_GROUPS)]
    ext_scr[0:N_META, :] = ext_scr[tm:tm + N_META, :]

    yb = _pool_branch(groups, poolw_ref, pscale_ref, wpool_ref)
    h1 = _merge_tail(o_scr[...], g_ref[...], ga_ref[...], gb_ref[...], yb, x_ref[...], wret_ref, wout_ref)
    h1_ref[...] = h1
    _route(h1, nffn_ref, wrt_ref, br_ref, run_scr, xn2_ref, idx_ref, gate_ref, rank_ref)
    cnt_ref[...] = jnp.broadcast_to(run_scr[...], cnt_ref.shape)

    @pl.when(j == nj - 1)
    def _():
        for h in range(RET_HEADS):
            sfin_ref[0, h] = s_scr[h]
        pfin_ref[0] = ext_scr[0:N_META, :]


def _mixer(proj, x2d, kmeta, vmeta, pmeta, dec, wts, cnt0, batch, seq):
    q, k, v, g, p, ga, gb = proj
    tm = MIXER_TILE
    nj = seq // tm
    rows = batch * seq
    row_map = lambda b, j: (b * nj + j, 0)
    lane_map = lambda b, j: (0, b * nj + j)
    c2 = lambda b, j: (0, 0)
    c3 = lambda b, j: (0, 0, 0)

    def whole(a):
        return pl.BlockSpec(a.shape, c2 if a.ndim == 2 else c3)

    mask, qdec, kdec, cdec = dec
    gn, poolw, pscale, wret, wpool, wout, nffn, wrt, br = wts
    in_arrays = [q, k, v, g, p, ga, gb, x2d, kmeta, vmeta, pmeta, mask, qdec, kdec, cdec, gn,
                 poolw, pscale, wret, wpool, wout, nffn, wrt, br, cnt0]
    in_specs = [pl.BlockSpec((tm, a.shape[1]), row_map) for a in in_arrays[:8]]
    in_specs += [whole(a) for a in in_arrays[8:]]
    out_shape = [
        jax.ShapeDtypeStruct((rows, D_MODEL), F32),
        jax.ShapeDtypeStruct((rows, D_MODEL), BF16),
        jax.ShapeDtypeStruct((TOP_K, rows), jnp.int32),
        jax.ShapeDtypeStruct((TOP_K, rows), F32),
        jax.ShapeDtypeStruct((TOP_K, rows), jnp.int32),
        jax.ShapeDtypeStruct((N_EXPERTS, LANES), F32),
        jax.ShapeDtypeStruct((batch, RET_HEADS, RET_DK, RET_DV), F32),
        jax.ShapeDtypeStruct((batch, N_META, POOL_WIDTH), F32),
    ]
    out_specs = [
        pl.BlockSpec((tm, D_MODEL), row_map),
        pl.BlockSpec((tm, D_MODEL), row_map),
        pl.BlockSpec((TOP_K, tm), lane_map),
        pl.BlockSpec((TOP_K, tm), lane_map),
        pl.BlockSpec((TOP_K, tm), lane_map),
        pl.BlockSpec((N_EXPERTS, LANES), c2),
        pl.BlockSpec((1, RET_HEADS, RET_DK, RET_DV), lambda b, j: (b, 0, 0, 0)),
        pl.BlockSpec((1, N_META, POOL_WIDTH), lambda b, j: (b, 0, 0)),
    ]
    return pl.pallas_call(
        _mixer_kernel,
        grid=(batch, nj),
        in_specs=in_specs,
        out_specs=out_specs,
        out_shape=out_shape,
        scratch_shapes=[
            pltpu.VMEM((RET_HEADS, RET_DK, RET_DV), F32),
            pltpu.VMEM((N_META + tm, POOL_WIDTH), F32),
            pltpu.VMEM((tm, RET_V), F32),
            pltpu.VMEM((N_EXPERTS, 1), F32),
        ],
        compiler_params=pltpu.CompilerParams(
            dimension_semantics=("arbitrary", "arbitrary"), vmem_limit_bytes=VMEM_LIMIT),
        name="mixer",
    )(*in_arrays)


def _sample_kernel(o_ref, g_ref, p_ref, ga_ref, gb_ref, x_ref, pool_ref, gn_ref,
                   poolw_ref, pscale_ref, wret_ref, wpool_ref, wout_ref, nffn_ref, wrt_ref, br_ref,
                   poolout_ref, h1_ref, xn2_ref, idx_ref, gate_ref, rank_ref, cnt_ref,
                   run_scr):
    run_scr[...] = jnp.zeros_like(run_scr)
    o = o_ref[...]
    o_norm = jnp.concatenate(
        [_group_norm(o[:, h * RET_DV:(h + 1) * RET_DV], gn_ref[:, h * RET_DV:(h + 1) * RET_DV])
         for h in range(RET_HEADS)], axis=1)
    p = p_ref[...]
    w = POOL_WIDTH
    g1 = POOL_GROUP_DIM

    def prev(r, lo):
        return pool_ref[:, r * w + lo:(r + 1) * w]

    s2 = p + prev(14, 0)
    s4 = s2[:, g1:] + prev(13, g1) + prev(12, g1)
    s8 = s4[:, g1:] + prev(11, 2 * g1) + prev(10, 2 * g1) + prev(9, 2 * g1) + prev(8, 2 * g1)
    s16 = s8[:, g1:]
    for r in range(7, -1, -1):
        s16 = s16 + prev(r, 3 * g1)
    sums = (s2[:, :g1], s4[:, :g1], s8[:, :g1], s16)
    groups = [sums[t] * (1.0 / POOL_WINDOWS[t]) - p[:, t * g1:(t + 1) * g1] for t in range(POOL_GROUPS)]
    poolout_ref[:, 0:(POOL_BUF - 1) * w] = pool_ref[:, w:POOL_BUF * w]
    poolout_ref[:, (POOL_BUF - 1) * w:] = p
    yb = _pool_branch(groups, poolw_ref, pscale_ref, wpool_ref)
    h1 = _merge_tail(o_norm, g_ref[...], ga_ref[...], gb_ref[...], yb, x_ref[...], wret_ref, wout_ref)
    h1_ref[...] = h1
    _route(h1, nffn_ref, wrt_ref, br_ref, run_scr, xn2_ref, idx_ref, gate_ref, rank_ref)
    cnt_ref[...] = jnp.broadcast_to(run_scr[...], cnt_ref.shape)


def _sample_mixer(o_s, proj, x2d, pool2d, wts):
    _, _, _, g, p, ga, gb = proj
    nb = x2d.shape[0]
    c2 = lambda i: (0, 0)

    def whole(a):
        return pl.BlockSpec(a.shape, c2)

    gn, poolw, pscale, wret, wpool, wout, nffn, wrt, br = wts
    in_arrays = [o_s, g, p, ga, gb, x2d, pool2d, gn, poolw, pscale, wret, wpool, wout, nffn, wrt, br]
    in_specs = [whole(a) for a in in_arrays[:8]]
    in_specs += [pl.BlockSpec(poolw.shape, lambda i: (0, 0, 0))]
    in_specs += [whole(a) for a in in_arrays[9:]]
    out_shape = [
        jax.ShapeDtypeStruct(pool2d.shape, F32),
        jax.ShapeDtypeStruct((nb, D_MODEL), F32),
        jax.ShapeDtypeStruct((nb, D_MODEL), BF16),
        jax.ShapeDtypeStruct((TOP_K, nb), jnp.int32),
        jax.ShapeDtypeStruct((TOP_K, nb), F32),
        jax.ShapeDtypeStruct((TOP_K, nb), jnp.int32),
        jax.ShapeDtypeStruct((N_EXPERTS, LANES), F32),
    ]
    return pl.pallas_call(
        _sample_kernel,
        grid=(1,),
        in_specs=in_specs,
        out_specs=[pl.BlockSpec(s.shape, c2) for s in out_shape],
        out_shape=out_shape,
        scratch_shapes=[pltpu.VMEM((N_EXPERTS, 1), F32)],
        compiler_params=pltpu.CompilerParams(
            dimension_semantics=("arbitrary",), vmem_limit_bytes=VMEM_LIMIT),
        name="sample_mixer",
    )(*in_arrays)


TILE_ROWS = ROW_TILE * TOP_K


def _run_copy(src_ref, src_row, dst_ref, dst_row, n_rows, sem):
    return pltpu.make_async_copy(src_ref.at[pl.ds(src_row, n_rows)], dst_ref.at[pl.ds(dst_row, n_rows)], sem)


def _for_each_run(tcnt_ref, tile, fn):
    def body(e, off):
        n = tcnt_ref[tile * N_EXPERTS + e]

        @pl.when(n > 0)
        def _():
            fn(e, off, n)

        return off + n

    lax.fori_loop(0, N_EXPERTS, body, 0)


def _dispatch_kernel(tcnt_ref, tdst_ref, ttot_ref, zrow_ref, zcnt_ref, nblk_ref, pos_ref, xp_ref, xs_ref, out_ref,
                     sorted_scr, zero_scr, sem, zsem):
    i = pl.program_id(0)
    n_prompt = pl.num_programs(0) - 1
    n_blocks = out_ref.shape[0] // MOE_BLOCK
    slot = i % 2

    def wait_tile(tile, s):
        _run_copy(out_ref, 0, sorted_scr.at[s], 0, ttot_ref[tile], sem.at[s]).wait()

    def for_each_pad(fn):
        def body(e, carry):
            n = zcnt_ref[e]

            @pl.when(n > 0)
            def _():
                fn(_run_copy(zero_scr, 0, out_ref, zrow_ref[e], n, zsem))

            dead = nblk_ref[0] + e

            @pl.when(dead < n_blocks)
            def _():
                fn(_run_copy(zero_scr, 0, out_ref, dead * MOE_BLOCK, MOE_BLOCK, zsem))

            return carry

        lax.fori_loop(0, N_EXPERTS, body, 0)

    @pl.when(i == 0)
    def _():
        zero_scr[...] = jnp.zeros_like(zero_scr)
        for_each_pad(lambda cp: cp.start())

    def sort_tile(x_ref):
        r_iota = lax.broadcasted_iota(jnp.int32, (TILE_ROWS, ROW_TILE), 0)
        hit = r_iota == pos_ref[0:1, :]
        for kk in range(1, TOP_K):
            hit = jnp.logical_or(hit, r_iota == pos_ref[kk:kk + 1, :])
        perm = jnp.where(hit, 1.0, 0.0).astype(BF16)
        xs = _dot(perm, x_ref[...])

        @pl.when(i >= 2)
        def _():
            wait_tile(i - 2, slot)

        _store_rows_as_tiles(sorted_scr.at[slot], xs)

    @pl.when(i < n_prompt)
    def _():
        sort_tile(xp_ref)

    @pl.when(i == n_prompt)
    def _():
        sort_tile(xs_ref)

    _for_each_run(tcnt_ref, i, lambda e, off, n: _run_copy(
        sorted_scr.at[slot], off, out_ref, tdst_ref[i * N_EXPERTS + e], n, sem.at[slot]).start())

    @pl.when(i == n_prompt)
    def _():
        wait_tile(i - 1, 1 - slot)
        wait_tile(i, slot)
        for_each_pad(lambda cp: cp.wait())


def _dispatch(tcnt, tdst, ttot, zrow, zcnt, nblk, pos, xn2_p, xn2_s, n_sorted):
    n_prompt = xn2_p.shape[0] // ROW_TILE
    grid_spec = pltpu.PrefetchScalarGridSpec(
        num_scalar_prefetch=6,
        grid=(n_prompt + 1,),
        in_specs=[
            pl.BlockSpec((TOP_K, ROW_TILE), lambda i, *_: (0, i)),
            pl.BlockSpec((ROW_TILE, D_MODEL), lambda i, *_: (jnp.minimum(i, n_prompt - 1), 0)),
            pl.BlockSpec((ROW_TILE, D_MODEL), lambda i, *_: (0, 0)),
        ],
        out_specs=pl.BlockSpec(memory_space=pl.ANY),
        scratch_shapes=[
            pltpu.VMEM((2, TILE_ROWS, ROW_CHUNKS, LANES), jnp.uint32),
            pltpu.VMEM((MOE_BLOCK, ROW_CHUNKS, LANES), jnp.uint32),
            pltpu.SemaphoreType.DMA((2,)), pltpu.SemaphoreType.DMA(()),
        ],
    )
    return pl.pallas_call(
        _dispatch_kernel,
        grid_spec=grid_spec,
        out_shape=jax.ShapeDtypeStruct((n_sorted, ROW_CHUNKS, LANES), jnp.uint32),
        compiler_params=pltpu.CompilerParams(dimension_semantics=("arbitrary",)),
        name="moe_dispatch",
    )(tcnt, tdst, ttot, zrow, zcnt, nblk, pos, xn2_p, xn2_s)


def _expert_kernel(be_ref, nblk_ref, eord_ref, enext_ref, bvalid_ref, x_ref, wgu_ref, bgu_ref, wd_ref, bd_ref, y_ref,
                   wgu_f32, wd_f32, wgu_bf, wd_bf, wsem):
    i = pl.program_id(0)
    live = i < nblk_ref[0]
    new_expert = jnp.logical_or(i == 0, be_ref[i] != be_ref[jnp.maximum(i - 1, 0)])

    def weight_copies(e, s):
        return (pltpu.make_async_copy(wgu_ref.at[e], wgu_f32.at[s], wsem.at[0, s]),
                pltpu.make_async_copy(wd_ref.at[e], wd_f32.at[s], wsem.at[1, s]))

    @pl.when(jnp.logical_and(live, new_expert))
    def _():
        slot = eord_ref[i] % 2

        @pl.when(i == 0)
        def _():
            for cp in weight_copies(be_ref[0], 0):
                cp.start(priority=1)

        for cp in weight_copies(be_ref[i], slot):
            cp.wait()

        @pl.when(enext_ref[i] >= 0)
        def _():
            for cp in weight_copies(enext_ref[i], 1 - slot):
                cp.start(priority=1)

        def cast(c, carry):
            rows = pl.ds(pl.multiple_of(c * WEIGHT_CAST_ROWS, WEIGHT_CAST_ROWS), WEIGHT_CAST_ROWS)
            wgu_bf[rows, :] = wgu_f32[slot, rows, :].astype(BF16)
            wd_bf[rows, :] = wd_f32[slot, rows, :].astype(BF16)
            return carry

        lax.fori_loop(0, D_MODEL // WEIGHT_CAST_ROWS, cast, 0)

    @pl.when(jnp.logical_not(live))
    def _():
        y_ref[...] = jnp.zeros_like(y_ref)

    def ffn(rows):
        x = _load_rows_from_tiles(x_ref.at[pl.ds(0, rows)], rows)
        h = _dot(x.astype(BF16), wgu_bf[...]) + bgu_ref[0]
        gate = jnp.minimum(h[:, :D_FF], SWIGLU_LIMIT)
        up = jnp.clip(h[:, D_FF:], -SWIGLU_LIMIT, SWIGLU_LIMIT)
        glu = gate * _sigmoid(gate * SWIGLU_ALPHA)
        y = _dot(((up + 1.0) * glu).astype(BF16), wd_bf[...]) + bd_ref[0]
        _store_rows_as_tiles(y_ref.at[pl.ds(0, rows)], y.astype(BF16).astype(F32))

    half = MOE_BLOCK // 2
    short = bvalid_ref[i] <= half

    @pl.when(jnp.logical_and(live, jnp.logical_not(short)))
    def _():
        ffn(MOE_BLOCK)

    @pl.when(jnp.logical_and(live, short))
    def _():
        ffn(half)
        y_ref[pl.ds(half, half)] = jnp.zeros((half,) + y_ref.shape[1:], y_ref.dtype)


def _experts(block_e, nblk, block_eord, block_enext, block_valid, x_sorted, wgu, bgu, wd, bd):
    n_blocks = x_sorted.shape[0] // MOE_BLOCK
    wmap = lambda i, be, *_: (be[i], 0, 0)
    rmap = lambda i, *_: (i, 0, 0)
    hbm = pl.BlockSpec(memory_space=pl.ANY)
    grid_spec = pltpu.PrefetchScalarGridSpec(
        num_scalar_prefetch=5,
        grid=(n_blocks,),
        in_specs=[
            pl.BlockSpec((MOE_BLOCK, ROW_CHUNKS, LANES), rmap),
            hbm,
            pl.BlockSpec((1, 1, 2 * D_FF), wmap),
            hbm,
            pl.BlockSpec((1, 1, D_MODEL), wmap),
        ],
        out_specs=pl.BlockSpec((MOE_BLOCK, ROW_CHUNKS, LANES), rmap),
        scratch_shapes=[
            pltpu.VMEM((2, D_MODEL, 2 * D_FF), F32), pltpu.VMEM((2, D_FF, D_MODEL), F32),
            pltpu.VMEM((D_MODEL, 2 * D_FF), BF16), pltpu.VMEM((D_FF, D_MODEL), BF16),
            pltpu.SemaphoreType.DMA((2, 2)),
        ],
    )
    return pl.pallas_call(
        _expert_kernel,
        grid_spec=grid_spec,
        out_shape=jax.ShapeDtypeStruct(x_sorted.shape, jnp.uint32),
        compiler_params=pltpu.CompilerParams(
            dimension_semantics=("arbitrary",), vmem_limit_bytes=VMEM_LIMIT),
        name="moe_experts",
    )(block_e, nblk, block_eord, block_enext, block_valid, x_sorted, wgu, bgu, wd, bd)


def _combine_kernel(tcnt_ref, tdst_ref, ttot_ref, pos_ref, gate_ref, ys_ref, hp_ref, hs_ref, nf_ref, yp_ref, ysmp_ref,
                    runs_scr, sem):
    i = pl.program_id(0)
    n_tiles = pl.num_programs(0)
    n_prompt = n_tiles - 1
    slot = i % 2

    def start_runs(tile, s):
        _for_each_run(tcnt_ref, tile, lambda e, off, n: _run_copy(
            ys_ref, tdst_ref[tile * N_EXPERTS + e], runs_scr.at[s], off, n, sem.at[s]).start())

    @pl.when(i == 0)
    def _():
        start_runs(0, 0)

    @pl.when(i + 1 < n_tiles)
    def _():
        start_runs(i + 1, 1 - slot)

    _run_copy(ys_ref, 0, runs_scr.at[slot], 0, ttot_ref[i], sem.at[slot]).wait()
    ys = _load_rows_from_tiles(runs_scr.at[slot], TILE_ROWS).astype(BF16)
    r_iota = lax.broadcasted_iota(jnp.int32, (TILE_ROWS, ROW_TILE), 0)
    gmat = jnp.zeros((TILE_ROWS, ROW_TILE), F32)
    for kk in range(TOP_K):
        gmat = jnp.where(r_iota == pos_ref[kk:kk + 1, :], gate_ref[kk:kk + 1, :], gmat)
    moe = lax.dot_general(gmat.astype(BF16), ys, (((0,), (0,)), ((), ())), preferred_element_type=F32)

    @pl.when(i < n_prompt)
    def _():
        yp_ref[...] = _rmsnorm(hp_ref[...] + moe, nf_ref[...])

    @pl.when(i == n_prompt)
    def _():
        ysmp_ref[...] = _rmsnorm(hs_ref[...] + moe, nf_ref[...])


def _combine(tcnt, tdst, ttot, pos, gates, y_sorted, h1_p, h1_s, nf):
    n_prompt = h1_p.shape[0] // ROW_TILE
    pmap = lambda i, *_: (jnp.minimum(i, n_prompt - 1), 0)
    smap = lambda i, *_: (0, 0)
    lmap = lambda i, *_: (0, i)
    grid_spec = pltpu.PrefetchScalarGridSpec(
        num_scalar_prefetch=3,
        grid=(n_prompt + 1,),
        in_specs=[
            pl.BlockSpec((TOP_K, ROW_TILE), lmap),
            pl.BlockSpec((TOP_K, ROW_TILE), lmap),
            pl.BlockSpec(memory_space=pl.ANY),
            pl.BlockSpec((ROW_TILE, D_MODEL), pmap),
            pl.BlockSpec((ROW_TILE, D_MODEL), smap),
            pl.BlockSpec((1, D_MODEL), smap),
        ],
        out_specs=[pl.BlockSpec((ROW_TILE, D_MODEL), pmap), pl.BlockSpec((ROW_TILE, D_MODEL), smap)],
        scratch_shapes=[pltpu.VMEM((2, TILE_ROWS, ROW_CHUNKS, LANES), jnp.uint32), pltpu.SemaphoreType.DMA((2,))],
    )
    return pl.pallas_call(
        _combine_kernel,
        grid_spec=grid_spec,
        out_shape=[jax.ShapeDtypeStruct(h1_p.shape, F32), jax.ShapeDtypeStruct(h1_s.shape, F32)],
        compiler_params=pltpu.CompilerParams(dimension_semantics=("arbitrary",)),
        name="moe_combine",
    )(tcnt, tdst, ttot, pos, gates, y_sorted, h1_p, h1_s, nf)


def _rotary_tables(pos):
    f = np.float32
    inv = np.power(f(ROPE_BASE), -np.arange(0, RET_DK, 2, dtype=f) / f(RET_DK)).astype(f)
    ang = (np.asarray(pos, f)[:, None] * inv[None, :]).astype(f)
    cos, sin = np.cos(ang).astype(f), np.sin(ang).astype(f)
    return np.concatenate([cos, cos], axis=1), np.concatenate([-sin, sin], axis=1)


def _decay_tables(chunk):
    f = np.float32
    log_g = np.log1p(-np.exp2(f(-5.0) - np.arange(RET_HEADS, dtype=f))).astype(f)
    i = np.arange(chunk, dtype=f)
    diff = i[:, None] - i[None, :]
    mask = np.where(diff[None] >= 0, np.exp(np.maximum(diff, f(0.0))[None] * log_g[:, None, None]), f(0.0)).astype(f)
    q_dec = np.exp((i + f(1.0))[None, :] * log_g[:, None]).astype(f)
    k_dec = np.exp((f(chunk) - f(1.0) - i)[None, :] * log_g[:, None]).astype(f)
    c_dec = np.exp(f(chunk) * log_g).astype(f)
    return mask, q_dec, k_dec, c_dec


def kernel(x_prompt, x_sample, state_ret, state_pool, meta_tokens, norm_mix, w_in, ret_gn, pool_w, pool_scale,
           w_ret_branch, w_pool_branch, w_out, norm_ffn, w_router, b_router, w_gate_up, b_gate_up, w_down, b_down,
           norm_final):
    batch, seq, _ = x_prompt.shape
    nb = x_sample.shape[0]
    past_len = 16384
    n_prompt_tok = batch * seq
    n_tok = n_prompt_tok + nb

    w_in_bf = w_in[0].astype(BF16)
    wts = (ret_gn[0][None, :], pool_w[0].astype(BF16), pool_scale[0][None, :],
           w_ret_branch[0].astype(BF16), w_pool_branch[0].astype(BF16), w_out[0].astype(BF16),
           norm_ffn[0][None, :], w_router[0].T.astype(BF16), b_router[0][:, None])
    wgu = w_gate_up[0]
    wd = w_down[0]
    bgu = b_gate_up[0][:, None, :]
    bd = b_down[0][:, None, :]
    nmix = norm_mix[0][None, :]

    cos_p, sin_p = _rotary_tables(N_META + np.arange(seq))
    cos_s, sin_s = _rotary_tables(np.concatenate([np.arange(N_META), np.full((nb,), past_len)]))
    mask, q_dec, k_dec, c_dec = _decay_tables(CHUNK)
    dec = (mask,
           np.ascontiguousarray(np.broadcast_to(q_dec[:, :, None], (RET_HEADS, CHUNK, RET_DK))),
           np.ascontiguousarray(np.broadcast_to(k_dec[:, :, None], (RET_HEADS, CHUNK, RET_DK))),
           np.ascontiguousarray(np.broadcast_to(c_dec[:, None, None], (RET_HEADS, 1, RET_DV))))
    m1, q1, k1, c1 = _decay_tables(1)
    sdec = np.stack([m1[:, 0, 0], q1[:, 0], k1[:, 0], c1], axis=1)

    x2d = x_prompt.reshape(n_prompt_tok, D_MODEL)
    xs2d = x_sample.reshape(nb, D_MODEL)
    x_small = jnp.concatenate([meta_tokens, xs2d], axis=0)
    proj_small = _inproj(x_small, nmix, w_in_bf, cos_s, sin_s, N_META + nb, 1, F32)
    proj_s = tuple(a[N_META:] for a in proj_small)

    grp = nb // (n_prompt_tok // INPROJ_TILE)

    def cols(a):
        return a.T.reshape(RET_QK, nb // grp, grp).transpose(1, 0, 2)

    *proj_p, st_s, o_s = _inproj(
        x2d, nmix, w_in_bf, cos_p, sin_p, INPROJ_TILE, batch, BF16,
        sample=(sdec, cols(proj_s[0]), cols(proj_s[1]), proj_s[0], proj_s[1], proj_s[2], state_ret[0]))

    lead = CHUNK - N_META
    kmeta = jnp.pad(proj_small[1][:N_META], ((lead, 0), (0, 0))).astype(BF16)
    vmeta = jnp.pad(proj_small[2][:N_META], ((lead, 0), (0, 0))).astype(BF16)
    pmeta = proj_small[4][:N_META]

    pool2d = state_pool[0].reshape(nb, POOL_BUF * POOL_WIDTH)
    (pool_s, h1_s, xn2_s, idx_s, gate_s, rank_s, cnt_s) = _sample_mixer(o_s, proj_s, xs2d, pool2d, wts)

    (h1_p, xn2_p, idx_p, gate_p, rank_p, cnt, s_fin, p_fin) = _mixer(
        proj_p, x2d, kmeta, vmeta, pmeta, dec, wts, cnt_s, batch, seq)

    assert n_prompt_tok % ROW_TILE == 0 and nb <= ROW_TILE
    n_phantom = ROW_TILE - nb
    n_tiles = n_prompt_tok // ROW_TILE + 1
    i32 = jnp.int32
    counts = cnt[:, 0].astype(i32)
    padded = ((counts + MOE_BLOCK - 1) // MOE_BLOCK) * MOE_BLOCK
    pad_end = jnp.cumsum(padded)
    pad_start = pad_end - padded
    n_blocks = (n_tok * TOP_K) // MOE_BLOCK + N_EXPERTS
    block_row = jnp.arange(n_blocks, dtype=i32) * MOE_BLOCK
    block_e = jnp.minimum(jnp.sum((pad_end[None, :] <= block_row[:, None]).astype(i32), axis=1), N_EXPERTS - 1)
    nblk = (pad_end[-1:] // MOE_BLOCK).astype(i32)
    e_row = jnp.arange(N_EXPERTS, dtype=i32)
    used = padded > 0
    e_ord = jnp.cumsum(used.astype(i32)) - 1
    later_used = jnp.logical_and(e_row[None, :] > e_row[:, None], used[None, :])
    e_next = jnp.min(jnp.where(later_used, e_row[None, :], N_EXPERTS), axis=1)
    e_next = jnp.where(e_next == N_EXPERTS, -1, e_next)
    of_block = block_e[:, None] == e_row[None, :]
    block_eord = jnp.sum(jnp.where(of_block, e_ord[None, :], 0), axis=1)
    block_enext = jnp.sum(jnp.where(of_block, e_next[None, :], 0), axis=1)
    block_left = jnp.sum(jnp.where(of_block, (pad_start + counts)[None, :], 0), axis=1) - block_row
    block_valid = jnp.clip(block_left, 0, MOE_BLOCK)
    phantom = lambda fill, dt: jnp.full((TOP_K, n_phantom), fill, dt)
    idx = jnp.concatenate([idx_p, idx_s, phantom(-1, i32)], axis=1)
    rank = jnp.concatenate([rank_p, rank_s, phantom(0, i32)], axis=1)
    gates = jnp.concatenate([gate_p, gate_s, phantom(0.0, F32)], axis=1)
    onehot = idx[None] == jnp.arange(N_EXPERTS, dtype=i32)[:, None, None]
    tile_cnt = jnp.sum(onehot.reshape(N_EXPERTS, TOP_K, n_tiles, ROW_TILE).astype(i32), axis=(1, 3)).T
    by_time = jnp.concatenate([tile_cnt[-1:], tile_cnt[:-1]], axis=0)
    before_time = jnp.cumsum(by_time, axis=0) - by_time
    run_before = jnp.concatenate([before_time[1:], before_time[:1]], axis=0)
    tile_off = jnp.cumsum(tile_cnt, axis=1) - tile_cnt
    tile_dst = pad_start[None, :] + run_before
    delta = jnp.repeat((tile_off - run_before).T, ROW_TILE, axis=1)
    pos = rank + jnp.sum(jnp.where(onehot, delta[:, None, :], 0), axis=0)
    pos = jnp.where(idx >= 0, pos, -1)
    tcnt, tdst, ttot = tile_cnt.reshape(-1), tile_dst.reshape(-1), jnp.sum(tile_cnt, axis=1)

    tail = ((0, n_phantom), (0, 0))
    x_sorted = _dispatch(tcnt, tdst, ttot, pad_start + counts, padded - counts, nblk, pos, xn2_p,
                         jnp.pad(xn2_s, tail), n_blocks * MOE_BLOCK)
    y_sorted = _experts(block_e, nblk, block_eord, block_enext, block_valid, x_sorted, wgu, bgu, wd, bd)
    y_p, y_s = _combine(tcnt, tdst, ttot, pos, gates, y_sorted, h1_p, jnp.pad(h1_s, tail), norm_final[None, :])

    y_prompt = y_p.reshape(batch, seq, D_MODEL)
    y_sample = y_s[:nb].reshape(nb, 1, D_MODEL)
    ret_state_prompt = s_fin[None]
    pool_state_prompt = p_fin[:, 1:, :][None]
    ret_state_sample = st_s[None]
    pool_state_sample = pool_s.reshape(nb, POOL_BUF, POOL_WIDTH)[None]
    return (y_prompt, y_sample, ret_state_prompt, pool_state_prompt, ret_state_sample, pool_state_sample)
```

```python
import functools

import jax
import jax.numpy as jnp
import numpy as np
from jax import lax
from jax.experimental import pallas as pl
from jax.experimental.pallas import tpu as pltpu

F32 = jnp.float32
BF16 = jnp.bfloat16

D_MODEL = 1024
N_META = 16
RET_HEADS = 4
RET_DK = 128
RET_DV = 256
RET_QK = RET_HEADS * RET_DK
RET_V = RET_HEADS * RET_DV
CHUNK = 128
ROPE_BASE = 10000.0
POOL_WINDOWS = (2, 4, 8, 16)
POOL_GROUPS = 4
POOL_GROUP_DIM = 128
POOL_WIDTH = POOL_GROUPS * POOL_GROUP_DIM
POOL_BUF = max(POOL_WINDOWS) - 1
N_EXPERTS = 32
TOP_K = 4
D_FF = D_MODEL
SWIGLU_LIMIT = 7.0
SWIGLU_ALPHA = 1.702
EPS = 1e-6
IN_WIDTHS = (RET_QK, RET_QK, RET_V, RET_V, POOL_WIDTH, D_MODEL, D_MODEL)
IN_TOTAL = sum(IN_WIDTHS)
IN_OFFS = tuple(int(s) for s in np.cumsum((0,) + IN_WIDTHS))

LANES = 128
ROW_CHUNKS = D_MODEL // (2 * LANES)
INPROJ_TILE = 512
MIXER_TILE = 512
MOE_BLOCK = 1024
MOE_SUB = 256
ROW_TILE = 256
SUBLANES = 8
WEIGHT_CAST_ROWS = 128
VMEM_LIMIT = 56 * 1024 * 1024

assert N_META + 1 >= max(POOL_WINDOWS)
assert POOL_WINDOWS == (2, 4, 8, 16)


def _dot(a, b):
    return jnp.dot(a, b, preferred_element_type=F32)


def _rmsnorm(x, w):
    return x * lax.rsqrt(jnp.mean(x * x, axis=-1, keepdims=True) + EPS) * w


def _sigmoid(x):
    return 0.5 * jnp.tanh(0.5 * x) + 0.5


def _store_rows_as_tiles(ref, x):
    half = D_MODEL // 2
    hi = pltpu.bitcast(x[:, :half], jnp.uint32) & jnp.uint32(0xFFFF0000)
    lo = pltpu.bitcast(x[:, half:], jnp.uint32) >> 16
    ref[...] = (hi | lo).reshape(x.shape[0], ROW_CHUNKS, LANES)


def _load_rows_from_tiles(ref, rows):
    w = ref[...].reshape(rows, D_MODEL // 2)
    hi = pltpu.bitcast(w & jnp.uint32(0xFFFF0000), F32)
    lo = pltpu.bitcast(w << 16, F32)
    return jnp.concatenate([hi, lo], axis=1)


def _inproj_kernel(x_ref, nw_ref, w_ref, cos_ref, sin_ref,
                   q_ref, k_ref, v_ref, g_ref, p_ref, ga_ref, gb_ref):
    xn = _rmsnorm(x_ref[...], nw_ref[...]).astype(BF16)
    cos = cos_ref[...]
    sin = sin_ref[...]

    def seg(i):
        return _dot(xn, w_ref[:, IN_OFFS[i]:IN_OFFS[i + 1]])

    def rot(a):
        return a * cos + pltpu.roll(a, RET_DK // 2, 1) * sin

    q = seg(0)
    k = seg(1)
    for h in range(RET_HEADS):
        sl = slice(h * RET_DK, (h + 1) * RET_DK)
        q_ref[:, sl] = rot(q[:, sl]).astype(q_ref.dtype)
        k_ref[:, sl] = (rot(k[:, sl]) * (RET_DK ** -0.5)).astype(k_ref.dtype)
    v_ref[...] = seg(2).astype(v_ref.dtype)
    g = seg(3)
    g_ref[...] = (g * _sigmoid(g)).astype(g_ref.dtype)
    p_ref[...] = seg(4)
    ga_ref[...] = _sigmoid(seg(5)).astype(ga_ref.dtype)
    gb_ref[...] = _sigmoid(seg(6)).astype(gb_ref.dtype)


def _sample_state_step(step, sdec_ref, qt_ref, kt_ref, q_ref, k_ref, v_ref, st_ref, stout_ref, o_ref):
    grp = st_ref.shape[0]
    row0 = step * grp
    if grp % SUBLANES == 0:
        rows = pl.ds(pl.multiple_of(row0, SUBLANES), grp)
        q8, k8, v8 = q_ref[rows, :], k_ref[rows, :], v_ref[rows, :]
    else:
        assert 2 * grp == SUBLANES
        rows = pl.ds(pl.multiple_of((step // 2) * SUBLANES, SUBLANES), SUBLANES)
        first = step % 2 == 0
        q8, k8, v8 = (jnp.where(first, a[:grp], a[grp:]) for a in (q_ref[rows, :], k_ref[rows, :], v_ref[rows, :]))
    for h in range(RET_HEADS):
        ksl = slice(h * RET_DK, (h + 1) * RET_DK)
        vsl = slice(h * RET_DV, (h + 1) * RET_DV)
        score = jnp.sum(q8[:, ksl] * k8[:, ksl], axis=1, keepdims=True) * sdec_ref[h, 0]
        intra = score * v8[:, vsl]
        for bb in range(grp):
            s_old = st_ref[bb, h]
            qcol = qt_ref[0, ksl, bb:bb + 1] * sdec_ref[h, 1]
            kcol = kt_ref[0, ksl, bb:bb + 1] * sdec_ref[h, 2]
            cross = jnp.sum(s_old * qcol, axis=0, keepdims=True)
            o_ref[pl.ds(row0 + bb, 1), vsl] = intra[bb:bb + 1, :] + cross
            stout_ref[bb, h] = s_old * sdec_ref[h, 3] + kcol * v8[bb:bb + 1, vsl]


def _inproj_sample_kernel(x_ref, nw_ref, w_ref, cos_ref, sin_ref,
                          sdec_ref, qt_ref, kt_ref, qs_ref, ks_ref, vs_ref, st_ref,
                          q_ref, k_ref, v_ref, g_ref, p_ref, ga_ref, gb_ref, stout_ref, os_ref):
    _inproj_kernel(x_ref, nw_ref, w_ref, cos_ref, sin_ref, q_ref, k_ref, v_ref, g_ref, p_ref, ga_ref, gb_ref)
    step = pl.program_id(0) * pl.num_programs(1) + pl.program_id(1)
    _sample_state_step(step, sdec_ref, qt_ref, kt_ref, qs_ref, ks_ref, vs_ref, st_ref, stout_ref, os_ref)


def _inproj(x2d, nw, w_in_bf, cosf, sinf, tile, n_outer, act_dtype, sample=None):
    rows = x2d.shape[0]
    n_inner = rows // (tile * n_outer)
    row_map = lambda b, j: (b * n_inner + j, 0)
    tab_map = lambda b, j: (j, 0)
    const = lambda b, j: (0, 0)
    widths = IN_WIDTHS
    dts = (act_dtype, act_dtype, act_dtype, act_dtype, F32, act_dtype, act_dtype)
    in_arrays = [x2d, nw, w_in_bf, cosf, sinf]
    in_specs = [
        pl.BlockSpec((tile, D_MODEL), row_map),
        pl.BlockSpec((1, D_MODEL), const),
        pl.BlockSpec((D_MODEL, IN_TOTAL), const, pipeline_mode=pl.Buffered(1)),
        pl.BlockSpec((tile, RET_DK), tab_map),
        pl.BlockSpec((tile, RET_DK), tab_map),
    ]
    out_specs = [pl.BlockSpec((tile, w), row_map) for w in widths]
    out_shape = [jax.ShapeDtypeStruct((rows, w), dt) for w, dt in zip(widths, dts)]
    body = _inproj_kernel
    if sample is not None:
        sdec, qt, kt, qs, ks, vs, state = sample
        nb = state.shape[0]
        grp = nb // (n_outer * n_inner)
        assert grp * n_outer * n_inner == nb and qt.shape == (nb // grp, RET_QK, grp)
        step_map = lambda b, j: (b * n_inner + j, 0, 0)
        st_spec = pl.BlockSpec((grp, RET_HEADS, RET_DK, RET_DV), lambda b, j: (b * n_inner + j, 0, 0, 0))
        in_arrays += [sdec, qt, kt, qs, ks, vs, state]
        in_specs += [pl.BlockSpec(memory_space=pltpu.SMEM),
                     pl.BlockSpec((1, RET_QK, grp), step_map), pl.BlockSpec((1, RET_QK, grp), step_map),
                     pl.BlockSpec(qs.shape, const), pl.BlockSpec(ks.shape, const), pl.BlockSpec(vs.shape, const),
                     st_spec]
        out_specs += [st_spec, pl.BlockSpec((nb, RET_V), const)]
        out_shape += [jax.ShapeDtypeStruct(state.shape, F32), jax.ShapeDtypeStruct((nb, RET_V), F32)]
        body = _inproj_sample_kernel
    return pl.pallas_call(
        body,
        grid=(n_outer, n_inner),
        in_specs=in_specs,
        out_specs=out_specs,
        out_shape=out_shape,
        compiler_params=pltpu.CompilerParams(
            dimension_semantics=("arbitrary", "arbitrary"), vmem_limit_bytes=VMEM_LIMIT),
        name="inproj",
    )(*in_arrays)


def _group_norm(o, gn_row):
    mu = jnp.mean(o, axis=-1, keepdims=True)
    var = jnp.mean(jnp.square(o - mu), axis=-1, keepdims=True)
    return (o - mu) * lax.rsqrt(var + EPS) * gn_row


def _pool_branch(groups, poolw_ref, pscale_ref, wpool_ref):
    pm = [_dot(g.astype(BF16), poolw_ref[i]) for i, g in enumerate(groups)]
    pm = jnp.concatenate(pm, axis=1) * pscale_ref[...]
    return _dot(pm.astype(BF16), wpool_ref[...])


def _merge_tail(o_norm, silu_g, sig_a, sig_b, yb, x, wret_ref, wout_ref):
    ya = _dot((silu_g.astype(F32) * o_norm).astype(BF16), wret_ref[...])
    merged = sig_a.astype(F32) * ya + sig_b.astype(F32) * yb
    return x + _dot(merged.astype(BF16), wout_ref[...])


def _route(h1, nffn_ref, wrt_ref, br_ref, run_scr,
           xn2_ref, idx_ref, gate_ref, rank_ref):
    tm = h1.shape[0]
    xn2 = _rmsnorm(h1, nffn_ref[...]).astype(BF16)
    xn2_ref[...] = xn2
    logits = lax.dot_general(wrt_ref[...], xn2, (((1,), (1,)), ((), ())),
                             preferred_element_type=F32) + br_ref[...]
    e_iota = lax.broadcasted_iota(jnp.int32, (N_EXPERTS, tm), 0)
    work = logits
    vals, sels = [], []
    chosen = jnp.zeros((N_EXPERTS, tm), F32)
    for _ in range(TOP_K):
        m = jnp.max(work, axis=0, keepdims=True)
        sel = jnp.min(jnp.where(work == m, e_iota, N_EXPERTS), axis=0, keepdims=True)
        hit = e_iota == sel
        vals.append(m)
        sels.append(sel)
        chosen = jnp.where(hit, 1.0, chosen)
        work = jnp.where(hit, -jnp.inf, work)
    exps = [jnp.exp(v - vals[0]) for v in vals]
    denom = exps[0] + exps[1] + exps[2] + exps[3]
    gates = [e / denom for e in exps]
    r_i = lax.broadcasted_iota(jnp.int32, (tm, tm), 0)
    c_i = lax.broadcasted_iota(jnp.int32, (tm, tm), 1)
    before = jnp.where(r_i < c_i, 1.0, 0.0).astype(BF16)
    base = run_scr[...] + _dot(chosen.astype(BF16), before)
    for kk in range(TOP_K):
        rk = jnp.sum(jnp.where(e_iota == sels[kk], base, 0.0), axis=0, keepdims=True)
        rank_ref[kk:kk + 1, :] = rk.astype(jnp.int32)
        idx_ref[kk:kk + 1, :] = sels[kk]
        gate_ref[kk:kk + 1, :] = gates[kk]
    run_scr[...] = run_scr[...] + jnp.sum(chosen, axis=1, keepdims=True)


def _mixer_kernel(q_ref, k_ref, v_ref, g_ref, p_ref, ga_ref, gb_ref, x_ref,
                  kmeta_ref, vmeta_ref, pmeta_ref, mask_ref, qdec_ref, kdec_ref, cdec_ref, gn_ref,
                  poolw_ref, pscale_ref, wret_ref, wpool_ref, wout_ref, nffn_ref, wrt_ref, br_ref,
                  cnt0_ref,
                  h1_ref, xn2_ref, idx_ref, gate_ref, rank_ref, cnt_ref, sfin_ref, pfin_ref,
                  s_scr, ext_scr, o_scr, run_scr):
    b = pl.program_id(0)
    j = pl.program_id(1)
    nj = pl.num_programs(1)
    tm = q_ref.shape[0]

    def state_update(s_old, kc, vc, h):
        kd = (kc.astype(F32) * kdec_ref[h]).astype(BF16)
        upd = lax.dot_general(kd, vc, (((0,), (0,)), ((), ())), preferred_element_type=F32)
        return s_old * cdec_ref[h] + upd

    @pl.when(jnp.logical_and(b == 0, j == 0))
    def _():
        run_scr[...] = cnt0_ref[:, 0:1]

    @pl.when(j == 0)
    def _():
        for h in range(RET_HEADS):
            kc = kmeta_ref[:, h * RET_DK:(h + 1) * RET_DK]
            vc = vmeta_ref[:, h * RET_DV:(h + 1) * RET_DV]
            s_scr[h] = state_update(jnp.zeros((RET_DK, RET_DV), F32), kc, vc, h)
        ext_scr[0:N_META, :] = pmeta_ref[...]

    n_chunks = tm // CHUNK
    units = [(c, h) for c in range(n_chunks) for h in range(RET_HEADS)]

    def operands(c, h):
        rows = slice(c * CHUNK, (c + 1) * CHUNK)
        return (q_ref[rows, h * RET_DK:(h + 1) * RET_DK], k_ref[rows, h * RET_DK:(h + 1) * RET_DK],
                v_ref[rows, h * RET_DV:(h + 1) * RET_DV])

    lhs, upd = {}, {}
    for c, h in units:
        qc, kc, vc = operands(c, h)
        scores = lax.dot_general(qc, kc, (((1,), (1,)), ((), ())), preferred_element_type=F32) * mask_ref[h]
        qd = (qc.astype(F32) * qdec_ref[h]).astype(BF16)
        lhs[c, h] = jnp.concatenate([scores.astype(BF16), qd], axis=1)
        kd = (kc.astype(F32) * kdec_ref[h]).astype(BF16)
        upd[c, h] = lax.dot_general(kd, vc, (((0,), (0,)), ((), ())), preferred_element_type=F32)
    state = {}
    for h in range(RET_HEADS):
        s = s_scr[h]
        for c in range(n_chunks):
            state[c, h] = s
            s = s * cdec_ref[h] + upd[c, h]
        s_scr[h] = s
    for c, h in units:
        rhs = jnp.concatenate([operands(c, h)[2], state[c, h].astype(BF16)], axis=0)
        o = _dot(lhs[c, h], rhs)
        o_scr[c * CHUNK:(c + 1) * CHUNK, h * RET_DV:(h + 1) * RET_DV] = _group_norm(
            o, gn_ref[:, h * RET_DV:(h + 1) * RET_DV])

    p = p_ref[...]
    ext_scr[N_META:N_META + tm, :] = p
    a = ext_scr[...]
    g1 = POOL_GROUP_DIM
    s2 = a + pltpu.roll(a, 1, 0)
    s4 = s2[:, g1:] + pltpu.roll(s2[:, g1:], 2, 0)
    s8 = s4[:, g1:] + pltpu.roll(s4[:, g1:], 4, 0)
    s16 = s8[:, g1:] + pltpu.roll(s8[:, g1:], 8, 0)
    sums = (s2[N_META:, :g1], s4[N_META:, :g1], s8[N_META:, :g1], s16[N_META:, :])
    groups = [sums[i] * (1.0 / POOL_WINDOWS[i]) - p[:, i * g1:(i + 1) * g1] for i in range(POOL_GROUPS)]
    ext_scr[0:N_META, :] = ext_scr[tm:tm + N_META, :]

    yb = _pool_branch(groups, poolw_ref, pscale_ref, wpool_ref)
    h1 = _merge_tail(o_scr[...], g_ref[...], ga_ref[...], gb_ref[...], yb, x_ref[...], wret_ref, wout_ref)
    h1_ref[...] = h1
    _route(h1, nffn_ref, wrt_ref, br_ref, run_scr, xn2_ref, idx_ref, gate_ref, rank_ref)
    cnt_ref[...] = jnp.broadcast_to(run_scr[...], cnt_ref.shape)

    @pl.when(j == nj - 1)
    def _():
        for h in range(RET_HEADS):
            sfin_ref[0, h] = s_scr[h]
        pfin_ref[0] = ext_scr[0:N_META, :]


def _mixer(proj, x2d, kmeta, vmeta, pmeta, dec, wts, cnt0, batch, seq):
    q, k, v, g, p, ga, gb = proj
    tm = MIXER_TILE
    nj = seq // tm
    rows = batch * seq
    row_map = lambda b, j: (b * nj + j, 0)
    lane_map = lambda b, j: (0, b * nj + j)
    c2 = lambda b, j: (0, 0)
    c3 = lambda b, j: (0, 0, 0)

    def whole(a):
        return pl.BlockSpec(a.shape, c2 if a.ndim == 2 else c3)

    mask, qdec, kdec, cdec = dec
    gn, poolw, pscale, wret, wpool, wout, nffn, wrt, br = wts
    in_arrays = [q, k, v, g, p, ga, gb, x2d, kmeta, vmeta, pmeta, mask, qdec, kdec, cdec, gn,
                 poolw, pscale, wret, wpool, wout, nffn, wrt, br, cnt0]
    in_specs = [pl.BlockSpec((tm, a.shape[1]), row_map) for a in in_arrays[:8]]
    in_specs += [whole(a) for a in in_arrays[8:]]
    out_shape = [
        jax.ShapeDtypeStruct((rows, D_MODEL), F32),
        jax.ShapeDtypeStruct((rows, D_MODEL), BF16),
        jax.ShapeDtypeStruct((TOP_K, rows), jnp.int32),
        jax.ShapeDtypeStruct((TOP_K, rows), F32),
        jax.ShapeDtypeStruct((TOP_K, rows), jnp.int32),
        jax.ShapeDtypeStruct((N_EXPERTS, LANES), F32),
        jax.ShapeDtypeStruct((batch, RET_HEADS, RET_DK, RET_DV), F32),
        jax.ShapeDtypeStruct((batch, N_META, POOL_WIDTH), F32),
    ]
    out_specs = [
        pl.BlockSpec((tm, D_MODEL), row_map),
        pl.BlockSpec((tm, D_MODEL), row_map),
        pl.BlockSpec((TOP_K, tm), lane_map),
        pl.BlockSpec((TOP_K, tm), lane_map),
        pl.BlockSpec((TOP_K, tm), lane_map),
        pl.BlockSpec((N_EXPERTS, LANES), c2),
        pl.BlockSpec((1, RET_HEADS, RET_DK, RET_DV), lambda b, j: (b, 0, 0, 0)),
        pl.BlockSpec((1, N_META, POOL_WIDTH), lambda b, j: (b, 0, 0)),
    ]
    return pl.pallas_call(
        _mixer_kernel,
        grid=(batch, nj),
        in_specs=in_specs,
        out_specs=out_specs,
        out_shape=out_shape,
        scratch_shapes=[
            pltpu.VMEM((RET_HEADS, RET_DK, RET_DV), F32),
            pltpu.VMEM((N_META + tm, POOL_WIDTH), F32),
            pltpu.VMEM((tm, RET_V), F32),
            pltpu.VMEM((N_EXPERTS, 1), F32),
        ],
        compiler_params=pltpu.CompilerParams(
            dimension_semantics=("arbitrary", "arbitrary"), vmem_limit_bytes=VMEM_LIMIT),
        name="mixer",
    )(*in_arrays)


def _sample_kernel(o_ref, g_ref, p_ref, ga_ref, gb_ref, x_ref, pool_ref, gn_ref,
                   poolw_ref, pscale_ref, wret_ref, wpool_ref, wout_ref, nffn_ref, wrt_ref, br_ref,
                   poolout_ref, h1_ref, xn2_ref, idx_ref, gate_ref, rank_ref, cnt_ref,
                   run_scr):
    run_scr[...] = jnp.zeros_like(run_scr)
    o = o_ref[...]
    o_norm = jnp.concatenate(
        [_group_norm(o[:, h * RET_DV:(h + 1) * RET_DV], gn_ref[:, h * RET_DV:(h + 1) * RET_DV])
         for h in range(RET_HEADS)], axis=1)
    p = p_ref[...]
    w = POOL_WIDTH
    g1 = POOL_GROUP_DIM

    def prev(r, lo):
        return pool_ref[:, r * w + lo:(r + 1) * w]

    s2 = p + prev(14, 0)
    s4 = s2[:, g1:] + prev(13, g1) + prev(12, g1)
    s8 = s4[:, g1:] + prev(11, 2 * g1) + prev(10, 2 * g1) + prev(9, 2 * g1) + prev(8, 2 * g1)
    s16 = s8[:, g1:]
    for r in range(7, -1, -1):
        s16 = s16 + prev(r, 3 * g1)
    sums = (s2[:, :g1], s4[:, :g1], s8[:, :g1], s16)
    groups = [sums[t] * (1.0 / POOL_WINDOWS[t]) - p[:, t * g1:(t + 1) * g1] for t in range(POOL_GROUPS)]
    poolout_ref[:, 0:(POOL_BUF - 1) * w] = pool_ref[:, w:POOL_BUF * w]
    poolout_ref[:, (POOL_BUF - 1) * w:] = p
    yb = _pool_branch(groups, poolw_ref, pscale_ref, wpool_ref)
    h1 = _merge_tail(o_norm, g_ref[...], ga_ref[...], gb_ref[...], yb, x_ref[...], wret_ref, wout_ref)
    h1_ref[...] = h1
    _route(h1, nffn_ref, wrt_ref, br_ref, run_scr, xn2_ref, idx_ref, gate_ref, rank_ref)
    cnt_ref[...] = jnp.broadcast_to(run_scr[...], cnt_ref.shape)


def _sample_mixer(o_s, proj, x2d, pool2d, wts):
    _, _, _, g, p, ga, gb = proj
    nb = x2d.shape[0]
    c2 = lambda i: (0, 0)

    def whole(a):
        return pl.BlockSpec(a.shape, c2)

    gn, poolw, pscale, wret, wpool, wout, nffn, wrt, br = wts
    in_arrays = [o_s, g, p, ga, gb, x2d, pool2d, gn, poolw, pscale, wret, wpool, wout, nffn, wrt, br]
    in_specs = [whole(a) for a in in_arrays[:8]]
    in_specs += [pl.BlockSpec(poolw.shape, lambda i: (0, 0, 0))]
    in_specs += [whole(a) for a in in_arrays[9:]]
    out_shape = [
        jax.ShapeDtypeStruct(pool2d.shape, F32),
        jax.ShapeDtypeStruct((nb, D_MODEL), F32),
        jax.ShapeDtypeStruct((nb, D_MODEL), BF16),
        jax.ShapeDtypeStruct((TOP_K, nb), jnp.int32),
        jax.ShapeDtypeStruct((TOP_K, nb), F32),
        jax.ShapeDtypeStruct((TOP_K, nb), jnp.int32),
        jax.ShapeDtypeStruct((N_EXPERTS, LANES), F32),
    ]
    return pl.pallas_call(
        _sample_kernel,
        grid=(1,),
        in_specs=in_specs,
        out_specs=[pl.BlockSpec(s.shape, c2) for s in out_shape],
        out_shape=out_shape,
        scratch_shapes=[pltpu.VMEM((N_EXPERTS, 1), F32)],
        compiler_params=pltpu.CompilerParams(
            dimension_semantics=("arbitrary",), vmem_limit_bytes=VMEM_LIMIT),
        name="sample_mixer",
    )(*in_arrays)


TILE_ROWS = ROW_TILE * TOP_K


def _run_copy(src_ref, src_row, dst_ref, dst_row, n_rows, sem):
    return pltpu.make_async_copy(src_ref.at[pl.ds(src_row, n_rows)], dst_ref.at[pl.ds(dst_row, n_rows)], sem)


def _for_each_run(tcnt_ref, tile, fn):
    def body(e, off):
        n = tcnt_ref[tile * N_EXPERTS + e]

        @pl.when(n > 0)
        def _():
            fn(e, off, n)

        return off + n

    lax.fori_loop(0, N_EXPERTS, body, 0)


def _dispatch_kernel(tcnt_ref, tdst_ref, ttot_ref, zrow_ref, zcnt_ref, nblk_ref, pos_ref, xp_ref, xs_ref, out_ref,
                     sorted_scr, zero_scr, sem, zsem):
    i = pl.program_id(0)
    n_prompt = pl.num_programs(0) - 1
    n_blocks = out_ref.shape[0] // MOE_BLOCK
    slot = i % 2

    def wait_tile(tile, s):
        _run_copy(out_ref, 0, sorted_scr.at[s], 0, ttot_ref[tile], sem.at[s]).wait()

    def for_each_pad(fn):
        def body(e, carry):
            n = zcnt_ref[e]

            @pl.when(n > 0)
            def _():
                fn(_run_copy(zero_scr, 0, out_ref, zrow_ref[e], n, zsem))

            dead = nblk_ref[0] + e

            @pl.when(dead < n_blocks)
            def _():
                fn(_run_copy(zero_scr, 0, out_ref, dead * MOE_BLOCK, MOE_BLOCK, zsem))

            return carry

        lax.fori_loop(0, N_EXPERTS, body, 0)

    @pl.when(i == 0)
    def _():
        zero_scr[...] = jnp.zeros_like(zero_scr)
        for_each_pad(lambda cp: cp.start())

    def sort_tile(x_ref):
        r_iota = lax.broadcasted_iota(jnp.int32, (TILE_ROWS, ROW_TILE), 0)
        hit = r_iota == pos_ref[0:1, :]
        for kk in range(1, TOP_K):
            hit = jnp.logical_or(hit, r_iota == pos_ref[kk:kk + 1, :])
        perm = jnp.where(hit, 1.0, 0.0).astype(BF16)
        xs = _dot(perm, x_ref[...])

        @pl.when(i >= 2)
        def _():
            wait_tile(i - 2, slot)

        _store_rows_as_tiles(sorted_scr.at[slot], xs)

    @pl.when(i < n_prompt)
    def _():
        sort_tile(xp_ref)

    @pl.when(i == n_prompt)
    def _():
        sort_tile(xs_ref)

    _for_each_run(tcnt_ref, i, lambda e, off, n: _run_copy(
        sorted_scr.at[slot], off, out_ref, tdst_ref[i * N_EXPERTS + e], n, sem.at[slot]).start())

    @pl.when(i == n_prompt)
    def _():
        wait_tile(i - 1, 1 - slot)
        wait_tile(i, slot)
        for_each_pad(lambda cp: cp.wait())


def _dispatch(tcnt, tdst, ttot, zrow, zcnt, nblk, pos, xn2_p, xn2_s, n_sorted):
    n_prompt = xn2_p.shape[0] // ROW_TILE
    grid_spec = pltpu.PrefetchScalarGridSpec(
        num_scalar_prefetch=6,
        grid=(n_prompt + 1,),
        in_specs=[
            pl.BlockSpec((TOP_K, ROW_TILE), lambda i, *_: (0, i)),
            pl.BlockSpec((ROW_TILE, D_MODEL), lambda i, *_: (jnp.minimum(i, n_prompt - 1), 0)),
            pl.BlockSpec((ROW_TILE, D_MODEL), lambda i, *_: (0, 0)),
        ],
        out_specs=pl.BlockSpec(memory_space=pl.ANY),
        scratch_shapes=[
            pltpu.VMEM((2, TILE_ROWS, ROW_CHUNKS, LANES), jnp.uint32),
            pltpu.VMEM((MOE_BLOCK, ROW_CHUNKS, LANES), jnp.uint32),
            pltpu.SemaphoreType.DMA((2,)), pltpu.SemaphoreType.DMA(()),
        ],
    )
    return pl.pallas_call(
        _dispatch_kernel,
        grid_spec=grid_spec,
        out_shape=jax.ShapeDtypeStruct((n_sorted, ROW_CHUNKS, LANES), jnp.uint32),
        compiler_params=pltpu.CompilerParams(dimension_semantics=("arbitrary",)),
        name="moe_dispatch",
    )(tcnt, tdst, ttot, zrow, zcnt, nblk, pos, xn2_p, xn2_s)


def _expert_kernel(be_ref, nblk_ref, eord_ref, enext_ref, bvalid_ref, x_ref, wgu_ref, bgu_ref, wd_ref, bd_ref, y_ref,
                   wgu_f32, wd_f32, wgu_bf, wd_bf, wsem):
    i = pl.program_id(0)
    live = i < nblk_ref[0]
    new_expert = jnp.logical_or(i == 0, be_ref[i] != be_ref[jnp.maximum(i - 1, 0)])

    def weight_copies(e, s):
        return (pltpu.make_async_copy(wgu_ref.at[e], wgu_f32.at[s], wsem.at[0, s]),
                pltpu.make_async_copy(wd_ref.at[e], wd_f32.at[s], wsem.at[1, s]))

    @pl.when(jnp.logical_and(live, new_expert))
    def _():
        slot = eord_ref[i] % 2

        @pl.when(i == 0)
        def _():
            for cp in weight_copies(be_ref[0], 0):
                cp.start(priority=1)

        for cp in weight_copies(be_ref[i], slot):
            cp.wait()

        @pl.when(enext_ref[i] >= 0)
        def _():
            for cp in weight_copies(enext_ref[i], 1 - slot):
                cp.start(priority=1)

        def cast(c, carry):
            rows = pl.ds(pl.multiple_of(c * WEIGHT_CAST_ROWS, WEIGHT_CAST_ROWS), WEIGHT_CAST_ROWS)
            wgu_bf[rows, :] = wgu_f32[slot, rows, :].astype(BF16)
            wd_bf[rows, :] = wd_f32[slot, rows, :].astype(BF16)
            return carry

        lax.fori_loop(0, D_MODEL // WEIGHT_CAST_ROWS, cast, 0)

    @pl.when(jnp.logical_not(live))
    def _():
        y_ref[...] = jnp.zeros_like(y_ref)

    for start in range(0, MOE_BLOCK, MOE_SUB):
        rows = pl.ds(start, MOE_SUB)
        has_tokens = jnp.logical_and(live, bvalid_ref[i] > start)

        @pl.when(has_tokens)
        def _(rows=rows):
            x = _load_rows_from_tiles(x_ref.at[rows], MOE_SUB)
            h = _dot(x.astype(BF16), wgu_bf[...]) + bgu_ref[0]
            gate = jnp.minimum(h[:, :D_FF], SWIGLU_LIMIT)
            up = jnp.clip(h[:, D_FF:], -SWIGLU_LIMIT, SWIGLU_LIMIT)
            glu = gate * _sigmoid(gate * SWIGLU_ALPHA)
            y = _dot(((up + 1.0) * glu).astype(BF16), wd_bf[...]) + bd_ref[0]
            _store_rows_as_tiles(y_ref.at[rows], y.astype(BF16).astype(F32))

        @pl.when(jnp.logical_and(live, jnp.logical_not(has_tokens)))
        def _(rows=rows):
            y_ref[rows] = jnp.zeros((MOE_SUB,) + y_ref.shape[1:], y_ref.dtype)


def _experts(block_e, nblk, block_eord, block_enext, block_valid, x_sorted, wgu, bgu, wd, bd):
    n_blocks = x_sorted.shape[0] // MOE_BLOCK
    wmap = lambda i, be, *_: (be[i], 0, 0)
    rmap = lambda i, *_: (i, 0, 0)
    hbm = pl.BlockSpec(memory_space=pl.ANY)
    grid_spec = pltpu.PrefetchScalarGridSpec(
        num_scalar_prefetch=5,
        grid=(n_blocks,),
        in_specs=[
            pl.BlockSpec((MOE_BLOCK, ROW_CHUNKS, LANES), rmap),
            hbm,
            pl.BlockSpec((1, 1, 2 * D_FF), wmap),
            hbm,
            pl.BlockSpec((1, 1, D_MODEL), wmap),
        ],
        out_specs=pl.BlockSpec((MOE_BLOCK, ROW_CHUNKS, LANES), rmap),
        scratch_shapes=[
            pltpu.VMEM((2, D_MODEL, 2 * D_FF), F32), pltpu.VMEM((2, D_FF, D_MODEL), F32),
            pltpu.VMEM((D_MODEL, 2 * D_FF), BF16), pltpu.VMEM((D_FF, D_MODEL), BF16),
            pltpu.SemaphoreType.DMA((2, 2)),
        ],
    )
    return pl.pallas_call(
        _expert_kernel,
        grid_spec=grid_spec,
        out_shape=jax.ShapeDtypeStruct(x_sorted.shape, jnp.uint32),
        compiler_params=pltpu.CompilerParams(
            dimension_semantics=("arbitrary",), vmem_limit_bytes=VMEM_LIMIT),
        name="moe_experts",
    )(block_e, nblk, block_eord, block_enext, block_valid, x_sorted, wgu, bgu, wd, bd)


def _combine_kernel(tcnt_ref, tdst_ref, ttot_ref, pos_ref, gate_ref, ys_ref, hp_ref, hs_ref, nf_ref, yp_ref, ysmp_ref,
                    runs_scr, sem):
    i = pl.program_id(0)
    n_tiles = pl.num_programs(0)
    n_prompt = n_tiles - 1
    slot = i % 2

    def start_runs(tile, s):
        _for_each_run(tcnt_ref, tile, lambda e, off, n: _run_copy(
            ys_ref, tdst_ref[tile * N_EXPERTS + e], runs_scr.at[s], off, n, sem.at[s]).start())

    @pl.when(i == 0)
    def _():
        start_runs(0, 0)

    @pl.when(i + 1 < n_tiles)
    def _():
        start_runs(i + 1, 1 - slot)

    _run_copy(ys_ref, 0, runs_scr.at[slot], 0, ttot_ref[i], sem.at[slot]).wait()
    ys = _load_rows_from_tiles(runs_scr.at[slot], TILE_ROWS).astype(BF16)
    r_iota = lax.broadcasted_iota(jnp.int32, (TILE_ROWS, ROW_TILE), 0)
    gmat = jnp.zeros((TILE_ROWS, ROW_TILE), F32)
    for kk in range(TOP_K):
        gmat = jnp.where(r_iota == pos_ref[kk:kk + 1, :], gate_ref[kk:kk + 1, :], gmat)
    moe = lax.dot_general(gmat.astype(BF16), ys, (((0,), (0,)), ((), ())), preferred_element_type=F32)

    @pl.when(i < n_prompt)
    def _():
        yp_ref[...] = _rmsnorm(hp_ref[...] + moe, nf_ref[...])

    @pl.when(i == n_prompt)
    def _():
        ysmp_ref[...] = _rmsnorm(hs_ref[...] + moe, nf_ref[...])


def _combine(tcnt, tdst, ttot, pos, gates, y_sorted, h1_p, h1_s, nf):
    n_prompt = h1_p.shape[0] // ROW_TILE
    pmap = lambda i, *_: (jnp.minimum(i, n_prompt - 1), 0)
    smap = lambda i, *_: (0, 0)
    lmap = lambda i, *_: (0, i)
    grid_spec = pltpu.PrefetchScalarGridSpec(
        num_scalar_prefetch=3,
        grid=(n_prompt + 1,),
        in_specs=[
            pl.BlockSpec((TOP_K, ROW_TILE), lmap),
            pl.BlockSpec((TOP_K, ROW_TILE), lmap),
            pl.BlockSpec(memory_space=pl.ANY),
            pl.BlockSpec((ROW_TILE, D_MODEL), pmap),
            pl.BlockSpec((ROW_TILE, D_MODEL), smap),
            pl.BlockSpec((1, D_MODEL), smap),
        ],
        out_specs=[pl.BlockSpec((ROW_TILE, D_MODEL), pmap), pl.BlockSpec((ROW_TILE, D_MODEL), smap)],
        scratch_shapes=[pltpu.VMEM((2, TILE_ROWS, ROW_CHUNKS, LANES), jnp.uint32), pltpu.SemaphoreType.DMA((2,))],
    )
    return pl.pallas_call(
        _combine_kernel,
        grid_spec=grid_spec,
        out_shape=[jax.ShapeDtypeStruct(h1_p.shape, F32), jax.ShapeDtypeStruct(h1_s.shape, F32)],
        compiler_params=pltpu.CompilerParams(dimension_semantics=("arbitrary",)),
        name="moe_combine",
    )(tcnt, tdst, ttot, pos, gates, y_sorted, h1_p, h1_s, nf)


def _rotary_tables(pos):
    f = np.float32
    inv = np.power(f(ROPE_BASE), -np.arange(0, RET_DK, 2, dtype=f) / f(RET_DK)).astype(f)
    ang = (np.asarray(pos, f)[:, None] * inv[None, :]).astype(f)
    cos, sin = np.cos(ang).astype(f), np.sin(ang).astype(f)
    return np.concatenate([cos, cos], axis=1), np.concatenate([-sin, sin], axis=1)


def _decay_tables(chunk):
    f = np.float32
    log_g = np.log1p(-np.exp2(f(-5.0) - np.arange(RET_HEADS, dtype=f))).astype(f)
    i = np.arange(chunk, dtype=f)
    diff = i[:, None] - i[None, :]
    mask = np.where(diff[None] >= 0, np.exp(np.maximum(diff, f(0.0))[None] * log_g[:, None, None]), f(0.0)).astype(f)
    q_dec = np.exp((i + f(1.0))[None, :] * log_g[:, None]).astype(f)
    k_dec = np.exp((f(chunk) - f(1.0) - i)[None, :] * log_g[:, None]).astype(f)
    c_dec = np.exp(f(chunk) * log_g).astype(f)
    return mask, q_dec, k_dec, c_dec


def kernel(x_prompt, x_sample, state_ret, state_pool, meta_tokens, norm_mix, w_in, ret_gn, pool_w, pool_scale,
           w_ret_branch, w_pool_branch, w_out, norm_ffn, w_router, b_router, w_gate_up, b_gate_up, w_down, b_down,
           norm_final):
    batch, seq, _ = x_prompt.shape
    nb = x_sample.shape[0]
    past_len = 16384
    n_prompt_tok = batch * seq
    n_tok = n_prompt_tok + nb

    w_in_bf = w_in[0].astype(BF16)
    wts = (ret_gn[0][None, :], pool_w[0].astype(BF16), pool_scale[0][None, :],
           w_ret_branch[0].astype(BF16), w_pool_branch[0].astype(BF16), w_out[0].astype(BF16),
           norm_ffn[0][None, :], w_router[0].T.astype(BF16), b_router[0][:, None])
    wgu = w_gate_up[0]
    wd = w_down[0]
    bgu = b_gate_up[0][:, None, :]
    bd = b_down[0][:, None, :]
    nmix = norm_mix[0][None, :]

    cos_p, sin_p = _rotary_tables(N_META + np.arange(seq))
    cos_s, sin_s = _rotary_tables(np.concatenate([np.arange(N_META), np.full((nb,), past_len)]))
    mask, q_dec, k_dec, c_dec = _decay_tables(CHUNK)
    dec = (mask,
           np.ascontiguousarray(np.broadcast_to(q_dec[:, :, None], (RET_HEADS, CHUNK, RET_DK))),
           np.ascontiguousarray(np.broadcast_to(k_dec[:, :, None], (RET_HEADS, CHUNK, RET_DK))),
           np.ascontiguousarray(np.broadcast_to(c_dec[:, None, None], (RET_HEADS, 1, RET_DV))))
    m1, q1, k1, c1 = _decay_tables(1)
    sdec = np.stack([m1[:, 0, 0], q1[:, 0], k1[:, 0], c1], axis=1)

    x2d = x_prompt.reshape(n_prompt_tok, D_MODEL)
    xs2d = x_sample.reshape(nb, D_MODEL)
    x_small = jnp.concatenate([meta_tokens, xs2d], axis=0)
    proj_small = _inproj(x_small, nmix, w_in_bf, cos_s, sin_s, N_META + nb, 1, F32)
    proj_s = tuple(a[N_META:] for a in proj_small)

    grp = nb // (n_prompt_tok // INPROJ_TILE)

    def cols(a):
        return a.T.reshape(RET_QK, nb // grp, grp).transpose(1, 0, 2)

    *proj_p, st_s, o_s = _inproj(
        x2d, nmix, w_in_bf, cos_p, sin_p, INPROJ_TILE, batch, BF16,
        sample=(sdec, cols(proj_s[0]), cols(proj_s[1]), proj_s[0], proj_s[1], proj_s[2], state_ret[0]))

    lead = CHUNK - N_META
    kmeta = jnp.pad(proj_small[1][:N_META], ((lead, 0), (0, 0))).astype(BF16)
    vmeta = jnp.pad(proj_small[2][:N_META], ((lead, 0), (0, 0))).astype(BF16)
    pmeta = proj_small[4][:N_META]

    pool2d = state_pool[0].reshape(nb, POOL_BUF * POOL_WIDTH)
    (pool_s, h1_s, xn2_s, idx_s, gate_s, rank_s, cnt_s) = _sample_mixer(o_s, proj_s, xs2d, pool2d, wts)

    (h1_p, xn2_p, idx_p, gate_p, rank_p, cnt, s_fin, p_fin) = _mixer(
        proj_p, x2d, kmeta, vmeta, pmeta, dec, wts, cnt_s, batch, seq)

    assert n_prompt_tok % ROW_TILE == 0 and nb <= ROW_TILE
    n_phantom = ROW_TILE - nb
    n_tiles = n_prompt_tok // ROW_TILE + 1
    i32 = jnp.int32
    counts = cnt[:, 0].astype(i32)
    padded = ((counts + MOE_BLOCK - 1) // MOE_BLOCK) * MOE_BLOCK
    pad_end = jnp.cumsum(padded)
    pad_start = pad_end - padded
    n_blocks = (n_tok * TOP_K) // MOE_BLOCK + N_EXPERTS
    block_row = jnp.arange(n_blocks, dtype=i32) * MOE_BLOCK
    block_e = jnp.minimum(jnp.sum((pad_end[None, :] <= block_row[:, None]).astype(i32), axis=1), N_EXPERTS - 1)
    nblk = (pad_end[-1:] // MOE_BLOCK).astype(i32)
    e_row = jnp.arange(N_EXPERTS, dtype=i32)
    used = padded > 0
    e_ord = jnp.cumsum(used.astype(i32)) - 1
    later_used = jnp.logical_and(e_row[None, :] > e_row[:, None], used[None, :])
    e_next = jnp.min(jnp.where(later_used, e_row[None, :], N_EXPERTS), axis=1)
    e_next = jnp.where(e_next == N_EXPERTS, -1, e_next)
    of_block = block_e[:, None] == e_row[None, :]
    block_eord = jnp.sum(jnp.where(of_block, e_ord[None, :], 0), axis=1)
    block_enext = jnp.sum(jnp.where(of_block, e_next[None, :], 0), axis=1)
    block_left = jnp.sum(jnp.where(of_block, (pad_start + counts)[None, :], 0), axis=1) - block_row
    block_valid = jnp.clip(block_left, 0, MOE_BLOCK)
    phantom = lambda fill, dt: jnp.full((TOP_K, n_phantom), fill, dt)
    idx = jnp.concatenate([idx_p, idx_s, phantom(-1, i32)], axis=1)
    rank = jnp.concatenate([rank_p, rank_s, phantom(0, i32)], axis=1)
    gates = jnp.concatenate([gate_p, gate_s, phantom(0.0, F32)], axis=1)
    onehot = idx[None] == jnp.arange(N_EXPERTS, dtype=i32)[:, None, None]
    tile_cnt = jnp.sum(onehot.reshape(N_EXPERTS, TOP_K, n_tiles, ROW_TILE).astype(i32), axis=(1, 3)).T
    by_time = jnp.concatenate([tile_cnt[-1:], tile_cnt[:-1]], axis=0)
    before_time = jnp.cumsum(by_time, axis=0) - by_time
    run_before = jnp.concatenate([before_time[1:], before_time[:1]], axis=0)
    tile_off = jnp.cumsum(tile_cnt, axis=1) - tile_cnt
    tile_dst = pad_start[None, :] + run_before
    delta = jnp.repeat((tile_off - run_before).T, ROW_TILE, axis=1)
    pos = rank + jnp.sum(jnp.where(onehot, delta[:, None, :], 0), axis=0)
    pos = jnp.where(idx >= 0, pos, -1)
    tcnt, tdst, ttot = tile_cnt.reshape(-1), tile_dst.reshape(-1), jnp.sum(tile_cnt, axis=1)

    tail = ((0, n_phantom), (0, 0))
    x_sorted = _dispatch(tcnt, tdst, ttot, pad_start + counts, padded - counts, nblk, pos, xn2_p,
                         jnp.pad(xn2_s, tail), n_blocks * MOE_BLOCK)
    y_sorted = _experts(block_e, nblk, block_eord, block_enext, block_valid, x_sorted, wgu, bgu, wd, bd)
    y_p, y_s = _combine(tcnt, tdst, ttot, pos, gates, y_sorted, h1_p, jnp.pad(h1_s, tail), norm_final[None, :])

    y_prompt = y_p.reshape(batch, seq, D_MODEL)
    y_sample = y_s[:nb].reshape(nb, 1, D_MODEL)
    ret_state_prompt = s_fin[None]
    pool_state_prompt = p_fin[:, 1:, :][None]
    ret_state_sample = st_s[None]
    pool_state_sample = pool_s.reshape(nb, POOL_BUF, POOL_WIDTH)[None]
    return (y_prompt, y_sample, ret_state_prompt, pool_state_prompt, ret_state_sample, pool_state_sample)
```

```python
import functools

import jax
import jax.numpy as jnp
import numpy as np
from jax import lax
from jax.experimental import pallas as pl
from jax.experimental.pallas import tpu as pltpu

F32 = jnp.float32
BF16 = jnp.bfloat16

D_MODEL = 1024
N_META = 16
RET_HEADS = 4
RET_DK = 128
RET_DV = 256
RET_QK = RET_HEADS * RET_DK
RET_V = RET_HEADS * RET_DV
CHUNK = 128
ROPE_BASE = 10000.0
POOL_WINDOWS = (2, 4, 8, 16)
POOL_GROUPS = 4
POOL_GROUP_DIM = 128
POOL_WIDTH = POOL_GROUPS * POOL_GROUP_DIM
POOL_BUF = max(POOL_WINDOWS) - 1
N_EXPERTS = 32
TOP_K = 4
D_FF = D_MODEL
SWIGLU_LIMIT = 7.0
SWIGLU_ALPHA = 1.702
EPS = 1e-6
IN_WIDTHS = (RET_QK, RET_QK, RET_V, RET_V, POOL_WIDTH, D_MODEL, D_MODEL)
IN_TOTAL = sum(IN_WIDTHS)
IN_OFFS = tuple(int(s) for s in np.cumsum((0,) + IN_WIDTHS))

LANES = 128
ROW_CHUNKS = D_MODEL // (2 * LANES)
INPROJ_TILE = 512
MIXER_TILE = 512
MOE_BLOCK = 512
ROW_TILE = 256
SUBLANES = 8
WEIGHT_CAST_ROWS = 128
VMEM_LIMIT = 56 * 1024 * 1024
PROMPT_VMEM_LIMIT = 60 * 1024 * 1024

assert N_META + 1 >= max(POOL_WINDOWS)
assert POOL_WINDOWS == (2, 4, 8, 16)


def _dot(a, b):
    return jnp.dot(a, b, preferred_element_type=F32)


def _rmsnorm(x, w):
    return x * lax.rsqrt(jnp.mean(x * x, axis=-1, keepdims=True) + EPS) * w


def _sigmoid(x):
    return 0.5 * jnp.tanh(0.5 * x) + 0.5


def _store_rows_as_tiles(ref, x):
    half = D_MODEL // 2
    hi = pltpu.bitcast(x[:, :half], jnp.uint32) & jnp.uint32(0xFFFF0000)
    lo = pltpu.bitcast(x[:, half:], jnp.uint32) >> 16
    ref[...] = (hi | lo).reshape(x.shape[0], ROW_CHUNKS, LANES)


def _load_rows_from_tiles(ref, rows):
    w = ref[...].reshape(rows, D_MODEL // 2)
    hi = pltpu.bitcast(w & jnp.uint32(0xFFFF0000), F32)
    lo = pltpu.bitcast(w << 16, F32)
    return jnp.concatenate([hi, lo], axis=1)


def _inproj_kernel(x_ref, nw_ref, w_ref, cos_ref, sin_ref,
                   q_ref, k_ref, v_ref, g_ref, p_ref, ga_ref, gb_ref):
    xn = _rmsnorm(x_ref[...], nw_ref[...]).astype(BF16)
    cos = cos_ref[...]
    sin = sin_ref[...]

    def seg(i):
        return _dot(xn, w_ref[:, IN_OFFS[i]:IN_OFFS[i + 1]])

    def rot(a):
        return a * cos + pltpu.roll(a, RET_DK // 2, 1) * sin

    q = seg(0)
    k = seg(1)
    for h in range(RET_HEADS):
        sl = slice(h * RET_DK, (h + 1) * RET_DK)
        q_ref[:, sl] = rot(q[:, sl]).astype(q_ref.dtype)
        k_ref[:, sl] = (rot(k[:, sl]) * (RET_DK ** -0.5)).astype(k_ref.dtype)
    v_ref[...] = seg(2).astype(v_ref.dtype)
    g = seg(3)
    g_ref[...] = (g * _sigmoid(g)).astype(g_ref.dtype)
    p_ref[...] = seg(4)
    ga_ref[...] = _sigmoid(seg(5)).astype(ga_ref.dtype)
    gb_ref[...] = _sigmoid(seg(6)).astype(gb_ref.dtype)


def _sample_state_step(step, sdec_ref, qt_ref, kt_ref, q_ref, k_ref, v_ref, st_ref, stout_ref, o_ref):
    grp = st_ref.shape[0]
    row0 = step * grp
    if grp % SUBLANES == 0:
        rows = pl.ds(pl.multiple_of(row0, SUBLANES), grp)
        q8, k8, v8 = q_ref[rows, :], k_ref[rows, :], v_ref[rows, :]
    else:
        assert 2 * grp == SUBLANES
        rows = pl.ds(pl.multiple_of((step // 2) * SUBLANES, SUBLANES), SUBLANES)
        first = step % 2 == 0
        q8, k8, v8 = (jnp.where(first, a[:grp], a[grp:]) for a in (q_ref[rows, :], k_ref[rows, :], v_ref[rows, :]))
    for h in range(RET_HEADS):
        ksl = slice(h * RET_DK, (h + 1) * RET_DK)
        vsl = slice(h * RET_DV, (h + 1) * RET_DV)
        score = jnp.sum(q8[:, ksl] * k8[:, ksl], axis=1, keepdims=True) * sdec_ref[h, 0]
        intra = score * v8[:, vsl]
        for bb in range(grp):
            s_old = st_ref[bb, h]
            qcol = qt_ref[0, ksl, bb:bb + 1] * sdec_ref[h, 1]
            kcol = kt_ref[0, ksl, bb:bb + 1] * sdec_ref[h, 2]
            cross = jnp.sum(s_old * qcol, axis=0, keepdims=True)
            o_ref[pl.ds(row0 + bb, 1), vsl] = intra[bb:bb + 1, :] + cross
            stout_ref[bb, h] = s_old * sdec_ref[h, 3] + kcol * v8[bb:bb + 1, vsl]


def _inproj_sample_kernel(x_ref, nw_ref, w_ref, cos_ref, sin_ref,
                          sdec_ref, qt_ref, kt_ref, qs_ref, ks_ref, vs_ref, st_ref,
                          q_ref, k_ref, v_ref, g_ref, p_ref, ga_ref, gb_ref, stout_ref, os_ref):
    _inproj_kernel(x_ref, nw_ref, w_ref, cos_ref, sin_ref, q_ref, k_ref, v_ref, g_ref, p_ref, ga_ref, gb_ref)
    step = pl.program_id(0) * pl.num_programs(1) + pl.program_id(1)
    _sample_state_step(step, sdec_ref, qt_ref, kt_ref, qs_ref, ks_ref, vs_ref, st_ref, stout_ref, os_ref)


def _inproj(x2d, nw, w_in_bf, cosf, sinf, tile, n_outer, act_dtype, sample=None):
    rows = x2d.shape[0]
    n_inner = rows // (tile * n_outer)
    row_map = lambda b, j: (b * n_inner + j, 0)
    tab_map = lambda b, j: (j, 0)
    const = lambda b, j: (0, 0)
    widths = IN_WIDTHS
    dts = (act_dtype, act_dtype, act_dtype, act_dtype, F32, act_dtype, act_dtype)
    in_arrays = [x2d, nw, w_in_bf, cosf, sinf]
    in_specs = [
        pl.BlockSpec((tile, D_MODEL), row_map),
        pl.BlockSpec((1, D_MODEL), const),
        pl.BlockSpec((D_MODEL, IN_TOTAL), const, pipeline_mode=pl.Buffered(1)),
        pl.BlockSpec((tile, RET_DK), tab_map),
        pl.BlockSpec((tile, RET_DK), tab_map),
    ]
    out_specs = [pl.BlockSpec((tile, w), row_map) for w in widths]
    out_shape = [jax.ShapeDtypeStruct((rows, w), dt) for w, dt in zip(widths, dts)]
    body = _inproj_kernel
    if sample is not None:
        sdec, qt, kt, qs, ks, vs, state = sample
        nb = state.shape[0]
        grp = nb // (n_outer * n_inner)
        assert grp * n_outer * n_inner == nb and qt.shape == (nb // grp, RET_QK, grp)
        step_map = lambda b, j: (b * n_inner + j, 0, 0)
        st_spec = pl.BlockSpec((grp, RET_HEADS, RET_DK, RET_DV), lambda b, j: (b * n_inner + j, 0, 0, 0))
        in_arrays += [sdec, qt, kt, qs, ks, vs, state]
        in_specs += [pl.BlockSpec(memory_space=pltpu.SMEM),
                     pl.BlockSpec((1, RET_QK, grp), step_map), pl.BlockSpec((1, RET_QK, grp), step_map),
                     pl.BlockSpec(qs.shape, const), pl.BlockSpec(ks.shape, const), pl.BlockSpec(vs.shape, const),
                     st_spec]
        out_specs += [st_spec, pl.BlockSpec((nb, RET_V), const)]
        out_shape += [jax.ShapeDtypeStruct(state.shape, F32), jax.ShapeDtypeStruct((nb, RET_V), F32)]
        body = _inproj_sample_kernel
    return pl.pallas_call(
        body,
        grid=(n_outer, n_inner),
        in_specs=in_specs,
        out_specs=out_specs,
        out_shape=out_shape,
        compiler_params=pltpu.CompilerParams(
            dimension_semantics=("arbitrary", "arbitrary"), vmem_limit_bytes=VMEM_LIMIT),
        name="inproj",
    )(*in_arrays)


def _group_norm(o, gn_row):
    mu = jnp.mean(o, axis=-1, keepdims=True)
    var = jnp.mean(jnp.square(o - mu), axis=-1, keepdims=True)
    return (o - mu) * lax.rsqrt(var + EPS) * gn_row


def _pool_branch(groups, poolw_ref, pscale_ref, wpool_ref):
    pm = [_dot(g.astype(BF16), poolw_ref[i]) for i, g in enumerate(groups)]
    pm = jnp.concatenate(pm, axis=1) * pscale_ref[...]
    return _dot(pm.astype(BF16), wpool_ref[...])


def _merge_tail(o_norm, silu_g, sig_a, sig_b, yb, x, wret_ref, wout_ref):
    ya = _dot((silu_g.astype(F32) * o_norm).astype(BF16), wret_ref[...])
    merged = sig_a.astype(F32) * ya + sig_b.astype(F32) * yb
    return x + _dot(merged.astype(BF16), wout_ref[...])


def _route(h1, nffn_ref, wrt_ref, br_ref, run_scr,
           xn2_ref, idx_ref, gate_ref, rank_ref):
    tm = h1.shape[0]
    xn2 = _rmsnorm(h1, nffn_ref[...]).astype(BF16)
    xn2_ref[...] = xn2
    logits = lax.dot_general(wrt_ref[...], xn2, (((1,), (1,)), ((), ())),
                             preferred_element_type=F32) + br_ref[...]
    e_iota = lax.broadcasted_iota(jnp.int32, (N_EXPERTS, tm), 0)
    work = logits
    vals, sels = [], []
    chosen = jnp.zeros((N_EXPERTS, tm), F32)
    for _ in range(TOP_K):
        m = jnp.max(work, axis=0, keepdims=True)
        sel = jnp.min(jnp.where(work == m, e_iota, N_EXPERTS), axis=0, keepdims=True)
        hit = e_iota == sel
        vals.append(m)
        sels.append(sel)
        chosen = jnp.where(hit, 1.0, chosen)
        work = jnp.where(hit, -jnp.inf, work)
    exps = [jnp.exp(v - vals[0]) for v in vals]
    denom = exps[0] + exps[1] + exps[2] + exps[3]
    gates = [e / denom for e in exps]
    r_i = lax.broadcasted_iota(jnp.int32, (tm, tm), 0)
    c_i = lax.broadcasted_iota(jnp.int32, (tm, tm), 1)
    before = jnp.where(r_i < c_i, 1.0, 0.0).astype(BF16)
    base = run_scr[...] + _dot(chosen.astype(BF16), before)
    for kk in range(TOP_K):
        rk = jnp.sum(jnp.where(e_iota == sels[kk], base, 0.0), axis=0, keepdims=True)
        rank_ref[kk:kk + 1, :] = rk.astype(jnp.int32)
        idx_ref[kk:kk + 1, :] = sels[kk]
        gate_ref[kk:kk + 1, :] = gates[kk]
    run_scr[...] = run_scr[...] + jnp.sum(chosen, axis=1, keepdims=True)


def _mixer_kernel(q_ref, k_ref, v_ref, g_ref, p_ref, ga_ref, gb_ref, x_ref,
                  kmeta_ref, vmeta_ref, pmeta_ref, mask_ref, qdec_ref, kdec_ref, cdec_ref, gn_ref,
                  poolw_ref, pscale_ref, wret_ref, wpool_ref, wout_ref, nffn_ref, wrt_ref, br_ref,
                  cnt0_ref,
                  h1_ref, xn2_ref, idx_ref, gate_ref, rank_ref, cnt_ref, sfin_ref, pfin_ref,
                  s_scr, ext_scr, o_scr, run_scr):
    b = pl.program_id(0)
    j = pl.program_id(1)
    nj = pl.num_programs(1)
    tm = q_ref.shape[0]

    def state_update(s_old, kc, vc, h):
        kd = (kc.astype(F32) * kdec_ref[h]).astype(BF16)
        upd = lax.dot_general(kd, vc, (((0,), (0,)), ((), ())), preferred_element_type=F32)
        return s_old * cdec_ref[h] + upd

    @pl.when(jnp.logical_and(b == 0, j == 0))
    def _():
        run_scr[...] = cnt0_ref[:, 0:1]

    @pl.when(j == 0)
    def _():
        for h in range(RET_HEADS):
            kc = kmeta_ref[:, h * RET_DK:(h + 1) * RET_DK]
            vc = vmeta_ref[:, h * RET_DV:(h + 1) * RET_DV]
            s_scr[h] = state_update(jnp.zeros((RET_DK, RET_DV), F32), kc, vc, h)
        ext_scr[0:N_META, :] = pmeta_ref[...]

    n_chunks = tm // CHUNK
    units = [(c, h) for c in range(n_chunks) for h in range(RET_HEADS)]

    def operands(c, h):
        rows = slice(c * CHUNK, (c + 1) * CHUNK)
        return (q_ref[rows, h * RET_DK:(h + 1) * RET_DK], k_ref[rows, h * RET_DK:(h + 1) * RET_DK],
                v_ref[rows, h * RET_DV:(h + 1) * RET_DV])

    lhs, upd = {}, {}
    for c, h in units:
        qc, kc, vc = operands(c, h)
        scores = lax.dot_general(qc, kc, (((1,), (1,)), ((), ())), preferred_element_type=F32) * mask_ref[h]
        qd = (qc.astype(F32) * qdec_ref[h]).astype(BF16)
        lhs[c, h] = jnp.concatenate([scores.astype(BF16), qd], axis=1)
        kd = (kc.astype(F32) * kdec_ref[h]).astype(BF16)
        upd[c, h] = lax.dot_general(kd, vc, (((0,), (0,)), ((), ())), preferred_element_type=F32)
    state = {}
    for h in range(RET_HEADS):
        s = s_scr[h]
        for c in range(n_chunks):
            state[c, h] = s
            s = s * cdec_ref[h] + upd[c, h]
        s_scr[h] = s
    for c, h in units:
        rhs = jnp.concatenate([operands(c, h)[2], state[c, h].astype(BF16)], axis=0)
        o = _dot(lhs[c, h], rhs)
        o_scr[c * CHUNK:(c + 1) * CHUNK, h * RET_DV:(h + 1) * RET_DV] = _group_norm(
            o, gn_ref[:, h * RET_DV:(h + 1) * RET_DV])

    p = p_ref[...]
    ext_scr[N_META:N_META + tm, :] = p
    a = ext_scr[...]
    g1 = POOL_GROUP_DIM
    s2 = a + pltpu.roll(a, 1, 0)
    s4 = s2[:, g1:] + pltpu.roll(s2[:, g1:], 2, 0)
    s8 = s4[:, g1:] + pltpu.roll(s4[:, g1:], 4, 0)
    s16 = s8[:, g1:] + pltpu.roll(s8[:, g1:], 8, 0)
    sums = (s2[N_META:, :g1], s4[N_META:, :g1], s8[N_META:, :g1], s16[N_META:, :])
    groups = [sums[i] * (1.0 / POOL_WINDOWS[i]) - p[:, i * g1:(i + 1) * g1] for i in range(POOL_GROUPS)]
    ext_scr[0:N_META, :] = ext_scr[tm:tm + N_META, :]

    yb = _pool_branch(groups, poolw_ref, pscale_ref, wpool_ref)
    h1 = _merge_tail(o_scr[...], g_ref[...], ga_ref[...], gb_ref[...], yb, x_ref[...], wret_ref, wout_ref)
    h1_ref[...] = h1
    _route(h1, nffn_ref, wrt_ref, br_ref, run_scr, xn2_ref, idx_ref, gate_ref, rank_ref)
    cnt_ref[...] = jnp.broadcast_to(run_scr[...], cnt_ref.shape)

    @pl.when(j == nj - 1)
    def _():
        for h in range(RET_HEADS):
            sfin_ref[0, h] = s_scr[h]
        pfin_ref[0] = ext_scr[0:N_META, :]


def _mixer(proj, x2d, kmeta, vmeta, pmeta, dec, wts, cnt0, batch, seq):
    q, k, v, g, p, ga, gb = proj
    tm = MIXER_TILE
    nj = seq // tm
    rows = batch * seq
    row_map = lambda b, j: (b * nj + j, 0)
    lane_map = lambda b, j: (0, b * nj + j)
    c2 = lambda b, j: (0, 0)
    c3 = lambda b, j: (0, 0, 0)

    def whole(a):
        return pl.BlockSpec(a.shape, c2 if a.ndim == 2 else c3)

    mask, qdec, kdec, cdec = dec
    gn, poolw, pscale, wret, wpool, wout, nffn, wrt, br = wts
    in_arrays = [q, k, v, g, p, ga, gb, x2d, kmeta, vmeta, pmeta, mask, qdec, kdec, cdec, gn,
                 poolw, pscale, wret, wpool, wout, nffn, wrt, br, cnt0]
    in_specs = [pl.BlockSpec((tm, a.shape[1]), row_map) for a in in_arrays[:8]]
    in_specs += [whole(a) for a in in_arrays[8:]]
    out_shape = [
        jax.ShapeDtypeStruct((rows, D_MODEL), F32),
        jax.ShapeDtypeStruct((rows, D_MODEL), BF16),
        jax.ShapeDtypeStruct((TOP_K, rows), jnp.int32),
        jax.ShapeDtypeStruct((TOP_K, rows), F32),
        jax.ShapeDtypeStruct((TOP_K, rows), jnp.int32),
        jax.ShapeDtypeStruct((N_EXPERTS, LANES), F32),
        jax.ShapeDtypeStruct((batch, RET_HEADS, RET_DK, RET_DV), F32),
        jax.ShapeDtypeStruct((batch, N_META, POOL_WIDTH), F32),
    ]
    out_specs = [
        pl.BlockSpec((tm, D_MODEL), row_map),
        pl.BlockSpec((tm, D_MODEL), row_map),
        pl.BlockSpec((TOP_K, tm), lane_map),
        pl.BlockSpec((TOP_K, tm), lane_map),
        pl.BlockSpec((TOP_K, tm), lane_map),
        pl.BlockSpec((N_EXPERTS, LANES), c2),
        pl.BlockSpec((1, RET_HEADS, RET_DK, RET_DV), lambda b, j: (b, 0, 0, 0)),
        pl.BlockSpec((1, N_META, POOL_WIDTH), lambda b, j: (b, 0, 0)),
    ]
    return pl.pallas_call(
        _mixer_kernel,
        grid=(batch, nj),
        in_specs=in_specs,
        out_specs=out_specs,
        out_shape=out_shape,
        scratch_shapes=[
            pltpu.VMEM((RET_HEADS, RET_DK, RET_DV), F32),
            pltpu.VMEM((N_META + tm, POOL_WIDTH), F32),
            pltpu.VMEM((tm, RET_V), F32),
            pltpu.VMEM((N_EXPERTS, 1), F32),
        ],
        compiler_params=pltpu.CompilerParams(
            dimension_semantics=("arbitrary", "arbitrary"), vmem_limit_bytes=VMEM_LIMIT),
        name="mixer",
    )(*in_arrays)


N_SAMPLE_IN = 7
N_MIXER_IN = 17
N_MIXER_OUT = 8
N_PROJ = len(IN_WIDTHS)


def _prompt_kernel(*refs):
    it = iter(refs)
    take = lambda n: [next(it) for _ in range(n)]
    x_ref, nw_ref, w_ref, cos_ref, sin_ref = take(5)
    sample_in = take(N_SAMPLE_IN)
    mixer_in = take(N_MIXER_IN)
    mixer_out = take(N_MIXER_OUT)
    stout_ref, os_ref = take(2)
    proj_scr = take(N_PROJ)
    mixer_scr = take(4)
    _inproj_kernel(x_ref, nw_ref, w_ref, cos_ref, sin_ref, *proj_scr)
    step = pl.program_id(0) * pl.num_programs(1) + pl.program_id(1)
    _sample_state_step(step, *sample_in, stout_ref, os_ref)
    _mixer_kernel(*proj_scr, x_ref, *mixer_in, *mixer_out, *mixer_scr)


def _prompt_layer(x2d, nw, w_in_bf, cosf, sinf, sample, kmeta, vmeta, pmeta, dec, wts, cnt0, batch, seq):
    tm = MIXER_TILE
    nj = seq // tm
    rows = batch * seq
    row_map = lambda b, j: (b * nj + j, 0)
    lane_map = lambda b, j: (0, b * nj + j)
    c2 = lambda b, j: (0, 0)
    c3 = lambda b, j: (0, 0, 0)

    def whole(a):
        mode = dict(pipeline_mode=pl.Buffered(1)) if a.size * a.dtype.itemsize >= 512 * 1024 else {}
        return pl.BlockSpec(a.shape, c2 if a.ndim == 2 else c3, **mode)

    sdec, qt, kt, qs, ks, vs, state = sample
    nb = state.shape[0]
    grp = nb // (batch * nj)
    assert grp * batch * nj == nb and qt.shape == (nb // grp, RET_QK, grp)
    step_map = lambda b, j: (b * nj + j, 0, 0)
    st_spec = pl.BlockSpec((grp, RET_HEADS, RET_DK, RET_DV), lambda b, j: (b * nj + j, 0, 0, 0))
    mask, qdec, kdec, cdec = dec
    mixer_in = [kmeta, vmeta, pmeta, mask, qdec, kdec, cdec, *wts, cnt0]
    assert len(mixer_in) == N_MIXER_IN
    in_arrays = [x2d, nw, w_in_bf, cosf, sinf, sdec, qt, kt, qs, ks, vs, state] + mixer_in
    in_specs = [
        pl.BlockSpec((tm, D_MODEL), row_map),
        pl.BlockSpec((1, D_MODEL), c2),
        pl.BlockSpec((D_MODEL, IN_TOTAL), c2, pipeline_mode=pl.Buffered(1)),
        pl.BlockSpec((tm, RET_DK), lambda b, j: (j, 0)),
        pl.BlockSpec((tm, RET_DK), lambda b, j: (j, 0)),
        pl.BlockSpec(memory_space=pltpu.SMEM),
        pl.BlockSpec((1, RET_QK, grp), step_map), pl.BlockSpec((1, RET_QK, grp), step_map),
        pl.BlockSpec(qs.shape, c2), pl.BlockSpec(ks.shape, c2), pl.BlockSpec(vs.shape, c2),
        st_spec,
    ] + [whole(a) for a in mixer_in]
    out_shape = [
        jax.ShapeDtypeStruct((rows, D_MODEL), F32),
        jax.ShapeDtypeStruct((rows, D_MODEL), BF16),
        jax.ShapeDtypeStruct((TOP_K, rows), jnp.int32),
        jax.ShapeDtypeStruct((TOP_K, rows), F32),
        jax.ShapeDtypeStruct((TOP_K, rows), jnp.int32),
        jax.ShapeDtypeStruct((N_EXPERTS, LANES), F32),
        jax.ShapeDtypeStruct((batch, RET_HEADS, RET_DK, RET_DV), F32),
        jax.ShapeDtypeStruct((batch, N_META, POOL_WIDTH), F32),
        jax.ShapeDtypeStruct(state.shape, F32),
        jax.ShapeDtypeStruct((nb, RET_V), F32),
    ]
    out_specs = [
        pl.BlockSpec((tm, D_MODEL), row_map),
        pl.BlockSpec((tm, D_MODEL), row_map),
        pl.BlockSpec((TOP_K, tm), lane_map),
        pl.BlockSpec((TOP_K, tm), lane_map),
        pl.BlockSpec((TOP_K, tm), lane_map),
        pl.BlockSpec((N_EXPERTS, LANES), c2),
        pl.BlockSpec((1, RET_HEADS, RET_DK, RET_DV), lambda b, j: (b, 0, 0, 0)),
        pl.BlockSpec((1, N_META, POOL_WIDTH), lambda b, j: (b, 0, 0)),
        st_spec,
        pl.BlockSpec((nb, RET_V), c2),
    ]
    proj_dts = (BF16, BF16, BF16, BF16, F32, BF16, BF16)
    return pl.pallas_call(
        _prompt_kernel,
        grid=(batch, nj),
        in_specs=in_specs,
        out_specs=out_specs,
        out_shape=out_shape,
        scratch_shapes=[pltpu.VMEM((tm, w), dt) for w, dt in zip(IN_WIDTHS, proj_dts)] + [
            pltpu.VMEM((RET_HEADS, RET_DK, RET_DV), F32),
            pltpu.VMEM((N_META + tm, POOL_WIDTH), F32),
            pltpu.VMEM((tm, RET_V), F32),
            pltpu.VMEM((N_EXPERTS, 1), F32),
        ],
        compiler_params=pltpu.CompilerParams(
            dimension_semantics=("arbitrary", "arbitrary"), vmem_limit_bytes=PROMPT_VMEM_LIMIT),
        name="prompt_layer",
    )(*in_arrays)


def _sample_kernel(o_ref, g_ref, p_ref, ga_ref, gb_ref, x_ref, pool_ref, gn_ref,
                   poolw_ref, pscale_ref, wret_ref, wpool_ref, wout_ref, nffn_ref, wrt_ref, br_ref, cnt0_ref,
                   poolout_ref, h1_ref, xn2_ref, idx_ref, gate_ref, rank_ref, cnt_ref,
                   run_scr):
    run_scr[...] = cnt0_ref[:, 0:1]
    o = o_ref[...]
    o_norm = jnp.concatenate(
        [_group_norm(o[:, h * RET_DV:(h + 1) * RET_DV], gn_ref[:, h * RET_DV:(h + 1) * RET_DV])
         for h in range(RET_HEADS)], axis=1)
    p = p_ref[...]
    w = POOL_WIDTH
    g1 = POOL_GROUP_DIM

    def prev(r, lo):
        return pool_ref[:, r * w + lo:(r + 1) * w]

    s2 = p + prev(14, 0)
    s4 = s2[:, g1:] + prev(13, g1) + prev(12, g1)
    s8 = s4[:, g1:] + prev(11, 2 * g1) + prev(10, 2 * g1) + prev(9, 2 * g1) + prev(8, 2 * g1)
    s16 = s8[:, g1:]
    for r in range(7, -1, -1):
        s16 = s16 + prev(r, 3 * g1)
    sums = (s2[:, :g1], s4[:, :g1], s8[:, :g1], s16)
    groups = [sums[t] * (1.0 / POOL_WINDOWS[t]) - p[:, t * g1:(t + 1) * g1] for t in range(POOL_GROUPS)]
    poolout_ref[:, 0:(POOL_BUF - 1) * w] = pool_ref[:, w:POOL_BUF * w]
    poolout_ref[:, (POOL_BUF - 1) * w:] = p
    yb = _pool_branch(groups, poolw_ref, pscale_ref, wpool_ref)
    h1 = _merge_tail(o_norm, g_ref[...], ga_ref[...], gb_ref[...], yb, x_ref[...], wret_ref, wout_ref)
    h1_ref[...] = h1
    _route(h1, nffn_ref, wrt_ref, br_ref, run_scr, xn2_ref, idx_ref, gate_ref, rank_ref)
    cnt_ref[...] = jnp.broadcast_to(run_scr[...], cnt_ref.shape)


def _sample_mixer(o_s, proj, x2d, pool2d, wts, cnt0):
    _, _, _, g, p, ga, gb = proj
    nb = x2d.shape[0]
    c2 = lambda i: (0, 0)

    def whole(a):
        return pl.BlockSpec(a.shape, c2)

    gn, poolw, pscale, wret, wpool, wout, nffn, wrt, br = wts
    in_arrays = [o_s, g, p, ga, gb, x2d, pool2d, gn, poolw, pscale, wret, wpool, wout, nffn, wrt, br, cnt0]
    in_specs = [whole(a) for a in in_arrays[:8]]
    in_specs += [pl.BlockSpec(poolw.shape, lambda i: (0, 0, 0))]
    in_specs += [whole(a) for a in in_arrays[9:]]
    out_shape = [
        jax.ShapeDtypeStruct(pool2d.shape, F32),
        jax.ShapeDtypeStruct((nb, D_MODEL), F32),
        jax.ShapeDtypeStruct((nb, D_MODEL), BF16),
        jax.ShapeDtypeStruct((TOP_K, nb), jnp.int32),
        jax.ShapeDtypeStruct((TOP_K, nb), F32),
        jax.ShapeDtypeStruct((TOP_K, nb), jnp.int32),
        jax.ShapeDtypeStruct((N_EXPERTS, LANES), F32),
    ]
    return pl.pallas_call(
        _sample_kernel,
        grid=(1,),
        in_specs=in_specs,
        out_specs=[pl.BlockSpec(s.shape, c2) for s in out_shape],
        out_shape=out_shape,
        scratch_shapes=[pltpu.VMEM((N_EXPERTS, 1), F32)],
        compiler_params=pltpu.CompilerParams(
            dimension_semantics=("arbitrary",), vmem_limit_bytes=VMEM_LIMIT),
        name="sample_mixer",
    )(*in_arrays)


TILE_ROWS = ROW_TILE * TOP_K


def _run_copy(src_ref, src_row, dst_ref, dst_row, n_rows, sem):
    return pltpu.make_async_copy(src_ref.at[pl.ds(src_row, n_rows)], dst_ref.at[pl.ds(dst_row, n_rows)], sem)


def _for_each_run(tcnt_ref, tile, fn):
    def body(e, off):
        n = tcnt_ref[tile * N_EXPERTS + e]

        @pl.when(n > 0)
        def _():
            fn(e, off, n)

        return off + n

    lax.fori_loop(0, N_EXPERTS, body, 0)


def _dispatch_kernel(tcnt_ref, tdst_ref, ttot_ref, zrow_ref, zcnt_ref, nblk_ref, pos_ref, xp_ref, xs_ref, out_ref,
                     sorted_scr, zero_scr, sem, zsem):
    i = pl.program_id(0)
    n_prompt = pl.num_programs(0) - 1
    n_blocks = out_ref.shape[0] // MOE_BLOCK
    slot = i % 2

    def wait_tile(tile, s):
        _run_copy(out_ref, 0, sorted_scr.at[s], 0, ttot_ref[tile], sem.at[s]).wait()

    def for_each_pad(fn):
        def body(e, carry):
            n = zcnt_ref[e]

            @pl.when(n > 0)
            def _():
                fn(_run_copy(zero_scr, 0, out_ref, zrow_ref[e], n, zsem))

            dead = nblk_ref[0] + e

            @pl.when(dead < n_blocks)
            def _():
                fn(_run_copy(zero_scr, 0, out_ref, dead * MOE_BLOCK, MOE_BLOCK, zsem))

            return carry

        lax.fori_loop(0, N_EXPERTS, body, 0)

    @pl.when(i == 0)
    def _():
        zero_scr[...] = jnp.zeros_like(zero_scr)
        for_each_pad(lambda cp: cp.start())

    def sort_tile(x_ref):
        r_iota = lax.broadcasted_iota(jnp.int32, (TILE_ROWS, ROW_TILE), 0)
        hit = r_iota == pos_ref[0:1, :]
        for kk in range(1, TOP_K):
            hit = jnp.logical_or(hit, r_iota == pos_ref[kk:kk + 1, :])
        perm = jnp.where(hit, 1.0, 0.0).astype(BF16)
        xs = _dot(perm, x_ref[...])

        @pl.when(i >= 2)
        def _():
            wait_tile(i - 2, slot)

        _store_rows_as_tiles(sorted_scr.at[slot], xs)

    @pl.when(i < n_prompt)
    def _():
        sort_tile(xp_ref)

    @pl.when(i == n_prompt)
    def _():
        sort_tile(xs_ref)

    _for_each_run(tcnt_ref, i, lambda e, off, n: _run_copy(
        sorted_scr.at[slot], off, out_ref, tdst_ref[i * N_EXPERTS + e], n, sem.at[slot]).start())

    @pl.when(i == n_prompt)
    def _():
        wait_tile(i - 1, 1 - slot)
        wait_tile(i, slot)
        for_each_pad(lambda cp: cp.wait())


def _dispatch(tcnt, tdst, ttot, zrow, zcnt, nblk, pos, xn2_p, xn2_s, n_sorted):
    n_prompt = xn2_p.shape[0] // ROW_TILE
    grid_spec = pltpu.PrefetchScalarGridSpec(
        num_scalar_prefetch=6,
        grid=(n_prompt + 1,),
        in_specs=[
            pl.BlockSpec((TOP_K, ROW_TILE), lambda i, *_: (0, i)),
            pl.BlockSpec((ROW_TILE, D_MODEL), lambda i, *_: (jnp.minimum(i, n_prompt - 1), 0)),
            pl.BlockSpec((ROW_TILE, D_MODEL), lambda i, *_: (0, 0)),
        ],
        out_specs=pl.BlockSpec(memory_space=pl.ANY),
        scratch_shapes=[
            pltpu.VMEM((2, TILE_ROWS, ROW_CHUNKS, LANES), jnp.uint32),
            pltpu.VMEM((MOE_BLOCK, ROW_CHUNKS, LANES), jnp.uint32),
            pltpu.SemaphoreType.DMA((2,)), pltpu.SemaphoreType.DMA(()),
        ],
    )
    return pl.pallas_call(
        _dispatch_kernel,
        grid_spec=grid_spec,
        out_shape=jax.ShapeDtypeStruct((n_sorted, ROW_CHUNKS, LANES), jnp.uint32),
        compiler_params=pltpu.CompilerParams(dimension_semantics=("arbitrary",)),
        name="moe_dispatch",
    )(tcnt, tdst, ttot, zrow, zcnt, nblk, pos, xn2_p, xn2_s)


def _expert_kernel(be_ref, nblk_ref, eord_ref, enext_ref, bvalid_ref, x_ref, wgu_ref, bgu_ref, wd_ref, bd_ref, y_ref,
                   wgu_f32, wd_f32, wgu_bf, wd_bf, wsem):
    i = pl.program_id(0)
    live = i < nblk_ref[0]
    new_expert = jnp.logical_or(i == 0, be_ref[i] != be_ref[jnp.maximum(i - 1, 0)])

    def weight_copies(e, s):
        return (pltpu.make_async_copy(wgu_ref.at[e], wgu_f32.at[s], wsem.at[0, s]),
                pltpu.make_async_copy(wd_ref.at[e], wd_f32.at[s], wsem.at[1, s]))

    @pl.when(jnp.logical_and(live, new_expert))
    def _():
        slot = eord_ref[i] % 2

        @pl.when(i == 0)
        def _():
            for cp in weight_copies(be_ref[0], 0):
                cp.start(priority=1)

        for cp in weight_copies(be_ref[i], slot):
            cp.wait()

        @pl.when(enext_ref[i] >= 0)
        def _():
            for cp in weight_copies(enext_ref[i], 1 - slot):
                cp.start(priority=1)

        def cast(c, carry):
            rows = pl.ds(pl.multiple_of(c * WEIGHT_CAST_ROWS, WEIGHT_CAST_ROWS), WEIGHT_CAST_ROWS)
            wgu_bf[rows, :] = wgu_f32[slot, rows, :].astype(BF16)
            wd_bf[rows, :] = wd_f32[slot, rows, :].astype(BF16)
            return carry

        lax.fori_loop(0, D_MODEL // WEIGHT_CAST_ROWS, cast, 0)

    @pl.when(jnp.logical_not(live))
    def _():
        y_ref[...] = jnp.zeros_like(y_ref)

    def ffn(rows):
        x = _load_rows_from_tiles(x_ref.at[pl.ds(0, rows)], rows)
        h = _dot(x.astype(BF16), wgu_bf[...]) + bgu_ref[0]
        gate = jnp.minimum(h[:, :D_FF], SWIGLU_LIMIT)
        up = jnp.clip(h[:, D_FF:], -SWIGLU_LIMIT, SWIGLU_LIMIT)
        glu = gate * _sigmoid(gate * SWIGLU_ALPHA)
        y = _dot(((up + 1.0) * glu).astype(BF16), wd_bf[...]) + bd_ref[0]
        _store_rows_as_tiles(y_ref.at[pl.ds(0, rows)], y.astype(BF16).astype(F32))

    half = MOE_BLOCK // 2
    short = bvalid_ref[i] <= half

    @pl.when(jnp.logical_and(live, jnp.logical_not(short)))
    def _():
        ffn(MOE_BLOCK)

    @pl.when(jnp.logical_and(live, short))
    def _():
        ffn(half)
        y_ref[pl.ds(half, half)] = jnp.zeros((half,) + y_ref.shape[1:], y_ref.dtype)


def _experts(block_e, nblk, block_eord, block_enext, block_valid, x_sorted, wgu, bgu, wd, bd):
    n_blocks = x_sorted.shape[0] // MOE_BLOCK
    wmap = lambda i, be, *_: (be[i], 0, 0)
    rmap = lambda i, *_: (i, 0, 0)
    hbm = pl.BlockSpec(memory_space=pl.ANY)
    grid_spec = pltpu.PrefetchScalarGridSpec(
        num_scalar_prefetch=5,
        grid=(n_blocks,),
        in_specs=[
            pl.BlockSpec((MOE_BLOCK, ROW_CHUNKS, LANES), rmap),
            hbm,
            pl.BlockSpec((1, 1, 2 * D_FF), wmap),
            hbm,
            pl.BlockSpec((1, 1, D_MODEL), wmap),
        ],
        out_specs=pl.BlockSpec((MOE_BLOCK, ROW_CHUNKS, LANES), rmap),
        scratch_shapes=[
            pltpu.VMEM((2, D_MODEL, 2 * D_FF), F32), pltpu.VMEM((2, D_FF, D_MODEL), F32),
            pltpu.VMEM((D_MODEL, 2 * D_FF), BF16), pltpu.VMEM((D_FF, D_MODEL), BF16),
            pltpu.SemaphoreType.DMA((2, 2)),
        ],
    )
    return pl.pallas_call(
        _expert_kernel,
        grid_spec=grid_spec,
        out_shape=jax.ShapeDtypeStruct(x_sorted.shape, jnp.uint32),
        compiler_params=pltpu.CompilerParams(
            dimension_semantics=("arbitrary",), vmem_limit_bytes=VMEM_LIMIT),
        name="moe_experts",
    )(block_e, nblk, block_eord, block_enext, block_valid, x_sorted, wgu, bgu, wd, bd)


def _combine_kernel(tcnt_ref, tdst_ref, ttot_ref, pos_ref, gate_ref, ys_ref, hp_ref, hs_ref, nf_ref, yp_ref, ysmp_ref,
                    runs_scr, sem):
    i = pl.program_id(0)
    n_tiles = pl.num_programs(0)
    n_prompt = n_tiles - 1
    slot = i % 2

    def start_runs(tile, s):
        _for_each_run(tcnt_ref, tile, lambda e, off, n: _run_copy(
            ys_ref, tdst_ref[tile * N_EXPERTS + e], runs_scr.at[s], off, n, sem.at[s]).start())

    @pl.when(i == 0)
    def _():
        start_runs(0, 0)

    @pl.when(i + 1 < n_tiles)
    def _():
        start_runs(i + 1, 1 - slot)

    _run_copy(ys_ref, 0, runs_scr.at[slot], 0, ttot_ref[i], sem.at[slot]).wait()
    ys = _load_rows_from_tiles(runs_scr.at[slot], TILE_ROWS).astype(BF16)
    r_iota = lax.broadcasted_iota(jnp.int32, (TILE_ROWS, ROW_TILE), 0)
    gmat = jnp.zeros((TILE_ROWS, ROW_TILE), F32)
    for kk in range(TOP_K):
        gmat = jnp.where(r_iota == pos_ref[kk:kk + 1, :], gate_ref[kk:kk + 1, :], gmat)
    moe = lax.dot_general(gmat.astype(BF16), ys, (((0,), (0,)), ((), ())), preferred_element_type=F32)

    @pl.when(i < n_prompt)
    def _():
        yp_ref[...] = _rmsnorm(hp_ref[...] + moe, nf_ref[...])

    @pl.when(i == n_prompt)
    def _():
        ysmp_ref[...] = _rmsnorm(hs_ref[...] + moe, nf_ref[...])


def _combine(tcnt, tdst, ttot, pos, gates, y_sorted, h1_p, h1_s, nf):
    n_prompt = h1_p.shape[0] // ROW_TILE
    pmap = lambda i, *_: (jnp.minimum(i, n_prompt - 1), 0)
    smap = lambda i, *_: (0, 0)
    lmap = lambda i, *_: (0, i)
    grid_spec = pltpu.PrefetchScalarGridSpec(
        num_scalar_prefetch=3,
        grid=(n_prompt + 1,),
        in_specs=[
            pl.BlockSpec((TOP_K, ROW_TILE), lmap),
            pl.BlockSpec((TOP_K, ROW_TILE), lmap),
            pl.BlockSpec(memory_space=pl.ANY),
            pl.BlockSpec((ROW_TILE, D_MODEL), pmap),
            pl.BlockSpec((ROW_TILE, D_MODEL), smap),
            pl.BlockSpec((1, D_MODEL), smap),
        ],
        out_specs=[pl.BlockSpec((ROW_TILE, D_MODEL), pmap), pl.BlockSpec((ROW_TILE, D_MODEL), smap)],
        scratch_shapes=[pltpu.VMEM((2, TILE_ROWS, ROW_CHUNKS, LANES), jnp.uint32), pltpu.SemaphoreType.DMA((2,))],
    )
    return pl.pallas_call(
        _combine_kernel,
        grid_spec=grid_spec,
        out_shape=[jax.ShapeDtypeStruct(h1_p.shape, F32), jax.ShapeDtypeStruct(h1_s.shape, F32)],
        compiler_params=pltpu.CompilerParams(dimension_semantics=("arbitrary",)),
        name="moe_combine",
    )(tcnt, tdst, ttot, pos, gates, y_sorted, h1_p, h1_s, nf)


def _rotary_tables(pos):
    f = np.float32
    inv = np.power(f(ROPE_BASE), -np.arange(0, RET_DK, 2, dtype=f) / f(RET_DK)).astype(f)
    ang = (np.asarray(pos, f)[:, None] * inv[None, :]).astype(f)
    cos, sin = np.cos(ang).astype(f), np.sin(ang).astype(f)
    return np.concatenate([cos, cos], axis=1), np.concatenate([-sin, sin], axis=1)


def _decay_tables(chunk):
    f = np.float32
    log_g = np.log1p(-np.exp2(f(-5.0) - np.arange(RET_HEADS, dtype=f))).astype(f)
    i = np.arange(chunk, dtype=f)
    diff = i[:, None] - i[None, :]
    mask = np.where(diff[None] >= 0, np.exp(np.maximum(diff, f(0.0))[None] * log_g[:, None, None]), f(0.0)).astype(f)
    q_dec = np.exp((i + f(1.0))[None, :] * log_g[:, None]).astype(f)
    k_dec = np.exp((f(chunk) - f(1.0) - i)[None, :] * log_g[:, None]).astype(f)
    c_dec = np.exp(f(chunk) * log_g).astype(f)
    return mask, q_dec, k_dec, c_dec


def kernel(x_prompt, x_sample, state_ret, state_pool, meta_tokens, norm_mix, w_in, ret_gn, pool_w, pool_scale,
           w_ret_branch, w_pool_branch, w_out, norm_ffn, w_router, b_router, w_gate_up, b_gate_up, w_down, b_down,
           norm_final):
    batch, seq, _ = x_prompt.shape
    nb = x_sample.shape[0]
    past_len = 16384
    n_prompt_tok = batch * seq
    n_tok = n_prompt_tok + nb

    w_in_bf = w_in[0].astype(BF16)
    wts = (ret_gn[0][None, :], pool_w[0].astype(BF16), pool_scale[0][None, :],
           w_ret_branch[0].astype(BF16), w_pool_branch[0].astype(BF16), w_out[0].astype(BF16),
           norm_ffn[0][None, :], w_router[0].T.astype(BF16), b_router[0][:, None])
    wgu = w_gate_up[0]
    wd = w_down[0]
    bgu = b_gate_up[0][:, None, :]
    bd = b_down[0][:, None, :]
    nmix = norm_mix[0][None, :]

    cos_p, sin_p = _rotary_tables(N_META + np.arange(seq))
    cos_s, sin_s = _rotary_tables(np.concatenate([np.arange(N_META), np.full((nb,), past_len)]))
    mask, q_dec, k_dec, c_dec = _decay_tables(CHUNK)
    dec = (mask,
           np.ascontiguousarray(np.broadcast_to(q_dec[:, :, None], (RET_HEADS, CHUNK, RET_DK))),
           np.ascontiguousarray(np.broadcast_to(k_dec[:, :, None], (RET_HEADS, CHUNK, RET_DK))),
           np.ascontiguousarray(np.broadcast_to(c_dec[:, None, None], (RET_HEADS, 1, RET_DV))))
    m1, q1, k1, c1 = _decay_tables(1)
    sdec = np.stack([m1[:, 0, 0], q1[:, 0], k1[:, 0], c1], axis=1)

    x2d = x_prompt.reshape(n_prompt_tok, D_MODEL)
    xs2d = x_sample.reshape(nb, D_MODEL)
    x_small = jnp.concatenate([meta_tokens, xs2d], axis=0)
    proj_small = _inproj(x_small, nmix, w_in_bf, cos_s, sin_s, N_META + nb, 1, F32)
    proj_s = tuple(a[N_META:] for a in proj_small)

    lead = CHUNK - N_META
    kmeta = jnp.pad(proj_small[1][:N_META], ((lead, 0), (0, 0))).astype(BF16)
    vmeta = jnp.pad(proj_small[2][:N_META], ((lead, 0), (0, 0))).astype(BF16)
    pmeta = proj_small[4][:N_META]

    grp = nb // (n_prompt_tok // MIXER_TILE)

    def cols(a):
        return a.T.reshape(RET_QK, nb // grp, grp).transpose(1, 0, 2)

    (h1_p, xn2_p, idx_p, gate_p, rank_p, cnt_p, s_fin, p_fin, st_s, o_s) = _prompt_layer(
        x2d, nmix, w_in_bf, cos_p, sin_p,
        (sdec, cols(proj_s[0]), cols(proj_s[1]), proj_s[0], proj_s[1], proj_s[2], state_ret[0]),
        kmeta, vmeta, pmeta, dec, wts, jnp.zeros((N_EXPERTS, LANES), F32), batch, seq)

    pool2d = state_pool[0].reshape(nb, POOL_BUF * POOL_WIDTH)
    (pool_s, h1_s, xn2_s, idx_s, gate_s, rank_s, cnt) = _sample_mixer(o_s, proj_s, xs2d, pool2d, wts, cnt_p)

    assert n_prompt_tok % ROW_TILE == 0 and nb <= ROW_TILE
    n_phantom = ROW_TILE - nb
    n_tiles = n_prompt_tok // ROW_TILE + 1
    i32 = jnp.int32
    counts = cnt[:, 0].astype(i32)
    padded = ((counts + MOE_BLOCK - 1) // MOE_BLOCK) * MOE_BLOCK
    pad_end = jnp.cumsum(padded)
    pad_start = pad_end - padded
    n_blocks = (n_tok * TOP_K) // MOE_BLOCK + N_EXPERTS
    block_row = jnp.arange(n_blocks, dtype=i32) * MOE_BLOCK
    block_e = jnp.minimum(jnp.sum((pad_end[None, :] <= block_row[:, None]).astype(i32), axis=1), N_EXPERTS - 1)
    nblk = (pad_end[-1:] // MOE_BLOCK).astype(i32)
    e_row = jnp.arange(N_EXPERTS, dtype=i32)
    used = padded > 0
    e_ord = jnp.cumsum(used.astype(i32)) - 1
    later_used = jnp.logical_and(e_row[None, :] > e_row[:, None], used[None, :])
    e_next = jnp.min(jnp.where(later_used, e_row[None, :], N_EXPERTS), axis=1)
    e_next = jnp.where(e_next == N_EXPERTS, -1, e_next)
    of_block = block_e[:, None] == e_row[None, :]
    block_eord = jnp.sum(jnp.where(of_block, e_ord[None, :], 0), axis=1)
    block_enext = jnp.sum(jnp.where(of_block, e_next[None, :], 0), axis=1)
    block_left = jnp.sum(jnp.where(of_block, (pad_start + counts)[None, :], 0), axis=1) - block_row
    block_valid = jnp.clip(block_left, 0, MOE_BLOCK)
    phantom = lambda fill, dt: jnp.full((TOP_K, n_phantom), fill, dt)
    idx = jnp.concatenate([idx_p, idx_s, phantom(-1, i32)], axis=1)
    rank = jnp.concatenate([rank_p, rank_s, phantom(0, i32)], axis=1)
    gates = jnp.concatenate([gate_p, gate_s, phantom(0.0, F32)], axis=1)
    onehot = idx[None] == jnp.arange(N_EXPERTS, dtype=i32)[:, None, None]
    tile_cnt = jnp.sum(onehot.reshape(N_EXPERTS, TOP_K, n_tiles, ROW_TILE).astype(i32), axis=(1, 3)).T
    run_before = jnp.cumsum(tile_cnt, axis=0) - tile_cnt
    tile_off = jnp.cumsum(tile_cnt, axis=1) - tile_cnt
    tile_dst = pad_start[None, :] + run_before
    delta = jnp.repeat((tile_off - run_before).T, ROW_TILE, axis=1)
    pos = rank + jnp.sum(jnp.where(onehot, delta[:, None, :], 0), axis=0)
    pos = jnp.where(idx >= 0, pos, -1)
    tcnt, tdst, ttot = tile_cnt.reshape(-1), tile_dst.reshape(-1), jnp.sum(tile_cnt, axis=1)

    tail = ((0, n_phantom), (0, 0))
    x_sorted = _dispatch(tcnt, tdst, ttot, pad_start + counts, padded - counts, nblk, pos, xn2_p,
                         jnp.pad(xn2_s, tail), n_blocks * MOE_BLOCK)
    y_sorted = _experts(block_e, nblk, block_eord, block_enext, block_valid, x_sorted, wgu, bgu, wd, bd)
    y_p, y_s = _combine(tcnt, tdst, ttot, pos, gates, y_sorted, h1_p, jnp.pad(h1_s, tail), norm_final[None, :])

    y_prompt = y_p.reshape(batch, seq, D_MODEL)
    y_sample = y_s[:nb].reshape(nb, 1, D_MODEL)
    ret_state_prompt = s_fin[None]
    pool_state_prompt = p_fin[:, 1:, :][None]
    ret_state_sample = st_s[None]
    pool_state_sample = pool_s.reshape(nb, POOL_BUF, POOL_WIDTH)[None]
    return (y_prompt, y_sample, ret_state_prompt, pool_state_prompt, ret_state_sample, pool_state_sample)
```

```python
import jax
import jax.numpy as jnp
import numpy as np
from jax import lax
from jax.experimental import pallas as pl
from jax.experimental.pallas import tpu as pltpu

F32 = jnp.float32
BF16 = jnp.bfloat16

D_MODEL = 1024
N_META = 16
RET_HEADS = 4
RET_DK = 128
RET_DV = 256
RET_QK = RET_HEADS * RET_DK
RET_V = RET_HEADS * RET_DV
CHUNK = 128
ROPE_BASE = 10000.0
POOL_WINDOWS = (2, 4, 8, 16)
POOL_GROUPS = 4
POOL_GROUP_DIM = 128
POOL_WIDTH = POOL_GROUPS * POOL_GROUP_DIM
POOL_BUF = max(POOL_WINDOWS) - 1
N_EXPERTS = 32
TOP_K = 4
D_FF = D_MODEL
SWIGLU_LIMIT = 7.0
SWIGLU_ALPHA = 1.702
EPS = 1e-6
IN_WIDTHS = (RET_QK, RET_QK, RET_V, RET_V, POOL_WIDTH, D_MODEL, D_MODEL)
IN_TOTAL = sum(IN_WIDTHS)
IN_OFFS = tuple(int(s) for s in np.cumsum((0,) + IN_WIDTHS))

LANES = 128
ROW_CHUNKS = D_MODEL // (2 * LANES)
MIXER_TILE = 512
MOE_BLOCK = 512
ROW_TILE = 256
SUBLANES = 8
WEIGHT_CAST_ROWS = 128
VMEM_LIMIT = 56 * 1024 * 1024
PROMPT_VMEM_LIMIT = 60 * 1024 * 1024

assert N_META + 1 >= max(POOL_WINDOWS)
assert POOL_WINDOWS == (2, 4, 8, 16)


def _dot(a, b):
    return jnp.dot(a, b, preferred_element_type=F32)


def _rmsnorm(x, w):
    return x * lax.rsqrt(jnp.mean(x * x, axis=-1, keepdims=True) + EPS) * w


def _sigmoid(x):
    return 0.5 * jnp.tanh(0.5 * x) + 0.5


def _store_rows_as_tiles(ref, x):
    half = D_MODEL // 2
    hi = pltpu.bitcast(x[:, :half], jnp.uint32) & jnp.uint32(0xFFFF0000)
    lo = pltpu.bitcast(x[:, half:], jnp.uint32) >> 16
    ref[...] = (hi | lo).reshape(x.shape[0], ROW_CHUNKS, LANES)


def _load_rows_from_tiles(ref, rows):
    w = ref[...].reshape(rows, D_MODEL // 2)
    hi = pltpu.bitcast(w & jnp.uint32(0xFFFF0000), F32)
    lo = pltpu.bitcast(w << 16, F32)
    return jnp.concatenate([hi, lo], axis=1)


def _inproj_kernel(x_ref, nw_ref, w_ref, cos_ref, sin_ref,
                   q_ref, k_ref, v_ref, g_ref, p_ref, ga_ref, gb_ref):
    xn = _rmsnorm(x_ref[...], nw_ref[...]).astype(BF16)
    cos = cos_ref[...]
    sin = sin_ref[...]

    def seg(i):
        return _dot(xn, w_ref[:, IN_OFFS[i]:IN_OFFS[i + 1]])

    def rot(a):
        return a * cos + pltpu.roll(a, RET_DK // 2, 1) * sin

    q = seg(0)
    k = seg(1)
    for h in range(RET_HEADS):
        sl = slice(h * RET_DK, (h + 1) * RET_DK)
        q_ref[:, sl] = rot(q[:, sl]).astype(q_ref.dtype)
        k_ref[:, sl] = (rot(k[:, sl]) * (RET_DK ** -0.5)).astype(k_ref.dtype)
    v_ref[...] = seg(2).astype(v_ref.dtype)
    g = seg(3)
    g_ref[...] = (g * _sigmoid(g)).astype(g_ref.dtype)
    p_ref[...] = seg(4)
    ga_ref[...] = _sigmoid(seg(5)).astype(ga_ref.dtype)
    gb_ref[...] = _sigmoid(seg(6)).astype(gb_ref.dtype)


def _sample_state_step(step, sdec_ref, qt_ref, kt_ref, q_ref, k_ref, v_ref, st_ref, stout_ref, o_ref):
    grp = st_ref.shape[0]
    row0 = step * grp
    if grp % SUBLANES == 0:
        rows = pl.ds(pl.multiple_of(row0, SUBLANES), grp)
        q8, k8, v8 = q_ref[rows, :], k_ref[rows, :], v_ref[rows, :]
    else:
        assert 2 * grp == SUBLANES
        rows = pl.ds(pl.multiple_of((step // 2) * SUBLANES, SUBLANES), SUBLANES)
        first = step % 2 == 0
        q8, k8, v8 = (jnp.where(first, a[:grp], a[grp:]) for a in (q_ref[rows, :], k_ref[rows, :], v_ref[rows, :]))
    for h in range(RET_HEADS):
        ksl = slice(h * RET_DK, (h + 1) * RET_DK)
        vsl = slice(h * RET_DV, (h + 1) * RET_DV)
        score = jnp.sum(q8[:, ksl] * k8[:, ksl], axis=1, keepdims=True) * sdec_ref[h, 0]
        intra = score * v8[:, vsl]
        for bb in range(grp):
            s_old = st_ref[bb, h]
            qcol = qt_ref[0, ksl, bb:bb + 1] * sdec_ref[h, 1]
            kcol = kt_ref[0, ksl, bb:bb + 1] * sdec_ref[h, 2]
            cross = jnp.sum(s_old * qcol, axis=0, keepdims=True)
            o_ref[pl.ds(row0 + bb, 1), vsl] = intra[bb:bb + 1, :] + cross
            stout_ref[bb, h] = s_old * sdec_ref[h, 3] + kcol * v8[bb:bb + 1, vsl]


def _inproj_small(x2d, nw, w_in_bf, cosf, sinf):
    rows = x2d.shape[0]
    const = lambda i: (0, 0)
    return pl.pallas_call(
        _inproj_kernel,
        grid=(1,),
        in_specs=[
            pl.BlockSpec((rows, D_MODEL), const),
            pl.BlockSpec((1, D_MODEL), const),
            pl.BlockSpec((D_MODEL, IN_TOTAL), const, pipeline_mode=pl.Buffered(1)),
            pl.BlockSpec((rows, RET_DK), const),
            pl.BlockSpec((rows, RET_DK), const),
        ],
        out_specs=[pl.BlockSpec((rows, w), const) for w in IN_WIDTHS],
        out_shape=[jax.ShapeDtypeStruct((rows, w), F32) for w in IN_WIDTHS],
        compiler_params=pltpu.CompilerParams(dimension_semantics=("arbitrary",), vmem_limit_bytes=VMEM_LIMIT),
        name="inproj",
    )(x2d, nw, w_in_bf, cosf, sinf)


def _group_norm(o, gn_row):
    mu = jnp.mean(o, axis=-1, keepdims=True)
    var = jnp.mean(jnp.square(o - mu), axis=-1, keepdims=True)
    return (o - mu) * lax.rsqrt(var + EPS) * gn_row


def _pool_branch(groups, poolw_ref, pscale_ref, wpool_ref):
    pm = [_dot(g.astype(BF16), poolw_ref[i]) for i, g in enumerate(groups)]
    pm = jnp.concatenate(pm, axis=1) * pscale_ref[...]
    return _dot(pm.astype(BF16), wpool_ref[...])


def _merge_tail(o_norm, silu_g, sig_a, sig_b, yb, x, wret_ref, wout_ref):
    ya = _dot((silu_g.astype(F32) * o_norm).astype(BF16), wret_ref[...])
    merged = sig_a.astype(F32) * ya + sig_b.astype(F32) * yb
    return x + _dot(merged.astype(BF16), wout_ref[...])


def _route(h1, nffn_ref, wrt_ref, br_ref, run_scr,
           xn2_ref, idx_ref, gate_ref, rank_ref):
    tm = h1.shape[0]
    xn2 = _rmsnorm(h1, nffn_ref[...]).astype(BF16)
    xn2_ref[...] = xn2
    logits = lax.dot_general(wrt_ref[...], xn2, (((1,), (1,)), ((), ())),
                             preferred_element_type=F32) + br_ref[...]
    e_iota = lax.broadcasted_iota(jnp.int32, (N_EXPERTS, tm), 0)
    work = logits
    vals, sels = [], []
    chosen = jnp.zeros((N_EXPERTS, tm), F32)
    for _ in range(TOP_K):
        m = jnp.max(work, axis=0, keepdims=True)
        sel = jnp.min(jnp.where(work == m, e_iota, N_EXPERTS), axis=0, keepdims=True)
        hit = e_iota == sel
        vals.append(m)
        sels.append(sel)
        chosen = jnp.where(hit, 1.0, chosen)
        work = jnp.where(hit, -jnp.inf, work)
    exps = [jnp.exp(v - vals[0]) for v in vals]
    denom = exps[0] + exps[1] + exps[2] + exps[3]
    gates = [e / denom for e in exps]
    r_i = lax.broadcasted_iota(jnp.int32, (tm, tm), 0)
    c_i = lax.broadcasted_iota(jnp.int32, (tm, tm), 1)
    before = jnp.where(r_i < c_i, 1.0, 0.0).astype(BF16)
    base = run_scr[...] + _dot(chosen.astype(BF16), before)
    for kk in range(TOP_K):
        rk = jnp.sum(jnp.where(e_iota == sels[kk], base, 0.0), axis=0, keepdims=True)
        rank_ref[kk:kk + 1, :] = rk.astype(jnp.int32)
        idx_ref[kk:kk + 1, :] = sels[kk]
        gate_ref[kk:kk + 1, :] = gates[kk]
    run_scr[...] = run_scr[...] + jnp.sum(chosen, axis=1, keepdims=True)


def _mixer_kernel(q_ref, k_ref, v_ref, g_ref, p_ref, ga_ref, gb_ref, x_ref,
                  kmeta_ref, vmeta_ref, pmeta_ref, mask_ref, qdec_ref, kdec_ref, cdec_ref, gn_ref,
                  poolw_ref, pscale_ref, wret_ref, wpool_ref, wout_ref, nffn_ref, wrt_ref, br_ref,
                  cnt0_ref,
                  h1_ref, xn2_ref, idx_ref, gate_ref, rank_ref, cnt_ref, sfin_ref, pfin_ref,
                  s_scr, ext_scr, o_scr, run_scr):
    b = pl.program_id(0)
    j = pl.program_id(1)
    nj = pl.num_programs(1)
    tm = q_ref.shape[0]

    def state_update(s_old, kc, vc, h):
        kd = (kc.astype(F32) * kdec_ref[h]).astype(BF16)
        upd = lax.dot_general(kd, vc, (((0,), (0,)), ((), ())), preferred_element_type=F32)
        return s_old * cdec_ref[h] + upd

    @pl.when(jnp.logical_and(b == 0, j == 0))
    def _():
        run_scr[...] = cnt0_ref[:, 0:1]

    @pl.when(j == 0)
    def _():
        for h in range(RET_HEADS):
            kc = kmeta_ref[:, h * RET_DK:(h + 1) * RET_DK]
            vc = vmeta_ref[:, h * RET_DV:(h + 1) * RET_DV]
            s_scr[h] = state_update(jnp.zeros((RET_DK, RET_DV), F32), kc, vc, h)
        ext_scr[0:N_META, :] = pmeta_ref[...]

    n_chunks = tm // CHUNK
    units = [(c, h) for c in range(n_chunks) for h in range(RET_HEADS)]

    def operands(c, h):
        rows = slice(c * CHUNK, (c + 1) * CHUNK)
        return (q_ref[rows, h * RET_DK:(h + 1) * RET_DK], k_ref[rows, h * RET_DK:(h + 1) * RET_DK],
                v_ref[rows, h * RET_DV:(h + 1) * RET_DV])

    lhs, upd = {}, {}
    for c, h in units:
        qc, kc, vc = operands(c, h)
        scores = lax.dot_general(qc, kc, (((1,), (1,)), ((), ())), preferred_element_type=F32) * mask_ref[h]
        qd = (qc.astype(F32) * qdec_ref[h]).astype(BF16)
        lhs[c, h] = jnp.concatenate([scores.astype(BF16), qd], axis=1)
        kd = (kc.astype(F32) * kdec_ref[h]).astype(BF16)
        upd[c, h] = lax.dot_general(kd, vc, (((0,), (0,)), ((), ())), preferred_element_type=F32)
    state = {}
    for h in range(RET_HEADS):
        s = s_scr[h]
        for c in range(n_chunks):
            state[c, h] = s
            s = s * cdec_ref[h] + upd[c, h]
        s_scr[h] = s
    for c, h in units:
        rhs = jnp.concatenate([operands(c, h)[2], state[c, h].astype(BF16)], axis=0)
        o = _dot(lhs[c, h], rhs)
        o_scr[c * CHUNK:(c + 1) * CHUNK, h * RET_DV:(h + 1) * RET_DV] = _group_norm(
            o, gn_ref[:, h * RET_DV:(h + 1) * RET_DV])

    p = p_ref[...]
    ext_scr[N_META:N_META + tm, :] = p
    a = ext_scr[...]
    g1 = POOL_GROUP_DIM
    s2 = a + pltpu.roll(a, 1, 0)
    s4 = s2[:, g1:] + pltpu.roll(s2[:, g1:], 2, 0)
    s8 = s4[:, g1:] + pltpu.roll(s4[:, g1:], 4, 0)
    s16 = s8[:, g1:] + pltpu.roll(s8[:, g1:], 8, 0)
    sums = (s2[N_META:, :g1], s4[N_META:, :g1], s8[N_META:, :g1], s16[N_META:, :])
    groups = [sums[i] * (1.0 / POOL_WINDOWS[i]) - p[:, i * g1:(i + 1) * g1] for i in range(POOL_GROUPS)]
    ext_scr[0:N_META, :] = ext_scr[tm:tm + N_META, :]

    yb = _pool_branch(groups, poolw_ref, pscale_ref, wpool_ref)
    h1 = _merge_tail(o_scr[...], g_ref[...], ga_ref[...], gb_ref[...], yb, x_ref[...], wret_ref, wout_ref)
    h1_ref[...] = h1
    _route(h1, nffn_ref, wrt_ref, br_ref, run_scr, xn2_ref, idx_ref, gate_ref, rank_ref)
    cnt_ref[...] = jnp.broadcast_to(run_scr[...], cnt_ref.shape)

    @pl.when(j == nj - 1)
    def _():
        for h in range(RET_HEADS):
            sfin_ref[0, h] = s_scr[h]
        pfin_ref[0] = ext_scr[0:N_META, :]


N_SAMPLE_IN = 7
N_MIXER_IN = 17
N_MIXER_OUT = 8
N_PROJ = len(IN_WIDTHS)


def _prompt_kernel(*refs):
    it = iter(refs)
    take = lambda n: [next(it) for _ in range(n)]
    x_ref, nw_ref, w_ref, cos_ref, sin_ref = take(5)
    sample_in = take(N_SAMPLE_IN)
    mixer_in = take(N_MIXER_IN)
    mixer_out = take(N_MIXER_OUT)
    stout_ref, os_ref = take(2)
    proj_scr = take(N_PROJ)
    mixer_scr = take(4)
    _inproj_kernel(x_ref, nw_ref, w_ref, cos_ref, sin_ref, *proj_scr)
    step = pl.program_id(0) * pl.num_programs(1) + pl.program_id(1)
    _sample_state_step(step, *sample_in, stout_ref, os_ref)
    _mixer_kernel(*proj_scr, x_ref, *mixer_in, *mixer_out, *mixer_scr)


def _prompt_layer(x2d, nw, w_in_bf, cosf, sinf, sample, kmeta, vmeta, pmeta, dec, wts, cnt0, batch, seq):
    tm = MIXER_TILE
    nj = seq // tm
    rows = batch * seq
    row_map = lambda b, j: (b * nj + j, 0)
    lane_map = lambda b, j: (0, b * nj + j)
    c2 = lambda b, j: (0, 0)
    c3 = lambda b, j: (0, 0, 0)

    def whole(a):
        mode = dict(pipeline_mode=pl.Buffered(1)) if a.size * a.dtype.itemsize >= 512 * 1024 else {}
        return pl.BlockSpec(a.shape, c2 if a.ndim == 2 else c3, **mode)

    sdec, qt, kt, qs, ks, vs, state = sample
    nb = state.shape[0]
    grp = nb // (batch * nj)
    assert grp * batch * nj == nb and qt.shape == (nb // grp, RET_QK, grp)
    step_map = lambda b, j: (b * nj + j, 0, 0)
    st_spec = pl.BlockSpec((grp, RET_HEADS, RET_DK, RET_DV), lambda b, j: (b * nj + j, 0, 0, 0))
    mask, qdec, kdec, cdec = dec
    mixer_in = [kmeta, vmeta, pmeta, mask, qdec, kdec, cdec, *wts, cnt0]
    assert len(mixer_in) == N_MIXER_IN
    in_arrays = [x2d, nw, w_in_bf, cosf, sinf, sdec, qt, kt, qs, ks, vs, state] + mixer_in
    in_specs = [
        pl.BlockSpec((tm, D_MODEL), row_map),
        pl.BlockSpec((1, D_MODEL), c2),
        pl.BlockSpec((D_MODEL, IN_TOTAL), c2, pipeline_mode=pl.Buffered(1)),
        pl.BlockSpec((tm, RET_DK), lambda b, j: (j, 0)),
        pl.BlockSpec((tm, RET_DK), lambda b, j: (j, 0)),
        pl.BlockSpec(memory_space=pltpu.SMEM),
        pl.BlockSpec((1, RET_QK, grp), step_map), pl.BlockSpec((1, RET_QK, grp), step_map),
        pl.BlockSpec(qs.shape, c2), pl.BlockSpec(ks.shape, c2), pl.BlockSpec(vs.shape, c2),
        st_spec,
    ] + [whole(a) for a in mixer_in]
    out_shape = [
        jax.ShapeDtypeStruct((rows, D_MODEL), F32),
        jax.ShapeDtypeStruct((rows, D_MODEL), BF16),
        jax.ShapeDtypeStruct((TOP_K, rows), jnp.int32),
        jax.ShapeDtypeStruct((TOP_K, rows), F32),
        jax.ShapeDtypeStruct((TOP_K, rows), jnp.int32),
        jax.ShapeDtypeStruct((N_EXPERTS, LANES), F32),
        jax.ShapeDtypeStruct((batch, RET_HEADS, RET_DK, RET_DV), F32),
        jax.ShapeDtypeStruct((batch, N_META, POOL_WIDTH), F32),
        jax.ShapeDtypeStruct(state.shape, F32),
        jax.ShapeDtypeStruct((nb, RET_V), F32),
    ]
    out_specs = [
        pl.BlockSpec((tm, D_MODEL), row_map),
        pl.BlockSpec((tm, D_MODEL), row_map),
        pl.BlockSpec((TOP_K, tm), lane_map),
        pl.BlockSpec((TOP_K, tm), lane_map),
        pl.BlockSpec((TOP_K, tm), lane_map),
        pl.BlockSpec((N_EXPERTS, LANES), c2),
        pl.BlockSpec((1, RET_HEADS, RET_DK, RET_DV), lambda b, j: (b, 0, 0, 0)),
        pl.BlockSpec((1, N_META, POOL_WIDTH), lambda b, j: (b, 0, 0)),
        st_spec,
        pl.BlockSpec((nb, RET_V), c2),
    ]
    proj_dts = (BF16, BF16, BF16, BF16, F32, BF16, BF16)
    return pl.pallas_call(
        _prompt_kernel,
        grid=(batch, nj),
        in_specs=in_specs,
        out_specs=out_specs,
        out_shape=out_shape,
        scratch_shapes=[pltpu.VMEM((tm, w), dt) for w, dt in zip(IN_WIDTHS, proj_dts)] + [
            pltpu.VMEM((RET_HEADS, RET_DK, RET_DV), F32),
            pltpu.VMEM((N_META + tm, POOL_WIDTH), F32),
            pltpu.VMEM((tm, RET_V), F32),
            pltpu.VMEM((N_EXPERTS, 1), F32),
        ],
        compiler_params=pltpu.CompilerParams(
            dimension_semantics=("arbitrary", "arbitrary"), vmem_limit_bytes=PROMPT_VMEM_LIMIT),
        name="prompt_layer",
    )(*in_arrays)


def _sample_kernel(o_ref, g_ref, p_ref, ga_ref, gb_ref, x_ref, pool_ref, gn_ref,
                   poolw_ref, pscale_ref, wret_ref, wpool_ref, wout_ref, nffn_ref, wrt_ref, br_ref, cnt0_ref,
                   poolout_ref, h1_ref, xn2_ref, idx_ref, gate_ref, rank_ref, cnt_ref,
                   run_scr):
    run_scr[...] = cnt0_ref[:, 0:1]
    o = o_ref[...]
    o_norm = jnp.concatenate(
        [_group_norm(o[:, h * RET_DV:(h + 1) * RET_DV], gn_ref[:, h * RET_DV:(h + 1) * RET_DV])
         for h in range(RET_HEADS)], axis=1)
    p = p_ref[...]
    w = POOL_WIDTH
    g1 = POOL_GROUP_DIM

    def prev(r, lo):
        return pool_ref[:, r * w + lo:(r + 1) * w]

    s2 = p + prev(14, 0)
    s4 = s2[:, g1:] + prev(13, g1) + prev(12, g1)
    s8 = s4[:, g1:] + prev(11, 2 * g1) + prev(10, 2 * g1) + prev(9, 2 * g1) + prev(8, 2 * g1)
    s16 = s8[:, g1:]
    for r in range(7, -1, -1):
        s16 = s16 + prev(r, 3 * g1)
    sums = (s2[:, :g1], s4[:, :g1], s8[:, :g1], s16)
    groups = [sums[t] * (1.0 / POOL_WINDOWS[t]) - p[:, t * g1:(t + 1) * g1] for t in range(POOL_GROUPS)]
    poolout_ref[:, 0:(POOL_BUF - 1) * w] = pool_ref[:, w:POOL_BUF * w]
    poolout_ref[:, (POOL_BUF - 1) * w:] = p
    yb = _pool_branch(groups, poolw_ref, pscale_ref, wpool_ref)
    h1 = _merge_tail(o_norm, g_ref[...], ga_ref[...], gb_ref[...], yb, x_ref[...], wret_ref, wout_ref)
    h1_ref[...] = h1
    _route(h1, nffn_ref, wrt_ref, br_ref, run_scr, xn2_ref, idx_ref, gate_ref, rank_ref)
    cnt_ref[...] = jnp.broadcast_to(run_scr[...], cnt_ref.shape)


def _sample_mixer(o_s, proj, x2d, pool2d, wts, cnt0):
    _, _, _, g, p, ga, gb = proj
    nb = x2d.shape[0]
    c2 = lambda i: (0, 0)

    def whole(a):
        return pl.BlockSpec(a.shape, c2)

    gn, poolw, pscale, wret, wpool, wout, nffn, wrt, br = wts
    in_arrays = [o_s, g, p, ga, gb, x2d, pool2d, gn, poolw, pscale, wret, wpool, wout, nffn, wrt, br, cnt0]
    in_specs = [whole(a) for a in in_arrays[:8]]
    in_specs += [pl.BlockSpec(poolw.shape, lambda i: (0, 0, 0))]
    in_specs += [whole(a) for a in in_arrays[9:]]
    out_shape = [
        jax.ShapeDtypeStruct(pool2d.shape, F32),
        jax.ShapeDtypeStruct((nb, D_MODEL), F32),
        jax.ShapeDtypeStruct((nb, D_MODEL), BF16),
        jax.ShapeDtypeStruct((TOP_K, nb), jnp.int32),
        jax.ShapeDtypeStruct((TOP_K, nb), F32),
        jax.ShapeDtypeStruct((TOP_K, nb), jnp.int32),
        jax.ShapeDtypeStruct((N_EXPERTS, LANES), F32),
    ]
    return pl.pallas_call(
        _sample_kernel,
        grid=(1,),
        in_specs=in_specs,
        out_specs=[pl.BlockSpec(s.shape, c2) for s in out_shape],
        out_shape=out_shape,
        scratch_shapes=[pltpu.VMEM((N_EXPERTS, 1), F32)],
        compiler_params=pltpu.CompilerParams(
            dimension_semantics=("arbitrary",), vmem_limit_bytes=VMEM_LIMIT),
        name="sample_mixer",
    )(*in_arrays)


TILE_ROWS = ROW_TILE * TOP_K


def _run_copy(src_ref, src_row, dst_ref, dst_row, n_rows, sem):
    return pltpu.make_async_copy(src_ref.at[pl.ds(src_row, n_rows)], dst_ref.at[pl.ds(dst_row, n_rows)], sem)


def _for_each_run(tcnt_ref, tile, fn):
    def body(e, off):
        n = tcnt_ref[tile * N_EXPERTS + e]

        @pl.when(n > 0)
        def _():
            fn(e, off, n)

        return off + n

    lax.fori_loop(0, N_EXPERTS, body, 0)


def _dispatch_kernel(tcnt_ref, tdst_ref, ttot_ref, zrow_ref, zcnt_ref, nblk_ref, pos_ref, xp_ref, xs_ref, out_ref,
                     sorted_scr, zero_scr, sem, zsem):
    i = pl.program_id(0)
    n_prompt = pl.num_programs(0) - 1
    n_blocks = out_ref.shape[0] // MOE_BLOCK
    slot = i % 2

    def wait_tile(tile, s):
        _run_copy(out_ref, 0, sorted_scr.at[s], 0, ttot_ref[tile], sem.at[s]).wait()

    def for_each_pad(fn):
        def body(e, carry):
            n = zcnt_ref[e]

            @pl.when(n > 0)
            def _():
                fn(_run_copy(zero_scr, 0, out_ref, zrow_ref[e], n, zsem))

            dead = nblk_ref[0] + e

            @pl.when(dead < n_blocks)
            def _():
                fn(_run_copy(zero_scr, 0, out_ref, dead * MOE_BLOCK, MOE_BLOCK, zsem))

            return carry

        lax.fori_loop(0, N_EXPERTS, body, 0)

    @pl.when(i == 0)
    def _():
        zero_scr[...] = jnp.zeros_like(zero_scr)
        for_each_pad(lambda cp: cp.start())

    def sort_tile(x_ref):
        r_iota = lax.broadcasted_iota(jnp.int32, (TILE_ROWS, ROW_TILE), 0)
        hit = r_iota == pos_ref[0:1, :]
        for kk in range(1, TOP_K):
            hit = jnp.logical_or(hit, r_iota == pos_ref[kk:kk + 1, :])
        perm = jnp.where(hit, 1.0, 0.0).astype(BF16)
        xs = _dot(perm, x_ref[...])

        @pl.when(i >= 2)
        def _():
            wait_tile(i - 2, slot)

        _store_rows_as_tiles(sorted_scr.at[slot], xs)

    @pl.when(i < n_prompt)
    def _():
        sort_tile(xp_ref)

    @pl.when(i == n_prompt)
    def _():
        sort_tile(xs_ref)

    _for_each_run(tcnt_ref, i, lambda e, off, n: _run_copy(
        sorted_scr.at[slot], off, out_ref, tdst_ref[i * N_EXPERTS + e], n, sem.at[slot]).start())

    @pl.when(i == n_prompt)
    def _():
        wait_tile(i - 1, 1 - slot)
        wait_tile(i, slot)
        for_each_pad(lambda cp: cp.wait())


def _dispatch(tcnt, tdst, ttot, zrow, zcnt, nblk, pos, xn2_p, xn2_s, n_sorted):
    n_prompt = xn2_p.shape[0] // ROW_TILE
    grid_spec = pltpu.PrefetchScalarGridSpec(
        num_scalar_prefetch=6,
        grid=(n_prompt + 1,),
        in_specs=[
            pl.BlockSpec((TOP_K, ROW_TILE), lambda i, *_: (0, i)),
            pl.BlockSpec((ROW_TILE, D_MODEL), lambda i, *_: (jnp.minimum(i, n_prompt - 1), 0)),
            pl.BlockSpec((ROW_TILE, D_MODEL), lambda i, *_: (0, 0)),
        ],
        out_specs=pl.BlockSpec(memory_space=pl.ANY),
        scratch_shapes=[
            pltpu.VMEM((2, TILE_ROWS, ROW_CHUNKS, LANES), jnp.uint32),
            pltpu.VMEM((MOE_BLOCK, ROW_CHUNKS, LANES), jnp.uint32),
            pltpu.SemaphoreType.DMA((2,)), pltpu.SemaphoreType.DMA(()),
        ],
    )
    return pl.pallas_call(
        _dispatch_kernel,
        grid_spec=grid_spec,
        out_shape=jax.ShapeDtypeStruct((n_sorted, ROW_CHUNKS, LANES), jnp.uint32),
        compiler_params=pltpu.CompilerParams(dimension_semantics=("arbitrary",)),
        name="moe_dispatch",
    )(tcnt, tdst, ttot, zrow, zcnt, nblk, pos, xn2_p, xn2_s)


def _expert_kernel(be_ref, nblk_ref, eord_ref, enext_ref, bvalid_ref, x_ref, wgu_ref, bgu_ref, wd_ref, bd_ref, y_ref,
                   wgu_f32, wd_f32, wgu_bf, wd_bf, wsem):
    i = pl.program_id(0)
    live = i < nblk_ref[0]
    new_expert = jnp.logical_or(i == 0, be_ref[i] != be_ref[jnp.maximum(i - 1, 0)])

    def weight_copies(e, s):
        return (pltpu.make_async_copy(wgu_ref.at[e], wgu_f32.at[s], wsem.at[0, s]),
                pltpu.make_async_copy(wd_ref.at[e], wd_f32.at[s], wsem.at[1, s]))

    @pl.when(jnp.logical_and(live, new_expert))
    def _():
        slot = eord_ref[i] % 2

        @pl.when(i == 0)
        def _():
            for cp in weight_copies(be_ref[0], 0):
                cp.start(priority=1)

        for cp in weight_copies(be_ref[i], slot):
            cp.wait()

        @pl.when(enext_ref[i] >= 0)
        def _():
            for cp in weight_copies(enext_ref[i], 1 - slot):
                cp.start(priority=1)

        def cast(c, carry):
            rows = pl.ds(pl.multiple_of(c * WEIGHT_CAST_ROWS, WEIGHT_CAST_ROWS), WEIGHT_CAST_ROWS)
            wgu_bf[rows, :] = wgu_f32[slot, rows, :].astype(BF16)
            wd_bf[rows, :] = wd_f32[slot, rows, :].astype(BF16)
            return carry

        lax.fori_loop(0, D_MODEL // WEIGHT_CAST_ROWS, cast, 0)

    @pl.when(jnp.logical_not(live))
    def _():
        y_ref[...] = jnp.zeros_like(y_ref)

    def ffn(rows):
        x = _load_rows_from_tiles(x_ref.at[pl.ds(0, rows)], rows)
        h = _dot(x.astype(BF16), wgu_bf[...]) + bgu_ref[0]
        gate = jnp.minimum(h[:, :D_FF], SWIGLU_LIMIT)
        up = jnp.clip(h[:, D_FF:], -SWIGLU_LIMIT, SWIGLU_LIMIT)
        glu = gate * _sigmoid(gate * SWIGLU_ALPHA)
        y = _dot(((up + 1.0) * glu).astype(BF16), wd_bf[...]) + bd_ref[0]
        _store_rows_as_tiles(y_ref.at[pl.ds(0, rows)], y.astype(BF16).astype(F32))

    half = MOE_BLOCK // 2
    short = bvalid_ref[i] <= half

    @pl.when(jnp.logical_and(live, jnp.logical_not(short)))
    def _():
        ffn(MOE_BLOCK)

    @pl.when(jnp.logical_and(live, short))
    def _():
        ffn(half)
        y_ref[pl.ds(half, half)] = jnp.zeros((half,) + y_ref.shape[1:], y_ref.dtype)


def _experts(block_e, nblk, block_eord, block_enext, block_valid, x_sorted, wgu, bgu, wd, bd):
    n_blocks = x_sorted.shape[0] // MOE_BLOCK
    wmap = lambda i, be, *_: (be[i], 0, 0)
    rmap = lambda i, *_: (i, 0, 0)
    hbm = pl.BlockSpec(memory_space=pl.ANY)
    grid_spec = pltpu.PrefetchScalarGridSpec(
        num_scalar_prefetch=5,
        grid=(n_blocks,),
        in_specs=[
            pl.BlockSpec((MOE_BLOCK, ROW_CHUNKS, LANES), rmap),
            hbm,
            pl.BlockSpec((1, 1, 2 * D_FF), wmap),
            hbm,
            pl.BlockSpec((1, 1, D_MODEL), wmap),
        ],
        out_specs=pl.BlockSpec((MOE_BLOCK, ROW_CHUNKS, LANES), rmap),
        scratch_shapes=[
            pltpu.VMEM((2, D_MODEL, 2 * D_FF), F32), pltpu.VMEM((2, D_FF, D_MODEL), F32),
            pltpu.VMEM((D_MODEL, 2 * D_FF), BF16), pltpu.VMEM((D_FF, D_MODEL), BF16),
            pltpu.SemaphoreType.DMA((2, 2)),
        ],
    )
    return pl.pallas_call(
        _expert_kernel,
        grid_spec=grid_spec,
        out_shape=jax.ShapeDtypeStruct(x_sorted.shape, jnp.uint32),
        compiler_params=pltpu.CompilerParams(
            dimension_semantics=("arbitrary",), vmem_limit_bytes=VMEM_LIMIT),
        name="moe_experts",
    )(block_e, nblk, block_eord, block_enext, block_valid, x_sorted, wgu, bgu, wd, bd)


def _combine_kernel(tcnt_ref, tdst_ref, ttot_ref, pos_ref, gate_ref, ys_ref, hp_ref, hs_ref, nf_ref, yp_ref, ysmp_ref,
                    runs_scr, sem):
    i = pl.program_id(0)
    n_tiles = pl.num_programs(0)
    n_prompt = n_tiles - 1
    slot = i % 2

    def start_runs(tile, s):
        _for_each_run(tcnt_ref, tile, lambda e, off, n: _run_copy(
            ys_ref, tdst_ref[tile * N_EXPERTS + e], runs_scr.at[s], off, n, sem.at[s]).start())

    @pl.when(i == 0)
    def _():
        start_runs(0, 0)

    @pl.when(i + 1 < n_tiles)
    def _():
        start_runs(i + 1, 1 - slot)

    _run_copy(ys_ref, 0, runs_scr.at[slot], 0, ttot_ref[i], sem.at[slot]).wait()
    ys = _load_rows_from_tiles(runs_scr.at[slot], TILE_ROWS).astype(BF16)
    r_iota = lax.broadcasted_iota(jnp.int32, (TILE_ROWS, ROW_TILE), 0)
    gmat = jnp.zeros((TILE_ROWS, ROW_TILE), F32)
    for kk in range(TOP_K):
        gmat = jnp.where(r_iota == pos_ref[kk:kk + 1, :], gate_ref[kk:kk + 1, :], gmat)
    moe = lax.dot_general(gmat.astype(BF16), ys, (((0,), (0,)), ((), ())), preferred_element_type=F32)

    @pl.when(i < n_prompt)
    def _():
        yp_ref[...] = _rmsnorm(hp_ref[...] + moe, nf_ref[...])

    @pl.when(i == n_prompt)
    def _():
        ysmp_ref[...] = _rmsnorm(hs_ref[...] + moe, nf_ref[...])


def _combine(tcnt, tdst, ttot, pos, gates, y_sorted, h1_p, h1_s, nf):
    n_prompt = h1_p.shape[0] // ROW_TILE
    pmap = lambda i, *_: (jnp.minimum(i, n_prompt - 1), 0)
    smap = lambda i, *_: (0, 0)
    lmap = lambda i, *_: (0, i)
    grid_spec = pltpu.PrefetchScalarGridSpec(
        num_scalar_prefetch=3,
        grid=(n_prompt + 1,),
        in_specs=[
            pl.BlockSpec((TOP_K, ROW_TILE), lmap),
            pl.BlockSpec((TOP_K, ROW_TILE), lmap),
            pl.BlockSpec(memory_space=pl.ANY),
            pl.BlockSpec((ROW_TILE, D_MODEL), pmap),
            pl.BlockSpec((ROW_TILE, D_MODEL), smap),
            pl.BlockSpec((1, D_MODEL), smap),
        ],
        out_specs=[pl.BlockSpec((ROW_TILE, D_MODEL), pmap), pl.BlockSpec((ROW_TILE, D_MODEL), smap)],
        scratch_shapes=[pltpu.VMEM((2, TILE_ROWS, ROW_CHUNKS, LANES), jnp.uint32), pltpu.SemaphoreType.DMA((2,))],
    )
    return pl.pallas_call(
        _combine_kernel,
        grid_spec=grid_spec,
        out_shape=[jax.ShapeDtypeStruct(h1_p.shape, F32), jax.ShapeDtypeStruct(h1_s.shape, F32)],
        compiler_params=pltpu.CompilerParams(dimension_semantics=("arbitrary",)),
        name="moe_combine",
    )(tcnt, tdst, ttot, pos, gates, y_sorted, h1_p, h1_s, nf)


def _rotary_tables(pos):
    f = np.float32
    inv = np.power(f(ROPE_BASE), -np.arange(0, RET_DK, 2, dtype=f) / f(RET_DK)).astype(f)
    ang = (np.asarray(pos, f)[:, None] * inv[None, :]).astype(f)
    cos, sin = np.cos(ang).astype(f), np.sin(ang).astype(f)
    return np.concatenate([cos, cos], axis=1), np.concatenate([-sin, sin], axis=1)


def _decay_tables(chunk):
    f = np.float32
    log_g = np.log1p(-np.exp2(f(-5.0) - np.arange(RET_HEADS, dtype=f))).astype(f)
    i = np.arange(chunk, dtype=f)
    diff = i[:, None] - i[None, :]
    mask = np.where(diff[None] >= 0, np.exp(np.maximum(diff, f(0.0))[None] * log_g[:, None, None]), f(0.0)).astype(f)
    q_dec = np.exp((i + f(1.0))[None, :] * log_g[:, None]).astype(f)
    k_dec = np.exp((f(chunk) - f(1.0) - i)[None, :] * log_g[:, None]).astype(f)
    c_dec = np.exp(f(chunk) * log_g).astype(f)
    return mask, q_dec, k_dec, c_dec


def kernel(x_prompt, x_sample, state_ret, state_pool, meta_tokens, norm_mix, w_in, ret_gn, pool_w, pool_scale,
           w_ret_branch, w_pool_branch, w_out, norm_ffn, w_router, b_router, w_gate_up, b_gate_up, w_down, b_down,
           norm_final):
    batch, seq, _ = x_prompt.shape
    nb = x_sample.shape[0]
    past_len = 16384
    n_prompt_tok = batch * seq
    n_tok = n_prompt_tok + nb

    w_in_bf = w_in[0].astype(BF16)
    wts = (ret_gn[0][None, :], pool_w[0].astype(BF16), pool_scale[0][None, :],
           w_ret_branch[0].astype(BF16), w_pool_branch[0].astype(BF16), w_out[0].astype(BF16),
           norm_ffn[0][None, :], w_router[0].T.astype(BF16), b_router[0][:, None])
    wgu = w_gate_up[0]
    wd = w_down[0]
    bgu = b_gate_up[0][:, None, :]
    bd = b_down[0][:, None, :]
    nmix = norm_mix[0][None, :]

    cos_p, sin_p = _rotary_tables(N_META + np.arange(seq))
    cos_s, sin_s = _rotary_tables(np.concatenate([np.arange(N_META), np.full((nb,), past_len)]))
    mask, q_dec, k_dec, c_dec = _decay_tables(CHUNK)
    dec = (mask,
           np.ascontiguousarray(np.broadcast_to(q_dec[:, :, None], (RET_HEADS, CHUNK, RET_DK))),
           np.ascontiguousarray(np.broadcast_to(k_dec[:, :, None], (RET_HEADS, CHUNK, RET_DK))),
           np.ascontiguousarray(np.broadcast_to(c_dec[:, None, None], (RET_HEADS, 1, RET_DV))))
    m1, q1, k1, c1 = _decay_tables(1)
    sdec = np.stack([m1[:, 0, 0], q1[:, 0], k1[:, 0], c1], axis=1)

    x2d = x_prompt.reshape(n_prompt_tok, D_MODEL)
    xs2d = x_sample.reshape(nb, D_MODEL)
    x_small = jnp.concatenate([meta_tokens, xs2d], axis=0)
    proj_small = _inproj_small(x_small, nmix, w_in_bf, cos_s, sin_s)
    proj_s = tuple(a[N_META:] for a in proj_small)

    lead = CHUNK - N_META
    kmeta = jnp.pad(proj_small[1][:N_META], ((lead, 0), (0, 0))).astype(BF16)
    vmeta = jnp.pad(proj_small[2][:N_META], ((lead, 0), (0, 0))).astype(BF16)
    pmeta = proj_small[4][:N_META]

    grp = nb // (n_prompt_tok // MIXER_TILE)

    def cols(a):
        return a.T.reshape(RET_QK, nb // grp, grp).transpose(1, 0, 2)

    (h1_p, xn2_p, idx_p, gate_p, rank_p, cnt_p, s_fin, p_fin, st_s, o_s) = _prompt_layer(
        x2d, nmix, w_in_bf, cos_p, sin_p,
        (sdec, cols(proj_s[0]), cols(proj_s[1]), proj_s[0], proj_s[1], proj_s[2], state_ret[0]),
        kmeta, vmeta, pmeta, dec, wts, jnp.zeros((N_EXPERTS, LANES), F32), batch, seq)

    pool2d = state_pool[0].reshape(nb, POOL_BUF * POOL_WIDTH)
    (pool_s, h1_s, xn2_s, idx_s, gate_s, rank_s, cnt) = _sample_mixer(o_s, proj_s, xs2d, pool2d, wts, cnt_p)

    assert n_prompt_tok % ROW_TILE == 0 and nb <= ROW_TILE
    n_phantom = ROW_TILE - nb
    n_tiles = n_prompt_tok // ROW_TILE + 1
    i32 = jnp.int32
    counts = cnt[:, 0].astype(i32)
    padded = ((counts + MOE_BLOCK - 1) // MOE_BLOCK) * MOE_BLOCK
    pad_end = jnp.cumsum(padded)
    pad_start = pad_end - padded
    n_blocks = (n_tok * TOP_K) // MOE_BLOCK + N_EXPERTS
    block_row = jnp.arange(n_blocks, dtype=i32) * MOE_BLOCK
    block_e = jnp.minimum(jnp.sum((pad_end[None, :] <= block_row[:, None]).astype(i32), axis=1), N_EXPERTS - 1)
    nblk = (pad_end[-1:] // MOE_BLOCK).astype(i32)
    e_row = jnp.arange(N_EXPERTS, dtype=i32)
    used = padded > 0
    e_ord = jnp.cumsum(used.astype(i32)) - 1
    later_used = jnp.logical_and(e_row[None, :] > e_row[:, None], used[None, :])
    e_next = jnp.min(jnp.where(later_used, e_row[None, :], N_EXPERTS), axis=1)
    e_next = jnp.where(e_next == N_EXPERTS, -1, e_next)
    of_block = block_e[:, None] == e_row[None, :]
    block_eord = jnp.sum(jnp.where(of_block, e_ord[None, :], 0), axis=1)
    block_enext = jnp.sum(jnp.where(of_block, e_next[None, :], 0), axis=1)
    block_left = jnp.sum(jnp.where(of_block, (pad_start + counts)[None, :], 0), axis=1) - block_row
    block_valid = jnp.clip(block_left, 0, MOE_BLOCK)
    phantom = lambda fill, dt: jnp.full((TOP_K, n_phantom), fill, dt)
    idx = jnp.concatenate([idx_p, idx_s, phantom(-1, i32)], axis=1)
    rank = jnp.concatenate([rank_p, rank_s, phantom(0, i32)], axis=1)
    gates = jnp.concatenate([gate_p, gate_s, phantom(0.0, F32)], axis=1)
    onehot = idx[None] == jnp.arange(N_EXPERTS, dtype=i32)[:, None, None]
    tile_cnt = jnp.sum(onehot.reshape(N_EXPERTS, TOP_K, n_tiles, ROW_TILE).astype(i32), axis=(1, 3)).T
    run_before = jnp.cumsum(tile_cnt, axis=0) - tile_cnt
    tile_off = jnp.cumsum(tile_cnt, axis=1) - tile_cnt
    tile_dst = pad_start[None, :] + run_before
    delta = jnp.repeat((tile_off - run_before).T, ROW_TILE, axis=1)
    pos = rank + jnp.sum(jnp.where(onehot, delta[:, None, :], 0), axis=0)
    pos = jnp.where(idx >= 0, pos, -1)
    tcnt, tdst, ttot = tile_cnt.reshape(-1), tile_dst.reshape(-1), jnp.sum(tile_cnt, axis=1)

    tail = ((0, n_phantom), (0, 0))
    x_sorted = _dispatch(tcnt, tdst, ttot, pad_start + counts, padded - counts, nblk, pos, xn2_p,
                         jnp.pad(xn2_s, tail), n_blocks * MOE_BLOCK)
    y_sorted = _experts(block_e, nblk, block_eord, block_enext, block_valid, x_sorted, wgu, bgu, wd, bd)
    y_p, y_s = _combine(tcnt, tdst, ttot, pos, gates, y_sorted, h1_p, jnp.pad(h1_s, tail), norm_final[None, :])

    y_prompt = y_p.reshape(batch, seq, D_MODEL)
    y_sample = y_s[:nb].reshape(nb, 1, D_MODEL)
    ret_state_prompt = s_fin[None]
    pool_state_prompt = p_fin[:, 1:, :][None]
    ret_state_sample = st_s[None]
    pool_state_sample = pool_s.reshape(nb, POOL_BUF, POOL_WIDTH)[None]
    return (y_prompt, y_sample, ret_state_prompt, pool_state_prompt, ret_state_sample, pool_state_sample)
```

```python
import jax
import jax.numpy as jnp
import numpy as np
from jax import lax
from jax.experimental import pallas as pl
from jax.experimental.pallas import tpu as pltpu

F32 = jnp.float32
BF16 = jnp.bfloat16

D_MODEL = 1024
N_META = 16
RET_HEADS = 4
RET_DK = 128
RET_DV = 256
RET_QK = RET_HEADS * RET_DK
RET_V = RET_HEADS * RET_DV
CHUNK = 128
ROPE_BASE = 10000.0
POOL_WINDOWS = (2, 4, 8, 16)
POOL_GROUPS = 4
POOL_GROUP_DIM = 128
POOL_WIDTH = POOL_GROUPS * POOL_GROUP_DIM
POOL_BUF = max(POOL_WINDOWS) - 1
N_EXPERTS = 32
TOP_K = 4
D_FF = D_MODEL
SWIGLU_LIMIT = 7.0
SWIGLU_ALPHA = 1.702
EPS = 1e-6
IN_WIDTHS = (RET_QK, RET_QK, RET_V, RET_V, POOL_WIDTH, D_MODEL, D_MODEL)
IN_TOTAL = sum(IN_WIDTHS)
IN_OFFS = tuple(int(s) for s in np.cumsum((0,) + IN_WIDTHS))

LANES = 128
ROW_CHUNKS = D_MODEL // (2 * LANES)
MIXER_TILE = 512
MOE_BLOCK = 512
ROW_TILE = 256
SUBLANES = 8
WEIGHT_CAST_ROWS = 128
VMEM_LIMIT = 56 * 1024 * 1024
PROMPT_VMEM_LIMIT = 60 * 1024 * 1024

assert N_META + 1 >= max(POOL_WINDOWS)
assert POOL_WINDOWS == (2, 4, 8, 16)


def _dot(a, b):
    return jnp.dot(a, b, preferred_element_type=F32)


def _rmsnorm(x, w):
    return x * lax.rsqrt(jnp.mean(x * x, axis=-1, keepdims=True) + EPS) * w


def _sigmoid(x):
    return 0.5 * jnp.tanh(0.5 * x) + 0.5


def _store_rows_as_tiles(ref, x):
    half = D_MODEL // 2
    hi = pltpu.bitcast(x[:, :half], jnp.uint32) & jnp.uint32(0xFFFF0000)
    lo = pltpu.bitcast(x[:, half:], jnp.uint32) >> 16
    ref[...] = (hi | lo).reshape(x.shape[0], ROW_CHUNKS, LANES)


def _load_rows_from_tiles(ref, rows):
    w = ref[...].reshape(rows, D_MODEL // 2)
    hi = pltpu.bitcast(w & jnp.uint32(0xFFFF0000), F32)
    lo = pltpu.bitcast(w << 16, F32)
    return jnp.concatenate([hi, lo], axis=1)


def _inproj_kernel(x_ref, nw_ref, w_ref, cos_ref, sin_ref,
                   q_ref, k_ref, v_ref, g_ref, p_ref, ga_ref, gb_ref, *, defer_gates=False):
    xn = _rmsnorm(x_ref[...], nw_ref[...]).astype(BF16)
    cos = cos_ref[...]
    sin = sin_ref[...]

    def seg(i):
        return _dot(xn, w_ref[:, IN_OFFS[i]:IN_OFFS[i + 1]])

    def rot(a):
        return a * cos + pltpu.roll(a, RET_DK // 2, 1) * sin

    q = seg(0)
    k = seg(1)
    for h in range(RET_HEADS):
        sl = slice(h * RET_DK, (h + 1) * RET_DK)
        q_ref[:, sl] = rot(q[:, sl]).astype(q_ref.dtype)
        k_ref[:, sl] = (rot(k[:, sl]) * (RET_DK ** -0.5)).astype(k_ref.dtype)
    v_ref[...] = seg(2).astype(v_ref.dtype)

    def gates():
        g = seg(3)
        g_ref[...] = (g * _sigmoid(g)).astype(g_ref.dtype)
        p_ref[...] = seg(4)
        ga_ref[...] = _sigmoid(seg(5)).astype(ga_ref.dtype)
        gb_ref[...] = _sigmoid(seg(6)).astype(gb_ref.dtype)

    if defer_gates:
        return gates
    gates()


def _sample_state_step(step, sdec_ref, qt_ref, kt_ref, q_ref, k_ref, v_ref, st_ref, stout_ref, o_ref):
    grp = st_ref.shape[0]
    row0 = step * grp
    if grp % SUBLANES == 0:
        rows = pl.ds(pl.multiple_of(row0, SUBLANES), grp)
        q8, k8, v8 = q_ref[rows, :], k_ref[rows, :], v_ref[rows, :]
    else:
        assert 2 * grp == SUBLANES
        rows = pl.ds(pl.multiple_of((step // 2) * SUBLANES, SUBLANES), SUBLANES)
        first = step % 2 == 0
        q8, k8, v8 = (jnp.where(first, a[:grp], a[grp:]) for a in (q_ref[rows, :], k_ref[rows, :], v_ref[rows, :]))
    for h in range(RET_HEADS):
        ksl = slice(h * RET_DK, (h + 1) * RET_DK)
        vsl = slice(h * RET_DV, (h + 1) * RET_DV)
        score = jnp.sum(q8[:, ksl] * k8[:, ksl], axis=1, keepdims=True) * sdec_ref[h, 0]
        intra = score * v8[:, vsl]
        for bb in range(grp):
            s_old = st_ref[bb, h]
            qcol = qt_ref[0, ksl, bb:bb + 1] * sdec_ref[h, 1]
            kcol = kt_ref[0, ksl, bb:bb + 1] * sdec_ref[h, 2]
            cross = jnp.sum(s_old * qcol, axis=0, keepdims=True)
            o_ref[pl.ds(row0 + bb, 1), vsl] = intra[bb:bb + 1, :] + cross
            stout_ref[bb, h] = s_old * sdec_ref[h, 3] + kcol * v8[bb:bb + 1, vsl]


def _inproj_small(x2d, nw, w_in_bf, cosf, sinf):
    rows = x2d.shape[0]
    const = lambda i: (0, 0)
    return pl.pallas_call(
        _inproj_kernel,
        grid=(1,),
        in_specs=[
            pl.BlockSpec((rows, D_MODEL), const),
            pl.BlockSpec((1, D_MODEL), const),
            pl.BlockSpec((D_MODEL, IN_TOTAL), const, pipeline_mode=pl.Buffered(1)),
            pl.BlockSpec((rows, RET_DK), const),
            pl.BlockSpec((rows, RET_DK), const),
        ],
        out_specs=[pl.BlockSpec((rows, w), const) for w in IN_WIDTHS],
        out_shape=[jax.ShapeDtypeStruct((rows, w), F32) for w in IN_WIDTHS],
        compiler_params=pltpu.CompilerParams(dimension_semantics=("arbitrary",), vmem_limit_bytes=VMEM_LIMIT),
        name="inproj",
    )(x2d, nw, w_in_bf, cosf, sinf)


def _group_norm(o, gn_row):
    mu = jnp.mean(o, axis=-1, keepdims=True)
    var = jnp.mean(jnp.square(o - mu), axis=-1, keepdims=True)
    return (o - mu) * lax.rsqrt(var + EPS) * gn_row


def _pool_branch(groups, poolw_ref, pscale_ref, wpool_ref):
    pm = [_dot(g.astype(BF16), poolw_ref[i]) for i, g in enumerate(groups)]
    pm = jnp.concatenate(pm, axis=1) * pscale_ref[...]
    return _dot(pm.astype(BF16), wpool_ref[...])


def _merge_tail(o_norm, silu_g, sig_a, sig_b, yb, x, wret_ref, wout_ref):
    ya = _dot((silu_g.astype(F32) * o_norm).astype(BF16), wret_ref[...])
    merged = sig_a.astype(F32) * ya + sig_b.astype(F32) * yb
    return x + _dot(merged.astype(BF16), wout_ref[...])


def _route(h1, nffn_ref, wrt_ref, br_ref, run_scr,
           xn2_ref, idx_ref, gate_ref, rank_ref):
    tm = h1.shape[0]
    xn2 = _rmsnorm(h1, nffn_ref[...]).astype(BF16)
    xn2_ref[...] = xn2
    logits = lax.dot_general(wrt_ref[...], xn2, (((1,), (1,)), ((), ())),
                             preferred_element_type=F32) + br_ref[...]
    e_iota = lax.broadcasted_iota(jnp.int32, (N_EXPERTS, tm), 0)
    work = logits
    vals, sels = [], []
    chosen = jnp.zeros((N_EXPERTS, tm), F32)
    for _ in range(TOP_K):
        m = jnp.max(work, axis=0, keepdims=True)
        sel = jnp.min(jnp.where(work == m, e_iota, N_EXPERTS), axis=0, keepdims=True)
        hit = e_iota == sel
        vals.append(m)
        sels.append(sel)
        chosen = jnp.where(hit, 1.0, chosen)
        work = jnp.where(hit, -jnp.inf, work)
    exps = [jnp.exp(v - vals[0]) for v in vals]
    denom = exps[0] + exps[1] + exps[2] + exps[3]
    gates = [e / denom for e in exps]
    r_i = lax.broadcasted_iota(jnp.int32, (tm, tm), 0)
    c_i = lax.broadcasted_iota(jnp.int32, (tm, tm), 1)
    before = jnp.where(r_i < c_i, 1.0, 0.0).astype(BF16)
    base = run_scr[...] + _dot(chosen.astype(BF16), before)
    for kk in range(TOP_K):
        rk = jnp.sum(jnp.where(e_iota == sels[kk], base, 0.0), axis=0, keepdims=True)
        rank_ref[kk:kk + 1, :] = rk.astype(jnp.int32)
        idx_ref[kk:kk + 1, :] = sels[kk]
        gate_ref[kk:kk + 1, :] = gates[kk]
    run_scr[...] = run_scr[...] + jnp.sum(chosen, axis=1, keepdims=True)


def _mixer_kernel(q_ref, k_ref, v_ref, g_ref, p_ref, ga_ref, gb_ref, x_ref,
                  kmeta_ref, vmeta_ref, pmeta_ref, mask_ref, qdec_ref, kdec_ref, cdec_ref, gn_ref,
                  poolw_ref, pscale_ref, wret_ref, wpool_ref, wout_ref, nffn_ref, wrt_ref, br_ref,
                  cnt0_ref,
                  h1_ref, xn2_ref, idx_ref, gate_ref, rank_ref, cnt_ref, sfin_ref, pfin_ref,
                  s_scr, ext_scr, o_scr, run_scr, *, after_first_pass=None):
    b = pl.program_id(0)
    j = pl.program_id(1)
    nj = pl.num_programs(1)
    tm = q_ref.shape[0]

    def state_update(s_old, kc, vc, h):
        kd = (kc.astype(F32) * kdec_ref[h]).astype(BF16)
        upd = lax.dot_general(kd, vc, (((0,), (0,)), ((), ())), preferred_element_type=F32)
        return s_old * cdec_ref[h] + upd

    @pl.when(jnp.logical_and(b == 0, j == 0))
    def _():
        run_scr[...] = cnt0_ref[:, 0:1]

    @pl.when(j == 0)
    def _():
        for h in range(RET_HEADS):
            kc = kmeta_ref[:, h * RET_DK:(h + 1) * RET_DK]
            vc = vmeta_ref[:, h * RET_DV:(h + 1) * RET_DV]
            s_scr[h] = state_update(jnp.zeros((RET_DK, RET_DV), F32), kc, vc, h)
        ext_scr[0:N_META, :] = pmeta_ref[...]

    n_chunks = tm // CHUNK
    units = [(c, h) for c in range(n_chunks) for h in range(RET_HEADS)]

    def operands(c, h):
        rows = slice(c * CHUNK, (c + 1) * CHUNK)
        return (q_ref[rows, h * RET_DK:(h + 1) * RET_DK], k_ref[rows, h * RET_DK:(h + 1) * RET_DK],
                v_ref[rows, h * RET_DV:(h + 1) * RET_DV])

    lhs, upd = {}, {}
    for c, h in units:
        qc, kc, vc = operands(c, h)
        scores = lax.dot_general(qc, kc, (((1,), (1,)), ((), ())), preferred_element_type=F32) * mask_ref[h]
        qd = (qc.astype(F32) * qdec_ref[h]).astype(BF16)
        lhs[c, h] = jnp.concatenate([scores.astype(BF16), qd], axis=1)
        kd = (kc.astype(F32) * kdec_ref[h]).astype(BF16)
        upd[c, h] = lax.dot_general(kd, vc, (((0,), (0,)), ((), ())), preferred_element_type=F32)
    if after_first_pass is not None:
        after_first_pass()
    state = {}
    for h in range(RET_HEADS):
        s = s_scr[h]
        for c in range(n_chunks):
            state[c, h] = s
            s = s * cdec_ref[h] + upd[c, h]
        s_scr[h] = s
    for c, h in units:
        rhs = jnp.concatenate([operands(c, h)[2], state[c, h].astype(BF16)], axis=0)
        o = _dot(lhs[c, h], rhs)
        o_scr[c * CHUNK:(c + 1) * CHUNK, h * RET_DV:(h + 1) * RET_DV] = _group_norm(
            o, gn_ref[:, h * RET_DV:(h + 1) * RET_DV])

    p = p_ref[...]
    ext_scr[N_META:N_META + tm, :] = p
    a = ext_scr[...]
    g1 = POOL_GROUP_DIM
    s2 = a + pltpu.roll(a, 1, 0)
    s4 = s2[:, g1:] + pltpu.roll(s2[:, g1:], 2, 0)
    s8 = s4[:, g1:] + pltpu.roll(s4[:, g1:], 4, 0)
    s16 = s8[:, g1:] + pltpu.roll(s8[:, g1:], 8, 0)
    sums = (s2[N_META:, :g1], s4[N_META:, :g1], s8[N_META:, :g1], s16[N_META:, :])
    groups = [sums[i] * (1.0 / POOL_WINDOWS[i]) - p[:, i * g1:(i + 1) * g1] for i in range(POOL_GROUPS)]
    ext_scr[0:N_META, :] = ext_scr[tm:tm + N_META, :]

    yb = _pool_branch(groups, poolw_ref, pscale_ref, wpool_ref)
    h1 = _merge_tail(o_scr[...], g_ref[...], ga_ref[...], gb_ref[...], yb, x_ref[...], wret_ref, wout_ref)
    h1_ref[...] = h1
    _route(h1, nffn_ref, wrt_ref, br_ref, run_scr, xn2_ref, idx_ref, gate_ref, rank_ref)
    cnt_ref[...] = jnp.broadcast_to(run_scr[...], cnt_ref.shape)

    @pl.when(j == nj - 1)
    def _():
        for h in range(RET_HEADS):
            sfin_ref[0, h] = s_scr[h]
        pfin_ref[0] = ext_scr[0:N_META, :]


N_SAMPLE_IN = 7
N_MIXER_IN = 17
N_MIXER_OUT = 8
N_PROJ = len(IN_WIDTHS)


def _prompt_kernel(*refs):
    it = iter(refs)
    take = lambda n: [next(it) for _ in range(n)]
    x_ref, nw_ref, w_ref, cos_ref, sin_ref = take(5)
    sample_in = take(N_SAMPLE_IN)
    mixer_in = take(N_MIXER_IN)
    mixer_out = take(N_MIXER_OUT)
    stout_ref, os_ref = take(2)
    proj_scr = take(N_PROJ)
    mixer_scr = take(4)
    gate_columns = _inproj_kernel(x_ref, nw_ref, w_ref, cos_ref, sin_ref, *proj_scr, defer_gates=True)
    step = pl.program_id(0) * pl.num_programs(1) + pl.program_id(1)
    _sample_state_step(step, *sample_in, stout_ref, os_ref)
    _mixer_kernel(*proj_scr, x_ref, *mixer_in, *mixer_out, *mixer_scr, after_first_pass=gate_columns)


def _prompt_layer(x2d, nw, w_in_bf, cosf, sinf, sample, kmeta, vmeta, pmeta, dec, wts, cnt0, batch, seq):
    tm = MIXER_TILE
    nj = seq // tm
    rows = batch * seq
    row_map = lambda b, j: (b * nj + j, 0)
    lane_map = lambda b, j: (0, b * nj + j)
    c2 = lambda b, j: (0, 0)
    c3 = lambda b, j: (0, 0, 0)

    def whole(a):
        mode = dict(pipeline_mode=pl.Buffered(1)) if a.size * a.dtype.itemsize >= 512 * 1024 else {}
        return pl.BlockSpec(a.shape, c2 if a.ndim == 2 else c3, **mode)

    sdec, qt, kt, qs, ks, vs, state = sample
    nb = state.shape[0]
    grp = nb // (batch * nj)
    assert grp * batch * nj == nb and qt.shape == (nb // grp, RET_QK, grp)
    step_map = lambda b, j: (b * nj + j, 0, 0)
    st_spec = pl.BlockSpec((grp, RET_HEADS, RET_DK, RET_DV), lambda b, j: (b * nj + j, 0, 0, 0))
    mask, qdec, kdec, cdec = dec
    mixer_in = [kmeta, vmeta, pmeta, mask, qdec, kdec, cdec, *wts, cnt0]
    assert len(mixer_in) == N_MIXER_IN
    in_arrays = [x2d, nw, w_in_bf, cosf, sinf, sdec, qt, kt, qs, ks, vs, state] + mixer_in
    in_specs = [
        pl.BlockSpec((tm, D_MODEL), row_map),
        pl.BlockSpec((1, D_MODEL), c2),
        pl.BlockSpec((D_MODEL, IN_TOTAL), c2, pipeline_mode=pl.Buffered(1)),
        pl.BlockSpec((tm, RET_DK), lambda b, j: (j, 0)),
        pl.BlockSpec((tm, RET_DK), lambda b, j: (j, 0)),
        pl.BlockSpec(memory_space=pltpu.SMEM),
        pl.BlockSpec((1, RET_QK, grp), step_map), pl.BlockSpec((1, RET_QK, grp), step_map),
        pl.BlockSpec(qs.shape, c2), pl.BlockSpec(ks.shape, c2), pl.BlockSpec(vs.shape, c2),
        st_spec,
    ] + [whole(a) for a in mixer_in]
    out_shape = [
        jax.ShapeDtypeStruct((rows, D_MODEL), F32),
        jax.ShapeDtypeStruct((rows, D_MODEL), BF16),
        jax.ShapeDtypeStruct((TOP_K, rows), jnp.int32),
        jax.ShapeDtypeStruct((TOP_K, rows), F32),
        jax.ShapeDtypeStruct((TOP_K, rows), jnp.int32),
        jax.ShapeDtypeStruct((N_EXPERTS, LANES), F32),
        jax.ShapeDtypeStruct((batch, RET_HEADS, RET_DK, RET_DV), F32),
        jax.ShapeDtypeStruct((batch, N_META, POOL_WIDTH), F32),
        jax.ShapeDtypeStruct(state.shape, F32),
        jax.ShapeDtypeStruct((nb, RET_V), F32),
    ]
    out_specs = [
        pl.BlockSpec((tm, D_MODEL), row_map),
        pl.BlockSpec((tm, D_MODEL), row_map),
        pl.BlockSpec((TOP_K, tm), lane_map),
        pl.BlockSpec((TOP_K, tm), lane_map),
        pl.BlockSpec((TOP_K, tm), lane_map),
        pl.BlockSpec((N_EXPERTS, LANES), c2),
        pl.BlockSpec((1, RET_HEADS, RET_DK, RET_DV), lambda b, j: (b, 0, 0, 0)),
        pl.BlockSpec((1, N_META, POOL_WIDTH), lambda b, j: (b, 0, 0)),
        st_spec,
        pl.BlockSpec((nb, RET_V), c2),
    ]
    proj_dts = (BF16, BF16, BF16, BF16, F32, BF16, BF16)
    return pl.pallas_call(
        _prompt_kernel,
        grid=(batch, nj),
        in_specs=in_specs,
        out_specs=out_specs,
        out_shape=out_shape,
        scratch_shapes=[pltpu.VMEM((tm, w), dt) for w, dt in zip(IN_WIDTHS, proj_dts)] + [
            pltpu.VMEM((RET_HEADS, RET_DK, RET_DV), F32),
            pltpu.VMEM((N_META + tm, POOL_WIDTH), F32),
            pltpu.VMEM((tm, RET_V), F32),
            pltpu.VMEM((N_EXPERTS, 1), F32),
        ],
        compiler_params=pltpu.CompilerParams(
            dimension_semantics=("arbitrary", "arbitrary"), vmem_limit_bytes=PROMPT_VMEM_LIMIT),
        name="prompt_layer",
    )(*in_arrays)


def _sample_kernel(o_ref, g_ref, p_ref, ga_ref, gb_ref, x_ref, pool_ref, gn_ref,
                   poolw_ref, pscale_ref, wret_ref, wpool_ref, wout_ref, nffn_ref, wrt_ref, br_ref, cnt0_ref,
                   poolout_ref, h1_ref, xn2_ref, idx_ref, gate_ref, rank_ref, cnt_ref,
                   run_scr):
    run_scr[...] = cnt0_ref[:, 0:1]
    o = o_ref[...]
    o_norm = jnp.concatenate(
        [_group_norm(o[:, h * RET_DV:(h + 1) * RET_DV], gn_ref[:, h * RET_DV:(h + 1) * RET_DV])
         for h in range(RET_HEADS)], axis=1)
    p = p_ref[...]
    w = POOL_WIDTH
    g1 = POOL_GROUP_DIM

    def prev(r, lo):
        return pool_ref[:, r * w + lo:(r + 1) * w]

    s2 = p + prev(14, 0)
    s4 = s2[:, g1:] + prev(13, g1) + prev(12, g1)
    s8 = s4[:, g1:] + prev(11, 2 * g1) + prev(10, 2 * g1) + prev(9, 2 * g1) + prev(8, 2 * g1)
    s16 = s8[:, g1:]
    for r in range(7, -1, -1):
        s16 = s16 + prev(r, 3 * g1)
    sums = (s2[:, :g1], s4[:, :g1], s8[:, :g1], s16)
    groups = [sums[t] * (1.0 / POOL_WINDOWS[t]) - p[:, t * g1:(t + 1) * g1] for t in range(POOL_GROUPS)]
    poolout_ref[:, 0:(POOL_BUF - 1) * w] = pool_ref[:, w:POOL_BUF * w]
    poolout_ref[:, (POOL_BUF - 1) * w:] = p
    yb = _pool_branch(groups, poolw_ref, pscale_ref, wpool_ref)
    h1 = _merge_tail(o_norm, g_ref[...], ga_ref[...], gb_ref[...], yb, x_ref[...], wret_ref, wout_ref)
    h1_ref[...] = h1
    _route(h1, nffn_ref, wrt_ref, br_ref, run_scr, xn2_ref, idx_ref, gate_ref, rank_ref)
    cnt_ref[...] = jnp.broadcast_to(run_scr[...], cnt_ref.shape)


def _sample_mixer(o_s, proj, x2d, pool2d, wts, cnt0):
    _, _, _, g, p, ga, gb = proj
    nb = x2d.shape[0]
    c2 = lambda i: (0, 0)

    def whole(a):
        return pl.BlockSpec(a.shape, c2)

    gn, poolw, pscale, wret, wpool, wout, nffn, wrt, br = wts
    in_arrays = [o_s, g, p, ga, gb, x2d, pool2d, gn, poolw, pscale, wret, wpool, wout, nffn, wrt, br, cnt0]
    in_specs = [whole(a) for a in in_arrays[:8]]
    in_specs += [pl.BlockSpec(poolw.shape, lambda i: (0, 0, 0))]
    in_specs += [whole(a) for a in in_arrays[9:]]
    out_shape = [
        jax.ShapeDtypeStruct(pool2d.shape, F32),
        jax.ShapeDtypeStruct((nb, D_MODEL), F32),
        jax.ShapeDtypeStruct((nb, D_MODEL), BF16),
        jax.ShapeDtypeStruct((TOP_K, nb), jnp.int32),
        jax.ShapeDtypeStruct((TOP_K, nb), F32),
        jax.ShapeDtypeStruct((TOP_K, nb), jnp.int32),
        jax.ShapeDtypeStruct((N_EXPERTS, LANES), F32),
    ]
    return pl.pallas_call(
        _sample_kernel,
        grid=(1,),
        in_specs=in_specs,
        out_specs=[pl.BlockSpec(s.shape, c2) for s in out_shape],
        out_shape=out_shape,
        scratch_shapes=[pltpu.VMEM((N_EXPERTS, 1), F32)],
        compiler_params=pltpu.CompilerParams(
            dimension_semantics=("arbitrary",), vmem_limit_bytes=VMEM_LIMIT),
        name="sample_mixer",
    )(*in_arrays)


TILE_ROWS = ROW_TILE * TOP_K


def _run_copy(src_ref, src_row, dst_ref, dst_row, n_rows, sem):
    return pltpu.make_async_copy(src_ref.at[pl.ds(src_row, n_rows)], dst_ref.at[pl.ds(dst_row, n_rows)], sem)


def _for_each_run(tcnt_ref, tile, fn):
    def body(e, off):
        n = tcnt_ref[tile * N_EXPERTS + e]

        @pl.when(n > 0)
        def _():
            fn(e, off, n)

        return off + n

    lax.fori_loop(0, N_EXPERTS, body, 0)


def _dispatch_kernel(tcnt_ref, tdst_ref, ttot_ref, zrow_ref, zcnt_ref, nblk_ref, pos_ref, xp_ref, xs_ref, out_ref,
                     sorted_scr, zero_scr, sem, zsem):
    i = pl.program_id(0)
    n_prompt = pl.num_programs(0) - 1
    n_blocks = out_ref.shape[0] // MOE_BLOCK
    slot = i % 2

    def wait_tile(tile, s):
        _run_copy(out_ref, 0, sorted_scr.at[s], 0, ttot_ref[tile], sem.at[s]).wait()

    def for_each_pad(fn):
        def body(e, carry):
            n = zcnt_ref[e]

            @pl.when(n > 0)
            def _():
                fn(_run_copy(zero_scr, 0, out_ref, zrow_ref[e], n, zsem))

            dead = nblk_ref[0] + e

            @pl.when(dead < n_blocks)
            def _():
                fn(_run_copy(zero_scr, 0, out_ref, dead * MOE_BLOCK, MOE_BLOCK, zsem))

            return carry

        lax.fori_loop(0, N_EXPERTS, body, 0)

    @pl.when(i == 0)
    def _():
        zero_scr[...] = jnp.zeros_like(zero_scr)
        for_each_pad(lambda cp: cp.start())

    def sort_tile(x_ref):
        r_iota = lax.broadcasted_iota(jnp.int32, (TILE_ROWS, ROW_TILE), 0)
        hit = r_iota == pos_ref[0:1, :]
        for kk in range(1, TOP_K):
            hit = jnp.logical_or(hit, r_iota == pos_ref[kk:kk + 1, :])
        perm = jnp.where(hit, 1.0, 0.0).astype(BF16)
        xs = _dot(perm, x_ref[...])

        @pl.when(i >= 2)
        def _():
            wait_tile(i - 2, slot)

        _store_rows_as_tiles(sorted_scr.at[slot], xs)

    @pl.when(i < n_prompt)
    def _():
        sort_tile(xp_ref)

    @pl.when(i == n_prompt)
    def _():
        sort_tile(xs_ref)

    _for_each_run(tcnt_ref, i, lambda e, off, n: _run_copy(
        sorted_scr.at[slot], off, out_ref, tdst_ref[i * N_EXPERTS + e], n, sem.at[slot]).start())

    @pl.when(i == n_prompt)
    def _():
        wait_tile(i - 1, 1 - slot)
        wait_tile(i, slot)
        for_each_pad(lambda cp: cp.wait())


def _dispatch(tcnt, tdst, ttot, zrow, zcnt, nblk, pos, xn2_p, xn2_s, n_sorted):
    n_prompt = xn2_p.shape[0] // ROW_TILE
    grid_spec = pltpu.PrefetchScalarGridSpec(
        num_scalar_prefetch=6,
        grid=(n_prompt + 1,),
        in_specs=[
            pl.BlockSpec((TOP_K, ROW_TILE), lambda i, *_: (0, i)),
            pl.BlockSpec((ROW_TILE, D_MODEL), lambda i, *_: (jnp.minimum(i, n_prompt - 1), 0)),
            pl.BlockSpec((ROW_TILE, D_MODEL), lambda i, *_: (0, 0)),
        ],
        out_specs=pl.BlockSpec(memory_space=pl.ANY),
        scratch_shapes=[
            pltpu.VMEM((2, TILE_ROWS, ROW_CHUNKS, LANES), jnp.uint32),
            pltpu.VMEM((MOE_BLOCK, ROW_CHUNKS, LANES), jnp.uint32),
            pltpu.SemaphoreType.DMA((2,)), pltpu.SemaphoreType.DMA(()),
        ],
    )
    return pl.pallas_call(
        _dispatch_kernel,
        grid_spec=grid_spec,
        out_shape=jax.ShapeDtypeStruct((n_sorted, ROW_CHUNKS, LANES), jnp.uint32),
        compiler_params=pltpu.CompilerParams(dimension_semantics=("arbitrary",)),
        name="moe_dispatch",
    )(tcnt, tdst, ttot, zrow, zcnt, nblk, pos, xn2_p, xn2_s)


def _expert_kernel(be_ref, nblk_ref, eord_ref, enext_ref, bvalid_ref, x_ref, wgu_ref, bgu_ref, wd_ref, bd_ref, y_ref,
                   wgu_f32, wd_f32, wgu_bf, wd_bf, wsem):
    i = pl.program_id(0)
    live = i < nblk_ref[0]
    new_expert = jnp.logical_or(i == 0, be_ref[i] != be_ref[jnp.maximum(i - 1, 0)])

    def weight_copies(e, s):
        return (pltpu.make_async_copy(wgu_ref.at[e], wgu_f32.at[s], wsem.at[0, s]),
                pltpu.make_async_copy(wd_ref.at[e], wd_f32.at[s], wsem.at[1, s]))

    @pl.when(jnp.logical_and(live, new_expert))
    def _():
        slot = eord_ref[i] % 2

        @pl.when(i == 0)
        def _():
            for cp in weight_copies(be_ref[0], 0):
                cp.start(priority=1)

        for cp in weight_copies(be_ref[i], slot):
            cp.wait()

        @pl.when(enext_ref[i] >= 0)
        def _():
            for cp in weight_copies(enext_ref[i], 1 - slot):
                cp.start(priority=1)

        def cast(c, carry):
            rows = pl.ds(pl.multiple_of(c * WEIGHT_CAST_ROWS, WEIGHT_CAST_ROWS), WEIGHT_CAST_ROWS)
            wgu_bf[rows, :] = wgu_f32[slot, rows, :].astype(BF16)
            wd_bf[rows, :] = wd_f32[slot, rows, :].astype(BF16)
            return carry

        lax.fori_loop(0, D_MODEL // WEIGHT_CAST_ROWS, cast, 0)

    @pl.when(jnp.logical_not(live))
    def _():
        y_ref[...] = jnp.zeros_like(y_ref)

    def ffn(rows):
        x = _load_rows_from_tiles(x_ref.at[pl.ds(0, rows)], rows)
        h = _dot(x.astype(BF16), wgu_bf[...]) + bgu_ref[0]
        gate = jnp.minimum(h[:, :D_FF], SWIGLU_LIMIT)
        up = jnp.clip(h[:, D_FF:], -SWIGLU_LIMIT, SWIGLU_LIMIT)
        glu = gate * _sigmoid(gate * SWIGLU_ALPHA)
        y = _dot(((up + 1.0) * glu).astype(BF16), wd_bf[...]) + bd_ref[0]
        _store_rows_as_tiles(y_ref.at[pl.ds(0, rows)], y.astype(BF16).astype(F32))

    half = MOE_BLOCK // 2
    short = bvalid_ref[i] <= half

    @pl.when(jnp.logical_and(live, jnp.logical_not(short)))
    def _():
        ffn(MOE_BLOCK)

    @pl.when(jnp.logical_and(live, short))
    def _():
        ffn(half)
        y_ref[pl.ds(half, half)] = jnp.zeros((half,) + y_ref.shape[1:], y_ref.dtype)


def _experts(block_e, nblk, block_eord, block_enext, block_valid, x_sorted, wgu, bgu, wd, bd):
    n_blocks = x_sorted.shape[0] // MOE_BLOCK
    wmap = lambda i, be, *_: (be[i], 0, 0)
    rmap = lambda i, *_: (i, 0, 0)
    hbm = pl.BlockSpec(memory_space=pl.ANY)
    grid_spec = pltpu.PrefetchScalarGridSpec(
        num_scalar_prefetch=5,
        grid=(n_blocks,),
        in_specs=[
            pl.BlockSpec((MOE_BLOCK, ROW_CHUNKS, LANES), rmap),
            hbm,
            pl.BlockSpec((1, 1, 2 * D_FF), wmap),
            hbm,
            pl.BlockSpec((1, 1, D_MODEL), wmap),
        ],
        out_specs=pl.BlockSpec((MOE_BLOCK, ROW_CHUNKS, LANES), rmap),
        scratch_shapes=[
            pltpu.VMEM((2, D_MODEL, 2 * D_FF), F32), pltpu.VMEM((2, D_FF, D_MODEL), F32),
            pltpu.VMEM((D_MODEL, 2 * D_FF), BF16), pltpu.VMEM((D_FF, D_MODEL), BF16),
            pltpu.SemaphoreType.DMA((2, 2)),
        ],
    )
    return pl.pallas_call(
        _expert_kernel,
        grid_spec=grid_spec,
        out_shape=jax.ShapeDtypeStruct(x_sorted.shape, jnp.uint32),
        compiler_params=pltpu.CompilerParams(
            dimension_semantics=("arbitrary",), vmem_limit_bytes=VMEM_LIMIT),
        name="moe_experts",
    )(block_e, nblk, block_eord, block_enext, block_valid, x_sorted, wgu, bgu, wd, bd)


def _combine_kernel(tcnt_ref, tdst_ref, ttot_ref, pos_ref, gate_ref, ys_ref, hp_ref, hs_ref, nf_ref, yp_ref, ysmp_ref,
                    runs_scr, sem):
    i = pl.program_id(0)
    n_tiles = pl.num_programs(0)
    n_prompt = n_tiles - 1
    slot = i % 2

    def start_runs(tile, s):
        _for_each_run(tcnt_ref, tile, lambda e, off, n: _run_copy(
            ys_ref, tdst_ref[tile * N_EXPERTS + e], runs_scr.at[s], off, n, sem.at[s]).start())

    @pl.when(i == 0)
    def _():
        start_runs(0, 0)

    @pl.when(i + 1 < n_tiles)
    def _():
        start_runs(i + 1, 1 - slot)

    _run_copy(ys_ref, 0, runs_scr.at[slot], 0, ttot_ref[i], sem.at[slot]).wait()
    ys = _load_rows_from_tiles(runs_scr.at[slot], TILE_ROWS).astype(BF16)
    r_iota = lax.broadcasted_iota(jnp.int32, (TILE_ROWS, ROW_TILE), 0)
    gmat = jnp.zeros((TILE_ROWS, ROW_TILE), F32)
    for kk in range(TOP_K):
        gmat = jnp.where(r_iota == pos_ref[kk:kk + 1, :], gate_ref[kk:kk + 1, :], gmat)
    moe = lax.dot_general(gmat.astype(BF16), ys, (((0,), (0,)), ((), ())), preferred_element_type=F32)

    @pl.when(i < n_prompt)
    def _():
        yp_ref[...] = _rmsnorm(hp_ref[...] + moe, nf_ref[...])

    @pl.when(i == n_prompt)
    def _():
        ysmp_ref[...] = _rmsnorm(hs_ref[...] + moe, nf_ref[...])


def _combine(tcnt, tdst, ttot, pos, gates, y_sorted, h1_p, h1_s, nf):
    n_prompt = h1_p.shape[0] // ROW_TILE
    pmap = lambda i, *_: (jnp.minimum(i, n_prompt - 1), 0)
    smap = lambda i, *_: (0, 0)
    lmap = lambda i, *_: (0, i)
    grid_spec = pltpu.PrefetchScalarGridSpec(
        num_scalar_prefetch=3,
        grid=(n_prompt + 1,),
        in_specs=[
            pl.BlockSpec((TOP_K, ROW_TILE), lmap),
            pl.BlockSpec((TOP_K, ROW_TILE), lmap),
            pl.BlockSpec(memory_space=pl.ANY),
            pl.BlockSpec((ROW_TILE, D_MODEL), pmap),
            pl.BlockSpec((ROW_TILE, D_MODEL), smap),
            pl.BlockSpec((1, D_MODEL), smap),
        ],
        out_specs=[pl.BlockSpec((ROW_TILE, D_MODEL), pmap), pl.BlockSpec((ROW_TILE, D_MODEL), smap)],
        scratch_shapes=[pltpu.VMEM((2, TILE_ROWS, ROW_CHUNKS, LANES), jnp.uint32), pltpu.SemaphoreType.DMA((2,))],
    )
    return pl.pallas_call(
        _combine_kernel,
        grid_spec=grid_spec,
        out_shape=[jax.ShapeDtypeStruct(h1_p.shape, F32), jax.ShapeDtypeStruct(h1_s.shape, F32)],
        compiler_params=pltpu.CompilerParams(dimension_semantics=("arbitrary",)),
        name="moe_combine",
    )(tcnt, tdst, ttot, pos, gates, y_sorted, h1_p, h1_s, nf)


def _rotary_tables(pos):
    f = np.float32
    inv = np.power(f(ROPE_BASE), -np.arange(0, RET_DK, 2, dtype=f) / f(RET_DK)).astype(f)
    ang = (np.asarray(pos, f)[:, None] * inv[None, :]).astype(f)
    cos, sin = np.cos(ang).astype(f), np.sin(ang).astype(f)
    return np.concatenate([cos, cos], axis=1), np.concatenate([-sin, sin], axis=1)


def _decay_tables(chunk):
    f = np.float32
    log_g = np.log1p(-np.exp2(f(-5.0) - np.arange(RET_HEADS, dtype=f))).astype(f)
    i = np.arange(chunk, dtype=f)
    diff = i[:, None] - i[None, :]
    mask = np.where(diff[None] >= 0, np.exp(np.maximum(diff, f(0.0))[None] * log_g[:, None, None]), f(0.0)).astype(f)
    q_dec = np.exp((i + f(1.0))[None, :] * log_g[:, None]).astype(f)
    k_dec = np.exp((f(chunk) - f(1.0) - i)[None, :] * log_g[:, None]).astype(f)
    c_dec = np.exp(f(chunk) * log_g).astype(f)
    return mask, q_dec, k_dec, c_dec


def kernel(x_prompt, x_sample, state_ret, state_pool, meta_tokens, norm_mix, w_in, ret_gn, pool_w, pool_scale,
           w_ret_branch, w_pool_branch, w_out, norm_ffn, w_router, b_router, w_gate_up, b_gate_up, w_down, b_down,
           norm_final):
    batch, seq, _ = x_prompt.shape
    nb = x_sample.shape[0]
    past_len = 16384
    n_prompt_tok = batch * seq
    n_tok = n_prompt_tok + nb

    w_in_bf = w_in[0].astype(BF16)
    wts = (ret_gn[0][None, :], pool_w[0].astype(BF16), pool_scale[0][None, :],
           w_ret_branch[0].astype(BF16), w_pool_branch[0].astype(BF16), w_out[0].astype(BF16),
           norm_ffn[0][None, :], w_router[0].T.astype(BF16), b_router[0][:, None])
    wgu = w_gate_up[0]
    wd = w_down[0]
    bgu = b_gate_up[0][:, None, :]
    bd = b_down[0][:, None, :]
    nmix = norm_mix[0][None, :]

    cos_p, sin_p = _rotary_tables(N_META + np.arange(seq))
    cos_s, sin_s = _rotary_tables(np.concatenate([np.arange(N_META), np.full((nb,), past_len)]))
    mask, q_dec, k_dec, c_dec = _decay_tables(CHUNK)
    dec = (mask,
           np.ascontiguousarray(np.broadcast_to(q_dec[:, :, None], (RET_HEADS, CHUNK, RET_DK))),
           np.ascontiguousarray(np.broadcast_to(k_dec[:, :, None], (RET_HEADS, CHUNK, RET_DK))),
           np.ascontiguousarray(np.broadcast_to(c_dec[:, None, None], (RET_HEADS, 1, RET_DV))))
    m1, q1, k1, c1 = _decay_tables(1)
    sdec = np.stack([m1[:, 0, 0], q1[:, 0], k1[:, 0], c1], axis=1)

    x2d = x_prompt.reshape(n_prompt_tok, D_MODEL)
    xs2d = x_sample.reshape(nb, D_MODEL)
    x_small = jnp.concatenate([meta_tokens, xs2d], axis=0)
    proj_small = _inproj_small(x_small, nmix, w_in_bf, cos_s, sin_s)
    proj_s = tuple(a[N_META:] for a in proj_small)

    lead = CHUNK - N_META
    kmeta = jnp.pad(proj_small[1][:N_META], ((lead, 0), (0, 0))).astype(BF16)
    vmeta = jnp.pad(proj_small[2][:N_META], ((lead, 0), (0, 0))).astype(BF16)
    pmeta = proj_small[4][:N_META]

    grp = nb // (n_prompt_tok // MIXER_TILE)

    def cols(a):
        return a.T.reshape(RET_QK, nb // grp, grp).transpose(1, 0, 2)

    (h1_p, xn2_p, idx_p, gate_p, rank_p, cnt_p, s_fin, p_fin, st_s, o_s) = _prompt_layer(
        x2d, nmix, w_in_bf, cos_p, sin_p,
        (sdec, cols(proj_s[0]), cols(proj_s[1]), proj_s[0], proj_s[1], proj_s[2], state_ret[0]),
        kmeta, vmeta, pmeta, dec, wts, jnp.zeros((N_EXPERTS, LANES), F32), batch, seq)

    pool2d = state_pool[0].reshape(nb, POOL_BUF * POOL_WIDTH)
    (pool_s, h1_s, xn2_s, idx_s, gate_s, rank_s, cnt) = _sample_mixer(o_s, proj_s, xs2d, pool2d, wts, cnt_p)

    assert n_prompt_tok % ROW_TILE == 0 and nb <= ROW_TILE
    n_phantom = ROW_TILE - nb
    n_tiles = n_prompt_tok // ROW_TILE + 1
    i32 = jnp.int32
    counts = cnt[:, 0].astype(i32)
    padded = ((counts + MOE_BLOCK - 1) // MOE_BLOCK) * MOE_BLOCK
    pad_end = jnp.cumsum(padded)
    pad_start = pad_end - padded
    n_blocks = (n_tok * TOP_K) // MOE_BLOCK + N_EXPERTS
    block_row = jnp.arange(n_blocks, dtype=i32) * MOE_BLOCK
    block_e = jnp.minimum(jnp.sum((pad_end[None, :] <= block_row[:, None]).astype(i32), axis=1), N_EXPERTS - 1)
    nblk = (pad_end[-1:] // MOE_BLOCK).astype(i32)
    e_row = jnp.arange(N_EXPERTS, dtype=i32)
    used = padded > 0
    e_ord = jnp.cumsum(used.astype(i32)) - 1
    later_used = jnp.logical_and(e_row[None, :] > e_row[:, None], used[None, :])
    e_next = jnp.min(jnp.where(later_used, e_row[None, :], N_EXPERTS), axis=1)
    e_next = jnp.where(e_next == N_EXPERTS, -1, e_next)
    of_block = block_e[:, None] == e_row[None, :]
    block_eord = jnp.sum(jnp.where(of_block, e_ord[None, :], 0), axis=1)
    block_enext = jnp.sum(jnp.where(of_block, e_next[None, :], 0), axis=1)
    block_left = jnp.sum(jnp.where(of_block, (pad_start + counts)[None, :], 0), axis=1) - block_row
    block_valid = jnp.clip(block_left, 0, MOE_BLOCK)
    phantom = lambda fill, dt: jnp.full((TOP_K, n_phantom), fill, dt)
    idx = jnp.concatenate([idx_p, idx_s, phantom(-1, i32)], axis=1)
    rank = jnp.concatenate([rank_p, rank_s, phantom(0, i32)], axis=1)
    gates = jnp.concatenate([gate_p, gate_s, phantom(0.0, F32)], axis=1)
    onehot = idx[None] == jnp.arange(N_EXPERTS, dtype=i32)[:, None, None]
    tile_cnt = jnp.sum(onehot.reshape(N_EXPERTS, TOP_K, n_tiles, ROW_TILE).astype(i32), axis=(1, 3)).T
    run_before = jnp.cumsum(tile_cnt, axis=0) - tile_cnt
    tile_off = jnp.cumsum(tile_cnt, axis=1) - tile_cnt
    tile_dst = pad_start[None, :] + run_before
    delta = jnp.repeat((tile_off - run_before).T, ROW_TILE, axis=1)
    pos = rank + jnp.sum(jnp.where(onehot, delta[:, None, :], 0), axis=0)
    pos = jnp.where(idx >= 0, pos, -1)
    tcnt, tdst, ttot = tile_cnt.reshape(-1), tile_dst.reshape(-1), jnp.sum(tile_cnt, axis=1)

    tail = ((0, n_phantom), (0, 0))
    x_sorted = _dispatch(tcnt, tdst, ttot, pad_start + counts, padded - counts, nblk, pos, xn2_p,
                         jnp.pad(xn2_s, tail), n_blocks * MOE_BLOCK)
    y_sorted = _experts(block_e, nblk, block_eord, block_enext, block_valid, x_sorted, wgu, bgu, wd, bd)
    y_p, y_s = _combine(tcnt, tdst, ttot, pos, gates, y_sorted, h1_p, jnp.pad(h1_s, tail), norm_final[None, :])

    y_prompt = y_p.reshape(batch, seq, D_MODEL)
    y_sample = y_s[:nb].reshape(nb, 1, D_MODEL)
    ret_state_prompt = s_fin[None]
    pool_state_prompt = p_fin[:, 1:, :][None]
    ret_state_sample = st_s[None]
    pool_state_sample = pool_s.reshape(nb, POOL_BUF, POOL_WIDTH)[None]
    return (y_prompt, y_sample, ret_state_prompt, pool_state_prompt, ret_state_sample, pool_state_sample)
```

```python
import jax
import jax.numpy as jnp
import numpy as np
from jax import lax
from jax.experimental import pallas as pl
from jax.experimental.pallas import tpu as pltpu

F32 = jnp.float32
BF16 = jnp.bfloat16

D_MODEL = 1024
N_META = 16
RET_HEADS = 4
RET_DK = 128
RET_DV = 256
RET_QK = RET_HEADS * RET_DK
RET_V = RET_HEADS * RET_DV
CHUNK = 128
ROPE_BASE = 10000.0
POOL_WINDOWS = (2, 4, 8, 16)
POOL_GROUPS = 4
POOL_GROUP_DIM = 128
POOL_WIDTH = POOL_GROUPS * POOL_GROUP_DIM
POOL_BUF = max(POOL_WINDOWS) - 1
N_EXPERTS = 32
TOP_K = 4
D_FF = D_MODEL
SWIGLU_LIMIT = 7.0
SWIGLU_ALPHA = 1.702
EPS = 1e-6
IN_WIDTHS = (RET_QK, RET_QK, RET_V, RET_V, POOL_WIDTH, D_MODEL, D_MODEL)
IN_TOTAL = sum(IN_WIDTHS)
IN_OFFS = tuple(int(s) for s in np.cumsum((0,) + IN_WIDTHS))

LANES = 128
ROW_CHUNKS = D_MODEL // (2 * LANES)
MIXER_TILE = 512
MOE_BLOCK = 512
ROW_TILE = 256
SUBLANES = 8
WEIGHT_CAST_ROWS = 128
VMEM_LIMIT = 56 * 1024 * 1024
PROMPT_VMEM_LIMIT = 60 * 1024 * 1024

assert N_META + 1 >= max(POOL_WINDOWS)
assert POOL_WINDOWS == (2, 4, 8, 16)


def _dot(a, b):
    return jnp.dot(a, b, preferred_element_type=F32)


def _rmsnorm(x, w):
    return x * lax.rsqrt(jnp.mean(x * x, axis=-1, keepdims=True) + EPS) * w


def _sigmoid(x):
    return 0.5 * jnp.tanh(0.5 * x) + 0.5


def _store_rows_as_tiles(ref, x):
    half = D_MODEL // 2
    hi = pltpu.bitcast(x[:, :half], jnp.uint32) & jnp.uint32(0xFFFF0000)
    lo = pltpu.bitcast(x[:, half:], jnp.uint32) >> 16
    ref[...] = (hi | lo).reshape(x.shape[0], ROW_CHUNKS, LANES)


def _load_rows_from_tiles(ref, rows):
    w = ref[...].reshape(rows, D_MODEL // 2)
    hi = pltpu.bitcast(w & jnp.uint32(0xFFFF0000), F32)
    lo = pltpu.bitcast(w << 16, F32)
    return jnp.concatenate([hi, lo], axis=1)


def _inproj_kernel(x_ref, nw_ref, w_ref, cos_ref, sin_ref,
                   q_ref, k_ref, v_ref, g_ref, p_ref, ga_ref, gb_ref, *, defer_gates=False):
    xn = _rmsnorm(x_ref[...], nw_ref[...]).astype(BF16)
    cos = cos_ref[...]
    sin = sin_ref[...]

    def seg(i):
        return _dot(xn, w_ref[:, IN_OFFS[i]:IN_OFFS[i + 1]])

    def rot(a):
        return a * cos + pltpu.roll(a, RET_DK // 2, 1) * sin

    q = seg(0)
    k = seg(1)
    for h in range(RET_HEADS):
        sl = slice(h * RET_DK, (h + 1) * RET_DK)
        q_ref[:, sl] = rot(q[:, sl]).astype(q_ref.dtype)
        k_ref[:, sl] = (rot(k[:, sl]) * (RET_DK ** -0.5)).astype(k_ref.dtype)
    v_ref[...] = seg(2).astype(v_ref.dtype)

    def gates():
        g = seg(3)
        g_ref[...] = (g * _sigmoid(g)).astype(g_ref.dtype)
        p_ref[...] = seg(4)
        ga_ref[...] = _sigmoid(seg(5)).astype(ga_ref.dtype)
        gb_ref[...] = _sigmoid(seg(6)).astype(gb_ref.dtype)

    if defer_gates:
        return gates
    gates()


def _sample_state_step(step, sdec_ref, qt_ref, kt_ref, q_ref, k_ref, v_ref, st_ref, stout_ref, o_ref):
    grp = st_ref.shape[0]
    row0 = step * grp
    if grp % SUBLANES == 0:
        rows = pl.ds(pl.multiple_of(row0, SUBLANES), grp)
        q8, k8, v8 = q_ref[rows, :], k_ref[rows, :], v_ref[rows, :]
    else:
        assert 2 * grp == SUBLANES
        rows = pl.ds(pl.multiple_of((step // 2) * SUBLANES, SUBLANES), SUBLANES)
        first = step % 2 == 0
        q8, k8, v8 = (jnp.where(first, a[:grp], a[grp:]) for a in (q_ref[rows, :], k_ref[rows, :], v_ref[rows, :]))
    for h in range(RET_HEADS):
        ksl = slice(h * RET_DK, (h + 1) * RET_DK)
        vsl = slice(h * RET_DV, (h + 1) * RET_DV)
        score = jnp.sum(q8[:, ksl] * k8[:, ksl], axis=1, keepdims=True) * sdec_ref[h, 0]
        intra = score * v8[:, vsl]
        for bb in range(grp):
            s_old = st_ref[bb, h]
            qcol = qt_ref[0, ksl, bb:bb + 1] * sdec_ref[h, 1]
            kcol = kt_ref[0, ksl, bb:bb + 1] * sdec_ref[h, 2]
            cross = jnp.sum(s_old * qcol, axis=0, keepdims=True)
            o_ref[pl.ds(row0 + bb, 1), vsl] = intra[bb:bb + 1, :] + cross
            stout_ref[bb, h] = s_old * sdec_ref[h, 3] + kcol * v8[bb:bb + 1, vsl]


def _inproj_small(x2d, nw, w_in_bf, cosf, sinf):
    rows = x2d.shape[0]
    const = lambda i: (0, 0)
    return pl.pallas_call(
        _inproj_kernel,
        grid=(1,),
        in_specs=[
            pl.BlockSpec((rows, D_MODEL), const),
            pl.BlockSpec((1, D_MODEL), const),
            pl.BlockSpec((D_MODEL, IN_TOTAL), const, pipeline_mode=pl.Buffered(1)),
            pl.BlockSpec((rows, RET_DK), const),
            pl.BlockSpec((rows, RET_DK), const),
        ],
        out_specs=[pl.BlockSpec((rows, w), const) for w in IN_WIDTHS],
        out_shape=[jax.ShapeDtypeStruct((rows, w), F32) for w in IN_WIDTHS],
        compiler_params=pltpu.CompilerParams(dimension_semantics=("arbitrary",), vmem_limit_bytes=VMEM_LIMIT),
        name="inproj",
    )(x2d, nw, w_in_bf, cosf, sinf)


def _group_norm(o, gn_row):
    mu = jnp.mean(o, axis=-1, keepdims=True)
    var = jnp.mean(jnp.square(o - mu), axis=-1, keepdims=True)
    return (o - mu) * lax.rsqrt(var + EPS) * gn_row


def _pool_branch(groups, poolw_ref, pscale_ref, wpool_ref):
    pm = [_dot(g.astype(BF16), poolw_ref[i]) for i, g in enumerate(groups)]
    pm = jnp.concatenate(pm, axis=1) * pscale_ref[...]
    return _dot(pm.astype(BF16), wpool_ref[...])


def _merge_tail(o_norm, silu_g, sig_a, sig_b, yb, x, wret_ref, wout_ref):
    ya = _dot((silu_g.astype(F32) * o_norm).astype(BF16), wret_ref[...])
    merged = sig_a.astype(F32) * ya + sig_b.astype(F32) * yb
    return x + _dot(merged.astype(BF16), wout_ref[...])


def _route(h1, nffn_ref, wrt_ref, br_ref, run_scr,
           xn2_ref, idx_ref, gate_ref, rank_ref):
    tm = h1.shape[0]
    xn2 = _rmsnorm(h1, nffn_ref[...]).astype(BF16)
    xn2_ref[...] = xn2
    logits = lax.dot_general(wrt_ref[...], xn2, (((1,), (1,)), ((), ())),
                             preferred_element_type=F32) + br_ref[...]
    e_iota = lax.broadcasted_iota(jnp.int32, (N_EXPERTS, tm), 0)
    work = logits
    vals, sels = [], []
    chosen = jnp.zeros((N_EXPERTS, tm), F32)
    for _ in range(TOP_K):
        m = jnp.max(work, axis=0, keepdims=True)
        sel = jnp.min(jnp.where(work == m, e_iota, N_EXPERTS), axis=0, keepdims=True)
        hit = e_iota == sel
        vals.append(m)
        sels.append(sel)
        chosen = jnp.where(hit, 1.0, chosen)
        work = jnp.where(hit, -jnp.inf, work)
    exps = [jnp.exp(v - vals[0]) for v in vals]
    denom = exps[0] + exps[1] + exps[2] + exps[3]
    gates = [e / denom for e in exps]
    r_i = lax.broadcasted_iota(jnp.int32, (tm, tm), 0)
    c_i = lax.broadcasted_iota(jnp.int32, (tm, tm), 1)
    before = jnp.where(r_i < c_i, 1.0, 0.0).astype(BF16)
    base = run_scr[...] + _dot(chosen.astype(BF16), before)
    for kk in range(TOP_K):
        rk = jnp.sum(jnp.where(e_iota == sels[kk], base, 0.0), axis=0, keepdims=True)
        rank_ref[kk:kk + 1, :] = rk.astype(jnp.int32)
        idx_ref[kk:kk + 1, :] = sels[kk]
        gate_ref[kk:kk + 1, :] = gates[kk]
    run_scr[...] = run_scr[...] + jnp.sum(chosen, axis=1, keepdims=True)


def _mixer_kernel(q_ref, k_ref, v_ref, g_ref, p_ref, ga_ref, gb_ref, x_ref,
                  kmeta_ref, vmeta_ref, pmeta_ref, mask_ref, qdec_ref, kdec_ref, cdec_ref, gn_ref,
                  poolw_ref, pscale_ref, wret_ref, wpool_ref, wout_ref, nffn_ref, wrt_ref, br_ref,
                  cnt0_ref,
                  h1_ref, xn2_ref, idx_ref, gate_ref, rank_ref, cnt_ref, sfin_ref, pfin_ref,
                  s_scr, ext_scr, o_scr, run_scr, *, after_first_pass=None):
    b = pl.program_id(0)
    j = pl.program_id(1)
    nj = pl.num_programs(1)
    tm = q_ref.shape[0]

    def state_update(s_old, kc, vc, h):
        kd = (kc.astype(F32) * kdec_ref[h]).astype(BF16)
        upd = lax.dot_general(kd, vc, (((0,), (0,)), ((), ())), preferred_element_type=F32)
        return s_old * cdec_ref[h] + upd

    @pl.when(jnp.logical_and(b == 0, j == 0))
    def _():
        run_scr[...] = cnt0_ref[:, 0:1]

    @pl.when(j == 0)
    def _():
        for h in range(RET_HEADS):
            kc = kmeta_ref[:, h * RET_DK:(h + 1) * RET_DK]
            vc = vmeta_ref[:, h * RET_DV:(h + 1) * RET_DV]
            s_scr[h] = state_update(jnp.zeros((RET_DK, RET_DV), F32), kc, vc, h)
        ext_scr[0:N_META, :] = pmeta_ref[...]

    n_chunks = tm // CHUNK
    units = [(c, h) for c in range(n_chunks) for h in range(RET_HEADS)]

    def operands(c, h):
        rows = slice(c * CHUNK, (c + 1) * CHUNK)
        return (q_ref[rows, h * RET_DK:(h + 1) * RET_DK], k_ref[rows, h * RET_DK:(h + 1) * RET_DK],
                v_ref[rows, h * RET_DV:(h + 1) * RET_DV])

    lhs, upd = {}, {}
    for c, h in units:
        qc, kc, vc = operands(c, h)
        scores = lax.dot_general(qc, kc, (((1,), (1,)), ((), ())), preferred_element_type=F32) * mask_ref[h]
        qd = (qc.astype(F32) * qdec_ref[h]).astype(BF16)
        lhs[c, h] = jnp.concatenate([scores.astype(BF16), qd], axis=1)
        kd = (kc.astype(F32) * kdec_ref[h]).astype(BF16)
        upd[c, h] = lax.dot_general(kd, vc, (((0,), (0,)), ((), ())), preferred_element_type=F32)
    if after_first_pass is not None:
        after_first_pass()
    state = {}
    for h in range(RET_HEADS):
        s = s_scr[h]
        for c in range(n_chunks):
            state[c, h] = s
            s = s * cdec_ref[h] + upd[c, h]
        s_scr[h] = s
    for c, h in units:
        rhs = jnp.concatenate([operands(c, h)[2], state[c, h].astype(BF16)], axis=0)
        o = _dot(lhs[c, h], rhs)
        o_scr[c * CHUNK:(c + 1) * CHUNK, h * RET_DV:(h + 1) * RET_DV] = _group_norm(
            o, gn_ref[:, h * RET_DV:(h + 1) * RET_DV])

    p = p_ref[...]
    ext_scr[N_META:N_META + tm, :] = p
    a = ext_scr[...]
    g1 = POOL_GROUP_DIM
    s2 = a + pltpu.roll(a, 1, 0)
    s4 = s2[:, g1:] + pltpu.roll(s2[:, g1:], 2, 0)
    s8 = s4[:, g1:] + pltpu.roll(s4[:, g1:], 4, 0)
    s16 = s8[:, g1:] + pltpu.roll(s8[:, g1:], 8, 0)
    sums = (s2[N_META:, :g1], s4[N_META:, :g1], s8[N_META:, :g1], s16[N_META:, :])
    groups = [sums[i] * (1.0 / POOL_WINDOWS[i]) - p[:, i * g1:(i + 1) * g1] for i in range(POOL_GROUPS)]
    ext_scr[0:N_META, :] = ext_scr[tm:tm + N_META, :]

    yb = _pool_branch(groups, poolw_ref, pscale_ref, wpool_ref)
    h1 = _merge_tail(o_scr[...], g_ref[...], ga_ref[...], gb_ref[...], yb, x_ref[...], wret_ref, wout_ref)
    h1_ref[...] = h1
    _route(h1, nffn_ref, wrt_ref, br_ref, run_scr, xn2_ref, idx_ref, gate_ref, rank_ref)
    cnt_ref[...] = jnp.broadcast_to(run_scr[...], cnt_ref.shape)

    @pl.when(j == nj - 1)
    def _():
        for h in range(RET_HEADS):
            sfin_ref[0, h] = s_scr[h]
        pfin_ref[0] = ext_scr[0:N_META, :]


N_SAMPLE_IN = 7
N_MIXER_IN = 17
N_MIXER_OUT = 8
N_PROJ = len(IN_WIDTHS)


def _prompt_kernel(*refs):
    it = iter(refs)
    take = lambda n: [next(it) for _ in range(n)]
    x_ref, nw_ref, w_ref, cos_ref, sin_ref = take(5)
    sample_in = take(N_SAMPLE_IN)
    mixer_in = take(N_MIXER_IN)
    mixer_out = take(N_MIXER_OUT)
    stout_ref, os_ref = take(2)
    proj_scr = take(N_PROJ)
    mixer_scr = take(4)
    gate_columns = _inproj_kernel(x_ref, nw_ref, w_ref, cos_ref, sin_ref, *proj_scr, defer_gates=True)
    step = pl.program_id(0) * pl.num_programs(1) + pl.program_id(1)
    _sample_state_step(step, *sample_in, stout_ref, os_ref)
    _mixer_kernel(*proj_scr, x_ref, *mixer_in, *mixer_out, *mixer_scr, after_first_pass=gate_columns)


def _prompt_layer(x2d, nw, w_in_bf, cosf, sinf, sample, kmeta, vmeta, pmeta, dec, wts, cnt0, batch, seq):
    tm = MIXER_TILE
    nj = seq // tm
    rows = batch * seq
    row_map = lambda b, j: (b * nj + j, 0)
    lane_map = lambda b, j: (0, b * nj + j)
    c2 = lambda b, j: (0, 0)
    c3 = lambda b, j: (0, 0, 0)

    def whole(a):
        mode = dict(pipeline_mode=pl.Buffered(1)) if a.size * a.dtype.itemsize >= 512 * 1024 else {}
        return pl.BlockSpec(a.shape, c2 if a.ndim == 2 else c3, **mode)

    sdec, qt, kt, qs, ks, vs, state = sample
    nb = state.shape[0]
    grp = nb // (batch * nj)
    assert grp * batch * nj == nb and qt.shape == (nb // grp, RET_QK, grp)
    step_map = lambda b, j: (b * nj + j, 0, 0)
    st_spec = pl.BlockSpec((grp, RET_HEADS, RET_DK, RET_DV), lambda b, j: (b * nj + j, 0, 0, 0))
    mask, qdec, kdec, cdec = dec
    mixer_in = [kmeta, vmeta, pmeta, mask, qdec, kdec, cdec, *wts, cnt0]
    assert len(mixer_in) == N_MIXER_IN
    in_arrays = [x2d, nw, w_in_bf, cosf, sinf, sdec, qt, kt, qs, ks, vs, state] + mixer_in
    in_specs = [
        pl.BlockSpec((tm, D_MODEL), row_map),
        pl.BlockSpec((1, D_MODEL), c2),
        pl.BlockSpec((D_MODEL, IN_TOTAL), c2, pipeline_mode=pl.Buffered(1)),
        pl.BlockSpec((tm, RET_DK), lambda b, j: (j, 0)),
        pl.BlockSpec((tm, RET_DK), lambda b, j: (j, 0)),
        pl.BlockSpec(memory_space=pltpu.SMEM),
        pl.BlockSpec((1, RET_QK, grp), step_map), pl.BlockSpec((1, RET_QK, grp), step_map),
        pl.BlockSpec(qs.shape, c2), pl.BlockSpec(ks.shape, c2), pl.BlockSpec(vs.shape, c2),
        st_spec,
    ] + [whole(a) for a in mixer_in]
    out_shape = [
        jax.ShapeDtypeStruct((rows, D_MODEL), F32),
        jax.ShapeDtypeStruct((rows, D_MODEL), BF16),
        jax.ShapeDtypeStruct((TOP_K, rows), jnp.int32),
        jax.ShapeDtypeStruct((TOP_K, rows), F32),
        jax.ShapeDtypeStruct((TOP_K, rows), jnp.int32),
        jax.ShapeDtypeStruct((N_EXPERTS, LANES), F32),
        jax.ShapeDtypeStruct((batch, RET_HEADS, RET_DK, RET_DV), F32),
        jax.ShapeDtypeStruct((batch, N_META, POOL_WIDTH), F32),
        jax.ShapeDtypeStruct(state.shape, F32),
        jax.ShapeDtypeStruct((nb, RET_V), F32),
    ]
    out_specs = [
        pl.BlockSpec((tm, D_MODEL), row_map),
        pl.BlockSpec((tm, D_MODEL), row_map),
        pl.BlockSpec((TOP_K, tm), lane_map),
        pl.BlockSpec((TOP_K, tm), lane_map),
        pl.BlockSpec((TOP_K, tm), lane_map),
        pl.BlockSpec((N_EXPERTS, LANES), c2),
        pl.BlockSpec((1, RET_HEADS, RET_DK, RET_DV), lambda b, j: (b, 0, 0, 0)),
        pl.BlockSpec((1, N_META, POOL_WIDTH), lambda b, j: (b, 0, 0)),
        st_spec,
        pl.BlockSpec((nb, RET_V), c2),
    ]
    proj_dts = (BF16, BF16, BF16, BF16, F32, BF16, BF16)
    return pl.pallas_call(
        _prompt_kernel,
        grid=(batch, nj),
        in_specs=in_specs,
        out_specs=out_specs,
        out_shape=out_shape,
        scratch_shapes=[pltpu.VMEM((tm, w), dt) for w, dt in zip(IN_WIDTHS, proj_dts)] + [
            pltpu.VMEM((RET_HEADS, RET_DK, RET_DV), F32),
            pltpu.VMEM((N_META + tm, POOL_WIDTH), F32),
            pltpu.VMEM((tm, RET_V), F32),
            pltpu.VMEM((N_EXPERTS, 1), F32),
        ],
        compiler_params=pltpu.CompilerParams(
            dimension_semantics=("arbitrary", "arbitrary"), vmem_limit_bytes=PROMPT_VMEM_LIMIT),
        name="prompt_layer",
    )(*in_arrays)


def _sample_kernel(o_ref, g_ref, p_ref, ga_ref, gb_ref, x_ref, pool_ref, gn_ref,
                   poolw_ref, pscale_ref, wret_ref, wpool_ref, wout_ref, nffn_ref, wrt_ref, br_ref, cnt0_ref,
                   poolout_ref, h1_ref, xn2_ref, idx_ref, gate_ref, rank_ref, cnt_ref,
                   run_scr):
    run_scr[...] = cnt0_ref[:, 0:1]
    o = o_ref[...]
    o_norm = jnp.concatenate(
        [_group_norm(o[:, h * RET_DV:(h + 1) * RET_DV], gn_ref[:, h * RET_DV:(h + 1) * RET_DV])
         for h in range(RET_HEADS)], axis=1)
    p = p_ref[...]
    w = POOL_WIDTH
    g1 = POOL_GROUP_DIM

    def prev(r, lo):
        return pool_ref[:, r * w + lo:(r + 1) * w]

    s2 = p + prev(14, 0)
    s4 = s2[:, g1:] + prev(13, g1) + prev(12, g1)
    s8 = s4[:, g1:] + prev(11, 2 * g1) + prev(10, 2 * g1) + prev(9, 2 * g1) + prev(8, 2 * g1)
    s16 = s8[:, g1:]
    for r in range(7, -1, -1):
        s16 = s16 + prev(r, 3 * g1)
    sums = (s2[:, :g1], s4[:, :g1], s8[:, :g1], s16)
    groups = [sums[t] * (1.0 / POOL_WINDOWS[t]) - p[:, t * g1:(t + 1) * g1] for t in range(POOL_GROUPS)]
    poolout_ref[:, 0:(POOL_BUF - 1) * w] = pool_ref[:, w:POOL_BUF * w]
    poolout_ref[:, (POOL_BUF - 1) * w:] = p
    yb = _pool_branch(groups, poolw_ref, pscale_ref, wpool_ref)
    h1 = _merge_tail(o_norm, g_ref[...], ga_ref[...], gb_ref[...], yb, x_ref[...], wret_ref, wout_ref)
    h1_ref[...] = h1
    _route(h1, nffn_ref, wrt_ref, br_ref, run_scr, xn2_ref, idx_ref, gate_ref, rank_ref)
    cnt_ref[...] = jnp.broadcast_to(run_scr[...], cnt_ref.shape)


def _sample_mixer(o_s, proj, x2d, pool2d, wts, cnt0):
    _, _, _, g, p, ga, gb = proj
    nb = x2d.shape[0]
    c2 = lambda i: (0, 0)

    def whole(a):
        return pl.BlockSpec(a.shape, c2)

    gn, poolw, pscale, wret, wpool, wout, nffn, wrt, br = wts
    in_arrays = [o_s, g, p, ga, gb, x2d, pool2d, gn, poolw, pscale, wret, wpool, wout, nffn, wrt, br, cnt0]
    in_specs = [whole(a) for a in in_arrays[:8]]
    in_specs += [pl.BlockSpec(poolw.shape, lambda i: (0, 0, 0))]
    in_specs += [whole(a) for a in in_arrays[9:]]
    out_shape = [
        jax.ShapeDtypeStruct(pool2d.shape, F32),
        jax.ShapeDtypeStruct((nb, D_MODEL), F32),
        jax.ShapeDtypeStruct((nb, D_MODEL), BF16),
        jax.ShapeDtypeStruct((TOP_K, nb), jnp.int32),
        jax.ShapeDtypeStruct((TOP_K, nb), F32),
        jax.ShapeDtypeStruct((TOP_K, nb), jnp.int32),
        jax.ShapeDtypeStruct((N_EXPERTS, LANES), F32),
    ]
    return pl.pallas_call(
        _sample_kernel,
        grid=(1,),
        in_specs=in_specs,
        out_specs=[pl.BlockSpec(s.shape, c2) for s in out_shape],
        out_shape=out_shape,
        scratch_shapes=[pltpu.VMEM((N_EXPERTS, 1), F32)],
        compiler_params=pltpu.CompilerParams(
            dimension_semantics=("arbitrary",), vmem_limit_bytes=VMEM_LIMIT),
        name="sample_mixer",
    )(*in_arrays)


TILE_ROWS = ROW_TILE * TOP_K


def _run_copy(src_ref, src_row, dst_ref, dst_row, n_rows, sem):
    return pltpu.make_async_copy(src_ref.at[pl.ds(src_row, n_rows)], dst_ref.at[pl.ds(dst_row, n_rows)], sem)


def _for_each_run(tcnt_ref, tile, fn):
    def body(e, off):
        n = tcnt_ref[tile * N_EXPERTS + e]

        @pl.when(n > 0)
        def _():
            fn(e, off, n)

        return off + n

    lax.fori_loop(0, N_EXPERTS, body, 0)


def _dispatch_kernel(tcnt_ref, tdst_ref, ttot_ref, zrow_ref, zcnt_ref, nblk_ref, pos_ref, xp_ref, xs_ref, out_ref,
                     sorted_scr, zero_scr, sem, zsem):
    i = pl.program_id(0)
    n_prompt = pl.num_programs(0) - 1
    n_blocks = out_ref.shape[0] // MOE_BLOCK
    slot = i % 2

    def wait_tile(tile, s):
        _run_copy(out_ref, 0, sorted_scr.at[s], 0, ttot_ref[tile], sem.at[s]).wait()

    def for_each_pad(fn):
        def body(e, carry):
            n = zcnt_ref[e]

            @pl.when(n > 0)
            def _():
                fn(_run_copy(zero_scr, 0, out_ref, zrow_ref[e], n, zsem))

            dead = nblk_ref[0] + e

            @pl.when(dead < n_blocks)
            def _():
                fn(_run_copy(zero_scr, 0, out_ref, dead * MOE_BLOCK, MOE_BLOCK, zsem))

            return carry

        lax.fori_loop(0, N_EXPERTS, body, 0)

    @pl.when(i == 0)
    def _():
        zero_scr[...] = jnp.zeros_like(zero_scr)
        for_each_pad(lambda cp: cp.start())

    def sort_tile(x_ref):
        r_iota = lax.broadcasted_iota(jnp.int32, (TILE_ROWS, ROW_TILE), 0)
        hit = r_iota == pos_ref[0:1, :]
        for kk in range(1, TOP_K):
            hit = jnp.logical_or(hit, r_iota == pos_ref[kk:kk + 1, :])
        perm = jnp.where(hit, 1.0, 0.0).astype(BF16)
        xs = _dot(perm, x_ref[...])

        @pl.when(i >= 2)
        def _():
            wait_tile(i - 2, slot)

        _store_rows_as_tiles(sorted_scr.at[slot], xs)

    @pl.when(i < n_prompt)
    def _():
        sort_tile(xp_ref)

    @pl.when(i == n_prompt)
    def _():
        sort_tile(xs_ref)

    _for_each_run(tcnt_ref, i, lambda e, off, n: _run_copy(
        sorted_scr.at[slot], off, out_ref, tdst_ref[i * N_EXPERTS + e], n, sem.at[slot]).start())

    @pl.when(i == n_prompt)
    def _():
        wait_tile(i - 1, 1 - slot)
        wait_tile(i, slot)
        for_each_pad(lambda cp: cp.wait())


def _dispatch(tcnt, tdst, ttot, zrow, zcnt, nblk, pos, xn2_p, xn2_s, n_sorted):
    n_prompt = xn2_p.shape[0] // ROW_TILE
    grid_spec = pltpu.PrefetchScalarGridSpec(
        num_scalar_prefetch=6,
        grid=(n_prompt + 1,),
        in_specs=[
            pl.BlockSpec((TOP_K, ROW_TILE), lambda i, *_: (0, i)),
            pl.BlockSpec((ROW_TILE, D_MODEL), lambda i, *_: (jnp.minimum(i, n_prompt - 1), 0)),
            pl.BlockSpec((ROW_TILE, D_MODEL), lambda i, *_: (0, 0)),
        ],
        out_specs=pl.BlockSpec(memory_space=pl.ANY),
        scratch_shapes=[
            pltpu.VMEM((2, TILE_ROWS, ROW_CHUNKS, LANES), jnp.uint32),
            pltpu.VMEM((MOE_BLOCK, ROW_CHUNKS, LANES), jnp.uint32),
            pltpu.SemaphoreType.DMA((2,)), pltpu.SemaphoreType.DMA(()),
        ],
    )
    return pl.pallas_call(
        _dispatch_kernel,
        grid_spec=grid_spec,
        out_shape=jax.ShapeDtypeStruct((n_sorted, ROW_CHUNKS, LANES), jnp.uint32),
        compiler_params=pltpu.CompilerParams(dimension_semantics=("arbitrary",)),
        name="moe_dispatch",
    )(tcnt, tdst, ttot, zrow, zcnt, nblk, pos, xn2_p, xn2_s)


def _expert_kernel(be_ref, nblk_ref, eord_ref, enext_ref, bvalid_ref, x_ref, wgu_ref, bgu_ref, wd_ref, bd_ref, y_ref,
                   wgu_f32, wd_f32, wgu_bf, wd_bf, wsem):
    i = pl.program_id(0)
    live = i < nblk_ref[0]
    new_expert = jnp.logical_or(i == 0, be_ref[i] != be_ref[jnp.maximum(i - 1, 0)])

    def weight_copies(e, s):
        return (pltpu.make_async_copy(wgu_ref.at[e], wgu_f32.at[s], wsem.at[0, s]),
                pltpu.make_async_copy(wd_ref.at[e], wd_f32.at[s], wsem.at[1, s]))

    @pl.when(jnp.logical_and(live, new_expert))
    def _():
        slot = eord_ref[i] % 2

        @pl.when(i == 0)
        def _():
            for cp in weight_copies(be_ref[0], 0):
                cp.start(priority=1)

        for cp in weight_copies(be_ref[i], slot):
            cp.wait()

        @pl.when(enext_ref[i] >= 0)
        def _():
            for cp in weight_copies(enext_ref[i], 1 - slot):
                cp.start(priority=1)

    @pl.when(jnp.logical_not(live))
    def _():
        y_ref[...] = jnp.zeros_like(y_ref)

    def ffn(rows, first):
        if first:
            slot = eord_ref[i] % 2
            wgu = wgu_f32[slot].astype(BF16)
            wgu_bf[...] = wgu
        else:
            wgu = wgu_bf[...]
        x = _load_rows_from_tiles(x_ref.at[pl.ds(0, rows)], rows)
        h = _dot(x.astype(BF16), wgu) + bgu_ref[0]
        gate = jnp.minimum(h[:, :D_FF], SWIGLU_LIMIT)
        up = jnp.clip(h[:, D_FF:], -SWIGLU_LIMIT, SWIGLU_LIMIT)
        glu = gate * _sigmoid(gate * SWIGLU_ALPHA)
        if first:
            wd_b = wd_f32[slot].astype(BF16)
            wd_bf[...] = wd_b
        else:
            wd_b = wd_bf[...]
        y = _dot(((up + 1.0) * glu).astype(BF16), wd_b) + bd_ref[0]
        _store_rows_as_tiles(y_ref.at[pl.ds(0, rows)], y.astype(BF16).astype(F32))

    half = MOE_BLOCK // 2
    short = bvalid_ref[i] <= half
    for first in (False, True):
        same = new_expert if first else jnp.logical_not(new_expert)

        @pl.when(jnp.logical_and(jnp.logical_and(live, same), jnp.logical_not(short)))
        def _(first=first):
            ffn(MOE_BLOCK, first)

        @pl.when(jnp.logical_and(jnp.logical_and(live, same), short))
        def _(first=first):
            ffn(half, first)
            y_ref[pl.ds(half, half)] = jnp.zeros((half,) + y_ref.shape[1:], y_ref.dtype)


def _experts(block_e, nblk, block_eord, block_enext, block_valid, x_sorted, wgu, bgu, wd, bd):
    n_blocks = x_sorted.shape[0] // MOE_BLOCK
    wmap = lambda i, be, *_: (be[i], 0, 0)
    rmap = lambda i, *_: (i, 0, 0)
    hbm = pl.BlockSpec(memory_space=pl.ANY)
    grid_spec = pltpu.PrefetchScalarGridSpec(
        num_scalar_prefetch=5,
        grid=(n_blocks,),
        in_specs=[
            pl.BlockSpec((MOE_BLOCK, ROW_CHUNKS, LANES), rmap),
            hbm,
            pl.BlockSpec((1, 1, 2 * D_FF), wmap),
            hbm,
            pl.BlockSpec((1, 1, D_MODEL), wmap),
        ],
        out_specs=pl.BlockSpec((MOE_BLOCK, ROW_CHUNKS, LANES), rmap),
        scratch_shapes=[
            pltpu.VMEM((2, D_MODEL, 2 * D_FF), F32), pltpu.VMEM((2, D_FF, D_MODEL), F32),
            pltpu.VMEM((D_MODEL, 2 * D_FF), BF16), pltpu.VMEM((D_FF, D_MODEL), BF16),
            pltpu.SemaphoreType.DMA((2, 2)),
        ],
    )
    return pl.pallas_call(
        _expert_kernel,
        grid_spec=grid_spec,
        out_shape=jax.ShapeDtypeStruct(x_sorted.shape, jnp.uint32),
        compiler_params=pltpu.CompilerParams(
            dimension_semantics=("arbitrary",), vmem_limit_bytes=VMEM_LIMIT),
        name="moe_experts",
    )(block_e, nblk, block_eord, block_enext, block_valid, x_sorted, wgu, bgu, wd, bd)


def _combine_kernel(tcnt_ref, tdst_ref, ttot_ref, pos_ref, gate_ref, ys_ref, hp_ref, hs_ref, nf_ref, yp_ref, ysmp_ref,
                    runs_scr, sem):
    i = pl.program_id(0)
    n_tiles = pl.num_programs(0)
    n_prompt = n_tiles - 1
    slot = i % 2

    def start_runs(tile, s):
        _for_each_run(tcnt_ref, tile, lambda e, off, n: _run_copy(
            ys_ref, tdst_ref[tile * N_EXPERTS + e], runs_scr.at[s], off, n, sem.at[s]).start())

    @pl.when(i == 0)
    def _():
        start_runs(0, 0)

    @pl.when(i + 1 < n_tiles)
    def _():
        start_runs(i + 1, 1 - slot)

    _run_copy(ys_ref, 0, runs_scr.at[slot], 0, ttot_ref[i], sem.at[slot]).wait()
    ys = _load_rows_from_tiles(runs_scr.at[slot], TILE_ROWS).astype(BF16)
    r_iota = lax.broadcasted_iota(jnp.int32, (TILE_ROWS, ROW_TILE), 0)
    gmat = jnp.zeros((TILE_ROWS, ROW_TILE), F32)
    for kk in range(TOP_K):
        gmat = jnp.where(r_iota == pos_ref[kk:kk + 1, :], gate_ref[kk:kk + 1, :], gmat)
    moe = lax.dot_general(gmat.astype(BF16), ys, (((0,), (0,)), ((), ())), preferred_element_type=F32)

    @pl.when(i < n_prompt)
    def _():
        yp_ref[...] = _rmsnorm(hp_ref[...] + moe, nf_ref[...])

    @pl.when(i == n_prompt)
    def _():
        ysmp_ref[...] = _rmsnorm(hs_ref[...] + moe, nf_ref[...])


def _combine(tcnt, tdst, ttot, pos, gates, y_sorted, h1_p, h1_s, nf):
    n_prompt = h1_p.shape[0] // ROW_TILE
    pmap = lambda i, *_: (jnp.minimum(i, n_prompt - 1), 0)
    smap = lambda i, *_: (0, 0)
    lmap = lambda i, *_: (0, i)
    grid_spec = pltpu.PrefetchScalarGridSpec(
        num_scalar_prefetch=3,
        grid=(n_prompt + 1,),
        in_specs=[
            pl.BlockSpec((TOP_K, ROW_TILE), lmap),
            pl.BlockSpec((TOP_K, ROW_TILE), lmap),
            pl.BlockSpec(memory_space=pl.ANY),
            pl.BlockSpec((ROW_TILE, D_MODEL), pmap),
            pl.BlockSpec((ROW_TILE, D_MODEL), smap),
            pl.BlockSpec((1, D_MODEL), smap),
        ],
        out_specs=[pl.BlockSpec((ROW_TILE, D_MODEL), pmap), pl.BlockSpec((ROW_TILE, D_MODEL), smap)],
        scratch_shapes=[pltpu.VMEM((2, TILE_ROWS, ROW_CHUNKS, LANES), jnp.uint32), pltpu.SemaphoreType.DMA((2,))],
    )
    return pl.pallas_call(
        _combine_kernel,
        grid_spec=grid_spec,
        out_shape=[jax.ShapeDtypeStruct(h1_p.shape, F32), jax.ShapeDtypeStruct(h1_s.shape, F32)],
        compiler_params=pltpu.CompilerParams(dimension_semantics=("arbitrary",)),
        name="moe_combine",
    )(tcnt, tdst, ttot, pos, gates, y_sorted, h1_p, h1_s, nf)


def _rotary_tables(pos):
    f = np.float32
    inv = np.power(f(ROPE_BASE), -np.arange(0, RET_DK, 2, dtype=f) / f(RET_DK)).astype(f)
    ang = (np.asarray(pos, f)[:, None] * inv[None, :]).astype(f)
    cos, sin = np.cos(ang).astype(f), np.sin(ang).astype(f)
    return np.concatenate([cos, cos], axis=1), np.concatenate([-sin, sin], axis=1)


def _decay_tables(chunk):
    f = np.float32
    log_g = np.log1p(-np.exp2(f(-5.0) - np.arange(RET_HEADS, dtype=f))).astype(f)
    i = np.arange(chunk, dtype=f)
    diff = i[:, None] - i[None, :]
    mask = np.where(diff[None] >= 0, np.exp(np.maximum(diff, f(0.0))[None] * log_g[:, None, None]), f(0.0)).astype(f)
    q_dec = np.exp((i + f(1.0))[None, :] * log_g[:, None]).astype(f)
    k_dec = np.exp((f(chunk) - f(1.0) - i)[None, :] * log_g[:, None]).astype(f)
    c_dec = np.exp(f(chunk) * log_g).astype(f)
    return mask, q_dec, k_dec, c_dec


def kernel(x_prompt, x_sample, state_ret, state_pool, meta_tokens, norm_mix, w_in, ret_gn, pool_w, pool_scale,
           w_ret_branch, w_pool_branch, w_out, norm_ffn, w_router, b_router, w_gate_up, b_gate_up, w_down, b_down,
           norm_final):
    batch, seq, _ = x_prompt.shape
    nb = x_sample.shape[0]
    past_len = 16384
    n_prompt_tok = batch * seq
    n_tok = n_prompt_tok + nb

    w_in_bf = w_in[0].astype(BF16)
    wts = (ret_gn[0][None, :], pool_w[0].astype(BF16), pool_scale[0][None, :],
           w_ret_branch[0].astype(BF16), w_pool_branch[0].astype(BF16), w_out[0].astype(BF16),
           norm_ffn[0][None, :], w_router[0].T.astype(BF16), b_router[0][:, None])
    wgu = w_gate_up[0]
    wd = w_down[0]
    bgu = b_gate_up[0][:, None, :]
    bd = b_down[0][:, None, :]
    nmix = norm_mix[0][None, :]

    cos_p, sin_p = _rotary_tables(N_META + np.arange(seq))
    cos_s, sin_s = _rotary_tables(np.concatenate([np.arange(N_META), np.full((nb,), past_len)]))
    mask, q_dec, k_dec, c_dec = _decay_tables(CHUNK)
    dec = (mask,
           np.ascontiguousarray(np.broadcast_to(q_dec[:, :, None], (RET_HEADS, CHUNK, RET_DK))),
           np.ascontiguousarray(np.broadcast_to(k_dec[:, :, None], (RET_HEADS, CHUNK, RET_DK))),
           np.ascontiguousarray(np.broadcast_to(c_dec[:, None, None], (RET_HEADS, 1, RET_DV))))
    m1, q1, k1, c1 = _decay_tables(1)
    sdec = np.stack([m1[:, 0, 0], q1[:, 0], k1[:, 0], c1], axis=1)

    x2d = x_prompt.reshape(n_prompt_tok, D_MODEL)
    xs2d = x_sample.reshape(nb, D_MODEL)
    x_small = jnp.concatenate([meta_tokens, xs2d], axis=0)
    proj_small = _inproj_small(x_small, nmix, w_in_bf, cos_s, sin_s)
    proj_s = tuple(a[N_META:] for a in proj_small)

    lead = CHUNK - N_META
    kmeta = jnp.pad(proj_small[1][:N_META], ((lead, 0), (0, 0))).astype(BF16)
    vmeta = jnp.pad(proj_small[2][:N_META], ((lead, 0), (0, 0))).astype(BF16)
    pmeta = proj_small[4][:N_META]

    grp = nb // (n_prompt_tok // MIXER_TILE)

    def cols(a):
        return a.T.reshape(RET_QK, nb // grp, grp).transpose(1, 0, 2)

    (h1_p, xn2_p, idx_p, gate_p, rank_p, cnt_p, s_fin, p_fin, st_s, o_s) = _prompt_layer(
        x2d, nmix, w_in_bf, cos_p, sin_p,
        (sdec, cols(proj_s[0]), cols(proj_s[1]), proj_s[0], proj_s[1], proj_s[2], state_ret[0]),
        kmeta, vmeta, pmeta, dec, wts, jnp.zeros((N_EXPERTS, LANES), F32), batch, seq)

    pool2d = state_pool[0].reshape(nb, POOL_BUF * POOL_WIDTH)
    (pool_s, h1_s, xn2_s, idx_s, gate_s, rank_s, cnt) = _sample_mixer(o_s, proj_s, xs2d, pool2d, wts, cnt_p)

    assert n_prompt_tok % ROW_TILE == 0 and nb <= ROW_TILE
    n_phantom = ROW_TILE - nb
    n_tiles = n_prompt_tok // ROW_TILE + 1
    i32 = jnp.int32
    counts = cnt[:, 0].astype(i32)
    padded = ((counts + MOE_BLOCK - 1) // MOE_BLOCK) * MOE_BLOCK
    pad_end = jnp.cumsum(padded)
    pad_start = pad_end - padded
    n_blocks = (n_tok * TOP_K) // MOE_BLOCK + N_EXPERTS
    block_row = jnp.arange(n_blocks, dtype=i32) * MOE_BLOCK
    block_e = jnp.minimum(jnp.sum((pad_end[None, :] <= block_row[:, None]).astype(i32), axis=1), N_EXPERTS - 1)
    nblk = (pad_end[-1:] // MOE_BLOCK).astype(i32)
    e_row = jnp.arange(N_EXPERTS, dtype=i32)
    used = padded > 0
    e_ord = jnp.cumsum(used.astype(i32)) - 1
    later_used = jnp.logical_and(e_row[None, :] > e_row[:, None], used[None, :])
    e_next = jnp.min(jnp.where(later_used, e_row[None, :], N_EXPERTS), axis=1)
    e_next = jnp.where(e_next == N_EXPERTS, -1, e_next)
    of_block = block_e[:, None] == e_row[None, :]
    block_eord = jnp.sum(jnp.where(of_block, e_ord[None, :], 0), axis=1)
    block_enext = jnp.sum(jnp.where(of_block, e_next[None, :], 0), axis=1)
    block_left = jnp.sum(jnp.where(of_block, (pad_start + counts)[None, :], 0), axis=1) - block_row
    block_valid = jnp.clip(block_left, 0, MOE_BLOCK)
    phantom = lambda fill, dt: jnp.full((TOP_K, n_phantom), fill, dt)
    idx = jnp.concatenate([idx_p, idx_s, phantom(-1, i32)], axis=1)
    rank = jnp.concatenate([rank_p, rank_s, phantom(0, i32)], axis=1)
    gates = jnp.concatenate([gate_p, gate_s, phantom(0.0, F32)], axis=1)
    onehot = idx[None] == jnp.arange(N_EXPERTS, dtype=i32)[:, None, None]
    tile_cnt = jnp.sum(onehot.reshape(N_EXPERTS, TOP_K, n_tiles, ROW_TILE).astype(i32), axis=(1, 3)).T
    run_before = jnp.cumsum(tile_cnt, axis=0) - tile_cnt
    tile_off = jnp.cumsum(tile_cnt, axis=1) - tile_cnt
    tile_dst = pad_start[None, :] + run_before
    delta = jnp.repeat((tile_off - run_before).T, ROW_TILE, axis=1)
    pos = rank + jnp.sum(jnp.where(onehot, delta[:, None, :], 0), axis=0)
    pos = jnp.where(idx >= 0, pos, -1)
    tcnt, tdst, ttot = tile_cnt.reshape(-1), tile_dst.reshape(-1), jnp.sum(tile_cnt, axis=1)

    tail = ((0, n_phantom), (0, 0))
    x_sorted = _dispatch(tcnt, tdst, ttot, pad_start + counts, padded - counts, nblk, pos, xn2_p,
                         jnp.pad(xn2_s, tail), n_blocks * MOE_BLOCK)
    y_sorted = _experts(block_e, nblk, block_eord, block_enext, block_valid, x_sorted, wgu, bgu, wd, bd)
    y_p, y_s = _combine(tcnt, tdst, ttot, pos, gates, y_sorted, h1_p, jnp.pad(h1_s, tail), norm_final[None, :])

    y_prompt = y_p.reshape(batch, seq, D_MODEL)
    y_sample = y_s[:nb].reshape(nb, 1, D_MODEL)
    ret_state_prompt = s_fin[None]
    pool_state_prompt = p_fin[:, 1:, :][None]
    ret_state_sample = st_s[None]
    pool_state_sample = pool_s.reshape(nb, POOL_BUF, POOL_WIDTH)[None]
    return (y_prompt, y_sample, ret_state_prompt, pool_state_prompt, ret_state_sample, pool_state_sample)
```

```python
import jax
import jax.numpy as jnp
import numpy as np
from jax import lax
from jax.experimental import pallas as pl
from jax.experimental.pallas import tpu as pltpu

F32 = jnp.float32
BF16 = jnp.bfloat16

D_MODEL = 1024
N_META = 16
RET_HEADS = 4
RET_DK = 128
RET_DV = 256
RET_QK = RET_HEADS * RET_DK
RET_V = RET_HEADS * RET_DV
CHUNK = 128
ROPE_BASE = 10000.0
POOL_WINDOWS = (2, 4, 8, 16)
POOL_GROUPS = 4
POOL_GROUP_DIM = 128
POOL_WIDTH = POOL_GROUPS * POOL_GROUP_DIM
POOL_BUF = max(POOL_WINDOWS) - 1
N_EXPERTS = 32
TOP_K = 4
D_FF = D_MODEL
SWIGLU_LIMIT = 7.0
SWIGLU_ALPHA = 1.702
EPS = 1e-6
IN_WIDTHS = (RET_QK, RET_QK, RET_V, RET_V, POOL_WIDTH, D_MODEL, D_MODEL)
IN_TOTAL = sum(IN_WIDTHS)
IN_OFFS = tuple(int(s) for s in np.cumsum((0,) + IN_WIDTHS))

LANES = 128
ROW_CHUNKS = D_MODEL // (2 * LANES)
MIXER_TILE = 512
MOE_BLOCK = 512
ROW_TILE = 256
SUBLANES = 8
VMEM_LIMIT = 56 * 1024 * 1024
PROMPT_VMEM_LIMIT = 60 * 1024 * 1024

assert N_META + 1 >= max(POOL_WINDOWS)
assert POOL_WINDOWS == (2, 4, 8, 16)


def _dot(a, b):
    return jnp.dot(a, b, preferred_element_type=F32)


def _rmsnorm(x, w):
    return x * lax.rsqrt(jnp.mean(x * x, axis=-1, keepdims=True) + EPS) * w


def _sigmoid(x):
    return 0.5 * jnp.tanh(0.5 * x) + 0.5


def _store_rows_as_tiles(ref, x):
    half = D_MODEL // 2
    hi = pltpu.bitcast(x[:, :half], jnp.uint32) & jnp.uint32(0xFFFF0000)
    lo = pltpu.bitcast(x[:, half:], jnp.uint32) >> 16
    ref[...] = (hi | lo).reshape(x.shape[0], ROW_CHUNKS, LANES)


def _load_rows_from_tiles(ref, rows):
    w = ref[...].reshape(rows, D_MODEL // 2)
    hi = pltpu.bitcast(w & jnp.uint32(0xFFFF0000), F32)
    lo = pltpu.bitcast(w << 16, F32)
    return jnp.concatenate([hi, lo], axis=1)


def _inproj_kernel(x_ref, nw_ref, w_ref, cos_ref, sin_ref,
                   q_ref, k_ref, v_ref, g_ref, p_ref, ga_ref, gb_ref, *, defer_gates=False):
    xn = _rmsnorm(x_ref[...], nw_ref[...]).astype(BF16)
    cos = cos_ref[...]
    sin = sin_ref[...]

    def seg(i):
        return _dot(xn, w_ref[:, IN_OFFS[i]:IN_OFFS[i + 1]])

    def rot(a):
        return a * cos + pltpu.roll(a, RET_DK // 2, 1) * sin

    q = seg(0)
    k = seg(1)
    for h in range(RET_HEADS):
        sl = slice(h * RET_DK, (h + 1) * RET_DK)
        q_ref[:, sl] = rot(q[:, sl]).astype(q_ref.dtype)
        k_ref[:, sl] = (rot(k[:, sl]) * (RET_DK ** -0.5)).astype(k_ref.dtype)
    v_ref[...] = seg(2).astype(v_ref.dtype)

    def gates():
        g = seg(3)
        g_ref[...] = (g * _sigmoid(g)).astype(g_ref.dtype)
        p_ref[...] = seg(4)
        ga_ref[...] = _sigmoid(seg(5)).astype(ga_ref.dtype)
        gb_ref[...] = _sigmoid(seg(6)).astype(gb_ref.dtype)

    if defer_gates:
        return gates
    gates()


def _sample_state_step(step, sdec_ref, qt_ref, kt_ref, q_ref, k_ref, v_ref, st_ref, stout_ref, o_ref):
    grp = st_ref.shape[0]
    row0 = step * grp
    if grp % SUBLANES == 0:
        rows = pl.ds(pl.multiple_of(row0, SUBLANES), grp)
        q8, k8, v8 = q_ref[rows, :], k_ref[rows, :], v_ref[rows, :]
    else:
        assert 2 * grp == SUBLANES
        rows = pl.ds(pl.multiple_of((step // 2) * SUBLANES, SUBLANES), SUBLANES)
        first = step % 2 == 0
        q8, k8, v8 = (jnp.where(first, a[:grp], a[grp:]) for a in (q_ref[rows, :], k_ref[rows, :], v_ref[rows, :]))
    for h in range(RET_HEADS):
        ksl = slice(h * RET_DK, (h + 1) * RET_DK)
        vsl = slice(h * RET_DV, (h + 1) * RET_DV)
        score = jnp.sum(q8[:, ksl] * k8[:, ksl], axis=1, keepdims=True) * sdec_ref[h, 0]
        intra = score * v8[:, vsl]
        for bb in range(grp):
            s_old = st_ref[bb, h]
            qcol = qt_ref[0, ksl, bb:bb + 1] * sdec_ref[h, 1]
            kcol = kt_ref[0, ksl, bb:bb + 1] * sdec_ref[h, 2]
            cross = jnp.sum(s_old * qcol, axis=0, keepdims=True)
            o_ref[pl.ds(row0 + bb, 1), vsl] = intra[bb:bb + 1, :] + cross
            stout_ref[bb, h] = s_old * sdec_ref[h, 3] + kcol * v8[bb:bb + 1, vsl]


def _inproj_small(x2d, nw, w_in_bf, cosf, sinf):
    rows = x2d.shape[0]
    const = lambda i: (0, 0)
    return pl.pallas_call(
        _inproj_kernel,
        grid=(1,),
        in_specs=[
            pl.BlockSpec((rows, D_MODEL), const),
            pl.BlockSpec((1, D_MODEL), const),
            pl.BlockSpec((D_MODEL, IN_TOTAL), const, pipeline_mode=pl.Buffered(1)),
            pl.BlockSpec((rows, RET_DK), const),
            pl.BlockSpec((rows, RET_DK), const),
        ],
        out_specs=[pl.BlockSpec((rows, w), const) for w in IN_WIDTHS],
        out_shape=[jax.ShapeDtypeStruct((rows, w), F32) for w in IN_WIDTHS],
        compiler_params=pltpu.CompilerParams(dimension_semantics=("arbitrary",), vmem_limit_bytes=VMEM_LIMIT),
        name="inproj",
    )(x2d, nw, w_in_bf, cosf, sinf)


def _group_norm(o, gn_row):
    mu = jnp.mean(o, axis=-1, keepdims=True)
    var = jnp.mean(jnp.square(o - mu), axis=-1, keepdims=True)
    return (o - mu) * lax.rsqrt(var + EPS) * gn_row


def _pool_branch(groups, poolw_ref, pscale_ref, wpool_ref):
    pm = [_dot(g.astype(BF16), poolw_ref[i]) for i, g in enumerate(groups)]
    pm = jnp.concatenate(pm, axis=1) * pscale_ref[...]
    return _dot(pm.astype(BF16), wpool_ref[...])


def _merge_tail(o_norm, silu_g, sig_a, sig_b, yb, x, wret_ref, wout_ref):
    ya = _dot((silu_g.astype(F32) * o_norm).astype(BF16), wret_ref[...])
    merged = sig_a.astype(F32) * ya + sig_b.astype(F32) * yb
    return x + _dot(merged.astype(BF16), wout_ref[...])


def _route(h1, nffn_ref, wrt_ref, br_ref, run_scr,
           xn2_ref, idx_ref, gate_ref, rank_ref):
    tm = h1.shape[0]
    xn2 = _rmsnorm(h1, nffn_ref[...]).astype(BF16)
    xn2_ref[...] = xn2
    logits = lax.dot_general(wrt_ref[...], xn2, (((1,), (1,)), ((), ())),
                             preferred_element_type=F32) + br_ref[...]
    e_iota = lax.broadcasted_iota(jnp.int32, (N_EXPERTS, tm), 0)
    work = logits
    vals, sels = [], []
    chosen = jnp.zeros((N_EXPERTS, tm), F32)
    for _ in range(TOP_K):
        m = jnp.max(work, axis=0, keepdims=True)
        sel = jnp.min(jnp.where(work == m, e_iota, N_EXPERTS), axis=0, keepdims=True)
        hit = e_iota == sel
        vals.append(m)
        sels.append(sel)
        chosen = jnp.where(hit, 1.0, chosen)
        work = jnp.where(hit, -jnp.inf, work)
    exps = [jnp.exp(v - vals[0]) for v in vals]
    denom = exps[0] + exps[1] + exps[2] + exps[3]
    gates = [e / denom for e in exps]
    r_i = lax.broadcasted_iota(jnp.int32, (tm, tm), 0)
    c_i = lax.broadcasted_iota(jnp.int32, (tm, tm), 1)
    before = jnp.where(r_i < c_i, 1.0, 0.0).astype(BF16)
    base = run_scr[...] + _dot(chosen.astype(BF16), before)
    for kk in range(TOP_K):
        rk = jnp.sum(jnp.where(e_iota == sels[kk], base, 0.0), axis=0, keepdims=True)
        rank_ref[kk:kk + 1, :] = rk.astype(jnp.int32)
        idx_ref[kk:kk + 1, :] = sels[kk]
        gate_ref[kk:kk + 1, :] = gates[kk]
    run_scr[...] = run_scr[...] + jnp.sum(chosen, axis=1, keepdims=True)


def _mixer_kernel(q_ref, k_ref, v_ref, g_ref, p_ref, ga_ref, gb_ref, x_ref,
                  kmeta_ref, vmeta_ref, pmeta_ref, mask_ref, qdec_ref, kdec_ref, cdec_ref, gn_ref,
                  poolw_ref, pscale_ref, wret_ref, wpool_ref, wout_ref, nffn_ref, wrt_ref, br_ref,
                  cnt0_ref,
                  h1_ref, xn2_ref, idx_ref, gate_ref, rank_ref, cnt_ref, sfin_ref, pfin_ref,
                  s_scr, ext_scr, o_scr, run_scr, *, after_first_pass=None):
    b = pl.program_id(0)
    j = pl.program_id(1)
    nj = pl.num_programs(1)
    tm = q_ref.shape[0]

    def state_update(s_old, kc, vc, h):
        kd = (kc.astype(F32) * kdec_ref[h]).astype(BF16)
        upd = lax.dot_general(kd, vc, (((0,), (0,)), ((), ())), preferred_element_type=F32)
        return s_old * cdec_ref[h] + upd

    @pl.when(jnp.logical_and(b == 0, j == 0))
    def _():
        run_scr[...] = cnt0_ref[:, 0:1]

    @pl.when(j == 0)
    def _():
        for h in range(RET_HEADS):
            kc = kmeta_ref[:, h * RET_DK:(h + 1) * RET_DK]
            vc = vmeta_ref[:, h * RET_DV:(h + 1) * RET_DV]
            s_scr[h] = state_update(jnp.zeros((RET_DK, RET_DV), F32), kc, vc, h)
        ext_scr[0:N_META, :] = pmeta_ref[...]

    n_chunks = tm // CHUNK
    units = [(c, h) for c in range(n_chunks) for h in range(RET_HEADS)]

    def operands(c, h):
        rows = slice(c * CHUNK, (c + 1) * CHUNK)
        return (q_ref[rows, h * RET_DK:(h + 1) * RET_DK], k_ref[rows, h * RET_DK:(h + 1) * RET_DK],
                v_ref[rows, h * RET_DV:(h + 1) * RET_DV])

    lhs, upd = {}, {}
    for c, h in units:
        qc, kc, vc = operands(c, h)
        scores = lax.dot_general(qc, kc, (((1,), (1,)), ((), ())), preferred_element_type=F32) * mask_ref[h]
        qd = (qc.astype(F32) * qdec_ref[h]).astype(BF16)
        lhs[c, h] = jnp.concatenate([scores.astype(BF16), qd], axis=1)
        kd = (kc.astype(F32) * kdec_ref[h]).astype(BF16)
        upd[c, h] = lax.dot_general(kd, vc, (((0,), (0,)), ((), ())), preferred_element_type=F32)
    if after_first_pass is not None:
        after_first_pass()
    state = {}
    for h in range(RET_HEADS):
        s = s_scr[h]
        for c in range(n_chunks):
            state[c, h] = s
            s = s * cdec_ref[h] + upd[c, h]
        s_scr[h] = s
    for c, h in units:
        rhs = jnp.concatenate([operands(c, h)[2], state[c, h].astype(BF16)], axis=0)
        o = _dot(lhs[c, h], rhs)
        o_scr[c * CHUNK:(c + 1) * CHUNK, h * RET_DV:(h + 1) * RET_DV] = _group_norm(
            o, gn_ref[:, h * RET_DV:(h + 1) * RET_DV])

    p = p_ref[...]
    ext_scr[N_META:N_META + tm, :] = p
    a = ext_scr[...]
    g1 = POOL_GROUP_DIM
    s2 = a + pltpu.roll(a, 1, 0)
    s4 = s2[:, g1:] + pltpu.roll(s2[:, g1:], 2, 0)
    s8 = s4[:, g1:] + pltpu.roll(s4[:, g1:], 4, 0)
    s16 = s8[:, g1:] + pltpu.roll(s8[:, g1:], 8, 0)
    sums = (s2[N_META:, :g1], s4[N_META:, :g1], s8[N_META:, :g1], s16[N_META:, :])
    groups = [sums[i] * (1.0 / POOL_WINDOWS[i]) - p[:, i * g1:(i + 1) * g1] for i in range(POOL_GROUPS)]
    ext_scr[0:N_META, :] = ext_scr[tm:tm + N_META, :]

    yb = _pool_branch(groups, poolw_ref, pscale_ref, wpool_ref)
    h1 = _merge_tail(o_scr[...], g_ref[...], ga_ref[...], gb_ref[...], yb, x_ref[...], wret_ref, wout_ref)
    h1_ref[...] = h1
    _route(h1, nffn_ref, wrt_ref, br_ref, run_scr, xn2_ref, idx_ref, gate_ref, rank_ref)
    cnt_ref[...] = jnp.broadcast_to(run_scr[...], cnt_ref.shape)

    @pl.when(j == nj - 1)
    def _():
        for h in range(RET_HEADS):
            sfin_ref[0, h] = s_scr[h]
        pfin_ref[0] = ext_scr[0:N_META, :]


N_SAMPLE_IN = 7
N_MIXER_IN = 17
N_MIXER_OUT = 8
N_PROJ = len(IN_WIDTHS)


def _prompt_kernel(*refs):
    it = iter(refs)
    take = lambda n: [next(it) for _ in range(n)]
    x_ref, nw_ref, w_ref, cos_ref, sin_ref = take(5)
    sample_in = take(N_SAMPLE_IN)
    mixer_in = take(N_MIXER_IN)
    mixer_out = take(N_MIXER_OUT)
    stout_ref, os_ref = take(2)
    proj_scr = take(N_PROJ)
    mixer_scr = take(4)
    gate_columns = _inproj_kernel(x_ref, nw_ref, w_ref, cos_ref, sin_ref, *proj_scr, defer_gates=True)
    step = pl.program_id(0) * pl.num_programs(1) + pl.program_id(1)
    _sample_state_step(step, *sample_in, stout_ref, os_ref)
    _mixer_kernel(*proj_scr, x_ref, *mixer_in, *mixer_out, *mixer_scr, after_first_pass=gate_columns)


def _prompt_layer(x2d, nw, w_in_bf, cosf, sinf, sample, kmeta, vmeta, pmeta, dec, wts, cnt0, batch, seq):
    tm = MIXER_TILE
    nj = seq // tm
    rows = batch * seq
    row_map = lambda b, j: (b * nj + j, 0)
    lane_map = lambda b, j: (0, b * nj + j)
    c2 = lambda b, j: (0, 0)
    c3 = lambda b, j: (0, 0, 0)

    def whole(a):
        mode = dict(pipeline_mode=pl.Buffered(1)) if a.size * a.dtype.itemsize >= 512 * 1024 else {}
        return pl.BlockSpec(a.shape, c2 if a.ndim == 2 else c3, **mode)

    sdec, qt, kt, qs, ks, vs, state = sample
    nb = state.shape[0]
    grp = nb // (batch * nj)
    assert grp * batch * nj == nb and qt.shape == (nb // grp, RET_QK, grp)
    step_map = lambda b, j: (b * nj + j, 0, 0)
    st_spec = pl.BlockSpec((grp, RET_HEADS, RET_DK, RET_DV), lambda b, j: (b * nj + j, 0, 0, 0))
    mask, qdec, kdec, cdec = dec
    mixer_in = [kmeta, vmeta, pmeta, mask, qdec, kdec, cdec, *wts, cnt0]
    assert len(mixer_in) == N_MIXER_IN
    in_arrays = [x2d, nw, w_in_bf, cosf, sinf, sdec, qt, kt, qs, ks, vs, state] + mixer_in
    in_specs = [
        pl.BlockSpec((tm, D_MODEL), row_map),
        pl.BlockSpec((1, D_MODEL), c2),
        pl.BlockSpec((D_MODEL, IN_TOTAL), c2, pipeline_mode=pl.Buffered(1)),
        pl.BlockSpec((tm, RET_DK), lambda b, j: (j, 0)),
        pl.BlockSpec((tm, RET_DK), lambda b, j: (j, 0)),
        pl.BlockSpec(memory_space=pltpu.SMEM),
        pl.BlockSpec((1, RET_QK, grp), step_map), pl.BlockSpec((1, RET_QK, grp), step_map),
        pl.BlockSpec(qs.shape, c2), pl.BlockSpec(ks.shape, c2), pl.BlockSpec(vs.shape, c2),
        st_spec,
    ] + [whole(a) for a in mixer_in]
    out_shape = [
        jax.ShapeDtypeStruct((rows, D_MODEL), F32),
        jax.ShapeDtypeStruct((rows, D_MODEL), BF16),
        jax.ShapeDtypeStruct((TOP_K, rows), jnp.int32),
        jax.ShapeDtypeStruct((TOP_K, rows), F32),
        jax.ShapeDtypeStruct((TOP_K, rows), jnp.int32),
        jax.ShapeDtypeStruct((N_EXPERTS, LANES), F32),
        jax.ShapeDtypeStruct((batch, RET_HEADS, RET_DK, RET_DV), F32),
        jax.ShapeDtypeStruct((batch, N_META, POOL_WIDTH), F32),
        jax.ShapeDtypeStruct(state.shape, F32),
        jax.ShapeDtypeStruct((nb, RET_V), F32),
    ]
    out_specs = [
        pl.BlockSpec((tm, D_MODEL), row_map),
        pl.BlockSpec((tm, D_MODEL), row_map),
        pl.BlockSpec((TOP_K, tm), lane_map),
        pl.BlockSpec((TOP_K, tm), lane_map),
        pl.BlockSpec((TOP_K, tm), lane_map),
        pl.BlockSpec((N_EXPERTS, LANES), c2),
        pl.BlockSpec((1, RET_HEADS, RET_DK, RET_DV), lambda b, j: (b, 0, 0, 0)),
        pl.BlockSpec((1, N_META, POOL_WIDTH), lambda b, j: (b, 0, 0)),
        st_spec,
        pl.BlockSpec((nb, RET_V), c2),
    ]
    proj_dts = (BF16, BF16, BF16, BF16, F32, BF16, BF16)
    return pl.pallas_call(
        _prompt_kernel,
        grid=(batch, nj),
        in_specs=in_specs,
        out_specs=out_specs,
        out_shape=out_shape,
        scratch_shapes=[pltpu.VMEM((tm, w), dt) for w, dt in zip(IN_WIDTHS, proj_dts)] + [
            pltpu.VMEM((RET_HEADS, RET_DK, RET_DV), F32),
            pltpu.VMEM((N_META + tm, POOL_WIDTH), F32),
            pltpu.VMEM((tm, RET_V), F32),
            pltpu.VMEM((N_EXPERTS, 1), F32),
        ],
        compiler_params=pltpu.CompilerParams(
            dimension_semantics=("arbitrary", "arbitrary"), vmem_limit_bytes=PROMPT_VMEM_LIMIT),
        name="prompt_layer",
    )(*in_arrays)


def _sample_kernel(o_ref, g_ref, p_ref, ga_ref, gb_ref, x_ref, pool_ref, gn_ref,
                   poolw_ref, pscale_ref, wret_ref, wpool_ref, wout_ref, nffn_ref, wrt_ref, br_ref, cnt0_ref,
                   poolout_ref, h1_ref, xn2_ref, idx_ref, gate_ref, rank_ref, cnt_ref,
                   run_scr):
    run_scr[...] = cnt0_ref[:, 0:1]
    o = o_ref[...]
    o_norm = jnp.concatenate(
        [_group_norm(o[:, h * RET_DV:(h + 1) * RET_DV], gn_ref[:, h * RET_DV:(h + 1) * RET_DV])
         for h in range(RET_HEADS)], axis=1)
    p = p_ref[...]
    w = POOL_WIDTH
    g1 = POOL_GROUP_DIM

    def prev(r, lo):
        return pool_ref[:, r * w + lo:(r + 1) * w]

    s2 = p + prev(14, 0)
    s4 = s2[:, g1:] + prev(13, g1) + prev(12, g1)
    s8 = s4[:, g1:] + prev(11, 2 * g1) + prev(10, 2 * g1) + prev(9, 2 * g1) + prev(8, 2 * g1)
    s16 = s8[:, g1:]
    for r in range(7, -1, -1):
        s16 = s16 + prev(r, 3 * g1)
    sums = (s2[:, :g1], s4[:, :g1], s8[:, :g1], s16)
    groups = [sums[t] * (1.0 / POOL_WINDOWS[t]) - p[:, t * g1:(t + 1) * g1] for t in range(POOL_GROUPS)]
    poolout_ref[:, 0:(POOL_BUF - 1) * w] = pool_ref[:, w:POOL_BUF * w]
    poolout_ref[:, (POOL_BUF - 1) * w:] = p
    yb = _pool_branch(groups, poolw_ref, pscale_ref, wpool_ref)
    h1 = _merge_tail(o_norm, g_ref[...], ga_ref[...], gb_ref[...], yb, x_ref[...], wret_ref, wout_ref)
    h1_ref[...] = h1
    _route(h1, nffn_ref, wrt_ref, br_ref, run_scr, xn2_ref, idx_ref, gate_ref, rank_ref)
    cnt_ref[...] = jnp.broadcast_to(run_scr[...], cnt_ref.shape)


def _sample_mixer(o_s, proj, x2d, pool2d, wts, cnt0):
    _, _, _, g, p, ga, gb = proj
    nb = x2d.shape[0]
    c2 = lambda i: (0, 0)

    def whole(a):
        return pl.BlockSpec(a.shape, c2)

    gn, poolw, pscale, wret, wpool, wout, nffn, wrt, br = wts
    in_arrays = [o_s, g, p, ga, gb, x2d, pool2d, gn, poolw, pscale, wret, wpool, wout, nffn, wrt, br, cnt0]
    in_specs = [whole(a) for a in in_arrays[:8]]
    in_specs += [pl.BlockSpec(poolw.shape, lambda i: (0, 0, 0))]
    in_specs += [whole(a) for a in in_arrays[9:]]
    out_shape = [
        jax.ShapeDtypeStruct(pool2d.shape, F32),
        jax.ShapeDtypeStruct((nb, D_MODEL), F32),
        jax.ShapeDtypeStruct((nb, D_MODEL), BF16),
        jax.ShapeDtypeStruct((TOP_K, nb), jnp.int32),
        jax.ShapeDtypeStruct((TOP_K, nb), F32),
        jax.ShapeDtypeStruct((TOP_K, nb), jnp.int32),
        jax.ShapeDtypeStruct((N_EXPERTS, LANES), F32),
    ]
    return pl.pallas_call(
        _sample_kernel,
        grid=(1,),
        in_specs=in_specs,
        out_specs=[pl.BlockSpec(s.shape, c2) for s in out_shape],
        out_shape=out_shape,
        scratch_shapes=[pltpu.VMEM((N_EXPERTS, 1), F32)],
        compiler_params=pltpu.CompilerParams(
            dimension_semantics=("arbitrary",), vmem_limit_bytes=VMEM_LIMIT),
        name="sample_mixer",
    )(*in_arrays)


TILE_ROWS = ROW_TILE * TOP_K


def _run_copy(src_ref, src_row, dst_ref, dst_row, n_rows, sem):
    return pltpu.make_async_copy(src_ref.at[pl.ds(src_row, n_rows)], dst_ref.at[pl.ds(dst_row, n_rows)], sem)


def _for_each_run(tcnt_ref, tile, fn, enabled=True):
    off = 0
    for e in range(N_EXPERTS):
        n = tcnt_ref[tile * N_EXPERTS + e]

        @pl.when(jnp.logical_and(n > 0, enabled))
        def _(e=e, off=off, n=n):
            fn(e, off, n)

        off = off + n


def _dispatch_kernel(tcnt_ref, tdst_ref, ttot_ref, zrow_ref, zcnt_ref, nblk_ref, pos_ref, xp_ref, xs_ref, out_ref,
                     sorted_scr, zero_scr, sem, zsem):
    i = pl.program_id(0)
    n_prompt = pl.num_programs(0) - 1
    n_blocks = out_ref.shape[0] // MOE_BLOCK
    slot = i % 2

    def wait_tile(tile, s):
        _run_copy(out_ref, 0, sorted_scr.at[s], 0, ttot_ref[tile], sem.at[s]).wait()

    def for_each_pad(fn):
        def body(e, carry):
            n = zcnt_ref[e]

            @pl.when(n > 0)
            def _():
                fn(_run_copy(zero_scr, 0, out_ref, zrow_ref[e], n, zsem))

            dead = nblk_ref[0] + e

            @pl.when(dead < n_blocks)
            def _():
                fn(_run_copy(zero_scr, 0, out_ref, dead * MOE_BLOCK, MOE_BLOCK, zsem))

            return carry

        lax.fori_loop(0, N_EXPERTS, body, 0)

    @pl.when(i == 0)
    def _():
        zero_scr[...] = jnp.zeros_like(zero_scr)
        for_each_pad(lambda cp: cp.start())

    def sort_tile(x_ref):
        r_iota = lax.broadcasted_iota(jnp.int32, (TILE_ROWS, ROW_TILE), 0)
        hit = r_iota == pos_ref[0:1, :]
        for kk in range(1, TOP_K):
            hit = jnp.logical_or(hit, r_iota == pos_ref[kk:kk + 1, :])
        perm = jnp.where(hit, 1.0, 0.0).astype(BF16)
        xs = _dot(perm, x_ref[...])
        _store_rows_as_tiles(sorted_scr.at[slot], xs)
        _for_each_run(tcnt_ref, i, lambda e, off, n: _run_copy(
            sorted_scr.at[slot], off, out_ref, tdst_ref[i * N_EXPERTS + e], n, sem.at[slot]).start())

    @pl.when(i >= 2)
    def _():
        wait_tile(i - 2, slot)

    @pl.when(i < n_prompt)
    def _():
        sort_tile(xp_ref)

    @pl.when(i == n_prompt)
    def _():
        sort_tile(xs_ref)

    @pl.when(i == n_prompt)
    def _():
        wait_tile(i - 1, 1 - slot)
        wait_tile(i, slot)
        for_each_pad(lambda cp: cp.wait())


def _dispatch(tcnt, tdst, ttot, zrow, zcnt, nblk, pos, xn2_p, xn2_s, n_sorted):
    n_prompt = xn2_p.shape[0] // ROW_TILE
    grid_spec = pltpu.PrefetchScalarGridSpec(
        num_scalar_prefetch=6,
        grid=(n_prompt + 1,),
        in_specs=[
            pl.BlockSpec((TOP_K, ROW_TILE), lambda i, *_: (0, i)),
            pl.BlockSpec((ROW_TILE, D_MODEL), lambda i, *_: (jnp.minimum(i, n_prompt - 1), 0)),
            pl.BlockSpec((ROW_TILE, D_MODEL), lambda i, *_: (0, 0)),
        ],
        out_specs=pl.BlockSpec(memory_space=pl.ANY),
        scratch_shapes=[
            pltpu.VMEM((2, TILE_ROWS, ROW_CHUNKS, LANES), jnp.uint32),
            pltpu.VMEM((MOE_BLOCK, ROW_CHUNKS, LANES), jnp.uint32),
            pltpu.SemaphoreType.DMA((2,)), pltpu.SemaphoreType.DMA(()),
        ],
    )
    return pl.pallas_call(
        _dispatch_kernel,
        grid_spec=grid_spec,
        out_shape=jax.ShapeDtypeStruct((n_sorted, ROW_CHUNKS, LANES), jnp.uint32),
        compiler_params=pltpu.CompilerParams(dimension_semantics=("arbitrary",)),
        name="moe_dispatch",
    )(tcnt, tdst, ttot, zrow, zcnt, nblk, pos, xn2_p, xn2_s)


def _expert_kernel(be_ref, nblk_ref, eord_ref, enext_ref, bvalid_ref, x_ref, wgu_ref, bgu_ref, wd_ref, bd_ref, y_ref,
                   wgu_f32, wd_f32, wgu_bf, wd_bf, wsem):
    i = pl.program_id(0)
    live = i < nblk_ref[0]
    new_expert = jnp.logical_or(i == 0, be_ref[i] != be_ref[jnp.maximum(i - 1, 0)])

    def weight_copies(e, s):
        return (pltpu.make_async_copy(wgu_ref.at[e], wgu_f32.at[s], wsem.at[0, s]),
                pltpu.make_async_copy(wd_ref.at[e], wd_f32.at[s], wsem.at[1, s]))

    @pl.when(jnp.logical_and(live, new_expert))
    def _():
        slot = eord_ref[i] % 2

        @pl.when(i == 0)
        def _():
            for cp in weight_copies(be_ref[0], 0):
                cp.start()

        for cp in weight_copies(be_ref[i], slot):
            cp.wait()

        @pl.when(enext_ref[i] >= 0)
        def _():
            for cp in weight_copies(enext_ref[i], 1 - slot):
                cp.start()

    @pl.when(jnp.logical_not(live))
    def _():
        y_ref[...] = jnp.zeros_like(y_ref)

    def ffn(rows, first):
        if first:
            slot = eord_ref[i] % 2
            wgu = wgu_f32[slot].astype(BF16)
            wgu_bf[...] = wgu
        else:
            wgu = wgu_bf[...]
        x = _load_rows_from_tiles(x_ref.at[pl.ds(0, rows)], rows)
        h = _dot(x.astype(BF16), wgu) + bgu_ref[0]
        gate = jnp.minimum(h[:, :D_FF], SWIGLU_LIMIT)
        up = jnp.clip(h[:, D_FF:], -SWIGLU_LIMIT, SWIGLU_LIMIT)
        glu = gate * _sigmoid(gate * SWIGLU_ALPHA)
        if first:
            wd_b = wd_f32[slot].astype(BF16)
            wd_bf[...] = wd_b
        else:
            wd_b = wd_bf[...]
        y = _dot(((up + 1.0) * glu).astype(BF16), wd_b) + bd_ref[0]
        _store_rows_as_tiles(y_ref.at[pl.ds(0, rows)], y.astype(BF16).astype(F32))

    half = MOE_BLOCK // 2
    short = bvalid_ref[i] <= half
    for first in (False, True):
        same = new_expert if first else jnp.logical_not(new_expert)

        @pl.when(jnp.logical_and(jnp.logical_and(live, same), jnp.logical_not(short)))
        def _(first=first):
            ffn(MOE_BLOCK, first)

        @pl.when(jnp.logical_and(jnp.logical_and(live, same), short))
        def _(first=first):
            ffn(half, first)
            y_ref[pl.ds(half, half)] = jnp.zeros((half,) + y_ref.shape[1:], y_ref.dtype)


def _experts(block_e, nblk, block_eord, block_enext, block_valid, x_sorted, wgu, bgu, wd, bd):
    n_blocks = x_sorted.shape[0] // MOE_BLOCK
    wmap = lambda i, be, *_: (be[i], 0, 0)
    rmap = lambda i, *_: (i, 0, 0)
    xmap = lambda i, be, nb, *_: (jnp.minimum(i, nb[0] - 1), 0, 0)
    hbm = pl.BlockSpec(memory_space=pl.ANY)
    grid_spec = pltpu.PrefetchScalarGridSpec(
        num_scalar_prefetch=5,
        grid=(n_blocks,),
        in_specs=[
            pl.BlockSpec((MOE_BLOCK, ROW_CHUNKS, LANES), xmap),
            hbm,
            pl.BlockSpec((1, 1, 2 * D_FF), wmap),
            hbm,
            pl.BlockSpec((1, 1, D_MODEL), wmap),
        ],
        out_specs=pl.BlockSpec((MOE_BLOCK, ROW_CHUNKS, LANES), rmap),
        scratch_shapes=[
            pltpu.VMEM((2, D_MODEL, 2 * D_FF), F32), pltpu.VMEM((2, D_FF, D_MODEL), F32),
            pltpu.VMEM((D_MODEL, 2 * D_FF), BF16), pltpu.VMEM((D_FF, D_MODEL), BF16),
            pltpu.SemaphoreType.DMA((2, 2)),
        ],
    )
    return pl.pallas_call(
        _expert_kernel,
        grid_spec=grid_spec,
        out_shape=jax.ShapeDtypeStruct(x_sorted.shape, jnp.uint32),
        compiler_params=pltpu.CompilerParams(
            dimension_semantics=("arbitrary",), vmem_limit_bytes=VMEM_LIMIT),
        name="moe_experts",
    )(block_e, nblk, block_eord, block_enext, block_valid, x_sorted, wgu, bgu, wd, bd)


def _combine_kernel(tcnt_ref, tdst_ref, ttot_ref, pos_ref, gate_ref, ys_ref, hp_ref, hs_ref, nf_ref, yp_ref, ysmp_ref,
                    runs_scr, sem):
    i = pl.program_id(0)
    n_tiles = pl.num_programs(0)
    n_prompt = n_tiles - 1
    slot = i % 2

    def start_runs(tile, s, enabled=True):
        _for_each_run(tcnt_ref, tile, lambda e, off, n: _run_copy(
            ys_ref, tdst_ref[tile * N_EXPERTS + e], runs_scr.at[s], off, n, sem.at[s]).start(), enabled)

    @pl.when(i == 0)
    def _():
        start_runs(0, 0)

    start_runs(jnp.minimum(i + 1, n_tiles - 1), 1 - slot, enabled=i + 1 < n_tiles)
    _run_copy(ys_ref, 0, runs_scr.at[slot], 0, ttot_ref[i], sem.at[slot]).wait()
    ys = _load_rows_from_tiles(runs_scr.at[slot], TILE_ROWS).astype(BF16)
    r_iota = lax.broadcasted_iota(jnp.int32, (TILE_ROWS, ROW_TILE), 0)
    gmat = jnp.zeros((TILE_ROWS, ROW_TILE), F32)
    for kk in range(TOP_K):
        gmat = jnp.where(r_iota == pos_ref[kk:kk + 1, :], gate_ref[kk:kk + 1, :], gmat)
    moe = lax.dot_general(gmat.astype(BF16), ys, (((0,), (0,)), ((), ())), preferred_element_type=F32)

    @pl.when(i < n_prompt)
    def _():
        yp_ref[...] = _rmsnorm(hp_ref[...] + moe, nf_ref[...])

    @pl.when(i == n_prompt)
    def _():
        ysmp_ref[...] = _rmsnorm(hs_ref[...] + moe, nf_ref[...])


def _combine(tcnt, tdst, ttot, pos, gates, y_sorted, h1_p, h1_s, nf):
    n_prompt = h1_p.shape[0] // ROW_TILE
    pmap = lambda i, *_: (jnp.minimum(i, n_prompt - 1), 0)
    smap = lambda i, *_: (0, 0)
    lmap = lambda i, *_: (0, i)
    grid_spec = pltpu.PrefetchScalarGridSpec(
        num_scalar_prefetch=3,
        grid=(n_prompt + 1,),
        in_specs=[
            pl.BlockSpec((TOP_K, ROW_TILE), lmap),
            pl.BlockSpec((TOP_K, ROW_TILE), lmap),
            pl.BlockSpec(memory_space=pl.ANY),
            pl.BlockSpec((ROW_TILE, D_MODEL), pmap),
            pl.BlockSpec((ROW_TILE, D_MODEL), smap),
            pl.BlockSpec((1, D_MODEL), smap),
        ],
        out_specs=[pl.BlockSpec((ROW_TILE, D_MODEL), pmap), pl.BlockSpec((ROW_TILE, D_MODEL), smap)],
        scratch_shapes=[pltpu.VMEM((2, TILE_ROWS, ROW_CHUNKS, LANES), jnp.uint32), pltpu.SemaphoreType.DMA((2,))],
    )
    return pl.pallas_call(
        _combine_kernel,
        grid_spec=grid_spec,
        out_shape=[jax.ShapeDtypeStruct(h1_p.shape, F32), jax.ShapeDtypeStruct(h1_s.shape, F32)],
        compiler_params=pltpu.CompilerParams(dimension_semantics=("arbitrary",)),
        name="moe_combine",
    )(tcnt, tdst, ttot, pos, gates, y_sorted, h1_p, h1_s, nf)


def _rotary_tables(pos):
    f = np.float32
    inv = np.power(f(ROPE_BASE), -np.arange(0, RET_DK, 2, dtype=f) / f(RET_DK)).astype(f)
    ang = (np.asarray(pos, f)[:, None] * inv[None, :]).astype(f)
    cos, sin = np.cos(ang).astype(f), np.sin(ang).astype(f)
    return np.concatenate([cos, cos], axis=1), np.concatenate([-sin, sin], axis=1)


def _decay_tables(chunk):
    f = np.float32
    log_g = np.log1p(-np.exp2(f(-5.0) - np.arange(RET_HEADS, dtype=f))).astype(f)
    i = np.arange(chunk, dtype=f)
    diff = i[:, None] - i[None, :]
    mask = np.where(diff[None] >= 0, np.exp(np.maximum(diff, f(0.0))[None] * log_g[:, None, None]), f(0.0)).astype(f)
    q_dec = np.exp((i + f(1.0))[None, :] * log_g[:, None]).astype(f)
    k_dec = np.exp((f(chunk) - f(1.0) - i)[None, :] * log_g[:, None]).astype(f)
    c_dec = np.exp(f(chunk) * log_g).astype(f)
    return mask, q_dec, k_dec, c_dec


def kernel(x_prompt, x_sample, state_ret, state_pool, meta_tokens, norm_mix, w_in, ret_gn, pool_w, pool_scale,
           w_ret_branch, w_pool_branch, w_out, norm_ffn, w_router, b_router, w_gate_up, b_gate_up, w_down, b_down,
           norm_final):
    batch, seq, _ = x_prompt.shape
    nb = x_sample.shape[0]
    past_len = 16384
    n_prompt_tok = batch * seq
    n_tok = n_prompt_tok + nb

    w_in_bf = w_in[0].astype(BF16)
    wts = (ret_gn[0][None, :], pool_w[0].astype(BF16), pool_scale[0][None, :],
           w_ret_branch[0].astype(BF16), w_pool_branch[0].astype(BF16), w_out[0].astype(BF16),
           norm_ffn[0][None, :], w_router[0].T.astype(BF16), b_router[0][:, None])
    wgu = w_gate_up[0]
    wd = w_down[0]
    bgu = b_gate_up[0][:, None, :]
    bd = b_down[0][:, None, :]
    nmix = norm_mix[0][None, :]

    cos_p, sin_p = _rotary_tables(N_META + np.arange(seq))
    cos_s, sin_s = _rotary_tables(np.concatenate([np.arange(N_META), np.full((nb,), past_len)]))
    mask, q_dec, k_dec, c_dec = _decay_tables(CHUNK)
    dec = (mask,
           np.ascontiguousarray(np.broadcast_to(q_dec[:, :, None], (RET_HEADS, CHUNK, RET_DK))),
           np.ascontiguousarray(np.broadcast_to(k_dec[:, :, None], (RET_HEADS, CHUNK, RET_DK))),
           np.ascontiguousarray(np.broadcast_to(c_dec[:, None, None], (RET_HEADS, 1, RET_DV))))
    m1, q1, k1, c1 = _decay_tables(1)
    sdec = np.stack([m1[:, 0, 0], q1[:, 0], k1[:, 0], c1], axis=1)

    x2d = x_prompt.reshape(n_prompt_tok, D_MODEL)
    xs2d = x_sample.reshape(nb, D_MODEL)
    x_small = jnp.concatenate([meta_tokens, xs2d], axis=0)
    proj_small = _inproj_small(x_small, nmix, w_in_bf, cos_s, sin_s)
    proj_s = tuple(a[N_META:] for a in proj_small)

    lead = CHUNK - N_META
    kmeta = jnp.pad(proj_small[1][:N_META], ((lead, 0), (0, 0))).astype(BF16)
    vmeta = jnp.pad(proj_small[2][:N_META], ((lead, 0), (0, 0))).astype(BF16)
    pmeta = proj_small[4][:N_META]

    grp = nb // (n_prompt_tok // MIXER_TILE)

    def cols(a):
        return a.T.reshape(RET_QK, nb // grp, grp).transpose(1, 0, 2)

    (h1_p, xn2_p, idx_p, gate_p, rank_p, cnt_p, s_fin, p_fin, st_s, o_s) = _prompt_layer(
        x2d, nmix, w_in_bf, cos_p, sin_p,
        (sdec, cols(proj_s[0]), cols(proj_s[1]), proj_s[0], proj_s[1], proj_s[2], state_ret[0]),
        kmeta, vmeta, pmeta, dec, wts, jnp.zeros((N_EXPERTS, LANES), F32), batch, seq)

    pool2d = state_pool[0].reshape(nb, POOL_BUF * POOL_WIDTH)
    (pool_s, h1_s, xn2_s, idx_s, gate_s, rank_s, cnt) = _sample_mixer(o_s, proj_s, xs2d, pool2d, wts, cnt_p)

    assert n_prompt_tok % ROW_TILE == 0 and nb <= ROW_TILE
    n_phantom = ROW_TILE - nb
    n_tiles = n_prompt_tok // ROW_TILE + 1
    i32 = jnp.int32
    counts = cnt[:, 0].astype(i32)
    padded = ((counts + MOE_BLOCK - 1) // MOE_BLOCK) * MOE_BLOCK
    pad_end = jnp.cumsum(padded)
    pad_start = pad_end - padded
    n_blocks = (n_tok * TOP_K) // MOE_BLOCK + N_EXPERTS
    block_row = jnp.arange(n_blocks, dtype=i32) * MOE_BLOCK
    block_e = jnp.minimum(jnp.sum((pad_end[None, :] <= block_row[:, None]).astype(i32), axis=1), N_EXPERTS - 1)
    nblk = (pad_end[-1:] // MOE_BLOCK).astype(i32)
    e_row = jnp.arange(N_EXPERTS, dtype=i32)
    used = padded > 0
    e_ord = jnp.cumsum(used.astype(i32)) - 1
    later_used = jnp.logical_and(e_row[None, :] > e_row[:, None], used[None, :])
    e_next = jnp.min(jnp.where(later_used, e_row[None, :], N_EXPERTS), axis=1)
    e_next = jnp.where(e_next == N_EXPERTS, -1, e_next)
    of_block = block_e[:, None] == e_row[None, :]
    block_eord = jnp.sum(jnp.where(of_block, e_ord[None, :], 0), axis=1)
    block_enext = jnp.sum(jnp.where(of_block, e_next[None, :], 0), axis=1)
    block_left = jnp.sum(jnp.where(of_block, (pad_start + counts)[None, :], 0), axis=1) - block_row
    block_valid = jnp.clip(block_left, 0, MOE_BLOCK)
    phantom = lambda fill, dt: jnp.full((TOP_K, n_phantom), fill, dt)
    idx = jnp.concatenate([idx_p, idx_s, phantom(-1, i32)], axis=1)
    rank = jnp.concatenate([rank_p, rank_s, phantom(0, i32)], axis=1)
    gates = jnp.concatenate([gate_p, gate_s, phantom(0.0, F32)], axis=1)
    onehot = idx[None] == jnp.arange(N_EXPERTS, dtype=i32)[:, None, None]
    tile_cnt = jnp.sum(onehot.reshape(N_EXPERTS, TOP_K, n_tiles, ROW_TILE).astype(i32), axis=(1, 3)).T
    run_before = jnp.cumsum(tile_cnt, axis=0) - tile_cnt
    tile_off = jnp.cumsum(tile_cnt, axis=1) - tile_cnt
    tile_dst = pad_start[None, :] + run_before
    delta = jnp.repeat((tile_off - run_before).T, ROW_TILE, axis=1)
    pos = rank + jnp.sum(jnp.where(onehot, delta[:, None, :], 0), axis=0)
    pos = jnp.where(idx >= 0, pos, -1)
    tcnt, tdst, ttot = tile_cnt.reshape(-1), tile_dst.reshape(-1), jnp.sum(tile_cnt, axis=1)

    tail = ((0, n_phantom), (0, 0))
    x_sorted = _dispatch(tcnt, tdst, ttot, pad_start + counts, padded - counts, nblk, pos, xn2_p,
                         jnp.pad(xn2_s, tail), n_blocks * MOE_BLOCK)
    y_sorted = _experts(block_e, nblk, block_eord, block_enext, block_valid, x_sorted, wgu, bgu, wd, bd)
    y_p, y_s = _combine(tcnt, tdst, ttot, pos, gates, y_sorted, h1_p, jnp.pad(h1_s, tail), norm_final[None, :])

    y_prompt = y_p.reshape(batch, seq, D_MODEL)
    y_sample = y_s[:nb].reshape(nb, 1, D_MODEL)
    ret_state_prompt = s_fin[None]
    pool_state_prompt = p_fin[:, 1:, :][None]
    ret_state_sample = st_s[None]
    pool_state_sample = pool_s.reshape(nb, POOL_BUF, POOL_WIDTH)[None]
    return (y_prompt, y_sample, ret_state_prompt, pool_state_prompt, ret_state_sample, pool_state_sample)
```

```python
import jax
import jax.numpy as jnp
import numpy as np
from jax import lax
from jax.experimental import pallas as pl
from jax.experimental.pallas import tpu as pltpu

F32 = jnp.float32
BF16 = jnp.bfloat16

D_MODEL = 1024
N_META = 16
RET_HEADS = 4
RET_DK = 128
RET_DV = 256
RET_QK = RET_HEADS * RET_DK
RET_V = RET_HEADS * RET_DV
CHUNK = 128
ROPE_BASE = 10000.0
POOL_WINDOWS = (2, 4, 8, 16)
POOL_GROUPS = 4
POOL_GROUP_DIM = 128
POOL_WIDTH = POOL_GROUPS * POOL_GROUP_DIM
POOL_BUF = max(POOL_WINDOWS) - 1
N_EXPERTS = 32
TOP_K = 4
D_FF = D_MODEL
SWIGLU_LIMIT = 7.0
SWIGLU_ALPHA = 1.702
EPS = 1e-6
IN_WIDTHS = (RET_QK, RET_QK, RET_V, RET_V, POOL_WIDTH, D_MODEL, D_MODEL)
IN_TOTAL = sum(IN_WIDTHS)
IN_OFFS = tuple(int(s) for s in np.cumsum((0,) + IN_WIDTHS))

LANES = 128
ROW_CHUNKS = D_MODEL // (2 * LANES)
MIXER_TILE = 512
MOE_BLOCK = 512
ROW_TILE = 256
SUBLANES = 8
VMEM_LIMIT = 56 * 1024 * 1024
PROMPT_VMEM_LIMIT = 60 * 1024 * 1024

assert N_META + 1 >= max(POOL_WINDOWS)
assert POOL_WINDOWS == (2, 4, 8, 16)


def _dot(a, b):
    return jnp.dot(a, b, preferred_element_type=F32)


def _rmsnorm(x, w):
    return x * lax.rsqrt(jnp.mean(x * x, axis=-1, keepdims=True) + EPS) * w


def _sigmoid(x):
    return 0.5 * jnp.tanh(0.5 * x) + 0.5


def _store_rows_as_tiles(ref, x):
    half = D_MODEL // 2
    hi = pltpu.bitcast(x[:, :half], jnp.uint32) & jnp.uint32(0xFFFF0000)
    lo = pltpu.bitcast(x[:, half:], jnp.uint32) >> 16
    ref[...] = (hi | lo).reshape(x.shape[0], ROW_CHUNKS, LANES)


def _load_rows_from_tiles(ref, rows):
    w = ref[...].reshape(rows, D_MODEL // 2)
    hi = pltpu.bitcast(w & jnp.uint32(0xFFFF0000), F32)
    lo = pltpu.bitcast(w << 16, F32)
    return jnp.concatenate([hi, lo], axis=1)


def _inproj_kernel(x_ref, nw_ref, w_ref, cos_ref, sin_ref,
                   q_ref, k_ref, v_ref, g_ref, p_ref, ga_ref, gb_ref, *, defer_gates=False):
    xn = _rmsnorm(x_ref[...], nw_ref[...]).astype(BF16)
    cos = cos_ref[...]
    sin = sin_ref[...]

    def seg(i):
        return _dot(xn, w_ref[:, IN_OFFS[i]:IN_OFFS[i + 1]])

    def rot(a):
        return a * cos + pltpu.roll(a, RET_DK // 2, 1) * sin

    q = seg(0)
    k = seg(1)
    for h in range(RET_HEADS):
        sl = slice(h * RET_DK, (h + 1) * RET_DK)
        q_ref[:, sl] = rot(q[:, sl]).astype(q_ref.dtype)
        k_ref[:, sl] = (rot(k[:, sl]) * (RET_DK ** -0.5)).astype(k_ref.dtype)
    v_ref[...] = seg(2).astype(v_ref.dtype)

    def gates():
        g = seg(3)
        g_ref[...] = (g * _sigmoid(g)).astype(g_ref.dtype)
        p_ref[...] = seg(4)
        ga_ref[...] = _sigmoid(seg(5)).astype(ga_ref.dtype)
        gb_ref[...] = _sigmoid(seg(6)).astype(gb_ref.dtype)

    if defer_gates:
        return gates
    gates()


def _sample_state_step(step, sdec_ref, qt_ref, kt_ref, q_ref, k_ref, v_ref, st_ref, stout_ref, o_ref):
    grp = st_ref.shape[0]
    row0 = step * grp
    if grp % SUBLANES == 0:
        rows = pl.ds(pl.multiple_of(row0, SUBLANES), grp)
        q8, k8, v8 = q_ref[rows, :], k_ref[rows, :], v_ref[rows, :]
    else:
        assert 2 * grp == SUBLANES
        rows = pl.ds(pl.multiple_of((step // 2) * SUBLANES, SUBLANES), SUBLANES)
        first = step % 2 == 0
        q8, k8, v8 = (jnp.where(first, a[:grp], a[grp:]) for a in (q_ref[rows, :], k_ref[rows, :], v_ref[rows, :]))
    for h in range(RET_HEADS):
        ksl = slice(h * RET_DK, (h + 1) * RET_DK)
        vsl = slice(h * RET_DV, (h + 1) * RET_DV)
        score = jnp.sum(q8[:, ksl] * k8[:, ksl], axis=1, keepdims=True) * sdec_ref[h, 0]
        intra = score * v8[:, vsl]
        for bb in range(grp):
            s_old = st_ref[bb, h]
            qcol = qt_ref[0, ksl, bb:bb + 1] * sdec_ref[h, 1]
            kcol = kt_ref[0, ksl, bb:bb + 1] * sdec_ref[h, 2]
            cross = jnp.sum(s_old * qcol, axis=0, keepdims=True)
            o_ref[pl.ds(row0 + bb, 1), vsl] = intra[bb:bb + 1, :] + cross
            stout_ref[bb, h] = s_old * sdec_ref[h, 3] + kcol * v8[bb:bb + 1, vsl]


def _inproj_small(x2d, nw, w_in_bf, cosf, sinf):
    rows = x2d.shape[0]
    const = lambda i: (0, 0)
    return pl.pallas_call(
        _inproj_kernel,
        grid=(1,),
        in_specs=[
            pl.BlockSpec((rows, D_MODEL), const),
            pl.BlockSpec((1, D_MODEL), const),
            pl.BlockSpec((D_MODEL, IN_TOTAL), const, pipeline_mode=pl.Buffered(1)),
            pl.BlockSpec((rows, RET_DK), const),
            pl.BlockSpec((rows, RET_DK), const),
        ],
        out_specs=[pl.BlockSpec((rows, w), const) for w in IN_WIDTHS],
        out_shape=[jax.ShapeDtypeStruct((rows, w), F32) for w in IN_WIDTHS],
        compiler_params=pltpu.CompilerParams(dimension_semantics=("arbitrary",), vmem_limit_bytes=VMEM_LIMIT),
        name="inproj",
    )(x2d, nw, w_in_bf, cosf, sinf)


def _group_norm(o, gn_row):
    mu = jnp.mean(o, axis=-1, keepdims=True)
    var = jnp.mean(jnp.square(o - mu), axis=-1, keepdims=True)
    return (o - mu) * lax.rsqrt(var + EPS) * gn_row


def _pool_branch(groups, poolw_ref, pscale_ref, wpool_ref):
    pm = [_dot(g.astype(BF16), poolw_ref[i]) for i, g in enumerate(groups)]
    pm = jnp.concatenate(pm, axis=1) * pscale_ref[...]
    return _dot(pm.astype(BF16), wpool_ref[...])


def _merge_tail(o_norm, silu_g, sig_a, sig_b, yb, x, wret_ref, wout_ref):
    ya = _dot((silu_g.astype(F32) * o_norm).astype(BF16), wret_ref[...])
    merged = sig_a.astype(F32) * ya + sig_b.astype(F32) * yb
    return x + _dot(merged.astype(BF16), wout_ref[...])


def _route(h1, nffn_ref, wrt_ref, br_ref, run_scr,
           xn2_ref, idx_ref, gate_ref, rank_ref):
    tm = h1.shape[0]
    xn2 = _rmsnorm(h1, nffn_ref[...]).astype(BF16)
    xn2_ref[...] = xn2
    logits = lax.dot_general(wrt_ref[...], xn2, (((1,), (1,)), ((), ())),
                             preferred_element_type=F32) + br_ref[...]
    e_iota = lax.broadcasted_iota(jnp.int32, (N_EXPERTS, tm), 0)
    work = logits
    vals, sels = [], []
    chosen = jnp.zeros((N_EXPERTS, tm), F32)
    for _ in range(TOP_K):
        m = jnp.max(work, axis=0, keepdims=True)
        sel = jnp.min(jnp.where(work == m, e_iota, N_EXPERTS), axis=0, keepdims=True)
        hit = e_iota == sel
        vals.append(m)
        sels.append(sel)
        chosen = jnp.where(hit, 1.0, chosen)
        work = jnp.where(hit, -jnp.inf, work)
    exps = [jnp.exp(v - vals[0]) for v in vals]
    denom = exps[0] + exps[1] + exps[2] + exps[3]
    gates = [e / denom for e in exps]
    r_i = lax.broadcasted_iota(jnp.int32, (tm, tm), 0)
    c_i = lax.broadcasted_iota(jnp.int32, (tm, tm), 1)
    before = jnp.where(r_i < c_i, 1.0, 0.0).astype(BF16)
    base = run_scr[...] + _dot(chosen.astype(BF16), before)
    for kk in range(TOP_K):
        rk = jnp.sum(jnp.where(e_iota == sels[kk], base, 0.0), axis=0, keepdims=True)
        rank_ref[kk:kk + 1, :] = rk.astype(jnp.int32)
        idx_ref[kk:kk + 1, :] = sels[kk]
        gate_ref[kk:kk + 1, :] = gates[kk]
    run_scr[...] = run_scr[...] + jnp.sum(chosen, axis=1, keepdims=True)


def _mixer_kernel(q_ref, k_ref, v_ref, g_ref, p_ref, ga_ref, gb_ref, x_ref,
                  kmeta_ref, vmeta_ref, pmeta_ref, mask_ref, qdec_ref, kdec_ref, cdec_ref, gn_ref,
                  poolw_ref, pscale_ref, wret_ref, wpool_ref, wout_ref, nffn_ref, wrt_ref, br_ref,
                  cnt0_ref,
                  h1_ref, xn2_ref, idx_ref, gate_ref, rank_ref, cnt_ref, sfin_ref, pfin_ref,
                  s_scr, ext_scr, o_scr, run_scr, *, after_first_pass=None):
    b = pl.program_id(0)
    j = pl.program_id(1)
    nj = pl.num_programs(1)
    tm = q_ref.shape[0]

    def state_update(s_old, kc, vc, h):
        kd = (kc.astype(F32) * kdec_ref[h]).astype(BF16)
        upd = lax.dot_general(kd, vc, (((0,), (0,)), ((), ())), preferred_element_type=F32)
        return s_old * cdec_ref[h] + upd

    @pl.when(jnp.logical_and(b == 0, j == 0))
    def _():
        run_scr[...] = cnt0_ref[:, 0:1]

    @pl.when(j == 0)
    def _():
        for h in range(RET_HEADS):
            kc = kmeta_ref[:, h * RET_DK:(h + 1) * RET_DK]
            vc = vmeta_ref[:, h * RET_DV:(h + 1) * RET_DV]
            s_scr[h] = state_update(jnp.zeros((RET_DK, RET_DV), F32), kc, vc, h)
        ext_scr[0:N_META, :] = pmeta_ref[...]

    n_chunks = tm // CHUNK
    units = [(c, h) for c in range(n_chunks) for h in range(RET_HEADS)]

    def operands(c, h):
        rows = slice(c * CHUNK, (c + 1) * CHUNK)
        return (q_ref[rows, h * RET_DK:(h + 1) * RET_DK], k_ref[rows, h * RET_DK:(h + 1) * RET_DK],
                v_ref[rows, h * RET_DV:(h + 1) * RET_DV])

    lhs, upd = {}, {}
    for c, h in units:
        qc, kc, vc = operands(c, h)
        scores = lax.dot_general(qc, kc, (((1,), (1,)), ((), ())), preferred_element_type=F32) * mask_ref[h]
        qd = (qc.astype(F32) * qdec_ref[h]).astype(BF16)
        lhs[c, h] = jnp.concatenate([scores.astype(BF16), qd], axis=1)
        kd = (kc.astype(F32) * kdec_ref[h]).astype(BF16)
        upd[c, h] = lax.dot_general(kd, vc, (((0,), (0,)), ((), ())), preferred_element_type=F32)
    if after_first_pass is not None:
        after_first_pass()
    state = {}
    for h in range(RET_HEADS):
        s = s_scr[h]
        for c in range(n_chunks):
            state[c, h] = s
            s = s * cdec_ref[h] + upd[c, h]
        s_scr[h] = s
    for c, h in units:
        rhs = jnp.concatenate([operands(c, h)[2], state[c, h].astype(BF16)], axis=0)
        o = _dot(lhs[c, h], rhs)
        o_scr[c * CHUNK:(c + 1) * CHUNK, h * RET_DV:(h + 1) * RET_DV] = _group_norm(
            o, gn_ref[:, h * RET_DV:(h + 1) * RET_DV])

    p = p_ref[...]
    ext_scr[N_META:N_META + tm, :] = p
    a = ext_scr[...]
    g1 = POOL_GROUP_DIM
    s2 = a + pltpu.roll(a, 1, 0)
    s4 = s2[:, g1:] + pltpu.roll(s2[:, g1:], 2, 0)
    s8 = s4[:, g1:] + pltpu.roll(s4[:, g1:], 4, 0)
    s16 = s8[:, g1:] + pltpu.roll(s8[:, g1:], 8, 0)
    sums = (s2[N_META:, :g1], s4[N_META:, :g1], s8[N_META:, :g1], s16[N_META:, :])
    groups = [sums[i] * (1.0 / POOL_WINDOWS[i]) - p[:, i * g1:(i + 1) * g1] for i in range(POOL_GROUPS)]
    ext_scr[0:N_META, :] = ext_scr[tm:tm + N_META, :]

    yb = _pool_branch(groups, poolw_ref, pscale_ref, wpool_ref)
    h1 = _merge_tail(o_scr[...], g_ref[...], ga_ref[...], gb_ref[...], yb, x_ref[...], wret_ref, wout_ref)
    h1_ref[...] = h1
    _route(h1, nffn_ref, wrt_ref, br_ref, run_scr, xn2_ref, idx_ref, gate_ref, rank_ref)
    cnt_ref[...] = jnp.broadcast_to(run_scr[...], cnt_ref.shape)

    @pl.when(j == nj - 1)
    def _():
        for h in range(RET_HEADS):
            sfin_ref[0, h] = s_scr[h]
        pfin_ref[0] = ext_scr[0:N_META, :]


N_SAMPLE_IN = 7
N_MIXER_IN = 17
N_MIXER_OUT = 8
N_PROJ = len(IN_WIDTHS)


def _prompt_kernel(*refs):
    it = iter(refs)
    take = lambda n: [next(it) for _ in range(n)]
    x_ref, nw_ref, w_ref, cos_ref, sin_ref = take(5)
    sample_in = take(N_SAMPLE_IN)
    mixer_in = take(N_MIXER_IN)
    mixer_out = take(N_MIXER_OUT)
    stout_ref, os_ref = take(2)
    proj_scr = take(N_PROJ)
    mixer_scr = take(4)
    gate_columns = _inproj_kernel(x_ref, nw_ref, w_ref, cos_ref, sin_ref, *proj_scr, defer_gates=True)
    step = pl.program_id(0) * pl.num_programs(1) + pl.program_id(1)
    _sample_state_step(step, *sample_in, stout_ref, os_ref)
    _mixer_kernel(*proj_scr, x_ref, *mixer_in, *mixer_out, *mixer_scr, after_first_pass=gate_columns)


def _prompt_layer(x2d, nw, w_in_bf, cosf, sinf, sample, kmeta, vmeta, pmeta, dec, wts, cnt0, batch, seq):
    tm = MIXER_TILE
    nj = seq // tm
    rows = batch * seq
    row_map = lambda b, j: (b * nj + j, 0)
    lane_map = lambda b, j: (0, b * nj + j)
    c2 = lambda b, j: (0, 0)
    c3 = lambda b, j: (0, 0, 0)

    def whole(a):
        mode = dict(pipeline_mode=pl.Buffered(1)) if a.size * a.dtype.itemsize >= 512 * 1024 else {}
        return pl.BlockSpec(a.shape, c2 if a.ndim == 2 else c3, **mode)

    sdec, qt, kt, qs, ks, vs, state = sample
    nb = state.shape[0]
    grp = nb // (batch * nj)
    assert grp * batch * nj == nb and qt.shape == (nb // grp, RET_QK, grp)
    step_map = lambda b, j: (b * nj + j, 0, 0)
    st_spec = pl.BlockSpec((grp, RET_HEADS, RET_DK, RET_DV), lambda b, j: (b * nj + j, 0, 0, 0))
    mask, qdec, kdec, cdec = dec
    mixer_in = [kmeta, vmeta, pmeta, mask, qdec, kdec, cdec, *wts, cnt0]
    assert len(mixer_in) == N_MIXER_IN
    in_arrays = [x2d, nw, w_in_bf, cosf, sinf, sdec, qt, kt, qs, ks, vs, state] + mixer_in
    in_specs = [
        pl.BlockSpec((tm, D_MODEL), row_map),
        pl.BlockSpec((1, D_MODEL), c2),
        pl.BlockSpec((D_MODEL, IN_TOTAL), c2, pipeline_mode=pl.Buffered(1)),
        pl.BlockSpec((tm, RET_DK), lambda b, j: (j, 0)),
        pl.BlockSpec((tm, RET_DK), lambda b, j: (j, 0)),
        pl.BlockSpec(memory_space=pltpu.SMEM),
        pl.BlockSpec((1, RET_QK, grp), step_map), pl.BlockSpec((1, RET_QK, grp), step_map),
        pl.BlockSpec(qs.shape, c2), pl.BlockSpec(ks.shape, c2), pl.BlockSpec(vs.shape, c2),
        st_spec,
    ] + [whole(a) for a in mixer_in]
    out_shape = [
        jax.ShapeDtypeStruct((rows, D_MODEL), F32),
        jax.ShapeDtypeStruct((rows, D_MODEL), BF16),
        jax.ShapeDtypeStruct((TOP_K, rows), jnp.int32),
        jax.ShapeDtypeStruct((TOP_K, rows), F32),
        jax.ShapeDtypeStruct((TOP_K, rows), jnp.int32),
        jax.ShapeDtypeStruct((N_EXPERTS, LANES), F32),
        jax.ShapeDtypeStruct((batch, RET_HEADS, RET_DK, RET_DV), F32),
        jax.ShapeDtypeStruct((batch, N_META, POOL_WIDTH), F32),
        jax.ShapeDtypeStruct(state.shape, F32),
        jax.ShapeDtypeStruct((nb, RET_V), F32),
    ]
    out_specs = [
        pl.BlockSpec((tm, D_MODEL), row_map),
        pl.BlockSpec((tm, D_MODEL), row_map),
        pl.BlockSpec((TOP_K, tm), lane_map),
        pl.BlockSpec((TOP_K, tm), lane_map),
        pl.BlockSpec((TOP_K, tm), lane_map),
        pl.BlockSpec((N_EXPERTS, LANES), c2),
        pl.BlockSpec((1, RET_HEADS, RET_DK, RET_DV), lambda b, j: (b, 0, 0, 0)),
        pl.BlockSpec((1, N_META, POOL_WIDTH), lambda b, j: (b, 0, 0)),
        st_spec,
        pl.BlockSpec((nb, RET_V), c2),
    ]
    proj_dts = (BF16, BF16, BF16, BF16, F32, BF16, BF16)
    return pl.pallas_call(
        _prompt_kernel,
        grid=(batch, nj),
        in_specs=in_specs,
        out_specs=out_specs,
        out_shape=out_shape,
        scratch_shapes=[pltpu.VMEM((tm, w), dt) for w, dt in zip(IN_WIDTHS, proj_dts)] + [
            pltpu.VMEM((RET_HEADS, RET_DK, RET_DV), F32),
            pltpu.VMEM((N_META + tm, POOL_WIDTH), F32),
            pltpu.VMEM((tm, RET_V), F32),
            pltpu.VMEM((N_EXPERTS, 1), F32),
        ],
        compiler_params=pltpu.CompilerParams(
            dimension_semantics=("arbitrary", "arbitrary"), vmem_limit_bytes=PROMPT_VMEM_LIMIT),
        name="prompt_layer",
    )(*in_arrays)


def _sample_kernel(o_ref, g_ref, p_ref, ga_ref, gb_ref, x_ref, pool_ref, gn_ref,
                   poolw_ref, pscale_ref, wret_ref, wpool_ref, wout_ref, nffn_ref, wrt_ref, br_ref, cnt0_ref,
                   poolout_ref, h1_ref, xn2_ref, idx_ref, gate_ref, rank_ref, cnt_ref,
                   run_scr):
    run_scr[...] = cnt0_ref[:, 0:1]
    o = o_ref[...]
    o_norm = jnp.concatenate(
        [_group_norm(o[:, h * RET_DV:(h + 1) * RET_DV], gn_ref[:, h * RET_DV:(h + 1) * RET_DV])
         for h in range(RET_HEADS)], axis=1)
    p = p_ref[...]
    w = POOL_WIDTH
    g1 = POOL_GROUP_DIM

    def prev(r, lo):
        return pool_ref[:, r * w + lo:(r + 1) * w]

    s2 = p + prev(14, 0)
    s4 = s2[:, g1:] + prev(13, g1) + prev(12, g1)
    s8 = s4[:, g1:] + prev(11, 2 * g1) + prev(10, 2 * g1) + prev(9, 2 * g1) + prev(8, 2 * g1)
    s16 = s8[:, g1:]
    for r in range(7, -1, -1):
        s16 = s16 + prev(r, 3 * g1)
    sums = (s2[:, :g1], s4[:, :g1], s8[:, :g1], s16)
    groups = [sums[t] * (1.0 / POOL_WINDOWS[t]) - p[:, t * g1:(t + 1) * g1] for t in range(POOL_GROUPS)]
    poolout_ref[:, 0:(POOL_BUF - 1) * w] = pool_ref[:, w:POOL_BUF * w]
    poolout_ref[:, (POOL_BUF - 1) * w:] = p
    yb = _pool_branch(groups, poolw_ref, pscale_ref, wpool_ref)
    h1 = _merge_tail(o_norm, g_ref[...], ga_ref[...], gb_ref[...], yb, x_ref[...], wret_ref, wout_ref)
    h1_ref[...] = h1
    _route(h1, nffn_ref, wrt_ref, br_ref, run_scr, xn2_ref, idx_ref, gate_ref, rank_ref)
    cnt_ref[...] = jnp.broadcast_to(run_scr[...], cnt_ref.shape)


def _sample_mixer(o_s, proj, x2d, pool2d, wts, cnt0):
    _, _, _, g, p, ga, gb = proj
    nb = x2d.shape[0]
    c2 = lambda i: (0, 0)

    def whole(a):
        return pl.BlockSpec(a.shape, c2)

    gn, poolw, pscale, wret, wpool, wout, nffn, wrt, br = wts
    in_arrays = [o_s, g, p, ga, gb, x2d, pool2d, gn, poolw, pscale, wret, wpool, wout, nffn, wrt, br, cnt0]
    in_specs = [whole(a) for a in in_arrays[:8]]
    in_specs += [pl.BlockSpec(poolw.shape, lambda i: (0, 0, 0))]
    in_specs += [whole(a) for a in in_arrays[9:]]
    out_shape = [
        jax.ShapeDtypeStruct(pool2d.shape, F32),
        jax.ShapeDtypeStruct((nb, D_MODEL), F32),
        jax.ShapeDtypeStruct((nb, D_MODEL), BF16),
        jax.ShapeDtypeStruct((TOP_K, nb), jnp.int32),
        jax.ShapeDtypeStruct((TOP_K, nb), F32),
        jax.ShapeDtypeStruct((TOP_K, nb), jnp.int32),
        jax.ShapeDtypeStruct((N_EXPERTS, LANES), F32),
    ]
    return pl.pallas_call(
        _sample_kernel,
        grid=(1,),
        in_specs=in_specs,
        out_specs=[pl.BlockSpec(s.shape, c2) for s in out_shape],
        out_shape=out_shape,
        scratch_shapes=[pltpu.VMEM((N_EXPERTS, 1), F32)],
        compiler_params=pltpu.CompilerParams(
            dimension_semantics=("arbitrary",), vmem_limit_bytes=VMEM_LIMIT),
        name="sample_mixer",
    )(*in_arrays)


TILE_ROWS = ROW_TILE * TOP_K


def _run_copy(src_ref, src_row, dst_ref, dst_row, n_rows, sem):
    return pltpu.make_async_copy(src_ref.at[pl.ds(src_row, n_rows)], dst_ref.at[pl.ds(dst_row, n_rows)], sem)


def _for_each_run(tcnt_ref, tile, fn, enabled=True):
    off = 0
    for e in range(N_EXPERTS):
        n = tcnt_ref[tile * N_EXPERTS + e]

        @pl.when(jnp.logical_and(n > 0, enabled))
        def _(e=e, off=off, n=n):
            fn(e, off, n)

        off = off + n


def _dispatch_kernel(tcnt_ref, tdst_ref, ttot_ref, zrow_ref, zcnt_ref, nblk_ref, pos_ref, xp_ref, xs_ref, out_ref,
                     sorted_scr, zero_scr, sem, zsem):
    i = pl.program_id(0)
    n_prompt = pl.num_programs(0) - 1
    n_blocks = out_ref.shape[0] // MOE_BLOCK
    slot = i % 2

    def wait_tile(tile, s):
        _run_copy(out_ref, 0, sorted_scr.at[s], 0, ttot_ref[tile], sem.at[s]).wait()

    def for_each_pad(fn):
        def body(e, carry):
            n = zcnt_ref[e]

            @pl.when(n > 0)
            def _():
                fn(_run_copy(zero_scr, 0, out_ref, zrow_ref[e], n, zsem))

            dead = nblk_ref[0] + e

            @pl.when(dead < n_blocks)
            def _():
                fn(_run_copy(zero_scr, 0, out_ref, dead * MOE_BLOCK, MOE_BLOCK, zsem))

            return carry

        lax.fori_loop(0, N_EXPERTS, body, 0)

    @pl.when(i == 0)
    def _():
        zero_scr[...] = jnp.zeros_like(zero_scr)
        for_each_pad(lambda cp: cp.start())

    def sort_tile(x_ref):
        r_iota = lax.broadcasted_iota(jnp.int32, (TILE_ROWS, ROW_TILE), 0)
        hit = r_iota == pos_ref[0:1, :]
        for kk in range(1, TOP_K):
            hit = jnp.logical_or(hit, r_iota == pos_ref[kk:kk + 1, :])
        perm = jnp.where(hit, 1.0, 0.0).astype(BF16)
        xs = _dot(perm, x_ref[...])
        _store_rows_as_tiles(sorted_scr.at[slot], xs)
        _for_each_run(tcnt_ref, i, lambda e, off, n: _run_copy(
            sorted_scr.at[slot], off, out_ref, tdst_ref[i * N_EXPERTS + e], n, sem.at[slot]).start())

    @pl.when(i >= 2)
    def _():
        wait_tile(i - 2, slot)

    @pl.when(i < n_prompt)
    def _():
        sort_tile(xp_ref)

    @pl.when(i == n_prompt)
    def _():
        sort_tile(xs_ref)

    @pl.when(i == n_prompt)
    def _():
        wait_tile(i - 1, 1 - slot)
        wait_tile(i, slot)
        for_each_pad(lambda cp: cp.wait())


def _dispatch(tcnt, tdst, ttot, zrow, zcnt, nblk, pos, xn2_p, xn2_s, n_sorted):
    n_prompt = xn2_p.shape[0] // ROW_TILE
    grid_spec = pltpu.PrefetchScalarGridSpec(
        num_scalar_prefetch=6,
        grid=(n_prompt + 1,),
        in_specs=[
            pl.BlockSpec((TOP_K, ROW_TILE), lambda i, *_: (0, i)),
            pl.BlockSpec((ROW_TILE, D_MODEL), lambda i, *_: (jnp.minimum(i, n_prompt - 1), 0)),
            pl.BlockSpec((ROW_TILE, D_MODEL), lambda i, *_: (0, 0)),
        ],
        out_specs=pl.BlockSpec(memory_space=pl.ANY),
        scratch_shapes=[
            pltpu.VMEM((2, TILE_ROWS, ROW_CHUNKS, LANES), jnp.uint32),
            pltpu.VMEM((MOE_BLOCK, ROW_CHUNKS, LANES), jnp.uint32),
            pltpu.SemaphoreType.DMA((2,)), pltpu.SemaphoreType.DMA(()),
        ],
    )
    return pl.pallas_call(
        _dispatch_kernel,
        grid_spec=grid_spec,
        out_shape=jax.ShapeDtypeStruct((n_sorted, ROW_CHUNKS, LANES), jnp.uint32),
        compiler_params=pltpu.CompilerParams(dimension_semantics=("arbitrary",)),
        name="moe_dispatch",
    )(tcnt, tdst, ttot, zrow, zcnt, nblk, pos, xn2_p, xn2_s)


def _expert_kernel(be_ref, nblk_ref, eord_ref, enext_ref, bvalid_ref, x_ref, wgu_ref, bgu_ref, wd_ref, bd_ref, y_ref,
                   wgu_f32, wd_f32, wgu_bf, wd_bf, wsem):
    i = pl.program_id(0)
    live = i < nblk_ref[0]
    new_expert = jnp.logical_or(i == 0, be_ref[i] != be_ref[jnp.maximum(i - 1, 0)])

    def weight_copies(e, s):
        return (pltpu.make_async_copy(wgu_ref.at[e], wgu_f32.at[s], wsem.at[0, s]),
                pltpu.make_async_copy(wd_ref.at[e], wd_f32.at[s], wsem.at[1, s]))

    @pl.when(jnp.logical_and(live, new_expert))
    def _():
        slot = eord_ref[i] % 2

        @pl.when(i == 0)
        def _():
            for cp in weight_copies(be_ref[0], 0):
                cp.start()

        for cp in weight_copies(be_ref[i], slot):
            cp.wait()

        @pl.when(enext_ref[i] >= 0)
        def _():
            for cp in weight_copies(enext_ref[i], 1 - slot):
                cp.start()

    @pl.when(jnp.logical_not(live))
    def _():
        y_ref[...] = jnp.zeros_like(y_ref)

    def ffn(rows, first):
        if first:
            slot = eord_ref[i] % 2
            wgu = wgu_f32[slot].astype(BF16)
            wgu_bf[...] = wgu
        else:
            wgu = wgu_bf[...]
        x = _load_rows_from_tiles(x_ref.at[pl.ds(0, rows)], rows)
        h = _dot(x.astype(BF16), wgu) + bgu_ref[0]
        gate = jnp.minimum(h[:, :D_FF], SWIGLU_LIMIT)
        up = jnp.clip(h[:, D_FF:], -SWIGLU_LIMIT, SWIGLU_LIMIT)
        glu = gate * _sigmoid(gate * SWIGLU_ALPHA)
        if first:
            wd_b = wd_f32[slot].astype(BF16)
            wd_bf[...] = wd_b
        else:
            wd_b = wd_bf[...]
        y = _dot(((up + 1.0) * glu).astype(BF16), wd_b) + bd_ref[0]
        _store_rows_as_tiles(y_ref.at[pl.ds(0, rows)], y.astype(BF16).astype(F32))

    half = MOE_BLOCK // 2
    short = bvalid_ref[i] <= half
    for first in (False, True):
        same = new_expert if first else jnp.logical_not(new_expert)

        @pl.when(jnp.logical_and(jnp.logical_and(live, same), jnp.logical_not(short)))
        def _(first=first):
            ffn(MOE_BLOCK, first)

        @pl.when(jnp.logical_and(jnp.logical_and(live, same), short))
        def _(first=first):
            ffn(half, first)
            y_ref[pl.ds(half, half)] = jnp.zeros((half,) + y_ref.shape[1:], y_ref.dtype)


def _experts(block_e, nblk, block_eord, block_enext, block_valid, x_sorted, wgu, bgu, wd, bd):
    n_blocks = x_sorted.shape[0] // MOE_BLOCK
    wmap = lambda i, be, *_: (be[i], 0, 0)
    rmap = lambda i, *_: (i, 0, 0)
    xmap = lambda i, be, nb, *_: (jnp.minimum(i, nb[0] - 1), 0, 0)
    hbm = pl.BlockSpec(memory_space=pl.ANY)
    grid_spec = pltpu.PrefetchScalarGridSpec(
        num_scalar_prefetch=5,
        grid=(n_blocks,),
        in_specs=[
            pl.BlockSpec((MOE_BLOCK, ROW_CHUNKS, LANES), xmap),
            hbm,
            pl.BlockSpec((1, 1, 2 * D_FF), wmap),
            hbm,
            pl.BlockSpec((1, 1, D_MODEL), wmap),
        ],
        out_specs=pl.BlockSpec((MOE_BLOCK, ROW_CHUNKS, LANES), rmap),
        scratch_shapes=[
            pltpu.VMEM((2, D_MODEL, 2 * D_FF), F32), pltpu.VMEM((2, D_FF, D_MODEL), F32),
            pltpu.VMEM((D_MODEL, 2 * D_FF), BF16), pltpu.VMEM((D_FF, D_MODEL), BF16),
            pltpu.SemaphoreType.DMA((2, 2)),
        ],
    )
    return pl.pallas_call(
        _expert_kernel,
        grid_spec=grid_spec,
        out_shape=jax.ShapeDtypeStruct(x_sorted.shape, jnp.uint32),
        compiler_params=pltpu.CompilerParams(
            dimension_semantics=("arbitrary",), vmem_limit_bytes=VMEM_LIMIT),
        name="moe_experts",
    )(block_e, nblk, block_eord, block_enext, block_valid, x_sorted, wgu, bgu, wd, bd)


def _combine_kernel(tcnt_ref, tdst_ref, ttot_ref, pos_ref, gate_ref, ys_ref, hp_ref, hs_ref, nf_ref, yp_ref, ysmp_ref,
                    runs_scr, sem):
    i = pl.program_id(0)
    n_tiles = pl.num_programs(0)
    n_prompt = n_tiles - 1
    slot = i % 2

    def start_runs(tile, s, enabled=True):
        _for_each_run(tcnt_ref, tile, lambda e, off, n: _run_copy(
            ys_ref, tdst_ref[tile * N_EXPERTS + e], runs_scr.at[s], off, n, sem.at[s]).start(), enabled)

    @pl.when(i == 0)
    def _():
        start_runs(0, 0)

    _run_copy(ys_ref, 0, runs_scr.at[slot], 0, ttot_ref[i], sem.at[slot]).wait()
    ys = _load_rows_from_tiles(runs_scr.at[slot], TILE_ROWS).astype(BF16)
    r_iota = lax.broadcasted_iota(jnp.int32, (TILE_ROWS, ROW_TILE), 0)
    gmat = jnp.zeros((TILE_ROWS, ROW_TILE), F32)
    for kk in range(TOP_K):
        gmat = jnp.where(r_iota == pos_ref[kk:kk + 1, :], gate_ref[kk:kk + 1, :], gmat)
    start_runs(jnp.minimum(i + 1, n_tiles - 1), 1 - slot, enabled=i + 1 < n_tiles)
    moe = lax.dot_general(gmat.astype(BF16), ys, (((0,), (0,)), ((), ())), preferred_element_type=F32)

    @pl.when(i < n_prompt)
    def _():
        yp_ref[...] = _rmsnorm(hp_ref[...] + moe, nf_ref[...])

    @pl.when(i == n_prompt)
    def _():
        ysmp_ref[...] = _rmsnorm(hs_ref[...] + moe, nf_ref[...])


def _combine(tcnt, tdst, ttot, pos, gates, y_sorted, h1_p, h1_s, nf):
    n_prompt = h1_p.shape[0] // ROW_TILE
    pmap = lambda i, *_: (jnp.minimum(i, n_prompt - 1), 0)
    smap = lambda i, *_: (0, 0)
    lmap = lambda i, *_: (0, i)
    grid_spec = pltpu.PrefetchScalarGridSpec(
        num_scalar_prefetch=3,
        grid=(n_prompt + 1,),
        in_specs=[
            pl.BlockSpec((TOP_K, ROW_TILE), lmap),
            pl.BlockSpec((TOP_K, ROW_TILE), lmap),
            pl.BlockSpec(memory_space=pl.ANY),
            pl.BlockSpec((ROW_TILE, D_MODEL), pmap),
            pl.BlockSpec((ROW_TILE, D_MODEL), smap),
            pl.BlockSpec((1, D_MODEL), smap),
        ],
        out_specs=[pl.BlockSpec((ROW_TILE, D_MODEL), pmap), pl.BlockSpec((ROW_TILE, D_MODEL), smap)],
        scratch_shapes=[pltpu.VMEM((2, TILE_ROWS, ROW_CHUNKS, LANES), jnp.uint32), pltpu.SemaphoreType.DMA((2,))],
    )
    return pl.pallas_call(
        _combine_kernel,
        grid_spec=grid_spec,
        out_shape=[jax.ShapeDtypeStruct(h1_p.shape, F32), jax.ShapeDtypeStruct(h1_s.shape, F32)],
        compiler_params=pltpu.CompilerParams(dimension_semantics=("arbitrary",)),
        name="moe_combine",
    )(tcnt, tdst, ttot, pos, gates, y_sorted, h1_p, h1_s, nf)


def _rotary_tables(pos):
    f = np.float32
    inv = np.power(f(ROPE_BASE), -np.arange(0, RET_DK, 2, dtype=f) / f(RET_DK)).astype(f)
    ang = (np.asarray(pos, f)[:, None] * inv[None, :]).astype(f)
    cos, sin = np.cos(ang).astype(f), np.sin(ang).astype(f)
    return np.concatenate([cos, cos], axis=1), np.concatenate([-sin, sin], axis=1)


def _decay_tables(chunk):
    f = np.float32
    log_g = np.log1p(-np.exp2(f(-5.0) - np.arange(RET_HEADS, dtype=f))).astype(f)
    i = np.arange(chunk, dtype=f)
    diff = i[:, None] - i[None, :]
    mask = np.where(diff[None] >= 0, np.exp(np.maximum(diff, f(0.0))[None] * log_g[:, None, None]), f(0.0)).astype(f)
    q_dec = np.exp((i + f(1.0))[None, :] * log_g[:, None]).astype(f)
    k_dec = np.exp((f(chunk) - f(1.0) - i)[None, :] * log_g[:, None]).astype(f)
    c_dec = np.exp(f(chunk) * log_g).astype(f)
    return mask, q_dec, k_dec, c_dec


def kernel(x_prompt, x_sample, state_ret, state_pool, meta_tokens, norm_mix, w_in, ret_gn, pool_w, pool_scale,
           w_ret_branch, w_pool_branch, w_out, norm_ffn, w_router, b_router, w_gate_up, b_gate_up, w_down, b_down,
           norm_final):
    batch, seq, _ = x_prompt.shape
    nb = x_sample.shape[0]
    past_len = 16384
    n_prompt_tok = batch * seq
    n_tok = n_prompt_tok + nb

    w_in_bf = w_in[0].astype(BF16)
    wts = (ret_gn[0][None, :], pool_w[0].astype(BF16), pool_scale[0][None, :],
           w_ret_branch[0].astype(BF16), w_pool_branch[0].astype(BF16), w_out[0].astype(BF16),
           norm_ffn[0][None, :], w_router[0].T.astype(BF16), b_router[0][:, None])
    wgu = w_gate_up[0]
    wd = w_down[0]
    bgu = b_gate_up[0][:, None, :]
    bd = b_down[0][:, None, :]
    nmix = norm_mix[0][None, :]

    cos_p, sin_p = _rotary_tables(N_META + np.arange(seq))
    cos_s, sin_s = _rotary_tables(np.concatenate([np.arange(N_META), np.full((nb,), past_len)]))
    mask, q_dec, k_dec, c_dec = _decay_tables(CHUNK)
    dec = (mask,
           np.ascontiguousarray(np.broadcast_to(q_dec[:, :, None], (RET_HEADS, CHUNK, RET_DK))),
           np.ascontiguousarray(np.broadcast_to(k_dec[:, :, None], (RET_HEADS, CHUNK, RET_DK))),
           np.ascontiguousarray(np.broadcast_to(c_dec[:, None, None], (RET_HEADS, 1, RET_DV))))
    m1, q1, k1, c1 = _decay_tables(1)
    sdec = np.stack([m1[:, 0, 0], q1[:, 0], k1[:, 0], c1], axis=1)

    x2d = x_prompt.reshape(n_prompt_tok, D_MODEL)
    xs2d = x_sample.reshape(nb, D_MODEL)
    x_small = jnp.concatenate([meta_tokens, xs2d], axis=0)
    proj_small = _inproj_small(x_small, nmix, w_in_bf, cos_s, sin_s)
    proj_s = tuple(a[N_META:] for a in proj_small)

    lead = CHUNK - N_META
    kmeta = jnp.pad(proj_small[1][:N_META], ((lead, 0), (0, 0))).astype(BF16)
    vmeta = jnp.pad(proj_small[2][:N_META], ((lead, 0), (0, 0))).astype(BF16)
    pmeta = proj_small[4][:N_META]

    grp = nb // (n_prompt_tok // MIXER_TILE)

    def cols(a):
        return a.T.reshape(RET_QK, nb // grp, grp).transpose(1, 0, 2)

    (h1_p, xn2_p, idx_p, gate_p, rank_p, cnt_p, s_fin, p_fin, st_s, o_s) = _prompt_layer(
        x2d, nmix, w_in_bf, cos_p, sin_p,
        (sdec, cols(proj_s[0]), cols(proj_s[1]), proj_s[0], proj_s[1], proj_s[2], state_ret[0]),
        kmeta, vmeta, pmeta, dec, wts, jnp.zeros((N_EXPERTS, LANES), F32), batch, seq)

    pool2d = state_pool[0].reshape(nb, POOL_BUF * POOL_WIDTH)
    (pool_s, h1_s, xn2_s, idx_s, gate_s, rank_s, cnt) = _sample_mixer(o_s, proj_s, xs2d, pool2d, wts, cnt_p)

    assert n_prompt_tok % ROW_TILE == 0 and nb <= ROW_TILE
    n_phantom = ROW_TILE - nb
    n_tiles = n_prompt_tok // ROW_TILE + 1
    i32 = jnp.int32
    counts = cnt[:, 0].astype(i32)
    padded = ((counts + MOE_BLOCK - 1) // MOE_BLOCK) * MOE_BLOCK
    pad_end = jnp.cumsum(padded)
    pad_start = pad_end - padded
    n_blocks = (n_tok * TOP_K) // MOE_BLOCK + N_EXPERTS
    block_row = jnp.arange(n_blocks, dtype=i32) * MOE_BLOCK
    block_e = jnp.minimum(jnp.sum((pad_end[None, :] <= block_row[:, None]).astype(i32), axis=1), N_EXPERTS - 1)
    nblk = (pad_end[-1:] // MOE_BLOCK).astype(i32)
    e_row = jnp.arange(N_EXPERTS, dtype=i32)
    used = padded > 0
    e_ord = jnp.cumsum(used.astype(i32)) - 1
    later_used = jnp.logical_and(e_row[None, :] > e_row[:, None], used[None, :])
    e_next = jnp.min(jnp.where(later_used, e_row[None, :], N_EXPERTS), axis=1)
    e_next = jnp.where(e_next == N_EXPERTS, -1, e_next)
    of_block = block_e[:, None] == e_row[None, :]
    block_eord = jnp.sum(jnp.where(of_block, e_ord[None, :], 0), axis=1)
    block_enext = jnp.sum(jnp.where(of_block, e_next[None, :], 0), axis=1)
    block_left = jnp.sum(jnp.where(of_block, (pad_start + counts)[None, :], 0), axis=1) - block_row
    block_valid = jnp.clip(block_left, 0, MOE_BLOCK)
    phantom = lambda fill, dt: jnp.full((TOP_K, n_phantom), fill, dt)
    idx = jnp.concatenate([idx_p, idx_s, phantom(-1, i32)], axis=1)
    rank = jnp.concatenate([rank_p, rank_s, phantom(0, i32)], axis=1)
    gates = jnp.concatenate([gate_p, gate_s, phantom(0.0, F32)], axis=1)
    onehot = idx[None] == jnp.arange(N_EXPERTS, dtype=i32)[:, None, None]
    tile_cnt = jnp.sum(onehot.reshape(N_EXPERTS, TOP_K, n_tiles, ROW_TILE).astype(i32), axis=(1, 3)).T
    run_before = jnp.cumsum(tile_cnt, axis=0) - tile_cnt
    tile_off = jnp.cumsum(tile_cnt, axis=1) - tile_cnt
    tile_dst = pad_start[None, :] + run_before
    delta = jnp.repeat((tile_off - run_before).T, ROW_TILE, axis=1)
    pos = rank + jnp.sum(jnp.where(onehot, delta[:, None, :], 0), axis=0)
    pos = jnp.where(idx >= 0, pos, -1)
    tcnt, tdst, ttot = tile_cnt.reshape(-1), tile_dst.reshape(-1), jnp.sum(tile_cnt, axis=1)

    tail = ((0, n_phantom), (0, 0))
    x_sorted = _dispatch(tcnt, tdst, ttot, pad_start + counts, padded - counts, nblk, pos, xn2_p,
                         jnp.pad(xn2_s, tail), n_blocks * MOE_BLOCK)
    y_sorted = _experts(block_e, nblk, block_eord, block_enext, block_valid, x_sorted, wgu, bgu, wd, bd)
    y_p, y_s = _combine(tcnt, tdst, ttot, pos, gates, y_sorted, h1_p, jnp.pad(h1_s, tail), norm_final[None, :])

    y_prompt = y_p.reshape(batch, seq, D_MODEL)
    y_sample = y_s[:nb].reshape(nb, 1, D_MODEL)
    ret_state_prompt = s_fin[None]
    pool_state_prompt = p_fin[:, 1:, :][None]
    ret_state_sample = st_s[None]
    pool_state_sample = pool_s.reshape(nb, POOL_BUF, POOL_WIDTH)[None]
    return (y_prompt, y_sample, ret_state_prompt, pool_state_prompt, ret_state_sample, pool_state_sample)
```

```python
import jax
import jax.numpy as jnp
import numpy as np
from jax import lax
from jax.experimental import pallas as pl
from jax.experimental.pallas import tpu as pltpu

F32 = jnp.float32
BF16 = jnp.bfloat16

D_MODEL = 1024
N_META = 16
RET_HEADS = 4
RET_DK = 128
RET_DV = 256
RET_QK = RET_HEADS * RET_DK
RET_V = RET_HEADS * RET_DV
CHUNK = 128
ROPE_BASE = 10000.0
POOL_WINDOWS = (2, 4, 8, 16)
POOL_GROUPS = 4
POOL_GROUP_DIM = 128
POOL_WIDTH = POOL_GROUPS * POOL_GROUP_DIM
POOL_BUF = max(POOL_WINDOWS) - 1
N_EXPERTS = 32
TOP_K = 4
D_FF = D_MODEL
SWIGLU_LIMIT = 7.0
SWIGLU_ALPHA = 1.702
EPS = 1e-6
IN_WIDTHS = (RET_QK, RET_QK, RET_V, RET_V, POOL_WIDTH, D_MODEL, D_MODEL)
IN_TOTAL = sum(IN_WIDTHS)
IN_OFFS = tuple(int(s) for s in np.cumsum((0,) + IN_WIDTHS))

LANES = 128
ROW_CHUNKS = D_MODEL // (2 * LANES)
MIXER_TILE = 512
MOE_BLOCK = 512
ROW_TILE = 256
SUBLANES = 8
COMBINE_RING = 3
VMEM_LIMIT = 56 * 1024 * 1024
PROMPT_VMEM_LIMIT = 60 * 1024 * 1024

assert N_META + 1 >= max(POOL_WINDOWS)
assert POOL_WINDOWS == (2, 4, 8, 16)


def _dot(a, b):
    return jnp.dot(a, b, preferred_element_type=F32)


def _rmsnorm(x, w):
    return x * lax.rsqrt(jnp.mean(x * x, axis=-1, keepdims=True) + EPS) * w


def _sigmoid(x):
    return 0.5 * jnp.tanh(0.5 * x) + 0.5


def _store_rows_as_tiles(ref, x):
    half = D_MODEL // 2
    hi = pltpu.bitcast(x[:, :half], jnp.uint32) & jnp.uint32(0xFFFF0000)
    lo = pltpu.bitcast(x[:, half:], jnp.uint32) >> 16
    ref[...] = (hi | lo).reshape(x.shape[0], ROW_CHUNKS, LANES)


def _load_rows_from_tiles(ref, rows):
    w = ref[...].reshape(rows, D_MODEL // 2)
    hi = pltpu.bitcast(w & jnp.uint32(0xFFFF0000), F32)
    lo = pltpu.bitcast(w << 16, F32)
    return jnp.concatenate([hi, lo], axis=1)


def _inproj_kernel(x_ref, nw_ref, w_ref, cos_ref, sin_ref,
                   q_ref, k_ref, v_ref, g_ref, p_ref, ga_ref, gb_ref, *, defer_gates=False):
    xn = _rmsnorm(x_ref[...], nw_ref[...]).astype(BF16)
    cos = cos_ref[...]
    sin = sin_ref[...]

    def seg(i):
        return _dot(xn, w_ref[:, IN_OFFS[i]:IN_OFFS[i + 1]])

    def rot(a):
        return a * cos + pltpu.roll(a, RET_DK // 2, 1) * sin

    q = seg(0)
    k = seg(1)
    for h in range(RET_HEADS):
        sl = slice(h * RET_DK, (h + 1) * RET_DK)
        q_ref[:, sl] = rot(q[:, sl]).astype(q_ref.dtype)
        k_ref[:, sl] = (rot(k[:, sl]) * (RET_DK ** -0.5)).astype(k_ref.dtype)
    v_ref[...] = seg(2).astype(v_ref.dtype)

    def gates():
        g = seg(3)
        g_ref[...] = (g * _sigmoid(g)).astype(g_ref.dtype)
        p_ref[...] = seg(4)
        ga_ref[...] = _sigmoid(seg(5)).astype(ga_ref.dtype)
        gb_ref[...] = _sigmoid(seg(6)).astype(gb_ref.dtype)

    if defer_gates:
        return gates
    gates()


def _sample_state_step(step, sdec_ref, qt_ref, kt_ref, q_ref, k_ref, v_ref, st_ref, stout_ref, o_ref):
    grp = st_ref.shape[0]
    row0 = step * grp
    if grp % SUBLANES == 0:
        rows = pl.ds(pl.multiple_of(row0, SUBLANES), grp)
        q8, k8, v8 = q_ref[rows, :], k_ref[rows, :], v_ref[rows, :]
    else:
        assert 2 * grp == SUBLANES
        rows = pl.ds(pl.multiple_of((step // 2) * SUBLANES, SUBLANES), SUBLANES)
        first = step % 2 == 0
        q8, k8, v8 = (jnp.where(first, a[:grp], a[grp:]) for a in (q_ref[rows, :], k_ref[rows, :], v_ref[rows, :]))
    for h in range(RET_HEADS):
        ksl = slice(h * RET_DK, (h + 1) * RET_DK)
        vsl = slice(h * RET_DV, (h + 1) * RET_DV)
        score = jnp.sum(q8[:, ksl] * k8[:, ksl], axis=1, keepdims=True) * sdec_ref[h, 0]
        intra = score * v8[:, vsl]
        for bb in range(grp):
            s_old = st_ref[bb, h]
            qcol = qt_ref[0, ksl, bb:bb + 1] * sdec_ref[h, 1]
            kcol = kt_ref[0, ksl, bb:bb + 1] * sdec_ref[h, 2]
            cross = jnp.sum(s_old * qcol, axis=0, keepdims=True)
            o_ref[pl.ds(row0 + bb, 1), vsl] = intra[bb:bb + 1, :] + cross
            stout_ref[bb, h] = s_old * sdec_ref[h, 3] + kcol * v8[bb:bb + 1, vsl]


def _inproj_small(x2d, nw, w_in_bf, cosf, sinf):
    rows = x2d.shape[0]
    const = lambda i: (0, 0)
    return pl.pallas_call(
        _inproj_kernel,
        grid=(1,),
        in_specs=[
            pl.BlockSpec((rows, D_MODEL), const),
            pl.BlockSpec((1, D_MODEL), const),
            pl.BlockSpec((D_MODEL, IN_TOTAL), const, pipeline_mode=pl.Buffered(1)),
            pl.BlockSpec((rows, RET_DK), const),
            pl.BlockSpec((rows, RET_DK), const),
        ],
        out_specs=[pl.BlockSpec((rows, w), const) for w in IN_WIDTHS],
        out_shape=[jax.ShapeDtypeStruct((rows, w), F32) for w in IN_WIDTHS],
        compiler_params=pltpu.CompilerParams(dimension_semantics=("arbitrary",), vmem_limit_bytes=VMEM_LIMIT),
        name="inproj",
    )(x2d, nw, w_in_bf, cosf, sinf)


def _group_norm(o, gn_row):
    mu = jnp.mean(o, axis=-1, keepdims=True)
    var = jnp.mean(jnp.square(o - mu), axis=-1, keepdims=True)
    return (o - mu) * lax.rsqrt(var + EPS) * gn_row


def _pool_branch(groups, poolw_ref, pscale_ref, wpool_ref):
    pm = [_dot(g.astype(BF16), poolw_ref[i]) for i, g in enumerate(groups)]
    pm = jnp.concatenate(pm, axis=1) * pscale_ref[...]
    return _dot(pm.astype(BF16), wpool_ref[...])


def _merge_tail(o_norm, silu_g, sig_a, sig_b, yb, x, wret_ref, wout_ref):
    ya = _dot((silu_g.astype(F32) * o_norm).astype(BF16), wret_ref[...])
    merged = sig_a.astype(F32) * ya + sig_b.astype(F32) * yb
    return x + _dot(merged.astype(BF16), wout_ref[...])


def _route(h1, nffn_ref, wrt_ref, br_ref, run_scr,
           xn2_ref, idx_ref, gate_ref, rank_ref):
    tm = h1.shape[0]
    xn2 = _rmsnorm(h1, nffn_ref[...]).astype(BF16)
    xn2_ref[...] = xn2
    logits = lax.dot_general(wrt_ref[...], xn2, (((1,), (1,)), ((), ())),
                             preferred_element_type=F32) + br_ref[...]
    e_iota = lax.broadcasted_iota(jnp.int32, (N_EXPERTS, tm), 0)
    work = logits
    vals, sels = [], []
    chosen = jnp.zeros((N_EXPERTS, tm), F32)
    for _ in range(TOP_K):
        m = jnp.max(work, axis=0, keepdims=True)
        sel = jnp.min(jnp.where(work == m, e_iota, N_EXPERTS), axis=0, keepdims=True)
        hit = e_iota == sel
        vals.append(m)
        sels.append(sel)
        chosen = jnp.where(hit, 1.0, chosen)
        work = jnp.where(hit, -jnp.inf, work)
    exps = [jnp.exp(v - vals[0]) for v in vals]
    denom = exps[0] + exps[1] + exps[2] + exps[3]
    gates = [e / denom for e in exps]
    r_i = lax.broadcasted_iota(jnp.int32, (tm, tm), 0)
    c_i = lax.broadcasted_iota(jnp.int32, (tm, tm), 1)
    before = jnp.where(r_i < c_i, 1.0, 0.0).astype(BF16)
    base = run_scr[...] + _dot(chosen.astype(BF16), before)
    for kk in range(TOP_K):
        rk = jnp.sum(jnp.where(e_iota == sels[kk], base, 0.0), axis=0, keepdims=True)
        rank_ref[kk:kk + 1, :] = rk.astype(jnp.int32)
        idx_ref[kk:kk + 1, :] = sels[kk]
        gate_ref[kk:kk + 1, :] = gates[kk]
    run_scr[...] = run_scr[...] + jnp.sum(chosen, axis=1, keepdims=True)


def _mixer_kernel(q_ref, k_ref, v_ref, g_ref, p_ref, ga_ref, gb_ref, x_ref,
                  kmeta_ref, vmeta_ref, pmeta_ref, mask_ref, qdec_ref, kdec_ref, cdec_ref, gn_ref,
                  poolw_ref, pscale_ref, wret_ref, wpool_ref, wout_ref, nffn_ref, wrt_ref, br_ref,
                  cnt0_ref,
                  h1_ref, xn2_ref, idx_ref, gate_ref, rank_ref, cnt_ref, sfin_ref, pfin_ref,
                  s_scr, ext_scr, o_scr, run_scr, *, after_first_pass=None):
    b = pl.program_id(0)
    j = pl.program_id(1)
    nj = pl.num_programs(1)
    tm = q_ref.shape[0]

    def state_update(s_old, kc, vc, h):
        kd = (kc.astype(F32) * kdec_ref[h]).astype(BF16)
        upd = lax.dot_general(kd, vc, (((0,), (0,)), ((), ())), preferred_element_type=F32)
        return s_old * cdec_ref[h] + upd

    @pl.when(jnp.logical_and(b == 0, j == 0))
    def _():
        run_scr[...] = cnt0_ref[:, 0:1]

    @pl.when(j == 0)
    def _():
        for h in range(RET_HEADS):
            kc = kmeta_ref[:, h * RET_DK:(h + 1) * RET_DK]
            vc = vmeta_ref[:, h * RET_DV:(h + 1) * RET_DV]
            s_scr[h] = state_update(jnp.zeros((RET_DK, RET_DV), F32), kc, vc, h)
        ext_scr[0:N_META, :] = pmeta_ref[...]

    n_chunks = tm // CHUNK
    units = [(c, h) for c in range(n_chunks) for h in range(RET_HEADS)]

    def operands(c, h):
        rows = slice(c * CHUNK, (c + 1) * CHUNK)
        return (q_ref[rows, h * RET_DK:(h + 1) * RET_DK], k_ref[rows, h * RET_DK:(h + 1) * RET_DK],
                v_ref[rows, h * RET_DV:(h + 1) * RET_DV])

    lhs, upd = {}, {}
    for c, h in units:
        qc, kc, vc = operands(c, h)
        scores = lax.dot_general(qc, kc, (((1,), (1,)), ((), ())), preferred_element_type=F32) * mask_ref[h]
        qd = (qc.astype(F32) * qdec_ref[h]).astype(BF16)
        lhs[c, h] = jnp.concatenate([scores.astype(BF16), qd], axis=1)
        kd = (kc.astype(F32) * kdec_ref[h]).astype(BF16)
        upd[c, h] = lax.dot_general(kd, vc, (((0,), (0,)), ((), ())), preferred_element_type=F32)
    if after_first_pass is not None:
        after_first_pass()
    state = {}
    for h in range(RET_HEADS):
        s = s_scr[h]
        for c in range(n_chunks):
            state[c, h] = s
            s = s * cdec_ref[h] + upd[c, h]
        s_scr[h] = s
    for c, h in units:
        rhs = jnp.concatenate([operands(c, h)[2], state[c, h].astype(BF16)], axis=0)
        o = _dot(lhs[c, h], rhs)
        o_scr[c * CHUNK:(c + 1) * CHUNK, h * RET_DV:(h + 1) * RET_DV] = _group_norm(
            o, gn_ref[:, h * RET_DV:(h + 1) * RET_DV])

    p = p_ref[...]
    ext_scr[N_META:N_META + tm, :] = p
    a = ext_scr[...]
    g1 = POOL_GROUP_DIM
    s2 = a + pltpu.roll(a, 1, 0)
    s4 = s2[:, g1:] + pltpu.roll(s2[:, g1:], 2, 0)
    s8 = s4[:, g1:] + pltpu.roll(s4[:, g1:], 4, 0)
    s16 = s8[:, g1:] + pltpu.roll(s8[:, g1:], 8, 0)
    sums = (s2[N_META:, :g1], s4[N_META:, :g1], s8[N_META:, :g1], s16[N_META:, :])
    groups = [sums[i] * (1.0 / POOL_WINDOWS[i]) - p[:, i * g1:(i + 1) * g1] for i in range(POOL_GROUPS)]
    ext_scr[0:N_META, :] = ext_scr[tm:tm + N_META, :]

    yb = _pool_branch(groups, poolw_ref, pscale_ref, wpool_ref)
    h1 = _merge_tail(o_scr[...], g_ref[...], ga_ref[...], gb_ref[...], yb, x_ref[...], wret_ref, wout_ref)
    h1_ref[...] = h1
    _route(h1, nffn_ref, wrt_ref, br_ref, run_scr, xn2_ref, idx_ref, gate_ref, rank_ref)
    cnt_ref[...] = jnp.broadcast_to(run_scr[...], cnt_ref.shape)

    @pl.when(j == nj - 1)
    def _():
        for h in range(RET_HEADS):
            sfin_ref[0, h] = s_scr[h]
        pfin_ref[0] = ext_scr[0:N_META, :]


N_SAMPLE_IN = 7
N_MIXER_IN = 17
N_MIXER_OUT = 8
N_PROJ = len(IN_WIDTHS)


def _prompt_kernel(*refs):
    it = iter(refs)
    take = lambda n: [next(it) for _ in range(n)]
    x_ref, nw_ref, w_ref, cos_ref, sin_ref = take(5)
    sample_in = take(N_SAMPLE_IN)
    mixer_in = take(N_MIXER_IN)
    mixer_out = take(N_MIXER_OUT)
    stout_ref, os_ref = take(2)
    proj_scr = take(N_PROJ)
    mixer_scr = take(4)
    gate_columns = _inproj_kernel(x_ref, nw_ref, w_ref, cos_ref, sin_ref, *proj_scr, defer_gates=True)
    step = pl.program_id(0) * pl.num_programs(1) + pl.program_id(1)
    _sample_state_step(step, *sample_in, stout_ref, os_ref)
    _mixer_kernel(*proj_scr, x_ref, *mixer_in, *mixer_out, *mixer_scr, after_first_pass=gate_columns)


def _prompt_layer(x2d, nw, w_in_bf, cosf, sinf, sample, kmeta, vmeta, pmeta, dec, wts, cnt0, batch, seq):
    tm = MIXER_TILE
    nj = seq // tm
    rows = batch * seq
    row_map = lambda b, j: (b * nj + j, 0)
    lane_map = lambda b, j: (0, b * nj + j)
    c2 = lambda b, j: (0, 0)
    c3 = lambda b, j: (0, 0, 0)

    def whole(a):
        mode = dict(pipeline_mode=pl.Buffered(1)) if a.size * a.dtype.itemsize >= 512 * 1024 else {}
        return pl.BlockSpec(a.shape, c2 if a.ndim == 2 else c3, **mode)

    sdec, qt, kt, qs, ks, vs, state = sample
    nb = state.shape[0]
    grp = nb // (batch * nj)
    assert grp * batch * nj == nb and qt.shape == (nb // grp, RET_QK, grp)
    step_map = lambda b, j: (b * nj + j, 0, 0)
    st_spec = pl.BlockSpec((grp, RET_HEADS, RET_DK, RET_DV), lambda b, j: (b * nj + j, 0, 0, 0))
    mask, qdec, kdec, cdec = dec
    mixer_in = [kmeta, vmeta, pmeta, mask, qdec, kdec, cdec, *wts, cnt0]
    assert len(mixer_in) == N_MIXER_IN
    in_arrays = [x2d, nw, w_in_bf, cosf, sinf, sdec, qt, kt, qs, ks, vs, state] + mixer_in
    in_specs = [
        pl.BlockSpec((tm, D_MODEL), row_map),
        pl.BlockSpec((1, D_MODEL), c2),
        pl.BlockSpec((D_MODEL, IN_TOTAL), c2, pipeline_mode=pl.Buffered(1)),
        pl.BlockSpec((tm, RET_DK), lambda b, j: (j, 0)),
        pl.BlockSpec((tm, RET_DK), lambda b, j: (j, 0)),
        pl.BlockSpec(memory_space=pltpu.SMEM),
        pl.BlockSpec((1, RET_QK, grp), step_map), pl.BlockSpec((1, RET_QK, grp), step_map),
        pl.BlockSpec(qs.shape, c2), pl.BlockSpec(ks.shape, c2), pl.BlockSpec(vs.shape, c2),
        st_spec,
    ] + [whole(a) for a in mixer_in]
    out_shape = [
        jax.ShapeDtypeStruct((rows, D_MODEL), F32),
        jax.ShapeDtypeStruct((rows, D_MODEL), BF16),
        jax.ShapeDtypeStruct((TOP_K, rows), jnp.int32),
        jax.ShapeDtypeStruct((TOP_K, rows), F32),
        jax.ShapeDtypeStruct((TOP_K, rows), jnp.int32),
        jax.ShapeDtypeStruct((N_EXPERTS, LANES), F32),
        jax.ShapeDtypeStruct((batch, RET_HEADS, RET_DK, RET_DV), F32),
        jax.ShapeDtypeStruct((batch, N_META, POOL_WIDTH), F32),
        jax.ShapeDtypeStruct(state.shape, F32),
        jax.ShapeDtypeStruct((nb, RET_V), F32),
    ]
    out_specs = [
        pl.BlockSpec((tm, D_MODEL), row_map),
        pl.BlockSpec((tm, D_MODEL), row_map),
        pl.BlockSpec((TOP_K, tm), lane_map),
        pl.BlockSpec((TOP_K, tm), lane_map),
        pl.BlockSpec((TOP_K, tm), lane_map),
        pl.BlockSpec((N_EXPERTS, LANES), c2),
        pl.BlockSpec((1, RET_HEADS, RET_DK, RET_DV), lambda b, j: (b, 0, 0, 0)),
        pl.BlockSpec((1, N_META, POOL_WIDTH), lambda b, j: (b, 0, 0)),
        st_spec,
        pl.BlockSpec((nb, RET_V), c2),
    ]
    proj_dts = (BF16, BF16, BF16, BF16, F32, BF16, BF16)
    return pl.pallas_call(
        _prompt_kernel,
        grid=(batch, nj),
        in_specs=in_specs,
        out_specs=out_specs,
        out_shape=out_shape,
        scratch_shapes=[pltpu.VMEM((tm, w), dt) for w, dt in zip(IN_WIDTHS, proj_dts)] + [
            pltpu.VMEM((RET_HEADS, RET_DK, RET_DV), F32),
            pltpu.VMEM((N_META + tm, POOL_WIDTH), F32),
            pltpu.VMEM((tm, RET_V), F32),
            pltpu.VMEM((N_EXPERTS, 1), F32),
        ],
        compiler_params=pltpu.CompilerParams(
            dimension_semantics=("arbitrary", "arbitrary"), vmem_limit_bytes=PROMPT_VMEM_LIMIT),
        name="prompt_layer",
    )(*in_arrays)


def _sample_kernel(o_ref, g_ref, p_ref, ga_ref, gb_ref, x_ref, pool_ref, gn_ref,
                   poolw_ref, pscale_ref, wret_ref, wpool_ref, wout_ref, nffn_ref, wrt_ref, br_ref, cnt0_ref,
                   poolout_ref, h1_ref, xn2_ref, idx_ref, gate_ref, rank_ref, cnt_ref,
                   run_scr):
    run_scr[...] = cnt0_ref[:, 0:1]
    o = o_ref[...]
    o_norm = jnp.concatenate(
        [_group_norm(o[:, h * RET_DV:(h + 1) * RET_DV], gn_ref[:, h * RET_DV:(h + 1) * RET_DV])
         for h in range(RET_HEADS)], axis=1)
    p = p_ref[...]
    w = POOL_WIDTH
    g1 = POOL_GROUP_DIM

    def prev(r, lo):
        return pool_ref[:, r * w + lo:(r + 1) * w]

    s2 = p + prev(14, 0)
    s4 = s2[:, g1:] + prev(13, g1) + prev(12, g1)
    s8 = s4[:, g1:] + prev(11, 2 * g1) + prev(10, 2 * g1) + prev(9, 2 * g1) + prev(8, 2 * g1)
    s16 = s8[:, g1:]
    for r in range(7, -1, -1):
        s16 = s16 + prev(r, 3 * g1)
    sums = (s2[:, :g1], s4[:, :g1], s8[:, :g1], s16)
    groups = [sums[t] * (1.0 / POOL_WINDOWS[t]) - p[:, t * g1:(t + 1) * g1] for t in range(POOL_GROUPS)]
    poolout_ref[:, 0:(POOL_BUF - 1) * w] = pool_ref[:, w:POOL_BUF * w]
    poolout_ref[:, (POOL_BUF - 1) * w:] = p
    yb = _pool_branch(groups, poolw_ref, pscale_ref, wpool_ref)
    h1 = _merge_tail(o_norm, g_ref[...], ga_ref[...], gb_ref[...], yb, x_ref[...], wret_ref, wout_ref)
    h1_ref[...] = h1
    _route(h1, nffn_ref, wrt_ref, br_ref, run_scr, xn2_ref, idx_ref, gate_ref, rank_ref)
    cnt_ref[...] = jnp.broadcast_to(run_scr[...], cnt_ref.shape)


def _sample_mixer(o_s, proj, x2d, pool2d, wts, cnt0):
    _, _, _, g, p, ga, gb = proj
    nb = x2d.shape[0]
    c2 = lambda i: (0, 0)

    def whole(a):
        return pl.BlockSpec(a.shape, c2)

    gn, poolw, pscale, wret, wpool, wout, nffn, wrt, br = wts
    in_arrays = [o_s, g, p, ga, gb, x2d, pool2d, gn, poolw, pscale, wret, wpool, wout, nffn, wrt, br, cnt0]
    in_specs = [whole(a) for a in in_arrays[:8]]
    in_specs += [pl.BlockSpec(poolw.shape, lambda i: (0, 0, 0))]
    in_specs += [whole(a) for a in in_arrays[9:]]
    out_shape = [
        jax.ShapeDtypeStruct(pool2d.shape, F32),
        jax.ShapeDtypeStruct((nb, D_MODEL), F32),
        jax.ShapeDtypeStruct((nb, D_MODEL), BF16),
        jax.ShapeDtypeStruct((TOP_K, nb), jnp.int32),
        jax.ShapeDtypeStruct((TOP_K, nb), F32),
        jax.ShapeDtypeStruct((TOP_K, nb), jnp.int32),
        jax.ShapeDtypeStruct((N_EXPERTS, LANES), F32),
    ]
    return pl.pallas_call(
        _sample_kernel,
        grid=(1,),
        in_specs=in_specs,
        out_specs=[pl.BlockSpec(s.shape, c2) for s in out_shape],
        out_shape=out_shape,
        scratch_shapes=[pltpu.VMEM((N_EXPERTS, 1), F32)],
        compiler_params=pltpu.CompilerParams(
            dimension_semantics=("arbitrary",), vmem_limit_bytes=VMEM_LIMIT),
        name="sample_mixer",
    )(*in_arrays)


TILE_ROWS = ROW_TILE * TOP_K


def _run_copy(src_ref, src_row, dst_ref, dst_row, n_rows, sem):
    return pltpu.make_async_copy(src_ref.at[pl.ds(src_row, n_rows)], dst_ref.at[pl.ds(dst_row, n_rows)], sem)


def _for_each_run(tcnt_ref, tile, fn, enabled=True):
    off = 0
    for e in range(N_EXPERTS):
        n = tcnt_ref[tile * N_EXPERTS + e]

        @pl.when(jnp.logical_and(n > 0, enabled))
        def _(e=e, off=off, n=n):
            fn(e, off, n)

        off = off + n


def _dispatch_kernel(tcnt_ref, tdst_ref, ttot_ref, zrow_ref, zcnt_ref, nblk_ref, pos_ref, xp_ref, xs_ref, out_ref,
                     sorted_scr, zero_scr, sem, zsem):
    i = pl.program_id(0)
    n_prompt = pl.num_programs(0) - 1
    n_blocks = out_ref.shape[0] // MOE_BLOCK
    slot = i % 2

    def wait_tile(tile, s):
        _run_copy(out_ref, 0, sorted_scr.at[s], 0, ttot_ref[tile], sem.at[s]).wait()

    def for_each_pad(fn):
        def body(e, carry):
            n = zcnt_ref[e]

            @pl.when(n > 0)
            def _():
                fn(_run_copy(zero_scr, 0, out_ref, zrow_ref[e], n, zsem))

            dead = nblk_ref[0] + e

            @pl.when(dead < n_blocks)
            def _():
                fn(_run_copy(zero_scr, 0, out_ref, dead * MOE_BLOCK, MOE_BLOCK, zsem))

            return carry

        lax.fori_loop(0, N_EXPERTS, body, 0)

    @pl.when(i == 0)
    def _():
        zero_scr[...] = jnp.zeros_like(zero_scr)
        for_each_pad(lambda cp: cp.start())

    def sort_tile(x_ref):
        r_iota = lax.broadcasted_iota(jnp.int32, (TILE_ROWS, ROW_TILE), 0)
        hit = r_iota == pos_ref[0:1, :]
        for kk in range(1, TOP_K):
            hit = jnp.logical_or(hit, r_iota == pos_ref[kk:kk + 1, :])
        perm = jnp.where(hit, 1.0, 0.0).astype(BF16)
        xs = _dot(perm, x_ref[...])
        _store_rows_as_tiles(sorted_scr.at[slot], xs)
        _for_each_run(tcnt_ref, i, lambda e, off, n: _run_copy(
            sorted_scr.at[slot], off, out_ref, tdst_ref[i * N_EXPERTS + e], n, sem.at[slot]).start())

    @pl.when(i >= 2)
    def _():
        wait_tile(i - 2, slot)

    @pl.when(i < n_prompt)
    def _():
        sort_tile(xp_ref)

    @pl.when(i == n_prompt)
    def _():
        sort_tile(xs_ref)

    @pl.when(i == n_prompt)
    def _():
        wait_tile(i - 1, 1 - slot)
        wait_tile(i, slot)
        for_each_pad(lambda cp: cp.wait())


def _dispatch(tcnt, tdst, ttot, zrow, zcnt, nblk, pos, xn2_p, xn2_s, n_sorted):
    n_prompt = xn2_p.shape[0] // ROW_TILE
    grid_spec = pltpu.PrefetchScalarGridSpec(
        num_scalar_prefetch=6,
        grid=(n_prompt + 1,),
        in_specs=[
            pl.BlockSpec((TOP_K, ROW_TILE), lambda i, *_: (0, i)),
            pl.BlockSpec((ROW_TILE, D_MODEL), lambda i, *_: (jnp.minimum(i, n_prompt - 1), 0)),
            pl.BlockSpec((ROW_TILE, D_MODEL), lambda i, *_: (0, 0)),
        ],
        out_specs=pl.BlockSpec(memory_space=pl.ANY),
        scratch_shapes=[
            pltpu.VMEM((2, TILE_ROWS, ROW_CHUNKS, LANES), jnp.uint32),
            pltpu.VMEM((MOE_BLOCK, ROW_CHUNKS, LANES), jnp.uint32),
            pltpu.SemaphoreType.DMA((2,)), pltpu.SemaphoreType.DMA(()),
        ],
    )
    return pl.pallas_call(
        _dispatch_kernel,
        grid_spec=grid_spec,
        out_shape=jax.ShapeDtypeStruct((n_sorted, ROW_CHUNKS, LANES), jnp.uint32),
        compiler_params=pltpu.CompilerParams(dimension_semantics=("arbitrary",)),
        name="moe_dispatch",
    )(tcnt, tdst, ttot, zrow, zcnt, nblk, pos, xn2_p, xn2_s)


def _expert_kernel(be_ref, nblk_ref, eord_ref, enext_ref, bvalid_ref, x_ref, wgu_ref, bgu_ref, wd_ref, bd_ref, y_ref,
                   wgu_f32, wd_f32, wgu_bf, wd_bf, wsem):
    i = pl.program_id(0)
    live = i < nblk_ref[0]
    new_expert = jnp.logical_or(i == 0, be_ref[i] != be_ref[jnp.maximum(i - 1, 0)])

    def weight_copies(e, s):
        return (pltpu.make_async_copy(wgu_ref.at[e], wgu_f32.at[s], wsem.at[0, s]),
                pltpu.make_async_copy(wd_ref.at[e], wd_f32.at[s], wsem.at[1, s]))

    @pl.when(jnp.logical_and(live, new_expert))
    def _():
        slot = eord_ref[i] % 2

        @pl.when(i == 0)
        def _():
            for cp in weight_copies(be_ref[0], 0):
                cp.start()

        for cp in weight_copies(be_ref[i], slot):
            cp.wait()

        @pl.when(enext_ref[i] >= 0)
        def _():
            for cp in weight_copies(enext_ref[i], 1 - slot):
                cp.start()

    @pl.when(jnp.logical_not(live))
    def _():
        y_ref[...] = jnp.zeros_like(y_ref)

    def ffn(rows, first):
        if first:
            slot = eord_ref[i] % 2
            wgu = wgu_f32[slot].astype(BF16)
            wgu_bf[...] = wgu
        else:
            wgu = wgu_bf[...]
        x = _load_rows_from_tiles(x_ref.at[pl.ds(0, rows)], rows)
        h = _dot(x.astype(BF16), wgu) + bgu_ref[0]
        gate = jnp.minimum(h[:, :D_FF], SWIGLU_LIMIT)
        up = jnp.clip(h[:, D_FF:], -SWIGLU_LIMIT, SWIGLU_LIMIT)
        glu = gate * _sigmoid(gate * SWIGLU_ALPHA)
        if first:
            wd_b = wd_f32[slot].astype(BF16)
            wd_bf[...] = wd_b
        else:
            wd_b = wd_bf[...]
        y = _dot(((up + 1.0) * glu).astype(BF16), wd_b) + bd_ref[0]
        _store_rows_as_tiles(y_ref.at[pl.ds(0, rows)], y.astype(BF16).astype(F32))

    half = MOE_BLOCK // 2
    short = bvalid_ref[i] <= half
    for first in (False, True):
        same = new_expert if first else jnp.logical_not(new_expert)

        @pl.when(jnp.logical_and(jnp.logical_and(live, same), jnp.logical_not(short)))
        def _(first=first):
            ffn(MOE_BLOCK, first)

        @pl.when(jnp.logical_and(jnp.logical_and(live, same), short))
        def _(first=first):
            ffn(half, first)
            y_ref[pl.ds(half, half)] = jnp.zeros((half,) + y_ref.shape[1:], y_ref.dtype)


def _experts(block_e, nblk, block_eord, block_enext, block_valid, x_sorted, wgu, bgu, wd, bd):
    n_blocks = x_sorted.shape[0] // MOE_BLOCK
    wmap = lambda i, be, *_: (be[i], 0, 0)
    rmap = lambda i, *_: (i, 0, 0)
    xmap = lambda i, be, nb, *_: (jnp.minimum(i, nb[0] - 1), 0, 0)
    hbm = pl.BlockSpec(memory_space=pl.ANY)
    grid_spec = pltpu.PrefetchScalarGridSpec(
        num_scalar_prefetch=5,
        grid=(n_blocks,),
        in_specs=[
            pl.BlockSpec((MOE_BLOCK, ROW_CHUNKS, LANES), xmap),
            hbm,
            pl.BlockSpec((1, 1, 2 * D_FF), wmap),
            hbm,
            pl.BlockSpec((1, 1, D_MODEL), wmap),
        ],
        out_specs=pl.BlockSpec((MOE_BLOCK, ROW_CHUNKS, LANES), rmap),
        scratch_shapes=[
            pltpu.VMEM((2, D_MODEL, 2 * D_FF), F32), pltpu.VMEM((2, D_FF, D_MODEL), F32),
            pltpu.VMEM((D_MODEL, 2 * D_FF), BF16), pltpu.VMEM((D_FF, D_MODEL), BF16),
            pltpu.SemaphoreType.DMA((2, 2)),
        ],
    )
    return pl.pallas_call(
        _expert_kernel,
        grid_spec=grid_spec,
        out_shape=jax.ShapeDtypeStruct(x_sorted.shape, jnp.uint32),
        compiler_params=pltpu.CompilerParams(
            dimension_semantics=("arbitrary",), vmem_limit_bytes=VMEM_LIMIT),
        name="moe_experts",
    )(block_e, nblk, block_eord, block_enext, block_valid, x_sorted, wgu, bgu, wd, bd)


def _combine_kernel(tcnt_ref, tdst_ref, ttot_ref, pos_ref, gate_ref, ys_ref, hp_ref, hs_ref, nf_ref, yp_ref, ysmp_ref,
                    runs_scr, sem):
    i = pl.program_id(0)
    n_tiles = pl.num_programs(0)
    n_prompt = n_tiles - 1
    ring = runs_scr.shape[0]
    ahead = ring - 1
    slot = i % ring

    def start_runs(tile, s, enabled=True):
        _for_each_run(tcnt_ref, tile, lambda e, off, n: _run_copy(
            ys_ref, tdst_ref[tile * N_EXPERTS + e], runs_scr.at[s], off, n, sem.at[s]).start(), enabled)

    @pl.when(i == 0)
    def _():
        for t in range(ahead):
            start_runs(t, t, enabled=t < n_tiles)

    start_runs(jnp.minimum(i + ahead, n_tiles - 1), (i + ahead) % ring, enabled=i + ahead < n_tiles)
    _run_copy(ys_ref, 0, runs_scr.at[slot], 0, ttot_ref[i], sem.at[slot]).wait()
    ys = _load_rows_from_tiles(runs_scr.at[slot], TILE_ROWS).astype(BF16)
    r_iota = lax.broadcasted_iota(jnp.int32, (TILE_ROWS, ROW_TILE), 0)
    gmat = jnp.zeros((TILE_ROWS, ROW_TILE), F32)
    for kk in range(TOP_K):
        gmat = jnp.where(r_iota == pos_ref[kk:kk + 1, :], gate_ref[kk:kk + 1, :], gmat)
    moe = lax.dot_general(gmat.astype(BF16), ys, (((0,), (0,)), ((), ())), preferred_element_type=F32)

    @pl.when(i < n_prompt)
    def _():
        yp_ref[...] = _rmsnorm(hp_ref[...] + moe, nf_ref[...])

    @pl.when(i == n_prompt)
    def _():
        ysmp_ref[...] = _rmsnorm(hs_ref[...] + moe, nf_ref[...])


def _combine(tcnt, tdst, ttot, pos, gates, y_sorted, h1_p, h1_s, nf):
    n_prompt = h1_p.shape[0] // ROW_TILE
    pmap = lambda i, *_: (jnp.minimum(i, n_prompt - 1), 0)
    smap = lambda i, *_: (0, 0)
    lmap = lambda i, *_: (0, i)
    grid_spec = pltpu.PrefetchScalarGridSpec(
        num_scalar_prefetch=3,
        grid=(n_prompt + 1,),
        in_specs=[
            pl.BlockSpec((TOP_K, ROW_TILE), lmap),
            pl.BlockSpec((TOP_K, ROW_TILE), lmap),
            pl.BlockSpec(memory_space=pl.ANY),
            pl.BlockSpec((ROW_TILE, D_MODEL), pmap),
            pl.BlockSpec((ROW_TILE, D_MODEL), smap),
            pl.BlockSpec((1, D_MODEL), smap),
        ],
        out_specs=[pl.BlockSpec((ROW_TILE, D_MODEL), pmap), pl.BlockSpec((ROW_TILE, D_MODEL), smap)],
        scratch_shapes=[pltpu.VMEM((COMBINE_RING, TILE_ROWS, ROW_CHUNKS, LANES), jnp.uint32),
                        pltpu.SemaphoreType.DMA((COMBINE_RING,))],
    )
    return pl.pallas_call(
        _combine_kernel,
        grid_spec=grid_spec,
        out_shape=[jax.ShapeDtypeStruct(h1_p.shape, F32), jax.ShapeDtypeStruct(h1_s.shape, F32)],
        compiler_params=pltpu.CompilerParams(dimension_semantics=("arbitrary",)),
        name="moe_combine",
    )(tcnt, tdst, ttot, pos, gates, y_sorted, h1_p, h1_s, nf)


def _rotary_tables(pos):
    f = np.float32
    inv = np.power(f(ROPE_BASE), -np.arange(0, RET_DK, 2, dtype=f) / f(RET_DK)).astype(f)
    ang = (np.asarray(pos, f)[:, None] * inv[None, :]).astype(f)
    cos, sin = np.cos(ang).astype(f), np.sin(ang).astype(f)
    return np.concatenate([cos, cos], axis=1), np.concatenate([-sin, sin], axis=1)


def _decay_tables(chunk):
    f = np.float32
    log_g = np.log1p(-np.exp2(f(-5.0) - np.arange(RET_HEADS, dtype=f))).astype(f)
    i = np.arange(chunk, dtype=f)
    diff = i[:, None] - i[None, :]
    mask = np.where(diff[None] >= 0, np.exp(np.maximum(diff, f(0.0))[None] * log_g[:, None, None]), f(0.0)).astype(f)
    q_dec = np.exp((i + f(1.0))[None, :] * log_g[:, None]).astype(f)
    k_dec = np.exp((f(chunk) - f(1.0) - i)[None, :] * log_g[:, None]).astype(f)
    c_dec = np.exp(f(chunk) * log_g).astype(f)
    return mask, q_dec, k_dec, c_dec


def kernel(x_prompt, x_sample, state_ret, state_pool, meta_tokens, norm_mix, w_in, ret_gn, pool_w, pool_scale,
           w_ret_branch, w_pool_branch, w_out, norm_ffn, w_router, b_router, w_gate_up, b_gate_up, w_down, b_down,
           norm_final):
    batch, seq, _ = x_prompt.shape
    nb = x_sample.shape[0]
    past_len = 16384
    n_prompt_tok = batch * seq
    n_tok = n_prompt_tok + nb

    w_in_bf = w_in[0].astype(BF16)
    wts = (ret_gn[0][None, :], pool_w[0].astype(BF16), pool_scale[0][None, :],
           w_ret_branch[0].astype(BF16), w_pool_branch[0].astype(BF16), w_out[0].astype(BF16),
           norm_ffn[0][None, :], w_router[0].T.astype(BF16), b_router[0][:, None])
    wgu = w_gate_up[0]
    wd = w_down[0]
    bgu = b_gate_up[0][:, None, :]
    bd = b_down[0][:, None, :]
    nmix = norm_mix[0][None, :]

    cos_p, sin_p = _rotary_tables(N_META + np.arange(seq))
    cos_s, sin_s = _rotary_tables(np.concatenate([np.arange(N_META), np.full((nb,), past_len)]))
    mask, q_dec, k_dec, c_dec = _decay_tables(CHUNK)
    dec = (mask,
           np.ascontiguousarray(np.broadcast_to(q_dec[:, :, None], (RET_HEADS, CHUNK, RET_DK))),
           np.ascontiguousarray(np.broadcast_to(k_dec[:, :, None], (RET_HEADS, CHUNK, RET_DK))),
           np.ascontiguousarray(np.broadcast_to(c_dec[:, None, None], (RET_HEADS, 1, RET_DV))))
    m1, q1, k1, c1 = _decay_tables(1)
    sdec = np.stack([m1[:, 0, 0], q1[:, 0], k1[:, 0], c1], axis=1)

    x2d = x_prompt.reshape(n_prompt_tok, D_MODEL)
    xs2d = x_sample.reshape(nb, D_MODEL)
    x_small = jnp.concatenate([meta_tokens, xs2d], axis=0)
    proj_small = _inproj_small(x_small, nmix, w_in_bf, cos_s, sin_s)
    proj_s = tuple(a[N_META:] for a in proj_small)

    lead = CHUNK - N_META
    kmeta = jnp.pad(proj_small[1][:N_META], ((lead, 0), (0, 0))).astype(BF16)
    vmeta = jnp.pad(proj_small[2][:N_META], ((lead, 0), (0, 0))).astype(BF16)
    pmeta = proj_small[4][:N_META]

    grp = nb // (n_prompt_tok // MIXER_TILE)

    def cols(a):
        return a.T.reshape(RET_QK, nb // grp, grp).transpose(1, 0, 2)

    (h1_p, xn2_p, idx_p, gate_p, rank_p, cnt_p, s_fin, p_fin, st_s, o_s) = _prompt_layer(
        x2d, nmix, w_in_bf, cos_p, sin_p,
        (sdec, cols(proj_s[0]), cols(proj_s[1]), proj_s[0], proj_s[1], proj_s[2], state_ret[0]),
        kmeta, vmeta, pmeta, dec, wts, jnp.zeros((N_EXPERTS, LANES), F32), batch, seq)

    pool2d = state_pool[0].reshape(nb, POOL_BUF * POOL_WIDTH)
    (pool_s, h1_s, xn2_s, idx_s, gate_s, rank_s, cnt) = _sample_mixer(o_s, proj_s, xs2d, pool2d, wts, cnt_p)

    assert n_prompt_tok % ROW_TILE == 0 and nb <= ROW_TILE
    n_phantom = ROW_TILE - nb
    n_tiles = n_prompt_tok // ROW_TILE + 1
    i32 = jnp.int32
    counts = cnt[:, 0].astype(i32)
    padded = ((counts + MOE_BLOCK - 1) // MOE_BLOCK) * MOE_BLOCK
    pad_end = jnp.cumsum(padded)
    pad_start = pad_end - padded
    n_blocks = (n_tok * TOP_K) // MOE_BLOCK + N_EXPERTS
    block_row = jnp.arange(n_blocks, dtype=i32) * MOE_BLOCK
    block_e = jnp.minimum(jnp.sum((pad_end[None, :] <= block_row[:, None]).astype(i32), axis=1), N_EXPERTS - 1)
    nblk = (pad_end[-1:] // MOE_BLOCK).astype(i32)
    e_row = jnp.arange(N_EXPERTS, dtype=i32)
    used = padded > 0
    e_ord = jnp.cumsum(used.astype(i32)) - 1
    later_used = jnp.logical_and(e_row[None, :] > e_row[:, None], used[None, :])
    e_next = jnp.min(jnp.where(later_used, e_row[None, :], N_EXPERTS), axis=1)
    e_next = jnp.where(e_next == N_EXPERTS, -1, e_next)
    of_block = block_e[:, None] == e_row[None, :]
    block_eord = jnp.sum(jnp.where(of_block, e_ord[None, :], 0), axis=1)
    block_enext = jnp.sum(jnp.where(of_block, e_next[None, :], 0), axis=1)
    block_left = jnp.sum(jnp.where(of_block, (pad_start + counts)[None, :], 0), axis=1) - block_row
    block_valid = jnp.clip(block_left, 0, MOE_BLOCK)
    phantom = lambda fill, dt: jnp.full((TOP_K, n_phantom), fill, dt)
    idx = jnp.concatenate([idx_p, idx_s, phantom(-1, i32)], axis=1)
    rank = jnp.concatenate([rank_p, rank_s, phantom(0, i32)], axis=1)
    gates = jnp.concatenate([gate_p, gate_s, phantom(0.0, F32)], axis=1)
    onehot = idx[None] == jnp.arange(N_EXPERTS, dtype=i32)[:, None, None]
    tile_cnt = jnp.sum(onehot.reshape(N_EXPERTS, TOP_K, n_tiles, ROW_TILE).astype(i32), axis=(1, 3)).T
    run_before = jnp.cumsum(tile_cnt, axis=0) - tile_cnt
    tile_off = jnp.cumsum(tile_cnt, axis=1) - tile_cnt
    tile_dst = pad_start[None, :] + run_before
    delta = jnp.repeat((tile_off - run_before).T, ROW_TILE, axis=1)
    pos = rank + jnp.sum(jnp.where(onehot, delta[:, None, :], 0), axis=0)
    pos = jnp.where(idx >= 0, pos, -1)
    tcnt, tdst, ttot = tile_cnt.reshape(-1), tile_dst.reshape(-1), jnp.sum(tile_cnt, axis=1)

    tail = ((0, n_phantom), (0, 0))
    x_sorted = _dispatch(tcnt, tdst, ttot, pad_start + counts, padded - counts, nblk, pos, xn2_p,
                         jnp.pad(xn2_s, tail), n_blocks * MOE_BLOCK)
    y_sorted = _experts(block_e, nblk, block_eord, block_enext, block_valid, x_sorted, wgu, bgu, wd, bd)
    y_p, y_s = _combine(tcnt, tdst, ttot, pos, gates, y_sorted, h1_p, jnp.pad(h1_s, tail), norm_final[None, :])

    y_prompt = y_p.reshape(batch, seq, D_MODEL)
    y_sample = y_s[:nb].reshape(nb, 1, D_MODEL)
    ret_state_prompt = s_fin[None]
    pool_state_prompt = p_fin[:, 1:, :][None]
    ret_state_sample = st_s[None]
    pool_state_sample = pool_s.reshape(nb, POOL_BUF, POOL_WIDTH)[None]
    return (y_prompt, y_sample, ret_state_prompt, pool_state_prompt, ret_state_sample, pool_state_sample)
```
